```python
import math
import jax, jax.numpy as jnp
from jax import lax
import numpy as np

D_MODEL = 1024
BATCH = 8
SEQ = 4096
DEPTH = 2

N_MEM = 256
EPS = 1e-6
NEG_INF = -1e30
MOBA_HEADS = 8
MOBA_HEAD_DIM = 64
MOBA_WIDTH = MOBA_HEADS * MOBA_HEAD_DIM
MOBA_BLOCK = 256
MOBA_TOPK = 3
MOBA_Q_CHUNK = 32
ROPE_THETA = 10000.0
SSM_WIDTH = 512
SSM_GROUP = 16
SSM_GROUPS = SSM_WIDTH // SSM_GROUP
SSM_STATE = 64
DT_MIN = 1e-3
DT_MAX = 1e-1
GDN_HEADS = 4
GDN_HEAD_DIM = 128
GDN_WIDTH = GDN_HEADS * GDN_HEAD_DIM
GDN_CONV = 4
GDN_CHUNK = 64
XA_HEADS = 4
XA_HEAD_DIM = 128
XA_WIDTH = XA_HEADS * XA_HEAD_DIM
D_FF = 2816
N_EXPERTS = 8
TOP_K = 2
N_DENSE = (DEPTH + 1) // 2
N_MOE = DEPTH // 2
N_BRANCH = 3
IN_SIZES = (MOBA_WIDTH, MOBA_WIDTH, MOBA_WIDTH,
            SSM_WIDTH,
            GDN_WIDTH, GDN_WIDTH, GDN_WIDTH,
            GDN_HEADS, GDN_HEADS,
            GDN_WIDTH,
            D_MODEL, D_MODEL, D_MODEL)
D_IN = 3 * MOBA_WIDTH + SSM_WIDTH + 4 * GDN_WIDTH + 2 * GDN_HEADS + N_BRANCH * D_MODEL

kernel_name = "hybrid_moba_s5_gdn_xattn_moe_block"


def rms_norm(x, g):
    xf = x.astype(jnp.float32)
    y = xf * lax.rsqrt(jnp.mean(xf * xf, axis=-1, keepdims=True) + EPS)
    return (y * g.astype(jnp.float32)).astype(x.dtype)


def l2_norm(x):
    xf = x.astype(jnp.float32)
    return xf * lax.rsqrt(jnp.sum(xf * xf, axis=-1, keepdims=True) + EPS)


def split_columns(p):
    parts, start = [], 0
    for size in IN_SIZES:
        parts.append(p[..., start:start + size])
        start += size
    return parts


def rope_tables(positions):
    inv = 1.0 / (ROPE_THETA ** (jnp.arange(0, MOBA_HEAD_DIM, 2, dtype=jnp.float32) / MOBA_HEAD_DIM))
    ang = positions.astype(jnp.float32)[..., None] * inv
    return jnp.cos(ang), jnp.sin(ang)


def apply_rope(x, cos, sin):
    x1, x2 = jnp.split(x.astype(jnp.float32), 2, axis=-1)
    c, s = cos[:, :, None, :], sin[:, :, None, :]
    return jnp.concatenate([x1 * c - x2 * s, x2 * c + x1 * s], axis=-1).astype(x.dtype)


def moba_attention(q, k, v):
    b_, s_, h_, dh = q.shape
    pad = (-s_) % MOBA_BLOCK
    widths = ((0, 0), (0, pad), (0, 0), (0, 0))
    q, k, v = (jnp.pad(t, widths) for t in (q, k, v))
    sp = s_ + pad
    nb = sp // MOBA_BLOCK
    nc = sp // MOBA_Q_CHUNK
    top = min(MOBA_TOPK, nb)
    scale = dh ** -0.5
    qh = jnp.transpose(q, (0, 2, 1, 3))
    kb = jnp.transpose(k, (0, 2, 1, 3)).reshape(b_, h_, nb, MOBA_BLOCK, dh)
    vb = jnp.transpose(v, (0, 2, 1, 3)).reshape(b_, h_, nb, MOBA_BLOCK, dh)
    k_mean = jnp.mean(kb.astype(jnp.float32), axis=3)
    q_blk = jnp.arange(sp) // MOBA_BLOCK
    gate = jnp.einsum('bhsd,bhnd->bhsn', qh.astype(jnp.float32), k_mean)
    gate = jnp.where(jnp.arange(nb)[None, :] < q_blk[:, None], gate, NEG_INF)
    _, sel = lax.top_k(gate, top)
    valid = jnp.arange(top)[None, :] < jnp.minimum(q_blk, top)[:, None]

    qc = qh.reshape(b_, h_, nc, MOBA_Q_CHUNK, dh).transpose(2, 0, 1, 3, 4)
    selc = sel.reshape(b_, h_, nc, MOBA_Q_CHUNK, top).transpose(2, 0, 1, 3, 4)
    validc = valid.reshape(nc, MOBA_Q_CHUNK, top)
    gather_blocks = jax.vmap(jax.vmap(lambda blocks, idx: blocks[idx]))

    def one_chunk(args):
        ci, q_c, sel_c, valid_c = args
        own = (ci * MOBA_Q_CHUNK) // MOBA_BLOCK
        k_own = lax.dynamic_index_in_dim(kb, own, axis=2, keepdims=False)
        v_own = lax.dynamic_index_in_dim(vb, own, axis=2, keepdims=False)
        k_sel = gather_blocks(kb, sel_c)
        v_sel = gather_blocks(vb, sel_c)
        s_sel = jnp.einsum('bhcd,bhckjd->bhckj', q_c, k_sel).astype(jnp.float32) * scale
        s_sel = jnp.where(valid_c[None, None, :, :, None], s_sel, NEG_INF)
        s_own = jnp.einsum('bhcd,bhjd->bhcj', q_c, k_own).astype(jnp.float32) * scale
        q_pos = ci * MOBA_Q_CHUNK + jnp.arange(MOBA_Q_CHUNK)
        k_pos = own * MOBA_BLOCK + jnp.arange(MOBA_BLOCK)
        s_own = jnp.where(k_pos[None, :] <= q_pos[:, None], s_own, NEG_INF)
        logits = jnp.concatenate([s_sel.reshape(b_, h_, MOBA_Q_CHUNK, top * MOBA_BLOCK), s_own], axis=-1)
        p = jax.nn.softmax(logits, axis=-1).astype(v_own.dtype)
        p_sel = p[..., :top * MOBA_BLOCK].reshape(b_, h_, MOBA_Q_CHUNK, top, MOBA_BLOCK)
        p_own = p[..., top * MOBA_BLOCK:]
        return (jnp.einsum('bhckj,bhckjd->bhcd', p_sel, v_sel)
                + jnp.einsum('bhcj,bhjd->bhcd', p_own, v_own))

    out = lax.map(one_chunk, (jnp.arange(nc), qc, selc, validc))
    out = out.transpose(1, 0, 3, 2, 4).reshape(b_, sp, h_ * dh)
    return out[:, :s_]


def s5_ssm(u, a_re, a_im, log_dt, b_re, b_im, c_re, c_im, d_skip):
    b_, s_, _ = u.shape
    uf = u.astype(jnp.float32).reshape(b_, s_, SSM_GROUPS, SSM_GROUP)
    lam = lax.complex(a_re.astype(jnp.float32), a_im.astype(jnp.float32))
    dt = jnp.exp(log_dt.astype(jnp.float32))[:, None]
    lam_bar = jnp.exp(lam * dt)
    b_mat = lax.complex(b_re.astype(jnp.float32), b_im.astype(jnp.float32))
    b_bar = ((lam_bar - 1.0) / lam)[..., None] * b_mat
    c_mat = lax.complex(c_re.astype(jnp.float32), c_im.astype(jnp.float32))
    bu = jnp.einsum('gpn,bsgn->bsgp', b_bar, uf)
    a_elems = jnp.broadcast_to(lam_bar, (1, s_, SSM_GROUPS, SSM_STATE))

    def combine(e1, e2):
        a1, h1 = e1
        a2, h2 = e2
        return a2 * a1, a2 * h1 + h2

    _, h = lax.associative_scan(combine, (a_elems, bu), axis=1)
    y = jnp.einsum('gnp,bsgp->bsgn', c_mat, h).real
    y = y + d_skip.astype(jnp.float32).reshape(SSM_GROUPS, SSM_GROUP) * uf
    return y.reshape(b_, s_, SSM_WIDTH).astype(u.dtype)


def causal_conv(x, w):
    return lax.conv_general_dilated(x, w[:, None, :], window_strides=(1,), padding=[(GDN_CONV - 1, 0)],
                                    dimension_numbers=('NWC', 'WIO', 'NWC'),
                                    feature_group_count=x.shape[-1])


def gated_delta_rule(q, k, v, g, beta):
    b_, s_, h_, dk = q.shape
    dv = v.shape[-1]
    L = GDN_CHUNK
    n = s_ // L

    def to_chunks(t):
        t = jnp.moveaxis(t.astype(jnp.float32), 2, 1)
        return t.reshape((b_, h_, n, L) + t.shape[3:])

    q = to_chunks(q) * (dk ** -0.5)
    k, v, g, beta = to_chunks(k), to_chunks(v), to_chunks(g), to_chunks(beta)
    g = jnp.cumsum(g, axis=-1)
    tril = jnp.tril(jnp.ones((L, L), dtype=bool))
    strict = jnp.tril(jnp.ones((L, L), dtype=bool), -1)
    diff = g[..., :, None] - g[..., None, :]
    decay = jnp.where(tril, jnp.exp(jnp.where(tril, diff, 0.0)), 0.0)
    k_beta = k * beta[..., None]
    v_beta = v * beta[..., None]
    a_low = jnp.where(strict, jnp.einsum('bhnid,bhnjd->bhnij', k_beta, k) * decay, 0.0)
    t_mat = a_low + jnp.eye(L, dtype=jnp.float32)
    u = lax.linalg.triangular_solve(t_mat, v_beta, left_side=True, lower=True, unit_diagonal=True)
    w = lax.linalg.triangular_solve(t_mat, k_beta * jnp.exp(g)[..., None], left_side=True,
                                    lower=True, unit_diagonal=True)
    qk = jnp.einsum('bhnid,bhnjd->bhnij', q, k) * decay
    g_last = g[..., -1]
    k_dec = k * jnp.exp(g_last[..., None] - g)[..., None]
    q_dec = q * jnp.exp(g)[..., None]
    xs = tuple(jnp.moveaxis(t, 2, 0) for t in (q_dec, qk, u, w, k_dec, g_last))

    def step(state, inp):
        q_i, qk_i, u_i, w_i, kd_i, gl_i = inp
        v_new = u_i - jnp.einsum('bhld,bhde->bhle', w_i, state)
        o = jnp.einsum('bhld,bhde->bhle', q_i, state) + jnp.einsum('bhij,bhje->bhie', qk_i, v_new)
        state = state * jnp.exp(gl_i)[..., None, None] + jnp.einsum('bhld,bhle->bhde', kd_i, v_new)
        return state, o

    state0 = jnp.zeros((b_, h_, dk, dv), jnp.float32)
    _, o = lax.scan(step, state0, xs)
    o = jnp.moveaxis(o, 0, 2).reshape(b_, h_, s_, dv)
    return jnp.moveaxis(o, 1, 2)


def cross_attention(h, mem_n, wq, wk, wv, wo):
    b_, s_, _ = h.shape
    m_ = mem_n.shape[1]
    q = (h @ wq).reshape(b_, s_, XA_HEADS, XA_HEAD_DIM)
    k = (mem_n @ wk).reshape(b_, m_, XA_HEADS, XA_HEAD_DIM)
    v = (mem_n @ wv).reshape(b_, m_, XA_HEADS, XA_HEAD_DIM)
    s = jnp.einsum('bshd,bmhd->bhsm', q, k).astype(jnp.float32) * (XA_HEAD_DIM ** -0.5)
    p = jax.nn.softmax(s, axis=-1).astype(v.dtype)
    o = jnp.einsum('bhsm,bmhd->bshd', p, v).reshape(b_, s_, XA_WIDTH)
    return o @ wo


def swiglu(h, w_gate, w_up, w_down):
    return (jax.nn.silu(h @ w_gate) * (h @ w_up)) @ w_down


def moe_swiglu(h, w_router, w_gate, w_up, w_down):
    logits = (h @ w_router).astype(jnp.float32)
    top_val, top_idx = lax.top_k(logits, TOP_K)
    top_w = jax.nn.softmax(top_val, axis=-1)
    comb = jnp.sum(jax.nn.one_hot(top_idx, N_EXPERTS, dtype=jnp.float32) * top_w[..., None], axis=-2)
    comb = comb.astype(h.dtype)
    out = jnp.zeros_like(h)
    for e in range(N_EXPERTS):
        out = out + comb[..., e:e + 1] * swiglu(h, w_gate[e], w_up[e], w_down[e])
    return out


def setup_inputs(seed: int = 0) -> dict:
    key = jax.random.key(seed)
    ks = iter(jax.random.split(key, 48))
    f32 = jnp.float32

    def nrm(shape, scale):
        return jax.random.normal(next(ks), shape, f32) * scale

    def gain(shape):
        return 1.0 + 0.02 * jax.random.normal(next(ks), shape, f32)

    def log_uniform(shape, lo, hi):
        return jax.random.uniform(next(ks), shape, f32, math.log(lo), math.log(hi))

    x = nrm((BATCH, SEQ, D_MODEL), 1.0)
    mem = nrm((BATCH, N_MEM, D_MODEL), 1.0)
    positions = (jax.random.randint(next(ks), (BATCH, 1), 0, 1024)
                 + jnp.arange(SEQ, dtype=jnp.int32)[None, :]).astype(jnp.int32)
    norm_mix = gain((DEPTH, D_MODEL))
    w_in = nrm((DEPTH, D_MODEL, D_IN), D_MODEL ** -0.5)
    n_idx = jnp.arange(SSM_STATE, dtype=f32)
    ssm_a_re = -0.5 + nrm((DEPTH, SSM_GROUPS, SSM_STATE), 0.01)
    ssm_a_im = math.pi * n_idx + nrm((DEPTH, SSM_GROUPS, SSM_STATE), 0.01)
    ssm_log_dt = log_uniform((DEPTH, SSM_GROUPS), DT_MIN, DT_MAX)
    ssm_b_re = nrm((DEPTH, SSM_GROUPS, SSM_STATE, SSM_GROUP), (2 * SSM_GROUP) ** -0.5)
    ssm_b_im = nrm((DEPTH, SSM_GROUPS, SSM_STATE, SSM_GROUP), (2 * SSM_GROUP) ** -0.5)
    ssm_c_re = nrm((DEPTH, SSM_GROUPS, SSM_GROUP, SSM_STATE), 0.5 ** 0.5)
    ssm_c_im = nrm((DEPTH, SSM_GROUPS, SSM_GROUP, SSM_STATE), 0.5 ** 0.5)
    ssm_d = nrm((DEPTH, SSM_WIDTH), 1.0)
    ssm_w_glu = nrm((DEPTH, SSM_WIDTH, 2 * D_MODEL), SSM_WIDTH ** -0.5)
    gdn_conv = nrm((DEPTH, GDN_CONV, 3 * GDN_WIDTH), GDN_CONV ** -0.5)
    gdn_a_log = jnp.log(jax.random.uniform(next(ks), (DEPTH, GDN_HEADS), f32, 1.0, 16.0))
    dt = jnp.exp(log_uniform((DEPTH, GDN_HEADS), DT_MIN, DT_MAX))
    gdn_dt_bias = dt + jnp.log(-jnp.expm1(-dt))
    gdn_norm = gain((DEPTH, GDN_HEAD_DIM))
    w_up_moba = nrm((DEPTH, MOBA_WIDTH, D_MODEL), MOBA_WIDTH ** -0.5)
    w_up_gdn = nrm((DEPTH, GDN_WIDTH, D_MODEL), GDN_WIDTH ** -0.5)
    w_out = nrm((DEPTH, D_MODEL, D_MODEL), D_MODEL ** -0.5)
    norm_xa = gain((DEPTH, D_MODEL))
    norm_mem = gain((DEPTH, D_MODEL))
    xa_wq = nrm((DEPTH, D_MODEL, XA_WIDTH), D_MODEL ** -0.5)
    xa_wk = nrm((DEPTH, D_MODEL, XA_WIDTH), D_MODEL ** -0.5)
    xa_wv = nrm((DEPTH, D_MODEL, XA_WIDTH), D_MODEL ** -0.5)
    xa_wo = nrm((DEPTH, XA_WIDTH, D_MODEL), XA_WIDTH ** -0.5)
    norm_ffn = gain((DEPTH, D_MODEL))
    ffn_w_gate = nrm((N_DENSE, D_MODEL, D_FF), D_MODEL ** -0.5)
    ffn_w_up = nrm((N_DENSE, D_MODEL, D_FF), D_MODEL ** -0.5)
    ffn_w_down = nrm((N_DENSE, D_FF, D_MODEL), D_FF ** -0.5)
    moe_w_router = nrm((N_MOE, D_MODEL, N_EXPERTS), D_MODEL ** -0.5)
    moe_w_gate = nrm((N_MOE, N_EXPERTS, D_MODEL, D_FF), D_MODEL ** -0.5)
    moe_w_up = nrm((N_MOE, N_EXPERTS, D_MODEL, D_FF), D_MODEL ** -0.5)
    moe_w_down = nrm((N_MOE, N_EXPERTS, D_FF, D_MODEL), D_FF ** -0.5)
    norm_final = gain((D_MODEL,))
    return {"x": x, "mem": mem, "positions": positions, "norm_mix": norm_mix, "w_in": w_in,
            "ssm_a_re": ssm_a_re, "ssm_a_im": ssm_a_im, "ssm_log_dt": ssm_log_dt,
            "ssm_b_re": ssm_b_re, "ssm_b_im": ssm_b_im, "ssm_c_re": ssm_c_re, "ssm_c_im": ssm_c_im,
            "ssm_d": ssm_d, "ssm_w_glu": ssm_w_glu, "gdn_conv": gdn_conv, "gdn_a_log": gdn_a_log,
            "gdn_dt_bias": gdn_dt_bias, "gdn_norm": gdn_norm, "w_up_moba": w_up_moba,
            "w_up_gdn": w_up_gdn, "w_out": w_out, "norm_xa": norm_xa, "norm_mem": norm_mem,
            "xa_wq": xa_wq, "xa_wk": xa_wk, "xa_wv": xa_wv, "xa_wo": xa_wo, "norm_ffn": norm_ffn,
            "ffn_w_gate": ffn_w_gate, "ffn_w_up": ffn_w_up, "ffn_w_down": ffn_w_down,
            "moe_w_router": moe_w_router, "moe_w_gate": moe_w_gate, "moe_w_up": moe_w_up,
            "moe_w_down": moe_w_down, "norm_final": norm_final}


def reference(x, mem, positions, norm_mix, w_in, ssm_a_re, ssm_a_im, ssm_log_dt, ssm_b_re, ssm_b_im,
              ssm_c_re, ssm_c_im, ssm_d, ssm_w_glu, gdn_conv, gdn_a_log, gdn_dt_bias, gdn_norm,
              w_up_moba, w_up_gdn, w_out, norm_xa, norm_mem, xa_wq, xa_wk, xa_wv, xa_wo, norm_ffn,
              ffn_w_gate, ffn_w_up, ffn_w_down, moe_w_router, moe_w_gate, moe_w_up, moe_w_down,
              norm_final):
    b_, s_, _ = x.shape
    cos, sin = rope_tables(positions)
    for l in range(DEPTH):
        h = rms_norm(x, norm_mix[l])
        (q_m, k_m, v_m, u_s, q_g, k_g, v_g, a_g, b_g, z_g,
         gate_a, gate_b, gate_c) = split_columns(h @ w_in[l])
        q_m = apply_rope(q_m.reshape(b_, s_, MOBA_HEADS, MOBA_HEAD_DIM), cos, sin)
        k_m = apply_rope(k_m.reshape(b_, s_, MOBA_HEADS, MOBA_HEAD_DIM), cos, sin)
        v_m = v_m.reshape(b_, s_, MOBA_HEADS, MOBA_HEAD_DIM)
        y_a = moba_attention(q_m, k_m, v_m) @ w_up_moba[l]
        y_s = jax.nn.gelu(s5_ssm(u_s, ssm_a_re[l], ssm_a_im[l], ssm_log_dt[l], ssm_b_re[l], ssm_b_im[l],
                                 ssm_c_re[l], ssm_c_im[l], ssm_d[l]))
        glu_a, glu_b = jnp.split(y_s @ ssm_w_glu[l], 2, axis=-1)
        y_b = glu_a * jax.nn.sigmoid(glu_b)
        qkv = jax.nn.silu(causal_conv(jnp.concatenate([q_g, k_g, v_g], axis=-1), gdn_conv[l]))
        q_g, k_g, v_g = jnp.split(qkv, 3, axis=-1)
        q_g = l2_norm(q_g.reshape(b_, s_, GDN_HEADS, GDN_HEAD_DIM))
        k_g = l2_norm(k_g.reshape(b_, s_, GDN_HEADS, GDN_HEAD_DIM))
        v_g = v_g.reshape(b_, s_, GDN_HEADS, GDN_HEAD_DIM)
        beta = jax.nn.sigmoid(b_g.astype(jnp.float32))
        log_decay = -jnp.exp(gdn_a_log[l].astype(jnp.float32)) * jax.nn.softplus(
            a_g.astype(jnp.float32) + gdn_dt_bias[l].astype(jnp.float32))
        o_g = gated_delta_rule(q_g, k_g, v_g, log_decay, beta).astype(h.dtype)
        o_g = rms_norm(o_g, gdn_norm[l]) * jax.nn.silu(z_g.reshape(b_, s_, GDN_HEADS, GDN_HEAD_DIM))
        y_c = o_g.reshape(b_, s_, GDN_WIDTH) @ w_up_gdn[l]
        merged = (jax.nn.sigmoid(gate_a) * y_a + jax.nn.sigmoid(gate_b) * y_b
                  + jax.nn.sigmoid(gate_c) * y_c)
        x = x + merged @ w_out[l]
        x = x + cross_attention(rms_norm(x, norm_xa[l]), rms_norm(mem, norm_mem[l]),
                                xa_wq[l], xa_wk[l], xa_wv[l], xa_wo[l])
        h = rms_norm(x, norm_ffn[l])
        if l % 2 == 0:
            x = x + swiglu(h, ffn_w_gate[l // 2], ffn_w_up[l // 2], ffn_w_down[l // 2])
        else:
            x = x + moe_swiglu(h, moe_w_router[l // 2], moe_w_gate[l // 2], moe_w_up[l // 2],
                               moe_w_down[l // 2])
    return rms_norm(x, norm_final)
```

```python
import functools
import math

import numpy as np
import jax
import jax.numpy as jnp
from jax import lax
from jax.experimental import pallas as pl
from jax.experimental.pallas import tpu as pltpu

F32 = jnp.float32
BF16 = jnp.bfloat16
HIGHEST = lax.Precision.HIGHEST

EPS = 1e-6
NEG_INF = -1e30
MOBA_HEADS = 8
MOBA_HEAD_DIM = 64
MOBA_WIDTH = MOBA_HEADS * MOBA_HEAD_DIM
MOBA_BLOCK = 256
MOBA_TOPK = 3
ROPE_THETA = 10000.0
SSM_WIDTH = 512
SSM_GROUP = 16
SSM_GROUPS = SSM_WIDTH // SSM_GROUP
SSM_STATE = 64
SSM_STATES = SSM_GROUPS * SSM_STATE
GDN_HEADS = 4
GDN_HEAD_DIM = 128
GDN_WIDTH = GDN_HEADS * GDN_HEAD_DIM
GDN_CONV = 4
GDN_CHUNK = 64
XA_HEADS = 4
XA_HEAD_DIM = 128
XA_WIDTH = XA_HEADS * XA_HEAD_DIM
N_EXPERTS = 8
TOP_K = 2

LANES = 128
SUBLANES = 8
VMEM_LIMIT = 56 * 1024 * 1024


def _cparams(*sem):
    return pltpu.CompilerParams(dimension_semantics=sem, vmem_limit_bytes=VMEM_LIMIT)


def _rms(x, g):
    return x * lax.rsqrt(jnp.mean(x * x, axis=-1, keepdims=True) + EPS) * g


def _sigmoid(x):
    return 1.0 / (1.0 + jnp.exp(-x))


def _silu(x):
    return x * _sigmoid(x)


def _dot(a, b):
    return jnp.dot(a, b, preferred_element_type=F32)


def _dot_nt(a, b):
    return lax.dot_general(a, b, (((1,), (1,)), ((), ())), preferred_element_type=F32)


def _dot_tn(a, b):
    return lax.dot_general(a, b, (((0,), (0,)), ((), ())), preferred_element_type=F32)


def _norm_matmul_body(x_ref, g_ref, w_ref, o_ref, h_ref):
    @pl.when(pl.program_id(1) == 0)
    def _():
        h_ref[...] = _rms(x_ref[...], g_ref[...]).astype(BF16)

    o_ref[...] = _dot(h_ref[...], w_ref[...]).astype(o_ref.dtype)


def norm_matmul(x, g, w, out_dtype, tm=512, tn=512):
    t, k = x.shape
    n = w.shape[1]
    tm, tn = min(tm, t), min(tn, n)
    return pl.pallas_call(
        _norm_matmul_body,
        grid=(t // tm, n // tn),
        in_specs=[pl.BlockSpec((tm, k), lambda i, j: (i, 0)),
                  pl.BlockSpec((1, k), lambda i, j: (0, 0)),
                  pl.BlockSpec((k, tn), lambda i, j: (0, j))],
        out_specs=pl.BlockSpec((tm, tn), lambda i, j: (i, j)),
        out_shape=jax.ShapeDtypeStruct((t, n), out_dtype),
        scratch_shapes=[pltpu.VMEM((tm, k), BF16)],
        compiler_params=_cparams("parallel", "arbitrary"),
        name="norm_matmul",
    )(x, g.reshape(1, k), w)


def _rope_body(pos_ref, inv_ref, cos_ref, sin_ref):
    ang = pos_ref[...] * inv_ref[...]
    lane = lax.broadcasted_iota(jnp.int32, ang.shape, 1)
    first_half = (lane % MOBA_HEAD_DIM) < (MOBA_HEAD_DIM // 2)
    cos_ref[...] = jnp.cos(ang)
    s = jnp.sin(ang)
    sin_ref[...] = jnp.where(first_half, -s, s)


def rope_tables(positions):
    t = positions.size
    half = MOBA_HEAD_DIM // 2
    inv = (1.0 / (np.float32(ROPE_THETA) ** (np.arange(0, MOBA_HEAD_DIM, 2, dtype=np.float32)
                                             / np.float32(MOBA_HEAD_DIM)))).astype(np.float32)
    inv_row = jnp.asarray(np.tile(inv, LANES // half).reshape(1, LANES))
    pos = positions.astype(F32).reshape(t, 1)
    ts = min(1024, t)
    return pl.pallas_call(
        _rope_body,
        grid=(t // ts,),
        in_specs=[pl.BlockSpec((ts, 1), lambda i: (i, 0)),
                  pl.BlockSpec((1, LANES), lambda i: (0, 0))],
        out_specs=[pl.BlockSpec((ts, LANES), lambda i: (i, 0))] * 2,
        out_shape=[jax.ShapeDtypeStruct((t, LANES), F32)] * 2,
        compiler_params=_cparams("parallel"),
        name="rope_tables",
    )(pos, inv_row)


def _rope_apply(x, cos, sin_signed):
    lane = lax.broadcasted_iota(jnp.int32, x.shape, 1)
    first_half = (lane % MOBA_HEAD_DIM) < (MOBA_HEAD_DIM // 2)
    partner = jnp.where(first_half, pltpu.roll(x, LANES - MOBA_HEAD_DIM // 2, 1),
                        pltpu.roll(x, MOBA_HEAD_DIM // 2, 1))
    return x * cos + partner * sin_signed


def _moba_body(q_ref, k_ref, v_ref, cq_ref, sq_ref, ck_ref, sk_ref, o_ref, kr_ref, km_ref, bias_ref,
               *, nb):
    blk = MOBA_BLOCK
    i = pl.program_id(2)

    @pl.when(i == 0)
    def _():
        def rope_blk(j, c):
            rows = pl.ds(pl.multiple_of(j * blk, blk), blk)
            kr = _rope_apply(k_ref[rows, :], ck_ref[rows, :], sk_ref[rows, :])
            kr_ref[rows, :] = kr.astype(BF16)
            km_ref[pl.ds(j, 1), :] = jnp.mean(kr, axis=0, keepdims=True)
            return c
        lax.fori_loop(0, nb, rope_blk, 0)

    lane = lax.broadcasted_iota(jnp.int32, (1, LANES), 1)
    head_a = lane < MOBA_HEAD_DIM
    q = _rope_apply(q_ref[...], cq_ref[...], sq_ref[...])
    scale = MOBA_HEAD_DIM ** -0.5
    q_heads = (jnp.where(head_a, q, 0.0), jnp.where(head_a, 0.0, q))

    km = km_ref[...]
    bidx = lax.broadcasted_iota(jnp.int32, (1, nb), 1)
    for h in range(2):
        gate = lax.dot_general(q_heads[h], km, (((1,), (1,)), ((), ())),
                               preferred_element_type=F32, precision=HIGHEST)
        gate = jnp.where(bidx < i, gate, NEG_INF)
        for j in range(nb):
            gj = gate[:, j:j + 1]
            ahead = (gate > gj) | ((gate == gj) & (bidx < j))
            rank = jnp.sum(jnp.where(ahead, 1.0, 0.0), axis=-1, keepdims=True)
            sel = (rank < float(MOBA_TOPK)) & (j < i)
            bias_ref[h, j] = jnp.broadcast_to(jnp.where(sel, 0.0, NEG_INF), (blk, LANES))

    qb = [(qh * scale).astype(BF16) for qh in q_heads]

    own = pl.ds(pl.multiple_of(i * blk, blk), blk)
    k_own = kr_ref[own, :]
    v_own = v_ref[own, :]
    r_idx = lax.broadcasted_iota(jnp.int32, (blk, blk), 0)
    c_idx = lax.broadcasted_iota(jnp.int32, (blk, blk), 1)
    causal = c_idx <= r_idx
    state = []
    for h in range(2):
        s = jnp.where(causal, _dot_nt(qb[h], k_own), NEG_INF)
        m = jnp.max(s, axis=-1, keepdims=True)
        p = jnp.exp(s - m)
        l = jnp.sum(p, axis=-1, keepdims=True)
        acc = _dot(p.astype(BF16), v_own)
        state += [m, l, acc]

    def past(j, st):
        rows = pl.ds(pl.multiple_of(j * blk, blk), blk)
        k_j = kr_ref[rows, :]
        v_j = v_ref[rows, :]
        out = []
        for h in range(2):
            m, l, acc = st[3 * h:3 * h + 3]
            b = bias_ref[h, j]
            s = _dot_nt(qb[h], k_j) + jnp.concatenate([b, b], axis=1)
            m_new = jnp.maximum(m, jnp.max(s, axis=-1, keepdims=True))
            alpha = jnp.exp(m - m_new)
            p = jnp.exp(s - m_new)
            l = alpha * l + jnp.sum(p, axis=-1, keepdims=True)
            acc = alpha * acc + _dot(p.astype(BF16), v_j)
            out += [m_new, l, acc]
        return tuple(out)

    st = lax.fori_loop(0, i, past, tuple(state))
    out_a = st[2] / st[1]
    out_b = st[5] / st[4]
    o_ref[...] = jnp.where(head_a, out_a, out_b).astype(o_ref.dtype)


def moba_attention(qk, v, cos, sin, batch, seq):
    nb = seq // MOBA_BLOCK
    pairs = MOBA_WIDTH // LANES
    blk = MOBA_BLOCK
    t = batch * seq
    return pl.pallas_call(
        functools.partial(_moba_body, nb=nb),
        grid=(batch, pairs, nb),
        in_specs=[pl.BlockSpec((blk, LANES), lambda b, p, i: (b * nb + i, p)),
                  pl.BlockSpec((seq, LANES), lambda b, p, i: (b, pairs + p)),
                  pl.BlockSpec((seq, LANES), lambda b, p, i: (b, p)),
                  pl.BlockSpec((blk, LANES), lambda b, p, i: (b * nb + i, 0)),
                  pl.BlockSpec((blk, LANES), lambda b, p, i: (b * nb + i, 0)),
                  pl.BlockSpec((seq, LANES), lambda b, p, i: (b, 0)),
                  pl.BlockSpec((seq, LANES), lambda b, p, i: (b, 0))],
        out_specs=pl.BlockSpec((blk, LANES), lambda b, p, i: (b * nb + i, p)),
        out_shape=jax.ShapeDtypeStruct((t, MOBA_WIDTH), BF16),
        scratch_shapes=[pltpu.VMEM((seq, LANES), BF16),
                        pltpu.VMEM((nb, LANES), F32),
                        pltpu.VMEM((2, nb, blk, LANES), F32)],
        compiler_params=_cparams("parallel", "parallel", "arbitrary"),
        name="moba_attention",
    )(qk, qk, v, cos, sin, cos, sin)


def _s5_disc_body(are_ref, aim_ref, ldt_ref, bre_ref, bim_ref, lre_ref, lim_ref, bbre_ref, bbim_ref):
    a_re, a_im = are_ref[...], aim_ref[...]
    dt = jnp.exp(ldt_ref[...])
    mag = jnp.exp(a_re * dt)
    l_re = mag * jnp.cos(a_im * dt)
    l_im = mag * jnp.sin(a_im * dt)
    lre_ref[...] = l_re
    lim_ref[...] = l_im
    x, y = l_re - 1.0, l_im
    den = a_re * a_re + a_im * a_im
    c_re = (x * a_re + y * a_im) / den
    c_im = (y * a_re - x * a_im) / den
    b_re, b_im = bre_ref[...], bim_ref[...]
    bbre_ref[...] = c_re * b_re - c_im * b_im
    bbim_ref[...] = c_re * b_im + c_im * b_re


def s5_discretise(a_re, a_im, log_dt, b_re, b_im):
    g, p = a_re.shape
    n = b_re.shape[-1]
    col = lambda a: a.reshape(g * p, 1)
    ldt = jnp.broadcast_to(log_dt[:, None], (g, p))
    outs = pl.pallas_call(
        _s5_disc_body,
        out_shape=[jax.ShapeDtypeStruct((g * p, 1), F32)] * 2 + [jax.ShapeDtypeStruct((g * p, n), F32)] * 2,
        name="s5_discretise",
    )(col(a_re), col(a_im), col(ldt), b_re.reshape(g * p, n), b_im.reshape(g * p, n))
    l_re, l_im, bb_re, bb_im = outs
    return l_re.reshape(g, p), l_im.reshape(g, p), bb_re.reshape(g, p, n), bb_im.reshape(g, p, n)


def _block_diag(blocks):
    g, r, c = blocks.shape
    eye = jnp.eye(g, dtype=blocks.dtype)
    return (blocks[:, :, None, :] * eye[:, None, :, None]).reshape(g * r, g * c)


def _s5_body(u_ref, bd_ref, cd_ref, lre_ref, lim_ref, d_ref, y_ref, h_ref, hb_ref, *, steps):
    rows = SUBLANES
    ns = SSM_STATES

    @pl.when(pl.program_id(0) == 0)
    def _():
        h_ref[...] = jnp.zeros_like(h_ref)

    u = u_ref[...]
    hb_ref[...] = _dot(u.astype(BF16), bd_ref[...])
    l_re, l_im = lre_ref[...], lim_ref[...]

    def step(t, carry):
        h_re, h_im = carry
        r = pl.ds(pl.multiple_of(t * rows, rows), rows)
        n_re = l_re * h_re - l_im * h_im + hb_ref[r, 0:ns]
        n_im = l_re * h_im + l_im * h_re + hb_ref[r, ns:2 * ns]
        hb_ref[r, 0:ns] = n_re
        hb_ref[r, ns:2 * ns] = n_im
        return n_re, n_im

    h_re, h_im = lax.fori_loop(0, steps, step, (h_ref[:, 0:ns], h_ref[:, ns:2 * ns]))
    h_ref[:, 0:ns] = h_re
    h_ref[:, ns:2 * ns] = h_im
    y = _dot(hb_ref[...].astype(BF16), cd_ref[...]) + d_ref[...] * u
    y_ref[...] = jax.nn.gelu(y).astype(y_ref.dtype)


def s5_gelu(u_tm, bd, cd, l_re, l_im, d_skip, steps=64):
    rows = u_tm.shape[0]
    blk = steps * SUBLANES
    ns2 = 2 * SSM_STATES
    return pl.pallas_call(
        functools.partial(_s5_body, steps=steps),
        grid=(rows // blk,),
        in_specs=[pl.BlockSpec((blk, SSM_WIDTH), lambda c: (c, 0)),
                  pl.BlockSpec((SSM_WIDTH, ns2), lambda c: (0, 0)),
                  pl.BlockSpec((ns2, SSM_WIDTH), lambda c: (0, 0)),
                  pl.BlockSpec((1, SSM_STATES), lambda c: (0, 0)),
                  pl.BlockSpec((1, SSM_STATES), lambda c: (0, 0)),
                  pl.BlockSpec((1, SSM_WIDTH), lambda c: (0, 0))],
        out_specs=pl.BlockSpec((blk, SSM_WIDTH), lambda c: (c, 0)),
        out_shape=jax.ShapeDtypeStruct((rows, SSM_WIDTH), BF16),
        scratch_shapes=[pltpu.VMEM((SUBLANES, ns2), F32),
                        pltpu.VMEM((blk, ns2), F32)],
        compiler_params=_cparams("arbitrary"),
        name="s5_scan",
    )(u_tm, bd, cd, l_re.reshape(1, -1), l_im.reshape(1, -1), d_skip.reshape(1, -1))


def _gdn_body(q_ref, k_ref, v_ref, z_ref, a_ref, b_ref, wq_ref, wk_ref, wv_ref, alog_ref, dtb_ref, gn_ref,
              o_ref, qs_ref, ks_ref, vs_ref, u_ref, w_ref, qd_ref, kd_ref, qk_ref, gl_ref, st_ref,
              *, seq):
    L = GDN_CHUNK
    dk = GDN_HEAD_DIM
    nc = seq // L
    hd = pl.program_id(1)
    cb = 256
    halo = SUBLANES

    def conv_blk(rb, c):
        base = pl.multiple_of(rb * cb, cb)
        prev = pl.multiple_of(jnp.maximum(base - halo, 0), halo)
        for src, wref, dst, norm, mul in ((q_ref, wq_ref, qs_ref, True, dk ** -0.5),
                                          (k_ref, wk_ref, ks_ref, True, 1.0),
                                          (v_ref, wv_ref, vs_ref, False, 1.0)):
            cur = src[pl.ds(base, cb), :]
            head = jnp.where(rb > 0, src[pl.ds(prev, halo), :], 0.0)
            ext = jnp.concatenate([head, cur], axis=0)
            wts = wref[...]
            y = ext[halo:halo + cb] * wts[GDN_CONV - 1:GDN_CONV]
            for tap in range(1, GDN_CONV):
                y = y + ext[halo - tap:halo - tap + cb] * wts[GDN_CONV - 1 - tap:GDN_CONV - tap]
            y = _silu(y)
            if norm:
                y = y * lax.rsqrt(jnp.sum(y * y, axis=-1, keepdims=True) + EPS) * mul
            dst[pl.ds(base, cb), :] = y
        return c
    lax.fori_loop(0, seq // cb, conv_blk, 0)

    ri = lax.broadcasted_iota(jnp.int32, (L, L), 0)
    ci = lax.broadcasted_iota(jnp.int32, (L, L), 1)
    tril = ci <= ri
    strict = ci < ri
    eye = ci == ri
    eye_f = jnp.where(eye, 1.0, 0.0)
    upper_f = jnp.where(ri <= ci, 1.0, 0.0)
    a_coef = -jnp.exp(jnp.full((1, L), alog_ref[hd], F32))
    dt_bias = dtb_ref[hd]

    def to_col(row):
        return jnp.sum(jnp.where(eye, jnp.broadcast_to(row, (L, L)), 0.0), axis=1, keepdims=True)

    def prep(c, carry):
        rows = pl.ds(pl.multiple_of(c * L, L), L)
        qc, kc, vc = qs_ref[rows, :], ks_ref[rows, :], vs_ref[rows, :]
        a_row = a_ref[pl.ds(c, 1), :]
        b_row = b_ref[pl.ds(c, 1), :]
        z = a_row + dt_bias
        softplus = jnp.maximum(z, 0.0) + jnp.log(1.0 + jnp.exp(-jnp.abs(z)))
        ld_row = a_coef * softplus
        g_row = jnp.dot(jnp.broadcast_to(ld_row, (SUBLANES, L)), upper_f,
                        preferred_element_type=F32, precision=HIGHEST)[0:1]
        beta_row = _sigmoid(b_row)
        g_col = to_col(g_row)
        beta_col = to_col(beta_row)
        g_last = g_row[:, L - 1:L]
        decay = jnp.where(tril, jnp.exp(jnp.where(tril, g_col - g_row, 0.0)), 0.0)
        k_beta = kc * beta_col
        v_beta = vc * beta_col
        kb16, k16 = k_beta.astype(BF16), kc.astype(BF16)
        a_low = jnp.where(strict, _dot_nt(kb16, k16) * decay, 0.0)
        hp = functools.partial(jnp.dot, preferred_element_type=F32, precision=HIGHEST)
        t_inv = eye_f - a_low
        pw = a_low
        span = 2
        while span < L:
            pw = hp(pw, pw)
            t_inv = t_inv + hp(t_inv, pw)
            span *= 2
        ti16 = t_inv.astype(BF16)
        u = _dot(ti16, v_beta.astype(BF16))
        w = _dot(ti16, (k_beta * jnp.exp(g_col)).astype(BF16))
        qk = _dot_nt(qc.astype(BF16), k16) * decay
        u_ref[rows, :] = u
        w_ref[rows, :] = w.astype(BF16)
        qd_ref[rows, :] = (qc * jnp.exp(g_col)).astype(BF16)
        kd_ref[rows, :] = (kc * jnp.exp(g_last - g_col)).astype(BF16)
        qk_ref[c] = qk.astype(BF16)
        gl_ref[pl.ds(c, 1), :] = jnp.broadcast_to(jnp.exp(g_last), (1, LANES))
        return carry
    lax.fori_loop(0, nc, prep, 0)

    st_ref[...] = jnp.zeros_like(st_ref)
    gn = gn_ref[...]

    def chunk(c, carry):
        rows = pl.ds(pl.multiple_of(c * L, L), L)
        s = st_ref[...]
        s16 = s.astype(BF16)
        v_new = u_ref[rows, :] - _dot(w_ref[rows, :], s16)
        vn16 = v_new.astype(BF16)
        o = _dot(qd_ref[rows, :], s16) + _dot(qk_ref[c], vn16)
        st_ref[...] = s * gl_ref[pl.ds(c, 1), :] + _dot_tn(kd_ref[rows, :], vn16)
        o_ref[rows, :] = (_rms(o, gn) * _silu(z_ref[rows, :].astype(F32))).astype(o_ref.dtype)
        return carry
    lax.fori_loop(0, nc, chunk, 0)


def gdn_mixer(qkv, z, a_rows, b_rows, conv_w, a_log, dt_bias, gnorm, batch, seq):
    t = batch * seq
    hds = GDN_HEADS
    nc = seq // GDN_CHUNK
    L = GDN_CHUNK
    d = GDN_HEAD_DIM
    seq_spec = lambda off: pl.BlockSpec((seq, d), lambda b, h: (b, off + h))
    w_spec = lambda off: pl.BlockSpec((GDN_CONV, d), lambda b, h: (0, off + h))
    row_spec = pl.BlockSpec((None, None, nc, L), lambda b, h: (b, h, 0, 0))
    smem = pl.BlockSpec(memory_space=pltpu.SMEM)
    return pl.pallas_call(
        functools.partial(_gdn_body, seq=seq),
        grid=(batch, hds),
        in_specs=[seq_spec(0), seq_spec(hds), seq_spec(2 * hds), seq_spec(0), row_spec, row_spec,
                  w_spec(0), w_spec(hds), w_spec(2 * hds), smem, smem,
                  pl.BlockSpec((1, d), lambda b, h: (0, 0))],
        out_specs=pl.BlockSpec((seq, d), lambda b, h: (b, h)),
        out_shape=jax.ShapeDtypeStruct((t, GDN_WIDTH), BF16),
        scratch_shapes=[pltpu.VMEM((seq, d), F32), pltpu.VMEM((seq, d), F32), pltpu.VMEM((seq, d), F32),
                        pltpu.VMEM((seq, d), F32), pltpu.VMEM((seq, d), BF16),
                        pltpu.VMEM((seq, d), BF16), pltpu.VMEM((seq, d), BF16),
                        pltpu.VMEM((nc, L, L), BF16), pltpu.VMEM((nc, LANES), F32),
                        pltpu.VMEM((d, d), F32)],
        compiler_params=_cparams("parallel", "parallel"),
        name="gdn_mixer",
    )(qkv, qkv, qkv, z, a_rows, b_rows, conv_w, conv_w, conv_w, a_log, dt_bias, gnorm.reshape(1, d))


def _merge_body(x_ref, ya_ref, ys_ref, yc_ref, gt_ref, wa_ref, wg_ref, wc_ref, wo_ref, o_ref):
    d = x_ref.shape[1]
    y_a = _dot(ya_ref[...], wa_ref[...])
    glu = _dot(ys_ref[...], wg_ref[...])
    y_b = glu[:, :d] * _sigmoid(glu[:, d:])
    y_c = _dot(yc_ref[...], wc_ref[...])
    gt = gt_ref[...].astype(F32)
    merged = (_sigmoid(gt[:, :d]) * y_a + _sigmoid(gt[:, d:2 * d]) * y_b
              + _sigmoid(gt[:, 2 * d:]) * y_c)
    o_ref[...] = x_ref[...] + _dot(merged.astype(BF16), wo_ref[...])


def merge_project(x, y_a, y_s, y_c, gates, w_a, w_glu, w_c, w_o, tm=256):
    t, d = x.shape
    row = lambda n: pl.BlockSpec((tm, n), lambda i: (i, 0))
    full = lambda a: pl.BlockSpec(a.shape, lambda i: (0, 0))
    return pl.pallas_call(
        _merge_body,
        grid=(t // tm,),
        in_specs=[row(d), row(y_a.shape[1]), row(y_s.shape[1]), row(y_c.shape[1]), row(3 * d),
                  full(w_a), full(w_glu), full(w_c), full(w_o)],
        out_specs=row(d),
        out_shape=jax.ShapeDtypeStruct((t, d), F32),
        compiler_params=_cparams("parallel"),
        name="merge_project",
    )(x, y_a, y_s, y_c, gates, w_a, w_glu, w_c, w_o)


def _xattn_body(x_ref, g_ref, wq_ref, k_ref, v_ref, wo_ref, o_ref):
    x = x_ref[...]
    h = _rms(x, g_ref[...]).astype(BF16)
    q = _dot(h, wq_ref[...]).astype(BF16)
    outs = []
    for hd in range(XA_HEADS):
        cols = slice(hd * XA_HEAD_DIM, (hd + 1) * XA_HEAD_DIM)
        s = _dot_nt(q[:, cols], k_ref[:, cols]) * (XA_HEAD_DIM ** -0.5)
        p = jnp.exp(s - jnp.max(s, axis=-1, keepdims=True))
        p = p / jnp.sum(p, axis=-1, keepdims=True)
        outs.append(_dot(p.astype(BF16), v_ref[:, cols]))
    o = jnp.concatenate(outs, axis=1).astype(BF16)
    o_ref[...] = x + _dot(o, wo_ref[...])


def cross_attention(x, g, wq, k, v, wo, batch, seq, tm=512):
    t, d = x.shape
    m = k.shape[0] // batch
    spb = seq // tm
    full = lambda a: pl.BlockSpec(a.shape, lambda i: (0, 0))
    return pl.pallas_call(
        _xattn_body,
        grid=(t // tm,),
        in_specs=[pl.BlockSpec((tm, d), lambda i: (i, 0)),
                  pl.BlockSpec((1, d), lambda i: (0, 0)),
                  full(wq),
                  pl.BlockSpec((m, XA_WIDTH), lambda i: (i // spb, 0)),
                  pl.BlockSpec((m, XA_WIDTH), lambda i: (i // spb, 0)),
                  full(wo)],
        out_specs=pl.BlockSpec((tm, d), lambda i: (i, 0)),
        out_shape=jax.ShapeDtypeStruct((t, d), F32),
        compiler_params=_cparams("parallel"),
        name="cross_attention",
    )(x, g.reshape(1, d), wq, k, v, wo)


def _ffn_body(x_ref, g_ref, comb_ref, wg_ref, wu_ref, wd_ref, o_ref, h_ref, acc_ref):
    e, f = pl.program_id(1), pl.program_id(2)

    @pl.when((e == 0) & (f == 0))
    def _():
        h_ref[...] = _rms(x_ref[...], g_ref[...]).astype(BF16)
        acc_ref[...] = jnp.zeros_like(acc_ref)

    h = h_ref[...]
    act = (_silu(_dot(h, wg_ref[...])) * _dot(h, wu_ref[...])).astype(BF16)
    lane = lax.broadcasted_iota(jnp.int32, comb_ref.shape, 1)
    wt = jnp.sum(jnp.where(lane == e, comb_ref[...], 0.0), axis=-1, keepdims=True)
    acc_ref[...] += wt * _dot(act, wd_ref[...])

    @pl.when((e == pl.num_programs(1) - 1) & (f == pl.num_programs(2) - 1))
    def _():
        o_ref[...] = x_ref[...] + acc_ref[...]


def ffn(x, g, comb, w_gate, w_up, w_down, tm=512, tf=1408):
    t, d = x.shape
    e, _, f = w_gate.shape
    return pl.pallas_call(
        _ffn_body,
        grid=(t // tm, e, f // tf),
        in_specs=[pl.BlockSpec((tm, d), lambda i, j, k: (i, 0)),
                  pl.BlockSpec((1, d), lambda i, j, k: (0, 0)),
                  pl.BlockSpec((tm, LANES), lambda i, j, k: (i, 0)),
                  pl.BlockSpec((None, d, tf), lambda i, j, k: (j, 0, k)),
                  pl.BlockSpec((None, d, tf), lambda i, j, k: (j, 0, k)),
                  pl.BlockSpec((None, tf, d), lambda i, j, k: (j, k, 0))],
        out_specs=pl.BlockSpec((tm, d), lambda i, j, k: (i, 0)),
        out_shape=jax.ShapeDtypeStruct((t, d), F32),
        scratch_shapes=[pltpu.VMEM((tm, d), BF16), pltpu.VMEM((tm, d), F32)],
        compiler_params=_cparams("parallel", "arbitrary", "arbitrary"),
        name="ffn",
    )(x, g.reshape(1, d), comb, w_gate, w_up, w_down)


def _router_body(x_ref, g_ref, w_ref, o_ref):
    h = _rms(x_ref[...], g_ref[...])
    logits = jnp.dot(h, w_ref[...], preferred_element_type=F32, precision=HIGHEST)
    lane = lax.broadcasted_iota(jnp.int32, logits.shape, 1).astype(F32)
    logits = jnp.where(lane < N_EXPERTS, logits, -jnp.inf)
    v1 = jnp.max(logits, axis=-1, keepdims=True)
    i1 = jnp.min(jnp.where(logits == v1, lane, float(LANES)), axis=-1, keepdims=True)
    rest = jnp.where(lane == i1, -jnp.inf, logits)
    v2 = jnp.max(rest, axis=-1, keepdims=True)
    i2 = jnp.min(jnp.where(rest == v2, lane, float(LANES)), axis=-1, keepdims=True)
    e2 = jnp.exp(v2 - v1)
    w1 = 1.0 / (1.0 + e2)
    w2 = e2 / (1.0 + e2)
    o_ref[...] = jnp.where(lane == i1, w1, 0.0) + jnp.where(lane == i2, w2, 0.0)


def moe_router(x, g, w_router, tm=512):
    t, d = x.shape
    w = jnp.pad(w_router, ((0, 0), (0, LANES - w_router.shape[1])))
    return pl.pallas_call(
        _router_body,
        grid=(t // tm,),
        in_specs=[pl.BlockSpec((tm, d), lambda i: (i, 0)),
                  pl.BlockSpec((1, d), lambda i: (0, 0)),
                  pl.BlockSpec((d, LANES), lambda i: (0, 0))],
        out_specs=pl.BlockSpec((tm, LANES), lambda i: (i, 0)),
        out_shape=jax.ShapeDtypeStruct((t, LANES), F32),
        compiler_params=_cparams("parallel"),
        name="moe_router",
    )(x, g.reshape(1, d), w)


def _rmsnorm_body(x_ref, g_ref, o_ref):
    o_ref[...] = _rms(x_ref[...], g_ref[...])


def rmsnorm(x, g, tm=1024):
    t, d = x.shape
    tm = min(tm, t)
    return pl.pallas_call(
        _rmsnorm_body,
        grid=(t // tm,),
        in_specs=[pl.BlockSpec((tm, d), lambda i: (i, 0)), pl.BlockSpec((1, d), lambda i: (0, 0))],
        out_specs=pl.BlockSpec((tm, d), lambda i: (i, 0)),
        out_shape=jax.ShapeDtypeStruct((t, d), F32),
        compiler_params=_cparams("parallel"),
        name="rmsnorm",
    )(x, g.reshape(1, d))


def _in_proj_slices(w):
    sizes = (MOBA_WIDTH, MOBA_WIDTH, MOBA_WIDTH, SSM_WIDTH, GDN_WIDTH, GDN_WIDTH, GDN_WIDTH,
             GDN_HEADS, GDN_HEADS, GDN_WIDTH, w.shape[0], w.shape[0], w.shape[0])
    parts, start = [], 0
    for size in sizes:
        parts.append(w[:, start:start + size])
        start += size
    return parts


def kernel(x, mem, positions, norm_mix, w_in, ssm_a_re, ssm_a_im, ssm_log_dt, ssm_b_re, ssm_b_im, ssm_c_re, ssm_c_im, ssm_d, ssm_w_glu, gdn_conv, gdn_a_log, gdn_dt_bias, gdn_norm, w_up_moba, w_up_gdn, w_out, norm_xa, norm_mem, xa_wq, xa_wk, xa_wv, xa_wo, norm_ffn, ffn_w_gate, ffn_w_up, ffn_w_down, moe_w_router, moe_w_gate, moe_w_up, moe_w_down, norm_final):
    batch, seq, d = x.shape
    depth = w_in.shape[0]
    t = batch * seq
    assert batch == SUBLANES, "the S5 scan packs the batch onto the 8 sublanes of a vreg"
    assert seq % MOBA_BLOCK == 0 and seq % GDN_CHUNK == 0
    nc = seq // GDN_CHUNK
    bf = lambda a: a.astype(BF16)

    xf = x.reshape(t, d)
    memf = mem.reshape(-1, d)
    cos, sin = rope_tables(positions)
    ones_comb = jnp.ones((t, LANES), F32)

    for l in range(depth):
        (wq_m, wk_m, wv_m, wu_s, wq_g, wk_g, wv_g, wa_g, wb_g, wz_g, wg_a, wg_b, wg_c) = _in_proj_slices(w_in[l])
        g_mix = norm_mix[l]
        qk_m = norm_matmul(xf, g_mix, bf(jnp.concatenate([wq_m, wk_m], axis=1)), F32)
        v_m = norm_matmul(xf, g_mix, bf(wv_m), BF16)
        u_s = norm_matmul(xf, g_mix, bf(wu_s), F32)
        qkv_g = norm_matmul(xf, g_mix, bf(jnp.concatenate([wq_g, wk_g, wv_g], axis=1)), F32)
        w_ab = jnp.pad(jnp.concatenate([wa_g, wb_g], axis=1), ((0, 0), (0, LANES - 2 * GDN_HEADS)))
        ab_g = norm_matmul(xf, g_mix, bf(w_ab), F32)
        z_g = norm_matmul(xf, g_mix, bf(wz_g), BF16)
        gates = norm_matmul(xf, g_mix, bf(jnp.concatenate([wg_a, wg_b, wg_c], axis=1)), BF16)

        y_a = moba_attention(qk_m, v_m, cos, sin, batch, seq)

        l_re, l_im, bb_re, bb_im = s5_discretise(ssm_a_re[l], ssm_a_im[l], ssm_log_dt[l], ssm_b_re[l], ssm_b_im[l])
        bd = jnp.concatenate([_block_diag(bb_re.transpose(0, 2, 1)), _block_diag(bb_im.transpose(0, 2, 1))], axis=1)
        cd = jnp.concatenate([_block_diag(ssm_c_re[l].transpose(0, 2, 1)),
                              _block_diag(-ssm_c_im[l].transpose(0, 2, 1))], axis=0)
        u_tm = u_s.reshape(batch, seq, SSM_WIDTH).transpose(1, 0, 2).reshape(t, SSM_WIDTH)
        y_s_tm = s5_gelu(u_tm, bf(bd), bf(cd), l_re, l_im, ssm_d[l])
        y_s = y_s_tm.reshape(seq, batch, SSM_WIDTH).transpose(1, 0, 2).reshape(t, SSM_WIDTH)

        def head_rows(cols):
            return cols.reshape(batch, nc, GDN_CHUNK, GDN_HEADS).transpose(0, 3, 1, 2)
        a_rows = head_rows(ab_g[:, :GDN_HEADS])
        b_rows = head_rows(ab_g[:, GDN_HEADS:2 * GDN_HEADS])
        y_c = gdn_mixer(qkv_g, z_g, a_rows, b_rows, gdn_conv[l], gdn_a_log[l], gdn_dt_bias[l], gdn_norm[l],
                        batch, seq)

        xf = merge_project(xf, y_a, y_s, y_c, gates, bf(w_up_moba[l]), bf(ssm_w_glu[l]), bf(w_up_gdn[l]),
                           bf(w_out[l]))

        k_x = norm_matmul(memf, norm_mem[l], bf(xa_wk[l]), BF16)
        v_x = norm_matmul(memf, norm_mem[l], bf(xa_wv[l]), BF16)
        xf = cross_attention(xf, norm_xa[l], bf(xa_wq[l]), k_x, v_x, bf(xa_wo[l]), batch, seq)

        if l % 2 == 0:
            i = l // 2
            xf = ffn(xf, norm_ffn[l], ones_comb, bf(ffn_w_gate[i:i + 1]), bf(ffn_w_up[i:i + 1]),
                     bf(ffn_w_down[i:i + 1]))
        else:
            i = l // 2
            comb = moe_router(xf, norm_ffn[l], moe_w_router[i])
            xf = ffn(xf, norm_ffn[l], comb, bf(moe_w_gate[i]), bf(moe_w_up[i]), bf(moe_w_down[i]))

    return rmsnorm(xf, norm_final).reshape(batch, seq, d)
```

```python
import functools
import math

import numpy as np
import jax
import jax.numpy as jnp
from jax import lax
from jax.experimental import pallas as pl
from jax.experimental.pallas import tpu as pltpu

F32 = jnp.float32
BF16 = jnp.bfloat16
HIGHEST = lax.Precision.HIGHEST

EPS = 1e-6
NEG_INF = -1e30
MOBA_HEADS = 8
MOBA_HEAD_DIM = 64
MOBA_WIDTH = MOBA_HEADS * MOBA_HEAD_DIM
MOBA_BLOCK = 256
MOBA_TOPK = 3
ROPE_THETA = 10000.0
SSM_WIDTH = 512
SSM_GROUP = 16
SSM_GROUPS = SSM_WIDTH // SSM_GROUP
SSM_STATE = 64
SSM_STATES = SSM_GROUPS * SSM_STATE
GDN_HEADS = 4
GDN_HEAD_DIM = 128
GDN_WIDTH = GDN_HEADS * GDN_HEAD_DIM
GDN_CONV = 4
GDN_CHUNK = 64
XA_HEADS = 4
XA_HEAD_DIM = 128
XA_WIDTH = XA_HEADS * XA_HEAD_DIM
N_EXPERTS = 8
TOP_K = 2

LANES = 128
SUBLANES = 8
VMEM_LIMIT = 56 * 1024 * 1024


def _cparams(*sem):
    return pltpu.CompilerParams(dimension_semantics=sem, vmem_limit_bytes=VMEM_LIMIT)


def _rms(x, g):
    return x * lax.rsqrt(jnp.mean(x * x, axis=-1, keepdims=True) + EPS) * g


def _sigmoid(x):
    return 1.0 / (1.0 + jnp.exp(-x))


def _silu(x):
    return x * _sigmoid(x)


def _dot(a, b):
    return jnp.dot(a, b, preferred_element_type=F32)


def _dot_nt(a, b):
    return lax.dot_general(a, b, (((1,), (1,)), ((), ())), preferred_element_type=F32)


def _dot_tn(a, b):
    return lax.dot_general(a, b, (((0,), (0,)), ((), ())), preferred_element_type=F32)


def _norm_matmul_body(x_ref, g_ref, w_ref, o_ref, h_ref):
    @pl.when(pl.program_id(1) == 0)
    def _():
        h_ref[...] = _rms(x_ref[...], g_ref[...]).astype(BF16)

    o_ref[...] = _dot(h_ref[...], w_ref[...]).astype(o_ref.dtype)


def norm_matmul(x, g, w, out_dtype, tm=512, tn=512):
    t, k = x.shape
    n = w.shape[1]
    tm, tn = min(tm, t), min(tn, n)
    return pl.pallas_call(
        _norm_matmul_body,
        grid=(t // tm, n // tn),
        in_specs=[pl.BlockSpec((tm, k), lambda i, j: (i, 0)),
                  pl.BlockSpec((1, k), lambda i, j: (0, 0)),
                  pl.BlockSpec((k, tn), lambda i, j: (0, j))],
        out_specs=pl.BlockSpec((tm, tn), lambda i, j: (i, j)),
        out_shape=jax.ShapeDtypeStruct((t, n), out_dtype),
        scratch_shapes=[pltpu.VMEM((tm, k), BF16)],
        compiler_params=_cparams("parallel", "arbitrary"),
        name="norm_matmul",
    )(x, g.reshape(1, k), w)


def _rope_body(pos_ref, inv_ref, cos_ref, sin_ref):
    ang = pos_ref[...] * inv_ref[...]
    lane = lax.broadcasted_iota(jnp.int32, ang.shape, 1)
    first_half = (lane % MOBA_HEAD_DIM) < (MOBA_HEAD_DIM // 2)
    cos_ref[...] = jnp.cos(ang)
    s = jnp.sin(ang)
    sin_ref[...] = jnp.where(first_half, -s, s)


def rope_tables(positions):
    t = positions.size
    half = MOBA_HEAD_DIM // 2
    inv = (1.0 / (np.float32(ROPE_THETA) ** (np.arange(0, MOBA_HEAD_DIM, 2, dtype=np.float32)
                                             / np.float32(MOBA_HEAD_DIM)))).astype(np.float32)
    inv_row = jnp.asarray(np.tile(inv, LANES // half).reshape(1, LANES))
    pos = positions.astype(F32).reshape(t, 1)
    ts = min(1024, t)
    return pl.pallas_call(
        _rope_body,
        grid=(t // ts,),
        in_specs=[pl.BlockSpec((ts, 1), lambda i: (i, 0)),
                  pl.BlockSpec((1, LANES), lambda i: (0, 0))],
        out_specs=[pl.BlockSpec((ts, LANES), lambda i: (i, 0))] * 2,
        out_shape=[jax.ShapeDtypeStruct((t, LANES), F32)] * 2,
        compiler_params=_cparams("parallel"),
        name="rope_tables",
    )(pos, inv_row)


def _rope_apply(x, cos, sin_signed):
    lane = lax.broadcasted_iota(jnp.int32, x.shape, 1)
    first_half = (lane % MOBA_HEAD_DIM) < (MOBA_HEAD_DIM // 2)
    partner = jnp.where(first_half, pltpu.roll(x, LANES - MOBA_HEAD_DIM // 2, 1),
                        pltpu.roll(x, MOBA_HEAD_DIM // 2, 1))
    return x * cos + partner * sin_signed


def _moba_body(q_ref, k_ref, v_ref, cq_ref, sq_ref, ck_ref, sk_ref, o_ref, ka_ref, va_ref, km_ref, s_ref,
               *, nb):
    blk = MOBA_BLOCK
    i = pl.program_id(2)

    @pl.when(i == 0)
    def _():
        km_ref[...] = jnp.zeros_like(km_ref)
        lane_b = lax.broadcasted_iota(jnp.int32, (blk, LANES), 1)

        def rope_blk(j, c):
            rows = pl.ds(pl.multiple_of(j * blk, blk), blk)
            kr = _rope_apply(k_ref[rows, :], ck_ref[rows, :], sk_ref[rows, :])
            ka_ref[rows, 0:LANES] = kr.astype(BF16)
            ka_ref[rows, LANES:2 * LANES] = jnp.where(lane_b == j, 1.0, 0.0).astype(BF16)
            km_ref[pl.ds(j, 1), :] = jnp.mean(kr, axis=0, keepdims=True)
            v = v_ref[rows, :].astype(F32)
            va_ref[0, rows, :] = jnp.where(lane_b < MOBA_HEAD_DIM, v,
                                           jnp.where(lane_b == MOBA_HEAD_DIM, 1.0, 0.0)).astype(BF16)
            va_ref[1, rows, :] = jnp.where(lane_b >= MOBA_HEAD_DIM, v,
                                           jnp.where(lane_b == 0, 1.0, 0.0)).astype(BF16)
            return c
        lax.fori_loop(0, nb, rope_blk, 0)

    lane = lax.broadcasted_iota(jnp.int32, (1, LANES), 1)
    head_a = lane < MOBA_HEAD_DIM
    q = _rope_apply(q_ref[...], cq_ref[...], sq_ref[...])
    scale = MOBA_HEAD_DIM ** -0.5
    km = km_ref[...]
    nbp = -(-nb // SUBLANES) * SUBLANES
    brow = lax.broadcasted_iota(jnp.int32, (nbp, blk), 0)
    own = pl.ds(pl.multiple_of(i * blk, blk), blk)
    r_idx = lax.broadcasted_iota(jnp.int32, (blk, blk), 0)
    c_idx = lax.broadcasted_iota(jnp.int32, (blk, blk), 1)
    causal = c_idx <= r_idx
    n_pairs = (i + 1) // 2
    outs = []
    for h in range(2):
        qh = jnp.where(head_a, q, 0.0) if h == 0 else jnp.where(head_a, 0.0, q)
        gate = lax.dot_general(km, qh, (((1,), (1,)), ((), ())),
                               preferred_element_type=F32, precision=HIGHEST)[:nbp]
        gate = jnp.where(brow < i, gate, NEG_INF)
        bias_rows = []
        for j in range(nb):
            gj = gate[j:j + 1, :]
            ahead = (gate > gj) | ((gate == gj) & (brow < j))
            rank = jnp.sum(jnp.where(ahead, 1.0, 0.0), axis=0, keepdims=True)
            sel = (rank < float(MOBA_TOPK)) & (j < i)
            bias_rows.append(jnp.where(sel, 0.0, NEG_INF))
        bias_t = jnp.concatenate(bias_rows + [jnp.zeros((LANES - nb, blk), F32)], axis=0)
        q_aug = jnp.concatenate([qh * scale, bias_t.T], axis=1).astype(BF16)

        s_own = jnp.where(causal, _dot_nt(q_aug[:, :LANES], ka_ref[own, 0:LANES]), NEG_INF)
        m_tile = jnp.maximum(s_own[:, :LANES], s_own[:, LANES:])

        def scores(jp, m_t):
            rows = pl.ds(pl.multiple_of(jp * (2 * blk), 2 * blk), 2 * blk)
            s2 = _dot_nt(q_aug, ka_ref[rows, :])
            s_ref[jp] = s2
            for c in range(0, 2 * blk, LANES):
                m_t = jnp.maximum(m_t, s2[:, c:c + LANES])
            return m_t
        m_tile = lax.fori_loop(0, n_pairs, scores, m_tile)
        m = jnp.max(m_tile, axis=-1, keepdims=True)

        acc = _dot(jnp.exp(s_own - m).astype(BF16), va_ref[h, own, :])

        def values(jp, a):
            rows = pl.ds(pl.multiple_of(jp * (2 * blk), 2 * blk), 2 * blk)
            p2 = jnp.exp(s_ref[jp] - m).astype(BF16)
            return a + _dot(p2, va_ref[h, rows, :])
        acc = lax.fori_loop(0, n_pairs, values, acc)
        ones_lane = MOBA_HEAD_DIM if h == 0 else 0
        outs.append(acc / acc[:, ones_lane:ones_lane + 1])
    o_ref[...] = jnp.where(head_a, outs[0], outs[1]).astype(o_ref.dtype)


def moba_attention(qk, v, cos, sin, batch, seq):
    nb = seq // MOBA_BLOCK
    pairs = MOBA_WIDTH // LANES
    blk = MOBA_BLOCK
    t = batch * seq
    return pl.pallas_call(
        functools.partial(_moba_body, nb=nb),
        grid=(batch, pairs, nb),
        in_specs=[pl.BlockSpec((blk, LANES), lambda b, p, i: (b * nb + i, p)),
                  pl.BlockSpec((seq, LANES), lambda b, p, i: (b, pairs + p)),
                  pl.BlockSpec((seq, LANES), lambda b, p, i: (b, p)),
                  pl.BlockSpec((blk, LANES), lambda b, p, i: (b * nb + i, 0)),
                  pl.BlockSpec((blk, LANES), lambda b, p, i: (b * nb + i, 0)),
                  pl.BlockSpec((seq, LANES), lambda b, p, i: (b, 0)),
                  pl.BlockSpec((seq, LANES), lambda b, p, i: (b, 0))],
        out_specs=pl.BlockSpec((blk, LANES), lambda b, p, i: (b * nb + i, p)),
        out_shape=jax.ShapeDtypeStruct((t, MOBA_WIDTH), BF16),
        scratch_shapes=[pltpu.VMEM((seq, 2 * LANES), BF16),
                        pltpu.VMEM((2, seq, LANES), BF16),
                        pltpu.VMEM((LANES, LANES), F32),
                        pltpu.VMEM((nb // 2, blk, 2 * blk), F32)],
        compiler_params=_cparams("parallel", "parallel", "arbitrary"),
        name="moba_attention",
    )(qk, qk, v, cos, sin, cos, sin)


def _s5_disc_body(are_ref, aim_ref, ldt_ref, bre_ref, bim_ref, lre_ref, lim_ref, bbre_ref, bbim_ref):
    a_re, a_im = are_ref[...], aim_ref[...]
    dt = jnp.exp(ldt_ref[...])
    mag = jnp.exp(a_re * dt)
    l_re = mag * jnp.cos(a_im * dt)
    l_im = mag * jnp.sin(a_im * dt)
    lre_ref[...] = l_re
    lim_ref[...] = l_im
    x, y = l_re - 1.0, l_im
    den = a_re * a_re + a_im * a_im
    c_re = (x * a_re + y * a_im) / den
    c_im = (y * a_re - x * a_im) / den
    b_re, b_im = bre_ref[...], bim_ref[...]
    bbre_ref[...] = c_re * b_re - c_im * b_im
    bbim_ref[...] = c_re * b_im + c_im * b_re


def s5_discretise(a_re, a_im, log_dt, b_re, b_im):
    g, p = a_re.shape
    n = b_re.shape[-1]
    col = lambda a: a.reshape(g * p, 1)
    ldt = jnp.broadcast_to(log_dt[:, None], (g, p))
    outs = pl.pallas_call(
        _s5_disc_body,
        out_shape=[jax.ShapeDtypeStruct((g * p, 1), F32)] * 2 + [jax.ShapeDtypeStruct((g * p, n), F32)] * 2,
        name="s5_discretise",
    )(col(a_re), col(a_im), col(ldt), b_re.reshape(g * p, n), b_im.reshape(g * p, n))
    l_re, l_im, bb_re, bb_im = outs
    return l_re.reshape(g, p), l_im.reshape(g, p), bb_re.reshape(g, p, n), bb_im.reshape(g, p, n)


def _block_diag(blocks):
    g, r, c = blocks.shape
    eye = jnp.eye(g, dtype=blocks.dtype)
    return (blocks[:, :, None, :] * eye[:, None, :, None]).reshape(g * r, g * c)


def _s5_body(u_ref, bd_ref, cd_ref, lre_ref, lim_ref, d_ref, y_ref, h_ref, hb_ref, *, steps):
    rows = SUBLANES
    ns = SSM_STATES

    @pl.when(pl.program_id(0) == 0)
    def _():
        h_ref[...] = jnp.zeros_like(h_ref)

    u = u_ref[...]
    hb_ref[...] = _dot(u.astype(BF16), bd_ref[...])
    l_re, l_im = lre_ref[...], lim_ref[...]

    def step(t, carry):
        h_re, h_im = carry
        r = pl.ds(pl.multiple_of(t * rows, rows), rows)
        n_re = l_re * h_re - l_im * h_im + hb_ref[r, 0:ns]
        n_im = l_re * h_im + l_im * h_re + hb_ref[r, ns:2 * ns]
        hb_ref[r, 0:ns] = n_re
        hb_ref[r, ns:2 * ns] = n_im
        return n_re, n_im

    h_re, h_im = lax.fori_loop(0, steps, step, (h_ref[:, 0:ns], h_ref[:, ns:2 * ns]))
    h_ref[:, 0:ns] = h_re
    h_ref[:, ns:2 * ns] = h_im
    y = _dot(hb_ref[...].astype(BF16), cd_ref[...]) + d_ref[...] * u
    y_ref[...] = jax.nn.gelu(y).astype(y_ref.dtype)


def s5_gelu(u_tm, bd, cd, l_re, l_im, d_skip, steps=64):
    rows = u_tm.shape[0]
    blk = steps * SUBLANES
    ns2 = 2 * SSM_STATES
    return pl.pallas_call(
        functools.partial(_s5_body, steps=steps),
        grid=(rows // blk,),
        in_specs=[pl.BlockSpec((blk, SSM_WIDTH), lambda c: (c, 0)),
                  pl.BlockSpec((SSM_WIDTH, ns2), lambda c: (0, 0)),
                  pl.BlockSpec((ns2, SSM_WIDTH), lambda c: (0, 0)),
                  pl.BlockSpec((1, SSM_STATES), lambda c: (0, 0)),
                  pl.BlockSpec((1, SSM_STATES), lambda c: (0, 0)),
                  pl.BlockSpec((1, SSM_WIDTH), lambda c: (0, 0))],
        out_specs=pl.BlockSpec((blk, SSM_WIDTH), lambda c: (c, 0)),
        out_shape=jax.ShapeDtypeStruct((rows, SSM_WIDTH), BF16),
        scratch_shapes=[pltpu.VMEM((SUBLANES, ns2), F32),
                        pltpu.VMEM((blk, ns2), F32)],
        compiler_params=_cparams("arbitrary"),
        name="s5_scan",
    )(u_tm, bd, cd, l_re.reshape(1, -1), l_im.reshape(1, -1), d_skip.reshape(1, -1))


GDN_GROUP = 8


def _gdn_prep_body(q_ref, k_ref, v_ref, a_ref, b_ref, wq_ref, wk_ref, wv_ref, alog_ref, dtb_ref,
                   u_ref, w_ref, qd_ref, kd_ref, qk_ref, gl_ref, qs_ref, ks_ref, vs_ref,
                   *, seq):
    L = GDN_CHUNK
    G = min(GDN_GROUP, seq // L)
    dk = GDN_HEAD_DIM
    nc = seq // L
    hd = pl.program_id(1)
    cb = 256
    halo = SUBLANES

    def conv_blk(rb, c):
        base = pl.multiple_of(rb * cb, cb)
        prev = pl.multiple_of(jnp.maximum(base - halo, 0), halo)
        for src, wref, dst, norm, mul in ((q_ref, wq_ref, qs_ref, True, dk ** -0.5),
                                          (k_ref, wk_ref, ks_ref, True, 1.0),
                                          (v_ref, wv_ref, vs_ref, False, 1.0)):
            cur = src[pl.ds(base, cb), :]
            head = jnp.where(rb > 0, src[pl.ds(prev, halo), :], 0.0)
            ext = jnp.concatenate([head, cur], axis=0)
            wts = wref[...]
            y = ext[halo:halo + cb] * wts[GDN_CONV - 1:GDN_CONV]
            for tap in range(1, GDN_CONV):
                y = y + ext[halo - tap:halo - tap + cb] * wts[GDN_CONV - 1 - tap:GDN_CONV - tap]
            y = _silu(y)
            if norm:
                y = y * lax.rsqrt(jnp.sum(y * y, axis=-1, keepdims=True) + EPS) * mul
            dst[pl.ds(base, cb), :] = y
        return c
    lax.fori_loop(0, seq // cb, conv_blk, 0)

    gl_rows = G * L
    ri = lax.broadcasted_iota(jnp.int32, (gl_rows, L), 0) % L
    ci = lax.broadcasted_iota(jnp.int32, (gl_rows, L), 1)
    tril = ci <= ri
    strict = ci < ri
    eye = ci == ri
    ri3 = lax.broadcasted_iota(jnp.int32, (G, L, L), 1)
    ci3 = lax.broadcasted_iota(jnp.int32, (G, L, L), 2)
    strict3 = ci3 < ri3
    eye_f = jnp.where(ci3 == ri3, 1.0, 0.0)
    r2 = lax.broadcasted_iota(jnp.int32, (L, L), 0)
    c2 = lax.broadcasted_iota(jnp.int32, (L, L), 1)
    upper_f = jnp.where(r2 <= c2, 1.0, 0.0)
    chunk_of_row = lax.broadcasted_iota(jnp.int32, (gl_rows, G), 0) // L
    expand = jnp.where(chunk_of_row == lax.broadcasted_iota(jnp.int32, (gl_rows, G), 1), 1.0, 0.0)
    a_coef = -jnp.exp(jnp.full((1, L), alog_ref[hd], F32))
    dt_bias = dtb_ref[hd]
    hp = functools.partial(jnp.dot, preferred_element_type=F32, precision=HIGHEST)

    def bmm(a, b):
        return jnp.einsum('gij,gjk->gik', a, b, preferred_element_type=F32)

    def bmm_nt(a, b):
        return jnp.einsum('gid,gjd->gij', a, b, preferred_element_type=F32)

    def to_col(rows_b):
        return jnp.sum(jnp.where(eye, rows_b, 0.0), axis=1, keepdims=True)

    def prep(cg, carry):
        c0 = pl.multiple_of(cg * G, G)
        rows = pl.ds(pl.multiple_of(cg * gl_rows, gl_rows), gl_rows)
        qc, kc, vc = qs_ref[rows, :], ks_ref[rows, :], vs_ref[rows, :]
        z = a_ref[pl.ds(c0, G), :] + dt_bias
        softplus = jnp.maximum(z, 0.0) + jnp.log(1.0 + jnp.exp(-jnp.abs(z)))
        g_rows = hp(a_coef * softplus, upper_f)
        beta_rows = _sigmoid(b_ref[pl.ds(c0, G), :])
        both = hp(expand, jnp.concatenate([g_rows, beta_rows], axis=1))
        g_rb = both[:, :L]
        g_col = to_col(g_rb)
        beta_col = to_col(both[:, L:])
        g_last = g_rb[:, L - 1:L]
        decay = jnp.where(tril, jnp.exp(jnp.where(tril, g_col - g_rb, 0.0)), 0.0)
        k_beta = kc * beta_col
        v_beta = vc * beta_col
        to3 = lambda a: a.reshape(G, L, a.shape[-1])
        kb16, k16 = to3(k_beta.astype(BF16)), to3(kc.astype(BF16))
        a_low = jnp.where(strict3, bmm_nt(kb16, k16) * to3(decay), 0.0)
        t_inv = eye_f - a_low
        pw = a_low
        span = 2
        while span < L:
            pw16 = pw.astype(BF16)
            pw = bmm(pw16, pw16)
            t_inv = t_inv + bmm(t_inv.astype(BF16), pw.astype(BF16))
            span *= 2
        ti16 = t_inv.astype(BF16)
        u = bmm(ti16, to3(v_beta.astype(BF16)))
        w = bmm(ti16, to3((k_beta * jnp.exp(g_col)).astype(BF16)))
        qk = bmm_nt(to3(qc.astype(BF16)), k16) * to3(decay)
        u_ref[rows, :] = u.reshape(gl_rows, dk)
        w_ref[rows, :] = w.reshape(gl_rows, dk).astype(BF16)
        qd_ref[rows, :] = (qc * jnp.exp(g_col)).astype(BF16)
        kd_ref[rows, :] = (kc * jnp.exp(g_last - g_col)).astype(BF16)
        qk_ref[pl.ds(c0, G)] = qk.astype(BF16)
        gl_ref[pl.ds(c0, G), :] = jnp.broadcast_to(jnp.exp(g_rows[:, L - 1:L]), (G, LANES))
        return carry
    lax.fori_loop(0, nc // G, prep, 0)


def _gdn_scan_body(u_ref, w_ref, qd_ref, kd_ref, qk_ref, gl_ref, z_ref, gn_ref, o_ref, st_ref, *, chunks):
    L = GDN_CHUNK
    d = GDN_HEAD_DIM

    @pl.when(pl.program_id(1) == 0)
    def _():
        st_ref[...] = jnp.zeros_like(st_ref)

    gn = gn_ref[...]

    def chunk(c, carry):
        rows = pl.ds(pl.multiple_of(c * L, L), L)
        for h in range(GDN_HEADS):
            cols = slice(h * d, (h + 1) * d)
            s = st_ref[h]
            s16 = s.astype(BF16)
            v_new = u_ref[rows, cols] - _dot(w_ref[rows, cols], s16)
            vn16 = v_new.astype(BF16)
            o = _dot(qd_ref[rows, cols], s16) + _dot(qk_ref[h, c], vn16)
            st_ref[h] = s * gl_ref[h, pl.ds(c, 1), :] + _dot_tn(kd_ref[rows, cols], vn16)
            o_ref[rows, cols] = (_rms(o, gn) * _silu(z_ref[rows, cols].astype(F32))).astype(o_ref.dtype)
        return carry
    lax.fori_loop(0, chunks, chunk, 0)


def gdn_mixer(qkv, z, a_rows, b_rows, conv_w, a_log, dt_bias, gnorm, batch, seq, ts=1024):
    t = batch * seq
    hds = GDN_HEADS
    nc = seq // GDN_CHUNK
    L = GDN_CHUNK
    d = GDN_HEAD_DIM
    seq_spec = lambda off: pl.BlockSpec((seq, d), lambda b, h: (b, off + h))
    w_spec = lambda off: pl.BlockSpec((GDN_CONV, d), lambda b, h: (0, off + h))
    row_spec = pl.BlockSpec((None, None, nc, L), lambda b, h: (b, h, 0, 0))
    smem = pl.BlockSpec(memory_space=pltpu.SMEM)
    head_out = pl.BlockSpec((seq, d), lambda b, h: (b, h))
    u, w, qd, kd, qk, gl = pl.pallas_call(
        functools.partial(_gdn_prep_body, seq=seq),
        grid=(batch, hds),
        in_specs=[seq_spec(0), seq_spec(hds), seq_spec(2 * hds), row_spec, row_spec,
                  w_spec(0), w_spec(hds), w_spec(2 * hds), smem, smem],
        out_specs=[head_out, head_out, head_out, head_out,
                   pl.BlockSpec((None, None, nc, L, L), lambda b, h: (b, h, 0, 0, 0)),
                   pl.BlockSpec((None, None, nc, LANES), lambda b, h: (b, h, 0, 0))],
        out_shape=[jax.ShapeDtypeStruct((t, GDN_WIDTH), F32),
                   jax.ShapeDtypeStruct((t, GDN_WIDTH), BF16),
                   jax.ShapeDtypeStruct((t, GDN_WIDTH), BF16),
                   jax.ShapeDtypeStruct((t, GDN_WIDTH), BF16),
                   jax.ShapeDtypeStruct((batch, hds, nc, L, L), BF16),
                   jax.ShapeDtypeStruct((batch, hds, nc, LANES), F32)],
        scratch_shapes=[pltpu.VMEM((seq, d), F32), pltpu.VMEM((seq, d), F32), pltpu.VMEM((seq, d), F32)],
        compiler_params=_cparams("parallel", "parallel"),
        name="gdn_prep",
    )(qkv, qkv, qkv, a_rows, b_rows, conv_w, conv_w, conv_w, a_log, dt_bias)

    ts = min(ts, seq)
    spb = seq // ts
    cps = ts // L
    tok = pl.BlockSpec((ts, GDN_WIDTH), lambda b, s: (b * spb + s, 0))
    return pl.pallas_call(
        functools.partial(_gdn_scan_body, chunks=cps),
        grid=(batch, spb),
        in_specs=[tok, tok, tok, tok,
                  pl.BlockSpec((None, hds, cps, L, L), lambda b, s: (b, 0, s, 0, 0)),
                  pl.BlockSpec((None, hds, cps, LANES), lambda b, s: (b, 0, s, 0)),
                  tok, pl.BlockSpec((1, d), lambda b, s: (0, 0))],
        out_specs=tok,
        out_shape=jax.ShapeDtypeStruct((t, GDN_WIDTH), BF16),
        scratch_shapes=[pltpu.VMEM((hds, d, d), F32)],
        compiler_params=_cparams("parallel", "arbitrary"),
        name="gdn_scan",
    )(u, w, qd, kd, qk, gl, z, gnorm.reshape(1, d))


def _merge_body(x_ref, ya_ref, ys_ref, yc_ref, gt_ref, wa_ref, wg_ref, wc_ref, wo_ref, o_ref):
    d = x_ref.shape[1]
    y_a = _dot(ya_ref[...], wa_ref[...])
    glu = _dot(ys_ref[...], wg_ref[...])
    y_b = glu[:, :d] * _sigmoid(glu[:, d:])
    y_c = _dot(yc_ref[...], wc_ref[...])
    gt = gt_ref[...].astype(F32)
    merged = (_sigmoid(gt[:, :d]) * y_a + _sigmoid(gt[:, d:2 * d]) * y_b
              + _sigmoid(gt[:, 2 * d:]) * y_c)
    o_ref[...] = x_ref[...] + _dot(merged.astype(BF16), wo_ref[...])


def merge_project(x, y_a, y_s, y_c, gates, w_a, w_glu, w_c, w_o, tm=256):
    t, d = x.shape
    row = lambda n: pl.BlockSpec((tm, n), lambda i: (i, 0))
    full = lambda a: pl.BlockSpec(a.shape, lambda i: (0, 0))
    return pl.pallas_call(
        _merge_body,
        grid=(t // tm,),
        in_specs=[row(d), row(y_a.shape[1]), row(y_s.shape[1]), row(y_c.shape[1]), row(3 * d),
                  full(w_a), full(w_glu), full(w_c), full(w_o)],
        out_specs=row(d),
        out_shape=jax.ShapeDtypeStruct((t, d), F32),
        compiler_params=_cparams("parallel"),
        name="merge_project",
    )(x, y_a, y_s, y_c, gates, w_a, w_glu, w_c, w_o)


def _xattn_body(x_ref, g_ref, wq_ref, k_ref, v_ref, wo_ref, o_ref):
    x = x_ref[...]
    h = _rms(x, g_ref[...]).astype(BF16)
    q = _dot(h, wq_ref[...]).astype(BF16)
    outs = []
    for hd in range(XA_HEADS):
        cols = slice(hd * XA_HEAD_DIM, (hd + 1) * XA_HEAD_DIM)
        s = _dot_nt(q[:, cols], k_ref[:, cols]) * (XA_HEAD_DIM ** -0.5)
        p = jnp.exp(s - jnp.max(s, axis=-1, keepdims=True))
        p = p / jnp.sum(p, axis=-1, keepdims=True)
        outs.append(_dot(p.astype(BF16), v_ref[:, cols]))
    o = jnp.concatenate(outs, axis=1).astype(BF16)
    o_ref[...] = x + _dot(o, wo_ref[...])


def cross_attention(x, g, wq, k, v, wo, batch, seq, tm=512):
    t, d = x.shape
    m = k.shape[0] // batch
    spb = seq // tm
    full = lambda a: pl.BlockSpec(a.shape, lambda i: (0, 0))
    return pl.pallas_call(
        _xattn_body,
        grid=(t // tm,),
        in_specs=[pl.BlockSpec((tm, d), lambda i: (i, 0)),
                  pl.BlockSpec((1, d), lambda i: (0, 0)),
                  full(wq),
                  pl.BlockSpec((m, XA_WIDTH), lambda i: (i // spb, 0)),
                  pl.BlockSpec((m, XA_WIDTH), lambda i: (i // spb, 0)),
                  full(wo)],
        out_specs=pl.BlockSpec((tm, d), lambda i: (i, 0)),
        out_shape=jax.ShapeDtypeStruct((t, d), F32),
        compiler_params=_cparams("parallel"),
        name="cross_attention",
    )(x, g.reshape(1, d), wq, k, v, wo)


def _ffn_body(x_ref, g_ref, comb_ref, wg_ref, wu_ref, wd_ref, o_ref, h_ref, acc_ref):
    e, f = pl.program_id(1), pl.program_id(2)

    @pl.when((e == 0) & (f == 0))
    def _():
        h_ref[...] = _rms(x_ref[...], g_ref[...]).astype(BF16)
        acc_ref[...] = jnp.zeros_like(acc_ref)

    h = h_ref[...]
    act = (_silu(_dot(h, wg_ref[...])) * _dot(h, wu_ref[...])).astype(BF16)
    lane = lax.broadcasted_iota(jnp.int32, comb_ref.shape, 1)
    wt = jnp.sum(jnp.where(lane == e, comb_ref[...], 0.0), axis=-1, keepdims=True)
    acc_ref[...] += wt * _dot(act, wd_ref[...])

    @pl.when((e == pl.num_programs(1) - 1) & (f == pl.num_programs(2) - 1))
    def _():
        o_ref[...] = x_ref[...] + acc_ref[...]


def ffn(x, g, comb, w_gate, w_up, w_down, tm=512, tf=1408):
    t, d = x.shape
    e, _, f = w_gate.shape
    return pl.pallas_call(
        _ffn_body,
        grid=(t // tm, e, f // tf),
        in_specs=[pl.BlockSpec((tm, d), lambda i, j, k: (i, 0)),
                  pl.BlockSpec((1, d), lambda i, j, k: (0, 0)),
                  pl.BlockSpec((tm, LANES), lambda i, j, k: (i, 0)),
                  pl.BlockSpec((None, d, tf), lambda i, j, k: (j, 0, k)),
                  pl.BlockSpec((None, d, tf), lambda i, j, k: (j, 0, k)),
                  pl.BlockSpec((None, tf, d), lambda i, j, k: (j, k, 0))],
        out_specs=pl.BlockSpec((tm, d), lambda i, j, k: (i, 0)),
        out_shape=jax.ShapeDtypeStruct((t, d), F32),
        scratch_shapes=[pltpu.VMEM((tm, d), BF16), pltpu.VMEM((tm, d), F32)],
        compiler_params=_cparams("parallel", "arbitrary", "arbitrary"),
        name="ffn",
    )(x, g.reshape(1, d), comb, w_gate, w_up, w_down)


def _router_body(x_ref, g_ref, w_ref, o_ref):
    h = _rms(x_ref[...], g_ref[...])
    logits = jnp.dot(h, w_ref[...], preferred_element_type=F32, precision=HIGHEST)
    lane = lax.broadcasted_iota(jnp.int32, logits.shape, 1).astype(F32)
    logits = jnp.where(lane < N_EXPERTS, logits, -jnp.inf)
    v1 = jnp.max(logits, axis=-1, keepdims=True)
    i1 = jnp.min(jnp.where(logits == v1, lane, float(LANES)), axis=-1, keepdims=True)
    rest = jnp.where(lane == i1, -jnp.inf, logits)
    v2 = jnp.max(rest, axis=-1, keepdims=True)
    i2 = jnp.min(jnp.where(rest == v2, lane, float(LANES)), axis=-1, keepdims=True)
    e2 = jnp.exp(v2 - v1)
    w1 = 1.0 / (1.0 + e2)
    w2 = e2 / (1.0 + e2)
    o_ref[...] = jnp.where(lane == i1, w1, 0.0) + jnp.where(lane == i2, w2, 0.0)


def moe_router(x, g, w_router, tm=512):
    t, d = x.shape
    w = jnp.pad(w_router, ((0, 0), (0, LANES - w_router.shape[1])))
    return pl.pallas_call(
        _router_body,
        grid=(t // tm,),
        in_specs=[pl.BlockSpec((tm, d), lambda i: (i, 0)),
                  pl.BlockSpec((1, d), lambda i: (0, 0)),
                  pl.BlockSpec((d, LANES), lambda i: (0, 0))],
        out_specs=pl.BlockSpec((tm, LANES), lambda i: (i, 0)),
        out_shape=jax.ShapeDtypeStruct((t, LANES), F32),
        compiler_params=_cparams("parallel"),
        name="moe_router",
    )(x, g.reshape(1, d), w)


def _rmsnorm_body(x_ref, g_ref, o_ref):
    o_ref[...] = _rms(x_ref[...], g_ref[...])


def rmsnorm(x, g, tm=1024):
    t, d = x.shape
    tm = min(tm, t)
    return pl.pallas_call(
        _rmsnorm_body,
        grid=(t // tm,),
        in_specs=[pl.BlockSpec((tm, d), lambda i: (i, 0)), pl.BlockSpec((1, d), lambda i: (0, 0))],
        out_specs=pl.BlockSpec((tm, d), lambda i: (i, 0)),
        out_shape=jax.ShapeDtypeStruct((t, d), F32),
        compiler_params=_cparams("parallel"),
        name="rmsnorm",
    )(x, g.reshape(1, d))


def _in_proj_slices(w):
    sizes = (MOBA_WIDTH, MOBA_WIDTH, MOBA_WIDTH, SSM_WIDTH, GDN_WIDTH, GDN_WIDTH, GDN_WIDTH,
             GDN_HEADS, GDN_HEADS, GDN_WIDTH, w.shape[0], w.shape[0], w.shape[0])
    parts, start = [], 0
    for size in sizes:
        parts.append(w[:, start:start + size])
        start += size
    return parts


def kernel(x, mem, positions, norm_mix, w_in, ssm_a_re, ssm_a_im, ssm_log_dt, ssm_b_re, ssm_b_im, ssm_c_re, ssm_c_im, ssm_d, ssm_w_glu, gdn_conv, gdn_a_log, gdn_dt_bias, gdn_norm, w_up_moba, w_up_gdn, w_out, norm_xa, norm_mem, xa_wq, xa_wk, xa_wv, xa_wo, norm_ffn, ffn_w_gate, ffn_w_up, ffn_w_down, moe_w_router, moe_w_gate, moe_w_up, moe_w_down, norm_final):
    batch, seq, d = x.shape
    depth = w_in.shape[0]
    t = batch * seq
    assert batch == SUBLANES, "the S5 scan packs the batch onto the 8 sublanes of a vreg"
    assert seq % MOBA_BLOCK == 0 and seq % GDN_CHUNK == 0
    nc = seq // GDN_CHUNK
    bf = lambda a: a.astype(BF16)

    xf = x.reshape(t, d)
    memf = mem.reshape(-1, d)
    cos, sin = rope_tables(positions)
    ones_comb = jnp.ones((t, LANES), F32)

    for l in range(depth):
        (wq_m, wk_m, wv_m, wu_s, wq_g, wk_g, wv_g, wa_g, wb_g, wz_g, wg_a, wg_b, wg_c) = _in_proj_slices(w_in[l])
        g_mix = norm_mix[l]
        qk_m = norm_matmul(xf, g_mix, bf(jnp.concatenate([wq_m, wk_m], axis=1)), F32)
        v_m = norm_matmul(xf, g_mix, bf(wv_m), BF16)
        u_s = norm_matmul(xf, g_mix, bf(wu_s), F32)
        qkv_g = norm_matmul(xf, g_mix, bf(jnp.concatenate([wq_g, wk_g, wv_g], axis=1)), F32)
        w_ab = jnp.pad(jnp.concatenate([wa_g, wb_g], axis=1), ((0, 0), (0, LANES - 2 * GDN_HEADS)))
        ab_g = norm_matmul(xf, g_mix, bf(w_ab), F32)
        z_g = norm_matmul(xf, g_mix, bf(wz_g), BF16)
        gates = norm_matmul(xf, g_mix, bf(jnp.concatenate([wg_a, wg_b, wg_c], axis=1)), BF16)

        y_a = moba_attention(qk_m, v_m, cos, sin, batch, seq)

        l_re, l_im, bb_re, bb_im = s5_discretise(ssm_a_re[l], ssm_a_im[l], ssm_log_dt[l], ssm_b_re[l], ssm_b_im[l])
        bd = jnp.concatenate([_block_diag(bb_re.transpose(0, 2, 1)), _block_diag(bb_im.transpose(0, 2, 1))], axis=1)
        cd = jnp.concatenate([_block_diag(ssm_c_re[l].transpose(0, 2, 1)),
                              _block_diag(-ssm_c_im[l].transpose(0, 2, 1))], axis=0)
        u_tm = u_s.reshape(batch, seq, SSM_WIDTH).transpose(1, 0, 2).reshape(t, SSM_WIDTH)
        y_s_tm = s5_gelu(u_tm, bf(bd), bf(cd), l_re, l_im, ssm_d[l])
        y_s = y_s_tm.reshape(seq, batch, SSM_WIDTH).transpose(1, 0, 2).reshape(t, SSM_WIDTH)

        def head_rows(cols):
            return cols.reshape(batch, nc, GDN_CHUNK, GDN_HEADS).transpose(0, 3, 1, 2)
        a_rows = head_rows(ab_g[:, :GDN_HEADS])
        b_rows = head_rows(ab_g[:, GDN_HEADS:2 * GDN_HEADS])
        y_c = gdn_mixer(qkv_g, z_g, a_rows, b_rows, gdn_conv[l], gdn_a_log[l], gdn_dt_bias[l], gdn_norm[l],
                        batch, seq)

        xf = merge_project(xf, y_a, y_s, y_c, gates, bf(w_up_moba[l]), bf(ssm_w_glu[l]), bf(w_up_gdn[l]),
                           bf(w_out[l]))

        k_x = norm_matmul(memf, norm_mem[l], bf(xa_wk[l]), BF16)
        v_x = norm_matmul(memf, norm_mem[l], bf(xa_wv[l]), BF16)
        xf = cross_attention(xf, norm_xa[l], bf(xa_wq[l]), k_x, v_x, bf(xa_wo[l]), batch, seq)

        if l % 2 == 0:
            i = l // 2
            xf = ffn(xf, norm_ffn[l], ones_comb, bf(ffn_w_gate[i:i + 1]), bf(ffn_w_up[i:i + 1]),
                     bf(ffn_w_down[i:i + 1]))
        else:
            i = l // 2
            comb = moe_router(xf, norm_ffn[l], moe_w_router[i])
            xf = ffn(xf, norm_ffn[l], comb, bf(moe_w_gate[i]), bf(moe_w_up[i]), bf(moe_w_down[i]))

    return rmsnorm(xf, norm_final).reshape(batch, seq, d)
```

```python
import functools
import math

import numpy as np
import jax
import jax.numpy as jnp
from jax import lax
from jax.experimental import pallas as pl
from jax.experimental.pallas import tpu as pltpu

F32 = jnp.float32
BF16 = jnp.bfloat16
HIGHEST = lax.Precision.HIGHEST

EPS = 1e-6
NEG_INF = -1e30
MOBA_HEADS = 8
MOBA_HEAD_DIM = 64
MOBA_WIDTH = MOBA_HEADS * MOBA_HEAD_DIM
MOBA_BLOCK = 256
MOBA_TOPK = 3
ROPE_THETA = 10000.0
SSM_WIDTH = 512
SSM_GROUP = 16
SSM_GROUPS = SSM_WIDTH // SSM_GROUP
SSM_STATE = 64
SSM_STATES = SSM_GROUPS * SSM_STATE
GDN_HEADS = 4
GDN_HEAD_DIM = 128
GDN_WIDTH = GDN_HEADS * GDN_HEAD_DIM
GDN_CONV = 4
GDN_CHUNK = 64
XA_HEADS = 4
XA_HEAD_DIM = 128
XA_WIDTH = XA_HEADS * XA_HEAD_DIM
N_EXPERTS = 8
TOP_K = 2

LANES = 128
SUBLANES = 8
VMEM_LIMIT = 56 * 1024 * 1024


def _cparams(*sem):
    return pltpu.CompilerParams(dimension_semantics=sem, vmem_limit_bytes=VMEM_LIMIT)


def _rms(x, g):
    return x * lax.rsqrt(jnp.mean(x * x, axis=-1, keepdims=True) + EPS) * g


def _sigmoid(x):
    return 1.0 / (1.0 + jnp.exp(-x))


def _silu(x):
    return x * _sigmoid(x)


def _dot(a, b):
    return jnp.dot(a, b, preferred_element_type=F32)


def _dot_nt(a, b):
    return lax.dot_general(a, b, (((1,), (1,)), ((), ())), preferred_element_type=F32)


def _dot_tn(a, b):
    return lax.dot_general(a, b, (((0,), (0,)), ((), ())), preferred_element_type=F32)


def _norm_matmul_body(x_ref, g_ref, w_ref, o_ref, h_ref):
    @pl.when(pl.program_id(1) == 0)
    def _():
        h_ref[...] = _rms(x_ref[...], g_ref[...]).astype(BF16)

    o_ref[...] = _dot(h_ref[...], w_ref[...]).astype(o_ref.dtype)


def norm_matmul(x, g, w, out_dtype, tm=512, tn=512):
    t, k = x.shape
    n = w.shape[1]
    tm, tn = min(tm, t), min(tn, n)
    return pl.pallas_call(
        _norm_matmul_body,
        grid=(t // tm, n // tn),
        in_specs=[pl.BlockSpec((tm, k), lambda i, j: (i, 0)),
                  pl.BlockSpec((1, k), lambda i, j: (0, 0)),
                  pl.BlockSpec((k, tn), lambda i, j: (0, j))],
        out_specs=pl.BlockSpec((tm, tn), lambda i, j: (i, j)),
        out_shape=jax.ShapeDtypeStruct((t, n), out_dtype),
        scratch_shapes=[pltpu.VMEM((tm, k), BF16)],
        compiler_params=_cparams("parallel", "arbitrary"),
        name="norm_matmul",
    )(x, g.reshape(1, k), w)


def _rope_body(pos_ref, inv_ref, cos_ref, sin_ref):
    ang = pos_ref[...] * inv_ref[...]
    lane = lax.broadcasted_iota(jnp.int32, ang.shape, 1)
    first_half = (lane % MOBA_HEAD_DIM) < (MOBA_HEAD_DIM // 2)
    cos_ref[...] = jnp.cos(ang)
    s = jnp.sin(ang)
    sin_ref[...] = jnp.where(first_half, -s, s)


def rope_tables(positions):
    t = positions.size
    half = MOBA_HEAD_DIM // 2
    inv = (1.0 / (np.float32(ROPE_THETA) ** (np.arange(0, MOBA_HEAD_DIM, 2, dtype=np.float32)
                                             / np.float32(MOBA_HEAD_DIM)))).astype(np.float32)
    inv_row = jnp.asarray(np.tile(inv, LANES // half).reshape(1, LANES))
    pos = positions.astype(F32).reshape(t, 1)
    ts = min(1024, t)
    return pl.pallas_call(
        _rope_body,
        grid=(t // ts,),
        in_specs=[pl.BlockSpec((ts, 1), lambda i: (i, 0)),
                  pl.BlockSpec((1, LANES), lambda i: (0, 0))],
        out_specs=[pl.BlockSpec((ts, LANES), lambda i: (i, 0))] * 2,
        out_shape=[jax.ShapeDtypeStruct((t, LANES), F32)] * 2,
        compiler_params=_cparams("parallel"),
        name="rope_tables",
    )(pos, inv_row)


def _rope_apply(x, cos, sin_signed):
    lane = lax.broadcasted_iota(jnp.int32, x.shape, 1)
    first_half = (lane % MOBA_HEAD_DIM) < (MOBA_HEAD_DIM // 2)
    partner = jnp.where(first_half, pltpu.roll(x, LANES - MOBA_HEAD_DIM // 2, 1),
                        pltpu.roll(x, MOBA_HEAD_DIM // 2, 1))
    return x * cos + partner * sin_signed


def _moba_body(q_ref, k_ref, v_ref, cq_ref, sq_ref, ck_ref, sk_ref, o_ref, ka_ref, va_ref, km_ref, s_ref,
               mt_ref, acc_ref, *, nb):
    blk = MOBA_BLOCK
    i = pl.program_id(2)

    @pl.when(i == 0)
    def _():
        km_ref[...] = jnp.zeros_like(km_ref)
        lane_b = lax.broadcasted_iota(jnp.int32, (blk, LANES), 1)

        def rope_blk(j, c):
            rows = pl.ds(pl.multiple_of(j * blk, blk), blk)
            kr = _rope_apply(k_ref[rows, :], ck_ref[rows, :], sk_ref[rows, :])
            ka_ref[rows, 0:LANES] = kr.astype(BF16)
            ka_ref[rows, LANES:2 * LANES] = jnp.where(lane_b == j, 1.0, 0.0).astype(BF16)
            km_ref[pl.ds(j, 1), :] = jnp.mean(kr, axis=0, keepdims=True)
            v = v_ref[rows, :].astype(F32)
            va_ref[0, rows, :] = jnp.where(lane_b < MOBA_HEAD_DIM, v,
                                           jnp.where(lane_b == MOBA_HEAD_DIM, 1.0, 0.0)).astype(BF16)
            va_ref[1, rows, :] = jnp.where(lane_b >= MOBA_HEAD_DIM, v,
                                           jnp.where(lane_b == 0, 1.0, 0.0)).astype(BF16)
            return c
        lax.fori_loop(0, nb, rope_blk, 0)

    lane = lax.broadcasted_iota(jnp.int32, (1, LANES), 1)
    head_a = lane < MOBA_HEAD_DIM
    q = _rope_apply(q_ref[...], cq_ref[...], sq_ref[...])
    scale = MOBA_HEAD_DIM ** -0.5
    km = km_ref[...]
    nbp = -(-nb // SUBLANES) * SUBLANES
    brow = lax.broadcasted_iota(jnp.int32, (nbp, blk), 0)
    own = pl.ds(pl.multiple_of(i * blk, blk), blk)
    r_idx = lax.broadcasted_iota(jnp.int32, (blk, blk), 0)
    c_idx = lax.broadcasted_iota(jnp.int32, (blk, blk), 1)
    causal = c_idx <= r_idx
    n_pairs = (i + 1) // 2
    q_augs, s_owns = [], []
    for h in range(2):
        qh = jnp.where(head_a, q, 0.0) if h == 0 else jnp.where(head_a, 0.0, q)
        gate = lax.dot_general(km, qh, (((1,), (1,)), ((), ())),
                               preferred_element_type=F32, precision=HIGHEST)[:nbp]
        gate = jnp.where(brow < i, gate, NEG_INF)
        bias_rows = []
        for j in range(nb):
            gj = gate[j:j + 1, :]
            ahead = (gate > gj) | ((gate == gj) & (brow < j))
            rank = jnp.sum(jnp.where(ahead, 1.0, 0.0), axis=0, keepdims=True)
            sel = (rank < float(MOBA_TOPK)) & (j < i)
            bias_rows.append(jnp.where(sel, 0.0, NEG_INF))
        bias_t = jnp.concatenate(bias_rows + [jnp.zeros((LANES - nb, blk), F32)], axis=0)
        q_aug = jnp.concatenate([qh * scale, bias_t.T], axis=1).astype(BF16)
        s_own = jnp.where(causal, _dot_nt(q_aug[:, :LANES], ka_ref[own, 0:LANES]), NEG_INF)
        mt_ref[h] = jnp.maximum(s_own[:, :LANES], s_own[:, LANES:])
        q_augs.append(q_aug)
        s_owns.append(s_own)

    def scores(jp, c):
        rows = pl.ds(pl.multiple_of(jp * (2 * blk), 2 * blk), 2 * blk)
        k2 = ka_ref[rows, :]
        for h in range(2):
            s2 = _dot_nt(q_augs[h], k2)
            s_ref[h, jp] = s2
            m_t = mt_ref[h]
            for col in range(0, 2 * blk, LANES):
                m_t = jnp.maximum(m_t, s2[:, col:col + LANES])
            mt_ref[h] = m_t
        return c
    lax.fori_loop(0, n_pairs, scores, 0)

    ms = []
    for h in range(2):
        m = jnp.max(mt_ref[h], axis=-1, keepdims=True)
        acc_ref[h] = _dot(jnp.exp(s_owns[h] - m).astype(BF16), va_ref[h, own, :])
        ms.append(m)

    def values(jp, c):
        rows = pl.ds(pl.multiple_of(jp * (2 * blk), 2 * blk), 2 * blk)
        for h in range(2):
            p2 = jnp.exp(s_ref[h, jp] - ms[h]).astype(BF16)
            acc_ref[h] += _dot(p2, va_ref[h, rows, :])
        return c
    lax.fori_loop(0, n_pairs, values, 0)
    acc_a, acc_b = acc_ref[0], acc_ref[1]
    out_a = acc_a / acc_a[:, MOBA_HEAD_DIM:MOBA_HEAD_DIM + 1]
    out_b = acc_b / acc_b[:, 0:1]
    o_ref[...] = jnp.where(head_a, out_a, out_b).astype(o_ref.dtype)


def moba_attention(qk, v, cos, sin, batch, seq, q_col=0, k_col=MOBA_WIDTH // LANES, v_col=0):
    nb = seq // MOBA_BLOCK
    pairs = MOBA_WIDTH // LANES
    blk = MOBA_BLOCK
    t = batch * seq
    return pl.pallas_call(
        functools.partial(_moba_body, nb=nb),
        grid=(batch, pairs, nb),
        in_specs=[pl.BlockSpec((blk, LANES), lambda b, p, i: (b * nb + i, q_col + p)),
                  pl.BlockSpec((seq, LANES), lambda b, p, i: (b, k_col + p)),
                  pl.BlockSpec((seq, LANES), lambda b, p, i: (b, v_col + p)),
                  pl.BlockSpec((blk, LANES), lambda b, p, i: (b * nb + i, 0)),
                  pl.BlockSpec((blk, LANES), lambda b, p, i: (b * nb + i, 0)),
                  pl.BlockSpec((seq, LANES), lambda b, p, i: (b, 0)),
                  pl.BlockSpec((seq, LANES), lambda b, p, i: (b, 0))],
        out_specs=pl.BlockSpec((blk, LANES), lambda b, p, i: (b * nb + i, p)),
        out_shape=jax.ShapeDtypeStruct((t, MOBA_WIDTH), BF16),
        scratch_shapes=[pltpu.VMEM((seq, 2 * LANES), BF16),
                        pltpu.VMEM((2, seq, LANES), BF16),
                        pltpu.VMEM((LANES, LANES), F32),
                        pltpu.VMEM((2, nb // 2, blk, 2 * blk), F32),
                        pltpu.VMEM((2, blk, LANES), F32),
                        pltpu.VMEM((2, blk, LANES), F32)],
        compiler_params=_cparams("parallel", "parallel", "arbitrary"),
        name="moba_attention",
    )(qk, qk, v, cos, sin, cos, sin)


def _s5_disc_body(are_ref, aim_ref, ldt_ref, bre_ref, bim_ref, lre_ref, lim_ref, bbre_ref, bbim_ref):
    a_re, a_im = are_ref[...], aim_ref[...]
    dt = jnp.exp(ldt_ref[...])
    mag = jnp.exp(a_re * dt)
    l_re = mag * jnp.cos(a_im * dt)
    l_im = mag * jnp.sin(a_im * dt)
    lre_ref[...] = l_re
    lim_ref[...] = l_im
    x, y = l_re - 1.0, l_im
    den = a_re * a_re + a_im * a_im
    c_re = (x * a_re + y * a_im) / den
    c_im = (y * a_re - x * a_im) / den
    b_re, b_im = bre_ref[...], bim_ref[...]
    bbre_ref[...] = c_re * b_re - c_im * b_im
    bbim_ref[...] = c_re * b_im + c_im * b_re


def s5_discretise(a_re, a_im, log_dt, b_re, b_im):
    g, p = a_re.shape
    n = b_re.shape[-1]
    col = lambda a: a.reshape(g * p, 1)
    ldt = jnp.broadcast_to(log_dt[:, None], (g, p))
    outs = pl.pallas_call(
        _s5_disc_body,
        out_shape=[jax.ShapeDtypeStruct((g * p, 1), F32)] * 2 + [jax.ShapeDtypeStruct((g * p, n), F32)] * 2,
        name="s5_discretise",
    )(col(a_re), col(a_im), col(ldt), b_re.reshape(g * p, n), b_im.reshape(g * p, n))
    l_re, l_im, bb_re, bb_im = outs
    return l_re.reshape(g, p), l_im.reshape(g, p), bb_re.reshape(g, p, n), bb_im.reshape(g, p, n)


def _block_diag(blocks):
    g, r, c = blocks.shape
    eye = jnp.eye(g, dtype=blocks.dtype)
    return (blocks[:, :, None, :] * eye[:, None, :, None]).reshape(g * r, g * c)


def _s5_body(u_ref, bd_ref, cd_ref, lre_ref, lim_ref, d_ref, y_ref, h_ref, hb_ref, *, steps):
    rows = SUBLANES
    ns = SSM_STATES

    @pl.when(pl.program_id(0) == 0)
    def _():
        h_ref[...] = jnp.zeros_like(h_ref)

    u = u_ref[...]
    hb_ref[...] = _dot(u.astype(BF16), bd_ref[...])
    l_re, l_im = lre_ref[...], lim_ref[...]

    def step(t, carry):
        h_re, h_im = carry
        r = pl.ds(pl.multiple_of(t * rows, rows), rows)
        n_re = l_re * h_re - l_im * h_im + hb_ref[r, 0:ns]
        n_im = l_re * h_im + l_im * h_re + hb_ref[r, ns:2 * ns]
        hb_ref[r, 0:ns] = n_re
        hb_ref[r, ns:2 * ns] = n_im
        return n_re, n_im

    h_re, h_im = lax.fori_loop(0, steps, step, (h_ref[:, 0:ns], h_ref[:, ns:2 * ns]))
    h_ref[:, 0:ns] = h_re
    h_ref[:, ns:2 * ns] = h_im
    y = _dot(hb_ref[...].astype(BF16), cd_ref[...]) + d_ref[...] * u
    y_ref[...] = jax.nn.gelu(y).astype(y_ref.dtype)


def s5_gelu(u_tm, bd, cd, l_re, l_im, d_skip, steps=64):
    rows = u_tm.shape[0]
    blk = steps * SUBLANES
    ns2 = 2 * SSM_STATES
    return pl.pallas_call(
        functools.partial(_s5_body, steps=steps),
        grid=(rows // blk,),
        in_specs=[pl.BlockSpec((blk, SSM_WIDTH), lambda c: (c, 0)),
                  pl.BlockSpec((SSM_WIDTH, ns2), lambda c: (0, 0)),
                  pl.BlockSpec((ns2, SSM_WIDTH), lambda c: (0, 0)),
                  pl.BlockSpec((1, SSM_STATES), lambda c: (0, 0)),
                  pl.BlockSpec((1, SSM_STATES), lambda c: (0, 0)),
                  pl.BlockSpec((1, SSM_WIDTH), lambda c: (0, 0))],
        out_specs=pl.BlockSpec((blk, SSM_WIDTH), lambda c: (c, 0)),
        out_shape=jax.ShapeDtypeStruct((rows, SSM_WIDTH), BF16),
        scratch_shapes=[pltpu.VMEM((SUBLANES, ns2), F32),
                        pltpu.VMEM((blk, ns2), F32)],
        compiler_params=_cparams("arbitrary"),
        name="s5_scan",
    )(u_tm, bd, cd, l_re.reshape(1, -1), l_im.reshape(1, -1), d_skip.reshape(1, -1))


GDN_GROUP = 8


def _gdn_prep_body(q_ref, k_ref, v_ref, a_ref, b_ref, wq_ref, wk_ref, wv_ref, alog_ref, dtb_ref,
                   u_ref, w_ref, qd_ref, kd_ref, qk_ref, gl_ref, qs_ref, ks_ref, vs_ref,
                   *, seq):
    L = GDN_CHUNK
    G = min(GDN_GROUP, seq // L)
    dk = GDN_HEAD_DIM
    nc = seq // L
    hd = pl.program_id(1)
    cb = 256
    halo = SUBLANES

    def conv_blk(rb, c):
        base = pl.multiple_of(rb * cb, cb)
        prev = pl.multiple_of(jnp.maximum(base - halo, 0), halo)
        for src, wref, dst, norm, mul in ((q_ref, wq_ref, qs_ref, True, dk ** -0.5),
                                          (k_ref, wk_ref, ks_ref, True, 1.0),
                                          (v_ref, wv_ref, vs_ref, False, 1.0)):
            cur = src[pl.ds(base, cb), :]
            head = jnp.where(rb > 0, src[pl.ds(prev, halo), :], 0.0)
            ext = jnp.concatenate([head, cur], axis=0)
            wts = wref[...]
            y = ext[halo:halo + cb] * wts[GDN_CONV - 1:GDN_CONV]
            for tap in range(1, GDN_CONV):
                y = y + ext[halo - tap:halo - tap + cb] * wts[GDN_CONV - 1 - tap:GDN_CONV - tap]
            y = _silu(y)
            if norm:
                y = y * lax.rsqrt(jnp.sum(y * y, axis=-1, keepdims=True) + EPS) * mul
            dst[pl.ds(base, cb), :] = y
        return c
    lax.fori_loop(0, seq // cb, conv_blk, 0)

    gl_rows = G * L
    ri = lax.broadcasted_iota(jnp.int32, (gl_rows, L), 0) % L
    ci = lax.broadcasted_iota(jnp.int32, (gl_rows, L), 1)
    tril = ci <= ri
    strict = ci < ri
    eye = ci == ri
    ri3 = lax.broadcasted_iota(jnp.int32, (G, L, L), 1)
    ci3 = lax.broadcasted_iota(jnp.int32, (G, L, L), 2)
    strict3 = ci3 < ri3
    eye_f = jnp.where(ci3 == ri3, 1.0, 0.0)
    r2 = lax.broadcasted_iota(jnp.int32, (L, L), 0)
    c2 = lax.broadcasted_iota(jnp.int32, (L, L), 1)
    upper_f = jnp.where(r2 <= c2, 1.0, 0.0)
    chunk_of_row = lax.broadcasted_iota(jnp.int32, (gl_rows, G), 0) // L
    expand = jnp.where(chunk_of_row == lax.broadcasted_iota(jnp.int32, (gl_rows, G), 1), 1.0, 0.0)
    a_coef = -jnp.exp(jnp.full((1, L), alog_ref[hd], F32))
    dt_bias = dtb_ref[hd]
    hp = functools.partial(jnp.dot, preferred_element_type=F32, precision=HIGHEST)

    def bmm(a, b):
        return jnp.einsum('gij,gjk->gik', a, b, preferred_element_type=F32)

    def bmm_nt(a, b):
        return jnp.einsum('gid,gjd->gij', a, b, preferred_element_type=F32)

    def to_col(rows_b):
        return jnp.sum(jnp.where(eye, rows_b, 0.0), axis=1, keepdims=True)

    def prep(cg, carry):
        c0 = pl.multiple_of(cg * G, G)
        rows = pl.ds(pl.multiple_of(cg * gl_rows, gl_rows), gl_rows)
        qc, kc, vc = qs_ref[rows, :], ks_ref[rows, :], vs_ref[rows, :]
        z = a_ref[pl.ds(c0, G), :] + dt_bias
        softplus = jnp.maximum(z, 0.0) + jnp.log(1.0 + jnp.exp(-jnp.abs(z)))
        g_rows = hp(a_coef * softplus, upper_f)
        beta_rows = _sigmoid(b_ref[pl.ds(c0, G), :])
        both = hp(expand, jnp.concatenate([g_rows, beta_rows], axis=1))
        g_rb = both[:, :L]
        g_col = to_col(g_rb)
        beta_col = to_col(both[:, L:])
        g_last = g_rb[:, L - 1:L]
        decay = jnp.where(tril, jnp.exp(jnp.where(tril, g_col - g_rb, 0.0)), 0.0)
        k_beta = kc * beta_col
        v_beta = vc * beta_col
        to3 = lambda a: a.reshape(G, L, a.shape[-1])
        kb16, k16 = to3(k_beta.astype(BF16)), to3(kc.astype(BF16))
        a_low = jnp.where(strict3, bmm_nt(kb16, k16) * to3(decay), 0.0)
        t_inv = eye_f - a_low
        pw = a_low
        span = 2
        while span < L:
            pw16 = pw.astype(BF16)
            pw = bmm(pw16, pw16)
            t_inv = t_inv + bmm(t_inv.astype(BF16), pw.astype(BF16))
            span *= 2
        ti16 = t_inv.astype(BF16)
        u = bmm(ti16, to3(v_beta.astype(BF16)))
        w = bmm(ti16, to3((k_beta * jnp.exp(g_col)).astype(BF16)))
        qk = bmm_nt(to3(qc.astype(BF16)), k16) * to3(decay)
        u_ref[rows, :] = u.reshape(gl_rows, dk)
        w_ref[rows, :] = w.reshape(gl_rows, dk).astype(BF16)
        qd_ref[rows, :] = (qc * jnp.exp(g_col)).astype(BF16)
        kd_ref[rows, :] = (kc * jnp.exp(g_last - g_col)).astype(BF16)
        qk_ref[pl.ds(c0, G)] = qk.astype(BF16)
        gl_ref[pl.ds(c0, G), :] = jnp.broadcast_to(jnp.exp(g_rows[:, L - 1:L]), (G, LANES))
        return carry
    lax.fori_loop(0, nc // G, prep, 0)


def _gdn_scan_body(u_ref, w_ref, qd_ref, kd_ref, qk_ref, gl_ref, z_ref, gn_ref, o_ref, st_ref, *, chunks):
    L = GDN_CHUNK
    d = GDN_HEAD_DIM

    @pl.when(pl.program_id(1) == 0)
    def _():
        st_ref[...] = jnp.zeros_like(st_ref)

    gn = gn_ref[...]

    def chunk(c, carry):
        rows = pl.ds(pl.multiple_of(c * L, L), L)
        for h in range(GDN_HEADS):
            cols = slice(h * d, (h + 1) * d)
            s = st_ref[h]
            s16 = s.astype(BF16)
            v_new = u_ref[rows, cols] - _dot(w_ref[rows, cols], s16)
            vn16 = v_new.astype(BF16)
            o = _dot(qd_ref[rows, cols], s16) + _dot(qk_ref[h, c], vn16)
            st_ref[h] = s * gl_ref[h, pl.ds(c, 1), :] + _dot_tn(kd_ref[rows, cols], vn16)
            o_ref[rows, cols] = (_rms(o, gn) * _silu(z_ref[rows, cols].astype(F32))).astype(o_ref.dtype)
        return carry
    lax.fori_loop(0, chunks, chunk, 0)


def gdn_mixer(qkv, z, a_rows, b_rows, conv_w, a_log, dt_bias, gnorm, batch, seq, ts=1024, qkv_col=0, z_col=0):
    t = batch * seq
    hds = GDN_HEADS
    nc = seq // GDN_CHUNK
    L = GDN_CHUNK
    d = GDN_HEAD_DIM
    seq_spec = lambda off: pl.BlockSpec((seq, d), lambda b, h: (b, qkv_col + off + h))
    w_spec = lambda off: pl.BlockSpec((GDN_CONV, d), lambda b, h: (0, off + h))
    row_spec = pl.BlockSpec((None, None, nc, L), lambda b, h: (b, h, 0, 0))
    smem = pl.BlockSpec(memory_space=pltpu.SMEM)
    head_out = pl.BlockSpec((seq, d), lambda b, h: (b, h))
    u, w, qd, kd, qk, gl = pl.pallas_call(
        functools.partial(_gdn_prep_body, seq=seq),
        grid=(batch, hds),
        in_specs=[seq_spec(0), seq_spec(hds), seq_spec(2 * hds), row_spec, row_spec,
                  w_spec(0), w_spec(hds), w_spec(2 * hds), smem, smem],
        out_specs=[head_out, head_out, head_out, head_out,
                   pl.BlockSpec((None, None, nc, L, L), lambda b, h: (b, h, 0, 0, 0)),
                   pl.BlockSpec((None, None, nc, LANES), lambda b, h: (b, h, 0, 0))],
        out_shape=[jax.ShapeDtypeStruct((t, GDN_WIDTH), F32),
                   jax.ShapeDtypeStruct((t, GDN_WIDTH), BF16),
                   jax.ShapeDtypeStruct((t, GDN_WIDTH), BF16),
                   jax.ShapeDtypeStruct((t, GDN_WIDTH), BF16),
                   jax.ShapeDtypeStruct((batch, hds, nc, L, L), BF16),
                   jax.ShapeDtypeStruct((batch, hds, nc, LANES), F32)],
        scratch_shapes=[pltpu.VMEM((seq, d), F32), pltpu.VMEM((seq, d), F32), pltpu.VMEM((seq, d), F32)],
        compiler_params=_cparams("parallel", "parallel"),
        name="gdn_prep",
    )(qkv, qkv, qkv, a_rows, b_rows, conv_w, conv_w, conv_w, a_log, dt_bias)

    ts = min(ts, seq)
    spb = seq // ts
    cps = ts // L
    tok = pl.BlockSpec((ts, GDN_WIDTH), lambda b, s: (b * spb + s, 0))
    return pl.pallas_call(
        functools.partial(_gdn_scan_body, chunks=cps),
        grid=(batch, spb),
        in_specs=[tok, tok, tok, tok,
                  pl.BlockSpec((None, hds, cps, L, L), lambda b, s: (b, 0, s, 0, 0)),
                  pl.BlockSpec((None, hds, cps, LANES), lambda b, s: (b, 0, s, 0)),
                  pl.BlockSpec((ts, GDN_WIDTH), lambda b, s: (b * spb + s, z_col)),
                  pl.BlockSpec((1, d), lambda b, s: (0, 0))],
        out_specs=tok,
        out_shape=jax.ShapeDtypeStruct((t, GDN_WIDTH), BF16),
        scratch_shapes=[pltpu.VMEM((hds, d, d), F32)],
        compiler_params=_cparams("parallel", "arbitrary"),
        name="gdn_scan",
    )(u, w, qd, kd, qk, gl, z, gnorm.reshape(1, d))


def _merge_body(x_ref, ya_ref, ys_ref, yc_ref, ga_ref, gb_ref, gc_ref, wa_ref, wg_ref, wc_ref, wo_ref, o_ref):
    d = x_ref.shape[1]
    y_a = _dot(ya_ref[...], wa_ref[...])
    glu = _dot(ys_ref[...], wg_ref[...])
    y_b = glu[:, :d] * _sigmoid(glu[:, d:])
    y_c = _dot(yc_ref[...], wc_ref[...])
    gate = lambda r: _sigmoid(r[...].astype(F32))
    merged = gate(ga_ref) * y_a + gate(gb_ref) * y_b + gate(gc_ref) * y_c
    o_ref[...] = x_ref[...] + _dot(merged.astype(BF16), wo_ref[...])


def merge_project(x, y_a, y_s, y_c, gates, w_a, w_glu, w_c, w_o, tm=512, gate_col=0):
    t, d = x.shape
    row = lambda n: pl.BlockSpec((tm, n), lambda i: (i, 0))
    gate = lambda k: pl.BlockSpec((tm, d), lambda i: (i, gate_col + k))
    full = lambda a: pl.BlockSpec(a.shape, lambda i: (0, 0))
    return pl.pallas_call(
        _merge_body,
        grid=(t // tm,),
        in_specs=[row(d), row(y_a.shape[1]), row(y_s.shape[1]), row(y_c.shape[1]), gate(0), gate(1), gate(2),
                  full(w_a), full(w_glu), full(w_c), full(w_o)],
        out_specs=row(d),
        out_shape=jax.ShapeDtypeStruct((t, d), F32),
        compiler_params=_cparams("parallel"),
        name="merge_project",
    )(x, y_a, y_s, y_c, gates, gates, gates, w_a, w_glu, w_c, w_o)


def _xattn_body(x_ref, g_ref, wq_ref, k_ref, v_ref, wo_ref, o_ref):
    x = x_ref[...]
    h = _rms(x, g_ref[...]).astype(BF16)
    q = _dot(h, wq_ref[...]).astype(BF16)
    outs = []
    for hd in range(XA_HEADS):
        cols = slice(hd * XA_HEAD_DIM, (hd + 1) * XA_HEAD_DIM)
        s = _dot_nt(q[:, cols], k_ref[:, cols]) * (XA_HEAD_DIM ** -0.5)
        p = jnp.exp(s - jnp.max(s, axis=-1, keepdims=True))
        p = p / jnp.sum(p, axis=-1, keepdims=True)
        outs.append(_dot(p.astype(BF16), v_ref[:, cols]))
    o = jnp.concatenate(outs, axis=1).astype(BF16)
    o_ref[...] = x + _dot(o, wo_ref[...])


def cross_attention(x, g, wq, k, v, wo, batch, seq, tm=512):
    t, d = x.shape
    m = k.shape[0] // batch
    spb = seq // tm
    full = lambda a: pl.BlockSpec(a.shape, lambda i: (0, 0))
    return pl.pallas_call(
        _xattn_body,
        grid=(t // tm,),
        in_specs=[pl.BlockSpec((tm, d), lambda i: (i, 0)),
                  pl.BlockSpec((1, d), lambda i: (0, 0)),
                  full(wq),
                  pl.BlockSpec((m, XA_WIDTH), lambda i: (i // spb, 0)),
                  pl.BlockSpec((m, XA_WIDTH), lambda i: (i // spb, 0)),
                  full(wo)],
        out_specs=pl.BlockSpec((tm, d), lambda i: (i, 0)),
        out_shape=jax.ShapeDtypeStruct((t, d), F32),
        compiler_params=_cparams("parallel"),
        name="cross_attention",
    )(x, g.reshape(1, d), wq, k, v, wo)


def _ffn_body(x_ref, g_ref, comb_ref, wg_ref, wu_ref, wd_ref, o_ref, h_ref, acc_ref):
    e, f = pl.program_id(1), pl.program_id(2)

    @pl.when((e == 0) & (f == 0))
    def _():
        h_ref[...] = _rms(x_ref[...], g_ref[...]).astype(BF16)
        acc_ref[...] = jnp.zeros_like(acc_ref)

    h = h_ref[...]
    act = (_silu(_dot(h, wg_ref[...])) * _dot(h, wu_ref[...])).astype(BF16)
    lane = lax.broadcasted_iota(jnp.int32, comb_ref.shape, 1)
    wt = jnp.sum(jnp.where(lane == e, comb_ref[...], 0.0), axis=-1, keepdims=True)
    acc_ref[...] += wt * _dot(act, wd_ref[...])

    @pl.when((e == pl.num_programs(1) - 1) & (f == pl.num_programs(2) - 1))
    def _():
        o_ref[...] = x_ref[...] + acc_ref[...]


def ffn(x, g, comb, w_gate, w_up, w_down, tm=512, tf=1408):
    t, d = x.shape
    e, _, f = w_gate.shape
    return pl.pallas_call(
        _ffn_body,
        grid=(t // tm, e, f // tf),
        in_specs=[pl.BlockSpec((tm, d), lambda i, j, k: (i, 0)),
                  pl.BlockSpec((1, d), lambda i, j, k: (0, 0)),
                  pl.BlockSpec((tm, LANES), lambda i, j, k: (i, 0)),
                  pl.BlockSpec((None, d, tf), lambda i, j, k: (j, 0, k)),
                  pl.BlockSpec((None, d, tf), lambda i, j, k: (j, 0, k)),
                  pl.BlockSpec((None, tf, d), lambda i, j, k: (j, k, 0))],
        out_specs=pl.BlockSpec((tm, d), lambda i, j, k: (i, 0)),
        out_shape=jax.ShapeDtypeStruct((t, d), F32),
        scratch_shapes=[pltpu.VMEM((tm, d), BF16), pltpu.VMEM((tm, d), F32)],
        compiler_params=_cparams("parallel", "arbitrary", "arbitrary"),
        name="ffn",
    )(x, g.reshape(1, d), comb, w_gate, w_up, w_down)


def _router_body(x_ref, g_ref, w_ref, r_ref, h_ref):
    h = _rms(x_ref[...], g_ref[...])
    h_ref[...] = h.astype(BF16)
    logits = jnp.dot(h, w_ref[...], preferred_element_type=F32, precision=HIGHEST)
    lane = lax.broadcasted_iota(jnp.int32, logits.shape, 1).astype(F32)
    logits = jnp.where(lane < N_EXPERTS, logits, -jnp.inf)
    v1 = jnp.max(logits, axis=-1, keepdims=True)
    i1 = jnp.min(jnp.where(logits == v1, lane, float(LANES)), axis=-1, keepdims=True)
    rest = jnp.where(lane == i1, -jnp.inf, logits)
    v2 = jnp.max(rest, axis=-1, keepdims=True)
    i2 = jnp.min(jnp.where(rest == v2, lane, float(LANES)), axis=-1, keepdims=True)
    e2 = jnp.exp(v2 - v1)
    w1 = 1.0 / (1.0 + e2)
    w2 = e2 / (1.0 + e2)
    r_ref[...] = (jnp.where(lane == 0.0, i1, 0.0) + jnp.where(lane == 1.0, i2, 0.0)
                  + jnp.where(lane == 2.0, w1, 0.0) + jnp.where(lane == 3.0, w2, 0.0))


def moe_router(x, g, w_router, tm=512):
    t, d = x.shape
    w = jnp.pad(w_router, ((0, 0), (0, LANES - w_router.shape[1])))
    return pl.pallas_call(
        _router_body,
        grid=(t // tm,),
        in_specs=[pl.BlockSpec((tm, d), lambda i: (i, 0)),
                  pl.BlockSpec((1, d), lambda i: (0, 0)),
                  pl.BlockSpec((d, LANES), lambda i: (0, 0))],
        out_specs=[pl.BlockSpec((tm, LANES), lambda i: (i, 0)), pl.BlockSpec((tm, d), lambda i: (i, 0))],
        out_shape=[jax.ShapeDtypeStruct((t, LANES), F32), jax.ShapeDtypeStruct((t, d), BF16)],
        compiler_params=_cparams("parallel"),
        name="moe_router",
    )(x, g.reshape(1, d), w)


MOE_TILE = 512
MOE_CHUNK = 512
MOE_RBLK = 256


def _moe_plan(route, t):
    e = N_EXPERTS
    tm, ck, rb = MOE_TILE, MOE_CHUNK, MOE_RBLK
    n_tiles = (TOP_K * t) // tm + e
    p = n_tiles * tm
    n_chunks = t // ck
    i32 = jnp.int32
    e1 = route[:, 0].astype(i32)
    e2 = route[:, 1].astype(i32)
    w1, w2 = route[:, 2], route[:, 3]
    tok = jnp.arange(t, dtype=i32)
    cnt = (jax.nn.one_hot(e1, e, dtype=i32) + jax.nn.one_hot(e2, e, dtype=i32))
    incl = jnp.cumsum(cnt, axis=0)
    excl = incl - cnt
    n_e = incl[-1]
    g_e = ((n_e + tm - 1) // tm) * tm
    off_end = jnp.cumsum(g_e)
    off = off_end - g_e
    pos_all = off[None, :] + excl
    pos1 = jnp.take_along_axis(pos_all, e1[:, None], axis=1)[:, 0]
    pos2 = jnp.take_along_axis(pos_all, e2[:, None], axis=1)[:, 0]
    src = jnp.full((p,), -1, i32).at[pos1].set(tok).at[pos2].set(tok)
    w_row = jnp.zeros((p,), F32).at[pos1].set(w1).at[pos2].set(w2)
    tile_start = jnp.arange(n_tiles, dtype=i32) * tm
    tile_expert = jnp.minimum(jnp.searchsorted(off_end, tile_start, side='right'), e - 1).astype(i32)
    valid_end = (off + n_e)[tile_expert]
    has_rows = tile_start < valid_end
    last_row = jnp.clip(jnp.minimum(tile_start + tm, valid_end) - 1, 0, p - 1)
    c_lo = jnp.where(has_rows, src[tile_start] // ck, 0)
    c_hi = jnp.where(has_rows, src[last_row] // ck, 0)
    n_it = c_hi - c_lo + 1
    it_end = jnp.cumsum(n_it)
    it_start = it_end - n_it
    wmax = n_tiles + e * n_chunks
    w_idx = jnp.arange(wmax, dtype=i32)
    g_tile = jnp.minimum(jnp.searchsorted(it_end, w_idx, side='right'), n_tiles - 1).astype(i32)
    g_valid = w_idx < it_end[-1]
    g_chunk = jnp.where(g_valid, c_lo[g_tile] + (w_idx - it_start[g_tile]), c_hi[n_tiles - 1])
    g_first = (g_valid & (w_idx == it_start[g_tile])).astype(i32)
    g_last = (g_valid & (w_idx == it_end[g_tile] - 1)).astype(i32)
    gather = (g_tile, g_chunk.astype(i32), g_first, g_last, g_valid.astype(i32))
    tt = jnp.arange(n_chunks, dtype=i32)
    lo = off[None, :] + excl[tt * ck]
    hi = off[None, :] + incl[tt * ck + ck - 1]
    nblk = jnp.where(hi > lo, (hi - 1) // rb - lo // rb + 1, 0).reshape(-1)
    blk_lo = (lo // rb).reshape(-1)
    cb_end = jnp.cumsum(nblk)
    cb_start = cb_end - nblk
    cmax = p // rb + e * n_chunks
    c_idx = jnp.arange(cmax, dtype=i32)
    pair = jnp.minimum(jnp.searchsorted(cb_end, c_idx, side='right'), nblk.shape[0] - 1).astype(i32)
    c_valid = c_idx < cb_end[-1]
    c_tile = jnp.where(c_valid, pair // e, n_chunks - 1)
    c_blk = jnp.where(c_valid, blk_lo[pair] + (c_idx - cb_start[pair]), 0)
    tile_first = jnp.searchsorted(c_tile, tt, side='left')
    tile_last = jnp.searchsorted(jnp.where(c_valid, c_tile, n_chunks), tt, side='right') - 1
    c_first = (c_valid & (c_idx == tile_first[c_tile])).astype(i32)
    c_last = (c_valid & (c_idx == tile_last[c_tile])).astype(i32)
    combine = (c_tile.astype(i32), c_blk.astype(i32), c_first, c_last, c_valid.astype(i32))
    return dict(n_tiles=n_tiles, p=p, src=src.reshape(p, 1), w_row=w_row.reshape(p, 1),
                tile_expert=tile_expert, pos1=pos1.reshape(t, 1), pos2=pos2.reshape(t, 1),
                gather=gather, combine=combine)


def _moe_gather_body(tile_ref, chunk_ref, first_ref, last_ref, valid_ref, src_ref, h_ref, o_ref, acc_ref):
    w = pl.program_id(0)

    @pl.when(first_ref[w] == 1)
    def _():
        acc_ref[...] = jnp.zeros_like(acc_ref)

    @pl.when(valid_ref[w] == 1)
    def _():
        rel = src_ref[...] - chunk_ref[w] * MOE_CHUNK
        col = lax.broadcasted_iota(jnp.int32, (MOE_TILE, MOE_CHUNK), 1)
        onehot = jnp.where(col == rel, 1.0, 0.0).astype(BF16)
        acc_ref[...] += _dot(onehot, h_ref[...])

    @pl.when(last_ref[w] == 1)
    def _():
        o_ref[...] = acc_ref[...].astype(o_ref.dtype)


def _moe_ffn_body(exp_ref, x_ref, wr_ref, wg_ref, wu_ref, wd_ref, o_ref, acc_ref):
    f = pl.program_id(1)

    @pl.when(f == 0)
    def _():
        acc_ref[...] = jnp.zeros_like(acc_ref)

    x = x_ref[...]
    act = (_silu(_dot(x, wg_ref[...])) * _dot(x, wu_ref[...])).astype(BF16)
    acc_ref[...] += _dot(act, wd_ref[...])

    @pl.when(f == pl.num_programs(1) - 1)
    def _():
        o_ref[...] = (acc_ref[...] * wr_ref[...]).astype(o_ref.dtype)


def _moe_combine_body(tile_ref, blk_ref, first_ref, last_ref, valid_ref, x_ref, p1_ref, p2_ref, y_ref, o_ref,
                      acc_ref):
    w = pl.program_id(0)

    @pl.when(first_ref[w] == 1)
    def _():
        acc_ref[...] = x_ref[...]

    @pl.when(valid_ref[w] == 1)
    def _():
        base = blk_ref[w] * MOE_RBLK
        col = lax.broadcasted_iota(jnp.int32, (MOE_CHUNK, MOE_RBLK), 1) + base
        hit = jnp.where(col == p1_ref[...], 1.0, jnp.where(col == p2_ref[...], 1.0, 0.0))
        acc_ref[...] += _dot(hit.astype(BF16), y_ref[...])

    @pl.when(last_ref[w] == 1)
    def _():
        o_ref[...] = acc_ref[...]


def moe_ffn(x, h, route, w_gate, w_up, w_down, tf=1408):
    t, d = x.shape
    f = w_gate.shape[2]
    tm, ck, rb = MOE_TILE, MOE_CHUNK, MOE_RBLK
    plan = _moe_plan(route, t)
    n_tiles, p = plan["n_tiles"], plan["p"]

    g_tile, g_chunk = plan["gather"][0], plan["gather"][1]
    xs = pl.pallas_call(
        _moe_gather_body,
        grid_spec=pltpu.PrefetchScalarGridSpec(
            num_scalar_prefetch=5,
            grid=(g_tile.shape[0],),
            in_specs=[pl.BlockSpec((tm, 1), lambda w, tl, ch, fi, la, va: (tl[w], 0)),
                      pl.BlockSpec((ck, d), lambda w, tl, ch, fi, la, va: (ch[w], 0))],
            out_specs=pl.BlockSpec((tm, d), lambda w, tl, ch, fi, la, va: (tl[w], 0)),
            scratch_shapes=[pltpu.VMEM((tm, d), F32)]),
        out_shape=jax.ShapeDtypeStruct((p, d), BF16),
        compiler_params=_cparams("arbitrary"),
        name="moe_gather",
    )(*plan["gather"], plan["src"], h)

    ys = pl.pallas_call(
        _moe_ffn_body,
        grid_spec=pltpu.PrefetchScalarGridSpec(
            num_scalar_prefetch=1,
            grid=(n_tiles, f // tf),
            in_specs=[pl.BlockSpec((tm, d), lambda i, k, ex: (i, 0)),
                      pl.BlockSpec((tm, 1), lambda i, k, ex: (i, 0)),
                      pl.BlockSpec((None, d, tf), lambda i, k, ex: (ex[i], 0, k)),
                      pl.BlockSpec((None, d, tf), lambda i, k, ex: (ex[i], 0, k)),
                      pl.BlockSpec((None, tf, d), lambda i, k, ex: (ex[i], k, 0))],
            out_specs=pl.BlockSpec((tm, d), lambda i, k, ex: (i, 0)),
            scratch_shapes=[pltpu.VMEM((tm, d), F32)]),
        out_shape=jax.ShapeDtypeStruct((p, d), BF16),
        compiler_params=_cparams("parallel", "arbitrary"),
        name="moe_ffn",
    )(plan["tile_expert"], xs, plan["w_row"], w_gate, w_up, w_down)

    c_tile = plan["combine"][0]
    return pl.pallas_call(
        _moe_combine_body,
        grid_spec=pltpu.PrefetchScalarGridSpec(
            num_scalar_prefetch=5,
            grid=(c_tile.shape[0],),
            in_specs=[pl.BlockSpec((ck, d), lambda w, tl, bk, fi, la, va: (tl[w], 0)),
                      pl.BlockSpec((ck, 1), lambda w, tl, bk, fi, la, va: (tl[w], 0)),
                      pl.BlockSpec((ck, 1), lambda w, tl, bk, fi, la, va: (tl[w], 0)),
                      pl.BlockSpec((rb, d), lambda w, tl, bk, fi, la, va: (bk[w], 0))],
            out_specs=pl.BlockSpec((ck, d), lambda w, tl, bk, fi, la, va: (tl[w], 0)),
            scratch_shapes=[pltpu.VMEM((ck, d), F32)]),
        out_shape=jax.ShapeDtypeStruct((t, d), F32),
        compiler_params=_cparams("arbitrary"),
        name="moe_combine",
    )(*plan["combine"], x, plan["pos1"], plan["pos2"], ys)


def _rmsnorm_body(x_ref, g_ref, o_ref):
    o_ref[...] = _rms(x_ref[...], g_ref[...])


def rmsnorm(x, g, tm=1024):
    t, d = x.shape
    tm = min(tm, t)
    return pl.pallas_call(
        _rmsnorm_body,
        grid=(t // tm,),
        in_specs=[pl.BlockSpec((tm, d), lambda i: (i, 0)), pl.BlockSpec((1, d), lambda i: (0, 0))],
        out_specs=pl.BlockSpec((tm, d), lambda i: (i, 0)),
        out_shape=jax.ShapeDtypeStruct((t, d), F32),
        compiler_params=_cparams("parallel"),
        name="rmsnorm",
    )(x, g.reshape(1, d))


def _in_proj_slices(w):
    sizes = (MOBA_WIDTH, MOBA_WIDTH, MOBA_WIDTH, SSM_WIDTH, GDN_WIDTH, GDN_WIDTH, GDN_WIDTH,
             GDN_HEADS, GDN_HEADS, GDN_WIDTH, w.shape[0], w.shape[0], w.shape[0])
    parts, start = [], 0
    for size in sizes:
        parts.append(w[:, start:start + size])
        start += size
    return parts


def kernel(x, mem, positions, norm_mix, w_in, ssm_a_re, ssm_a_im, ssm_log_dt, ssm_b_re, ssm_b_im, ssm_c_re, ssm_c_im, ssm_d, ssm_w_glu, gdn_conv, gdn_a_log, gdn_dt_bias, gdn_norm, w_up_moba, w_up_gdn, w_out, norm_xa, norm_mem, xa_wq, xa_wk, xa_wv, xa_wo, norm_ffn, ffn_w_gate, ffn_w_up, ffn_w_down, moe_w_router, moe_w_gate, moe_w_up, moe_w_down, norm_final):
    batch, seq, d = x.shape
    depth = w_in.shape[0]
    t = batch * seq
    assert batch == SUBLANES, "the S5 scan packs the batch onto the 8 sublanes of a vreg"
    assert seq % MOBA_BLOCK == 0 and seq % GDN_CHUNK == 0
    nc = seq // GDN_CHUNK
    bf = lambda a: a.astype(BF16)

    xf = x.reshape(t, d)
    memf = mem.reshape(-1, d)
    cos, sin = rope_tables(positions)
    ones_comb = jnp.ones((t, LANES), F32)

    for l in range(depth):
        (wq_m, wk_m, wv_m, wu_s, wq_g, wk_g, wv_g, wa_g, wb_g, wz_g, wg_a, wg_b, wg_c) = _in_proj_slices(w_in[l])
        g_mix = norm_mix[l]
        w_ab = jnp.pad(jnp.concatenate([wa_g, wb_g], axis=1), ((0, 0), (0, LANES - 2 * GDN_HEADS)))
        w32 = jnp.concatenate([wq_m, wk_m, wu_s, wq_g, wk_g, wv_g, w_ab], axis=1)
        w16 = jnp.concatenate([wv_m, wz_g, wg_a, wg_b, wg_c], axis=1)
        p32 = norm_matmul(xf, g_mix, bf(w32), F32, tm=1024, tn=640)
        p16 = norm_matmul(xf, g_mix, bf(w16), BF16, tm=1024, tn=1024)
        c_us = 2 * MOBA_WIDTH
        c_qkv = c_us + SSM_WIDTH
        c_ab = c_qkv + 3 * GDN_WIDTH
        u_s = p32[:, c_us:c_us + SSM_WIDTH]
        ab_g = p32[:, c_ab:c_ab + 2 * GDN_HEADS]

        y_a = moba_attention(p32, p16, cos, sin, batch, seq, q_col=0, k_col=MOBA_WIDTH // LANES, v_col=0)

        l_re, l_im, bb_re, bb_im = s5_discretise(ssm_a_re[l], ssm_a_im[l], ssm_log_dt[l], ssm_b_re[l], ssm_b_im[l])
        bd = jnp.concatenate([_block_diag(bb_re.transpose(0, 2, 1)), _block_diag(bb_im.transpose(0, 2, 1))], axis=1)
        cd = jnp.concatenate([_block_diag(ssm_c_re[l].transpose(0, 2, 1)),
                              _block_diag(-ssm_c_im[l].transpose(0, 2, 1))], axis=0)
        u_tm = u_s.reshape(batch, seq, SSM_WIDTH).transpose(1, 0, 2).reshape(t, SSM_WIDTH)
        y_s_tm = s5_gelu(u_tm, bf(bd), bf(cd), l_re, l_im, ssm_d[l])
        y_s = y_s_tm.reshape(seq, batch, SSM_WIDTH).transpose(1, 0, 2).reshape(t, SSM_WIDTH)

        def head_rows(cols):
            return cols.reshape(batch, nc, GDN_CHUNK, GDN_HEADS).transpose(0, 3, 1, 2)
        a_rows = head_rows(ab_g[:, :GDN_HEADS])
        b_rows = head_rows(ab_g[:, GDN_HEADS:2 * GDN_HEADS])
        y_c = gdn_mixer(p32, p16, a_rows, b_rows, gdn_conv[l], gdn_a_log[l], gdn_dt_bias[l], gdn_norm[l],
                        batch, seq, qkv_col=c_qkv // LANES, z_col=MOBA_WIDTH // GDN_WIDTH)

        xf = merge_project(xf, y_a, y_s, y_c, p16, bf(w_up_moba[l]), bf(ssm_w_glu[l]), bf(w_up_gdn[l]),
                           bf(w_out[l]), gate_col=(MOBA_WIDTH + GDN_WIDTH) // d)

        k_x = norm_matmul(memf, norm_mem[l], bf(xa_wk[l]), BF16)
        v_x = norm_matmul(memf, norm_mem[l], bf(xa_wv[l]), BF16)
        xf = cross_attention(xf, norm_xa[l], bf(xa_wq[l]), k_x, v_x, bf(xa_wo[l]), batch, seq)

        if l % 2 == 0:
            i = l // 2
            xf = ffn(xf, norm_ffn[l], ones_comb, bf(ffn_w_gate[i:i + 1]), bf(ffn_w_up[i:i + 1]),
                     bf(ffn_w_down[i:i + 1]))
        else:
            i = l // 2
            route, h_moe = moe_router(xf, norm_ffn[l], moe_w_router[i])
            xf = moe_ffn(xf, h_moe, route, bf(moe_w_gate[i]), bf(moe_w_up[i]), bf(moe_w_down[i]))

    return rmsnorm(xf, norm_final).reshape(batch, seq, d)
```

```python
import functools
import math

import numpy as np
import jax
import jax.numpy as jnp
from jax import lax
from jax.experimental import pallas as pl
from jax.experimental.pallas import tpu as pltpu

F32 = jnp.float32
BF16 = jnp.bfloat16
HIGHEST = lax.Precision.HIGHEST

EPS = 1e-6
NEG_INF = -1e30
MOBA_HEADS = 8
MOBA_HEAD_DIM = 64
MOBA_WIDTH = MOBA_HEADS * MOBA_HEAD_DIM
MOBA_BLOCK = 256
MOBA_TOPK = 3
ROPE_THETA = 10000.0
SSM_WIDTH = 512
SSM_GROUP = 16
SSM_GROUPS = SSM_WIDTH // SSM_GROUP
SSM_STATE = 64
SSM_STATES = SSM_GROUPS * SSM_STATE
GDN_HEADS = 4
GDN_HEAD_DIM = 128
GDN_WIDTH = GDN_HEADS * GDN_HEAD_DIM
GDN_CONV = 4
GDN_CHUNK = 64
XA_HEADS = 4
XA_HEAD_DIM = 128
XA_WIDTH = XA_HEADS * XA_HEAD_DIM
N_EXPERTS = 8
TOP_K = 2

LANES = 128
SUBLANES = 8
VMEM_LIMIT = 56 * 1024 * 1024


def _cparams(*sem):
    return pltpu.CompilerParams(dimension_semantics=sem, vmem_limit_bytes=VMEM_LIMIT)


def _rms(x, g):
    return x * lax.rsqrt(jnp.mean(x * x, axis=-1, keepdims=True) + EPS) * g


def _sigmoid(x):
    return 1.0 / (1.0 + jnp.exp(-x))


def _silu(x):
    return x * _sigmoid(x)


def _dot(a, b):
    return jnp.dot(a, b, preferred_element_type=F32)


def _dot_nt(a, b):
    return lax.dot_general(a, b, (((1,), (1,)), ((), ())), preferred_element_type=F32)


def _dot_tn(a, b):
    return lax.dot_general(a, b, (((0,), (0,)), ((), ())), preferred_element_type=F32)


def _norm_matmul_body(x_ref, g_ref, w_ref, o_ref, h_ref):
    @pl.when(pl.program_id(1) == 0)
    def _():
        h_ref[...] = _rms(x_ref[...], g_ref[...]).astype(BF16)

    o_ref[...] = _dot(h_ref[...], w_ref[...]).astype(o_ref.dtype)


def norm_matmul(x, g, w, out_dtype, tm=512, tn=512):
    t, k = x.shape
    n = w.shape[1]
    tm, tn = min(tm, t), min(tn, n)
    return pl.pallas_call(
        _norm_matmul_body,
        grid=(t // tm, n // tn),
        in_specs=[pl.BlockSpec((tm, k), lambda i, j: (i, 0)),
                  pl.BlockSpec((1, k), lambda i, j: (0, 0)),
                  pl.BlockSpec((k, tn), lambda i, j: (0, j))],
        out_specs=pl.BlockSpec((tm, tn), lambda i, j: (i, j)),
        out_shape=jax.ShapeDtypeStruct((t, n), out_dtype),
        scratch_shapes=[pltpu.VMEM((tm, k), BF16)],
        compiler_params=_cparams("parallel", "arbitrary"),
        name="norm_matmul",
    )(x, g.reshape(1, k), w)


def _rope_body(pos_ref, inv_ref, cos_ref, sin_ref):
    ang = pos_ref[...] * inv_ref[...]
    lane = lax.broadcasted_iota(jnp.int32, ang.shape, 1)
    first_half = (lane % MOBA_HEAD_DIM) < (MOBA_HEAD_DIM // 2)
    cos_ref[...] = jnp.cos(ang)
    s = jnp.sin(ang)
    sin_ref[...] = jnp.where(first_half, -s, s)


def rope_tables(positions):
    t = positions.size
    half = MOBA_HEAD_DIM // 2
    inv = (1.0 / (np.float32(ROPE_THETA) ** (np.arange(0, MOBA_HEAD_DIM, 2, dtype=np.float32)
                                             / np.float32(MOBA_HEAD_DIM)))).astype(np.float32)
    inv_row = jnp.asarray(np.tile(inv, LANES // half).reshape(1, LANES))
    pos = positions.astype(F32).reshape(t, 1)
    ts = min(1024, t)
    return pl.pallas_call(
        _rope_body,
        grid=(t // ts,),
        in_specs=[pl.BlockSpec((ts, 1), lambda i: (i, 0)),
                  pl.BlockSpec((1, LANES), lambda i: (0, 0))],
        out_specs=[pl.BlockSpec((ts, LANES), lambda i: (i, 0))] * 2,
        out_shape=[jax.ShapeDtypeStruct((t, LANES), F32)] * 2,
        compiler_params=_cparams("parallel"),
        name="rope_tables",
    )(pos, inv_row)


def _rope_apply(x, cos, sin_signed):
    lane = lax.broadcasted_iota(jnp.int32, x.shape, 1)
    first_half = (lane % MOBA_HEAD_DIM) < (MOBA_HEAD_DIM // 2)
    partner = jnp.where(first_half, pltpu.roll(x, LANES - MOBA_HEAD_DIM // 2, 1),
                        pltpu.roll(x, MOBA_HEAD_DIM // 2, 1))
    return x * cos + partner * sin_signed


def _moba_body(q_ref, k_ref, v_ref, cq_ref, sq_ref, ck_ref, sk_ref, o_ref, ka_ref, va_ref, km_ref, s_ref,
               mt_ref, acc_ref, *, nb):
    blk = MOBA_BLOCK
    i = pl.program_id(2)

    @pl.when(i == 0)
    def _():
        km_ref[...] = jnp.zeros_like(km_ref)
        lane_b = lax.broadcasted_iota(jnp.int32, (blk, LANES), 1)

        def rope_blk(j, c):
            rows = pl.ds(pl.multiple_of(j * blk, blk), blk)
            kr = _rope_apply(k_ref[rows, :], ck_ref[rows, :], sk_ref[rows, :])
            ka_ref[rows, 0:LANES] = kr.astype(BF16)
            ka_ref[rows, LANES:2 * LANES] = jnp.where(lane_b == j, 1.0, 0.0).astype(BF16)
            km_ref[pl.ds(j, 1), :] = jnp.mean(kr, axis=0, keepdims=True)
            v = v_ref[rows, :].astype(F32)
            va_ref[0, rows, :] = jnp.where(lane_b < MOBA_HEAD_DIM, v,
                                           jnp.where(lane_b == MOBA_HEAD_DIM, 1.0, 0.0)).astype(BF16)
            va_ref[1, rows, :] = jnp.where(lane_b >= MOBA_HEAD_DIM, v,
                                           jnp.where(lane_b == 0, 1.0, 0.0)).astype(BF16)
            return c
        lax.fori_loop(0, nb, rope_blk, 0)

    lane = lax.broadcasted_iota(jnp.int32, (1, LANES), 1)
    head_a = lane < MOBA_HEAD_DIM
    q = _rope_apply(q_ref[...], cq_ref[...], sq_ref[...])
    scale = MOBA_HEAD_DIM ** -0.5
    km = km_ref[...]
    nbp = km_ref.shape[0]
    brow = lax.broadcasted_iota(jnp.int32, (nbp, blk), 0)
    own = pl.ds(pl.multiple_of(i * blk, blk), blk)
    r_idx = lax.broadcasted_iota(jnp.int32, (blk, blk), 0)
    c_idx = lax.broadcasted_iota(jnp.int32, (blk, blk), 1)
    causal = c_idx <= r_idx
    n_pairs = (i + 1) // 2
    q_augs, s_owns = [], []
    for h in range(2):
        qh = jnp.where(head_a, q, 0.0) if h == 0 else jnp.where(head_a, 0.0, q)
        gate = lax.dot_general(km, qh, (((1,), (1,)), ((), ())),
                               preferred_element_type=F32, precision=HIGHEST)
        gate = jnp.where(brow < i, gate, NEG_INF)
        bias_rows = []
        for j in range(nb):
            gj = gate[j:j + 1, :]
            ahead = (gate > gj) | ((gate == gj) & (brow < j))
            rank = jnp.sum(jnp.where(ahead, 1.0, 0.0), axis=0, keepdims=True)
            sel = (rank < float(MOBA_TOPK)) & (j < i)
            bias_rows.append(jnp.where(sel, 0.0, NEG_INF))
        bias_t = jnp.concatenate(bias_rows + [jnp.zeros((LANES - nb, blk), F32)], axis=0)
        q_aug = jnp.concatenate([qh * scale, bias_t.T], axis=1).astype(BF16)
        s_own = jnp.where(causal, _dot_nt(q_aug[:, :LANES], ka_ref[own, 0:LANES]), NEG_INF)
        mt_ref[h] = jnp.maximum(s_own[:, :LANES], s_own[:, LANES:])
        q_augs.append(q_aug)
        s_owns.append(s_own)

    def loop2(n, body):
        def two(k, c):
            body(2 * k)
            body(2 * k + 1)
            return c
        lax.fori_loop(0, lax.shift_right_logical(n, 1), two, 0)

        @pl.when((n & 1) == 1)
        def _():
            body(n - 1)

    def scores(jp):
        rows = pl.ds(pl.multiple_of(jp * (2 * blk), 2 * blk), 2 * blk)
        k2 = ka_ref[rows, :]
        for h in range(2):
            s2 = _dot_nt(q_augs[h], k2)
            s_ref[h, jp] = s2
            m_t = mt_ref[h]
            for col in range(0, 2 * blk, LANES):
                m_t = jnp.maximum(m_t, s2[:, col:col + LANES])
            mt_ref[h] = m_t
    loop2(n_pairs, scores)

    ms = []
    for h in range(2):
        m = jnp.max(mt_ref[h], axis=-1, keepdims=True)
        acc_ref[h] = _dot(jnp.exp(s_owns[h] - m).astype(BF16), va_ref[h, own, :])
        ms.append(m)

    def values(jp):
        rows = pl.ds(pl.multiple_of(jp * (2 * blk), 2 * blk), 2 * blk)
        for h in range(2):
            p2 = jnp.exp(s_ref[h, jp] - ms[h]).astype(BF16)
            acc_ref[h] += _dot(p2, va_ref[h, rows, :])
    loop2(n_pairs, values)
    acc_a, acc_b = acc_ref[0], acc_ref[1]
    out_a = acc_a / acc_a[:, MOBA_HEAD_DIM:MOBA_HEAD_DIM + 1]
    out_b = acc_b / acc_b[:, 0:1]
    o_ref[...] = jnp.where(head_a, out_a, out_b).astype(o_ref.dtype)


def moba_attention(qk, v, cos, sin, batch, seq, q_col=0, k_col=MOBA_WIDTH // LANES, v_col=0):
    nb = seq // MOBA_BLOCK
    pairs = MOBA_WIDTH // LANES
    blk = MOBA_BLOCK
    t = batch * seq
    return pl.pallas_call(
        functools.partial(_moba_body, nb=nb),
        grid=(batch, pairs, nb),
        in_specs=[pl.BlockSpec((blk, LANES), lambda b, p, i: (b * nb + i, q_col + p)),
                  pl.BlockSpec((seq, LANES), lambda b, p, i: (b, k_col + p)),
                  pl.BlockSpec((seq, LANES), lambda b, p, i: (b, v_col + p)),
                  pl.BlockSpec((blk, LANES), lambda b, p, i: (b * nb + i, 0)),
                  pl.BlockSpec((blk, LANES), lambda b, p, i: (b * nb + i, 0)),
                  pl.BlockSpec((seq, LANES), lambda b, p, i: (b, 0)),
                  pl.BlockSpec((seq, LANES), lambda b, p, i: (b, 0))],
        out_specs=pl.BlockSpec((blk, LANES), lambda b, p, i: (b * nb + i, p)),
        out_shape=jax.ShapeDtypeStruct((t, MOBA_WIDTH), BF16),
        scratch_shapes=[pltpu.VMEM((seq, 2 * LANES), BF16),
                        pltpu.VMEM((2, seq, LANES), BF16),
                        pltpu.VMEM((-(-nb // SUBLANES) * SUBLANES, LANES), F32),
                        pltpu.VMEM((2, nb // 2, blk, 2 * blk), F32),
                        pltpu.VMEM((2, blk, LANES), F32),
                        pltpu.VMEM((2, blk, LANES), F32)],
        compiler_params=_cparams("parallel", "parallel", "arbitrary"),
        name="moba_attention",
    )(qk, qk, v, cos, sin, cos, sin)


def _s5_disc_body(are_ref, aim_ref, ldt_ref, bre_ref, bim_ref, lre_ref, lim_ref, bbre_ref, bbim_ref):
    a_re, a_im = are_ref[...], aim_ref[...]
    dt = jnp.exp(ldt_ref[...])
    mag = jnp.exp(a_re * dt)
    l_re = mag * jnp.cos(a_im * dt)
    l_im = mag * jnp.sin(a_im * dt)
    lre_ref[...] = l_re
    lim_ref[...] = l_im
    x, y = l_re - 1.0, l_im
    den = a_re * a_re + a_im * a_im
    c_re = (x * a_re + y * a_im) / den
    c_im = (y * a_re - x * a_im) / den
    b_re, b_im = bre_ref[...], bim_ref[...]
    bbre_ref[...] = c_re * b_re - c_im * b_im
    bbim_ref[...] = c_re * b_im + c_im * b_re


def s5_discretise(a_re, a_im, log_dt, b_re, b_im):
    g, p = a_re.shape
    n = b_re.shape[-1]
    col = lambda a: a.reshape(g * p, 1)
    ldt = jnp.broadcast_to(log_dt[:, None], (g, p))
    outs = pl.pallas_call(
        _s5_disc_body,
        out_shape=[jax.ShapeDtypeStruct((g * p, 1), F32)] * 2 + [jax.ShapeDtypeStruct((g * p, n), F32)] * 2,
        name="s5_discretise",
    )(col(a_re), col(a_im), col(ldt), b_re.reshape(g * p, n), b_im.reshape(g * p, n))
    l_re, l_im, bb_re, bb_im = outs
    return l_re.reshape(g, p), l_im.reshape(g, p), bb_re.reshape(g, p, n), bb_im.reshape(g, p, n)


def _block_diag(blocks):
    g, r, c = blocks.shape
    eye = jnp.eye(g, dtype=blocks.dtype)
    return (blocks[:, :, None, :] * eye[:, None, :, None]).reshape(g * r, g * c)


def _s5_body(u_ref, bd_ref, cd_ref, lre_ref, lim_ref, d_ref, y_ref, h_ref, hb_ref, *, steps):
    rows = SUBLANES
    ns = SSM_STATES

    @pl.when(pl.program_id(0) == 0)
    def _():
        h_ref[...] = jnp.zeros_like(h_ref)

    u = u_ref[...]
    hb_ref[...] = _dot(u.astype(BF16), bd_ref[...])
    l_re, l_im = lre_ref[...], lim_ref[...]

    def step(t, carry):
        h_re, h_im = carry
        r = pl.ds(pl.multiple_of(t * rows, rows), rows)
        n_re = l_re * h_re - l_im * h_im + hb_ref[r, 0:ns]
        n_im = l_re * h_im + l_im * h_re + hb_ref[r, ns:2 * ns]
        hb_ref[r, 0:ns] = n_re
        hb_ref[r, ns:2 * ns] = n_im
        return n_re, n_im

    h_re, h_im = lax.fori_loop(0, steps, step, (h_ref[:, 0:ns], h_ref[:, ns:2 * ns]))
    h_ref[:, 0:ns] = h_re
    h_ref[:, ns:2 * ns] = h_im
    y = _dot(hb_ref[...].astype(BF16), cd_ref[...]) + d_ref[...] * u
    y_ref[...] = jax.nn.gelu(y).astype(y_ref.dtype)


def s5_gelu(u_tm, bd, cd, l_re, l_im, d_skip, steps=64):
    rows = u_tm.shape[0]
    blk = steps * SUBLANES
    ns2 = 2 * SSM_STATES
    return pl.pallas_call(
        functools.partial(_s5_body, steps=steps),
        grid=(rows // blk,),
        in_specs=[pl.BlockSpec((blk, SSM_WIDTH), lambda c: (c, 0)),
                  pl.BlockSpec((SSM_WIDTH, ns2), lambda c: (0, 0)),
                  pl.BlockSpec((ns2, SSM_WIDTH), lambda c: (0, 0)),
                  pl.BlockSpec((1, SSM_STATES), lambda c: (0, 0)),
                  pl.BlockSpec((1, SSM_STATES), lambda c: (0, 0)),
                  pl.BlockSpec((1, SSM_WIDTH), lambda c: (0, 0))],
        out_specs=pl.BlockSpec((blk, SSM_WIDTH), lambda c: (c, 0)),
        out_shape=jax.ShapeDtypeStruct((rows, SSM_WIDTH), BF16),
        scratch_shapes=[pltpu.VMEM((SUBLANES, ns2), F32),
                        pltpu.VMEM((blk, ns2), F32)],
        compiler_params=_cparams("arbitrary"),
        name="s5_scan",
    )(u_tm, bd, cd, l_re.reshape(1, -1), l_im.reshape(1, -1), d_skip.reshape(1, -1))


GDN_GROUP = 16


def _gdn_prep_body(q_ref, k_ref, v_ref, a_ref, b_ref, wq_ref, wk_ref, wv_ref, alog_ref, dtb_ref,
                   u_ref, w_ref, qd_ref, kd_ref, qk_ref, gl_ref, qs_ref, ks_ref, vs_ref,
                   *, seq):
    L = GDN_CHUNK
    G = min(GDN_GROUP, seq // L)
    dk = GDN_HEAD_DIM
    nc = seq // L
    hd = pl.program_id(1)
    cb = 256
    halo = SUBLANES

    def conv_blk(rb, c):
        base = pl.multiple_of(rb * cb, cb)
        prev = pl.multiple_of(jnp.maximum(base - halo, 0), halo)
        for src, wref, dst, norm, mul in ((q_ref, wq_ref, qs_ref, True, dk ** -0.5),
                                          (k_ref, wk_ref, ks_ref, True, 1.0),
                                          (v_ref, wv_ref, vs_ref, False, 1.0)):
            cur = src[pl.ds(base, cb), :]
            head = jnp.where(rb > 0, src[pl.ds(prev, halo), :], 0.0)
            ext = jnp.concatenate([head, cur], axis=0)
            wts = wref[...]
            y = ext[halo:halo + cb] * wts[GDN_CONV - 1:GDN_CONV]
            for tap in range(1, GDN_CONV):
                y = y + ext[halo - tap:halo - tap + cb] * wts[GDN_CONV - 1 - tap:GDN_CONV - tap]
            y = _silu(y)
            if norm:
                y = y * lax.rsqrt(jnp.sum(y * y, axis=-1, keepdims=True) + EPS) * mul
            dst[pl.ds(base, cb), :] = y
        return c
    lax.fori_loop(0, seq // cb, conv_blk, 0)

    gl_rows = G * L
    ri = lax.broadcasted_iota(jnp.int32, (gl_rows, L), 0) % L
    ci = lax.broadcasted_iota(jnp.int32, (gl_rows, L), 1)
    tril = ci <= ri
    strict = ci < ri
    eye = ci == ri
    ri3 = lax.broadcasted_iota(jnp.int32, (G, L, L), 1)
    ci3 = lax.broadcasted_iota(jnp.int32, (G, L, L), 2)
    strict3 = ci3 < ri3
    eye_f = jnp.where(ci3 == ri3, 1.0, 0.0)
    r2 = lax.broadcasted_iota(jnp.int32, (L, L), 0)
    c2 = lax.broadcasted_iota(jnp.int32, (L, L), 1)
    upper_f = jnp.where(r2 <= c2, 1.0, 0.0)
    chunk_of_row = lax.broadcasted_iota(jnp.int32, (gl_rows, G), 0) // L
    expand = jnp.where(chunk_of_row == lax.broadcasted_iota(jnp.int32, (gl_rows, G), 1), 1.0, 0.0)
    a_coef = -jnp.exp(jnp.full((1, L), alog_ref[hd], F32))
    dt_bias = dtb_ref[hd]
    hp = functools.partial(jnp.dot, preferred_element_type=F32, precision=HIGHEST)

    def bmm(a, b):
        return jnp.einsum('gij,gjk->gik', a, b, preferred_element_type=F32)

    def bmm_nt(a, b):
        return jnp.einsum('gid,gjd->gij', a, b, preferred_element_type=F32)

    def to_col(rows_b):
        return jnp.sum(jnp.where(eye, rows_b, 0.0), axis=1, keepdims=True)

    def prep(cg, carry):
        c0 = pl.multiple_of(cg * G, G)
        rows = pl.ds(pl.multiple_of(cg * gl_rows, gl_rows), gl_rows)
        qc, kc, vc = qs_ref[rows, :], ks_ref[rows, :], vs_ref[rows, :]
        z = a_ref[pl.ds(c0, G), :] + dt_bias
        softplus = jnp.maximum(z, 0.0) + jnp.log(1.0 + jnp.exp(-jnp.abs(z)))
        g_rows = hp(a_coef * softplus, upper_f)
        beta_rows = _sigmoid(b_ref[pl.ds(c0, G), :])
        both = hp(expand, jnp.concatenate([g_rows, beta_rows], axis=1))
        g_rb = both[:, :L]
        g_col = to_col(g_rb)
        beta_col = to_col(both[:, L:])
        g_last = g_rb[:, L - 1:L]
        decay = jnp.where(tril, jnp.exp(jnp.where(tril, g_col - g_rb, 0.0)), 0.0)
        k_beta = kc * beta_col
        v_beta = vc * beta_col
        to3 = lambda a: a.reshape(G, L, a.shape[-1])
        kb16, k16 = to3(k_beta.astype(BF16)), to3(kc.astype(BF16))
        a_low = jnp.where(strict3, bmm_nt(kb16, k16) * to3(decay), 0.0)
        t_inv = eye_f - a_low
        pw = a_low
        span = 2
        while span < L:
            pw16 = pw.astype(BF16)
            pw = bmm(pw16, pw16)
            t_inv = t_inv + bmm(t_inv.astype(BF16), pw.astype(BF16))
            span *= 2
        ti16 = t_inv.astype(BF16)
        u = bmm(ti16, to3(v_beta.astype(BF16)))
        w = bmm(ti16, to3((k_beta * jnp.exp(g_col)).astype(BF16)))
        qk = bmm_nt(to3(qc.astype(BF16)), k16) * to3(decay)
        u_ref[rows, :] = u.reshape(gl_rows, dk)
        w_ref[rows, :] = w.reshape(gl_rows, dk).astype(BF16)
        qd_ref[rows, :] = (qc * jnp.exp(g_col)).astype(BF16)
        kd_ref[rows, :] = (kc * jnp.exp(g_last - g_col)).astype(BF16)
        qk_ref[pl.ds(c0, G)] = qk.astype(BF16)
        gl_ref[pl.ds(c0, G), :] = jnp.broadcast_to(jnp.exp(g_rows[:, L - 1:L]), (G, LANES))
        return carry
    lax.fori_loop(0, nc // G, prep, 0)


def _gdn_scan_body(u_ref, w_ref, qd_ref, kd_ref, qk_ref, gl_ref, z_ref, gn_ref, o_ref, st_ref, *, chunks):
    L = GDN_CHUNK
    d = GDN_HEAD_DIM

    @pl.when(pl.program_id(1) == 0)
    def _():
        st_ref[...] = jnp.zeros_like(st_ref)

    gn = gn_ref[...]

    def chunk(c, carry):
        rows = pl.ds(pl.multiple_of(c * L, L), L)
        for h in range(GDN_HEADS):
            cols = slice(h * d, (h + 1) * d)
            s = st_ref[h]
            s16 = s.astype(BF16)
            v_new = u_ref[rows, cols] - _dot(w_ref[rows, cols], s16)
            vn16 = v_new.astype(BF16)
            o = _dot(qd_ref[rows, cols], s16) + _dot(qk_ref[h, c], vn16)
            st_ref[h] = s * gl_ref[h, pl.ds(c, 1), :] + _dot_tn(kd_ref[rows, cols], vn16)
            o_ref[rows, cols] = (_rms(o, gn) * _silu(z_ref[rows, cols].astype(F32))).astype(o_ref.dtype)
        return carry
    lax.fori_loop(0, chunks, chunk, 0)


def gdn_mixer(qkv, z, a_rows, b_rows, conv_w, a_log, dt_bias, gnorm, batch, seq, ts=1024, qkv_col=0, z_col=0):
    t = batch * seq
    hds = GDN_HEADS
    nc = seq // GDN_CHUNK
    L = GDN_CHUNK
    d = GDN_HEAD_DIM
    seq_spec = lambda off: pl.BlockSpec((seq, d), lambda b, h: (b, qkv_col + off + h))
    w_spec = lambda off: pl.BlockSpec((GDN_CONV, d), lambda b, h: (0, off + h))
    row_spec = pl.BlockSpec((None, None, nc, L), lambda b, h: (b, h, 0, 0))
    smem = pl.BlockSpec(memory_space=pltpu.SMEM)
    head_out = pl.BlockSpec((seq, d), lambda b, h: (b, h))
    u, w, qd, kd, qk, gl = pl.pallas_call(
        functools.partial(_gdn_prep_body, seq=seq),
        grid=(batch, hds),
        in_specs=[seq_spec(0), seq_spec(hds), seq_spec(2 * hds), row_spec, row_spec,
                  w_spec(0), w_spec(hds), w_spec(2 * hds), smem, smem],
        out_specs=[head_out, head_out, head_out, head_out,
                   pl.BlockSpec((None, None, nc, L, L), lambda b, h: (b, h, 0, 0, 0)),
                   pl.BlockSpec((None, None, nc, LANES), lambda b, h: (b, h, 0, 0))],
        out_shape=[jax.ShapeDtypeStruct((t, GDN_WIDTH), F32),
                   jax.ShapeDtypeStruct((t, GDN_WIDTH), BF16),
                   jax.ShapeDtypeStruct((t, GDN_WIDTH), BF16),
                   jax.ShapeDtypeStruct((t, GDN_WIDTH), BF16),
                   jax.ShapeDtypeStruct((batch, hds, nc, L, L), BF16),
                   jax.ShapeDtypeStruct((batch, hds, nc, LANES), F32)],
        scratch_shapes=[pltpu.VMEM((seq, d), F32), pltpu.VMEM((seq, d), F32), pltpu.VMEM((seq, d), F32)],
        compiler_params=_cparams("parallel", "parallel"),
        name="gdn_prep",
    )(qkv, qkv, qkv, a_rows, b_rows, conv_w, conv_w, conv_w, a_log, dt_bias)

    ts = min(ts, seq)
    spb = seq // ts
    cps = ts // L
    tok = pl.BlockSpec((ts, GDN_WIDTH), lambda b, s: (b * spb + s, 0))
    return pl.pallas_call(
        functools.partial(_gdn_scan_body, chunks=cps),
        grid=(batch, spb),
        in_specs=[tok, tok, tok, tok,
                  pl.BlockSpec((None, hds, cps, L, L), lambda b, s: (b, 0, s, 0, 0)),
                  pl.BlockSpec((None, hds, cps, LANES), lambda b, s: (b, 0, s, 0)),
                  pl.BlockSpec((ts, GDN_WIDTH), lambda b, s: (b * spb + s, z_col)),
                  pl.BlockSpec((1, d), lambda b, s: (0, 0))],
        out_specs=tok,
        out_shape=jax.ShapeDtypeStruct((t, GDN_WIDTH), BF16),
        scratch_shapes=[pltpu.VMEM((hds, d, d), F32)],
        compiler_params=_cparams("parallel", "arbitrary"),
        name="gdn_scan",
    )(u, w, qd, kd, qk, gl, z, gnorm.reshape(1, d))


def _merge_body(x_ref, ya_ref, ys_ref, yc_ref, ga_ref, gb_ref, gc_ref, wa_ref, wg_ref, wc_ref, wo_ref, o_ref):
    d = x_ref.shape[1]
    y_a = _dot(ya_ref[...], wa_ref[...])
    glu = _dot(ys_ref[...], wg_ref[...])
    y_b = glu[:, :d] * _sigmoid(glu[:, d:])
    y_c = _dot(yc_ref[...], wc_ref[...])
    gate = lambda r: _sigmoid(r[...].astype(F32))
    merged = gate(ga_ref) * y_a + gate(gb_ref) * y_b + gate(gc_ref) * y_c
    o_ref[...] = x_ref[...] + _dot(merged.astype(BF16), wo_ref[...])


def merge_project(x, y_a, y_s, y_c, gates, w_a, w_glu, w_c, w_o, tm=512, gate_col=0):
    t, d = x.shape
    row = lambda n: pl.BlockSpec((tm, n), lambda i: (i, 0))
    gate = lambda k: pl.BlockSpec((tm, d), lambda i: (i, gate_col + k))
    full = lambda a: pl.BlockSpec(a.shape, lambda i: (0, 0))
    return pl.pallas_call(
        _merge_body,
        grid=(t // tm,),
        in_specs=[row(d), row(y_a.shape[1]), row(y_s.shape[1]), row(y_c.shape[1]), gate(0), gate(1), gate(2),
                  full(w_a), full(w_glu), full(w_c), full(w_o)],
        out_specs=row(d),
        out_shape=jax.ShapeDtypeStruct((t, d), F32),
        compiler_params=_cparams("parallel"),
        name="merge_project",
    )(x, y_a, y_s, y_c, gates, gates, gates, w_a, w_glu, w_c, w_o)


def _xattn_body(x_ref, g_ref, wq_ref, k_ref, v_ref, wo_ref, o_ref):
    x = x_ref[...]
    h = _rms(x, g_ref[...]).astype(BF16)
    q = _dot(h, wq_ref[...]).astype(BF16)
    outs = []
    for hd in range(XA_HEADS):
        cols = slice(hd * XA_HEAD_DIM, (hd + 1) * XA_HEAD_DIM)
        s = _dot_nt(q[:, cols], k_ref[:, cols]) * (XA_HEAD_DIM ** -0.5)
        p = jnp.exp(s - jnp.max(s, axis=-1, keepdims=True))
        p = p / jnp.sum(p, axis=-1, keepdims=True)
        outs.append(_dot(p.astype(BF16), v_ref[:, cols]))
    o = jnp.concatenate(outs, axis=1).astype(BF16)
    o_ref[...] = x + _dot(o, wo_ref[...])


def cross_attention(x, g, wq, k, v, wo, batch, seq, tm=512):
    t, d = x.shape
    m = k.shape[0] // batch
    spb = seq // tm
    full = lambda a: pl.BlockSpec(a.shape, lambda i: (0, 0))
    return pl.pallas_call(
        _xattn_body,
        grid=(t // tm,),
        in_specs=[pl.BlockSpec((tm, d), lambda i: (i, 0)),
                  pl.BlockSpec((1, d), lambda i: (0, 0)),
                  full(wq),
                  pl.BlockSpec((m, XA_WIDTH), lambda i: (i // spb, 0)),
                  pl.BlockSpec((m, XA_WIDTH), lambda i: (i // spb, 0)),
                  full(wo)],
        out_specs=pl.BlockSpec((tm, d), lambda i: (i, 0)),
        out_shape=jax.ShapeDtypeStruct((t, d), F32),
        compiler_params=_cparams("parallel"),
        name="cross_attention",
    )(x, g.reshape(1, d), wq, k, v, wo)


def _ffn_body(x_ref, g_ref, comb_ref, wg_ref, wu_ref, wd_ref, o_ref, h_ref, acc_ref):
    e, f = pl.program_id(1), pl.program_id(2)

    @pl.when((e == 0) & (f == 0))
    def _():
        h_ref[...] = _rms(x_ref[...], g_ref[...]).astype(BF16)
        acc_ref[...] = jnp.zeros_like(acc_ref)

    h = h_ref[...]
    act = (_silu(_dot(h, wg_ref[...])) * _dot(h, wu_ref[...])).astype(BF16)
    lane = lax.broadcasted_iota(jnp.int32, comb_ref.shape, 1)
    wt = jnp.sum(jnp.where(lane == e, comb_ref[...], 0.0), axis=-1, keepdims=True)
    acc_ref[...] += wt * _dot(act, wd_ref[...])

    @pl.when((e == pl.num_programs(1) - 1) & (f == pl.num_programs(2) - 1))
    def _():
        o_ref[...] = x_ref[...] + acc_ref[...]


def ffn(x, g, comb, w_gate, w_up, w_down, tm=512, tf=1408):
    t, d = x.shape
    e, _, f = w_gate.shape
    return pl.pallas_call(
        _ffn_body,
        grid=(t // tm, e, f // tf),
        in_specs=[pl.BlockSpec((tm, d), lambda i, j, k: (i, 0)),
                  pl.BlockSpec((1, d), lambda i, j, k: (0, 0)),
                  pl.BlockSpec((tm, LANES), lambda i, j, k: (i, 0)),
                  pl.BlockSpec((None, d, tf), lambda i, j, k: (j, 0, k)),
                  pl.BlockSpec((None, d, tf), lambda i, j, k: (j, 0, k)),
                  pl.BlockSpec((None, tf, d), lambda i, j, k: (j, k, 0))],
        out_specs=pl.BlockSpec((tm, d), lambda i, j, k: (i, 0)),
        out_shape=jax.ShapeDtypeStruct((t, d), F32),
        scratch_shapes=[pltpu.VMEM((tm, d), BF16), pltpu.VMEM((tm, d), F32)],
        compiler_params=_cparams("parallel", "arbitrary", "arbitrary"),
        name="ffn",
    )(x, g.reshape(1, d), comb, w_gate, w_up, w_down)


def _router_body(x_ref, g_ref, w_ref, r_ref, h_ref):
    h = _rms(x_ref[...], g_ref[...])
    h_ref[...] = h.astype(BF16)
    logits = jnp.dot(h, w_ref[...], preferred_element_type=F32, precision=HIGHEST)
    lane = lax.broadcasted_iota(jnp.int32, logits.shape, 1).astype(F32)
    logits = jnp.where(lane < N_EXPERTS, logits, -jnp.inf)
    v1 = jnp.max(logits, axis=-1, keepdims=True)
    i1 = jnp.min(jnp.where(logits == v1, lane, float(LANES)), axis=-1, keepdims=True)
    rest = jnp.where(lane == i1, -jnp.inf, logits)
    v2 = jnp.max(rest, axis=-1, keepdims=True)
    i2 = jnp.min(jnp.where(rest == v2, lane, float(LANES)), axis=-1, keepdims=True)
    e2 = jnp.exp(v2 - v1)
    w1 = 1.0 / (1.0 + e2)
    w2 = e2 / (1.0 + e2)
    r_ref[...] = (jnp.where(lane == 0.0, i1, 0.0) + jnp.where(lane == 1.0, i2, 0.0)
                  + jnp.where(lane == 2.0, w1, 0.0) + jnp.where(lane == 3.0, w2, 0.0))


def moe_router(x, g, w_router, tm=512):
    t, d = x.shape
    w = jnp.pad(w_router, ((0, 0), (0, LANES - w_router.shape[1])))
    return pl.pallas_call(
        _router_body,
        grid=(t // tm,),
        in_specs=[pl.BlockSpec((tm, d), lambda i: (i, 0)),
                  pl.BlockSpec((1, d), lambda i: (0, 0)),
                  pl.BlockSpec((d, LANES), lambda i: (0, 0))],
        out_specs=[pl.BlockSpec((tm, LANES), lambda i: (i, 0)), pl.BlockSpec((tm, d), lambda i: (i, 0))],
        out_shape=[jax.ShapeDtypeStruct((t, LANES), F32), jax.ShapeDtypeStruct((t, d), BF16)],
        compiler_params=_cparams("parallel"),
        name="moe_router",
    )(x, g.reshape(1, d), w)


MOE_TILE = 512
MOE_CHUNK = 512
MOE_RBLK = 256


def _count_le(sorted_vals, queries):
    return jnp.sum((sorted_vals[None, :] <= queries[:, None]).astype(jnp.int32), axis=1)


def _moe_plan(route, t):
    e = N_EXPERTS
    tm, ck, rb = MOE_TILE, MOE_CHUNK, MOE_RBLK
    n_tiles = (TOP_K * t) // tm + e
    p = n_tiles * tm
    n_chunks = t // ck
    i32 = jnp.int32
    e1 = route[:, 0].astype(i32)
    e2 = route[:, 1].astype(i32)
    oh1 = jax.nn.one_hot(e1, e, dtype=i32)
    oh2 = jax.nn.one_hot(e2, e, dtype=i32)
    cnt = oh1 + oh2
    incl = jnp.cumsum(cnt, axis=0)
    excl = incl - cnt
    n_e = incl[-1]
    g_e = ((n_e + tm - 1) // tm) * tm
    off_end = jnp.cumsum(g_e)
    off = off_end - g_e
    pos_all = off[None, :] + excl
    pos1 = jnp.sum(pos_all * oh1, axis=1)
    pos2 = jnp.sum(pos_all * oh2, axis=1)
    tile_start = jnp.arange(n_tiles, dtype=i32) * tm
    tile_expert = jnp.minimum(_count_le(off_end, tile_start), e - 1)
    sel = jax.nn.one_hot(tile_expert, e, dtype=i32)
    r0 = tile_start - jnp.sum(sel * off[None, :], axis=1)
    n_sel = jnp.sum(sel * n_e[None, :], axis=1)
    has_rows = r0 < n_sel
    r1 = jnp.minimum(r0 + tm, n_sel) - 1
    ends = incl[ck - 1::ck]
    ends_sel = jnp.sum(ends[None, :, :] * sel[:, None, :], axis=2)
    c_lo = jnp.where(has_rows, jnp.sum((ends_sel <= r0[:, None]).astype(i32), axis=1), 0)
    c_hi = jnp.where(has_rows, jnp.sum((ends_sel <= r1[:, None]).astype(i32), axis=1), 0)
    n_it = c_hi - c_lo + 1
    it_end = jnp.cumsum(n_it)
    it_start = it_end - n_it
    wmax = n_tiles + e * n_chunks
    w_idx = jnp.arange(wmax, dtype=i32)
    g_tile = jnp.minimum(_count_le(it_end, w_idx), n_tiles - 1)
    g_valid = w_idx < it_end[-1]
    g_chunk = jnp.where(g_valid, c_lo[g_tile] + (w_idx - it_start[g_tile]), c_hi[n_tiles - 1])
    g_first = (g_valid & (w_idx == it_start[g_tile])).astype(i32)
    g_last = (g_valid & (w_idx == it_end[g_tile] - 1)).astype(i32)
    gather = (g_tile, g_chunk.astype(i32), g_first, g_last, g_valid.astype(i32))
    lo = off[None, :] + excl[::ck]
    hi = off[None, :] + ends
    nblk = jnp.where(hi > lo, (hi - 1) // rb - lo // rb + 1, 0).reshape(-1)
    blk_lo = (lo // rb).reshape(-1)
    cb_end = jnp.cumsum(nblk)
    cb_start = cb_end - nblk
    cmax = p // rb + e * n_chunks
    c_idx = jnp.arange(cmax, dtype=i32)
    pair = jnp.minimum(_count_le(cb_end, c_idx), nblk.shape[0] - 1)
    c_valid = c_idx < cb_end[-1]
    c_tile = jnp.where(c_valid, pair // e, n_chunks - 1)
    c_blk = jnp.where(c_valid, blk_lo[pair] + (c_idx - cb_start[pair]), 0)
    per_tile = jnp.sum(nblk.reshape(n_chunks, e), axis=1)
    t_end = jnp.cumsum(per_tile)
    c_first = (c_valid & (c_idx == (t_end - per_tile)[c_tile])).astype(i32)
    c_last = (c_valid & (c_idx == t_end[c_tile] - 1)).astype(i32)
    combine = (c_tile.astype(i32), c_blk.astype(i32), c_first, c_last, c_valid.astype(i32))
    return dict(n_tiles=n_tiles, p=p, tile_expert=tile_expert.astype(i32), pos1=pos1, pos2=pos2,
                gather=gather, combine=combine)


def _moe_gather_body(tile_ref, chunk_ref, first_ref, last_ref, valid_ref, p1_ref, p2_ref, w1_ref, w2_ref, h_ref,
                     o_ref, wr_ref, acc_ref, wacc_ref):
    w = pl.program_id(0)

    @pl.when(first_ref[w] == 1)
    def _():
        acc_ref[...] = jnp.zeros_like(acc_ref)
        wacc_ref[...] = jnp.zeros_like(wacc_ref)

    @pl.when(valid_ref[w] == 1)
    def _():
        row = lax.broadcasted_iota(jnp.int32, (MOE_TILE, MOE_CHUNK), 0) + tile_ref[w] * MOE_TILE
        hit1 = row == p1_ref[...]
        hit2 = row == p2_ref[...]
        onehot = jnp.where(hit1, 1.0, jnp.where(hit2, 1.0, 0.0)).astype(BF16)
        acc_ref[...] += _dot(onehot, h_ref[...])
        wts = jnp.where(hit1, w1_ref[...], jnp.where(hit2, w2_ref[...], 0.0))
        wacc_ref[...] += jnp.sum(wts, axis=1, keepdims=True)

    @pl.when(last_ref[w] == 1)
    def _():
        o_ref[...] = acc_ref[...].astype(o_ref.dtype)
        wr_ref[...] = wacc_ref[...]


def _moe_ffn_body(exp_ref, x_ref, wr_ref, wg_ref, wu_ref, wd_ref, o_ref, acc_ref):
    f = pl.program_id(1)

    @pl.when(f == 0)
    def _():
        acc_ref[...] = jnp.zeros_like(acc_ref)

    x = x_ref[...]
    act = (_silu(_dot(x, wg_ref[...])) * _dot(x, wu_ref[...])).astype(BF16)
    acc_ref[...] += _dot(act, wd_ref[...])

    @pl.when(f == pl.num_programs(1) - 1)
    def _():
        o_ref[...] = (acc_ref[...] * wr_ref[...]).astype(o_ref.dtype)


def _moe_combine_body(tile_ref, blk_ref, first_ref, last_ref, valid_ref, x_ref, p1_ref, p2_ref, y_ref, gf_ref,
                      o_ref, acc_ref, *, final_norm):
    w = pl.program_id(0)

    @pl.when(first_ref[w] == 1)
    def _():
        acc_ref[...] = x_ref[...]

    @pl.when(valid_ref[w] == 1)
    def _():
        base = blk_ref[w] * MOE_RBLK
        col = lax.broadcasted_iota(jnp.int32, (MOE_CHUNK, MOE_RBLK), 1) + base
        hit = jnp.where(col == p1_ref[...], 1.0, jnp.where(col == p2_ref[...], 1.0, 0.0))
        acc_ref[...] += _dot(hit.astype(BF16), y_ref[...])

    @pl.when(last_ref[w] == 1)
    def _():
        o_ref[...] = _rms(acc_ref[...], gf_ref[...]) if final_norm else acc_ref[...]


def moe_ffn(x, h, route, w_gate, w_up, w_down, final_gain=None, tf=1408):
    t, d = x.shape
    f = w_gate.shape[2]
    tm, ck, rb = MOE_TILE, MOE_CHUNK, MOE_RBLK
    plan = _moe_plan(route, t)
    n_tiles, p = plan["n_tiles"], plan["p"]
    row = lambda a: a.reshape(1, t)
    col = lambda a: a.reshape(t, 1)
    chunk_row = pl.BlockSpec((1, ck), lambda w, tl, ch, fi, la, va: (0, ch[w]))

    xs, w_row = pl.pallas_call(
        _moe_gather_body,
        grid_spec=pltpu.PrefetchScalarGridSpec(
            num_scalar_prefetch=5,
            grid=(plan["gather"][0].shape[0],),
            in_specs=[chunk_row, chunk_row, chunk_row, chunk_row,
                      pl.BlockSpec((ck, d), lambda w, tl, ch, fi, la, va: (ch[w], 0))],
            out_specs=[pl.BlockSpec((tm, d), lambda w, tl, ch, fi, la, va: (tl[w], 0)),
                       pl.BlockSpec((tm, 1), lambda w, tl, ch, fi, la, va: (tl[w], 0))],
            scratch_shapes=[pltpu.VMEM((tm, d), F32), pltpu.VMEM((tm, 1), F32)]),
        out_shape=[jax.ShapeDtypeStruct((p, d), BF16), jax.ShapeDtypeStruct((p, 1), F32)],
        compiler_params=_cparams("arbitrary"),
        name="moe_gather",
    )(*plan["gather"], row(plan["pos1"]), row(plan["pos2"]), row(route[:, 2]), row(route[:, 3]), h)

    ys = pl.pallas_call(
        _moe_ffn_body,
        grid_spec=pltpu.PrefetchScalarGridSpec(
            num_scalar_prefetch=1,
            grid=(n_tiles, f // tf),
            in_specs=[pl.BlockSpec((tm, d), lambda i, k, ex: (i, 0)),
                      pl.BlockSpec((tm, 1), lambda i, k, ex: (i, 0)),
                      pl.BlockSpec((None, d, tf), lambda i, k, ex: (ex[i], 0, k)),
                      pl.BlockSpec((None, d, tf), lambda i, k, ex: (ex[i], 0, k)),
                      pl.BlockSpec((None, tf, d), lambda i, k, ex: (ex[i], k, 0))],
            out_specs=pl.BlockSpec((tm, d), lambda i, k, ex: (i, 0)),
            scratch_shapes=[pltpu.VMEM((tm, d), F32)]),
        out_shape=jax.ShapeDtypeStruct((p, d), BF16),
        compiler_params=_cparams("parallel", "arbitrary"),
        name="moe_ffn",
    )(plan["tile_expert"], xs, w_row, w_gate, w_up, w_down)

    final_norm = final_gain is not None
    gain = (final_gain if final_norm else jnp.ones((d,), F32)).reshape(1, d)
    tile_col = pl.BlockSpec((ck, 1), lambda w, tl, bk, fi, la, va: (tl[w], 0))
    return pl.pallas_call(
        functools.partial(_moe_combine_body, final_norm=final_norm),
        grid_spec=pltpu.PrefetchScalarGridSpec(
            num_scalar_prefetch=5,
            grid=(plan["combine"][0].shape[0],),
            in_specs=[pl.BlockSpec((ck, d), lambda w, tl, bk, fi, la, va: (tl[w], 0)),
                      tile_col, tile_col,
                      pl.BlockSpec((rb, d), lambda w, tl, bk, fi, la, va: (bk[w], 0)),
                      pl.BlockSpec((1, d), lambda w, tl, bk, fi, la, va: (0, 0))],
            out_specs=pl.BlockSpec((ck, d), lambda w, tl, bk, fi, la, va: (tl[w], 0)),
            scratch_shapes=[pltpu.VMEM((ck, d), F32)]),
        out_shape=jax.ShapeDtypeStruct((t, d), F32),
        compiler_params=_cparams("arbitrary"),
        name="moe_combine",
    )(*plan["combine"], x, col(plan["pos1"]), col(plan["pos2"]), ys, gain)


def _rmsnorm_body(x_ref, g_ref, o_ref):
    o_ref[...] = _rms(x_ref[...], g_ref[...])


def rmsnorm(x, g, tm=1024):
    t, d = x.shape
    tm = min(tm, t)
    return pl.pallas_call(
        _rmsnorm_body,
        grid=(t // tm,),
        in_specs=[pl.BlockSpec((tm, d), lambda i: (i, 0)), pl.BlockSpec((1, d), lambda i: (0, 0))],
        out_specs=pl.BlockSpec((tm, d), lambda i: (i, 0)),
        out_shape=jax.ShapeDtypeStruct((t, d), F32),
        compiler_params=_cparams("parallel"),
        name="rmsnorm",
    )(x, g.reshape(1, d))


def _in_proj_slices(w):
    sizes = (MOBA_WIDTH, MOBA_WIDTH, MOBA_WIDTH, SSM_WIDTH, GDN_WIDTH, GDN_WIDTH, GDN_WIDTH,
             GDN_HEADS, GDN_HEADS, GDN_WIDTH, w.shape[0], w.shape[0], w.shape[0])
    parts, start = [], 0
    for size in sizes:
        parts.append(w[:, start:start + size])
        start += size
    return parts


def kernel(x, mem, positions, norm_mix, w_in, ssm_a_re, ssm_a_im, ssm_log_dt, ssm_b_re, ssm_b_im, ssm_c_re, ssm_c_im, ssm_d, ssm_w_glu, gdn_conv, gdn_a_log, gdn_dt_bias, gdn_norm, w_up_moba, w_up_gdn, w_out, norm_xa, norm_mem, xa_wq, xa_wk, xa_wv, xa_wo, norm_ffn, ffn_w_gate, ffn_w_up, ffn_w_down, moe_w_router, moe_w_gate, moe_w_up, moe_w_down, norm_final):
    batch, seq, d = x.shape
    depth = w_in.shape[0]
    t = batch * seq
    assert batch == SUBLANES, "the S5 scan packs the batch onto the 8 sublanes of a vreg"
    assert seq % MOBA_BLOCK == 0 and seq % GDN_CHUNK == 0
    nc = seq // GDN_CHUNK
    bf = lambda a: a.astype(BF16)

    xf = x.reshape(t, d)
    memf = mem.reshape(-1, d)
    cos, sin = rope_tables(positions)
    ones_comb = jnp.ones((t, LANES), F32)

    for l in range(depth):
        (wq_m, wk_m, wv_m, wu_s, wq_g, wk_g, wv_g, wa_g, wb_g, wz_g, wg_a, wg_b, wg_c) = _in_proj_slices(w_in[l])
        g_mix = norm_mix[l]
        w_ab = jnp.pad(jnp.concatenate([wa_g, wb_g], axis=1), ((0, 0), (0, LANES - 2 * GDN_HEADS)))
        w32 = jnp.concatenate([wq_m, wk_m, wu_s, wq_g, wk_g, wv_g, w_ab], axis=1)
        w16 = jnp.concatenate([wv_m, wz_g, wg_a, wg_b, wg_c], axis=1)
        p32 = norm_matmul(xf, g_mix, bf(w32), F32, tm=1024, tn=640)
        p16 = norm_matmul(xf, g_mix, bf(w16), BF16, tm=1024, tn=1024)
        c_us = 2 * MOBA_WIDTH
        c_qkv = c_us + SSM_WIDTH
        c_ab = c_qkv + 3 * GDN_WIDTH
        u_s = p32[:, c_us:c_us + SSM_WIDTH]
        ab_g = p32[:, c_ab:c_ab + 2 * GDN_HEADS]

        y_a = moba_attention(p32, p16, cos, sin, batch, seq, q_col=0, k_col=MOBA_WIDTH // LANES, v_col=0)

        l_re, l_im, bb_re, bb_im = s5_discretise(ssm_a_re[l], ssm_a_im[l], ssm_log_dt[l], ssm_b_re[l], ssm_b_im[l])
        bd = jnp.concatenate([_block_diag(bb_re.transpose(0, 2, 1)), _block_diag(bb_im.transpose(0, 2, 1))], axis=1)
        cd = jnp.concatenate([_block_diag(ssm_c_re[l].transpose(0, 2, 1)),
                              _block_diag(-ssm_c_im[l].transpose(0, 2, 1))], axis=0)
        u_tm = u_s.reshape(batch, seq, SSM_WIDTH).transpose(1, 0, 2).reshape(t, SSM_WIDTH)
        y_s_tm = s5_gelu(u_tm, bf(bd), bf(cd), l_re, l_im, ssm_d[l])
        y_s = y_s_tm.reshape(seq, batch, SSM_WIDTH).transpose(1, 0, 2).reshape(t, SSM_WIDTH)

        def head_rows(cols):
            return cols.reshape(batch, nc, GDN_CHUNK, GDN_HEADS).transpose(0, 3, 1, 2)
        a_rows = head_rows(ab_g[:, :GDN_HEADS])
        b_rows = head_rows(ab_g[:, GDN_HEADS:2 * GDN_HEADS])
        y_c = gdn_mixer(p32, p16, a_rows, b_rows, gdn_conv[l], gdn_a_log[l], gdn_dt_bias[l], gdn_norm[l],
                        batch, seq, qkv_col=c_qkv // LANES, z_col=MOBA_WIDTH // GDN_WIDTH)

        xf = merge_project(xf, y_a, y_s, y_c, p16, bf(w_up_moba[l]), bf(ssm_w_glu[l]), bf(w_up_gdn[l]),
                           bf(w_out[l]), gate_col=(MOBA_WIDTH + GDN_WIDTH) // d)

        k_x = norm_matmul(memf, norm_mem[l], bf(xa_wk[l]), BF16)
        v_x = norm_matmul(memf, norm_mem[l], bf(xa_wv[l]), BF16)
        xf = cross_attention(xf, norm_xa[l], bf(xa_wq[l]), k_x, v_x, bf(xa_wo[l]), batch, seq)

        if l % 2 == 0:
            i = l // 2
            xf = ffn(xf, norm_ffn[l], ones_comb, bf(ffn_w_gate[i:i + 1]), bf(ffn_w_up[i:i + 1]),
                     bf(ffn_w_down[i:i + 1]))
        else:
            i = l // 2
            route, h_moe = moe_router(xf, norm_ffn[l], moe_w_router[i])
            xf = moe_ffn(xf, h_moe, route, bf(moe_w_gate[i]), bf(moe_w_up[i]), bf(moe_w_down[i]),
                         final_gain=norm_final if l == depth - 1 else None)

    if depth % 2 == 1:
        xf = rmsnorm(xf, norm_final)
    return xf.reshape(batch, seq, d)
```

```python
import functools
import math

import numpy as np
import jax
import jax.numpy as jnp
from jax import lax
from jax.experimental import pallas as pl
from jax.experimental.pallas import tpu as pltpu

F32 = jnp.float32
BF16 = jnp.bfloat16
HIGHEST = lax.Precision.HIGHEST

EPS = 1e-6
NEG_INF = -1e30
MOBA_HEADS = 8
MOBA_HEAD_DIM = 64
MOBA_WIDTH = MOBA_HEADS * MOBA_HEAD_DIM
MOBA_BLOCK = 256
MOBA_TOPK = 3
ROPE_THETA = 10000.0
SSM_WIDTH = 512
SSM_GROUP = 16
SSM_GROUPS = SSM_WIDTH // SSM_GROUP
SSM_STATE = 64
SSM_STATES = SSM_GROUPS * SSM_STATE
GDN_HEADS = 4
GDN_HEAD_DIM = 128
GDN_WIDTH = GDN_HEADS * GDN_HEAD_DIM
GDN_CONV = 4
GDN_CHUNK = 64
XA_HEADS = 4
XA_HEAD_DIM = 128
XA_WIDTH = XA_HEADS * XA_HEAD_DIM
N_EXPERTS = 8
TOP_K = 2

LANES = 128
SUBLANES = 8
VMEM_LIMIT = 56 * 1024 * 1024


def _cparams(*sem):
    return pltpu.CompilerParams(dimension_semantics=sem, vmem_limit_bytes=VMEM_LIMIT)


def _rms(x, g):
    return x * lax.rsqrt(jnp.mean(x * x, axis=-1, keepdims=True) + EPS) * g


def _sigmoid(x):
    return 1.0 / (1.0 + jnp.exp(-x))


def _silu(x):
    return x * _sigmoid(x)


def _dot(a, b):
    return jnp.dot(a, b, preferred_element_type=F32)


def _dot_nt(a, b):
    return lax.dot_general(a, b, (((1,), (1,)), ((), ())), preferred_element_type=F32)


def _dot_tn(a, b):
    return lax.dot_general(a, b, (((0,), (0,)), ((), ())), preferred_element_type=F32)


def _norm_matmul_body(x_ref, g_ref, w_ref, o_ref, h_ref):
    @pl.when(pl.program_id(1) == 0)
    def _():
        h_ref[...] = _rms(x_ref[...], g_ref[...]).astype(BF16)

    o_ref[...] = _dot(h_ref[...], w_ref[...]).astype(o_ref.dtype)


def norm_matmul(x, g, w, out_dtype, tm=512, tn=512):
    t, k = x.shape
    n = w.shape[1]
    tm, tn = min(tm, t), min(tn, n)
    return pl.pallas_call(
        _norm_matmul_body,
        grid=(t // tm, n // tn),
        in_specs=[pl.BlockSpec((tm, k), lambda i, j: (i, 0)),
                  pl.BlockSpec((1, k), lambda i, j: (0, 0)),
                  pl.BlockSpec((k, tn), lambda i, j: (0, j))],
        out_specs=pl.BlockSpec((tm, tn), lambda i, j: (i, j)),
        out_shape=jax.ShapeDtypeStruct((t, n), out_dtype),
        scratch_shapes=[pltpu.VMEM((tm, k), BF16)],
        compiler_params=_cparams("parallel", "arbitrary"),
        name="norm_matmul",
    )(x, g.reshape(1, k), w)


def _rope_body(pos_ref, inv_ref, cos_ref, sin_ref):
    ang = pos_ref[...] * inv_ref[...]
    lane = lax.broadcasted_iota(jnp.int32, ang.shape, 1)
    first_half = (lane % MOBA_HEAD_DIM) < (MOBA_HEAD_DIM // 2)
    cos_ref[...] = jnp.cos(ang)
    s = jnp.sin(ang)
    sin_ref[...] = jnp.where(first_half, -s, s)


def rope_tables(positions):
    t = positions.size
    half = MOBA_HEAD_DIM // 2
    inv = (1.0 / (np.float32(ROPE_THETA) ** (np.arange(0, MOBA_HEAD_DIM, 2, dtype=np.float32)
                                             / np.float32(MOBA_HEAD_DIM)))).astype(np.float32)
    inv_row = jnp.asarray(np.tile(inv, LANES // half).reshape(1, LANES))
    pos = positions.astype(F32).reshape(t, 1)
    ts = min(1024, t)
    return pl.pallas_call(
        _rope_body,
        grid=(t // ts,),
        in_specs=[pl.BlockSpec((ts, 1), lambda i: (i, 0)),
                  pl.BlockSpec((1, LANES), lambda i: (0, 0))],
        out_specs=[pl.BlockSpec((ts, LANES), lambda i: (i, 0))] * 2,
        out_shape=[jax.ShapeDtypeStruct((t, LANES), F32)] * 2,
        compiler_params=_cparams("parallel"),
        name="rope_tables",
    )(pos, inv_row)


def _rope_apply(x, cos, sin_signed):
    lane = lax.broadcasted_iota(jnp.int32, x.shape, 1)
    first_half = (lane % MOBA_HEAD_DIM) < (MOBA_HEAD_DIM // 2)
    partner = jnp.where(first_half, pltpu.roll(x, LANES - MOBA_HEAD_DIM // 2, 1),
                        pltpu.roll(x, MOBA_HEAD_DIM // 2, 1))
    return x * cos + partner * sin_signed


def _moba_body(q_ref, k_ref, v_ref, cq_ref, sq_ref, ck_ref, sk_ref, o_ref, ka_ref, va_ref, km_ref, s_ref,
               mt_ref, acc_ref, *, nb):
    blk = MOBA_BLOCK
    i = pl.program_id(2)

    @pl.when(i == 0)
    def _():
        km_ref[...] = jnp.zeros_like(km_ref)
        lane_b = lax.broadcasted_iota(jnp.int32, (blk, LANES), 1)

        def rope_blk(j, c):
            rows = pl.ds(pl.multiple_of(j * blk, blk), blk)
            kr = _rope_apply(k_ref[rows, :], ck_ref[rows, :], sk_ref[rows, :])
            ka_ref[rows, 0:LANES] = kr.astype(BF16)
            ka_ref[rows, LANES:2 * LANES] = jnp.where(lane_b == j, 1.0, 0.0).astype(BF16)
            km_ref[pl.ds(j, 1), :] = jnp.mean(kr, axis=0, keepdims=True)
            v = v_ref[rows, :].astype(F32)
            va_ref[0, rows, :] = jnp.where(lane_b < MOBA_HEAD_DIM, v,
                                           jnp.where(lane_b == MOBA_HEAD_DIM, 1.0, 0.0)).astype(BF16)
            va_ref[1, rows, :] = jnp.where(lane_b >= MOBA_HEAD_DIM, v,
                                           jnp.where(lane_b == 0, 1.0, 0.0)).astype(BF16)
            return c
        lax.fori_loop(0, nb, rope_blk, 0)

    lane = lax.broadcasted_iota(jnp.int32, (1, LANES), 1)
    head_a = lane < MOBA_HEAD_DIM
    q = _rope_apply(q_ref[...], cq_ref[...], sq_ref[...])
    scale = MOBA_HEAD_DIM ** -0.5
    km = km_ref[...]
    nbp = km_ref.shape[0]
    brow = lax.broadcasted_iota(jnp.int32, (nbp, blk), 0)
    own = pl.ds(pl.multiple_of(i * blk, blk), blk)
    r_idx = lax.broadcasted_iota(jnp.int32, (blk, blk), 0)
    c_idx = lax.broadcasted_iota(jnp.int32, (blk, blk), 1)
    causal = c_idx <= r_idx
    n_pairs = (i + 1) // 2
    q_augs, s_owns = [], []
    for h in range(2):
        qh = jnp.where(head_a, q, 0.0) if h == 0 else jnp.where(head_a, 0.0, q)
        gate = lax.dot_general(km, qh, (((1,), (1,)), ((), ())),
                               preferred_element_type=F32, precision=HIGHEST)
        gate = jnp.where(brow < i, gate, NEG_INF)
        bias_rows = []
        for j in range(nb):
            gj = gate[j:j + 1, :]
            ahead = (gate > gj) | ((gate == gj) & (brow < j))
            rank = jnp.sum(jnp.where(ahead, 1.0, 0.0), axis=0, keepdims=True)
            sel = (rank < float(MOBA_TOPK)) & (j < i)
            bias_rows.append(jnp.where(sel, 0.0, NEG_INF))
        bias_t = jnp.concatenate(bias_rows + [jnp.zeros((LANES - nb, blk), F32)], axis=0)
        q_aug = jnp.concatenate([qh * scale, bias_t.T], axis=1).astype(BF16)
        s_own = jnp.where(causal, _dot_nt(q_aug[:, :LANES], ka_ref[own, 0:LANES]), NEG_INF)
        mt_ref[h] = jnp.maximum(s_own[:, :LANES], s_own[:, LANES:])
        q_augs.append(q_aug)
        s_owns.append(s_own)

    def loop2(n, body):
        def two(k, c):
            body(2 * k)
            body(2 * k + 1)
            return c
        lax.fori_loop(0, lax.shift_right_logical(n, 1), two, 0)

        @pl.when((n & 1) == 1)
        def _():
            body(n - 1)

    def scores(jp):
        rows = pl.ds(pl.multiple_of(jp * (2 * blk), 2 * blk), 2 * blk)
        k2 = ka_ref[rows, :]
        for h in range(2):
            s2 = _dot_nt(q_augs[h], k2)
            s_ref[h, jp] = s2
            m_t = mt_ref[h]
            for col in range(0, 2 * blk, LANES):
                m_t = jnp.maximum(m_t, s2[:, col:col + LANES])
            mt_ref[h] = m_t
    loop2(n_pairs, scores)

    ms = []
    for h in range(2):
        m = jnp.max(mt_ref[h], axis=-1, keepdims=True)
        acc_ref[h] = _dot(jnp.exp(s_owns[h] - m).astype(BF16), va_ref[h, own, :])
        ms.append(m)

    def values(jp):
        rows = pl.ds(pl.multiple_of(jp * (2 * blk), 2 * blk), 2 * blk)
        for h in range(2):
            p2 = jnp.exp(s_ref[h, jp] - ms[h]).astype(BF16)
            acc_ref[h] += _dot(p2, va_ref[h, rows, :])
    loop2(n_pairs, values)
    acc_a, acc_b = acc_ref[0], acc_ref[1]
    out_a = acc_a / acc_a[:, MOBA_HEAD_DIM:MOBA_HEAD_DIM + 1]
    out_b = acc_b / acc_b[:, 0:1]
    o_ref[...] = jnp.where(head_a, out_a, out_b).astype(o_ref.dtype)


def moba_attention(qk, v, cos, sin, batch, seq, q_col=0, k_col=MOBA_WIDTH // LANES, v_col=0):
    nb = seq // MOBA_BLOCK
    pairs = MOBA_WIDTH // LANES
    blk = MOBA_BLOCK
    t = batch * seq
    return pl.pallas_call(
        functools.partial(_moba_body, nb=nb),
        grid=(batch, pairs, nb),
        in_specs=[pl.BlockSpec((blk, LANES), lambda b, p, i: (b * nb + i, q_col + p)),
                  pl.BlockSpec((seq, LANES), lambda b, p, i: (b, k_col + p)),
                  pl.BlockSpec((seq, LANES), lambda b, p, i: (b, v_col + p)),
                  pl.BlockSpec((blk, LANES), lambda b, p, i: (b * nb + i, 0)),
                  pl.BlockSpec((blk, LANES), lambda b, p, i: (b * nb + i, 0)),
                  pl.BlockSpec((seq, LANES), lambda b, p, i: (b, 0)),
                  pl.BlockSpec((seq, LANES), lambda b, p, i: (b, 0))],
        out_specs=pl.BlockSpec((blk, LANES), lambda b, p, i: (b * nb + i, p)),
        out_shape=jax.ShapeDtypeStruct((t, MOBA_WIDTH), BF16),
        scratch_shapes=[pltpu.VMEM((seq, 2 * LANES), BF16),
                        pltpu.VMEM((2, seq, LANES), BF16),
                        pltpu.VMEM((-(-nb // SUBLANES) * SUBLANES, LANES), F32),
                        pltpu.VMEM((2, nb // 2, blk, 2 * blk), F32),
                        pltpu.VMEM((2, blk, LANES), F32),
                        pltpu.VMEM((2, blk, LANES), F32)],
        compiler_params=_cparams("parallel", "parallel", "arbitrary"),
        name="moba_attention",
    )(qk, qk, v, cos, sin, cos, sin)


def _s5_disc_body(are_ref, aim_ref, ldt_ref, bre_ref, bim_ref, lre_ref, lim_ref, bbre_ref, bbim_ref):
    a_re, a_im = are_ref[...], aim_ref[...]
    dt = jnp.exp(ldt_ref[...])
    mag = jnp.exp(a_re * dt)
    l_re = mag * jnp.cos(a_im * dt)
    l_im = mag * jnp.sin(a_im * dt)
    lre_ref[...] = l_re
    lim_ref[...] = l_im
    x, y = l_re - 1.0, l_im
    den = a_re * a_re + a_im * a_im
    c_re = (x * a_re + y * a_im) / den
    c_im = (y * a_re - x * a_im) / den
    b_re, b_im = bre_ref[...], bim_ref[...]
    bbre_ref[...] = c_re * b_re - c_im * b_im
    bbim_ref[...] = c_re * b_im + c_im * b_re


def s5_discretise(a_re, a_im, log_dt, b_re, b_im):
    g, p = a_re.shape
    n = b_re.shape[-1]
    col = lambda a: a.reshape(g * p, 1)
    ldt = jnp.broadcast_to(log_dt[:, None], (g, p))
    outs = pl.pallas_call(
        _s5_disc_body,
        out_shape=[jax.ShapeDtypeStruct((g * p, 1), F32)] * 2 + [jax.ShapeDtypeStruct((g * p, n), F32)] * 2,
        name="s5_discretise",
    )(col(a_re), col(a_im), col(ldt), b_re.reshape(g * p, n), b_im.reshape(g * p, n))
    l_re, l_im, bb_re, bb_im = outs
    return l_re.reshape(g, p), l_im.reshape(g, p), bb_re.reshape(g, p, n), bb_im.reshape(g, p, n)


def _block_diag(blocks):
    g, r, c = blocks.shape
    eye = jnp.eye(g, dtype=blocks.dtype)
    return (blocks[:, :, None, :] * eye[:, None, :, None]).reshape(g * r, g * c)


def _s5_body(u_ref, bre_ref, bim_ref, cre_ref, cim_ref, lre_ref, lim_ref, d_ref, y_ref, h_ref, hb_ref, *, steps):
    rows = SUBLANES
    ns = SSM_STATES
    nq = SSM_WIDTH // LANES
    sq = ns // nq

    @pl.when(pl.program_id(0) == 0)
    def _():
        h_ref[...] = jnp.zeros_like(h_ref)

    u = u_ref[...]
    u16 = u.astype(BF16)
    for q in range(nq):
        uq = u16[:, q * LANES:(q + 1) * LANES]
        hb_ref[:, q * sq:(q + 1) * sq] = _dot(uq, bre_ref[q])
        hb_ref[:, ns + q * sq:ns + (q + 1) * sq] = _dot(uq, bim_ref[q])
    l_re, l_im = lre_ref[...], lim_ref[...]

    def step(t, carry):
        h_re, h_im = carry
        r = pl.ds(pl.multiple_of(t * rows, rows), rows)
        n_re = l_re * h_re - l_im * h_im + hb_ref[r, 0:ns]
        n_im = l_re * h_im + l_im * h_re + hb_ref[r, ns:2 * ns]
        hb_ref[r, 0:ns] = n_re
        hb_ref[r, ns:2 * ns] = n_im
        return n_re, n_im

    h_re, h_im = lax.fori_loop(0, steps, step, (h_ref[:, 0:ns], h_ref[:, ns:2 * ns]))
    h_ref[:, 0:ns] = h_re
    h_ref[:, ns:2 * ns] = h_im
    ys = []
    for q in range(nq):
        ys.append(_dot(hb_ref[:, q * sq:(q + 1) * sq].astype(BF16), cre_ref[q])
                  + _dot(hb_ref[:, ns + q * sq:ns + (q + 1) * sq].astype(BF16), cim_ref[q]))
    y = jnp.concatenate(ys, axis=1) + d_ref[...] * u
    y_ref[...] = jax.nn.gelu(y).astype(y_ref.dtype)


def s5_block_maps(bb_re, bb_im, c_re, c_im):
    nq = SSM_WIDTH // LANES
    gq = SSM_GROUPS // nq
    per_q = lambda a: jnp.stack([_block_diag(a[q * gq:(q + 1) * gq].transpose(0, 2, 1)) for q in range(nq)])
    return (per_q(bb_re).astype(BF16), per_q(bb_im).astype(BF16),
            per_q(c_re).astype(BF16), per_q(-c_im).astype(BF16))


def s5_gelu(u_tm, maps, l_re, l_im, d_skip, steps=64):
    rows = u_tm.shape[0]
    blk = steps * SUBLANES
    ns2 = 2 * SSM_STATES
    full3 = lambda a: pl.BlockSpec(a.shape, lambda c: (0, 0, 0))
    return pl.pallas_call(
        functools.partial(_s5_body, steps=steps),
        grid=(rows // blk,),
        in_specs=[pl.BlockSpec((blk, SSM_WIDTH), lambda c: (c, 0)),
                  full3(maps[0]), full3(maps[1]), full3(maps[2]), full3(maps[3]),
                  pl.BlockSpec((1, SSM_STATES), lambda c: (0, 0)),
                  pl.BlockSpec((1, SSM_STATES), lambda c: (0, 0)),
                  pl.BlockSpec((1, SSM_WIDTH), lambda c: (0, 0))],
        out_specs=pl.BlockSpec((blk, SSM_WIDTH), lambda c: (c, 0)),
        out_shape=jax.ShapeDtypeStruct((rows, SSM_WIDTH), BF16),
        scratch_shapes=[pltpu.VMEM((SUBLANES, ns2), F32),
                        pltpu.VMEM((blk, ns2), F32)],
        compiler_params=_cparams("arbitrary"),
        name="s5_scan",
    )(u_tm, *maps, l_re.reshape(1, -1), l_im.reshape(1, -1), d_skip.reshape(1, -1))


GDN_GROUP = 16


def _gdn_prep_body(q_ref, k_ref, v_ref, a_ref, b_ref, wq_ref, wk_ref, wv_ref, alog_ref, dtb_ref,
                   u_ref, w_ref, qd_ref, kd_ref, qk_ref, gl_ref, qs_ref, ks_ref, vs_ref,
                   *, seq):
    L = GDN_CHUNK
    G = min(GDN_GROUP, seq // L)
    dk = GDN_HEAD_DIM
    nc = seq // L
    hd = pl.program_id(1)
    cb = 256
    halo = SUBLANES

    def conv_blk(rb, c):
        base = pl.multiple_of(rb * cb, cb)
        prev = pl.multiple_of(jnp.maximum(base - halo, 0), halo)
        for src, wref, dst, norm, mul in ((q_ref, wq_ref, qs_ref, True, dk ** -0.5),
                                          (k_ref, wk_ref, ks_ref, True, 1.0),
                                          (v_ref, wv_ref, vs_ref, False, 1.0)):
            cur = src[pl.ds(base, cb), :]
            head = jnp.where(rb > 0, src[pl.ds(prev, halo), :], 0.0)
            ext = jnp.concatenate([head, cur], axis=0)
            wts = wref[...]
            y = ext[halo:halo + cb] * wts[GDN_CONV - 1:GDN_CONV]
            for tap in range(1, GDN_CONV):
                y = y + ext[halo - tap:halo - tap + cb] * wts[GDN_CONV - 1 - tap:GDN_CONV - tap]
            y = _silu(y)
            if norm:
                y = y * lax.rsqrt(jnp.sum(y * y, axis=-1, keepdims=True) + EPS) * mul
            dst[pl.ds(base, cb), :] = y
        return c
    lax.fori_loop(0, seq // cb, conv_blk, 0)

    gl_rows = G * L
    ri = lax.broadcasted_iota(jnp.int32, (gl_rows, L), 0) % L
    ci = lax.broadcasted_iota(jnp.int32, (gl_rows, L), 1)
    tril = ci <= ri
    strict = ci < ri
    eye = ci == ri
    ri3 = lax.broadcasted_iota(jnp.int32, (G, L, L), 1)
    ci3 = lax.broadcasted_iota(jnp.int32, (G, L, L), 2)
    strict3 = ci3 < ri3
    eye_f = jnp.where(ci3 == ri3, 1.0, 0.0)
    r2 = lax.broadcasted_iota(jnp.int32, (L, L), 0)
    c2 = lax.broadcasted_iota(jnp.int32, (L, L), 1)
    upper_f = jnp.where(r2 <= c2, 1.0, 0.0)
    chunk_of_row = lax.broadcasted_iota(jnp.int32, (gl_rows, G), 0) // L
    expand = jnp.where(chunk_of_row == lax.broadcasted_iota(jnp.int32, (gl_rows, G), 1), 1.0, 0.0)
    a_coef = -jnp.exp(jnp.full((1, L), alog_ref[hd], F32))
    dt_bias = dtb_ref[hd]
    hp = functools.partial(jnp.dot, preferred_element_type=F32, precision=HIGHEST)

    def bmm(a, b):
        return jnp.einsum('gij,gjk->gik', a, b, preferred_element_type=F32)

    def bmm_nt(a, b):
        return jnp.einsum('gid,gjd->gij', a, b, preferred_element_type=F32)

    def to_col(rows_b):
        return jnp.sum(jnp.where(eye, rows_b, 0.0), axis=1, keepdims=True)

    def prep(cg, carry):
        c0 = pl.multiple_of(cg * G, G)
        rows = pl.ds(pl.multiple_of(cg * gl_rows, gl_rows), gl_rows)
        qc, kc, vc = qs_ref[rows, :], ks_ref[rows, :], vs_ref[rows, :]
        z = a_ref[pl.ds(c0, G), :] + dt_bias
        softplus = jnp.maximum(z, 0.0) + jnp.log(1.0 + jnp.exp(-jnp.abs(z)))
        g_rows = hp(a_coef * softplus, upper_f)
        beta_rows = _sigmoid(b_ref[pl.ds(c0, G), :])
        both = hp(expand, jnp.concatenate([g_rows, beta_rows], axis=1))
        g_rb = both[:, :L]
        g_col = to_col(g_rb)
        beta_col = to_col(both[:, L:])
        g_last = g_rb[:, L - 1:L]
        decay = jnp.where(tril, jnp.exp(jnp.where(tril, g_col - g_rb, 0.0)), 0.0)
        k_beta = kc * beta_col
        v_beta = vc * beta_col
        to3 = lambda a: a.reshape(G, L, a.shape[-1])
        kb16, k16 = to3(k_beta.astype(BF16)), to3(kc.astype(BF16))
        a_low = jnp.where(strict3, bmm_nt(kb16, k16) * to3(decay), 0.0)
        t_inv = eye_f - a_low
        pw = a_low
        span = 2
        while span < L:
            pw16 = pw.astype(BF16)
            pw = bmm(pw16, pw16)
            t_inv = t_inv + bmm(t_inv.astype(BF16), pw.astype(BF16))
            span *= 2
        ti16 = t_inv.astype(BF16)
        u = bmm(ti16, to3(v_beta.astype(BF16)))
        w = bmm(ti16, to3((k_beta * jnp.exp(g_col)).astype(BF16)))
        qk = bmm_nt(to3(qc.astype(BF16)), k16) * to3(decay)
        u_ref[rows, :] = u.reshape(gl_rows, dk)
        w_ref[rows, :] = w.reshape(gl_rows, dk).astype(BF16)
        qd_ref[rows, :] = (qc * jnp.exp(g_col)).astype(BF16)
        kd_ref[rows, :] = (kc * jnp.exp(g_last - g_col)).astype(BF16)
        qk_ref[pl.ds(c0, G)] = qk.astype(BF16)
        gl_ref[pl.ds(c0, G), :] = jnp.broadcast_to(jnp.exp(g_rows[:, L - 1:L]), (G, LANES))
        return carry
    lax.fori_loop(0, nc // G, prep, 0)


def _gdn_scan_body(u_ref, w_ref, qd_ref, kd_ref, qk_ref, gl_ref, z_ref, gn_ref, o_ref, st_ref, *, chunks):
    L = GDN_CHUNK
    d = GDN_HEAD_DIM
    nrow = u_ref.shape[0]

    @pl.when(pl.program_id(1) == 0)
    def _():
        st_ref[...] = jnp.zeros_like(st_ref)

    gn = gn_ref[...]

    def chunk(c, carry):
        rows = pl.ds(pl.multiple_of(c * L, L), L)
        for r in range(nrow):
            for h in range(GDN_HEADS):
                cols = slice(h * d, (h + 1) * d)
                s = st_ref[r, h]
                s16 = s.astype(BF16)
                v_new = u_ref[r, rows, cols] - _dot(w_ref[r, rows, cols], s16)
                vn16 = v_new.astype(BF16)
                o = _dot(qd_ref[r, rows, cols], s16) + _dot(qk_ref[r, h, c], vn16)
                st_ref[r, h] = s * gl_ref[r, h, pl.ds(c, 1), :] + _dot_tn(kd_ref[r, rows, cols], vn16)
                o_ref[r, rows, cols] = (_rms(o, gn)
                                        * _silu(z_ref[r, rows, cols].astype(F32))).astype(o_ref.dtype)
        return carry
    lax.fori_loop(0, chunks, chunk, 0)


def gdn_mixer(qkv, z, a_rows, b_rows, conv_w, a_log, dt_bias, gnorm, batch, seq, ts=1024, qkv_col=0, z_col=0):
    t = batch * seq
    hds = GDN_HEADS
    nc = seq // GDN_CHUNK
    L = GDN_CHUNK
    d = GDN_HEAD_DIM
    seq_spec = lambda off: pl.BlockSpec((seq, d), lambda b, h: (b, qkv_col + off + h))
    w_spec = lambda off: pl.BlockSpec((GDN_CONV, d), lambda b, h: (0, off + h))
    row_spec = pl.BlockSpec((None, None, nc, L), lambda b, h: (b, h, 0, 0))
    smem = pl.BlockSpec(memory_space=pltpu.SMEM)
    head_out = pl.BlockSpec((seq, d), lambda b, h: (b, h))
    u, w, qd, kd, qk, gl = pl.pallas_call(
        functools.partial(_gdn_prep_body, seq=seq),
        grid=(batch, hds),
        in_specs=[seq_spec(0), seq_spec(hds), seq_spec(2 * hds), row_spec, row_spec,
                  w_spec(0), w_spec(hds), w_spec(2 * hds), smem, smem],
        out_specs=[head_out, head_out, head_out, head_out,
                   pl.BlockSpec((None, None, nc, L, L), lambda b, h: (b, h, 0, 0, 0)),
                   pl.BlockSpec((None, None, nc, LANES), lambda b, h: (b, h, 0, 0))],
        out_shape=[jax.ShapeDtypeStruct((t, GDN_WIDTH), F32),
                   jax.ShapeDtypeStruct((t, GDN_WIDTH), BF16),
                   jax.ShapeDtypeStruct((t, GDN_WIDTH), BF16),
                   jax.ShapeDtypeStruct((t, GDN_WIDTH), BF16),
                   jax.ShapeDtypeStruct((batch, hds, nc, L, L), BF16),
                   jax.ShapeDtypeStruct((batch, hds, nc, LANES), F32)],
        scratch_shapes=[pltpu.VMEM((seq, d), F32), pltpu.VMEM((seq, d), F32), pltpu.VMEM((seq, d), F32)],
        compiler_params=_cparams("parallel", "parallel"),
        name="gdn_prep",
    )(qkv, qkv, qkv, a_rows, b_rows, conv_w, conv_w, conv_w, a_log, dt_bias)

    ts = min(ts, seq)
    spb = seq // ts
    cps = ts // L
    nrow = 2 if batch % 2 == 0 else 1
    by_row = lambda a: a.reshape(batch, seq, a.shape[-1])
    tok = pl.BlockSpec((nrow, ts, GDN_WIDTH), lambda b, s: (b, s, 0))
    out = pl.pallas_call(
        functools.partial(_gdn_scan_body, chunks=cps),
        grid=(batch // nrow, spb),
        in_specs=[tok, tok, tok, tok,
                  pl.BlockSpec((nrow, hds, cps, L, L), lambda b, s: (b, 0, s, 0, 0)),
                  pl.BlockSpec((nrow, hds, cps, LANES), lambda b, s: (b, 0, s, 0)),
                  pl.BlockSpec((nrow, ts, GDN_WIDTH), lambda b, s: (b, s, z_col)),
                  pl.BlockSpec((1, d), lambda b, s: (0, 0))],
        out_specs=tok,
        out_shape=jax.ShapeDtypeStruct((batch, seq, GDN_WIDTH), BF16),
        scratch_shapes=[pltpu.VMEM((nrow, hds, d, d), F32)],
        compiler_params=_cparams("parallel", "arbitrary"),
        name="gdn_scan",
    )(by_row(u), by_row(w), by_row(qd), by_row(kd), qk, gl, by_row(z), gnorm.reshape(1, d))
    return out.reshape(t, GDN_WIDTH)


def _merge_body(x_ref, ya_ref, ys_ref, yc_ref, ga_ref, gb_ref, gc_ref, wa_ref, wg_ref, wc_ref, wo_ref, o_ref):
    d = x_ref.shape[1]
    y_a = _dot(ya_ref[...], wa_ref[...])
    glu = _dot(ys_ref[...], wg_ref[...])
    y_b = glu[:, :d] * _sigmoid(glu[:, d:])
    y_c = _dot(yc_ref[...], wc_ref[...])
    gate = lambda r: _sigmoid(r[...].astype(F32))
    merged = gate(ga_ref) * y_a + gate(gb_ref) * y_b + gate(gc_ref) * y_c
    o_ref[...] = x_ref[...] + _dot(merged.astype(BF16), wo_ref[...])


def merge_project(x, y_a, y_s, y_c, gates, w_a, w_glu, w_c, w_o, tm=512, gate_col=0):
    t, d = x.shape
    row = lambda n: pl.BlockSpec((tm, n), lambda i: (i, 0))
    gate = lambda k: pl.BlockSpec((tm, d), lambda i: (i, gate_col + k))
    full = lambda a: pl.BlockSpec(a.shape, lambda i: (0, 0))
    return pl.pallas_call(
        _merge_body,
        grid=(t // tm,),
        in_specs=[row(d), row(y_a.shape[1]), row(y_s.shape[1]), row(y_c.shape[1]), gate(0), gate(1), gate(2),
                  full(w_a), full(w_glu), full(w_c), full(w_o)],
        out_specs=row(d),
        out_shape=jax.ShapeDtypeStruct((t, d), F32),
        compiler_params=_cparams("parallel"),
        name="merge_project",
    )(x, y_a, y_s, y_c, gates, gates, gates, w_a, w_glu, w_c, w_o)


def _xattn_body(x_ref, g_ref, wq_ref, k_ref, v_ref, wo_ref, o_ref):
    x = x_ref[...]
    h = _rms(x, g_ref[...]).astype(BF16)
    q = _dot(h, wq_ref[...]).astype(BF16)
    outs = []
    for hd in range(XA_HEADS):
        cols = slice(hd * XA_HEAD_DIM, (hd + 1) * XA_HEAD_DIM)
        s = _dot_nt(q[:, cols], k_ref[:, cols]) * (XA_HEAD_DIM ** -0.5)
        p = jnp.exp(s - jnp.max(s, axis=-1, keepdims=True))
        p = p / jnp.sum(p, axis=-1, keepdims=True)
        outs.append(_dot(p.astype(BF16), v_ref[:, cols]))
    o = jnp.concatenate(outs, axis=1).astype(BF16)
    o_ref[...] = x + _dot(o, wo_ref[...])


def cross_attention(x, g, wq, k, v, wo, batch, seq, tm=512):
    t, d = x.shape
    m = k.shape[0] // batch
    spb = seq // tm
    full = lambda a: pl.BlockSpec(a.shape, lambda i: (0, 0))
    return pl.pallas_call(
        _xattn_body,
        grid=(t // tm,),
        in_specs=[pl.BlockSpec((tm, d), lambda i: (i, 0)),
                  pl.BlockSpec((1, d), lambda i: (0, 0)),
                  full(wq),
                  pl.BlockSpec((m, XA_WIDTH), lambda i: (i // spb, 0)),
                  pl.BlockSpec((m, XA_WIDTH), lambda i: (i // spb, 0)),
                  full(wo)],
        out_specs=pl.BlockSpec((tm, d), lambda i: (i, 0)),
        out_shape=jax.ShapeDtypeStruct((t, d), F32),
        compiler_params=_cparams("parallel"),
        name="cross_attention",
    )(x, g.reshape(1, d), wq, k, v, wo)


def _ffn_body(x_ref, g_ref, comb_ref, wg_ref, wu_ref, wd_ref, o_ref, h_ref, acc_ref):
    e, f = pl.program_id(1), pl.program_id(2)

    @pl.when((e == 0) & (f == 0))
    def _():
        h_ref[...] = _rms(x_ref[...], g_ref[...]).astype(BF16)
        acc_ref[...] = jnp.zeros_like(acc_ref)

    h = h_ref[...]
    act = (_silu(_dot(h, wg_ref[...])) * _dot(h, wu_ref[...])).astype(BF16)
    lane = lax.broadcasted_iota(jnp.int32, comb_ref.shape, 1)
    wt = jnp.sum(jnp.where(lane == e, comb_ref[...], 0.0), axis=-1, keepdims=True)
    acc_ref[...] += wt * _dot(act, wd_ref[...])

    @pl.when((e == pl.num_programs(1) - 1) & (f == pl.num_programs(2) - 1))
    def _():
        o_ref[...] = x_ref[...] + acc_ref[...]


def ffn(x, g, comb, w_gate, w_up, w_down, tm=512, tf=1408):
    t, d = x.shape
    e, _, f = w_gate.shape
    return pl.pallas_call(
        _ffn_body,
        grid=(t // tm, e, f // tf),
        in_specs=[pl.BlockSpec((tm, d), lambda i, j, k: (i, 0)),
                  pl.BlockSpec((1, d), lambda i, j, k: (0, 0)),
                  pl.BlockSpec((tm, LANES), lambda i, j, k: (i, 0)),
                  pl.BlockSpec((None, d, tf), lambda i, j, k: (j, 0, k)),
                  pl.BlockSpec((None, d, tf), lambda i, j, k: (j, 0, k)),
                  pl.BlockSpec((None, tf, d), lambda i, j, k: (j, k, 0))],
        out_specs=pl.BlockSpec((tm, d), lambda i, j, k: (i, 0)),
        out_shape=jax.ShapeDtypeStruct((t, d), F32),
        scratch_shapes=[pltpu.VMEM((tm, d), BF16), pltpu.VMEM((tm, d), F32)],
        compiler_params=_cparams("parallel", "arbitrary", "arbitrary"),
        name="ffn",
    )(x, g.reshape(1, d), comb, w_gate, w_up, w_down)


def _router_body(x_ref, g_ref, w_ref, r_ref, h_ref):
    h = _rms(x_ref[...], g_ref[...])
    h_ref[...] = h.astype(BF16)
    logits = jnp.dot(h, w_ref[...], preferred_element_type=F32, precision=HIGHEST)
    lane = lax.broadcasted_iota(jnp.int32, logits.shape, 1).astype(F32)
    logits = jnp.where(lane < N_EXPERTS, logits, -jnp.inf)
    v1 = jnp.max(logits, axis=-1, keepdims=True)
    i1 = jnp.min(jnp.where(logits == v1, lane, float(LANES)), axis=-1, keepdims=True)
    rest = jnp.where(lane == i1, -jnp.inf, logits)
    v2 = jnp.max(rest, axis=-1, keepdims=True)
    i2 = jnp.min(jnp.where(rest == v2, lane, float(LANES)), axis=-1, keepdims=True)
    e2 = jnp.exp(v2 - v1)
    w1 = 1.0 / (1.0 + e2)
    w2 = e2 / (1.0 + e2)
    r_ref[...] = (jnp.where(lane == 0.0, i1, 0.0) + jnp.where(lane == 1.0, i2, 0.0)
                  + jnp.where(lane == 2.0, w1, 0.0) + jnp.where(lane == 3.0, w2, 0.0))


def moe_router(x, g, w_router, tm=512):
    t, d = x.shape
    w = jnp.pad(w_router, ((0, 0), (0, LANES - w_router.shape[1])))
    return pl.pallas_call(
        _router_body,
        grid=(t // tm,),
        in_specs=[pl.BlockSpec((tm, d), lambda i: (i, 0)),
                  pl.BlockSpec((1, d), lambda i: (0, 0)),
                  pl.BlockSpec((d, LANES), lambda i: (0, 0))],
        out_specs=[pl.BlockSpec((tm, LANES), lambda i: (i, 0)), pl.BlockSpec((tm, d), lambda i: (i, 0))],
        out_shape=[jax.ShapeDtypeStruct((t, LANES), F32), jax.ShapeDtypeStruct((t, d), BF16)],
        compiler_params=_cparams("parallel"),
        name="moe_router",
    )(x, g.reshape(1, d), w)


MOE_TILE = 512
MOE_GTILE = 256
MOE_CHUNK = 512
MOE_RBLK = 256


def _count_le(sorted_vals, queries):
    return jnp.sum((sorted_vals[None, :] <= queries[:, None]).astype(jnp.int32), axis=1)


def _moe_plan(route, t):
    e = N_EXPERTS
    tm, ck, rb = MOE_TILE, MOE_CHUNK, MOE_RBLK
    n_tiles = (TOP_K * t) // tm + e
    p = n_tiles * tm
    n_chunks = t // ck
    i32 = jnp.int32
    e1 = route[:, 0].astype(i32)
    e2 = route[:, 1].astype(i32)
    oh1 = jax.nn.one_hot(e1, e, dtype=i32)
    oh2 = jax.nn.one_hot(e2, e, dtype=i32)
    cnt = oh1 + oh2
    incl = jnp.cumsum(cnt, axis=0)
    excl = incl - cnt
    n_e = incl[-1]
    g_e = ((n_e + tm - 1) // tm) * tm
    off_end = jnp.cumsum(g_e)
    off = off_end - g_e
    pos_all = off[None, :] + excl
    pos1 = jnp.sum(pos_all * oh1, axis=1)
    pos2 = jnp.sum(pos_all * oh2, axis=1)
    tile_expert = jnp.minimum(_count_le(off_end, jnp.arange(n_tiles, dtype=i32) * tm), e - 1)
    gm = MOE_GTILE
    n_gt = p // gm
    gt_start = jnp.arange(n_gt, dtype=i32) * gm
    sel = jax.nn.one_hot(jnp.minimum(_count_le(off_end, gt_start), e - 1), e, dtype=i32)
    r0 = gt_start - jnp.sum(sel * off[None, :], axis=1)
    n_sel = jnp.sum(sel * n_e[None, :], axis=1)
    has_rows = r0 < n_sel
    r1 = jnp.minimum(r0 + gm, n_sel) - 1
    ends = incl[ck - 1::ck]
    ends_sel = jnp.sum(ends[None, :, :] * sel[:, None, :], axis=2)
    c_lo = jnp.where(has_rows, jnp.sum((ends_sel <= r0[:, None]).astype(i32), axis=1), 0)
    c_hi = jnp.where(has_rows, jnp.sum((ends_sel <= r1[:, None]).astype(i32), axis=1), 0)
    n_it = c_hi - c_lo + 1
    it_end = jnp.cumsum(n_it)
    it_start = it_end - n_it
    wmax = n_gt + e * n_chunks
    w_idx = jnp.arange(wmax, dtype=i32)
    g_tile = jnp.minimum(_count_le(it_end, w_idx), n_gt - 1)
    g_valid = w_idx < it_end[-1]
    g_chunk = jnp.where(g_valid, c_lo[g_tile] + (w_idx - it_start[g_tile]), c_hi[n_gt - 1])
    g_first = (g_valid & (w_idx == it_start[g_tile])).astype(i32)
    g_last = (g_valid & (w_idx == it_end[g_tile] - 1)).astype(i32)
    gather = (g_tile, g_chunk.astype(i32), g_first, g_last, g_valid.astype(i32))
    lo = off[None, :] + excl[::ck]
    hi = off[None, :] + ends
    nblk = jnp.where(hi > lo, (hi - 1) // rb - lo // rb + 1, 0).reshape(-1)
    blk_lo = (lo // rb).reshape(-1)
    cb_end = jnp.cumsum(nblk)
    cb_start = cb_end - nblk
    cmax = p // rb + e * n_chunks
    c_idx = jnp.arange(cmax, dtype=i32)
    pair = jnp.minimum(_count_le(cb_end, c_idx), nblk.shape[0] - 1)
    c_valid = c_idx < cb_end[-1]
    c_tile = jnp.where(c_valid, pair // e, n_chunks - 1)
    c_blk = jnp.where(c_valid, blk_lo[pair] + (c_idx - cb_start[pair]), 0)
    per_tile = jnp.sum(nblk.reshape(n_chunks, e), axis=1)
    t_end = jnp.cumsum(per_tile)
    c_first = (c_valid & (c_idx == (t_end - per_tile)[c_tile])).astype(i32)
    c_last = (c_valid & (c_idx == t_end[c_tile] - 1)).astype(i32)
    combine = (c_tile.astype(i32), c_blk.astype(i32), c_first, c_last, c_valid.astype(i32))
    return dict(n_tiles=n_tiles, p=p, tile_expert=tile_expert.astype(i32), pos1=pos1, pos2=pos2,
                gather=gather, combine=combine)


def _moe_gather_body(tile_ref, chunk_ref, first_ref, last_ref, valid_ref, p1_ref, p2_ref, w1_ref, w2_ref, h_ref,
                     o_ref, wr_ref, acc_ref, wacc_ref):
    w = pl.program_id(0)

    @pl.when(first_ref[w] == 1)
    def _():
        acc_ref[...] = jnp.zeros_like(acc_ref)
        wacc_ref[...] = jnp.zeros_like(wacc_ref)

    @pl.when(valid_ref[w] == 1)
    def _():
        row = lax.broadcasted_iota(jnp.int32, (MOE_GTILE, MOE_CHUNK), 0) + tile_ref[w] * MOE_GTILE
        hit1 = row == p1_ref[...]
        hit2 = row == p2_ref[...]
        onehot = jnp.where(hit1, 1.0, jnp.where(hit2, 1.0, 0.0)).astype(BF16)
        acc_ref[...] += _dot(onehot, h_ref[...])
        wts = jnp.where(hit1, w1_ref[...], jnp.where(hit2, w2_ref[...], 0.0))
        wacc_ref[...] += jnp.sum(wts, axis=1, keepdims=True)

    @pl.when(last_ref[w] == 1)
    def _():
        o_ref[...] = acc_ref[...].astype(o_ref.dtype)
        wr_ref[...] = wacc_ref[...]


def _moe_ffn_body(exp_ref, x_ref, wr_ref, wg_ref, wu_ref, wd_ref, o_ref, acc_ref):
    f = pl.program_id(1)

    @pl.when(f == 0)
    def _():
        acc_ref[...] = jnp.zeros_like(acc_ref)

    x = x_ref[...]
    act = (_silu(_dot(x, wg_ref[...])) * _dot(x, wu_ref[...])).astype(BF16)
    acc_ref[...] += _dot(act, wd_ref[...])

    @pl.when(f == pl.num_programs(1) - 1)
    def _():
        o_ref[...] = (acc_ref[...] * wr_ref[...]).astype(o_ref.dtype)


def _moe_combine_body(tile_ref, blk_ref, first_ref, last_ref, valid_ref, x_ref, p1_ref, p2_ref, y_ref, gf_ref,
                      o_ref, acc_ref, *, final_norm):
    w = pl.program_id(0)

    @pl.when(first_ref[w] == 1)
    def _():
        acc_ref[...] = x_ref[...]

    @pl.when(valid_ref[w] == 1)
    def _():
        base = blk_ref[w] * MOE_RBLK
        col = lax.broadcasted_iota(jnp.int32, (MOE_CHUNK, MOE_RBLK), 1) + base
        hit = jnp.where(col == p1_ref[...], 1.0, jnp.where(col == p2_ref[...], 1.0, 0.0))
        acc_ref[...] += _dot(hit.astype(BF16), y_ref[...])

    @pl.when(last_ref[w] == 1)
    def _():
        o_ref[...] = _rms(acc_ref[...], gf_ref[...]) if final_norm else acc_ref[...]


def moe_ffn(x, h, route, w_gate, w_up, w_down, final_gain=None, tf=1408):
    t, d = x.shape
    f = w_gate.shape[2]
    tm, ck, rb = MOE_TILE, MOE_CHUNK, MOE_RBLK
    plan = _moe_plan(route, t)
    n_tiles, p = plan["n_tiles"], plan["p"]
    row = lambda a: a.reshape(1, t)
    col = lambda a: a.reshape(t, 1)
    chunk_row = pl.BlockSpec((1, ck), lambda w, tl, ch, fi, la, va: (0, ch[w]))

    xs, w_row = pl.pallas_call(
        _moe_gather_body,
        grid_spec=pltpu.PrefetchScalarGridSpec(
            num_scalar_prefetch=5,
            grid=(plan["gather"][0].shape[0],),
            in_specs=[chunk_row, chunk_row, chunk_row, chunk_row,
                      pl.BlockSpec((ck, d), lambda w, tl, ch, fi, la, va: (ch[w], 0))],
            out_specs=[pl.BlockSpec((MOE_GTILE, d), lambda w, tl, ch, fi, la, va: (tl[w], 0)),
                       pl.BlockSpec((MOE_GTILE, 1), lambda w, tl, ch, fi, la, va: (tl[w], 0))],
            scratch_shapes=[pltpu.VMEM((MOE_GTILE, d), F32), pltpu.VMEM((MOE_GTILE, 1), F32)]),
        out_shape=[jax.ShapeDtypeStruct((p, d), BF16), jax.ShapeDtypeStruct((p, 1), F32)],
        compiler_params=_cparams("arbitrary"),
        name="moe_gather",
    )(*plan["gather"], row(plan["pos1"]), row(plan["pos2"]), row(route[:, 2]), row(route[:, 3]), h)

    ys = pl.pallas_call(
        _moe_ffn_body,
        grid_spec=pltpu.PrefetchScalarGridSpec(
            num_scalar_prefetch=1,
            grid=(n_tiles, f // tf),
            in_specs=[pl.BlockSpec((tm, d), lambda i, k, ex: (i, 0)),
                      pl.BlockSpec((tm, 1), lambda i, k, ex: (i, 0)),
                      pl.BlockSpec((None, d, tf), lambda i, k, ex: (ex[i], 0, k)),
                      pl.BlockSpec((None, d, tf), lambda i, k, ex: (ex[i], 0, k)),
                      pl.BlockSpec((None, tf, d), lambda i, k, ex: (ex[i], k, 0))],
            out_specs=pl.BlockSpec((tm, d), lambda i, k, ex: (i, 0)),
            scratch_shapes=[pltpu.VMEM((tm, d), F32)]),
        out_shape=jax.ShapeDtypeStruct((p, d), BF16),
        compiler_params=_cparams("parallel", "arbitrary"),
        name="moe_ffn",
    )(plan["tile_expert"], xs, w_row, w_gate, w_up, w_down)

    final_norm = final_gain is not None
    gain = (final_gain if final_norm else jnp.ones((d,), F32)).reshape(1, d)
    tile_col = pl.BlockSpec((ck, 1), lambda w, tl, bk, fi, la, va: (tl[w], 0))
    return pl.pallas_call(
        functools.partial(_moe_combine_body, final_norm=final_norm),
        grid_spec=pltpu.PrefetchScalarGridSpec(
            num_scalar_prefetch=5,
            grid=(plan["combine"][0].shape[0],),
            in_specs=[pl.BlockSpec((ck, d), lambda w, tl, bk, fi, la, va: (tl[w], 0)),
                      tile_col, tile_col,
                      pl.BlockSpec((rb, d), lambda w, tl, bk, fi, la, va: (bk[w], 0)),
                      pl.BlockSpec((1, d), lambda w, tl, bk, fi, la, va: (0, 0))],
            out_specs=pl.BlockSpec((ck, d), lambda w, tl, bk, fi, la, va: (tl[w], 0)),
            scratch_shapes=[pltpu.VMEM((ck, d), F32)]),
        out_shape=jax.ShapeDtypeStruct((t, d), F32),
        compiler_params=_cparams("arbitrary"),
        name="moe_combine",
    )(*plan["combine"], x, col(plan["pos1"]), col(plan["pos2"]), ys, gain)


def _rmsnorm_body(x_ref, g_ref, o_ref):
    o_ref[...] = _rms(x_ref[...], g_ref[...])


def rmsnorm(x, g, tm=1024):
    t, d = x.shape
    tm = min(tm, t)
    return pl.pallas_call(
        _rmsnorm_body,
        grid=(t // tm,),
        in_specs=[pl.BlockSpec((tm, d), lambda i: (i, 0)), pl.BlockSpec((1, d), lambda i: (0, 0))],
        out_specs=pl.BlockSpec((tm, d), lambda i: (i, 0)),
        out_shape=jax.ShapeDtypeStruct((t, d), F32),
        compiler_params=_cparams("parallel"),
        name="rmsnorm",
    )(x, g.reshape(1, d))


def _in_proj_slices(w):
    sizes = (MOBA_WIDTH, MOBA_WIDTH, MOBA_WIDTH, SSM_WIDTH, GDN_WIDTH, GDN_WIDTH, GDN_WIDTH,
             GDN_HEADS, GDN_HEADS, GDN_WIDTH, w.shape[0], w.shape[0], w.shape[0])
    parts, start = [], 0
    for size in sizes:
        parts.append(w[:, start:start + size])
        start += size
    return parts


def kernel(x, mem, positions, norm_mix, w_in, ssm_a_re, ssm_a_im, ssm_log_dt, ssm_b_re, ssm_b_im, ssm_c_re, ssm_c_im, ssm_d, ssm_w_glu, gdn_conv, gdn_a_log, gdn_dt_bias, gdn_norm, w_up_moba, w_up_gdn, w_out, norm_xa, norm_mem, xa_wq, xa_wk, xa_wv, xa_wo, norm_ffn, ffn_w_gate, ffn_w_up, ffn_w_down, moe_w_router, moe_w_gate, moe_w_up, moe_w_down, norm_final):
    batch, seq, d = x.shape
    depth = w_in.shape[0]
    t = batch * seq
    assert batch == SUBLANES, "the S5 scan packs the batch onto the 8 sublanes of a vreg"
    assert seq % MOBA_BLOCK == 0 and seq % GDN_CHUNK == 0
    nc = seq // GDN_CHUNK
    bf = lambda a: a.astype(BF16)

    xf = x.reshape(t, d)
    memf = mem.reshape(-1, d)
    cos, sin = rope_tables(positions)
    ones_comb = jnp.ones((t, LANES), F32)

    for l in range(depth):
        (wq_m, wk_m, wv_m, wu_s, wq_g, wk_g, wv_g, wa_g, wb_g, wz_g, wg_a, wg_b, wg_c) = _in_proj_slices(w_in[l])
        g_mix = norm_mix[l]
        w_ab = jnp.pad(jnp.concatenate([wa_g, wb_g], axis=1), ((0, 0), (0, LANES - 2 * GDN_HEADS)))
        w32 = jnp.concatenate([wq_m, wk_m, wu_s, wq_g, wk_g, wv_g, w_ab], axis=1)
        w16 = jnp.concatenate([wv_m, wz_g, wg_a, wg_b, wg_c], axis=1)
        p32 = norm_matmul(xf, g_mix, bf(w32), F32, tm=1024, tn=640)
        p16 = norm_matmul(xf, g_mix, bf(w16), BF16, tm=1024, tn=1024)
        c_us = 2 * MOBA_WIDTH
        c_qkv = c_us + SSM_WIDTH
        c_ab = c_qkv + 3 * GDN_WIDTH
        u_s = p32[:, c_us:c_us + SSM_WIDTH]
        ab_g = p32[:, c_ab:c_ab + 2 * GDN_HEADS]

        y_a = moba_attention(p32, p16, cos, sin, batch, seq, q_col=0, k_col=MOBA_WIDTH // LANES, v_col=0)

        l_re, l_im, bb_re, bb_im = s5_discretise(ssm_a_re[l], ssm_a_im[l], ssm_log_dt[l], ssm_b_re[l], ssm_b_im[l])
        maps = s5_block_maps(bb_re, bb_im, ssm_c_re[l], ssm_c_im[l])
        u_tm = u_s.reshape(batch, seq, SSM_WIDTH).transpose(1, 0, 2).reshape(t, SSM_WIDTH)
        y_s_tm = s5_gelu(u_tm, maps, l_re, l_im, ssm_d[l])
        y_s = y_s_tm.reshape(seq, batch, SSM_WIDTH).transpose(1, 0, 2).reshape(t, SSM_WIDTH)

        def head_rows(cols):
            return cols.reshape(batch, nc, GDN_CHUNK, GDN_HEADS).transpose(0, 3, 1, 2)
        a_rows = head_rows(ab_g[:, :GDN_HEADS])
        b_rows = head_rows(ab_g[:, GDN_HEADS:2 * GDN_HEADS])
        y_c = gdn_mixer(p32, p16, a_rows, b_rows, gdn_conv[l], gdn_a_log[l], gdn_dt_bias[l], gdn_norm[l],
                        batch, seq, qkv_col=c_qkv // LANES, z_col=MOBA_WIDTH // GDN_WIDTH)

        xf = merge_project(xf, y_a, y_s, y_c, p16, bf(w_up_moba[l]), bf(ssm_w_glu[l]), bf(w_up_gdn[l]),
                           bf(w_out[l]), gate_col=(MOBA_WIDTH + GDN_WIDTH) // d)

        k_x = norm_matmul(memf, norm_mem[l], bf(xa_wk[l]), BF16)
        v_x = norm_matmul(memf, norm_mem[l], bf(xa_wv[l]), BF16)
        xf = cross_attention(xf, norm_xa[l], bf(xa_wq[l]), k_x, v_x, bf(xa_wo[l]), batch, seq)

        if l % 2 == 0:
            i = l // 2
            xf = ffn(xf, norm_ffn[l], ones_comb, bf(ffn_w_gate[i:i + 1]), bf(ffn_w_up[i:i + 1]),
                     bf(ffn_w_down[i:i + 1]))
        else:
            i = l // 2
            route, h_moe = moe_router(xf, norm_ffn[l], moe_w_router[i])
            xf = moe_ffn(xf, h_moe, route, bf(moe_w_gate[i]), bf(moe_w_up[i]), bf(moe_w_down[i]),
                         final_gain=norm_final if l == depth - 1 else None)

    if depth % 2 == 1:
        xf = rmsnorm(xf, norm_final)
    return xf.reshape(batch, seq, d)
```

```python
import functools
import math

import numpy as np
import jax
import jax.numpy as jnp
from jax import lax
from jax.experimental import pallas as pl
from jax.experimental.pallas import tpu as pltpu

F32 = jnp.float32
BF16 = jnp.bfloat16
HIGHEST = lax.Precision.HIGHEST

EPS = 1e-6
NEG_INF = -1e30
MOBA_HEADS = 8
MOBA_HEAD_DIM = 64
MOBA_WIDTH = MOBA_HEADS * MOBA_HEAD_DIM
MOBA_BLOCK = 256
MOBA_TOPK = 3
ROPE_THETA = 10000.0
SSM_WIDTH = 512
SSM_GROUP = 16
SSM_GROUPS = SSM_WIDTH // SSM_GROUP
SSM_STATE = 64
SSM_STATES = SSM_GROUPS * SSM_STATE
GDN_HEADS = 4
GDN_HEAD_DIM = 128
GDN_WIDTH = GDN_HEADS * GDN_HEAD_DIM
GDN_CONV = 4
GDN_CHUNK = 64
XA_HEADS = 4
XA_HEAD_DIM = 128
XA_WIDTH = XA_HEADS * XA_HEAD_DIM
N_EXPERTS = 8
TOP_K = 2

LANES = 128
SUBLANES = 8
VMEM_LIMIT = 56 * 1024 * 1024


def _cparams(*sem):
    return pltpu.CompilerParams(dimension_semantics=sem, vmem_limit_bytes=VMEM_LIMIT)


def _rms(x, g):
    return x * lax.rsqrt(jnp.mean(x * x, axis=-1, keepdims=True) + EPS) * g


def _sigmoid(x):
    return 1.0 / (1.0 + jnp.exp(-x))


def _silu(x):
    return x * _sigmoid(x)


def _dot(a, b):
    return jnp.dot(a, b, preferred_element_type=F32)


def _dot_nt(a, b):
    return lax.dot_general(a, b, (((1,), (1,)), ((), ())), preferred_element_type=F32)


def _dot_tn(a, b):
    return lax.dot_general(a, b, (((0,), (0,)), ((), ())), preferred_element_type=F32)


def _norm_matmul_body(x_ref, g_ref, w_ref, o_ref, h_ref):
    @pl.when(pl.program_id(1) == 0)
    def _():
        h_ref[...] = _rms(x_ref[...], g_ref[...]).astype(BF16)

    o_ref[...] = _dot(h_ref[...], w_ref[...]).astype(o_ref.dtype)


def norm_matmul(x, g, w, out_dtype, tm=512, tn=512):
    t, k = x.shape
    n = w.shape[1]
    tm, tn = min(tm, t), min(tn, n)
    return pl.pallas_call(
        _norm_matmul_body,
        grid=(t // tm, n // tn),
        in_specs=[pl.BlockSpec((tm, k), lambda i, j: (i, 0)),
                  pl.BlockSpec((1, k), lambda i, j: (0, 0)),
                  pl.BlockSpec((k, tn), lambda i, j: (0, j))],
        out_specs=pl.BlockSpec((tm, tn), lambda i, j: (i, j)),
        out_shape=jax.ShapeDtypeStruct((t, n), out_dtype),
        scratch_shapes=[pltpu.VMEM((tm, k), BF16)],
        compiler_params=_cparams("parallel", "arbitrary"),
        name="norm_matmul",
    )(x, g.reshape(1, k), w)


def _rope_body(pos_ref, inv_ref, cos_ref, sin_ref):
    ang = pos_ref[...] * inv_ref[...]
    lane = lax.broadcasted_iota(jnp.int32, ang.shape, 1)
    first_half = (lane % MOBA_HEAD_DIM) < (MOBA_HEAD_DIM // 2)
    cos_ref[...] = jnp.cos(ang)
    s = jnp.sin(ang)
    sin_ref[...] = jnp.where(first_half, -s, s)


def rope_tables(positions):
    t = positions.size
    half = MOBA_HEAD_DIM // 2
    inv = (1.0 / (np.float32(ROPE_THETA) ** (np.arange(0, MOBA_HEAD_DIM, 2, dtype=np.float32)
                                             / np.float32(MOBA_HEAD_DIM)))).astype(np.float32)
    inv_row = jnp.asarray(np.tile(inv, LANES // half).reshape(1, LANES))
    pos = positions.astype(F32).reshape(t, 1)
    ts = min(1024, t)
    return pl.pallas_call(
        _rope_body,
        grid=(t // ts,),
        in_specs=[pl.BlockSpec((ts, 1), lambda i: (i, 0)),
                  pl.BlockSpec((1, LANES), lambda i: (0, 0))],
        out_specs=[pl.BlockSpec((ts, LANES), lambda i: (i, 0))] * 2,
        out_shape=[jax.ShapeDtypeStruct((t, LANES), F32)] * 2,
        compiler_params=_cparams("parallel"),
        name="rope_tables",
    )(pos, inv_row)


def _rope_apply(x, cos, sin_signed):
    lane = lax.broadcasted_iota(jnp.int32, x.shape, 1)
    first_half = (lane % MOBA_HEAD_DIM) < (MOBA_HEAD_DIM // 2)
    partner = jnp.where(first_half, pltpu.roll(x, LANES - MOBA_HEAD_DIM // 2, 1),
                        pltpu.roll(x, MOBA_HEAD_DIM // 2, 1))
    return x * cos + partner * sin_signed


def _moba_body(q_ref, k_ref, v_ref, cq_ref, sq_ref, ck_ref, sk_ref, o_ref, ka_ref, va_ref, km_ref, s_ref,
               mt_ref, acc_ref, *, nb):
    blk = MOBA_BLOCK
    i = pl.program_id(2)

    @pl.when(i == 0)
    def _():
        km_ref[...] = jnp.zeros_like(km_ref)
        lane_b = lax.broadcasted_iota(jnp.int32, (blk, LANES), 1)

        def rope_blk(j, c):
            rows = pl.ds(pl.multiple_of(j * blk, blk), blk)
            kr = _rope_apply(k_ref[rows, :], ck_ref[rows, :], sk_ref[rows, :])
            ka_ref[rows, 0:LANES] = kr.astype(BF16)
            ka_ref[rows, LANES:2 * LANES] = jnp.where(lane_b == j, 1.0, 0.0).astype(BF16)
            km_ref[pl.ds(j, 1), :] = jnp.mean(kr, axis=0, keepdims=True)
            v = v_ref[rows, :].astype(F32)
            va_ref[0, rows, :] = jnp.where(lane_b < MOBA_HEAD_DIM, v,
                                           jnp.where(lane_b == MOBA_HEAD_DIM, 1.0, 0.0)).astype(BF16)
            va_ref[1, rows, :] = jnp.where(lane_b >= MOBA_HEAD_DIM, v,
                                           jnp.where(lane_b == 0, 1.0, 0.0)).astype(BF16)
            return c
        lax.fori_loop(0, nb, rope_blk, 0)

    lane = lax.broadcasted_iota(jnp.int32, (1, LANES), 1)
    head_a = lane < MOBA_HEAD_DIM
    q = _rope_apply(q_ref[...], cq_ref[...], sq_ref[...])
    scale = MOBA_HEAD_DIM ** -0.5
    km = km_ref[...]
    nbp = km_ref.shape[0]
    brow = lax.broadcasted_iota(jnp.int32, (nbp, blk), 0)
    own = pl.ds(pl.multiple_of(i * blk, blk), blk)
    r_idx = lax.broadcasted_iota(jnp.int32, (blk, blk), 0)
    c_idx = lax.broadcasted_iota(jnp.int32, (blk, blk), 1)
    causal = c_idx <= r_idx
    n_pairs = (i + 1) // 2
    q_augs, s_owns = [], []
    for h in range(2):
        qh = jnp.where(head_a, q, 0.0) if h == 0 else jnp.where(head_a, 0.0, q)
        gate = lax.dot_general(km, qh, (((1,), (1,)), ((), ())),
                               preferred_element_type=F32, precision=HIGHEST)
        gate = jnp.where(brow < i, gate, NEG_INF)
        bias_rows = []
        for j in range(nb):
            gj = gate[j:j + 1, :]
            ahead = (gate > gj) | ((gate == gj) & (brow < j))
            rank = jnp.sum(jnp.where(ahead, 1.0, 0.0), axis=0, keepdims=True)
            sel = (rank < float(MOBA_TOPK)) & (j < i)
            bias_rows.append(jnp.where(sel, 0.0, NEG_INF))
        bias_t = jnp.concatenate(bias_rows + [jnp.zeros((LANES - nb, blk), F32)], axis=0)
        q_aug = jnp.concatenate([qh * scale, bias_t.T], axis=1).astype(BF16)
        s_own = jnp.where(causal, _dot_nt(q_aug[:, :LANES], ka_ref[own, 0:LANES]), NEG_INF)
        mt_ref[h] = jnp.maximum(s_own[:, :LANES], s_own[:, LANES:])
        q_augs.append(q_aug)
        s_owns.append(s_own)

    def loop2(n, body):
        def two(k, c):
            body(2 * k)
            body(2 * k + 1)
            return c
        lax.fori_loop(0, lax.shift_right_logical(n, 1), two, 0)

        @pl.when((n & 1) == 1)
        def _():
            body(n - 1)

    def scores(jp):
        rows = pl.ds(pl.multiple_of(jp * (2 * blk), 2 * blk), 2 * blk)
        k2 = ka_ref[rows, :]
        for h in range(2):
            s2 = _dot_nt(q_augs[h], k2)
            s_ref[h, jp] = s2
            m_t = mt_ref[h]
            for col in range(0, 2 * blk, LANES):
                m_t = jnp.maximum(m_t, s2[:, col:col + LANES])
            mt_ref[h] = m_t
    loop2(n_pairs, scores)

    ms = []
    for h in range(2):
        m = jnp.max(mt_ref[h], axis=-1, keepdims=True)
        acc_ref[h] = _dot(jnp.exp(s_owns[h] - m).astype(BF16), va_ref[h, own, :])
        ms.append(m)

    def values(jp):
        rows = pl.ds(pl.multiple_of(jp * (2 * blk), 2 * blk), 2 * blk)
        for h in range(2):
            p2 = jnp.exp(s_ref[h, jp] - ms[h]).astype(BF16)
            acc_ref[h] += _dot(p2, va_ref[h, rows, :])
    loop2(n_pairs, values)
    acc_a, acc_b = acc_ref[0], acc_ref[1]
    out_a = acc_a / acc_a[:, MOBA_HEAD_DIM:MOBA_HEAD_DIM + 1]
    out_b = acc_b / acc_b[:, 0:1]
    o_ref[...] = jnp.where(head_a, out_a, out_b).astype(o_ref.dtype)


def moba_attention(qk, v, cos, sin, batch, seq, q_col=0, k_col=MOBA_WIDTH // LANES, v_col=0):
    nb = seq // MOBA_BLOCK
    pairs = MOBA_WIDTH // LANES
    blk = MOBA_BLOCK
    t = batch * seq
    return pl.pallas_call(
        functools.partial(_moba_body, nb=nb),
        grid=(batch, pairs, nb),
        in_specs=[pl.BlockSpec((blk, LANES), lambda b, p, i: (b * nb + i, q_col + p)),
                  pl.BlockSpec((seq, LANES), lambda b, p, i: (b, k_col + p)),
                  pl.BlockSpec((seq, LANES), lambda b, p, i: (b, v_col + p)),
                  pl.BlockSpec((blk, LANES), lambda b, p, i: (b * nb + i, 0)),
                  pl.BlockSpec((blk, LANES), lambda b, p, i: (b * nb + i, 0)),
                  pl.BlockSpec((seq, LANES), lambda b, p, i: (b, 0)),
                  pl.BlockSpec((seq, LANES), lambda b, p, i: (b, 0))],
        out_specs=pl.BlockSpec((blk, LANES), lambda b, p, i: (b * nb + i, p)),
        out_shape=jax.ShapeDtypeStruct((t, MOBA_WIDTH), BF16),
        scratch_shapes=[pltpu.VMEM((seq, 2 * LANES), BF16),
                        pltpu.VMEM((2, seq, LANES), BF16),
                        pltpu.VMEM((-(-nb // SUBLANES) * SUBLANES, LANES), F32),
                        pltpu.VMEM((2, nb // 2, blk, 2 * blk), F32),
                        pltpu.VMEM((2, blk, LANES), F32),
                        pltpu.VMEM((2, blk, LANES), F32)],
        compiler_params=_cparams("parallel", "parallel", "arbitrary"),
        name="moba_attention",
    )(qk, qk, v, cos, sin, cos, sin)


def _s5_disc_body(are_ref, aim_ref, ldt_ref, bre_ref, bim_ref, lre_ref, lim_ref, bbre_ref, bbim_ref):
    a_re, a_im = are_ref[...], aim_ref[...]
    dt = jnp.exp(ldt_ref[...])
    mag = jnp.exp(a_re * dt)
    l_re = mag * jnp.cos(a_im * dt)
    l_im = mag * jnp.sin(a_im * dt)
    lre_ref[...] = l_re
    lim_ref[...] = l_im
    x, y = l_re - 1.0, l_im
    den = a_re * a_re + a_im * a_im
    c_re = (x * a_re + y * a_im) / den
    c_im = (y * a_re - x * a_im) / den
    b_re, b_im = bre_ref[...], bim_ref[...]
    bbre_ref[...] = c_re * b_re - c_im * b_im
    bbim_ref[...] = c_re * b_im + c_im * b_re


def s5_discretise(a_re, a_im, log_dt, b_re, b_im):
    g, p = a_re.shape
    n = b_re.shape[-1]
    col = lambda a: a.reshape(g * p, 1)
    ldt = jnp.broadcast_to(log_dt[:, None], (g, p))
    outs = pl.pallas_call(
        _s5_disc_body,
        out_shape=[jax.ShapeDtypeStruct((g * p, 1), F32)] * 2 + [jax.ShapeDtypeStruct((g * p, n), F32)] * 2,
        name="s5_discretise",
    )(col(a_re), col(a_im), col(ldt), b_re.reshape(g * p, n), b_im.reshape(g * p, n))
    l_re, l_im, bb_re, bb_im = outs
    return l_re.reshape(g, p), l_im.reshape(g, p), bb_re.reshape(g, p, n), bb_im.reshape(g, p, n)


def _block_diag(blocks):
    g, r, c = blocks.shape
    eye = jnp.eye(g, dtype=blocks.dtype)
    return (blocks[:, :, None, :] * eye[:, None, :, None]).reshape(g * r, g * c)


def _s5_body(u_ref, bre_ref, bim_ref, cre_ref, cim_ref, lre_ref, lim_ref, d_ref, y_ref, h_ref, hb_ref, *, steps):
    rows = SUBLANES
    ns = SSM_STATES
    nq = SSM_WIDTH // LANES
    sq = ns // nq

    @pl.when(pl.program_id(0) == 0)
    def _():
        h_ref[...] = jnp.zeros_like(h_ref)

    u = u_ref[...]
    u16 = u.astype(BF16)
    for q in range(nq):
        uq = u16[:, q * LANES:(q + 1) * LANES]
        hb_ref[:, q * sq:(q + 1) * sq] = _dot(uq, bre_ref[q])
        hb_ref[:, ns + q * sq:ns + (q + 1) * sq] = _dot(uq, bim_ref[q])
    l_re, l_im = lre_ref[...], lim_ref[...]

    def step(t, carry):
        h_re, h_im = carry
        r = pl.ds(pl.multiple_of(t * rows, rows), rows)
        n_re = l_re * h_re - l_im * h_im + hb_ref[r, 0:ns]
        n_im = l_re * h_im + l_im * h_re + hb_ref[r, ns:2 * ns]
        hb_ref[r, 0:ns] = n_re
        hb_ref[r, ns:2 * ns] = n_im
        return n_re, n_im

    h_re, h_im = lax.fori_loop(0, steps, step, (h_ref[:, 0:ns], h_ref[:, ns:2 * ns]))
    h_ref[:, 0:ns] = h_re
    h_ref[:, ns:2 * ns] = h_im
    ys = []
    for q in range(nq):
        ys.append(_dot(hb_ref[:, q * sq:(q + 1) * sq].astype(BF16), cre_ref[q])
                  + _dot(hb_ref[:, ns + q * sq:ns + (q + 1) * sq].astype(BF16), cim_ref[q]))
    y = jnp.concatenate(ys, axis=1) + d_ref[...] * u
    y_ref[...] = jax.nn.gelu(y).astype(y_ref.dtype)


def s5_block_maps(bb_re, bb_im, c_re, c_im):
    nq = SSM_WIDTH // LANES
    gq = SSM_GROUPS // nq
    per_q = lambda a: jnp.stack([_block_diag(a[q * gq:(q + 1) * gq].transpose(0, 2, 1)) for q in range(nq)])
    return (per_q(bb_re).astype(BF16), per_q(bb_im).astype(BF16),
            per_q(c_re).astype(BF16), per_q(-c_im).astype(BF16))


def s5_gelu(u_tm, maps, l_re, l_im, d_skip, steps=64):
    rows = u_tm.shape[0]
    blk = steps * SUBLANES
    ns2 = 2 * SSM_STATES
    full3 = lambda a: pl.BlockSpec(a.shape, lambda c: (0, 0, 0))
    return pl.pallas_call(
        functools.partial(_s5_body, steps=steps),
        grid=(rows // blk,),
        in_specs=[pl.BlockSpec((blk, SSM_WIDTH), lambda c: (c, 0)),
                  full3(maps[0]), full3(maps[1]), full3(maps[2]), full3(maps[3]),
                  pl.BlockSpec((1, SSM_STATES), lambda c: (0, 0)),
                  pl.BlockSpec((1, SSM_STATES), lambda c: (0, 0)),
                  pl.BlockSpec((1, SSM_WIDTH), lambda c: (0, 0))],
        out_specs=pl.BlockSpec((blk, SSM_WIDTH), lambda c: (c, 0)),
        out_shape=jax.ShapeDtypeStruct((rows, SSM_WIDTH), BF16),
        scratch_shapes=[pltpu.VMEM((SUBLANES, ns2), F32),
                        pltpu.VMEM((blk, ns2), F32)],
        compiler_params=_cparams("arbitrary"),
        name="s5_scan",
    )(u_tm, *maps, l_re.reshape(1, -1), l_im.reshape(1, -1), d_skip.reshape(1, -1))


GDN_GROUP = 16


def _gdn_prep_body(q_ref, k_ref, v_ref, a_ref, b_ref, wq_ref, wk_ref, wv_ref, alog_ref, dtb_ref,
                   u_ref, w_ref, qd_ref, kd_ref, qk_ref, gl_ref, qs_ref, ks_ref, vs_ref,
                   *, seq):
    L = GDN_CHUNK
    G = min(GDN_GROUP, seq // L)
    dk = GDN_HEAD_DIM
    nc = seq // L
    hd = pl.program_id(1)
    cb = 256
    halo = SUBLANES

    def conv_blk(rb, c):
        base = pl.multiple_of(rb * cb, cb)
        prev = pl.multiple_of(jnp.maximum(base - halo, 0), halo)
        for src, wref, dst, norm, mul in ((q_ref, wq_ref, qs_ref, True, dk ** -0.5),
                                          (k_ref, wk_ref, ks_ref, True, 1.0),
                                          (v_ref, wv_ref, vs_ref, False, 1.0)):
            cur = src[pl.ds(base, cb), :]
            head = jnp.where(rb > 0, src[pl.ds(prev, halo), :], 0.0)
            ext = jnp.concatenate([head, cur], axis=0)
            wts = wref[...]
            y = ext[halo:halo + cb] * wts[GDN_CONV - 1:GDN_CONV]
            for tap in range(1, GDN_CONV):
                y = y + ext[halo - tap:halo - tap + cb] * wts[GDN_CONV - 1 - tap:GDN_CONV - tap]
            y = _silu(y)
            if norm:
                y = y * lax.rsqrt(jnp.sum(y * y, axis=-1, keepdims=True) + EPS) * mul
            dst[pl.ds(base, cb), :] = y
        return c
    lax.fori_loop(0, seq // cb, conv_blk, 0)

    gl_rows = G * L
    ri = lax.broadcasted_iota(jnp.int32, (gl_rows, L), 0) % L
    ci = lax.broadcasted_iota(jnp.int32, (gl_rows, L), 1)
    tril = ci <= ri
    strict = ci < ri
    eye = ci == ri
    ri3 = lax.broadcasted_iota(jnp.int32, (G, L, L), 1)
    ci3 = lax.broadcasted_iota(jnp.int32, (G, L, L), 2)
    strict3 = ci3 < ri3
    eye_f = jnp.where(ci3 == ri3, 1.0, 0.0)
    r2 = lax.broadcasted_iota(jnp.int32, (L, L), 0)
    c2 = lax.broadcasted_iota(jnp.int32, (L, L), 1)
    upper_f = jnp.where(r2 <= c2, 1.0, 0.0)
    chunk_of_row = lax.broadcasted_iota(jnp.int32, (gl_rows, G), 0) // L
    expand = jnp.where(chunk_of_row == lax.broadcasted_iota(jnp.int32, (gl_rows, G), 1), 1.0, 0.0)
    a_coef = -jnp.exp(jnp.full((1, L), alog_ref[hd], F32))
    dt_bias = dtb_ref[hd]
    hp = functools.partial(jnp.dot, preferred_element_type=F32, precision=HIGHEST)

    def bmm(a, b):
        return jnp.einsum('gij,gjk->gik', a, b, preferred_element_type=F32)

    def bmm_nt(a, b):
        return jnp.einsum('gid,gjd->gij', a, b, preferred_element_type=F32)

    def to_col(rows_b):
        return jnp.sum(jnp.where(eye, rows_b, 0.0), axis=1, keepdims=True)

    def prep(cg, carry):
        c0 = pl.multiple_of(cg * G, G)
        rows = pl.ds(pl.multiple_of(cg * gl_rows, gl_rows), gl_rows)
        qc, kc, vc = qs_ref[rows, :], ks_ref[rows, :], vs_ref[rows, :]
        z = a_ref[pl.ds(c0, G), :] + dt_bias
        softplus = jnp.maximum(z, 0.0) + jnp.log(1.0 + jnp.exp(-jnp.abs(z)))
        g_rows = hp(a_coef * softplus, upper_f)
        beta_rows = _sigmoid(b_ref[pl.ds(c0, G), :])
        both = hp(expand, jnp.concatenate([g_rows, beta_rows], axis=1))
        g_rb = both[:, :L]
        g_col = to_col(g_rb)
        beta_col = to_col(both[:, L:])
        g_last = g_rb[:, L - 1:L]
        decay = jnp.where(tril, jnp.exp(jnp.where(tril, g_col - g_rb, 0.0)), 0.0)
        k_beta = kc * beta_col
        v_beta = vc * beta_col
        to3 = lambda a: a.reshape(G, L, a.shape[-1])
        kb16, k16 = to3(k_beta.astype(BF16)), to3(kc.astype(BF16))
        a_low = jnp.where(strict3, bmm_nt(kb16, k16) * to3(decay), 0.0)
        t_inv = eye_f - a_low
        pw = a_low
        span = 2
        while span < L:
            pw16 = pw.astype(BF16)
            pw = bmm(pw16, pw16)
            t_inv = t_inv + bmm(t_inv.astype(BF16), pw.astype(BF16))
            span *= 2
        ti16 = t_inv.astype(BF16)
        u = bmm(ti16, to3(v_beta.astype(BF16)))
        w = bmm(ti16, to3((k_beta * jnp.exp(g_col)).astype(BF16)))
        qk = bmm_nt(to3(qc.astype(BF16)), k16) * to3(decay)
        u_ref[rows, :] = u.reshape(gl_rows, dk)
        w_ref[rows, :] = w.reshape(gl_rows, dk).astype(BF16)
        qd_ref[rows, :] = (qc * jnp.exp(g_col)).astype(BF16)
        kd_ref[rows, :] = (kc * jnp.exp(g_last - g_col)).astype(BF16)
        qk_ref[pl.ds(c0, G)] = qk.astype(BF16)
        gl_ref[pl.ds(c0, G), :] = jnp.broadcast_to(jnp.exp(g_rows[:, L - 1:L]), (G, LANES))
        return carry
    lax.fori_loop(0, nc // G, prep, 0)


def _gdn_scan_body(u_ref, w_ref, qd_ref, kd_ref, qk_ref, gl_ref, z_ref, gn_ref, o_ref, st_ref, *, chunks):
    L = GDN_CHUNK
    d = GDN_HEAD_DIM
    nrow = u_ref.shape[0]

    @pl.when(pl.program_id(1) == 0)
    def _():
        st_ref[...] = jnp.zeros_like(st_ref)

    gn = gn_ref[...]

    def chunk(c, carry):
        rows = pl.ds(pl.multiple_of(c * L, L), L)
        for r in range(nrow):
            for h in range(GDN_HEADS):
                cols = slice(h * d, (h + 1) * d)
                s = st_ref[r, h]
                s16 = s.astype(BF16)
                v_new = u_ref[r, rows, cols] - _dot(w_ref[r, rows, cols], s16)
                vn16 = v_new.astype(BF16)
                o = _dot(qd_ref[r, rows, cols], s16) + _dot(qk_ref[r, h, c], vn16)
                st_ref[r, h] = s * gl_ref[r, h, pl.ds(c, 1), :] + _dot_tn(kd_ref[r, rows, cols], vn16)
                o_ref[r, rows, cols] = (_rms(o, gn)
                                        * _silu(z_ref[r, rows, cols].astype(F32))).astype(o_ref.dtype)
        return carry
    lax.fori_loop(0, chunks, chunk, 0)


def gdn_mixer(qkv, z, a_rows, b_rows, conv_w, a_log, dt_bias, gnorm, batch, seq, ts=1024, qkv_col=0, z_col=0):
    t = batch * seq
    hds = GDN_HEADS
    nc = seq // GDN_CHUNK
    L = GDN_CHUNK
    d = GDN_HEAD_DIM
    seq_spec = lambda off: pl.BlockSpec((seq, d), lambda b, h: (b, qkv_col + off + h))
    w_spec = lambda off: pl.BlockSpec((GDN_CONV, d), lambda b, h: (0, off + h))
    row_spec = pl.BlockSpec((None, None, nc, L), lambda b, h: (b, h, 0, 0))
    smem = pl.BlockSpec(memory_space=pltpu.SMEM)
    head_out = pl.BlockSpec((seq, d), lambda b, h: (b, h))
    u, w, qd, kd, qk, gl = pl.pallas_call(
        functools.partial(_gdn_prep_body, seq=seq),
        grid=(batch, hds),
        in_specs=[seq_spec(0), seq_spec(hds), seq_spec(2 * hds), row_spec, row_spec,
                  w_spec(0), w_spec(hds), w_spec(2 * hds), smem, smem],
        out_specs=[head_out, head_out, head_out, head_out,
                   pl.BlockSpec((None, None, nc, L, L), lambda b, h: (b, h, 0, 0, 0)),
                   pl.BlockSpec((None, None, nc, LANES), lambda b, h: (b, h, 0, 0))],
        out_shape=[jax.ShapeDtypeStruct((t, GDN_WIDTH), F32),
                   jax.ShapeDtypeStruct((t, GDN_WIDTH), BF16),
                   jax.ShapeDtypeStruct((t, GDN_WIDTH), BF16),
                   jax.ShapeDtypeStruct((t, GDN_WIDTH), BF16),
                   jax.ShapeDtypeStruct((batch, hds, nc, L, L), BF16),
                   jax.ShapeDtypeStruct((batch, hds, nc, LANES), F32)],
        scratch_shapes=[pltpu.VMEM((seq, d), F32), pltpu.VMEM((seq, d), F32), pltpu.VMEM((seq, d), F32)],
        compiler_params=_cparams("parallel", "parallel"),
        name="gdn_prep",
    )(qkv, qkv, qkv, a_rows, b_rows, conv_w, conv_w, conv_w, a_log, dt_bias)

    ts = min(ts, seq)
    spb = seq // ts
    cps = ts // L
    nrow = 2 if batch % 2 == 0 else 1
    by_row = lambda a: a.reshape(batch, seq, a.shape[-1])
    tok = pl.BlockSpec((nrow, ts, GDN_WIDTH), lambda b, s: (b, s, 0))
    out = pl.pallas_call(
        functools.partial(_gdn_scan_body, chunks=cps),
        grid=(batch // nrow, spb),
        in_specs=[tok, tok, tok, tok,
                  pl.BlockSpec((nrow, hds, cps, L, L), lambda b, s: (b, 0, s, 0, 0)),
                  pl.BlockSpec((nrow, hds, cps, LANES), lambda b, s: (b, 0, s, 0)),
                  pl.BlockSpec((nrow, ts, GDN_WIDTH), lambda b, s: (b, s, z_col)),
                  pl.BlockSpec((1, d), lambda b, s: (0, 0))],
        out_specs=tok,
        out_shape=jax.ShapeDtypeStruct((batch, seq, GDN_WIDTH), BF16),
        scratch_shapes=[pltpu.VMEM((nrow, hds, d, d), F32)],
        compiler_params=_cparams("parallel", "arbitrary"),
        name="gdn_scan",
    )(by_row(u), by_row(w), by_row(qd), by_row(kd), qk, gl, by_row(z), gnorm.reshape(1, d))
    return out.reshape(t, GDN_WIDTH)


def _merge_body(x_ref, ya_ref, ys_ref, yc_ref, ga_ref, gb_ref, gc_ref, wa_ref, wg_ref, wc_ref, wo_ref, o_ref):
    d = x_ref.shape[1]
    y_a = _dot(ya_ref[...], wa_ref[...])
    glu = _dot(ys_ref[...], wg_ref[...])
    y_b = glu[:, :d] * _sigmoid(glu[:, d:])
    y_c = _dot(yc_ref[...], wc_ref[...])
    gate = lambda r: _sigmoid(r[...].astype(F32))
    merged = gate(ga_ref) * y_a + gate(gb_ref) * y_b + gate(gc_ref) * y_c
    o_ref[...] = x_ref[...] + _dot(merged.astype(BF16), wo_ref[...])


def merge_project(x, y_a, y_s, y_c, gates, w_a, w_glu, w_c, w_o, tm=512, gate_col=0):
    t, d = x.shape
    row = lambda n: pl.BlockSpec((tm, n), lambda i: (i, 0))
    gate = lambda k: pl.BlockSpec((tm, d), lambda i: (i, gate_col + k))
    full = lambda a: pl.BlockSpec(a.shape, lambda i: (0, 0))
    return pl.pallas_call(
        _merge_body,
        grid=(t // tm,),
        in_specs=[row(d), row(y_a.shape[1]), row(y_s.shape[1]), row(y_c.shape[1]), gate(0), gate(1), gate(2),
                  full(w_a), full(w_glu), full(w_c), full(w_o)],
        out_specs=row(d),
        out_shape=jax.ShapeDtypeStruct((t, d), F32),
        compiler_params=_cparams("parallel"),
        name="merge_project",
    )(x, y_a, y_s, y_c, gates, gates, gates, w_a, w_glu, w_c, w_o)


def _xattn_body(x_ref, g_ref, wq_ref, k_ref, v_ref, wo_ref, o_ref):
    x = x_ref[...]
    h = _rms(x, g_ref[...]).astype(BF16)
    q = _dot(h, wq_ref[...]).astype(BF16)
    outs = []
    for hd in range(XA_HEADS):
        cols = slice(hd * XA_HEAD_DIM, (hd + 1) * XA_HEAD_DIM)
        s = _dot_nt(q[:, cols], k_ref[:, cols]) * (XA_HEAD_DIM ** -0.5)
        p = jnp.exp(s - jnp.max(s, axis=-1, keepdims=True))
        p = p / jnp.sum(p, axis=-1, keepdims=True)
        outs.append(_dot(p.astype(BF16), v_ref[:, cols]))
    o = jnp.concatenate(outs, axis=1).astype(BF16)
    o_ref[...] = x + _dot(o, wo_ref[...])


def cross_attention(x, g, wq, k, v, wo, batch, seq, tm=512):
    t, d = x.shape
    m = k.shape[0] // batch
    spb = seq // tm
    full = lambda a: pl.BlockSpec(a.shape, lambda i: (0, 0))
    return pl.pallas_call(
        _xattn_body,
        grid=(t // tm,),
        in_specs=[pl.BlockSpec((tm, d), lambda i: (i, 0)),
                  pl.BlockSpec((1, d), lambda i: (0, 0)),
                  full(wq),
                  pl.BlockSpec((m, XA_WIDTH), lambda i: (i // spb, 0)),
                  pl.BlockSpec((m, XA_WIDTH), lambda i: (i // spb, 0)),
                  full(wo)],
        out_specs=pl.BlockSpec((tm, d), lambda i: (i, 0)),
        out_shape=jax.ShapeDtypeStruct((t, d), F32),
        compiler_params=_cparams("parallel"),
        name="cross_attention",
    )(x, g.reshape(1, d), wq, k, v, wo)


def _ffn_body(x_ref, g_ref, wg_ref, wu_ref, wd_ref, o_ref, h_ref, acc_ref):
    f = pl.program_id(1)

    @pl.when(f == 0)
    def _():
        h_ref[...] = _rms(x_ref[...], g_ref[...]).astype(BF16)
        acc_ref[...] = jnp.zeros_like(acc_ref)

    h = h_ref[...]
    act = (_silu(_dot(h, wg_ref[...])) * _dot(h, wu_ref[...])).astype(BF16)
    acc_ref[...] += _dot(act, wd_ref[...])

    @pl.when(f == pl.num_programs(1) - 1)
    def _():
        o_ref[...] = x_ref[...] + acc_ref[...]


def ffn(x, g, w_gate, w_up, w_down, tm=512, tf=1408):
    t, d = x.shape
    f = w_gate.shape[1]
    return pl.pallas_call(
        _ffn_body,
        grid=(t // tm, f // tf),
        in_specs=[pl.BlockSpec((tm, d), lambda i, k: (i, 0)),
                  pl.BlockSpec((1, d), lambda i, k: (0, 0)),
                  pl.BlockSpec((d, tf), lambda i, k: (0, k)),
                  pl.BlockSpec((d, tf), lambda i, k: (0, k)),
                  pl.BlockSpec((tf, d), lambda i, k: (k, 0))],
        out_specs=pl.BlockSpec((tm, d), lambda i, k: (i, 0)),
        out_shape=jax.ShapeDtypeStruct((t, d), F32),
        scratch_shapes=[pltpu.VMEM((tm, d), BF16), pltpu.VMEM((tm, d), F32)],
        compiler_params=_cparams("parallel", "arbitrary"),
        name="ffn",
    )(x, g.reshape(1, d), w_gate, w_up, w_down)


def _router_body(x_ref, g_ref, w_ref, r_ref, h_ref):
    h = _rms(x_ref[...], g_ref[...])
    h_ref[...] = h.astype(BF16)
    logits = jnp.dot(h, w_ref[...], preferred_element_type=F32, precision=HIGHEST)
    lane = lax.broadcasted_iota(jnp.int32, logits.shape, 1).astype(F32)
    logits = jnp.where(lane < N_EXPERTS, logits, -jnp.inf)
    v1 = jnp.max(logits, axis=-1, keepdims=True)
    i1 = jnp.min(jnp.where(logits == v1, lane, float(LANES)), axis=-1, keepdims=True)
    rest = jnp.where(lane == i1, -jnp.inf, logits)
    v2 = jnp.max(rest, axis=-1, keepdims=True)
    i2 = jnp.min(jnp.where(rest == v2, lane, float(LANES)), axis=-1, keepdims=True)
    e2 = jnp.exp(v2 - v1)
    w1 = 1.0 / (1.0 + e2)
    w2 = e2 / (1.0 + e2)
    r_ref[...] = (jnp.where(lane == 0.0, i1, 0.0) + jnp.where(lane == 1.0, i2, 0.0)
                  + jnp.where(lane == 2.0, w1, 0.0) + jnp.where(lane == 3.0, w2, 0.0))


def moe_router(x, g, w_router, tm=512):
    t, d = x.shape
    w = jnp.pad(w_router, ((0, 0), (0, LANES - w_router.shape[1])))
    return pl.pallas_call(
        _router_body,
        grid=(t // tm,),
        in_specs=[pl.BlockSpec((tm, d), lambda i: (i, 0)),
                  pl.BlockSpec((1, d), lambda i: (0, 0)),
                  pl.BlockSpec((d, LANES), lambda i: (0, 0))],
        out_specs=[pl.BlockSpec((tm, LANES), lambda i: (i, 0)), pl.BlockSpec((tm, d), lambda i: (i, 0))],
        out_shape=[jax.ShapeDtypeStruct((t, LANES), F32), jax.ShapeDtypeStruct((t, d), BF16)],
        compiler_params=_cparams("parallel"),
        name="moe_router",
    )(x, g.reshape(1, d), w)


MOE_TILE = 512
MOE_GTILE = 256
MOE_CHUNK = 512
MOE_RBLK = 256


def _count_le(sorted_vals, queries):
    return jnp.sum((sorted_vals[None, :] <= queries[:, None]).astype(jnp.int32), axis=1)


def _moe_plan(route, t):
    e = N_EXPERTS
    tm, ck, rb = MOE_TILE, MOE_CHUNK, MOE_RBLK
    n_tiles = (TOP_K * t) // tm + e
    p = n_tiles * tm
    n_chunks = t // ck
    i32 = jnp.int32
    e1 = route[:, 0].astype(i32)
    e2 = route[:, 1].astype(i32)
    oh1 = jax.nn.one_hot(e1, e, dtype=i32)
    oh2 = jax.nn.one_hot(e2, e, dtype=i32)
    cnt = oh1 + oh2
    incl = jnp.cumsum(cnt, axis=0)
    excl = incl - cnt
    n_e = incl[-1]
    g_e = ((n_e + tm - 1) // tm) * tm
    off_end = jnp.cumsum(g_e)
    off = off_end - g_e
    pos_all = off[None, :] + excl
    pos1 = jnp.sum(pos_all * oh1, axis=1)
    pos2 = jnp.sum(pos_all * oh2, axis=1)
    tile_expert = jnp.minimum(_count_le(off_end, jnp.arange(n_tiles, dtype=i32) * tm), e - 1)
    gm = MOE_GTILE
    n_gt = p // gm
    gt_start = jnp.arange(n_gt, dtype=i32) * gm
    sel = jax.nn.one_hot(jnp.minimum(_count_le(off_end, gt_start), e - 1), e, dtype=i32)
    r0 = gt_start - jnp.sum(sel * off[None, :], axis=1)
    n_sel = jnp.sum(sel * n_e[None, :], axis=1)
    has_rows = r0 < n_sel
    r1 = jnp.minimum(r0 + gm, n_sel) - 1
    ends = incl[ck - 1::ck]
    ends_sel = jnp.sum(ends[None, :, :] * sel[:, None, :], axis=2)
    c_lo = jnp.where(has_rows, jnp.sum((ends_sel <= r0[:, None]).astype(i32), axis=1), 0)
    c_hi = jnp.where(has_rows, jnp.sum((ends_sel <= r1[:, None]).astype(i32), axis=1), 0)
    n_it = c_hi - c_lo + 1
    it_end = jnp.cumsum(n_it)
    it_start = it_end - n_it
    wmax = n_gt + e * n_chunks
    w_idx = jnp.arange(wmax, dtype=i32)
    g_tile = jnp.minimum(_count_le(it_end, w_idx), n_gt - 1)
    g_valid = w_idx < it_end[-1]
    g_chunk = jnp.where(g_valid, c_lo[g_tile] + (w_idx - it_start[g_tile]), c_hi[n_gt - 1])
    g_first = (g_valid & (w_idx == it_start[g_tile])).astype(i32)
    g_last = (g_valid & (w_idx == it_end[g_tile] - 1)).astype(i32)
    gather = (g_tile, g_chunk.astype(i32), g_first, g_last, g_valid.astype(i32))
    lo = off[None, :] + excl[::ck]
    hi = off[None, :] + ends
    nblk = jnp.where(hi > lo, (hi - 1) // rb - lo // rb + 1, 0).reshape(-1)
    blk_lo = (lo // rb).reshape(-1)
    cb_end = jnp.cumsum(nblk)
    cb_start = cb_end - nblk
    cmax = p // rb + e * n_chunks
    c_idx = jnp.arange(cmax, dtype=i32)
    pair = jnp.minimum(_count_le(cb_end, c_idx), nblk.shape[0] - 1)
    c_valid = c_idx < cb_end[-1]
    c_tile = jnp.where(c_valid, pair // e, n_chunks - 1)
    c_blk = jnp.where(c_valid, blk_lo[pair] + (c_idx - cb_start[pair]), 0)
    per_tile = jnp.sum(nblk.reshape(n_chunks, e), axis=1)
    t_end = jnp.cumsum(per_tile)
    c_first = (c_valid & (c_idx == (t_end - per_tile)[c_tile])).astype(i32)
    c_last = (c_valid & (c_idx == t_end[c_tile] - 1)).astype(i32)
    combine = (c_tile.astype(i32), c_blk.astype(i32), c_first, c_last, c_valid.astype(i32))
    return dict(n_tiles=n_tiles, p=p, tile_expert=tile_expert.astype(i32), pos1=pos1, pos2=pos2,
                gather=gather, combine=combine)


def _moe_gather_body(tile_ref, chunk_ref, first_ref, last_ref, valid_ref, meta_ref, h_ref,
                     o_ref, wr_ref, acc_ref, wacc_ref):
    w = pl.program_id(0)

    @pl.when(first_ref[w] == 1)
    def _():
        acc_ref[...] = jnp.zeros_like(acc_ref)
        wacc_ref[...] = jnp.zeros_like(wacc_ref)

    @pl.when(valid_ref[w] == 1)
    def _():
        row = (lax.broadcasted_iota(jnp.int32, (MOE_GTILE, MOE_CHUNK), 0) + tile_ref[w] * MOE_GTILE).astype(F32)
        hit1 = row == meta_ref[0:1, :]
        hit2 = row == meta_ref[1:2, :]
        onehot = jnp.where(hit1, 1.0, jnp.where(hit2, 1.0, 0.0)).astype(BF16)
        acc_ref[...] += _dot(onehot, h_ref[...])
        wts = jnp.where(hit1, meta_ref[2:3, :], jnp.where(hit2, meta_ref[3:4, :], 0.0))
        wacc_ref[...] += jnp.sum(wts, axis=1, keepdims=True)

    @pl.when(last_ref[w] == 1)
    def _():
        o_ref[...] = acc_ref[...].astype(o_ref.dtype)
        wr_ref[...] = jnp.broadcast_to(wacc_ref[...], wr_ref.shape)


def _moe_ffn_body(exp_ref, x_ref, wr_ref, wg_ref, wu_ref, wd_ref, o_ref, acc_ref):
    f = pl.program_id(1)

    @pl.when(f == 0)
    def _():
        acc_ref[...] = jnp.zeros_like(acc_ref)

    x = x_ref[...]
    act = (_silu(_dot(x, wg_ref[...])) * _dot(x, wu_ref[...])).astype(BF16)
    acc_ref[...] += _dot(act, wd_ref[...])

    @pl.when(f == pl.num_programs(1) - 1)
    def _():
        o_ref[...] = (acc_ref[...] * wr_ref[:, 0:1]).astype(o_ref.dtype)


def _moe_combine_body(tile_ref, blk_ref, first_ref, last_ref, valid_ref, x_ref, pos_ref, y_ref, gf_ref,
                      o_ref, acc_ref, *, final_norm):
    w = pl.program_id(0)

    @pl.when(first_ref[w] == 1)
    def _():
        acc_ref[...] = x_ref[...]

    @pl.when(valid_ref[w] == 1)
    def _():
        base = blk_ref[w] * MOE_RBLK
        col = lax.broadcasted_iota(jnp.int32, (MOE_CHUNK, MOE_RBLK), 1) + base
        reps = MOE_RBLK // LANES
        p1 = jnp.concatenate([pos_ref[:, 0:LANES]] * reps, axis=1)
        p2 = jnp.concatenate([pos_ref[:, LANES:2 * LANES]] * reps, axis=1)
        hit = jnp.where(col == p1, 1.0, jnp.where(col == p2, 1.0, 0.0))
        acc_ref[...] += _dot(hit.astype(BF16), y_ref[...])

    @pl.when(last_ref[w] == 1)
    def _():
        o_ref[...] = _rms(acc_ref[...], gf_ref[...]) if final_norm else acc_ref[...]


def moe_ffn(x, h, route, w_gate, w_up, w_down, final_gain=None, tf=1408):
    t, d = x.shape
    f = w_gate.shape[2]
    tm, ck, rb = MOE_TILE, MOE_CHUNK, MOE_RBLK
    plan = _moe_plan(route, t)
    n_tiles, p = plan["n_tiles"], plan["p"]
    assert p < 2 ** 24, "row positions are carried exactly in f32"
    meta = jnp.stack([plan["pos1"].astype(F32), plan["pos2"].astype(F32), route[:, 2], route[:, 3]]
                     + [jnp.zeros((t,), F32)] * (SUBLANES - 4))
    pos_lanes = jnp.concatenate([jnp.broadcast_to(plan["pos1"][:, None], (t, LANES)),
                                 jnp.broadcast_to(plan["pos2"][:, None], (t, LANES))], axis=1)

    xs, w_row = pl.pallas_call(
        _moe_gather_body,
        grid_spec=pltpu.PrefetchScalarGridSpec(
            num_scalar_prefetch=5,
            grid=(plan["gather"][0].shape[0],),
            in_specs=[pl.BlockSpec((SUBLANES, ck), lambda w, tl, ch, fi, la, va: (0, ch[w])),
                      pl.BlockSpec((ck, d), lambda w, tl, ch, fi, la, va: (ch[w], 0))],
            out_specs=[pl.BlockSpec((MOE_GTILE, d), lambda w, tl, ch, fi, la, va: (tl[w], 0)),
                       pl.BlockSpec((MOE_GTILE, LANES), lambda w, tl, ch, fi, la, va: (tl[w], 0))],
            scratch_shapes=[pltpu.VMEM((MOE_GTILE, d), F32), pltpu.VMEM((MOE_GTILE, 1), F32)]),
        out_shape=[jax.ShapeDtypeStruct((p, d), BF16), jax.ShapeDtypeStruct((p, LANES), F32)],
        compiler_params=_cparams("arbitrary"),
        name="moe_gather",
    )(*plan["gather"], meta, h)

    ys = pl.pallas_call(
        _moe_ffn_body,
        grid_spec=pltpu.PrefetchScalarGridSpec(
            num_scalar_prefetch=1,
            grid=(n_tiles, f // tf),
            in_specs=[pl.BlockSpec((tm, d), lambda i, k, ex: (i, 0)),
                      pl.BlockSpec((tm, LANES), lambda i, k, ex: (i, 0)),
                      pl.BlockSpec((None, d, tf), lambda i, k, ex: (ex[i], 0, k)),
                      pl.BlockSpec((None, d, tf), lambda i, k, ex: (ex[i], 0, k)),
                      pl.BlockSpec((None, tf, d), lambda i, k, ex: (ex[i], k, 0))],
            out_specs=pl.BlockSpec((tm, d), lambda i, k, ex: (i, 0)),
            scratch_shapes=[pltpu.VMEM((tm, d), F32)]),
        out_shape=jax.ShapeDtypeStruct((p, d), BF16),
        compiler_params=_cparams("parallel", "arbitrary"),
        name="moe_ffn",
    )(plan["tile_expert"], xs, w_row, w_gate, w_up, w_down)

    final_norm = final_gain is not None
    gain = (final_gain if final_norm else jnp.ones((d,), F32)).reshape(1, d)
    return pl.pallas_call(
        functools.partial(_moe_combine_body, final_norm=final_norm),
        grid_spec=pltpu.PrefetchScalarGridSpec(
            num_scalar_prefetch=5,
            grid=(plan["combine"][0].shape[0],),
            in_specs=[pl.BlockSpec((ck, d), lambda w, tl, bk, fi, la, va: (tl[w], 0)),
                      pl.BlockSpec((ck, 2 * LANES), lambda w, tl, bk, fi, la, va: (tl[w], 0)),
                      pl.BlockSpec((rb, d), lambda w, tl, bk, fi, la, va: (bk[w], 0)),
                      pl.BlockSpec((1, d), lambda w, tl, bk, fi, la, va: (0, 0))],
            out_specs=pl.BlockSpec((ck, d), lambda w, tl, bk, fi, la, va: (tl[w], 0)),
            scratch_shapes=[pltpu.VMEM((ck, d), F32)]),
        out_shape=jax.ShapeDtypeStruct((t, d), F32),
        compiler_params=_cparams("arbitrary"),
        name="moe_combine",
    )(*plan["combine"], x, pos_lanes, ys, gain)


def _rmsnorm_body(x_ref, g_ref, o_ref):
    o_ref[...] = _rms(x_ref[...], g_ref[...])


def rmsnorm(x, g, tm=1024):
    t, d = x.shape
    tm = min(tm, t)
    return pl.pallas_call(
        _rmsnorm_body,
        grid=(t // tm,),
        in_specs=[pl.BlockSpec((tm, d), lambda i: (i, 0)), pl.BlockSpec((1, d), lambda i: (0, 0))],
        out_specs=pl.BlockSpec((tm, d), lambda i: (i, 0)),
        out_shape=jax.ShapeDtypeStruct((t, d), F32),
        compiler_params=_cparams("parallel"),
        name="rmsnorm",
    )(x, g.reshape(1, d))


def _in_proj_slices(w):
    sizes = (MOBA_WIDTH, MOBA_WIDTH, MOBA_WIDTH, SSM_WIDTH, GDN_WIDTH, GDN_WIDTH, GDN_WIDTH,
             GDN_HEADS, GDN_HEADS, GDN_WIDTH, w.shape[0], w.shape[0], w.shape[0])
    parts, start = [], 0
    for size in sizes:
        parts.append(w[:, start:start + size])
        start += size
    return parts


def kernel(x, mem, positions, norm_mix, w_in, ssm_a_re, ssm_a_im, ssm_log_dt, ssm_b_re, ssm_b_im, ssm_c_re, ssm_c_im, ssm_d, ssm_w_glu, gdn_conv, gdn_a_log, gdn_dt_bias, gdn_norm, w_up_moba, w_up_gdn, w_out, norm_xa, norm_mem, xa_wq, xa_wk, xa_wv, xa_wo, norm_ffn, ffn_w_gate, ffn_w_up, ffn_w_down, moe_w_router, moe_w_gate, moe_w_up, moe_w_down, norm_final):
    batch, seq, d = x.shape
    depth = w_in.shape[0]
    t = batch * seq
    assert batch == SUBLANES, "the S5 scan packs the batch onto the 8 sublanes of a vreg"
    assert seq % MOBA_BLOCK == 0 and seq % GDN_CHUNK == 0
    nc = seq // GDN_CHUNK
    bf = lambda a: a.astype(BF16)

    xf = x.reshape(t, d)
    memf = mem.reshape(-1, d)
    cos, sin = rope_tables(positions)

    for l in range(depth):
        (wq_m, wk_m, wv_m, wu_s, wq_g, wk_g, wv_g, wa_g, wb_g, wz_g, wg_a, wg_b, wg_c) = _in_proj_slices(w_in[l])
        g_mix = norm_mix[l]
        w_ab = jnp.pad(jnp.concatenate([wa_g, wb_g], axis=1), ((0, 0), (0, LANES - 2 * GDN_HEADS)))
        w32 = jnp.concatenate([wq_m, wk_m, wu_s, wq_g, wk_g, wv_g, w_ab], axis=1)
        w16 = jnp.concatenate([wv_m, wz_g, wg_a, wg_b, wg_c], axis=1)
        p32 = norm_matmul(xf, g_mix, bf(w32), F32, tm=2048, tn=640)
        p16 = norm_matmul(xf, g_mix, bf(w16), BF16, tm=2048, tn=1024)
        c_us = 2 * MOBA_WIDTH
        c_qkv = c_us + SSM_WIDTH
        c_ab = c_qkv + 3 * GDN_WIDTH
        u_s = p32[:, c_us:c_us + SSM_WIDTH]
        ab_g = p32[:, c_ab:c_ab + 2 * GDN_HEADS]

        y_a = moba_attention(p32, p16, cos, sin, batch, seq, q_col=0, k_col=MOBA_WIDTH // LANES, v_col=0)

        l_re, l_im, bb_re, bb_im = s5_discretise(ssm_a_re[l], ssm_a_im[l], ssm_log_dt[l], ssm_b_re[l], ssm_b_im[l])
        maps = s5_block_maps(bb_re, bb_im, ssm_c_re[l], ssm_c_im[l])
        u_tm = u_s.reshape(batch, seq, SSM_WIDTH).transpose(1, 0, 2).reshape(t, SSM_WIDTH)
        y_s_tm = s5_gelu(u_tm, maps, l_re, l_im, ssm_d[l])
        y_s = y_s_tm.reshape(seq, batch, SSM_WIDTH).transpose(1, 0, 2).reshape(t, SSM_WIDTH)

        def head_rows(cols):
            return cols.reshape(batch, nc, GDN_CHUNK, GDN_HEADS).transpose(0, 3, 1, 2)
        a_rows = head_rows(ab_g[:, :GDN_HEADS])
        b_rows = head_rows(ab_g[:, GDN_HEADS:2 * GDN_HEADS])
        y_c = gdn_mixer(p32, p16, a_rows, b_rows, gdn_conv[l], gdn_a_log[l], gdn_dt_bias[l], gdn_norm[l],
                        batch, seq, qkv_col=c_qkv // LANES, z_col=MOBA_WIDTH // GDN_WIDTH)

        xf = merge_project(xf, y_a, y_s, y_c, p16, bf(w_up_moba[l]), bf(ssm_w_glu[l]), bf(w_up_gdn[l]),
                           bf(w_out[l]), gate_col=(MOBA_WIDTH + GDN_WIDTH) // d)

        k_x = norm_matmul(memf, norm_mem[l], bf(xa_wk[l]), BF16)
        v_x = norm_matmul(memf, norm_mem[l], bf(xa_wv[l]), BF16)
        xf = cross_attention(xf, norm_xa[l], bf(xa_wq[l]), k_x, v_x, bf(xa_wo[l]), batch, seq)

        if l % 2 == 0:
            i = l // 2
            xf = ffn(xf, norm_ffn[l], bf(ffn_w_gate[i]), bf(ffn_w_up[i]), bf(ffn_w_down[i]))
        else:
            i = l // 2
            route, h_moe = moe_router(xf, norm_ffn[l], moe_w_router[i])
            xf = moe_ffn(xf, h_moe, route, bf(moe_w_gate[i]), bf(moe_w_up[i]), bf(moe_w_down[i]),
                         final_gain=norm_final if l == depth - 1 else None)

    if depth % 2 == 1:
        xf = rmsnorm(xf, norm_final)
    return xf.reshape(batch, seq, d)
```

```python
import functools
import math

import numpy as np
import jax
import jax.numpy as jnp
from jax import lax
from jax.experimental import pallas as pl
from jax.experimental.pallas import tpu as pltpu

F32 = jnp.float32
BF16 = jnp.bfloat16
HIGHEST = lax.Precision.HIGHEST

EPS = 1e-6
NEG_INF = -1e30
MOBA_HEADS = 8
MOBA_HEAD_DIM = 64
MOBA_WIDTH = MOBA_HEADS * MOBA_HEAD_DIM
MOBA_BLOCK = 256
MOBA_TOPK = 3
ROPE_THETA = 10000.0
SSM_WIDTH = 512
SSM_GROUP = 16
SSM_GROUPS = SSM_WIDTH // SSM_GROUP
SSM_STATE = 64
SSM_STATES = SSM_GROUPS * SSM_STATE
GDN_HEADS = 4
GDN_HEAD_DIM = 128
GDN_WIDTH = GDN_HEADS * GDN_HEAD_DIM
GDN_CONV = 4
GDN_CHUNK = 64
XA_HEADS = 4
XA_HEAD_DIM = 128
XA_WIDTH = XA_HEADS * XA_HEAD_DIM
N_EXPERTS = 8
TOP_K = 2

LANES = 128
SUBLANES = 8
VMEM_LIMIT = 56 * 1024 * 1024


def _cparams(*sem):
    return pltpu.CompilerParams(dimension_semantics=sem, vmem_limit_bytes=VMEM_LIMIT)


def _rms(x, g):
    return x * lax.rsqrt(jnp.mean(x * x, axis=-1, keepdims=True) + EPS) * g


def _sigmoid(x):
    return 1.0 / (1.0 + jnp.exp(-x))


def _silu(x):
    return x * _sigmoid(x)


def _dot(a, b):
    return jnp.dot(a, b, preferred_element_type=F32)


def _dot_nt(a, b):
    return lax.dot_general(a, b, (((1,), (1,)), ((), ())), preferred_element_type=F32)


def _dot_tn(a, b):
    return lax.dot_general(a, b, (((0,), (0,)), ((), ())), preferred_element_type=F32)


def _norm_matmul_body(x_ref, g_ref, w_ref, o_ref, h_ref):
    @pl.when(pl.program_id(1) == 0)
    def _():
        h_ref[...] = _rms(x_ref[...], g_ref[...]).astype(BF16)

    o_ref[...] = _dot(h_ref[...], w_ref[...]).astype(o_ref.dtype)


def norm_matmul(x, g, w, out_dtype, tm=512, tn=512):
    t, k = x.shape
    n = w.shape[1]
    tm, tn = min(tm, t), min(tn, n)
    return pl.pallas_call(
        _norm_matmul_body,
        grid=(t // tm, n // tn),
        in_specs=[pl.BlockSpec((tm, k), lambda i, j: (i, 0)),
                  pl.BlockSpec((1, k), lambda i, j: (0, 0)),
                  pl.BlockSpec((k, tn), lambda i, j: (0, j))],
        out_specs=pl.BlockSpec((tm, tn), lambda i, j: (i, j)),
        out_shape=jax.ShapeDtypeStruct((t, n), out_dtype),
        scratch_shapes=[pltpu.VMEM((tm, k), BF16)],
        compiler_params=_cparams("parallel", "arbitrary"),
        name="norm_matmul",
    )(x, g.reshape(1, k), w)


def _rope_body(pos_ref, inv_ref, cos_ref, sin_ref):
    ang = pos_ref[...] * inv_ref[...]
    lane = lax.broadcasted_iota(jnp.int32, ang.shape, 1)
    first_half = (lane % MOBA_HEAD_DIM) < (MOBA_HEAD_DIM // 2)
    cos_ref[...] = jnp.cos(ang)
    s = jnp.sin(ang)
    sin_ref[...] = jnp.where(first_half, -s, s)


def rope_tables(positions):
    t = positions.size
    half = MOBA_HEAD_DIM // 2
    inv = (1.0 / (np.float32(ROPE_THETA) ** (np.arange(0, MOBA_HEAD_DIM, 2, dtype=np.float32)
                                             / np.float32(MOBA_HEAD_DIM)))).astype(np.float32)
    inv_row = jnp.asarray(np.tile(inv, LANES // half).reshape(1, LANES))
    pos = positions.astype(F32).reshape(t, 1)
    ts = min(1024, t)
    return pl.pallas_call(
        _rope_body,
        grid=(t // ts,),
        in_specs=[pl.BlockSpec((ts, 1), lambda i: (i, 0)),
                  pl.BlockSpec((1, LANES), lambda i: (0, 0))],
        out_specs=[pl.BlockSpec((ts, LANES), lambda i: (i, 0))] * 2,
        out_shape=[jax.ShapeDtypeStruct((t, LANES), F32)] * 2,
        compiler_params=_cparams("parallel"),
        name="rope_tables",
    )(pos, inv_row)


def _rope_apply(x, cos, sin_signed):
    lane = lax.broadcasted_iota(jnp.int32, x.shape, 1)
    first_half = (lane % MOBA_HEAD_DIM) < (MOBA_HEAD_DIM // 2)
    partner = jnp.where(first_half, pltpu.roll(x, LANES - MOBA_HEAD_DIM // 2, 1),
                        pltpu.roll(x, MOBA_HEAD_DIM // 2, 1))
    return x * cos + partner * sin_signed


def _moba_body(q_ref, k_ref, v_ref, cq_ref, sq_ref, ck_ref, sk_ref, o_ref, ka_ref, va_ref, km_ref, s_ref,
               mt_ref, acc_ref, *, nb):
    blk = MOBA_BLOCK
    i = pl.program_id(2)

    @pl.when(i == 0)
    def _():
        km_ref[...] = jnp.zeros_like(km_ref)
        lane_b = lax.broadcasted_iota(jnp.int32, (blk, LANES), 1)

        def rope_blk(j, c):
            rows = pl.ds(pl.multiple_of(j * blk, blk), blk)
            kr = _rope_apply(k_ref[rows, :], ck_ref[rows, :], sk_ref[rows, :])
            ka_ref[rows, 0:LANES] = kr.astype(BF16)
            ka_ref[rows, LANES:2 * LANES] = jnp.where(lane_b == j, 1.0, 0.0).astype(BF16)
            km_ref[pl.ds(j, 1), :] = jnp.mean(kr, axis=0, keepdims=True)
            v = v_ref[rows, :].astype(F32)
            va_ref[0, rows, :] = jnp.where(lane_b < MOBA_HEAD_DIM, v,
                                           jnp.where(lane_b == MOBA_HEAD_DIM, 1.0, 0.0)).astype(BF16)
            va_ref[1, rows, :] = jnp.where(lane_b >= MOBA_HEAD_DIM, v,
                                           jnp.where(lane_b == 0, 1.0, 0.0)).astype(BF16)
            return c
        lax.fori_loop(0, nb, rope_blk, 0)

    lane = lax.broadcasted_iota(jnp.int32, (1, LANES), 1)
    head_a = lane < MOBA_HEAD_DIM
    q = _rope_apply(q_ref[...], cq_ref[...], sq_ref[...])
    scale = MOBA_HEAD_DIM ** -0.5
    km = km_ref[...]
    nbp = km_ref.shape[0]
    brow = lax.broadcasted_iota(jnp.int32, (nbp, blk), 0)
    own = pl.ds(pl.multiple_of(i * blk, blk), blk)
    r_idx = lax.broadcasted_iota(jnp.int32, (blk, blk), 0)
    c_idx = lax.broadcasted_iota(jnp.int32, (blk, blk), 1)
    causal = c_idx <= r_idx
    n_pairs = (i + 1) // 2
    heads = (0, 1)
    q_heads = [jnp.where(head_a, q, 0.0), jnp.where(head_a, 0.0, q)]
    gates = [lax.dot_general(km, qh, (((1,), (1,)), ((), ())), preferred_element_type=F32, precision=HIGHEST)
             for qh in q_heads]
    q_augs = []
    for h in heads:
        gate = jnp.where(brow < i, gates[h], NEG_INF)
        bias_rows = []
        for j in range(nb):
            gj = gate[j:j + 1, :]
            ahead = (gate > gj) | ((gate == gj) & (brow < j))
            rank = jnp.sum(jnp.where(ahead, 1.0, 0.0), axis=0, keepdims=True)
            sel = (rank < float(MOBA_TOPK)) & (j < i)
            bias_rows.append(jnp.where(sel, 0.0, NEG_INF))
        bias_t = jnp.concatenate(bias_rows + [jnp.zeros((LANES - nb, blk), F32)], axis=0)
        q_augs.append(jnp.concatenate([q_heads[h] * scale, bias_t.T], axis=1).astype(BF16))
    k_own = ka_ref[own, 0:LANES]
    s_owns = [_dot_nt(q_augs[h][:, :LANES], k_own) for h in heads]
    s_owns = [jnp.where(causal, s, NEG_INF) for s in s_owns]
    for h in heads:
        mt_ref[h] = jnp.maximum(s_owns[h][:, :LANES], s_owns[h][:, LANES:])

    def loop2(n, body):
        def two(k, c):
            body([2 * k, 2 * k + 1])
            return c
        lax.fori_loop(0, lax.shift_right_logical(n, 1), two, 0)

        @pl.when((n & 1) == 1)
        def _():
            body([n - 1])

    def pair_rows(jp):
        return pl.ds(pl.multiple_of(jp * (2 * blk), 2 * blk), 2 * blk)

    def scores(jps):
        k2s = [ka_ref[pair_rows(jp), :] for jp in jps]
        s2s = [[_dot_nt(q_augs[h], k2) for h in heads] for k2 in k2s]
        for jp, s2h in zip(jps, s2s):
            for h in heads:
                s_ref[h, jp] = s2h[h]
        for h in heads:
            m_t = mt_ref[h]
            for s2h in s2s:
                for col in range(0, 2 * blk, LANES):
                    m_t = jnp.maximum(m_t, s2h[h][:, col:col + LANES])
            mt_ref[h] = m_t
    loop2(n_pairs, scores)

    ms = [jnp.max(mt_ref[h], axis=-1, keepdims=True) for h in heads]
    p_owns = [jnp.exp(s_owns[h] - ms[h]).astype(BF16) for h in heads]
    for h in heads:
        acc_ref[h] = _dot(p_owns[h], va_ref[h, own, :])

    def values(jps):
        ps = [[jnp.exp(s_ref[h, jp] - ms[h]).astype(BF16) for h in heads] for jp in jps]
        pvs = [[_dot(p[h], va_ref[h, pair_rows(jp), :]) for h in heads] for jp, p in zip(jps, ps)]
        for h in heads:
            tot = pvs[0][h]
            for pv in pvs[1:]:
                tot = tot + pv[h]
            acc_ref[h] += tot
    loop2(n_pairs, values)
    acc_a, acc_b = acc_ref[0], acc_ref[1]
    out_a = acc_a / acc_a[:, MOBA_HEAD_DIM:MOBA_HEAD_DIM + 1]
    out_b = acc_b / acc_b[:, 0:1]
    o_ref[...] = jnp.where(head_a, out_a, out_b).astype(o_ref.dtype)


def moba_attention(qk, v, cos, sin, batch, seq, q_col=0, k_col=MOBA_WIDTH // LANES, v_col=0):
    nb = seq // MOBA_BLOCK
    pairs = MOBA_WIDTH // LANES
    blk = MOBA_BLOCK
    t = batch * seq
    return pl.pallas_call(
        functools.partial(_moba_body, nb=nb),
        grid=(batch, pairs, nb),
        in_specs=[pl.BlockSpec((blk, LANES), lambda b, p, i: (b * nb + i, q_col + p)),
                  pl.BlockSpec((seq, LANES), lambda b, p, i: (b, k_col + p)),
                  pl.BlockSpec((seq, LANES), lambda b, p, i: (b, v_col + p)),
                  pl.BlockSpec((blk, LANES), lambda b, p, i: (b * nb + i, 0)),
                  pl.BlockSpec((blk, LANES), lambda b, p, i: (b * nb + i, 0)),
                  pl.BlockSpec((seq, LANES), lambda b, p, i: (b, 0)),
                  pl.BlockSpec((seq, LANES), lambda b, p, i: (b, 0))],
        out_specs=pl.BlockSpec((blk, LANES), lambda b, p, i: (b * nb + i, p)),
        out_shape=jax.ShapeDtypeStruct((t, MOBA_WIDTH), BF16),
        scratch_shapes=[pltpu.VMEM((seq, 2 * LANES), BF16),
                        pltpu.VMEM((2, seq, LANES), BF16),
                        pltpu.VMEM((-(-nb // SUBLANES) * SUBLANES, LANES), F32),
                        pltpu.VMEM((2, nb // 2, blk, 2 * blk), F32),
                        pltpu.VMEM((2, blk, LANES), F32),
                        pltpu.VMEM((2, blk, LANES), F32)],
        compiler_params=_cparams("parallel", "parallel", "arbitrary"),
        name="moba_attention",
    )(qk, qk, v, cos, sin, cos, sin)


def _s5_disc_body(are_ref, aim_ref, ldt_ref, bre_ref, bim_ref, lre_ref, lim_ref, bbre_ref, bbim_ref):
    a_re, a_im = are_ref[...], aim_ref[...]
    dt = jnp.exp(ldt_ref[...])
    mag = jnp.exp(a_re * dt)
    l_re = mag * jnp.cos(a_im * dt)
    l_im = mag * jnp.sin(a_im * dt)
    lre_ref[...] = l_re
    lim_ref[...] = l_im
    x, y = l_re - 1.0, l_im
    den = a_re * a_re + a_im * a_im
    c_re = (x * a_re + y * a_im) / den
    c_im = (y * a_re - x * a_im) / den
    b_re, b_im = bre_ref[...], bim_ref[...]
    bbre_ref[...] = c_re * b_re - c_im * b_im
    bbim_ref[...] = c_re * b_im + c_im * b_re


def s5_discretise(a_re, a_im, log_dt, b_re, b_im):
    g, p = a_re.shape
    n = b_re.shape[-1]
    col = lambda a: a.reshape(g * p, 1)
    ldt = jnp.broadcast_to(log_dt[:, None], (g, p))
    outs = pl.pallas_call(
        _s5_disc_body,
        out_shape=[jax.ShapeDtypeStruct((g * p, 1), F32)] * 2 + [jax.ShapeDtypeStruct((g * p, n), F32)] * 2,
        name="s5_discretise",
    )(col(a_re), col(a_im), col(ldt), b_re.reshape(g * p, n), b_im.reshape(g * p, n))
    l_re, l_im, bb_re, bb_im = outs
    return l_re.reshape(g, p), l_im.reshape(g, p), bb_re.reshape(g, p, n), bb_im.reshape(g, p, n)


def _block_diag(blocks):
    g, r, c = blocks.shape
    eye = jnp.eye(g, dtype=blocks.dtype)
    return (blocks[:, :, None, :] * eye[:, None, :, None]).reshape(g * r, g * c)


def _s5_body(u_ref, bre_ref, bim_ref, cre_ref, cim_ref, lre_ref, lim_ref, d_ref, y_ref, h_ref, hb_ref, *, steps):
    rows = SUBLANES
    ns = SSM_STATES
    nq = SSM_WIDTH // LANES
    sq = ns // nq

    @pl.when(pl.program_id(0) == 0)
    def _():
        h_ref[...] = jnp.zeros_like(h_ref)

    u = u_ref[...]
    u16 = u.astype(BF16)
    for q in range(nq):
        uq = u16[:, q * LANES:(q + 1) * LANES]
        hb_ref[:, q * sq:(q + 1) * sq] = _dot(uq, bre_ref[q])
        hb_ref[:, ns + q * sq:ns + (q + 1) * sq] = _dot(uq, bim_ref[q])
    l_re, l_im = lre_ref[...], lim_ref[...]

    def step(t, carry):
        h_re, h_im = carry
        r = pl.ds(pl.multiple_of(t * rows, rows), rows)
        n_re = l_re * h_re - l_im * h_im + hb_ref[r, 0:ns]
        n_im = l_re * h_im + l_im * h_re + hb_ref[r, ns:2 * ns]
        hb_ref[r, 0:ns] = n_re
        hb_ref[r, ns:2 * ns] = n_im
        return n_re, n_im

    h_re, h_im = lax.fori_loop(0, steps, step, (h_ref[:, 0:ns], h_ref[:, ns:2 * ns]))
    h_ref[:, 0:ns] = h_re
    h_ref[:, ns:2 * ns] = h_im
    ys = []
    for q in range(nq):
        ys.append(_dot(hb_ref[:, q * sq:(q + 1) * sq].astype(BF16), cre_ref[q])
                  + _dot(hb_ref[:, ns + q * sq:ns + (q + 1) * sq].astype(BF16), cim_ref[q]))
    y = jnp.concatenate(ys, axis=1) + d_ref[...] * u
    y_ref[...] = jax.nn.gelu(y).astype(y_ref.dtype)


def s5_block_maps(bb_re, bb_im, c_re, c_im):
    nq = SSM_WIDTH // LANES
    gq = SSM_GROUPS // nq
    per_q = lambda a: jnp.stack([_block_diag(a[q * gq:(q + 1) * gq].transpose(0, 2, 1)) for q in range(nq)])
    return (per_q(bb_re).astype(BF16), per_q(bb_im).astype(BF16),
            per_q(c_re).astype(BF16), per_q(-c_im).astype(BF16))


def s5_gelu(u_tm, maps, l_re, l_im, d_skip, steps=64):
    rows = u_tm.shape[0]
    blk = steps * SUBLANES
    ns2 = 2 * SSM_STATES
    full3 = lambda a: pl.BlockSpec(a.shape, lambda c: (0, 0, 0))
    return pl.pallas_call(
        functools.partial(_s5_body, steps=steps),
        grid=(rows // blk,),
        in_specs=[pl.BlockSpec((blk, SSM_WIDTH), lambda c: (c, 0)),
                  full3(maps[0]), full3(maps[1]), full3(maps[2]), full3(maps[3]),
                  pl.BlockSpec((1, SSM_STATES), lambda c: (0, 0)),
                  pl.BlockSpec((1, SSM_STATES), lambda c: (0, 0)),
                  pl.BlockSpec((1, SSM_WIDTH), lambda c: (0, 0))],
        out_specs=pl.BlockSpec((blk, SSM_WIDTH), lambda c: (c, 0)),
        out_shape=jax.ShapeDtypeStruct((rows, SSM_WIDTH), BF16),
        scratch_shapes=[pltpu.VMEM((SUBLANES, ns2), F32),
                        pltpu.VMEM((blk, ns2), F32)],
        compiler_params=_cparams("arbitrary"),
        name="s5_scan",
    )(u_tm, *maps, l_re.reshape(1, -1), l_im.reshape(1, -1), d_skip.reshape(1, -1))


GDN_GROUP = 16


def _gdn_prep_body(q_ref, k_ref, v_ref, a_ref, b_ref, wq_ref, wk_ref, wv_ref, alog_ref, dtb_ref,
                   u_ref, w_ref, qd_ref, kd_ref, qk_ref, gl_ref, qs_ref, ks_ref, vs_ref,
                   *, seq):
    L = GDN_CHUNK
    G = min(GDN_GROUP, seq // L)
    dk = GDN_HEAD_DIM
    nc = seq // L
    hd = pl.program_id(1)
    cb = 256
    halo = SUBLANES

    def conv_blk(rb, c):
        base = pl.multiple_of(rb * cb, cb)
        prev = pl.multiple_of(jnp.maximum(base - halo, 0), halo)
        for src, wref, dst, norm, mul in ((q_ref, wq_ref, qs_ref, True, dk ** -0.5),
                                          (k_ref, wk_ref, ks_ref, True, 1.0),
                                          (v_ref, wv_ref, vs_ref, False, 1.0)):
            cur = src[pl.ds(base, cb), :]
            head = jnp.where(rb > 0, src[pl.ds(prev, halo), :], 0.0)
            ext = jnp.concatenate([head, cur], axis=0)
            wts = wref[...]
            y = ext[halo:halo + cb] * wts[GDN_CONV - 1:GDN_CONV]
            for tap in range(1, GDN_CONV):
                y = y + ext[halo - tap:halo - tap + cb] * wts[GDN_CONV - 1 - tap:GDN_CONV - tap]
            y = _silu(y)
            if norm:
                y = y * lax.rsqrt(jnp.sum(y * y, axis=-1, keepdims=True) + EPS) * mul
            dst[pl.ds(base, cb), :] = y
        return c
    lax.fori_loop(0, seq // cb, conv_blk, 0)

    gl_rows = G * L
    ri = lax.broadcasted_iota(jnp.int32, (gl_rows, L), 0) % L
    ci = lax.broadcasted_iota(jnp.int32, (gl_rows, L), 1)
    tril = ci <= ri
    strict = ci < ri
    eye = ci == ri
    ri3 = lax.broadcasted_iota(jnp.int32, (G, L, L), 1)
    ci3 = lax.broadcasted_iota(jnp.int32, (G, L, L), 2)
    strict3 = ci3 < ri3
    eye_f = jnp.where(ci3 == ri3, 1.0, 0.0)
    r2 = lax.broadcasted_iota(jnp.int32, (L, L), 0)
    c2 = lax.broadcasted_iota(jnp.int32, (L, L), 1)
    upper_f = jnp.where(r2 <= c2, 1.0, 0.0)
    a_coef = -jnp.exp(jnp.full((1, L), alog_ref[hd], F32))
    dt_bias = dtb_ref[hd]
    hp = functools.partial(jnp.dot, preferred_element_type=F32, precision=HIGHEST)

    def bmm(a, b):
        return jnp.einsum('gij,gjk->gik', a, b, preferred_element_type=F32)

    def bmm_nt(a, b):
        return jnp.einsum('gid,gjd->gij', a, b, preferred_element_type=F32)

    def to_col(rows_b):
        return jnp.sum(jnp.where(eye, rows_b, 0.0), axis=1, keepdims=True)

    def prep(cg, carry):
        c0 = pl.multiple_of(cg * G, G)
        rows = pl.ds(pl.multiple_of(cg * gl_rows, gl_rows), gl_rows)
        qc, kc, vc = qs_ref[rows, :], ks_ref[rows, :], vs_ref[rows, :]
        z = a_ref[pl.ds(c0, G), :] + dt_bias
        softplus = jnp.maximum(z, 0.0) + jnp.log(1.0 + jnp.exp(-jnp.abs(z)))
        g_rows = hp(a_coef * softplus, upper_f)
        beta_rows = _sigmoid(b_ref[pl.ds(c0, G), :])
        spread = lambda r: jnp.concatenate([jnp.broadcast_to(r[c:c + 1, :], (L, L)) for c in range(G)], axis=0)
        g_rb = spread(g_rows)
        g_col = to_col(g_rb)
        beta_col = to_col(spread(beta_rows))
        g_last = g_rb[:, L - 1:L]
        decay = jnp.where(tril, jnp.exp(jnp.where(tril, g_col - g_rb, 0.0)), 0.0)
        k_beta = kc * beta_col
        v_beta = vc * beta_col
        to3 = lambda a: a.reshape(G, L, a.shape[-1])
        kb16, k16, q16 = to3(k_beta.astype(BF16)), to3(kc.astype(BF16)), to3(qc.astype(BF16))
        kk_qk = bmm_nt(jnp.concatenate([kb16, q16], axis=1), k16)
        decay3 = to3(decay)
        a_low = jnp.where(strict3, kk_qk[:, :L] * decay3, 0.0)
        qk = kk_qk[:, L:] * decay3
        t_inv = eye_f - a_low
        a16 = a_low.astype(BF16)
        pw = bmm(a16, a16)
        span = 2
        while 2 * span < L:
            pw16 = pw.astype(BF16)
            both = bmm(jnp.concatenate([t_inv.astype(BF16), pw16], axis=1), pw16)
            t_inv = t_inv + both[:, :L]
            pw = both[:, L:]
            span *= 2
        t_inv = t_inv + bmm(t_inv.astype(BF16), pw.astype(BF16))
        rhs = jnp.concatenate([v_beta.astype(BF16), (k_beta * jnp.exp(g_col)).astype(BF16)], axis=1)
        uw = bmm(t_inv.astype(BF16), to3(rhs)).reshape(gl_rows, 2 * dk)
        u_ref[rows, :] = uw[:, :dk]
        w_ref[rows, :] = uw[:, dk:].astype(BF16)
        qd_ref[rows, :] = (qc * jnp.exp(g_col)).astype(BF16)
        k_dec = to3(kc * jnp.exp(g_last - g_col))
        kd_ref[pl.ds(c0, G)] = jnp.swapaxes(k_dec, 1, 2).astype(BF16)
        qk_ref[pl.ds(c0, G)] = qk.astype(BF16)
        gl_ref[pl.ds(c0, G), :] = jnp.broadcast_to(jnp.exp(g_rows[:, L - 1:L]), (G, LANES))
        return carry
    lax.fori_loop(0, nc // G, prep, 0)


def _gdn_scan_body(u_ref, w_ref, qd_ref, kd_ref, qk_ref, gl_ref, z_ref, gn_ref, o_ref, *st_refs, chunks):
    L = GDN_CHUNK
    d = GDN_HEAD_DIM
    nrow = u_ref.shape[0]

    @pl.when(pl.program_id(1) == 0)
    def _():
        for st in st_refs:
            st[...] = jnp.zeros_like(st)

    gn = gn_ref[...]

    def chunk(c, carry):
        rows = pl.ds(pl.multiple_of(c * L, L), L)
        chains = [(r, h) for r in range(nrow) for h in range(GDN_HEADS)]
        states = [st_refs[r * GDN_HEADS + h][...] for r, h in chains]
        firsts = []
        for (r, h), s in zip(chains, states):
            cols = slice(h * d, (h + 1) * d)
            firsts.append(_dot(jnp.concatenate([w_ref[r, rows, cols], qd_ref[r, rows, cols]], axis=0),
                               s.astype(BF16)))
        vns = []
        for (r, h), both in zip(chains, firsts):
            cols = slice(h * d, (h + 1) * d)
            vns.append((u_ref[r, rows, cols] - both[:L]).astype(BF16))
        outs = [both[L:] + _dot(qk_ref[r, h, c], vn) for (r, h), both, vn in zip(chains, firsts, vns)]
        for (r, h), s, vn in zip(chains, states, vns):
            st_refs[r * GDN_HEADS + h][...] = s * gl_ref[r, h, pl.ds(c, 1), :] + _dot(kd_ref[r, h, c], vn)
        for (r, h), o in zip(chains, outs):
            cols = slice(h * d, (h + 1) * d)
            o_ref[r, rows, cols] = (_rms(o, gn) * _silu(z_ref[r, rows, cols].astype(F32))).astype(o_ref.dtype)
        return carry
    lax.fori_loop(0, chunks, chunk, 0)


def gdn_mixer(qkv, z, a_rows, b_rows, conv_w, a_log, dt_bias, gnorm, batch, seq, ts=1024, qkv_col=0, z_col=0):
    t = batch * seq
    hds = GDN_HEADS
    nc = seq // GDN_CHUNK
    L = GDN_CHUNK
    d = GDN_HEAD_DIM
    seq_spec = lambda off: pl.BlockSpec((seq, d), lambda b, h: (b, qkv_col + off + h))
    w_spec = lambda off: pl.BlockSpec((GDN_CONV, d), lambda b, h: (0, off + h))
    row_spec = pl.BlockSpec((None, None, nc, L), lambda b, h: (b, h, 0, 0))
    smem = pl.BlockSpec(memory_space=pltpu.SMEM)
    head_out = pl.BlockSpec((seq, d), lambda b, h: (b, h))
    u, w, qd, kd, qk, gl = pl.pallas_call(
        functools.partial(_gdn_prep_body, seq=seq),
        grid=(batch, hds),
        in_specs=[seq_spec(0), seq_spec(hds), seq_spec(2 * hds), row_spec, row_spec,
                  w_spec(0), w_spec(hds), w_spec(2 * hds), smem, smem],
        out_specs=[head_out, head_out, head_out,
                   pl.BlockSpec((None, None, nc, d, L), lambda b, h: (b, h, 0, 0, 0)),
                   pl.BlockSpec((None, None, nc, L, L), lambda b, h: (b, h, 0, 0, 0)),
                   pl.BlockSpec((None, None, nc, LANES), lambda b, h: (b, h, 0, 0))],
        out_shape=[jax.ShapeDtypeStruct((t, GDN_WIDTH), F32),
                   jax.ShapeDtypeStruct((t, GDN_WIDTH), BF16),
                   jax.ShapeDtypeStruct((t, GDN_WIDTH), BF16),
                   jax.ShapeDtypeStruct((batch, hds, nc, d, L), BF16),
                   jax.ShapeDtypeStruct((batch, hds, nc, L, L), BF16),
                   jax.ShapeDtypeStruct((batch, hds, nc, LANES), F32)],
        scratch_shapes=[pltpu.VMEM((seq, d), F32), pltpu.VMEM((seq, d), F32), pltpu.VMEM((seq, d), F32)],
        compiler_params=_cparams("parallel", "parallel"),
        name="gdn_prep",
    )(qkv, qkv, qkv, a_rows, b_rows, conv_w, conv_w, conv_w, a_log, dt_bias)

    ts = min(ts, seq)
    spb = seq // ts
    cps = ts // L
    nrow = 2 if batch % 2 == 0 else 1
    by_row = lambda a: a.reshape(batch, seq, a.shape[-1])
    tok = pl.BlockSpec((nrow, ts, GDN_WIDTH), lambda b, s: (b, s, 0))
    out = pl.pallas_call(
        functools.partial(_gdn_scan_body, chunks=cps),
        grid=(batch // nrow, spb),
        in_specs=[tok, tok, tok,
                  pl.BlockSpec((nrow, hds, cps, d, L), lambda b, s: (b, 0, s, 0, 0)),
                  pl.BlockSpec((nrow, hds, cps, L, L), lambda b, s: (b, 0, s, 0, 0)),
                  pl.BlockSpec((nrow, hds, cps, LANES), lambda b, s: (b, 0, s, 0)),
                  pl.BlockSpec((nrow, ts, GDN_WIDTH), lambda b, s: (b, s, z_col)),
                  pl.BlockSpec((1, d), lambda b, s: (0, 0))],
        out_specs=tok,
        out_shape=jax.ShapeDtypeStruct((batch, seq, GDN_WIDTH), BF16),
        scratch_shapes=[pltpu.VMEM((d, d), F32)] * (nrow * hds),
        compiler_params=_cparams("parallel", "arbitrary"),
        name="gdn_scan",
    )(by_row(u), by_row(w), by_row(qd), kd, qk, gl, by_row(z), gnorm.reshape(1, d))
    return out.reshape(t, GDN_WIDTH)


def _merge_body(x_ref, ya_ref, ys_ref, yc_ref, ga_ref, gb_ref, gc_ref, wa_ref, wg_ref, wc_ref, wo_ref, o_ref):
    d = x_ref.shape[1]
    y_a = _dot(ya_ref[...], wa_ref[...])
    glu = _dot(ys_ref[...], wg_ref[...])
    y_b = glu[:, :d] * _sigmoid(glu[:, d:])
    y_c = _dot(yc_ref[...], wc_ref[...])
    gate = lambda r: _sigmoid(r[...].astype(F32))
    merged = gate(ga_ref) * y_a + gate(gb_ref) * y_b + gate(gc_ref) * y_c
    o_ref[...] = x_ref[...] + _dot(merged.astype(BF16), wo_ref[...])


def merge_project(x, y_a, y_s, y_c, gates, w_a, w_glu, w_c, w_o, tm=512, gate_col=0):
    t, d = x.shape
    row = lambda n: pl.BlockSpec((tm, n), lambda i: (i, 0))
    gate = lambda k: pl.BlockSpec((tm, d), lambda i: (i, gate_col + k))
    full = lambda a: pl.BlockSpec(a.shape, lambda i: (0, 0))
    return pl.pallas_call(
        _merge_body,
        grid=(t // tm,),
        in_specs=[row(d), row(y_a.shape[1]), row(y_s.shape[1]), row(y_c.shape[1]), gate(0), gate(1), gate(2),
                  full(w_a), full(w_glu), full(w_c), full(w_o)],
        out_specs=row(d),
        out_shape=jax.ShapeDtypeStruct((t, d), F32),
        compiler_params=_cparams("parallel"),
        name="merge_project",
    )(x, y_a, y_s, y_c, gates, gates, gates, w_a, w_glu, w_c, w_o)


def _xattn_body(x_ref, g_ref, wq_ref, k_ref, v_ref, wo_ref, o_ref):
    x = x_ref[...]
    h = _rms(x, g_ref[...]).astype(BF16)
    q = _dot(h, wq_ref[...]).astype(BF16)
    outs = []
    for hd in range(XA_HEADS):
        cols = slice(hd * XA_HEAD_DIM, (hd + 1) * XA_HEAD_DIM)
        s = _dot_nt(q[:, cols], k_ref[:, cols]) * (XA_HEAD_DIM ** -0.5)
        p = jnp.exp(s - jnp.max(s, axis=-1, keepdims=True))
        p = p / jnp.sum(p, axis=-1, keepdims=True)
        outs.append(_dot(p.astype(BF16), v_ref[:, cols]))
    o = jnp.concatenate(outs, axis=1).astype(BF16)
    o_ref[...] = x + _dot(o, wo_ref[...])


def cross_attention(x, g, wq, k, v, wo, batch, seq, tm=512):
    t, d = x.shape
    m = k.shape[0] // batch
    spb = seq // tm
    full = lambda a: pl.BlockSpec(a.shape, lambda i: (0, 0))
    return pl.pallas_call(
        _xattn_body,
        grid=(t // tm,),
        in_specs=[pl.BlockSpec((tm, d), lambda i: (i, 0)),
                  pl.BlockSpec((1, d), lambda i: (0, 0)),
                  full(wq),
                  pl.BlockSpec((m, XA_WIDTH), lambda i: (i // spb, 0)),
                  pl.BlockSpec((m, XA_WIDTH), lambda i: (i // spb, 0)),
                  full(wo)],
        out_specs=pl.BlockSpec((tm, d), lambda i: (i, 0)),
        out_shape=jax.ShapeDtypeStruct((t, d), F32),
        compiler_params=_cparams("parallel"),
        name="cross_attention",
    )(x, g.reshape(1, d), wq, k, v, wo)


def _ffn_body(x_ref, g_ref, wg_ref, wu_ref, wd_ref, o_ref, h_ref, acc_ref):
    f = pl.program_id(1)

    @pl.when(f == 0)
    def _():
        h_ref[...] = _rms(x_ref[...], g_ref[...]).astype(BF16)
        acc_ref[...] = jnp.zeros_like(acc_ref)

    h = h_ref[...]
    act = (_silu(_dot(h, wg_ref[...])) * _dot(h, wu_ref[...])).astype(BF16)
    acc_ref[...] += _dot(act, wd_ref[...])

    @pl.when(f == pl.num_programs(1) - 1)
    def _():
        o_ref[...] = x_ref[...] + acc_ref[...]


def ffn(x, g, w_gate, w_up, w_down, tm=512, tf=1408):
    t, d = x.shape
    f = w_gate.shape[1]
    return pl.pallas_call(
        _ffn_body,
        grid=(t // tm, f // tf),
        in_specs=[pl.BlockSpec((tm, d), lambda i, k: (i, 0)),
                  pl.BlockSpec((1, d), lambda i, k: (0, 0)),
                  pl.BlockSpec((d, tf), lambda i, k: (0, k)),
                  pl.BlockSpec((d, tf), lambda i, k: (0, k)),
                  pl.BlockSpec((tf, d), lambda i, k: (k, 0))],
        out_specs=pl.BlockSpec((tm, d), lambda i, k: (i, 0)),
        out_shape=jax.ShapeDtypeStruct((t, d), F32),
        scratch_shapes=[pltpu.VMEM((tm, d), BF16), pltpu.VMEM((tm, d), F32)],
        compiler_params=_cparams("parallel", "arbitrary"),
        name="ffn",
    )(x, g.reshape(1, d), w_gate, w_up, w_down)


def _router_body(x_ref, g_ref, w_ref, r_ref, h_ref):
    h = _rms(x_ref[...], g_ref[...])
    h_ref[...] = h.astype(BF16)
    logits = jnp.dot(h, w_ref[...], preferred_element_type=F32, precision=HIGHEST)
    lane = lax.broadcasted_iota(jnp.int32, logits.shape, 1).astype(F32)
    logits = jnp.where(lane < N_EXPERTS, logits, -jnp.inf)
    v1 = jnp.max(logits, axis=-1, keepdims=True)
    i1 = jnp.min(jnp.where(logits == v1, lane, float(LANES)), axis=-1, keepdims=True)
    rest = jnp.where(lane == i1, -jnp.inf, logits)
    v2 = jnp.max(rest, axis=-1, keepdims=True)
    i2 = jnp.min(jnp.where(rest == v2, lane, float(LANES)), axis=-1, keepdims=True)
    e2 = jnp.exp(v2 - v1)
    w1 = 1.0 / (1.0 + e2)
    w2 = e2 / (1.0 + e2)
    r_ref[...] = (jnp.where(lane == 0.0, i1, 0.0) + jnp.where(lane == 1.0, i2, 0.0)
                  + jnp.where(lane == 2.0, w1, 0.0) + jnp.where(lane == 3.0, w2, 0.0))


def moe_router(x, g, w_router, tm=512):
    t, d = x.shape
    w = jnp.pad(w_router, ((0, 0), (0, LANES - w_router.shape[1])))
    return pl.pallas_call(
        _router_body,
        grid=(t // tm,),
        in_specs=[pl.BlockSpec((tm, d), lambda i: (i, 0)),
                  pl.BlockSpec((1, d), lambda i: (0, 0)),
                  pl.BlockSpec((d, LANES), lambda i: (0, 0))],
        out_specs=[pl.BlockSpec((tm, LANES), lambda i: (i, 0)), pl.BlockSpec((tm, d), lambda i: (i, 0))],
        out_shape=[jax.ShapeDtypeStruct((t, LANES), F32), jax.ShapeDtypeStruct((t, d), BF16)],
        compiler_params=_cparams("parallel"),
        name="moe_router",
    )(x, g.reshape(1, d), w)


MOE_TILE = 512
MOE_GTILE = 256
MOE_CHUNK = 512
MOE_RBLK = 256


def _count_le(sorted_vals, queries):
    return jnp.sum((sorted_vals[None, :] <= queries[:, None]).astype(jnp.int32), axis=1)


def _moe_plan(route, t):
    e = N_EXPERTS
    tm, ck, rb = MOE_TILE, MOE_CHUNK, MOE_RBLK
    n_tiles = (TOP_K * t) // tm + e
    p = n_tiles * tm
    n_chunks = t // ck
    i32 = jnp.int32
    e1 = route[:, 0].astype(i32)
    e2 = route[:, 1].astype(i32)
    oh1 = jax.nn.one_hot(e1, e, dtype=i32)
    oh2 = jax.nn.one_hot(e2, e, dtype=i32)
    cnt = oh1 + oh2
    incl = jnp.cumsum(cnt, axis=0)
    excl = incl - cnt
    n_e = incl[-1]
    g_e = ((n_e + tm - 1) // tm) * tm
    off_end = jnp.cumsum(g_e)
    off = off_end - g_e
    pos_all = off[None, :] + excl
    pos1 = jnp.sum(pos_all * oh1, axis=1)
    pos2 = jnp.sum(pos_all * oh2, axis=1)
    tile_expert = jnp.minimum(_count_le(off_end, jnp.arange(n_tiles, dtype=i32) * tm), e - 1)
    gm = MOE_GTILE
    n_gt = p // gm
    gt_start = jnp.arange(n_gt, dtype=i32) * gm
    sel = jax.nn.one_hot(jnp.minimum(_count_le(off_end, gt_start), e - 1), e, dtype=i32)
    r0 = gt_start - jnp.sum(sel * off[None, :], axis=1)
    n_sel = jnp.sum(sel * n_e[None, :], axis=1)
    has_rows = r0 < n_sel
    r1 = jnp.minimum(r0 + gm, n_sel) - 1
    ends = incl[ck - 1::ck]
    ends_sel = jnp.sum(ends[None, :, :] * sel[:, None, :], axis=2)
    c_lo = jnp.where(has_rows, jnp.sum((ends_sel <= r0[:, None]).astype(i32), axis=1), 0)
    c_hi = jnp.where(has_rows, jnp.sum((ends_sel <= r1[:, None]).astype(i32), axis=1), 0)
    n_it = c_hi - c_lo + 1
    it_end = jnp.cumsum(n_it)
    it_start = it_end - n_it
    wmax = n_gt + e * n_chunks
    w_idx = jnp.arange(wmax, dtype=i32)
    g_tile = jnp.minimum(_count_le(it_end, w_idx), n_gt - 1)
    g_valid = w_idx < it_end[-1]
    g_chunk = jnp.where(g_valid, c_lo[g_tile] + (w_idx - it_start[g_tile]), c_hi[n_gt - 1])
    g_first = (g_valid & (w_idx == it_start[g_tile])).astype(i32)
    g_last = (g_valid & (w_idx == it_end[g_tile] - 1)).astype(i32)
    gather = (g_tile, g_chunk.astype(i32), g_first, g_last, g_valid.astype(i32))
    lo = off[None, :] + excl[::ck]
    hi = off[None, :] + ends
    nblk = jnp.where(hi > lo, (hi - 1) // rb - lo // rb + 1, 0).reshape(-1)
    blk_lo = (lo // rb).reshape(-1)
    cb_end = jnp.cumsum(nblk)
    cb_start = cb_end - nblk
    cmax = p // rb + e * n_chunks
    c_idx = jnp.arange(cmax, dtype=i32)
    pair = jnp.minimum(_count_le(cb_end, c_idx), nblk.shape[0] - 1)
    c_valid = c_idx < cb_end[-1]
    c_tile = jnp.where(c_valid, pair // e, n_chunks - 1)
    c_blk = jnp.where(c_valid, blk_lo[pair] + (c_idx - cb_start[pair]), 0)
    per_tile = jnp.sum(nblk.reshape(n_chunks, e), axis=1)
    t_end = jnp.cumsum(per_tile)
    c_first = (c_valid & (c_idx == (t_end - per_tile)[c_tile])).astype(i32)
    c_last = (c_valid & (c_idx == t_end[c_tile] - 1)).astype(i32)
    combine = (c_tile.astype(i32), c_blk.astype(i32), c_first, c_last, c_valid.astype(i32))
    return dict(n_tiles=n_tiles, p=p, tile_expert=tile_expert.astype(i32), pos1=pos1, pos2=pos2,
                gather=gather, combine=combine)


def _moe_gather_body(tile_ref, chunk_ref, first_ref, last_ref, valid_ref, meta_ref, h_ref,
                     o_ref, wr_ref, acc_ref, wacc_ref):
    w = pl.program_id(0)

    @pl.when(first_ref[w] == 1)
    def _():
        acc_ref[...] = jnp.zeros_like(acc_ref)
        wacc_ref[...] = jnp.zeros_like(wacc_ref)

    @pl.when(valid_ref[w] == 1)
    def _():
        row = (lax.broadcasted_iota(jnp.int32, (MOE_GTILE, MOE_CHUNK), 0) + tile_ref[w] * MOE_GTILE).astype(F32)
        hit1 = row == meta_ref[0:1, :]
        hit2 = row == meta_ref[1:2, :]
        onehot = jnp.where(hit1, 1.0, jnp.where(hit2, 1.0, 0.0)).astype(BF16)
        acc_ref[...] += _dot(onehot, h_ref[...])
        wts = jnp.where(hit1, meta_ref[2:3, :], jnp.where(hit2, meta_ref[3:4, :], 0.0))
        wacc_ref[...] += jnp.sum(wts, axis=1, keepdims=True)

    @pl.when(last_ref[w] == 1)
    def _():
        o_ref[...] = acc_ref[...].astype(o_ref.dtype)
        wr_ref[...] = jnp.broadcast_to(wacc_ref[...], wr_ref.shape)


def _moe_ffn_body(exp_ref, x_ref, wr_ref, wg_ref, wu_ref, wd_ref, o_ref, acc_ref):
    f = pl.program_id(1)

    @pl.when(f == 0)
    def _():
        acc_ref[...] = jnp.zeros_like(acc_ref)

    x = x_ref[...]
    act = (_silu(_dot(x, wg_ref[...])) * _dot(x, wu_ref[...])).astype(BF16)
    acc_ref[...] += _dot(act, wd_ref[...])

    @pl.when(f == pl.num_programs(1) - 1)
    def _():
        o_ref[...] = (acc_ref[...] * wr_ref[:, 0:1]).astype(o_ref.dtype)


def _moe_combine_body(tile_ref, blk_ref, first_ref, last_ref, valid_ref, x_ref, pos_ref, y_ref, gf_ref,
                      o_ref, acc_ref, *, final_norm):
    w = pl.program_id(0)

    @pl.when(first_ref[w] == 1)
    def _():
        acc_ref[...] = x_ref[...]

    @pl.when(valid_ref[w] == 1)
    def _():
        base = blk_ref[w] * MOE_RBLK
        col = lax.broadcasted_iota(jnp.int32, (MOE_CHUNK, MOE_RBLK), 1) + base
        reps = MOE_RBLK // LANES
        p1 = jnp.concatenate([pos_ref[:, 0:LANES]] * reps, axis=1)
        p2 = jnp.concatenate([pos_ref[:, LANES:2 * LANES]] * reps, axis=1)
        hit = jnp.where(col == p1, 1.0, jnp.where(col == p2, 1.0, 0.0))
        acc_ref[...] += _dot(hit.astype(BF16), y_ref[...])

    @pl.when(last_ref[w] == 1)
    def _():
        o_ref[...] = _rms(acc_ref[...], gf_ref[...]) if final_norm else acc_ref[...]


def moe_ffn(x, h, route, w_gate, w_up, w_down, final_gain=None, tf=1408):
    t, d = x.shape
    f = w_gate.shape[2]
    tm, ck, rb = MOE_TILE, MOE_CHUNK, MOE_RBLK
    plan = _moe_plan(route, t)
    n_tiles, p = plan["n_tiles"], plan["p"]
    assert p < 2 ** 24, "row positions are carried exactly in f32"
    meta = jnp.stack([plan["pos1"].astype(F32), plan["pos2"].astype(F32), route[:, 2], route[:, 3]]
                     + [jnp.zeros((t,), F32)] * (SUBLANES - 4))
    pos_lanes = jnp.concatenate([jnp.broadcast_to(plan["pos1"][:, None], (t, LANES)),
                                 jnp.broadcast_to(plan["pos2"][:, None], (t, LANES))], axis=1)

    xs, w_row = pl.pallas_call(
        _moe_gather_body,
        grid_spec=pltpu.PrefetchScalarGridSpec(
            num_scalar_prefetch=5,
            grid=(plan["gather"][0].shape[0],),
            in_specs=[pl.BlockSpec((SUBLANES, ck), lambda w, tl, ch, fi, la, va: (0, ch[w])),
                      pl.BlockSpec((ck, d), lambda w, tl, ch, fi, la, va: (ch[w], 0))],
            out_specs=[pl.BlockSpec((MOE_GTILE, d), lambda w, tl, ch, fi, la, va: (tl[w], 0)),
                       pl.BlockSpec((MOE_GTILE, LANES), lambda w, tl, ch, fi, la, va: (tl[w], 0))],
            scratch_shapes=[pltpu.VMEM((MOE_GTILE, d), F32), pltpu.VMEM((MOE_GTILE, 1), F32)]),
        out_shape=[jax.ShapeDtypeStruct((p, d), BF16), jax.ShapeDtypeStruct((p, LANES), F32)],
        compiler_params=_cparams("arbitrary"),
        name="moe_gather",
    )(*plan["gather"], meta, h)

    ys = pl.pallas_call(
        _moe_ffn_body,
        grid_spec=pltpu.PrefetchScalarGridSpec(
            num_scalar_prefetch=1,
            grid=(n_tiles, f // tf),
            in_specs=[pl.BlockSpec((tm, d), lambda i, k, ex: (i, 0)),
                      pl.BlockSpec((tm, LANES), lambda i, k, ex: (i, 0)),
                      pl.BlockSpec((None, d, tf), lambda i, k, ex: (ex[i], 0, k)),
                      pl.BlockSpec((None, d, tf), lambda i, k, ex: (ex[i], 0, k)),
                      pl.BlockSpec((None, tf, d), lambda i, k, ex: (ex[i], k, 0))],
            out_specs=pl.BlockSpec((tm, d), lambda i, k, ex: (i, 0)),
            scratch_shapes=[pltpu.VMEM((tm, d), F32)]),
        out_shape=jax.ShapeDtypeStruct((p, d), BF16),
        compiler_params=_cparams("parallel", "arbitrary"),
        name="moe_ffn",
    )(plan["tile_expert"], xs, w_row, w_gate, w_up, w_down)

    final_norm = final_gain is not None
    gain = (final_gain if final_norm else jnp.ones((d,), F32)).reshape(1, d)
    return pl.pallas_call(
        functools.partial(_moe_combine_body, final_norm=final_norm),
        grid_spec=pltpu.PrefetchScalarGridSpec(
            num_scalar_prefetch=5,
            grid=(plan["combine"][0].shape[0],),
            in_specs=[pl.BlockSpec((ck, d), lambda w, tl, bk, fi, la, va: (tl[w], 0)),
                      pl.BlockSpec((ck, 2 * LANES), lambda w, tl, bk, fi, la, va: (tl[w], 0)),
                      pl.BlockSpec((rb, d), lambda w, tl, bk, fi, la, va: (bk[w], 0)),
                      pl.BlockSpec((1, d), lambda w, tl, bk, fi, la, va: (0, 0))],
            out_specs=pl.BlockSpec((ck, d), lambda w, tl, bk, fi, la, va: (tl[w], 0)),
            scratch_shapes=[pltpu.VMEM((ck, d), F32)]),
        out_shape=jax.ShapeDtypeStruct((t, d), F32),
        compiler_params=_cparams("arbitrary"),
        name="moe_combine",
    )(*plan["combine"], x, pos_lanes, ys, gain)


def _rmsnorm_body(x_ref, g_ref, o_ref):
    o_ref[...] = _rms(x_ref[...], g_ref[...])


def rmsnorm(x, g, tm=1024):
    t, d = x.shape
    tm = min(tm, t)
    return pl.pallas_call(
        _rmsnorm_body,
        grid=(t // tm,),
        in_specs=[pl.BlockSpec((tm, d), lambda i: (i, 0)), pl.BlockSpec((1, d), lambda i: (0, 0))],
        out_specs=pl.BlockSpec((tm, d), lambda i: (i, 0)),
        out_shape=jax.ShapeDtypeStruct((t, d), F32),
        compiler_params=_cparams("parallel"),
        name="rmsnorm",
    )(x, g.reshape(1, d))


def _in_proj_slices(w):
    sizes = (MOBA_WIDTH, MOBA_WIDTH, MOBA_WIDTH, SSM_WIDTH, GDN_WIDTH, GDN_WIDTH, GDN_WIDTH,
             GDN_HEADS, GDN_HEADS, GDN_WIDTH, w.shape[0], w.shape[0], w.shape[0])
    parts, start = [], 0
    for size in sizes:
        parts.append(w[:, start:start + size])
        start += size
    return parts


def kernel(x, mem, positions, norm_mix, w_in, ssm_a_re, ssm_a_im, ssm_log_dt, ssm_b_re, ssm_b_im, ssm_c_re, ssm_c_im, ssm_d, ssm_w_glu, gdn_conv, gdn_a_log, gdn_dt_bias, gdn_norm, w_up_moba, w_up_gdn, w_out, norm_xa, norm_mem, xa_wq, xa_wk, xa_wv, xa_wo, norm_ffn, ffn_w_gate, ffn_w_up, ffn_w_down, moe_w_router, moe_w_gate, moe_w_up, moe_w_down, norm_final):
    batch, seq, d = x.shape
    depth = w_in.shape[0]
    t = batch * seq
    assert batch == SUBLANES, "the S5 scan packs the batch onto the 8 sublanes of a vreg"
    assert seq % MOBA_BLOCK == 0 and seq % GDN_CHUNK == 0
    nc = seq // GDN_CHUNK
    bf = lambda a: a.astype(BF16)

    xf = x.reshape(t, d)
    memf = mem.reshape(-1, d)
    cos, sin = rope_tables(positions)

    for l in range(depth):
        (wq_m, wk_m, wv_m, wu_s, wq_g, wk_g, wv_g, wa_g, wb_g, wz_g, wg_a, wg_b, wg_c) = _in_proj_slices(w_in[l])
        g_mix = norm_mix[l]
        w_ab = jnp.pad(jnp.concatenate([wa_g, wb_g], axis=1), ((0, 0), (0, LANES - 2 * GDN_HEADS)))
        w32 = jnp.concatenate([wq_m, wk_m, wu_s, wq_g, wk_g, wv_g, w_ab], axis=1)
        w16 = jnp.concatenate([wv_m, wz_g, wg_a, wg_b, wg_c], axis=1)
        p32 = norm_matmul(xf, g_mix, bf(w32), F32, tm=2048, tn=640)
        p16 = norm_matmul(xf, g_mix, bf(w16), BF16, tm=2048, tn=1024)
        c_us = 2 * MOBA_WIDTH
        c_qkv = c_us + SSM_WIDTH
        c_ab = c_qkv + 3 * GDN_WIDTH
        u_s = p32[:, c_us:c_us + SSM_WIDTH]
        ab_g = p32[:, c_ab:c_ab + 2 * GDN_HEADS]

        y_a = moba_attention(p32, p16, cos, sin, batch, seq, q_col=0, k_col=MOBA_WIDTH // LANES, v_col=0)

        l_re, l_im, bb_re, bb_im = s5_discretise(ssm_a_re[l], ssm_a_im[l], ssm_log_dt[l], ssm_b_re[l], ssm_b_im[l])
        maps = s5_block_maps(bb_re, bb_im, ssm_c_re[l], ssm_c_im[l])
        u_tm = u_s.reshape(batch, seq, SSM_WIDTH).transpose(1, 0, 2).reshape(t, SSM_WIDTH)
        y_s_tm = s5_gelu(u_tm, maps, l_re, l_im, ssm_d[l])
        y_s = y_s_tm.reshape(seq, batch, SSM_WIDTH).transpose(1, 0, 2).reshape(t, SSM_WIDTH)

        def head_rows(cols):
            return cols.reshape(batch, nc, GDN_CHUNK, GDN_HEADS).transpose(0, 3, 1, 2)
        a_rows = head_rows(ab_g[:, :GDN_HEADS])
        b_rows = head_rows(ab_g[:, GDN_HEADS:2 * GDN_HEADS])
        y_c = gdn_mixer(p32, p16, a_rows, b_rows, gdn_conv[l], gdn_a_log[l], gdn_dt_bias[l], gdn_norm[l],
                        batch, seq, qkv_col=c_qkv // LANES, z_col=MOBA_WIDTH // GDN_WIDTH)

        xf = merge_project(xf, y_a, y_s, y_c, p16, bf(w_up_moba[l]), bf(ssm_w_glu[l]), bf(w_up_gdn[l]),
                           bf(w_out[l]), gate_col=(MOBA_WIDTH + GDN_WIDTH) // d)

        k_x = norm_matmul(memf, norm_mem[l], bf(xa_wk[l]), BF16)
        v_x = norm_matmul(memf, norm_mem[l], bf(xa_wv[l]), BF16)
        xf = cross_attention(xf, norm_xa[l], bf(xa_wq[l]), k_x, v_x, bf(xa_wo[l]), batch, seq)

        if l % 2 == 0:
            i = l // 2
            xf = ffn(xf, norm_ffn[l], bf(ffn_w_gate[i]), bf(ffn_w_up[i]), bf(ffn_w_down[i]))
        else:
            i = l // 2
            route, h_moe = moe_router(xf, norm_ffn[l], moe_w_router[i])
            xf = moe_ffn(xf, h_moe, route, bf(moe_w_gate[i]), bf(moe_w_up[i]), bf(moe_w_down[i]),
                         final_gain=norm_final if l == depth - 1 else None)

    if depth % 2 == 1:
        xf = rmsnorm(xf, norm_final)
    return xf.reshape(batch, seq, d)
```

```python
import functools
import math

import numpy as np
import jax
import jax.numpy as jnp
from jax import lax
from jax.experimental import pallas as pl
from jax.experimental.pallas import tpu as pltpu

F32 = jnp.float32
BF16 = jnp.bfloat16
HIGHEST = lax.Precision.HIGHEST

EPS = 1e-6
NEG_INF = -1e30
MOBA_HEADS = 8
MOBA_HEAD_DIM = 64
MOBA_WIDTH = MOBA_HEADS * MOBA_HEAD_DIM
MOBA_BLOCK = 256
MOBA_TOPK = 3
ROPE_THETA = 10000.0
SSM_WIDTH = 512
SSM_GROUP = 16
SSM_GROUPS = SSM_WIDTH // SSM_GROUP
SSM_STATE = 64
SSM_STATES = SSM_GROUPS * SSM_STATE
GDN_HEADS = 4
GDN_HEAD_DIM = 128
GDN_WIDTH = GDN_HEADS * GDN_HEAD_DIM
GDN_CONV = 4
GDN_CHUNK = 64
XA_HEADS = 4
XA_HEAD_DIM = 128
XA_WIDTH = XA_HEADS * XA_HEAD_DIM
N_EXPERTS = 8
TOP_K = 2

LANES = 128
SUBLANES = 8
VMEM_LIMIT = 56 * 1024 * 1024


def _cparams(*sem):
    return pltpu.CompilerParams(dimension_semantics=sem, vmem_limit_bytes=VMEM_LIMIT)


def _rms(x, g):
    return x * lax.rsqrt(jnp.mean(x * x, axis=-1, keepdims=True) + EPS) * g


def _sigmoid(x):
    return 1.0 / (1.0 + jnp.exp(-x))


def _silu(x):
    return x * _sigmoid(x)


def _dot(a, b):
    return jnp.dot(a, b, preferred_element_type=F32)


def _dot_nt(a, b):
    return lax.dot_general(a, b, (((1,), (1,)), ((), ())), preferred_element_type=F32)


def _dot_tn(a, b):
    return lax.dot_general(a, b, (((0,), (0,)), ((), ())), preferred_element_type=F32)


def _norm_matmul_body(x_ref, g_ref, w_ref, o_ref, h_ref):
    @pl.when(pl.program_id(1) == 0)
    def _():
        h_ref[...] = _rms(x_ref[...], g_ref[...]).astype(BF16)

    o_ref[...] = _dot(h_ref[...], w_ref[...]).astype(o_ref.dtype)


def norm_matmul(x, g, w, out_dtype, tm=512, tn=512):
    t, k = x.shape
    n = w.shape[1]
    tm, tn = min(tm, t), min(tn, n)
    return pl.pallas_call(
        _norm_matmul_body,
        grid=(t // tm, n // tn),
        in_specs=[pl.BlockSpec((tm, k), lambda i, j: (i, 0)),
                  pl.BlockSpec((1, k), lambda i, j: (0, 0)),
                  pl.BlockSpec((k, tn), lambda i, j: (0, j))],
        out_specs=pl.BlockSpec((tm, tn), lambda i, j: (i, j)),
        out_shape=jax.ShapeDtypeStruct((t, n), out_dtype),
        scratch_shapes=[pltpu.VMEM((tm, k), BF16)],
        compiler_params=_cparams("parallel", "arbitrary"),
        name="norm_matmul",
    )(x, g.reshape(1, k), w)


def _rope_body(pos_ref, inv_ref, cos_ref, sin_ref):
    ang = pos_ref[...] * inv_ref[...]
    lane = lax.broadcasted_iota(jnp.int32, ang.shape, 1)
    first_half = (lane % MOBA_HEAD_DIM) < (MOBA_HEAD_DIM // 2)
    cos_ref[...] = jnp.cos(ang)
    s = jnp.sin(ang)
    sin_ref[...] = jnp.where(first_half, -s, s)


def rope_tables(positions):
    t = positions.size
    half = MOBA_HEAD_DIM // 2
    inv = (1.0 / (np.float32(ROPE_THETA) ** (np.arange(0, MOBA_HEAD_DIM, 2, dtype=np.float32)
                                             / np.float32(MOBA_HEAD_DIM)))).astype(np.float32)
    inv_row = jnp.asarray(np.tile(inv, LANES // half).reshape(1, LANES))
    pos = positions.astype(F32).reshape(t, 1)
    ts = min(1024, t)
    return pl.pallas_call(
        _rope_body,
        grid=(t // ts,),
        in_specs=[pl.BlockSpec((ts, 1), lambda i: (i, 0)),
                  pl.BlockSpec((1, LANES), lambda i: (0, 0))],
        out_specs=[pl.BlockSpec((ts, LANES), lambda i: (i, 0))] * 2,
        out_shape=[jax.ShapeDtypeStruct((t, LANES), F32)] * 2,
        compiler_params=_cparams("parallel"),
        name="rope_tables",
    )(pos, inv_row)


def _rope_apply(x, cos, sin_signed):
    lane = lax.broadcasted_iota(jnp.int32, x.shape, 1)
    first_half = (lane % MOBA_HEAD_DIM) < (MOBA_HEAD_DIM // 2)
    partner = jnp.where(first_half, pltpu.roll(x, LANES - MOBA_HEAD_DIM // 2, 1),
                        pltpu.roll(x, MOBA_HEAD_DIM // 2, 1))
    return x * cos + partner * sin_signed


def _moba_body(q_ref, k_ref, v_ref, cq_ref, sq_ref, ck_ref, sk_ref, o_ref, ka_ref, va_ref, km_ref, s_ref,
               mt_ref, acc_ref, *, nb):
    blk = MOBA_BLOCK
    i = pl.program_id(2)

    @pl.when(i == 0)
    def _():
        km_ref[...] = jnp.zeros_like(km_ref)
        lane_b = lax.broadcasted_iota(jnp.int32, (blk, LANES), 1)

        def rope_blk(j, c):
            rows = pl.ds(pl.multiple_of(j * blk, blk), blk)
            kr = _rope_apply(k_ref[rows, :], ck_ref[rows, :], sk_ref[rows, :])
            ka_ref[rows, 0:LANES] = kr.astype(BF16)
            ka_ref[rows, LANES:2 * LANES] = jnp.where(lane_b == j, 1.0, 0.0).astype(BF16)
            km_ref[pl.ds(j, 1), :] = jnp.mean(kr, axis=0, keepdims=True)
            v = v_ref[rows, :].astype(F32)
            va_ref[0, rows, :] = jnp.where(lane_b < MOBA_HEAD_DIM, v,
                                           jnp.where(lane_b == MOBA_HEAD_DIM, 1.0, 0.0)).astype(BF16)
            va_ref[1, rows, :] = jnp.where(lane_b >= MOBA_HEAD_DIM, v,
                                           jnp.where(lane_b == 0, 1.0, 0.0)).astype(BF16)
            return c
        lax.fori_loop(0, nb, rope_blk, 0)

    tq = 2 * blk
    lane = lax.broadcasted_iota(jnp.int32, (1, LANES), 1)
    head_a = lane < MOBA_HEAD_DIM
    q = _rope_apply(q_ref[...], cq_ref[...], sq_ref[...])
    scale = MOBA_HEAD_DIM ** -0.5
    km = km_ref[...]
    nbp = km_ref.shape[0]
    brow = lax.broadcasted_iota(jnp.int32, (nbp, tq), 0)
    qblk = 2 * i + jnp.where(lax.broadcasted_iota(jnp.int32, (1, tq), 1) >= blk, 1, 0)
    heads = (0, 1)
    q_heads = [jnp.where(head_a, q, 0.0), jnp.where(head_a, 0.0, q)]
    gates = [lax.dot_general(km, qh, (((1,), (1,)), ((), ())), preferred_element_type=F32, precision=HIGHEST)
             for qh in q_heads]
    q_augs = []
    for h in heads:
        gate = jnp.where(brow < qblk, gates[h], NEG_INF)
        bias_rows = []
        for j in range(nb):
            gj = gate[j:j + 1, :]
            ahead = (gate > gj) | ((gate == gj) & (brow < j))
            rank = jnp.sum(jnp.where(ahead, 1.0, 0.0), axis=0, keepdims=True)
            keep = ((rank < float(MOBA_TOPK)) & (j < qblk)) | (j == qblk)
            bias_rows.append(jnp.where(keep, 0.0, NEG_INF))
        bias_t = jnp.concatenate(bias_rows + [jnp.zeros((LANES - nb, tq), F32)], axis=0)
        q_augs.append(jnp.concatenate([q_heads[h] * scale, bias_t.T], axis=1).astype(BF16))

    def pair_rows(jp):
        return pl.ds(pl.multiple_of(jp * tq, tq), tq)

    r_idx = lax.broadcasted_iota(jnp.int32, (tq, tq), 0)
    c_idx = lax.broadcasted_iota(jnp.int32, (tq, tq), 1)
    k_diag = ka_ref[pair_rows(i), :]
    s_diag = [_dot_nt(q_augs[h], k_diag) for h in heads]
    for h in heads:
        s_d = jnp.where(c_idx <= r_idx, s_diag[h], NEG_INF)
        s_ref[h, i] = s_d
        m_t = s_d[:, 0:LANES]
        for col in range(LANES, tq, LANES):
            m_t = jnp.maximum(m_t, s_d[:, col:col + LANES])
        mt_ref[h] = m_t

    def loop2(n, body):
        def two(k, c):
            body([2 * k, 2 * k + 1])
            return c
        lax.fori_loop(0, lax.shift_right_logical(n, 1), two, 0)

        @pl.when((n & 1) == 1)
        def _():
            body([n - 1])

    def scores(jps):
        k2s = [ka_ref[pair_rows(jp), :] for jp in jps]
        s2s = [[_dot_nt(q_augs[h], k2) for h in heads] for k2 in k2s]
        for jp, s2h in zip(jps, s2s):
            for h in heads:
                s_ref[h, jp] = s2h[h]
        for h in heads:
            m_t = mt_ref[h]
            for s2h in s2s:
                for col in range(0, tq, LANES):
                    m_t = jnp.maximum(m_t, s2h[h][:, col:col + LANES])
            mt_ref[h] = m_t
    loop2(i, scores)

    ms = [jnp.max(mt_ref[h], axis=-1, keepdims=True) for h in heads]
    for h in heads:
        acc_ref[h] = jnp.zeros((tq, LANES), F32)

    def values(jps):
        ps = [[jnp.exp(s_ref[h, jp] - ms[h]).astype(BF16) for h in heads] for jp in jps]
        pvs = [[_dot(p[h], va_ref[h, pair_rows(jp), :]) for h in heads] for jp, p in zip(jps, ps)]
        for h in heads:
            tot = pvs[0][h]
            for pv in pvs[1:]:
                tot = tot + pv[h]
            acc_ref[h] += tot
    loop2(i + 1, values)
    acc_a, acc_b = acc_ref[0], acc_ref[1]
    out_a = acc_a / acc_a[:, MOBA_HEAD_DIM:MOBA_HEAD_DIM + 1]
    out_b = acc_b / acc_b[:, 0:1]
    o_ref[...] = jnp.where(head_a, out_a, out_b).astype(o_ref.dtype)


def moba_attention(qk, v, cos, sin, batch, seq, q_col=0, k_col=MOBA_WIDTH // LANES, v_col=0):
    nb = seq // MOBA_BLOCK
    assert nb % 2 == 0
    npair = nb // 2
    pairs = MOBA_WIDTH // LANES
    tq = 2 * MOBA_BLOCK
    t = batch * seq
    return pl.pallas_call(
        functools.partial(_moba_body, nb=nb),
        grid=(batch, pairs, npair),
        in_specs=[pl.BlockSpec((tq, LANES), lambda b, p, i: (b * npair + i, q_col + p)),
                  pl.BlockSpec((seq, LANES), lambda b, p, i: (b, k_col + p)),
                  pl.BlockSpec((seq, LANES), lambda b, p, i: (b, v_col + p)),
                  pl.BlockSpec((tq, LANES), lambda b, p, i: (b * npair + i, 0)),
                  pl.BlockSpec((tq, LANES), lambda b, p, i: (b * npair + i, 0)),
                  pl.BlockSpec((seq, LANES), lambda b, p, i: (b, 0)),
                  pl.BlockSpec((seq, LANES), lambda b, p, i: (b, 0))],
        out_specs=pl.BlockSpec((tq, LANES), lambda b, p, i: (b * npair + i, p)),
        out_shape=jax.ShapeDtypeStruct((t, MOBA_WIDTH), BF16),
        scratch_shapes=[pltpu.VMEM((seq, 2 * LANES), BF16),
                        pltpu.VMEM((2, seq, LANES), BF16),
                        pltpu.VMEM((-(-nb // SUBLANES) * SUBLANES, LANES), F32),
                        pltpu.VMEM((2, npair, tq, tq), F32),
                        pltpu.VMEM((2, tq, LANES), F32),
                        pltpu.VMEM((2, tq, LANES), F32)],
        compiler_params=_cparams("parallel", "parallel", "arbitrary"),
        name="moba_attention",
    )(qk, qk, v, cos, sin, cos, sin)


def _s5_disc_body(are_ref, aim_ref, ldt_ref, bre_ref, bim_ref, lre_ref, lim_ref, bbre_ref, bbim_ref):
    a_re, a_im = are_ref[...], aim_ref[...]
    dt = jnp.exp(ldt_ref[...])
    mag = jnp.exp(a_re * dt)
    l_re = mag * jnp.cos(a_im * dt)
    l_im = mag * jnp.sin(a_im * dt)
    lre_ref[...] = l_re
    lim_ref[...] = l_im
    x, y = l_re - 1.0, l_im
    den = a_re * a_re + a_im * a_im
    c_re = (x * a_re + y * a_im) / den
    c_im = (y * a_re - x * a_im) / den
    b_re, b_im = bre_ref[...], bim_ref[...]
    bbre_ref[...] = c_re * b_re - c_im * b_im
    bbim_ref[...] = c_re * b_im + c_im * b_re


def s5_discretise(a_re, a_im, log_dt, b_re, b_im):
    g, p = a_re.shape
    n = b_re.shape[-1]
    col = lambda a: a.reshape(g * p, 1)
    ldt = jnp.broadcast_to(log_dt[:, None], (g, p))
    outs = pl.pallas_call(
        _s5_disc_body,
        out_shape=[jax.ShapeDtypeStruct((g * p, 1), F32)] * 2 + [jax.ShapeDtypeStruct((g * p, n), F32)] * 2,
        name="s5_discretise",
    )(col(a_re), col(a_im), col(ldt), b_re.reshape(g * p, n), b_im.reshape(g * p, n))
    l_re, l_im, bb_re, bb_im = outs
    return l_re.reshape(g, p), l_im.reshape(g, p), bb_re.reshape(g, p, n), bb_im.reshape(g, p, n)


def _block_diag(blocks):
    g, r, c = blocks.shape
    eye = jnp.eye(g, dtype=blocks.dtype)
    return (blocks[:, :, None, :] * eye[:, None, :, None]).reshape(g * r, g * c)


def _s5_body(u_ref, bre_ref, bim_ref, cre_ref, cim_ref, lre_ref, lim_ref, d_ref, y_ref, h_ref, hb_ref, *, steps):
    rows = SUBLANES
    ns = SSM_STATES
    nq = SSM_WIDTH // LANES
    sq = ns // nq

    @pl.when(pl.program_id(0) == 0)
    def _():
        h_ref[...] = jnp.zeros_like(h_ref)

    u = u_ref[...]
    u16 = u.astype(BF16)
    for q in range(nq):
        uq = u16[:, q * LANES:(q + 1) * LANES]
        hb_ref[:, q * sq:(q + 1) * sq] = _dot(uq, bre_ref[q])
        hb_ref[:, ns + q * sq:ns + (q + 1) * sq] = _dot(uq, bim_ref[q])
    l_re, l_im = lre_ref[...], lim_ref[...]

    def step(t, carry):
        h_re, h_im = carry
        r = pl.ds(pl.multiple_of(t * rows, rows), rows)
        n_re = l_re * h_re - l_im * h_im + hb_ref[r, 0:ns]
        n_im = l_re * h_im + l_im * h_re + hb_ref[r, ns:2 * ns]
        hb_ref[r, 0:ns] = n_re
        hb_ref[r, ns:2 * ns] = n_im
        return n_re, n_im

    h_re, h_im = lax.fori_loop(0, steps, step, (h_ref[:, 0:ns], h_ref[:, ns:2 * ns]))
    h_ref[:, 0:ns] = h_re
    h_ref[:, ns:2 * ns] = h_im
    ys = []
    for q in range(nq):
        ys.append(_dot(hb_ref[:, q * sq:(q + 1) * sq].astype(BF16), cre_ref[q])
                  + _dot(hb_ref[:, ns + q * sq:ns + (q + 1) * sq].astype(BF16), cim_ref[q]))
    y = jnp.concatenate(ys, axis=1) + d_ref[...] * u
    y_ref[...] = jax.nn.gelu(y).astype(y_ref.dtype)


def s5_block_maps(bb_re, bb_im, c_re, c_im):
    nq = SSM_WIDTH // LANES
    gq = SSM_GROUPS // nq
    per_q = lambda a: jnp.stack([_block_diag(a[q * gq:(q + 1) * gq].transpose(0, 2, 1)) for q in range(nq)])
    return (per_q(bb_re).astype(BF16), per_q(bb_im).astype(BF16),
            per_q(c_re).astype(BF16), per_q(-c_im).astype(BF16))


def s5_gelu(u_tm, maps, l_re, l_im, d_skip, steps=64):
    rows = u_tm.shape[0]
    blk = steps * SUBLANES
    ns2 = 2 * SSM_STATES
    full3 = lambda a: pl.BlockSpec(a.shape, lambda c: (0, 0, 0))
    return pl.pallas_call(
        functools.partial(_s5_body, steps=steps),
        grid=(rows // blk,),
        in_specs=[pl.BlockSpec((blk, SSM_WIDTH), lambda c: (c, 0)),
                  full3(maps[0]), full3(maps[1]), full3(maps[2]), full3(maps[3]),
                  pl.BlockSpec((1, SSM_STATES), lambda c: (0, 0)),
                  pl.BlockSpec((1, SSM_STATES), lambda c: (0, 0)),
                  pl.BlockSpec((1, SSM_WIDTH), lambda c: (0, 0))],
        out_specs=pl.BlockSpec((blk, SSM_WIDTH), lambda c: (c, 0)),
        out_shape=jax.ShapeDtypeStruct((rows, SSM_WIDTH), BF16),
        scratch_shapes=[pltpu.VMEM((SUBLANES, ns2), F32),
                        pltpu.VMEM((blk, ns2), F32)],
        compiler_params=_cparams("arbitrary"),
        name="s5_scan",
    )(u_tm, *maps, l_re.reshape(1, -1), l_im.reshape(1, -1), d_skip.reshape(1, -1))


GDN_GROUP = 16


def _gdn_prep_body(q_ref, k_ref, v_ref, a_ref, b_ref, wq_ref, wk_ref, wv_ref, alog_ref, dtb_ref,
                   u_ref, w_ref, qd_ref, kd_ref, qk_ref, gl_ref, qs_ref, ks_ref, vs_ref,
                   *, seq):
    L = GDN_CHUNK
    G = min(GDN_GROUP, seq // L)
    dk = GDN_HEAD_DIM
    nc = seq // L
    hd = pl.program_id(1)
    cb = 256
    halo = SUBLANES

    def conv_blk(rb, c):
        base = pl.multiple_of(rb * cb, cb)
        prev = pl.multiple_of(jnp.maximum(base - halo, 0), halo)
        for src, wref, dst, norm, mul in ((q_ref, wq_ref, qs_ref, True, dk ** -0.5),
                                          (k_ref, wk_ref, ks_ref, True, 1.0),
                                          (v_ref, wv_ref, vs_ref, False, 1.0)):
            cur = src[pl.ds(base, cb), :]
            head = jnp.where(rb > 0, src[pl.ds(prev, halo), :], 0.0)
            ext = jnp.concatenate([head, cur], axis=0)
            wts = wref[...]
            y = ext[halo:halo + cb] * wts[GDN_CONV - 1:GDN_CONV]
            for tap in range(1, GDN_CONV):
                y = y + ext[halo - tap:halo - tap + cb] * wts[GDN_CONV - 1 - tap:GDN_CONV - tap]
            y = _silu(y)
            if norm:
                y = y * lax.rsqrt(jnp.sum(y * y, axis=-1, keepdims=True) + EPS) * mul
            dst[pl.ds(base, cb), :] = y
        return c
    lax.fori_loop(0, seq // cb, conv_blk, 0)

    gl_rows = G * L
    ri = lax.broadcasted_iota(jnp.int32, (gl_rows, L), 0) % L
    ci = lax.broadcasted_iota(jnp.int32, (gl_rows, L), 1)
    tril = ci <= ri
    strict = ci < ri
    eye = ci == ri
    ri3 = lax.broadcasted_iota(jnp.int32, (G, L, L), 1)
    ci3 = lax.broadcasted_iota(jnp.int32, (G, L, L), 2)
    strict3 = ci3 < ri3
    eye_f = jnp.where(ci3 == ri3, 1.0, 0.0)
    r2 = lax.broadcasted_iota(jnp.int32, (L, L), 0)
    c2 = lax.broadcasted_iota(jnp.int32, (L, L), 1)
    upper_f = jnp.where(r2 <= c2, 1.0, 0.0)
    a_coef = -jnp.exp(jnp.full((1, L), alog_ref[hd], F32))
    dt_bias = dtb_ref[hd]
    hp = functools.partial(jnp.dot, preferred_element_type=F32, precision=HIGHEST)

    def bmm(a, b):
        return jnp.einsum('gij,gjk->gik', a, b, preferred_element_type=F32)

    def bmm_nt(a, b):
        return jnp.einsum('gid,gjd->gij', a, b, preferred_element_type=F32)

    def to_col(rows_b):
        return jnp.sum(jnp.where(eye, rows_b, 0.0), axis=1, keepdims=True)

    def prep(cg, carry):
        c0 = pl.multiple_of(cg * G, G)
        rows = pl.ds(pl.multiple_of(cg * gl_rows, gl_rows), gl_rows)
        qc, kc, vc = qs_ref[rows, :], ks_ref[rows, :], vs_ref[rows, :]
        z = a_ref[pl.ds(c0, G), :] + dt_bias
        softplus = jnp.maximum(z, 0.0) + jnp.log(1.0 + jnp.exp(-jnp.abs(z)))
        g_rows = hp(a_coef * softplus, upper_f)
        beta_rows = _sigmoid(b_ref[pl.ds(c0, G), :])
        spread = lambda r: jnp.concatenate([jnp.broadcast_to(r[c:c + 1, :], (L, L)) for c in range(G)], axis=0)
        g_rb = spread(g_rows)
        g_col = to_col(g_rb)
        beta_col = to_col(spread(beta_rows))
        g_last = g_rb[:, L - 1:L]
        decay = jnp.where(tril, jnp.exp(jnp.where(tril, g_col - g_rb, 0.0)), 0.0)
        k_beta = kc * beta_col
        v_beta = vc * beta_col
        to3 = lambda a: a.reshape(G, L, a.shape[-1])
        kb16, k16, q16 = to3(k_beta.astype(BF16)), to3(kc.astype(BF16)), to3(qc.astype(BF16))
        kk_qk = bmm_nt(jnp.concatenate([kb16, q16], axis=1), k16)
        decay3 = to3(decay)
        a_low = jnp.where(strict3, kk_qk[:, :L] * decay3, 0.0)
        qk = kk_qk[:, L:] * decay3
        t_inv = eye_f - a_low
        a16 = a_low.astype(BF16)
        pw = bmm(a16, a16)
        span = 2
        while 2 * span < L:
            pw16 = pw.astype(BF16)
            both = bmm(jnp.concatenate([t_inv.astype(BF16), pw16], axis=1), pw16)
            t_inv = t_inv + both[:, :L]
            pw = both[:, L:]
            span *= 2
        t_inv = t_inv + bmm(t_inv.astype(BF16), pw.astype(BF16))
        rhs = jnp.concatenate([v_beta.astype(BF16), (k_beta * jnp.exp(g_col)).astype(BF16)], axis=1)
        uw = bmm(t_inv.astype(BF16), to3(rhs)).reshape(gl_rows, 2 * dk)
        u_ref[rows, :] = uw[:, :dk]
        w_ref[rows, :] = uw[:, dk:].astype(BF16)
        qd_ref[rows, :] = (qc * jnp.exp(g_col)).astype(BF16)
        k_dec = to3(kc * jnp.exp(g_last - g_col))
        kd_ref[pl.ds(c0, G)] = jnp.swapaxes(k_dec, 1, 2).astype(BF16)
        qk_ref[pl.ds(c0, G)] = qk.astype(BF16)
        gl_ref[pl.ds(c0, G), :] = jnp.broadcast_to(jnp.exp(g_rows[:, L - 1:L]), (G, LANES))
        return carry
    lax.fori_loop(0, nc // G, prep, 0)


def _gdn_scan_body(u_ref, w_ref, qd_ref, kd_ref, qk_ref, gl_ref, z_ref, gn_ref, o_ref, *st_refs, chunks):
    L = GDN_CHUNK
    d = GDN_HEAD_DIM
    nrow = u_ref.shape[0]

    @pl.when(pl.program_id(1) == 0)
    def _():
        for st in st_refs:
            st[...] = jnp.zeros_like(st)

    gn = gn_ref[...]

    def chunk(c, carry):
        rows = pl.ds(pl.multiple_of(c * L, L), L)
        chains = [(r, h) for r in range(nrow) for h in range(GDN_HEADS)]
        states = [st_refs[r * GDN_HEADS + h][...] for r, h in chains]
        firsts = []
        for (r, h), s in zip(chains, states):
            cols = slice(h * d, (h + 1) * d)
            firsts.append(_dot(jnp.concatenate([w_ref[r, rows, cols], qd_ref[r, rows, cols]], axis=0),
                               s.astype(BF16)))
        vns = []
        for (r, h), both in zip(chains, firsts):
            cols = slice(h * d, (h + 1) * d)
            vns.append((u_ref[r, rows, cols] - both[:L]).astype(BF16))
        outs = [both[L:] + _dot(qk_ref[r, h, c], vn) for (r, h), both, vn in zip(chains, firsts, vns)]
        for (r, h), s, vn in zip(chains, states, vns):
            st_refs[r * GDN_HEADS + h][...] = s * gl_ref[r, h, pl.ds(c, 1), :] + _dot(kd_ref[r, h, c], vn)
        for (r, h), o in zip(chains, outs):
            cols = slice(h * d, (h + 1) * d)
            o_ref[r, rows, cols] = (_rms(o, gn) * _silu(z_ref[r, rows, cols].astype(F32))).astype(o_ref.dtype)
        return carry
    lax.fori_loop(0, chunks, chunk, 0)


def gdn_mixer(qkv, z, a_rows, b_rows, conv_w, a_log, dt_bias, gnorm, batch, seq, ts=1024, qkv_col=0, z_col=0):
    t = batch * seq
    hds = GDN_HEADS
    nc = seq // GDN_CHUNK
    L = GDN_CHUNK
    d = GDN_HEAD_DIM
    seq_spec = lambda off: pl.BlockSpec((seq, d), lambda b, h: (b, qkv_col + off + h))
    w_spec = lambda off: pl.BlockSpec((GDN_CONV, d), lambda b, h: (0, off + h))
    row_spec = pl.BlockSpec((None, None, nc, L), lambda b, h: (b, h, 0, 0))
    smem = pl.BlockSpec(memory_space=pltpu.SMEM)
    head_out = pl.BlockSpec((seq, d), lambda b, h: (b, h))
    u, w, qd, kd, qk, gl = pl.pallas_call(
        functools.partial(_gdn_prep_body, seq=seq),
        grid=(batch, hds),
        in_specs=[seq_spec(0), seq_spec(hds), seq_spec(2 * hds), row_spec, row_spec,
                  w_spec(0), w_spec(hds), w_spec(2 * hds), smem, smem],
        out_specs=[head_out, head_out, head_out,
                   pl.BlockSpec((None, None, nc, d, L), lambda b, h: (b, h, 0, 0, 0)),
                   pl.BlockSpec((None, None, nc, L, L), lambda b, h: (b, h, 0, 0, 0)),
                   pl.BlockSpec((None, None, nc, LANES), lambda b, h: (b, h, 0, 0))],
        out_shape=[jax.ShapeDtypeStruct((t, GDN_WIDTH), F32),
                   jax.ShapeDtypeStruct((t, GDN_WIDTH), BF16),
                   jax.ShapeDtypeStruct((t, GDN_WIDTH), BF16),
                   jax.ShapeDtypeStruct((batch, hds, nc, d, L), BF16),
                   jax.ShapeDtypeStruct((batch, hds, nc, L, L), BF16),
                   jax.ShapeDtypeStruct((batch, hds, nc, LANES), F32)],
        scratch_shapes=[pltpu.VMEM((seq, d), F32), pltpu.VMEM((seq, d), F32), pltpu.VMEM((seq, d), F32)],
        compiler_params=_cparams("parallel", "parallel"),
        name="gdn_prep",
    )(qkv, qkv, qkv, a_rows, b_rows, conv_w, conv_w, conv_w, a_log, dt_bias)

    ts = min(ts, seq)
    spb = seq // ts
    cps = ts // L
    nrow = 2 if batch % 2 == 0 else 1
    by_row = lambda a: a.reshape(batch, seq, a.shape[-1])
    tok = pl.BlockSpec((nrow, ts, GDN_WIDTH), lambda b, s: (b, s, 0))
    out = pl.pallas_call(
        functools.partial(_gdn_scan_body, chunks=cps),
        grid=(batch // nrow, spb),
        in_specs=[tok, tok, tok,
                  pl.BlockSpec((nrow, hds, cps, d, L), lambda b, s: (b, 0, s, 0, 0)),
                  pl.BlockSpec((nrow, hds, cps, L, L), lambda b, s: (b, 0, s, 0, 0)),
                  pl.BlockSpec((nrow, hds, cps, LANES), lambda b, s: (b, 0, s, 0)),
                  pl.BlockSpec((nrow, ts, GDN_WIDTH), lambda b, s: (b, s, z_col)),
                  pl.BlockSpec((1, d), lambda b, s: (0, 0))],
        out_specs=tok,
        out_shape=jax.ShapeDtypeStruct((batch, seq, GDN_WIDTH), BF16),
        scratch_shapes=[pltpu.VMEM((d, d), F32)] * (nrow * hds),
        compiler_params=_cparams("parallel", "arbitrary"),
        name="gdn_scan",
    )(by_row(u), by_row(w), by_row(qd), kd, qk, gl, by_row(z), gnorm.reshape(1, d))
    return out.reshape(t, GDN_WIDTH)


def _merge_body(x_ref, ya_ref, ys_ref, yc_ref, ga_ref, gb_ref, gc_ref, wa_ref, wg_ref, wc_ref, wo_ref, o_ref):
    d = x_ref.shape[1]
    y_a = _dot(ya_ref[...], wa_ref[...])
    glu = _dot(ys_ref[...], wg_ref[...])
    y_b = glu[:, :d] * _sigmoid(glu[:, d:])
    y_c = _dot(yc_ref[...], wc_ref[...])
    gate = lambda r: _sigmoid(r[...].astype(F32))
    merged = gate(ga_ref) * y_a + gate(gb_ref) * y_b + gate(gc_ref) * y_c
    o_ref[...] = x_ref[...] + _dot(merged.astype(BF16), wo_ref[...])


def merge_project(x, y_a, y_s, y_c, gates, w_a, w_glu, w_c, w_o, tm=512, gate_col=0):
    t, d = x.shape
    row = lambda n: pl.BlockSpec((tm, n), lambda i: (i, 0))
    gate = lambda k: pl.BlockSpec((tm, d), lambda i: (i, gate_col + k))
    full = lambda a: pl.BlockSpec(a.shape, lambda i: (0, 0))
    return pl.pallas_call(
        _merge_body,
        grid=(t // tm,),
        in_specs=[row(d), row(y_a.shape[1]), row(y_s.shape[1]), row(y_c.shape[1]), gate(0), gate(1), gate(2),
                  full(w_a), full(w_glu), full(w_c), full(w_o)],
        out_specs=row(d),
        out_shape=jax.ShapeDtypeStruct((t, d), F32),
        compiler_params=_cparams("parallel"),
        name="merge_project",
    )(x, y_a, y_s, y_c, gates, gates, gates, w_a, w_glu, w_c, w_o)


def _xattn_body(x_ref, g_ref, wq_ref, k_ref, v_ref, wo_ref, o_ref):
    x = x_ref[...]
    h = _rms(x, g_ref[...]).astype(BF16)
    q = _dot(h, wq_ref[...]).astype(BF16)
    cols = [slice(hd * XA_HEAD_DIM, (hd + 1) * XA_HEAD_DIM) for hd in range(XA_HEADS)]
    scores = [_dot_nt(q[:, c], k_ref[:, c]) * (XA_HEAD_DIM ** -0.5) for c in cols]
    probs = []
    for s in scores:
        p = jnp.exp(s - jnp.max(s, axis=-1, keepdims=True))
        probs.append((p / jnp.sum(p, axis=-1, keepdims=True)).astype(BF16))
    outs = [_dot(p, v_ref[:, c]) for p, c in zip(probs, cols)]
    o = jnp.concatenate(outs, axis=1).astype(BF16)
    o_ref[...] = x + _dot(o, wo_ref[...])


def cross_attention(x, g, wq, k, v, wo, batch, seq, tm=512):
    t, d = x.shape
    m = k.shape[0] // batch
    spb = seq // tm
    full = lambda a: pl.BlockSpec(a.shape, lambda i: (0, 0))
    return pl.pallas_call(
        _xattn_body,
        grid=(t // tm,),
        in_specs=[pl.BlockSpec((tm, d), lambda i: (i, 0)),
                  pl.BlockSpec((1, d), lambda i: (0, 0)),
                  full(wq),
                  pl.BlockSpec((m, XA_WIDTH), lambda i: (i // spb, 0)),
                  pl.BlockSpec((m, XA_WIDTH), lambda i: (i // spb, 0)),
                  full(wo)],
        out_specs=pl.BlockSpec((tm, d), lambda i: (i, 0)),
        out_shape=jax.ShapeDtypeStruct((t, d), F32),
        compiler_params=_cparams("parallel"),
        name="cross_attention",
    )(x, g.reshape(1, d), wq, k, v, wo)


def _ffn_body(x_ref, g_ref, wg_ref, wu_ref, wd_ref, o_ref, h_ref, acc_ref):
    f = pl.program_id(1)

    @pl.when(f == 0)
    def _():
        h_ref[...] = _rms(x_ref[...], g_ref[...]).astype(BF16)
        acc_ref[...] = jnp.zeros_like(acc_ref)

    h = h_ref[...]
    act = (_silu(_dot(h, wg_ref[...])) * _dot(h, wu_ref[...])).astype(BF16)
    acc_ref[...] += _dot(act, wd_ref[...])

    @pl.when(f == pl.num_programs(1) - 1)
    def _():
        o_ref[...] = x_ref[...] + acc_ref[...]


def ffn(x, g, w_gate, w_up, w_down, tm=512, tf=1408):
    t, d = x.shape
    f = w_gate.shape[1]
    return pl.pallas_call(
        _ffn_body,
        grid=(t // tm, f // tf),
        in_specs=[pl.BlockSpec((tm, d), lambda i, k: (i, 0)),
                  pl.BlockSpec((1, d), lambda i, k: (0, 0)),
                  pl.BlockSpec((d, tf), lambda i, k: (0, k)),
                  pl.BlockSpec((d, tf), lambda i, k: (0, k)),
                  pl.BlockSpec((tf, d), lambda i, k: (k, 0))],
        out_specs=pl.BlockSpec((tm, d), lambda i, k: (i, 0)),
        out_shape=jax.ShapeDtypeStruct((t, d), F32),
        scratch_shapes=[pltpu.VMEM((tm, d), BF16), pltpu.VMEM((tm, d), F32)],
        compiler_params=_cparams("parallel", "arbitrary"),
        name="ffn",
    )(x, g.reshape(1, d), w_gate, w_up, w_down)


def _router_body(x_ref, g_ref, w_ref, r_ref, h_ref):
    h = _rms(x_ref[...], g_ref[...])
    h_ref[...] = h.astype(BF16)
    logits = jnp.dot(h, w_ref[...], preferred_element_type=F32, precision=HIGHEST)
    lane = lax.broadcasted_iota(jnp.int32, logits.shape, 1).astype(F32)
    logits = jnp.where(lane < N_EXPERTS, logits, -jnp.inf)
    v1 = jnp.max(logits, axis=-1, keepdims=True)
    i1 = jnp.min(jnp.where(logits == v1, lane, float(LANES)), axis=-1, keepdims=True)
    rest = jnp.where(lane == i1, -jnp.inf, logits)
    v2 = jnp.max(rest, axis=-1, keepdims=True)
    i2 = jnp.min(jnp.where(rest == v2, lane, float(LANES)), axis=-1, keepdims=True)
    e2 = jnp.exp(v2 - v1)
    w1 = 1.0 / (1.0 + e2)
    w2 = e2 / (1.0 + e2)
    r_ref[...] = (jnp.where(lane == 0.0, i1, 0.0) + jnp.where(lane == 1.0, i2, 0.0)
                  + jnp.where(lane == 2.0, w1, 0.0) + jnp.where(lane == 3.0, w2, 0.0))


def moe_router(x, g, w_router, tm=512):
    t, d = x.shape
    w = jnp.pad(w_router, ((0, 0), (0, LANES - w_router.shape[1])))
    return pl.pallas_call(
        _router_body,
        grid=(t // tm,),
        in_specs=[pl.BlockSpec((tm, d), lambda i: (i, 0)),
                  pl.BlockSpec((1, d), lambda i: (0, 0)),
                  pl.BlockSpec((d, LANES), lambda i: (0, 0))],
        out_specs=[pl.BlockSpec((tm, LANES), lambda i: (i, 0)), pl.BlockSpec((tm, d), lambda i: (i, 0))],
        out_shape=[jax.ShapeDtypeStruct((t, LANES), F32), jax.ShapeDtypeStruct((t, d), BF16)],
        compiler_params=_cparams("parallel"),
        name="moe_router",
    )(x, g.reshape(1, d), w)


MOE_TILE = 512
MOE_GTILE = 256
MOE_CHUNK = 512
MOE_RBLK = 256


def _count_le(sorted_vals, queries):
    return jnp.sum((sorted_vals[None, :] <= queries[:, None]).astype(jnp.int32), axis=1)


def _moe_plan(route, t):
    e = N_EXPERTS
    tm, ck, rb = MOE_TILE, MOE_CHUNK, MOE_RBLK
    n_tiles = (TOP_K * t) // tm + e
    p = n_tiles * tm
    n_chunks = t // ck
    i32 = jnp.int32
    e1 = route[:, 0].astype(i32)
    e2 = route[:, 1].astype(i32)
    oh1 = jax.nn.one_hot(e1, e, dtype=i32)
    oh2 = jax.nn.one_hot(e2, e, dtype=i32)
    cnt = oh1 + oh2
    incl = jnp.cumsum(cnt, axis=0)
    excl = incl - cnt
    n_e = incl[-1]
    g_e = ((n_e + tm - 1) // tm) * tm
    off_end = jnp.cumsum(g_e)
    off = off_end - g_e
    pos_all = off[None, :] + excl
    pos1 = jnp.sum(pos_all * oh1, axis=1)
    pos2 = jnp.sum(pos_all * oh2, axis=1)
    tile_expert = jnp.minimum(_count_le(off_end, jnp.arange(n_tiles, dtype=i32) * tm), e - 1)
    gm = MOE_GTILE
    n_gt = p // gm
    gt_start = jnp.arange(n_gt, dtype=i32) * gm
    sel = jax.nn.one_hot(jnp.minimum(_count_le(off_end, gt_start), e - 1), e, dtype=i32)
    r0 = gt_start - jnp.sum(sel * off[None, :], axis=1)
    n_sel = jnp.sum(sel * n_e[None, :], axis=1)
    has_rows = r0 < n_sel
    r1 = jnp.minimum(r0 + gm, n_sel) - 1
    ends = incl[ck - 1::ck]
    ends_sel = jnp.sum(ends[None, :, :] * sel[:, None, :], axis=2)
    c_lo = jnp.where(has_rows, jnp.sum((ends_sel <= r0[:, None]).astype(i32), axis=1), 0)
    c_hi = jnp.where(has_rows, jnp.sum((ends_sel <= r1[:, None]).astype(i32), axis=1), 0)
    n_it = c_hi - c_lo + 1
    it_end = jnp.cumsum(n_it)
    it_start = it_end - n_it
    wmax = n_gt + e * n_chunks
    w_idx = jnp.arange(wmax, dtype=i32)
    g_tile = jnp.minimum(_count_le(it_end, w_idx), n_gt - 1)
    g_valid = w_idx < it_end[-1]
    g_chunk = jnp.where(g_valid, c_lo[g_tile] + (w_idx - it_start[g_tile]), c_hi[n_gt - 1])
    g_first = (g_valid & (w_idx == it_start[g_tile])).astype(i32)
    g_last = (g_valid & (w_idx == it_end[g_tile] - 1)).astype(i32)
    gather = (g_tile, g_chunk.astype(i32), g_first, g_last, g_valid.astype(i32))
    lo = off[None, :] + excl[::ck]
    hi = off[None, :] + ends
    nblk = jnp.where(hi > lo, (hi - 1) // rb - lo // rb + 1, 0).reshape(-1)
    blk_lo = (lo // rb).reshape(-1)
    cb_end = jnp.cumsum(nblk)
    cb_start = cb_end - nblk
    cmax = p // rb + e * n_chunks
    c_idx = jnp.arange(cmax, dtype=i32)
    pair = jnp.minimum(_count_le(cb_end, c_idx), nblk.shape[0] - 1)
    c_valid = c_idx < cb_end[-1]
    c_tile = jnp.where(c_valid, pair // e, n_chunks - 1)
    c_blk = jnp.where(c_valid, blk_lo[pair] + (c_idx - cb_start[pair]), 0)
    per_tile = jnp.sum(nblk.reshape(n_chunks, e), axis=1)
    t_end = jnp.cumsum(per_tile)
    c_first = (c_valid & (c_idx == (t_end - per_tile)[c_tile])).astype(i32)
    c_last = (c_valid & (c_idx == t_end[c_tile] - 1)).astype(i32)
    combine = (c_tile.astype(i32), c_blk.astype(i32), c_first, c_last, c_valid.astype(i32))
    return dict(n_tiles=n_tiles, p=p, tile_expert=tile_expert.astype(i32), pos1=pos1, pos2=pos2,
                gather=gather, combine=combine)


def _moe_gather_body(tile_ref, chunk_ref, first_ref, last_ref, valid_ref, meta_ref, h_ref,
                     o_ref, wr_ref, acc_ref, wacc_ref):
    w = pl.program_id(0)

    @pl.when(first_ref[w] == 1)
    def _():
        acc_ref[...] = jnp.zeros_like(acc_ref)
        wacc_ref[...] = jnp.zeros_like(wacc_ref)

    @pl.when(valid_ref[w] == 1)
    def _():
        row = (lax.broadcasted_iota(jnp.int32, (MOE_GTILE, MOE_CHUNK), 0) + tile_ref[w] * MOE_GTILE).astype(F32)
        hit1 = row == meta_ref[0:1, :]
        hit2 = row == meta_ref[1:2, :]
        onehot = jnp.where(hit1, 1.0, jnp.where(hit2, 1.0, 0.0)).astype(BF16)
        acc_ref[...] += _dot(onehot, h_ref[...])
        wts = jnp.where(hit1, meta_ref[2:3, :], jnp.where(hit2, meta_ref[3:4, :], 0.0))
        wacc_ref[...] += jnp.sum(wts, axis=1, keepdims=True)

    @pl.when(last_ref[w] == 1)
    def _():
        o_ref[...] = acc_ref[...].astype(o_ref.dtype)
        wr_ref[...] = jnp.broadcast_to(wacc_ref[...], wr_ref.shape)


def _moe_ffn_body(exp_ref, x_ref, wr_ref, wg_ref, wu_ref, wd_ref, o_ref, acc_ref):
    f = pl.program_id(1)

    @pl.when(f == 0)
    def _():
        acc_ref[...] = jnp.zeros_like(acc_ref)

    x = x_ref[...]
    act = (_silu(_dot(x, wg_ref[...])) * _dot(x, wu_ref[...])).astype(BF16)
    acc_ref[...] += _dot(act, wd_ref[...])

    @pl.when(f == pl.num_programs(1) - 1)
    def _():
        o_ref[...] = (acc_ref[...] * wr_ref[:, 0:1]).astype(o_ref.dtype)


def _moe_combine_body(tile_ref, blk_ref, first_ref, last_ref, valid_ref, x_ref, pos_ref, y_ref, gf_ref,
                      o_ref, acc_ref, *, final_norm):
    w = pl.program_id(0)

    @pl.when(first_ref[w] == 1)
    def _():
        acc_ref[...] = x_ref[...]

    @pl.when(valid_ref[w] == 1)
    def _():
        base = blk_ref[w] * MOE_RBLK
        col = lax.broadcasted_iota(jnp.int32, (MOE_CHUNK, MOE_RBLK), 1) + base
        reps = MOE_RBLK // LANES
        p1 = jnp.concatenate([pos_ref[:, 0:LANES]] * reps, axis=1)
        p2 = jnp.concatenate([pos_ref[:, LANES:2 * LANES]] * reps, axis=1)
        hit = jnp.where(col == p1, 1.0, jnp.where(col == p2, 1.0, 0.0))
        acc_ref[...] += _dot(hit.astype(BF16), y_ref[...])

    @pl.when(last_ref[w] == 1)
    def _():
        o_ref[...] = _rms(acc_ref[...], gf_ref[...]) if final_norm else acc_ref[...]


def moe_ffn(x, h, route, w_gate, w_up, w_down, final_gain=None, tf=1408):
    t, d = x.shape
    f = w_gate.shape[2]
    tm, ck, rb = MOE_TILE, MOE_CHUNK, MOE_RBLK
    plan = _moe_plan(route, t)
    n_tiles, p = plan["n_tiles"], plan["p"]
    assert p < 2 ** 24, "row positions are carried exactly in f32"
    meta = jnp.stack([plan["pos1"].astype(F32), plan["pos2"].astype(F32), route[:, 2], route[:, 3]]
                     + [jnp.zeros((t,), F32)] * (SUBLANES - 4))
    pos_lanes = jnp.concatenate([jnp.broadcast_to(plan["pos1"][:, None], (t, LANES)),
                                 jnp.broadcast_to(plan["pos2"][:, None], (t, LANES))], axis=1)

    xs, w_row = pl.pallas_call(
        _moe_gather_body,
        grid_spec=pltpu.PrefetchScalarGridSpec(
            num_scalar_prefetch=5,
            grid=(plan["gather"][0].shape[0],),
            in_specs=[pl.BlockSpec((SUBLANES, ck), lambda w, tl, ch, fi, la, va: (0, ch[w])),
                      pl.BlockSpec((ck, d), lambda w, tl, ch, fi, la, va: (ch[w], 0))],
            out_specs=[pl.BlockSpec((MOE_GTILE, d), lambda w, tl, ch, fi, la, va: (tl[w], 0)),
                       pl.BlockSpec((MOE_GTILE, LANES), lambda w, tl, ch, fi, la, va: (tl[w], 0))],
            scratch_shapes=[pltpu.VMEM((MOE_GTILE, d), F32), pltpu.VMEM((MOE_GTILE, 1), F32)]),
        out_shape=[jax.ShapeDtypeStruct((p, d), BF16), jax.ShapeDtypeStruct((p, LANES), F32)],
        compiler_params=_cparams("arbitrary"),
        name="moe_gather",
    )(*plan["gather"], meta, h)

    ys = pl.pallas_call(
        _moe_ffn_body,
        grid_spec=pltpu.PrefetchScalarGridSpec(
            num_scalar_prefetch=1,
            grid=(n_tiles, f // tf),
            in_specs=[pl.BlockSpec((tm, d), lambda i, k, ex: (i, 0)),
                      pl.BlockSpec((tm, LANES), lambda i, k, ex: (i, 0)),
                      pl.BlockSpec((None, d, tf), lambda i, k, ex: (ex[i], 0, k)),
                      pl.BlockSpec((None, d, tf), lambda i, k, ex: (ex[i], 0, k)),
                      pl.BlockSpec((None, tf, d), lambda i, k, ex: (ex[i], k, 0))],
            out_specs=pl.BlockSpec((tm, d), lambda i, k, ex: (i, 0)),
            scratch_shapes=[pltpu.VMEM((tm, d), F32)]),
        out_shape=jax.ShapeDtypeStruct((p, d), BF16),
        compiler_params=_cparams("parallel", "arbitrary"),
        name="moe_ffn",
    )(plan["tile_expert"], xs, w_row, w_gate, w_up, w_down)

    final_norm = final_gain is not None
    gain = (final_gain if final_norm else jnp.ones((d,), F32)).reshape(1, d)
    return pl.pallas_call(
        functools.partial(_moe_combine_body, final_norm=final_norm),
        grid_spec=pltpu.PrefetchScalarGridSpec(
            num_scalar_prefetch=5,
            grid=(plan["combine"][0].shape[0],),
            in_specs=[pl.BlockSpec((ck, d), lambda w, tl, bk, fi, la, va: (tl[w], 0)),
                      pl.BlockSpec((ck, 2 * LANES), lambda w, tl, bk, fi, la, va: (tl[w], 0)),
                      pl.BlockSpec((rb, d), lambda w, tl, bk, fi, la, va: (bk[w], 0)),
                      pl.BlockSpec((1, d), lambda w, tl, bk, fi, la, va: (0, 0))],
            out_specs=pl.BlockSpec((ck, d), lambda w, tl, bk, fi, la, va: (tl[w], 0)),
            scratch_shapes=[pltpu.VMEM((ck, d), F32)]),
        out_shape=jax.ShapeDtypeStruct((t, d), F32),
        compiler_params=_cparams("arbitrary"),
        name="moe_combine",
    )(*plan["combine"], x, pos_lanes, ys, gain)


def _rmsnorm_body(x_ref, g_ref, o_ref):
    o_ref[...] = _rms(x_ref[...], g_ref[...])


def rmsnorm(x, g, tm=1024):
    t, d = x.shape
    tm = min(tm, t)
    return pl.pallas_call(
        _rmsnorm_body,
        grid=(t // tm,),
        in_specs=[pl.BlockSpec((tm, d), lambda i: (i, 0)), pl.BlockSpec((1, d), lambda i: (0, 0))],
        out_specs=pl.BlockSpec((tm, d), lambda i: (i, 0)),
        out_shape=jax.ShapeDtypeStruct((t, d), F32),
        compiler_params=_cparams("parallel"),
        name="rmsnorm",
    )(x, g.reshape(1, d))


def _in_proj_slices(w):
    sizes = (MOBA_WIDTH, MOBA_WIDTH, MOBA_WIDTH, SSM_WIDTH, GDN_WIDTH, GDN_WIDTH, GDN_WIDTH,
             GDN_HEADS, GDN_HEADS, GDN_WIDTH, w.shape[0], w.shape[0], w.shape[0])
    parts, start = [], 0
    for size in sizes:
        parts.append(w[:, start:start + size])
        start += size
    return parts


def kernel(x, mem, positions, norm_mix, w_in, ssm_a_re, ssm_a_im, ssm_log_dt, ssm_b_re, ssm_b_im, ssm_c_re, ssm_c_im, ssm_d, ssm_w_glu, gdn_conv, gdn_a_log, gdn_dt_bias, gdn_norm, w_up_moba, w_up_gdn, w_out, norm_xa, norm_mem, xa_wq, xa_wk, xa_wv, xa_wo, norm_ffn, ffn_w_gate, ffn_w_up, ffn_w_down, moe_w_router, moe_w_gate, moe_w_up, moe_w_down, norm_final):
    batch, seq, d = x.shape
    depth = w_in.shape[0]
    t = batch * seq
    assert batch == SUBLANES, "the S5 scan packs the batch onto the 8 sublanes of a vreg"
    assert seq % MOBA_BLOCK == 0 and seq % GDN_CHUNK == 0
    nc = seq // GDN_CHUNK
    bf = lambda a: a.astype(BF16)

    xf = x.reshape(t, d)
    memf = mem.reshape(-1, d)
    cos, sin = rope_tables(positions)

    for l in range(depth):
        (wq_m, wk_m, wv_m, wu_s, wq_g, wk_g, wv_g, wa_g, wb_g, wz_g, wg_a, wg_b, wg_c) = _in_proj_slices(w_in[l])
        g_mix = norm_mix[l]
        w_ab = jnp.pad(jnp.concatenate([wa_g, wb_g], axis=1), ((0, 0), (0, LANES - 2 * GDN_HEADS)))
        w32 = jnp.concatenate([wq_m, wk_m, wu_s, wq_g, wk_g, wv_g, w_ab], axis=1)
        w16 = jnp.concatenate([wv_m, wz_g, wg_a, wg_b, wg_c], axis=1)
        p32 = norm_matmul(xf, g_mix, bf(w32), F32, tm=2048, tn=640)
        p16 = norm_matmul(xf, g_mix, bf(w16), BF16, tm=2048, tn=1024)
        c_us = 2 * MOBA_WIDTH
        c_qkv = c_us + SSM_WIDTH
        c_ab = c_qkv + 3 * GDN_WIDTH
        u_s = p32[:, c_us:c_us + SSM_WIDTH]
        ab_g = p32[:, c_ab:c_ab + 2 * GDN_HEADS]

        y_a = moba_attention(p32, p16, cos, sin, batch, seq, q_col=0, k_col=MOBA_WIDTH // LANES, v_col=0)

        l_re, l_im, bb_re, bb_im = s5_discretise(ssm_a_re[l], ssm_a_im[l], ssm_log_dt[l], ssm_b_re[l], ssm_b_im[l])
        maps = s5_block_maps(bb_re, bb_im, ssm_c_re[l], ssm_c_im[l])
        u_tm = u_s.reshape(batch, seq, SSM_WIDTH).transpose(1, 0, 2).reshape(t, SSM_WIDTH)
        y_s_tm = s5_gelu(u_tm, maps, l_re, l_im, ssm_d[l])
        y_s = y_s_tm.reshape(seq, batch, SSM_WIDTH).transpose(1, 0, 2).reshape(t, SSM_WIDTH)

        def head_rows(cols):
            return cols.reshape(batch, nc, GDN_CHUNK, GDN_HEADS).transpose(0, 3, 1, 2)
        a_rows = head_rows(ab_g[:, :GDN_HEADS])
        b_rows = head_rows(ab_g[:, GDN_HEADS:2 * GDN_HEADS])
        y_c = gdn_mixer(p32, p16, a_rows, b_rows, gdn_conv[l], gdn_a_log[l], gdn_dt_bias[l], gdn_norm[l],
                        batch, seq, qkv_col=c_qkv // LANES, z_col=MOBA_WIDTH // GDN_WIDTH)

        xf = merge_project(xf, y_a, y_s, y_c, p16, bf(w_up_moba[l]), bf(ssm_w_glu[l]), bf(w_up_gdn[l]),
                           bf(w_out[l]), gate_col=(MOBA_WIDTH + GDN_WIDTH) // d)

        k_x = norm_matmul(memf, norm_mem[l], bf(xa_wk[l]), BF16)
        v_x = norm_matmul(memf, norm_mem[l], bf(xa_wv[l]), BF16)
        xf = cross_attention(xf, norm_xa[l], bf(xa_wq[l]), k_x, v_x, bf(xa_wo[l]), batch, seq)

        if l % 2 == 0:
            i = l // 2
            xf = ffn(xf, norm_ffn[l], bf(ffn_w_gate[i]), bf(ffn_w_up[i]), bf(ffn_w_down[i]))
        else:
            i = l // 2
            route, h_moe = moe_router(xf, norm_ffn[l], moe_w_router[i])
            xf = moe_ffn(xf, h_moe, route, bf(moe_w_gate[i]), bf(moe_w_up[i]), bf(moe_w_down[i]),
                         final_gain=norm_final if l == depth - 1 else None)

    if depth % 2 == 1:
        xf = rmsnorm(xf, norm_final)
    return xf.reshape(batch, seq, d)
```

```python
import functools
import math

import numpy as np
import jax
import jax.numpy as jnp
from jax import lax
from jax.experimental import pallas as pl
from jax.experimental.pallas import tpu as pltpu

F32 = jnp.float32
BF16 = jnp.bfloat16
HIGHEST = lax.Precision.HIGHEST

EPS = 1e-6
NEG_INF = -1e30
MOBA_HEADS = 8
MOBA_HEAD_DIM = 64
MOBA_WIDTH = MOBA_HEADS * MOBA_HEAD_DIM
MOBA_BLOCK = 256
MOBA_TOPK = 3
ROPE_THETA = 10000.0
SSM_WIDTH = 512
SSM_GROUP = 16
SSM_GROUPS = SSM_WIDTH // SSM_GROUP
SSM_STATE = 64
SSM_STATES = SSM_GROUPS * SSM_STATE
GDN_HEADS = 4
GDN_HEAD_DIM = 128
GDN_WIDTH = GDN_HEADS * GDN_HEAD_DIM
GDN_CONV = 4
GDN_CHUNK = 64
XA_HEADS = 4
XA_HEAD_DIM = 128
XA_WIDTH = XA_HEADS * XA_HEAD_DIM
N_EXPERTS = 8
TOP_K = 2

LANES = 128
SUBLANES = 8
VMEM_LIMIT = 56 * 1024 * 1024


def _cparams(*sem):
    return pltpu.CompilerParams(dimension_semantics=sem, vmem_limit_bytes=VMEM_LIMIT)


def _rms(x, g):
    return x * lax.rsqrt(jnp.mean(x * x, axis=-1, keepdims=True) + EPS) * g


def _sigmoid(x):
    return 1.0 / (1.0 + jnp.exp(-x))


def _silu(x):
    return x * _sigmoid(x)


def _dot(a, b):
    return jnp.dot(a, b, preferred_element_type=F32)


def _dot_nt(a, b):
    return lax.dot_general(a, b, (((1,), (1,)), ((), ())), preferred_element_type=F32)


def _dot_tn(a, b):
    return lax.dot_general(a, b, (((0,), (0,)), ((), ())), preferred_element_type=F32)


def _norm_matmul_body(x_ref, g_ref, w_ref, o_ref, h_ref):
    @pl.when(pl.program_id(1) == 0)
    def _():
        h_ref[...] = _rms(x_ref[...], g_ref[...]).astype(BF16)

    o_ref[...] = _dot(h_ref[...], w_ref[...]).astype(o_ref.dtype)


def norm_matmul(x, g, w, out_dtype, tm=512, tn=512):
    t, k = x.shape
    n = w.shape[1]
    tm, tn = min(tm, t), min(tn, n)
    return pl.pallas_call(
        _norm_matmul_body,
        grid=(t // tm, n // tn),
        in_specs=[pl.BlockSpec((tm, k), lambda i, j: (i, 0)),
                  pl.BlockSpec((1, k), lambda i, j: (0, 0)),
                  pl.BlockSpec((k, tn), lambda i, j: (0, j))],
        out_specs=pl.BlockSpec((tm, tn), lambda i, j: (i, j)),
        out_shape=jax.ShapeDtypeStruct((t, n), out_dtype),
        scratch_shapes=[pltpu.VMEM((tm, k), BF16)],
        compiler_params=_cparams("parallel", "arbitrary"),
        name="norm_matmul",
    )(x, g.reshape(1, k), w)


def _norm_matmul2_body(x_ref, g_ref, wa_ref, wb_ref, oa_ref, ob_ref, h_ref, *, na):
    j = pl.program_id(1)

    @pl.when(j == 0)
    def _():
        h_ref[...] = _rms(x_ref[...], g_ref[...]).astype(BF16)

    @pl.when(j < na)
    def _():
        oa_ref[...] = _dot(h_ref[...], wa_ref[...]).astype(oa_ref.dtype)

    @pl.when(j >= na)
    def _():
        ob_ref[...] = _dot(h_ref[...], wb_ref[...]).astype(ob_ref.dtype)


def norm_matmul2(x, g, wa, wb, dtype_a, dtype_b, tm, tna, tnb):
    t, k = x.shape
    tm = min(tm, t)
    na, nb = wa.shape[1] // tna, wb.shape[1] // tnb
    return pl.pallas_call(
        functools.partial(_norm_matmul2_body, na=na),
        grid=(t // tm, na + nb),
        in_specs=[pl.BlockSpec((tm, k), lambda i, j: (i, 0)),
                  pl.BlockSpec((1, k), lambda i, j: (0, 0)),
                  pl.BlockSpec((k, tna), lambda i, j: (0, jnp.minimum(j, na - 1))),
                  pl.BlockSpec((k, tnb), lambda i, j: (0, jnp.maximum(j - na, 0)))],
        out_specs=[pl.BlockSpec((tm, tna), lambda i, j: (i, jnp.minimum(j, na - 1))),
                   pl.BlockSpec((tm, tnb), lambda i, j: (i, jnp.maximum(j - na, 0)))],
        out_shape=[jax.ShapeDtypeStruct((t, wa.shape[1]), dtype_a), jax.ShapeDtypeStruct((t, wb.shape[1]), dtype_b)],
        scratch_shapes=[pltpu.VMEM((tm, k), BF16)],
        compiler_params=_cparams("parallel", "arbitrary"),
        name="norm_matmul2",
    )(x, g.reshape(1, k), wa, wb)


def _rope_body(pos_ref, inv_ref, cos_ref, sin_ref):
    ang = pos_ref[...] * inv_ref[...]
    lane = lax.broadcasted_iota(jnp.int32, ang.shape, 1)
    first_half = (lane % MOBA_HEAD_DIM) < (MOBA_HEAD_DIM // 2)
    cos_ref[...] = jnp.cos(ang)
    s = jnp.sin(ang)
    sin_ref[...] = jnp.where(first_half, -s, s)


def rope_tables(positions):
    t = positions.size
    half = MOBA_HEAD_DIM // 2
    inv = (1.0 / (np.float32(ROPE_THETA) ** (np.arange(0, MOBA_HEAD_DIM, 2, dtype=np.float32)
                                             / np.float32(MOBA_HEAD_DIM)))).astype(np.float32)
    inv_row = jnp.asarray(np.tile(inv, LANES // half).reshape(1, LANES))
    pos = positions.astype(F32).reshape(t, 1)
    ts = min(1024, t)
    return pl.pallas_call(
        _rope_body,
        grid=(t // ts,),
        in_specs=[pl.BlockSpec((ts, 1), lambda i: (i, 0)),
                  pl.BlockSpec((1, LANES), lambda i: (0, 0))],
        out_specs=[pl.BlockSpec((ts, LANES), lambda i: (i, 0))] * 2,
        out_shape=[jax.ShapeDtypeStruct((t, LANES), F32)] * 2,
        compiler_params=_cparams("parallel"),
        name="rope_tables",
    )(pos, inv_row)


def _rope_apply(x, cos, sin_signed):
    lane = lax.broadcasted_iota(jnp.int32, x.shape, 1)
    first_half = (lane % MOBA_HEAD_DIM) < (MOBA_HEAD_DIM // 2)
    partner = jnp.where(first_half, pltpu.roll(x, LANES - MOBA_HEAD_DIM // 2, 1),
                        pltpu.roll(x, MOBA_HEAD_DIM // 2, 1))
    return x * cos + partner * sin_signed


def _moba_body(q_ref, k_ref, v_ref, cq_ref, sq_ref, ck_ref, sk_ref, o_ref, ka_ref, va_ref, km_ref, s_ref,
               mt_ref, acc_ref, *, nb):
    blk = MOBA_BLOCK
    i = pl.program_id(2)

    @pl.when(i == 0)
    def _():
        km_ref[...] = jnp.zeros_like(km_ref)
        lane_b = lax.broadcasted_iota(jnp.int32, (blk, LANES), 1)

        def rope_blk(j, c):
            rows = pl.ds(pl.multiple_of(j * blk, blk), blk)
            kr = _rope_apply(k_ref[rows, :], ck_ref[rows, :], sk_ref[rows, :])
            ka_ref[rows, 0:LANES] = kr.astype(BF16)
            ka_ref[rows, LANES:2 * LANES] = jnp.where(lane_b == j, 1.0, 0.0).astype(BF16)
            km_ref[pl.ds(j, 1), :] = jnp.mean(kr, axis=0, keepdims=True)
            v = v_ref[rows, :].astype(F32)
            va_ref[0, rows, :] = jnp.where(lane_b < MOBA_HEAD_DIM, v,
                                           jnp.where(lane_b == MOBA_HEAD_DIM, 1.0, 0.0)).astype(BF16)
            va_ref[1, rows, :] = jnp.where(lane_b >= MOBA_HEAD_DIM, v,
                                           jnp.where(lane_b == 0, 1.0, 0.0)).astype(BF16)
            return c
        lax.fori_loop(0, nb, rope_blk, 0)

    tq = 2 * blk
    lane = lax.broadcasted_iota(jnp.int32, (1, LANES), 1)
    head_a = lane < MOBA_HEAD_DIM
    q = _rope_apply(q_ref[...], cq_ref[...], sq_ref[...])
    scale = MOBA_HEAD_DIM ** -0.5
    km = km_ref[...]
    nbp = km_ref.shape[0]
    brow = lax.broadcasted_iota(jnp.int32, (nbp, tq), 0)
    qblk = 2 * i + jnp.where(lax.broadcasted_iota(jnp.int32, (1, tq), 1) >= blk, 1, 0)
    heads = (0, 1)
    q_heads = [jnp.where(head_a, q, 0.0), jnp.where(head_a, 0.0, q)]
    gates = [lax.dot_general(km, qh, (((1,), (1,)), ((), ())), preferred_element_type=F32, precision=HIGHEST)
             for qh in q_heads]
    q_augs = []
    brow_f = brow.astype(F32)
    for h in heads:
        gate = jnp.where(brow < qblk, gates[h], NEG_INF)
        picked = jnp.zeros((nbp, tq), F32)
        for _ in range(MOBA_TOPK):
            top = jnp.max(gate, axis=0, keepdims=True)
            first = jnp.min(jnp.where(gate == top, brow_f, float(nbp)), axis=0, keepdims=True)
            hit = brow_f == first
            picked = jnp.where(hit, 1.0, picked)
            gate = jnp.where(hit, -jnp.inf, gate)
        keep = ((picked > 0.5) & (brow < qblk)) | (brow == qblk)
        bias_t = jnp.concatenate([jnp.where(keep, 0.0, NEG_INF), jnp.zeros((LANES - nbp, tq), F32)],
                                 axis=0)
        q_augs.append(jnp.concatenate([q_heads[h] * scale, bias_t.T], axis=1).astype(BF16))

    def pair_rows(jp):
        return pl.ds(pl.multiple_of(jp * tq, tq), tq)

    r_idx = lax.broadcasted_iota(jnp.int32, (tq, tq), 0)
    c_idx = lax.broadcasted_iota(jnp.int32, (tq, tq), 1)
    k_diag = ka_ref[pair_rows(i), :]
    s_diag = [_dot_nt(q_augs[h], k_diag) for h in heads]
    for h in heads:
        s_d = jnp.where(c_idx <= r_idx, s_diag[h], NEG_INF)
        s_ref[h, i] = s_d
        m_t = s_d[:, 0:LANES]
        for col in range(LANES, tq, LANES):
            m_t = jnp.maximum(m_t, s_d[:, col:col + LANES])
        mt_ref[h] = m_t

    def loop2(n, body):
        def two(k, c):
            body([2 * k, 2 * k + 1])
            return c
        lax.fori_loop(0, lax.shift_right_logical(n, 1), two, 0)

        @pl.when((n & 1) == 1)
        def _():
            body([n - 1])

    def scores(jps):
        k2s = [ka_ref[pair_rows(jp), :] for jp in jps]
        s2s = [[_dot_nt(q_augs[h], k2) for h in heads] for k2 in k2s]
        for jp, s2h in zip(jps, s2s):
            for h in heads:
                s_ref[h, jp] = s2h[h]
        for h in heads:
            m_t = mt_ref[h]
            for s2h in s2s:
                for col in range(0, tq, LANES):
                    m_t = jnp.maximum(m_t, s2h[h][:, col:col + LANES])
            mt_ref[h] = m_t
    loop2(i, scores)

    ms = [jnp.max(mt_ref[h], axis=-1, keepdims=True) for h in heads]
    for h in heads:
        acc_ref[h] = jnp.zeros((tq, LANES), F32)

    def values(jps):
        ps = [[jnp.exp(s_ref[h, jp] - ms[h]).astype(BF16) for h in heads] for jp in jps]
        pvs = [[_dot(p[h], va_ref[h, pair_rows(jp), :]) for h in heads] for jp, p in zip(jps, ps)]
        for h in heads:
            tot = pvs[0][h]
            for pv in pvs[1:]:
                tot = tot + pv[h]
            acc_ref[h] += tot
    loop2(i + 1, values)
    acc_a, acc_b = acc_ref[0], acc_ref[1]
    out_a = acc_a / acc_a[:, MOBA_HEAD_DIM:MOBA_HEAD_DIM + 1]
    out_b = acc_b / acc_b[:, 0:1]
    o_ref[...] = jnp.where(head_a, out_a, out_b).astype(o_ref.dtype)


def moba_attention(qk, v, cos, sin, batch, seq, q_col=0, k_col=MOBA_WIDTH // LANES, v_col=0):
    nb = seq // MOBA_BLOCK
    assert nb % 2 == 0
    npair = nb // 2
    pairs = MOBA_WIDTH // LANES
    tq = 2 * MOBA_BLOCK
    t = batch * seq
    return pl.pallas_call(
        functools.partial(_moba_body, nb=nb),
        grid=(batch, pairs, npair),
        in_specs=[pl.BlockSpec((tq, LANES), lambda b, p, i: (b * npair + i, q_col + p)),
                  pl.BlockSpec((seq, LANES), lambda b, p, i: (b, k_col + p)),
                  pl.BlockSpec((seq, LANES), lambda b, p, i: (b, v_col + p)),
                  pl.BlockSpec((tq, LANES), lambda b, p, i: (b * npair + i, 0)),
                  pl.BlockSpec((tq, LANES), lambda b, p, i: (b * npair + i, 0)),
                  pl.BlockSpec((seq, LANES), lambda b, p, i: (b, 0)),
                  pl.BlockSpec((seq, LANES), lambda b, p, i: (b, 0))],
        out_specs=pl.BlockSpec((tq, LANES), lambda b, p, i: (b * npair + i, p)),
        out_shape=jax.ShapeDtypeStruct((t, MOBA_WIDTH), BF16),
        scratch_shapes=[pltpu.VMEM((seq, 2 * LANES), BF16),
                        pltpu.VMEM((2, seq, LANES), BF16),
                        pltpu.VMEM((-(-nb // SUBLANES) * SUBLANES, LANES), F32),
                        pltpu.VMEM((2, npair, tq, tq), F32),
                        pltpu.VMEM((2, tq, LANES), F32),
                        pltpu.VMEM((2, tq, LANES), F32)],
        compiler_params=_cparams("parallel", "parallel", "arbitrary"),
        name="moba_attention",
    )(qk, qk, v, cos, sin, cos, sin)


def _s5_disc_body(are_ref, aim_ref, ldt_ref, bre_ref, bim_ref, lre_ref, lim_ref, bbre_ref, bbim_ref):
    a_re, a_im = are_ref[...], aim_ref[...]
    dt = jnp.exp(ldt_ref[...])
    mag = jnp.exp(a_re * dt)
    l_re = mag * jnp.cos(a_im * dt)
    l_im = mag * jnp.sin(a_im * dt)
    lre_ref[...] = l_re
    lim_ref[...] = l_im
    x, y = l_re - 1.0, l_im
    den = a_re * a_re + a_im * a_im
    c_re = (x * a_re + y * a_im) / den
    c_im = (y * a_re - x * a_im) / den
    b_re, b_im = bre_ref[...], bim_ref[...]
    bbre_ref[...] = c_re * b_re - c_im * b_im
    bbim_ref[...] = c_re * b_im + c_im * b_re


def s5_discretise(a_re, a_im, log_dt, b_re, b_im):
    g, p = a_re.shape
    n = b_re.shape[-1]
    col = lambda a: a.reshape(g * p, 1)
    ldt = jnp.broadcast_to(log_dt[:, None], (g, p))
    outs = pl.pallas_call(
        _s5_disc_body,
        out_shape=[jax.ShapeDtypeStruct((g * p, 1), F32)] * 2 + [jax.ShapeDtypeStruct((g * p, n), F32)] * 2,
        name="s5_discretise",
    )(col(a_re), col(a_im), col(ldt), b_re.reshape(g * p, n), b_im.reshape(g * p, n))
    l_re, l_im, bb_re, bb_im = outs
    return l_re.reshape(g, p), l_im.reshape(g, p), bb_re.reshape(g, p, n), bb_im.reshape(g, p, n)


def _block_diag(blocks):
    g, r, c = blocks.shape
    eye = jnp.eye(g, dtype=blocks.dtype)
    return (blocks[:, :, None, :] * eye[:, None, :, None]).reshape(g * r, g * c)


def _s5_body(u_ref, bre_ref, bim_ref, cre_ref, cim_ref, lre_ref, lim_ref, d_ref, y_ref, h_ref, hb_ref, *, steps):
    rows = SUBLANES
    ns = SSM_STATES
    nq = SSM_WIDTH // LANES
    sq = ns // nq

    @pl.when(pl.program_id(0) == 0)
    def _():
        h_ref[...] = jnp.zeros_like(h_ref)

    u = u_ref[...]
    u16 = u.astype(BF16)
    for q in range(nq):
        uq = u16[:, q * LANES:(q + 1) * LANES]
        hb_ref[:, q * sq:(q + 1) * sq] = _dot(uq, bre_ref[q])
        hb_ref[:, ns + q * sq:ns + (q + 1) * sq] = _dot(uq, bim_ref[q])
    l_re, l_im = lre_ref[...], lim_ref[...]

    def step(t, carry):
        h_re, h_im = carry
        r = pl.ds(pl.multiple_of(t * rows, rows), rows)
        n_re = l_re * h_re - l_im * h_im + hb_ref[r, 0:ns]
        n_im = l_re * h_im + l_im * h_re + hb_ref[r, ns:2 * ns]
        hb_ref[r, 0:ns] = n_re
        hb_ref[r, ns:2 * ns] = n_im
        return n_re, n_im

    h_re, h_im = lax.fori_loop(0, steps, step, (h_ref[:, 0:ns], h_ref[:, ns:2 * ns]))
    h_ref[:, 0:ns] = h_re
    h_ref[:, ns:2 * ns] = h_im
    ys = []
    for q in range(nq):
        ys.append(_dot(hb_ref[:, q * sq:(q + 1) * sq].astype(BF16), cre_ref[q])
                  + _dot(hb_ref[:, ns + q * sq:ns + (q + 1) * sq].astype(BF16), cim_ref[q]))
    y = jnp.concatenate(ys, axis=1) + d_ref[...] * u
    y_ref[...] = jax.nn.gelu(y).astype(y_ref.dtype)


def s5_block_maps(bb_re, bb_im, c_re, c_im):
    nq = SSM_WIDTH // LANES
    gq = SSM_GROUPS // nq
    per_q = lambda a: jnp.stack([_block_diag(a[q * gq:(q + 1) * gq].transpose(0, 2, 1)) for q in range(nq)])
    return (per_q(bb_re).astype(BF16), per_q(bb_im).astype(BF16),
            per_q(c_re).astype(BF16), per_q(-c_im).astype(BF16))


def s5_gelu(u_tm, maps, l_re, l_im, d_skip, steps=64):
    rows = u_tm.shape[0]
    blk = steps * SUBLANES
    ns2 = 2 * SSM_STATES
    full3 = lambda a: pl.BlockSpec(a.shape, lambda c: (0, 0, 0))
    return pl.pallas_call(
        functools.partial(_s5_body, steps=steps),
        grid=(rows // blk,),
        in_specs=[pl.BlockSpec((blk, SSM_WIDTH), lambda c: (c, 0)),
                  full3(maps[0]), full3(maps[1]), full3(maps[2]), full3(maps[3]),
                  pl.BlockSpec((1, SSM_STATES), lambda c: (0, 0)),
                  pl.BlockSpec((1, SSM_STATES), lambda c: (0, 0)),
                  pl.BlockSpec((1, SSM_WIDTH), lambda c: (0, 0))],
        out_specs=pl.BlockSpec((blk, SSM_WIDTH), lambda c: (c, 0)),
        out_shape=jax.ShapeDtypeStruct((rows, SSM_WIDTH), BF16),
        scratch_shapes=[pltpu.VMEM((SUBLANES, ns2), F32),
                        pltpu.VMEM((blk, ns2), F32)],
        compiler_params=_cparams("arbitrary"),
        name="s5_scan",
    )(u_tm, *maps, l_re.reshape(1, -1), l_im.reshape(1, -1), d_skip.reshape(1, -1))


GDN_GROUP = 16


def _gdn_prep_body(q_ref, k_ref, v_ref, a_ref, b_ref, wq_ref, wk_ref, wv_ref, alog_ref, dtb_ref,
                   u_ref, w_ref, qd_ref, kd_ref, qk_ref, gl_ref, qs_ref, ks_ref, vs_ref,
                   *, seq):
    L = GDN_CHUNK
    G = min(GDN_GROUP, seq // L)
    dk = GDN_HEAD_DIM
    nc = seq // L
    hd = pl.program_id(1)
    cb = 256
    halo = SUBLANES

    def conv_blk(rb, c):
        base = pl.multiple_of(rb * cb, cb)
        prev = pl.multiple_of(jnp.maximum(base - halo, 0), halo)
        for src, wref, dst, norm, mul in ((q_ref, wq_ref, qs_ref, True, dk ** -0.5),
                                          (k_ref, wk_ref, ks_ref, True, 1.0),
                                          (v_ref, wv_ref, vs_ref, False, 1.0)):
            cur = src[pl.ds(base, cb), :]
            head = jnp.where(rb > 0, src[pl.ds(prev, halo), :], 0.0)
            ext = jnp.concatenate([head, cur], axis=0)
            wts = wref[...]
            y = ext[halo:halo + cb] * wts[GDN_CONV - 1:GDN_CONV]
            for tap in range(1, GDN_CONV):
                y = y + ext[halo - tap:halo - tap + cb] * wts[GDN_CONV - 1 - tap:GDN_CONV - tap]
            y = _silu(y)
            if norm:
                y = y * lax.rsqrt(jnp.sum(y * y, axis=-1, keepdims=True) + EPS) * mul
            dst[pl.ds(base, cb), :] = y
        return c
    lax.fori_loop(0, seq // cb, conv_blk, 0)

    gl_rows = G * L
    ri = lax.broadcasted_iota(jnp.int32, (gl_rows, L), 0) % L
    ci = lax.broadcasted_iota(jnp.int32, (gl_rows, L), 1)
    tril = ci <= ri
    strict = ci < ri
    eye = ci == ri
    ri3 = lax.broadcasted_iota(jnp.int32, (G, L, L), 1)
    ci3 = lax.broadcasted_iota(jnp.int32, (G, L, L), 2)
    strict3 = ci3 < ri3
    eye_f = jnp.where(ci3 == ri3, 1.0, 0.0)
    r2 = lax.broadcasted_iota(jnp.int32, (L, L), 0)
    c2 = lax.broadcasted_iota(jnp.int32, (L, L), 1)
    upper_f = jnp.where(r2 <= c2, 1.0, 0.0)
    a_coef = -jnp.exp(jnp.full((1, L), alog_ref[hd], F32))
    dt_bias = dtb_ref[hd]
    hp = functools.partial(jnp.dot, preferred_element_type=F32, precision=HIGHEST)

    def bmm(a, b):
        return jnp.einsum('gij,gjk->gik', a, b, preferred_element_type=F32)

    def bmm_nt(a, b):
        return jnp.einsum('gid,gjd->gij', a, b, preferred_element_type=F32)

    def to_col(rows_b):
        return jnp.sum(jnp.where(eye, rows_b, 0.0), axis=1, keepdims=True)

    def prep(cg, carry):
        c0 = pl.multiple_of(cg * G, G)
        rows = pl.ds(pl.multiple_of(cg * gl_rows, gl_rows), gl_rows)
        qc, kc, vc = qs_ref[rows, :], ks_ref[rows, :], vs_ref[rows, :]
        z = a_ref[pl.ds(c0, G), :] + dt_bias
        softplus = jnp.maximum(z, 0.0) + jnp.log(1.0 + jnp.exp(-jnp.abs(z)))
        g_rows = hp(a_coef * softplus, upper_f)
        beta_rows = _sigmoid(b_ref[pl.ds(c0, G), :])
        spread = lambda r: jnp.concatenate([jnp.broadcast_to(r[c:c + 1, :], (L, L)) for c in range(G)], axis=0)
        g_rb = spread(g_rows)
        g_col = to_col(g_rb)
        beta_col = to_col(spread(beta_rows))
        g_last = g_rb[:, L - 1:L]
        decay = jnp.where(tril, jnp.exp(jnp.where(tril, g_col - g_rb, 0.0)), 0.0)
        k_beta = kc * beta_col
        v_beta = vc * beta_col
        to3 = lambda a: a.reshape(G, L, a.shape[-1])
        kb16, k16, q16 = to3(k_beta.astype(BF16)), to3(kc.astype(BF16)), to3(qc.astype(BF16))
        kk_qk = bmm_nt(jnp.concatenate([kb16, q16], axis=1), k16)
        decay3 = to3(decay)
        a_low = jnp.where(strict3, kk_qk[:, :L] * decay3, 0.0)
        qk = kk_qk[:, L:] * decay3
        t_inv = eye_f - a_low
        a16 = a_low.astype(BF16)
        pw = bmm(a16, a16)
        span = 2
        while 2 * span < L:
            pw16 = pw.astype(BF16)
            both = bmm(jnp.concatenate([t_inv.astype(BF16), pw16], axis=1), pw16)
            t_inv = t_inv + both[:, :L]
            pw = both[:, L:]
            span *= 2
        t_inv = t_inv + bmm(t_inv.astype(BF16), pw.astype(BF16))
        rhs = jnp.concatenate([v_beta.astype(BF16), (k_beta * jnp.exp(g_col)).astype(BF16)], axis=1)
        uw = bmm(t_inv.astype(BF16), to3(rhs)).reshape(gl_rows, 2 * dk)
        u_ref[rows, :] = uw[:, :dk]
        w_ref[rows, :] = uw[:, dk:].astype(BF16)
        qd_ref[rows, :] = (qc * jnp.exp(g_col)).astype(BF16)
        k_dec = to3(kc * jnp.exp(g_last - g_col))
        kd_ref[pl.ds(c0, G)] = jnp.swapaxes(k_dec, 1, 2).astype(BF16)
        qk_ref[pl.ds(c0, G)] = qk.astype(BF16)
        gl_ref[pl.ds(c0, G), :] = jnp.broadcast_to(jnp.exp(g_rows[:, L - 1:L]), (G, LANES))
        return carry
    lax.fori_loop(0, nc // G, prep, 0)


def _gdn_scan_body(u_ref, w_ref, qd_ref, kd_ref, qk_ref, gl_ref, z_ref, gn_ref, o_ref, *st_refs, chunks):
    L = GDN_CHUNK
    d = GDN_HEAD_DIM
    nrow = u_ref.shape[0]

    @pl.when(pl.program_id(1) == 0)
    def _():
        for st in st_refs:
            st[...] = jnp.zeros_like(st)

    gn = gn_ref[...]

    def chunk(c, carry):
        rows = pl.ds(pl.multiple_of(c * L, L), L)
        chains = [(r, h) for r in range(nrow) for h in range(GDN_HEADS)]
        states = [st_refs[r * GDN_HEADS + h][...] for r, h in chains]
        firsts = []
        for (r, h), s in zip(chains, states):
            cols = slice(h * d, (h + 1) * d)
            firsts.append(_dot(jnp.concatenate([w_ref[r, rows, cols], qd_ref[r, rows, cols]], axis=0),
                               s.astype(BF16)))
        vns = []
        for (r, h), both in zip(chains, firsts):
            cols = slice(h * d, (h + 1) * d)
            vns.append((u_ref[r, rows, cols] - both[:L]).astype(BF16))
        outs = [both[L:] + _dot(qk_ref[r, h, c], vn) for (r, h), both, vn in zip(chains, firsts, vns)]
        for (r, h), s, vn in zip(chains, states, vns):
            st_refs[r * GDN_HEADS + h][...] = s * gl_ref[r, h, pl.ds(c, 1), :] + _dot(kd_ref[r, h, c], vn)
        for (r, h), o in zip(chains, outs):
            cols = slice(h * d, (h + 1) * d)
            o_ref[r, rows, cols] = (_rms(o, gn) * _silu(z_ref[r, rows, cols].astype(F32))).astype(o_ref.dtype)
        return carry
    lax.fori_loop(0, chunks, chunk, 0)


def gdn_mixer(qkv, z, a_rows, b_rows, conv_w, a_log, dt_bias, gnorm, batch, seq, ts=1024, qkv_col=0, z_col=0):
    t = batch * seq
    hds = GDN_HEADS
    nc = seq // GDN_CHUNK
    L = GDN_CHUNK
    d = GDN_HEAD_DIM
    seq_spec = lambda off: pl.BlockSpec((seq, d), lambda b, h: (b, qkv_col + off + h))
    w_spec = lambda off: pl.BlockSpec((GDN_CONV, d), lambda b, h: (0, off + h))
    row_spec = pl.BlockSpec((None, None, nc, L), lambda b, h: (b, h, 0, 0))
    smem = pl.BlockSpec(memory_space=pltpu.SMEM)
    head_out = pl.BlockSpec((seq, d), lambda b, h: (b, h))
    u, w, qd, kd, qk, gl = pl.pallas_call(
        functools.partial(_gdn_prep_body, seq=seq),
        grid=(batch, hds),
        in_specs=[seq_spec(0), seq_spec(hds), seq_spec(2 * hds), row_spec, row_spec,
                  w_spec(0), w_spec(hds), w_spec(2 * hds), smem, smem],
        out_specs=[head_out, head_out, head_out,
                   pl.BlockSpec((None, None, nc, d, L), lambda b, h: (b, h, 0, 0, 0)),
                   pl.BlockSpec((None, None, nc, L, L), lambda b, h: (b, h, 0, 0, 0)),
                   pl.BlockSpec((None, None, nc, LANES), lambda b, h: (b, h, 0, 0))],
        out_shape=[jax.ShapeDtypeStruct((t, GDN_WIDTH), F32),
                   jax.ShapeDtypeStruct((t, GDN_WIDTH), BF16),
                   jax.ShapeDtypeStruct((t, GDN_WIDTH), BF16),
                   jax.ShapeDtypeStruct((batch, hds, nc, d, L), BF16),
                   jax.ShapeDtypeStruct((batch, hds, nc, L, L), BF16),
                   jax.ShapeDtypeStruct((batch, hds, nc, LANES), F32)],
        scratch_shapes=[pltpu.VMEM((seq, d), F32), pltpu.VMEM((seq, d), F32), pltpu.VMEM((seq, d), F32)],
        compiler_params=_cparams("parallel", "parallel"),
        name="gdn_prep",
    )(qkv, qkv, qkv, a_rows, b_rows, conv_w, conv_w, conv_w, a_log, dt_bias)

    ts = min(ts, seq)
    spb = seq // ts
    cps = ts // L
    nrow = 2 if batch % 2 == 0 else 1
    by_row = lambda a: a.reshape(batch, seq, a.shape[-1])
    tok = pl.BlockSpec((nrow, ts, GDN_WIDTH), lambda b, s: (b, s, 0))
    out = pl.pallas_call(
        functools.partial(_gdn_scan_body, chunks=cps),
        grid=(batch // nrow, spb),
        in_specs=[tok, tok, tok,
                  pl.BlockSpec((nrow, hds, cps, d, L), lambda b, s: (b, 0, s, 0, 0)),
                  pl.BlockSpec((nrow, hds, cps, L, L), lambda b, s: (b, 0, s, 0, 0)),
                  pl.BlockSpec((nrow, hds, cps, LANES), lambda b, s: (b, 0, s, 0)),
                  pl.BlockSpec((nrow, ts, GDN_WIDTH), lambda b, s: (b, s, z_col)),
                  pl.BlockSpec((1, d), lambda b, s: (0, 0))],
        out_specs=tok,
        out_shape=jax.ShapeDtypeStruct((batch, seq, GDN_WIDTH), BF16),
        scratch_shapes=[pltpu.VMEM((d, d), F32)] * (nrow * hds),
        compiler_params=_cparams("parallel", "arbitrary"),
        name="gdn_scan",
    )(by_row(u), by_row(w), by_row(qd), kd, qk, gl, by_row(z), gnorm.reshape(1, d))
    return out.reshape(t, GDN_WIDTH)


def _merge_body(x_ref, ya_ref, ys_ref, yc_ref, ga_ref, gb_ref, gc_ref, wa_ref, wg_ref, wc_ref, wo_ref, o_ref):
    d = x_ref.shape[1]
    y_a = _dot(ya_ref[...], wa_ref[...])
    glu = _dot(ys_ref[...], wg_ref[...])
    y_b = glu[:, :d] * _sigmoid(glu[:, d:])
    y_c = _dot(yc_ref[...], wc_ref[...])
    gate = lambda r: _sigmoid(r[...].astype(F32))
    merged = gate(ga_ref) * y_a + gate(gb_ref) * y_b + gate(gc_ref) * y_c
    o_ref[...] = x_ref[...] + _dot(merged.astype(BF16), wo_ref[...])


def merge_project(x, y_a, y_s, y_c, gates, w_a, w_glu, w_c, w_o, tm=512, gate_col=0):
    t, d = x.shape
    row = lambda n: pl.BlockSpec((tm, n), lambda i: (i, 0))
    gate = lambda k: pl.BlockSpec((tm, d), lambda i: (i, gate_col + k))
    full = lambda a: pl.BlockSpec(a.shape, lambda i: (0, 0))
    return pl.pallas_call(
        _merge_body,
        grid=(t // tm,),
        in_specs=[row(d), row(y_a.shape[1]), row(y_s.shape[1]), row(y_c.shape[1]), gate(0), gate(1), gate(2),
                  full(w_a), full(w_glu), full(w_c), full(w_o)],
        out_specs=row(d),
        out_shape=jax.ShapeDtypeStruct((t, d), F32),
        compiler_params=_cparams("parallel"),
        name="merge_project",
    )(x, y_a, y_s, y_c, gates, gates, gates, w_a, w_glu, w_c, w_o)


def _xattn_body(x_ref, g_ref, wq_ref, k_ref, v_ref, wo_ref, o_ref):
    x = x_ref[...]
    h = _rms(x, g_ref[...]).astype(BF16)
    q = _dot(h, wq_ref[...]).astype(BF16)
    cols = [slice(hd * XA_HEAD_DIM, (hd + 1) * XA_HEAD_DIM) for hd in range(XA_HEADS)]
    scores = [_dot_nt(q[:, c], k_ref[:, c]) * (XA_HEAD_DIM ** -0.5) for c in cols]
    probs = []
    for s in scores:
        p = jnp.exp(s - jnp.max(s, axis=-1, keepdims=True))
        probs.append((p / jnp.sum(p, axis=-1, keepdims=True)).astype(BF16))
    outs = [_dot(p, v_ref[:, c]) for p, c in zip(probs, cols)]
    o = jnp.concatenate(outs, axis=1).astype(BF16)
    o_ref[...] = x + _dot(o, wo_ref[...])


def cross_attention(x, g, wq, k, v, wo, batch, seq, tm=512):
    t, d = x.shape
    m = k.shape[0] // batch
    spb = seq // tm
    full = lambda a: pl.BlockSpec(a.shape, lambda i: (0, 0))
    return pl.pallas_call(
        _xattn_body,
        grid=(t // tm,),
        in_specs=[pl.BlockSpec((tm, d), lambda i: (i, 0)),
                  pl.BlockSpec((1, d), lambda i: (0, 0)),
                  full(wq),
                  pl.BlockSpec((m, XA_WIDTH), lambda i: (i // spb, 0)),
                  pl.BlockSpec((m, XA_WIDTH), lambda i: (i // spb, 0)),
                  full(wo)],
        out_specs=pl.BlockSpec((tm, d), lambda i: (i, 0)),
        out_shape=jax.ShapeDtypeStruct((t, d), F32),
        compiler_params=_cparams("parallel"),
        name="cross_attention",
    )(x, g.reshape(1, d), wq, k, v, wo)


def _ffn_body(x_ref, g_ref, wg_ref, wu_ref, wd_ref, o_ref, h_ref, acc_ref):
    f = pl.program_id(1)

    @pl.when(f == 0)
    def _():
        h_ref[...] = _rms(x_ref[...], g_ref[...]).astype(BF16)
        acc_ref[...] = jnp.zeros_like(acc_ref)

    h = h_ref[...]
    act = (_silu(_dot(h, wg_ref[...])) * _dot(h, wu_ref[...])).astype(BF16)
    acc_ref[...] += _dot(act, wd_ref[...])

    @pl.when(f == pl.num_programs(1) - 1)
    def _():
        o_ref[...] = x_ref[...] + acc_ref[...]


def ffn(x, g, w_gate, w_up, w_down, tm=512, tf=1408):
    t, d = x.shape
    f = w_gate.shape[1]
    return pl.pallas_call(
        _ffn_body,
        grid=(t // tm, f // tf),
        in_specs=[pl.BlockSpec((tm, d), lambda i, k: (i, 0)),
                  pl.BlockSpec((1, d), lambda i, k: (0, 0)),
                  pl.BlockSpec((d, tf), lambda i, k: (0, k)),
                  pl.BlockSpec((d, tf), lambda i, k: (0, k)),
                  pl.BlockSpec((tf, d), lambda i, k: (k, 0))],
        out_specs=pl.BlockSpec((tm, d), lambda i, k: (i, 0)),
        out_shape=jax.ShapeDtypeStruct((t, d), F32),
        scratch_shapes=[pltpu.VMEM((tm, d), BF16), pltpu.VMEM((tm, d), F32)],
        compiler_params=_cparams("parallel", "arbitrary"),
        name="ffn",
    )(x, g.reshape(1, d), w_gate, w_up, w_down)


def _router_body(x_ref, g_ref, w_ref, r_ref, h_ref):
    h = _rms(x_ref[...], g_ref[...])
    h_ref[...] = h.astype(BF16)
    logits = jnp.dot(h, w_ref[...], preferred_element_type=F32, precision=HIGHEST)
    lane = lax.broadcasted_iota(jnp.int32, logits.shape, 1).astype(F32)
    logits = jnp.where(lane < N_EXPERTS, logits, -jnp.inf)
    v1 = jnp.max(logits, axis=-1, keepdims=True)
    i1 = jnp.min(jnp.where(logits == v1, lane, float(LANES)), axis=-1, keepdims=True)
    rest = jnp.where(lane == i1, -jnp.inf, logits)
    v2 = jnp.max(rest, axis=-1, keepdims=True)
    i2 = jnp.min(jnp.where(rest == v2, lane, float(LANES)), axis=-1, keepdims=True)
    e2 = jnp.exp(v2 - v1)
    w1 = 1.0 / (1.0 + e2)
    w2 = e2 / (1.0 + e2)
    r_ref[...] = (jnp.where(lane == 0.0, i1, 0.0) + jnp.where(lane == 1.0, i2, 0.0)
                  + jnp.where(lane == 2.0, w1, 0.0) + jnp.where(lane == 3.0, w2, 0.0))


def moe_router(x, g, w_router, tm=512):
    t, d = x.shape
    w = jnp.pad(w_router, ((0, 0), (0, LANES - w_router.shape[1])))
    return pl.pallas_call(
        _router_body,
        grid=(t // tm,),
        in_specs=[pl.BlockSpec((tm, d), lambda i: (i, 0)),
                  pl.BlockSpec((1, d), lambda i: (0, 0)),
                  pl.BlockSpec((d, LANES), lambda i: (0, 0))],
        out_specs=[pl.BlockSpec((tm, LANES), lambda i: (i, 0)), pl.BlockSpec((tm, d), lambda i: (i, 0))],
        out_shape=[jax.ShapeDtypeStruct((t, LANES), F32), jax.ShapeDtypeStruct((t, d), BF16)],
        compiler_params=_cparams("parallel"),
        name="moe_router",
    )(x, g.reshape(1, d), w)


MOE_TILE = 512
MOE_GTILE = 256
MOE_CHUNK = 512
MOE_RBLK = 256


def _count_le(sorted_vals, queries):
    return jnp.sum((sorted_vals[None, :] <= queries[:, None]).astype(jnp.int32), axis=1)


def _moe_plan(route, t):
    e = N_EXPERTS
    tm, ck, rb = MOE_TILE, MOE_CHUNK, MOE_RBLK
    n_tiles = (TOP_K * t) // tm + e
    p = n_tiles * tm
    n_chunks = t // ck
    i32 = jnp.int32
    e1 = route[:, 0].astype(i32)
    e2 = route[:, 1].astype(i32)
    oh1 = jax.nn.one_hot(e1, e, dtype=i32)
    oh2 = jax.nn.one_hot(e2, e, dtype=i32)
    cnt = oh1 + oh2
    incl = jnp.cumsum(cnt, axis=0)
    excl = incl - cnt
    n_e = incl[-1]
    g_e = ((n_e + tm - 1) // tm) * tm
    off_end = jnp.cumsum(g_e)
    off = off_end - g_e
    pos_all = off[None, :] + excl
    pos1 = jnp.sum(pos_all * oh1, axis=1)
    pos2 = jnp.sum(pos_all * oh2, axis=1)
    tile_expert = jnp.minimum(_count_le(off_end, jnp.arange(n_tiles, dtype=i32) * tm), e - 1)
    gm = MOE_GTILE
    n_gt = p // gm
    gt_start = jnp.arange(n_gt, dtype=i32) * gm
    sel = jax.nn.one_hot(jnp.minimum(_count_le(off_end, gt_start), e - 1), e, dtype=i32)
    r0 = gt_start - jnp.sum(sel * off[None, :], axis=1)
    n_sel = jnp.sum(sel * n_e[None, :], axis=1)
    has_rows = r0 < n_sel
    r1 = jnp.minimum(r0 + gm, n_sel) - 1
    ends = incl[ck - 1::ck]
    ends_sel = jnp.sum(ends[None, :, :] * sel[:, None, :], axis=2)
    c_lo = jnp.where(has_rows, jnp.sum((ends_sel <= r0[:, None]).astype(i32), axis=1), 0)
    c_hi = jnp.where(has_rows, jnp.sum((ends_sel <= r1[:, None]).astype(i32), axis=1), 0)
    n_it = c_hi - c_lo + 1
    it_end = jnp.cumsum(n_it)
    it_start = it_end - n_it
    wmax = n_gt + e * n_chunks
    w_idx = jnp.arange(wmax, dtype=i32)
    g_tile = jnp.minimum(_count_le(it_end, w_idx), n_gt - 1)
    g_valid = w_idx < it_end[-1]
    g_chunk = jnp.where(g_valid, c_lo[g_tile] + (w_idx - it_start[g_tile]), c_hi[n_gt - 1])
    g_first = (g_valid & (w_idx == it_start[g_tile])).astype(i32)
    g_last = (g_valid & (w_idx == it_end[g_tile] - 1)).astype(i32)
    gather = (g_tile, g_chunk.astype(i32), g_first, g_last, g_valid.astype(i32))
    lo = off[None, :] + excl[::ck]
    hi = off[None, :] + ends
    nblk = jnp.where(hi > lo, (hi - 1) // rb - lo // rb + 1, 0).reshape(-1)
    blk_lo = (lo // rb).reshape(-1)
    cb_end = jnp.cumsum(nblk)
    cb_start = cb_end - nblk
    cmax = p // rb + e * n_chunks
    c_idx = jnp.arange(cmax, dtype=i32)
    pair = jnp.minimum(_count_le(cb_end, c_idx), nblk.shape[0] - 1)
    c_valid = c_idx < cb_end[-1]
    c_tile = jnp.where(c_valid, pair // e, n_chunks - 1)
    c_blk = jnp.where(c_valid, blk_lo[pair] + (c_idx - cb_start[pair]), 0)
    per_tile = jnp.sum(nblk.reshape(n_chunks, e), axis=1)
    t_end = jnp.cumsum(per_tile)
    c_first = (c_valid & (c_idx == (t_end - per_tile)[c_tile])).astype(i32)
    c_last = (c_valid & (c_idx == t_end[c_tile] - 1)).astype(i32)
    combine = (c_tile.astype(i32), c_blk.astype(i32), c_first, c_last, c_valid.astype(i32))
    return dict(n_tiles=n_tiles, p=p, tile_expert=tile_expert.astype(i32), pos1=pos1, pos2=pos2,
                gather=gather, combine=combine)


def _moe_gather_body(tile_ref, chunk_ref, first_ref, last_ref, valid_ref, meta_ref, h_ref,
                     o_ref, wr_ref, acc_ref, wacc_ref):
    w = pl.program_id(0)

    @pl.when(first_ref[w] == 1)
    def _():
        acc_ref[...] = jnp.zeros_like(acc_ref)
        wacc_ref[...] = jnp.zeros_like(wacc_ref)

    @pl.when(valid_ref[w] == 1)
    def _():
        row = (lax.broadcasted_iota(jnp.int32, (MOE_GTILE, MOE_CHUNK), 0) + tile_ref[w] * MOE_GTILE).astype(F32)
        hit1 = row == meta_ref[0:1, :]
        hit2 = row == meta_ref[1:2, :]
        onehot = jnp.where(hit1, 1.0, jnp.where(hit2, 1.0, 0.0)).astype(BF16)
        acc_ref[...] += _dot(onehot, h_ref[...])
        wts = jnp.where(hit1, meta_ref[2:3, :], jnp.where(hit2, meta_ref[3:4, :], 0.0))
        wacc_ref[...] += jnp.sum(wts, axis=1, keepdims=True)

    @pl.when(last_ref[w] == 1)
    def _():
        o_ref[...] = acc_ref[...].astype(o_ref.dtype)
        wr_ref[...] = jnp.broadcast_to(wacc_ref[...], wr_ref.shape)


def _moe_ffn_body(exp_ref, x_ref, wr_ref, wg_ref, wu_ref, wd_ref, o_ref, acc_ref):
    f = pl.program_id(1)

    @pl.when(f == 0)
    def _():
        acc_ref[...] = jnp.zeros_like(acc_ref)

    x = x_ref[...]
    act = (_silu(_dot(x, wg_ref[...])) * _dot(x, wu_ref[...])).astype(BF16)
    acc_ref[...] += _dot(act, wd_ref[...])

    @pl.when(f == pl.num_programs(1) - 1)
    def _():
        o_ref[...] = (acc_ref[...] * wr_ref[:, 0:1]).astype(o_ref.dtype)


def _moe_combine_body(tile_ref, blk_ref, first_ref, last_ref, valid_ref, x_ref, pos_ref, y_ref, gf_ref,
                      o_ref, acc_ref, *, final_norm):
    w = pl.program_id(0)

    @pl.when(first_ref[w] == 1)
    def _():
        acc_ref[...] = x_ref[...]

    @pl.when(valid_ref[w] == 1)
    def _():
        base = blk_ref[w] * MOE_RBLK
        col = lax.broadcasted_iota(jnp.int32, (MOE_CHUNK, MOE_RBLK), 1) + base
        reps = MOE_RBLK // LANES
        p1 = jnp.concatenate([pos_ref[:, 0:LANES]] * reps, axis=1)
        p2 = jnp.concatenate([pos_ref[:, LANES:2 * LANES]] * reps, axis=1)
        hit = jnp.where(col == p1, 1.0, jnp.where(col == p2, 1.0, 0.0))
        acc_ref[...] += _dot(hit.astype(BF16), y_ref[...])

    @pl.when(last_ref[w] == 1)
    def _():
        o_ref[...] = _rms(acc_ref[...], gf_ref[...]) if final_norm else acc_ref[...]


def moe_ffn(x, h, route, w_gate, w_up, w_down, final_gain=None, tf=1408):
    t, d = x.shape
    f = w_gate.shape[2]
    tm, ck, rb = MOE_TILE, MOE_CHUNK, MOE_RBLK
    plan = _moe_plan(route, t)
    n_tiles, p = plan["n_tiles"], plan["p"]
    assert p < 2 ** 24, "row positions are carried exactly in f32"
    meta = jnp.stack([plan["pos1"].astype(F32), plan["pos2"].astype(F32), route[:, 2], route[:, 3]]
                     + [jnp.zeros((t,), F32)] * (SUBLANES - 4))
    pos_lanes = jnp.concatenate([jnp.broadcast_to(plan["pos1"][:, None], (t, LANES)),
                                 jnp.broadcast_to(plan["pos2"][:, None], (t, LANES))], axis=1)

    xs, w_row = pl.pallas_call(
        _moe_gather_body,
        grid_spec=pltpu.PrefetchScalarGridSpec(
            num_scalar_prefetch=5,
            grid=(plan["gather"][0].shape[0],),
            in_specs=[pl.BlockSpec((SUBLANES, ck), lambda w, tl, ch, fi, la, va: (0, ch[w])),
                      pl.BlockSpec((ck, d), lambda w, tl, ch, fi, la, va: (ch[w], 0))],
            out_specs=[pl.BlockSpec((MOE_GTILE, d), lambda w, tl, ch, fi, la, va: (tl[w], 0)),
                       pl.BlockSpec((MOE_GTILE, LANES), lambda w, tl, ch, fi, la, va: (tl[w], 0))],
            scratch_shapes=[pltpu.VMEM((MOE_GTILE, d), F32), pltpu.VMEM((MOE_GTILE, 1), F32)]),
        out_shape=[jax.ShapeDtypeStruct((p, d), BF16), jax.ShapeDtypeStruct((p, LANES), F32)],
        compiler_params=_cparams("arbitrary"),
        name="moe_gather",
    )(*plan["gather"], meta, h)

    ys = pl.pallas_call(
        _moe_ffn_body,
        grid_spec=pltpu.PrefetchScalarGridSpec(
            num_scalar_prefetch=1,
            grid=(n_tiles, f // tf),
            in_specs=[pl.BlockSpec((tm, d), lambda i, k, ex: (i, 0)),
                      pl.BlockSpec((tm, LANES), lambda i, k, ex: (i, 0)),
                      pl.BlockSpec((None, d, tf), lambda i, k, ex: (ex[i], 0, k)),
                      pl.BlockSpec((None, d, tf), lambda i, k, ex: (ex[i], 0, k)),
                      pl.BlockSpec((None, tf, d), lambda i, k, ex: (ex[i], k, 0))],
            out_specs=pl.BlockSpec((tm, d), lambda i, k, ex: (i, 0)),
            scratch_shapes=[pltpu.VMEM((tm, d), F32)]),
        out_shape=jax.ShapeDtypeStruct((p, d), BF16),
        compiler_params=_cparams("parallel", "arbitrary"),
        name="moe_ffn",
    )(plan["tile_expert"], xs, w_row, w_gate, w_up, w_down)

    final_norm = final_gain is not None
    gain = (final_gain if final_norm else jnp.ones((d,), F32)).reshape(1, d)
    return pl.pallas_call(
        functools.partial(_moe_combine_body, final_norm=final_norm),
        grid_spec=pltpu.PrefetchScalarGridSpec(
            num_scalar_prefetch=5,
            grid=(plan["combine"][0].shape[0],),
            in_specs=[pl.BlockSpec((ck, d), lambda w, tl, bk, fi, la, va: (tl[w], 0)),
                      pl.BlockSpec((ck, 2 * LANES), lambda w, tl, bk, fi, la, va: (tl[w], 0)),
                      pl.BlockSpec((rb, d), lambda w, tl, bk, fi, la, va: (bk[w], 0)),
                      pl.BlockSpec((1, d), lambda w, tl, bk, fi, la, va: (0, 0))],
            out_specs=pl.BlockSpec((ck, d), lambda w, tl, bk, fi, la, va: (tl[w], 0)),
            scratch_shapes=[pltpu.VMEM((ck, d), F32)]),
        out_shape=jax.ShapeDtypeStruct((t, d), F32),
        compiler_params=_cparams("arbitrary"),
        name="moe_combine",
    )(*plan["combine"], x, pos_lanes, ys, gain)


def _rmsnorm_body(x_ref, g_ref, o_ref):
    o_ref[...] = _rms(x_ref[...], g_ref[...])


def rmsnorm(x, g, tm=1024):
    t, d = x.shape
    tm = min(tm, t)
    return pl.pallas_call(
        _rmsnorm_body,
        grid=(t // tm,),
        in_specs=[pl.BlockSpec((tm, d), lambda i: (i, 0)), pl.BlockSpec((1, d), lambda i: (0, 0))],
        out_specs=pl.BlockSpec((tm, d), lambda i: (i, 0)),
        out_shape=jax.ShapeDtypeStruct((t, d), F32),
        compiler_params=_cparams("parallel"),
        name="rmsnorm",
    )(x, g.reshape(1, d))


def _in_proj_slices(w):
    sizes = (MOBA_WIDTH, MOBA_WIDTH, MOBA_WIDTH, SSM_WIDTH, GDN_WIDTH, GDN_WIDTH, GDN_WIDTH,
             GDN_HEADS, GDN_HEADS, GDN_WIDTH, w.shape[0], w.shape[0], w.shape[0])
    parts, start = [], 0
    for size in sizes:
        parts.append(w[:, start:start + size])
        start += size
    return parts


def kernel(x, mem, positions, norm_mix, w_in, ssm_a_re, ssm_a_im, ssm_log_dt, ssm_b_re, ssm_b_im, ssm_c_re, ssm_c_im, ssm_d, ssm_w_glu, gdn_conv, gdn_a_log, gdn_dt_bias, gdn_norm, w_up_moba, w_up_gdn, w_out, norm_xa, norm_mem, xa_wq, xa_wk, xa_wv, xa_wo, norm_ffn, ffn_w_gate, ffn_w_up, ffn_w_down, moe_w_router, moe_w_gate, moe_w_up, moe_w_down, norm_final):
    batch, seq, d = x.shape
    depth = w_in.shape[0]
    t = batch * seq
    assert batch == SUBLANES, "the S5 scan packs the batch onto the 8 sublanes of a vreg"
    assert seq % MOBA_BLOCK == 0 and seq % GDN_CHUNK == 0
    nc = seq // GDN_CHUNK
    bf = lambda a: a.astype(BF16)

    xf = x.reshape(t, d)
    memf = mem.reshape(-1, d)
    cos, sin = rope_tables(positions)

    for l in range(depth):
        (wq_m, wk_m, wv_m, wu_s, wq_g, wk_g, wv_g, wa_g, wb_g, wz_g, wg_a, wg_b, wg_c) = _in_proj_slices(w_in[l])
        g_mix = norm_mix[l]
        w_ab = jnp.pad(jnp.concatenate([wa_g, wb_g], axis=1), ((0, 0), (0, LANES - 2 * GDN_HEADS)))
        w32 = jnp.concatenate([wq_m, wk_m, wu_s, wq_g, wk_g, wv_g, w_ab], axis=1)
        w16 = jnp.concatenate([wv_m, wz_g, wg_a, wg_b, wg_c], axis=1)
        p32, p16 = norm_matmul2(xf, g_mix, bf(w32), bf(w16), F32, BF16, tm=2048, tna=640, tnb=1024)
        c_us = 2 * MOBA_WIDTH
        c_qkv = c_us + SSM_WIDTH
        c_ab = c_qkv + 3 * GDN_WIDTH
        u_s = p32[:, c_us:c_us + SSM_WIDTH]
        ab_g = p32[:, c_ab:c_ab + 2 * GDN_HEADS]

        y_a = moba_attention(p32, p16, cos, sin, batch, seq, q_col=0, k_col=MOBA_WIDTH // LANES, v_col=0)

        l_re, l_im, bb_re, bb_im = s5_discretise(ssm_a_re[l], ssm_a_im[l], ssm_log_dt[l], ssm_b_re[l], ssm_b_im[l])
        maps = s5_block_maps(bb_re, bb_im, ssm_c_re[l], ssm_c_im[l])
        u_tm = u_s.reshape(batch, seq, SSM_WIDTH).transpose(1, 0, 2).reshape(t, SSM_WIDTH)
        y_s_tm = s5_gelu(u_tm, maps, l_re, l_im, ssm_d[l])
        y_s = y_s_tm.reshape(seq, batch, SSM_WIDTH).transpose(1, 0, 2).reshape(t, SSM_WIDTH)

        def head_rows(cols):
            return cols.reshape(batch, nc, GDN_CHUNK, GDN_HEADS).transpose(0, 3, 1, 2)
        a_rows = head_rows(ab_g[:, :GDN_HEADS])
        b_rows = head_rows(ab_g[:, GDN_HEADS:2 * GDN_HEADS])
        y_c = gdn_mixer(p32, p16, a_rows, b_rows, gdn_conv[l], gdn_a_log[l], gdn_dt_bias[l], gdn_norm[l],
                        batch, seq, qkv_col=c_qkv // LANES, z_col=MOBA_WIDTH // GDN_WIDTH)

        xf = merge_project(xf, y_a, y_s, y_c, p16, bf(w_up_moba[l]), bf(ssm_w_glu[l]), bf(w_up_gdn[l]),
                           bf(w_out[l]), gate_col=(MOBA_WIDTH + GDN_WIDTH) // d)

        k_x = norm_matmul(memf, norm_mem[l], bf(xa_wk[l]), BF16)
        v_x = norm_matmul(memf, norm_mem[l], bf(xa_wv[l]), BF16)
        xf = cross_attention(xf, norm_xa[l], bf(xa_wq[l]), k_x, v_x, bf(xa_wo[l]), batch, seq)

        if l % 2 == 0:
            i = l // 2
            xf = ffn(xf, norm_ffn[l], bf(ffn_w_gate[i]), bf(ffn_w_up[i]), bf(ffn_w_down[i]))
        else:
            i = l // 2
            route, h_moe = moe_router(xf, norm_ffn[l], moe_w_router[i])
            xf = moe_ffn(xf, h_moe, route, bf(moe_w_gate[i]), bf(moe_w_up[i]), bf(moe_w_down[i]),
                         final_gain=norm_final if l == depth - 1 else None)

    if depth % 2 == 1:
        xf = rmsnorm(xf, norm_final)
    return xf.reshape(batch, seq, d)
```

```python
import functools
import math

import numpy as np
import jax
import jax.numpy as jnp
from jax import lax
from jax.experimental import pallas as pl
from jax.experimental.pallas import tpu as pltpu

F32 = jnp.float32
BF16 = jnp.bfloat16
HIGHEST = lax.Precision.HIGHEST

EPS = 1e-6
NEG_INF = -1e30
MOBA_HEADS = 8
MOBA_HEAD_DIM = 64
MOBA_WIDTH = MOBA_HEADS * MOBA_HEAD_DIM
MOBA_BLOCK = 256
MOBA_TOPK = 3
ROPE_THETA = 10000.0
SSM_WIDTH = 512
SSM_GROUP = 16
SSM_GROUPS = SSM_WIDTH // SSM_GROUP
SSM_STATE = 64
SSM_STATES = SSM_GROUPS * SSM_STATE
GDN_HEADS = 4
GDN_HEAD_DIM = 128
GDN_WIDTH = GDN_HEADS * GDN_HEAD_DIM
GDN_CONV = 4
GDN_CHUNK = 64
XA_HEADS = 4
XA_HEAD_DIM = 128
XA_WIDTH = XA_HEADS * XA_HEAD_DIM
N_EXPERTS = 8
TOP_K = 2

LANES = 128
SUBLANES = 8
VMEM_LIMIT = 56 * 1024 * 1024


def _cparams(*sem):
    return pltpu.CompilerParams(dimension_semantics=sem, vmem_limit_bytes=VMEM_LIMIT)


def _rms(x, g):
    return x * lax.rsqrt(jnp.mean(x * x, axis=-1, keepdims=True) + EPS) * g


def _sigmoid(x):
    return 1.0 / (1.0 + jnp.exp(-x))


def _silu(x):
    return x * _sigmoid(x)


def _dot(a, b):
    return jnp.dot(a, b, preferred_element_type=F32)


def _dot_nt(a, b):
    return lax.dot_general(a, b, (((1,), (1,)), ((), ())), preferred_element_type=F32)


def _dot_tn(a, b):
    return lax.dot_general(a, b, (((0,), (0,)), ((), ())), preferred_element_type=F32)


def _norm_matmul_body(x_ref, g_ref, w_ref, o_ref, h_ref):
    @pl.when(pl.program_id(1) == 0)
    def _():
        h_ref[...] = _rms(x_ref[...], g_ref[...]).astype(BF16)

    o_ref[...] = _dot(h_ref[...], w_ref[...]).astype(o_ref.dtype)


def norm_matmul(x, g, w, out_dtype, tm=512, tn=512):
    t, k = x.shape
    n = w.shape[1]
    tm, tn = min(tm, t), min(tn, n)
    return pl.pallas_call(
        _norm_matmul_body,
        grid=(t // tm, n // tn),
        in_specs=[pl.BlockSpec((tm, k), lambda i, j: (i, 0)),
                  pl.BlockSpec((1, k), lambda i, j: (0, 0)),
                  pl.BlockSpec((k, tn), lambda i, j: (0, j))],
        out_specs=pl.BlockSpec((tm, tn), lambda i, j: (i, j)),
        out_shape=jax.ShapeDtypeStruct((t, n), out_dtype),
        scratch_shapes=[pltpu.VMEM((tm, k), BF16)],
        compiler_params=_cparams("parallel", "arbitrary"),
        name="norm_matmul",
    )(x, g.reshape(1, k), w)


def _norm_matmul2_body(x_ref, g_ref, wa_ref, wb_ref, oa_ref, ob_ref, h_ref, *, na):
    j = pl.program_id(1)

    @pl.when(j == 0)
    def _():
        h_ref[...] = _rms(x_ref[...], g_ref[...]).astype(BF16)

    @pl.when(j < na)
    def _():
        oa_ref[...] = _dot(h_ref[...], wa_ref[...]).astype(oa_ref.dtype)

    @pl.when(j >= na)
    def _():
        ob_ref[...] = _dot(h_ref[...], wb_ref[...]).astype(ob_ref.dtype)


def norm_matmul2(x, g, wa, wb, dtype_a, dtype_b, tm, tna, tnb):
    t, k = x.shape
    tm = min(tm, t)
    na, nb = wa.shape[1] // tna, wb.shape[1] // tnb
    return pl.pallas_call(
        functools.partial(_norm_matmul2_body, na=na),
        grid=(t // tm, na + nb),
        in_specs=[pl.BlockSpec((tm, k), lambda i, j: (i, 0)),
                  pl.BlockSpec((1, k), lambda i, j: (0, 0)),
                  pl.BlockSpec((k, tna), lambda i, j: (0, jnp.minimum(j, na - 1))),
                  pl.BlockSpec((k, tnb), lambda i, j: (0, jnp.maximum(j - na, 0)))],
        out_specs=[pl.BlockSpec((tm, tna), lambda i, j: (i, jnp.minimum(j, na - 1))),
                   pl.BlockSpec((tm, tnb), lambda i, j: (i, jnp.maximum(j - na, 0)))],
        out_shape=[jax.ShapeDtypeStruct((t, wa.shape[1]), dtype_a), jax.ShapeDtypeStruct((t, wb.shape[1]), dtype_b)],
        scratch_shapes=[pltpu.VMEM((tm, k), BF16)],
        compiler_params=_cparams("parallel", "arbitrary"),
        name="norm_matmul2",
    )(x, g.reshape(1, k), wa, wb)


def _rope_body(pos_ref, inv_ref, cos_ref, sin_ref):
    ang = pos_ref[...] * inv_ref[...]
    lane = lax.broadcasted_iota(jnp.int32, ang.shape, 1)
    first_half = (lane % MOBA_HEAD_DIM) < (MOBA_HEAD_DIM // 2)
    cos_ref[...] = jnp.cos(ang)
    s = jnp.sin(ang)
    sin_ref[...] = jnp.where(first_half, -s, s)


def rope_tables(positions):
    t = positions.size
    half = MOBA_HEAD_DIM // 2
    inv = (1.0 / (np.float32(ROPE_THETA) ** (np.arange(0, MOBA_HEAD_DIM, 2, dtype=np.float32)
                                             / np.float32(MOBA_HEAD_DIM)))).astype(np.float32)
    inv_row = jnp.asarray(np.tile(inv, LANES // half).reshape(1, LANES))
    pos = positions.astype(F32).reshape(t, 1)
    ts = min(1024, t)
    return pl.pallas_call(
        _rope_body,
        grid=(t // ts,),
        in_specs=[pl.BlockSpec((ts, 1), lambda i: (i, 0)),
                  pl.BlockSpec((1, LANES), lambda i: (0, 0))],
        out_specs=[pl.BlockSpec((ts, LANES), lambda i: (i, 0))] * 2,
        out_shape=[jax.ShapeDtypeStruct((t, LANES), F32)] * 2,
        compiler_params=_cparams("parallel"),
        name="rope_tables",
    )(pos, inv_row)


def _rope_apply(x, cos, sin_signed):
    lane = lax.broadcasted_iota(jnp.int32, x.shape, 1)
    first_half = (lane % MOBA_HEAD_DIM) < (MOBA_HEAD_DIM // 2)
    partner = jnp.where(first_half, pltpu.roll(x, LANES - MOBA_HEAD_DIM // 2, 1),
                        pltpu.roll(x, MOBA_HEAD_DIM // 2, 1))
    return x * cos + partner * sin_signed


def _moba_body(q_ref, k_ref, v_ref, cq_ref, sq_ref, ck_ref, sk_ref, o_ref, ka_ref, va_ref, km_ref, s_ref,
               mt_ref, acc_ref, *, nb):
    blk = MOBA_BLOCK
    i = pl.program_id(2)

    @pl.when(i == 0)
    def _():
        km_ref[...] = jnp.zeros_like(km_ref)
        lane_b = lax.broadcasted_iota(jnp.int32, (blk, LANES), 1)

        def rope_blk(j, c):
            rows = pl.ds(pl.multiple_of(j * blk, blk), blk)
            kr = _rope_apply(k_ref[rows, :], ck_ref[rows, :], sk_ref[rows, :])
            ka_ref[rows, 0:LANES] = kr.astype(BF16)
            ka_ref[rows, LANES:2 * LANES] = jnp.where(lane_b == j, 1.0, 0.0).astype(BF16)
            km_ref[pl.ds(j, 1), :] = jnp.mean(kr, axis=0, keepdims=True)
            v = v_ref[rows, :].astype(F32)
            va_ref[0, rows, :] = jnp.where(lane_b < MOBA_HEAD_DIM, v,
                                           jnp.where(lane_b == MOBA_HEAD_DIM, 1.0, 0.0)).astype(BF16)
            va_ref[1, rows, :] = jnp.where(lane_b >= MOBA_HEAD_DIM, v,
                                           jnp.where(lane_b == 0, 1.0, 0.0)).astype(BF16)
            return c
        lax.fori_loop(0, nb, rope_blk, 0)

    tq = 2 * blk
    lane = lax.broadcasted_iota(jnp.int32, (1, LANES), 1)
    head_a = lane < MOBA_HEAD_DIM
    q = _rope_apply(q_ref[...], cq_ref[...], sq_ref[...])
    scale = MOBA_HEAD_DIM ** -0.5
    km = km_ref[...]
    nbp = km_ref.shape[0]
    brow = lax.broadcasted_iota(jnp.int32, (nbp, tq), 0)
    qblk = 2 * i + jnp.where(lax.broadcasted_iota(jnp.int32, (1, tq), 1) >= blk, 1, 0)
    heads = (0, 1)
    q_heads = [jnp.where(head_a, q, 0.0), jnp.where(head_a, 0.0, q)]
    gates = [lax.dot_general(km, qh, (((1,), (1,)), ((), ())), preferred_element_type=F32, precision=HIGHEST)
             for qh in q_heads]
    q_augs = []
    brow_f = brow.astype(F32)
    for h in heads:
        gate = jnp.where(brow < qblk, gates[h], NEG_INF)
        picked = jnp.zeros((nbp, tq), F32)
        for _ in range(MOBA_TOPK):
            top = jnp.max(gate, axis=0, keepdims=True)
            first = jnp.min(jnp.where(gate == top, brow_f, float(nbp)), axis=0, keepdims=True)
            hit = brow_f == first
            picked = jnp.where(hit, 1.0, picked)
            gate = jnp.where(hit, -jnp.inf, gate)
        keep = ((picked > 0.5) & (brow < qblk)) | (brow == qblk)
        bias_t = jnp.concatenate([jnp.where(keep, 0.0, NEG_INF), jnp.zeros((LANES - nbp, tq), F32)],
                                 axis=0)
        q_augs.append(jnp.concatenate([q_heads[h] * scale, bias_t.T], axis=1).astype(BF16))

    def pair_rows(jp):
        return pl.ds(pl.multiple_of(jp * tq, tq), tq)

    r_idx = lax.broadcasted_iota(jnp.int32, (tq, tq), 0)
    c_idx = lax.broadcasted_iota(jnp.int32, (tq, tq), 1)
    k_diag = ka_ref[pair_rows(i), :]
    s_diag = [_dot_nt(q_augs[h], k_diag) for h in heads]
    for h in heads:
        s_d = jnp.where(c_idx <= r_idx, s_diag[h], NEG_INF)
        s_ref[h, i] = s_d
        m_t = s_d[:, 0:LANES]
        for col in range(LANES, tq, LANES):
            m_t = jnp.maximum(m_t, s_d[:, col:col + LANES])
        mt_ref[h] = m_t

    def loop2(n, body):
        def two(k, c):
            body([2 * k, 2 * k + 1])
            return c
        lax.fori_loop(0, lax.shift_right_logical(n, 1), two, 0)

        @pl.when((n & 1) == 1)
        def _():
            body([n - 1])

    def scores(jps):
        k2s = [ka_ref[pair_rows(jp), :] for jp in jps]
        s2s = [[_dot_nt(q_augs[h], k2) for h in heads] for k2 in k2s]
        for jp, s2h in zip(jps, s2s):
            for h in heads:
                s_ref[h, jp] = s2h[h]
        for h in heads:
            m_t = mt_ref[h]
            for s2h in s2s:
                for col in range(0, tq, LANES):
                    m_t = jnp.maximum(m_t, s2h[h][:, col:col + LANES])
            mt_ref[h] = m_t
    loop2(i, scores)

    ms = [jnp.max(mt_ref[h], axis=-1, keepdims=True) for h in heads]
    for h in heads:
        acc_ref[h] = jnp.zeros((tq, LANES), F32)

    def values(jps):
        ps = [[jnp.exp(s_ref[h, jp] - ms[h]).astype(BF16) for h in heads] for jp in jps]
        pvs = [[_dot(p[h], va_ref[h, pair_rows(jp), :]) for h in heads] for jp, p in zip(jps, ps)]
        for h in heads:
            tot = pvs[0][h]
            for pv in pvs[1:]:
                tot = tot + pv[h]
            acc_ref[h] += tot
    loop2(i + 1, values)
    acc_a, acc_b = acc_ref[0], acc_ref[1]
    out_a = acc_a / acc_a[:, MOBA_HEAD_DIM:MOBA_HEAD_DIM + 1]
    out_b = acc_b / acc_b[:, 0:1]
    o_ref[...] = jnp.where(head_a, out_a, out_b).astype(o_ref.dtype)


def moba_attention(qk, v, cos, sin, batch, seq, q_col=0, k_col=MOBA_WIDTH // LANES, v_col=0):
    nb = seq // MOBA_BLOCK
    assert nb % 2 == 0
    npair = nb // 2
    pairs = MOBA_WIDTH // LANES
    tq = 2 * MOBA_BLOCK
    t = batch * seq
    return pl.pallas_call(
        functools.partial(_moba_body, nb=nb),
        grid=(batch, pairs, npair),
        in_specs=[pl.BlockSpec((tq, LANES), lambda b, p, i: (b * npair + i, q_col + p)),
                  pl.BlockSpec((seq, LANES), lambda b, p, i: (b, k_col + p)),
                  pl.BlockSpec((seq, LANES), lambda b, p, i: (b, v_col + p)),
                  pl.BlockSpec((tq, LANES), lambda b, p, i: (b * npair + i, 0)),
                  pl.BlockSpec((tq, LANES), lambda b, p, i: (b * npair + i, 0)),
                  pl.BlockSpec((seq, LANES), lambda b, p, i: (b, 0)),
                  pl.BlockSpec((seq, LANES), lambda b, p, i: (b, 0))],
        out_specs=pl.BlockSpec((tq, LANES), lambda b, p, i: (b * npair + i, p)),
        out_shape=jax.ShapeDtypeStruct((t, MOBA_WIDTH), BF16),
        scratch_shapes=[pltpu.VMEM((seq, 2 * LANES), BF16),
                        pltpu.VMEM((2, seq, LANES), BF16),
                        pltpu.VMEM((-(-nb // SUBLANES) * SUBLANES, LANES), F32),
                        pltpu.VMEM((2, npair, tq, tq), F32),
                        pltpu.VMEM((2, tq, LANES), F32),
                        pltpu.VMEM((2, tq, LANES), F32)],
        compiler_params=_cparams("parallel", "parallel", "arbitrary"),
        name="moba_attention",
    )(qk, qk, v, cos, sin, cos, sin)


def _s5_disc_body(are_ref, aim_ref, ldt_ref, bre_ref, bim_ref, lre_ref, lim_ref, bbre_ref, bbim_ref):
    a_re, a_im = are_ref[...], aim_ref[...]
    dt = jnp.exp(ldt_ref[...])
    mag = jnp.exp(a_re * dt)
    l_re = mag * jnp.cos(a_im * dt)
    l_im = mag * jnp.sin(a_im * dt)
    lre_ref[...] = l_re
    lim_ref[...] = l_im
    x, y = l_re - 1.0, l_im
    den = a_re * a_re + a_im * a_im
    c_re = (x * a_re + y * a_im) / den
    c_im = (y * a_re - x * a_im) / den
    b_re, b_im = bre_ref[...], bim_ref[...]
    bbre_ref[...] = c_re * b_re - c_im * b_im
    bbim_ref[...] = c_re * b_im + c_im * b_re


def s5_discretise(a_re, a_im, log_dt, b_re, b_im):
    g, p = a_re.shape
    n = b_re.shape[-1]
    col = lambda a: a.reshape(g * p, 1)
    ldt = jnp.broadcast_to(log_dt[:, None], (g, p))
    outs = pl.pallas_call(
        _s5_disc_body,
        out_shape=[jax.ShapeDtypeStruct((g * p, 1), F32)] * 2 + [jax.ShapeDtypeStruct((g * p, n), F32)] * 2,
        name="s5_discretise",
    )(col(a_re), col(a_im), col(ldt), b_re.reshape(g * p, n), b_im.reshape(g * p, n))
    l_re, l_im, bb_re, bb_im = outs
    return l_re.reshape(g, p), l_im.reshape(g, p), bb_re.reshape(g, p, n), bb_im.reshape(g, p, n)


def _block_diag(blocks):
    g, r, c = blocks.shape
    eye = jnp.eye(g, dtype=blocks.dtype)
    return (blocks[:, :, None, :] * eye[:, None, :, None]).reshape(g * r, g * c)


def _s5_body(u_ref, bre_ref, bim_ref, cre_ref, cim_ref, lre_ref, lim_ref, d_ref, y_ref, h_ref, hb_ref, *, steps):
    rows = SUBLANES
    ns = SSM_STATES
    nq = SSM_WIDTH // LANES
    sq = ns // nq

    @pl.when(pl.program_id(0) == 0)
    def _():
        h_ref[...] = jnp.zeros_like(h_ref)

    u = u_ref[...]
    u16 = u.astype(BF16)
    for q in range(nq):
        uq = u16[:, q * LANES:(q + 1) * LANES]
        hb_ref[:, q * sq:(q + 1) * sq] = _dot(uq, bre_ref[q])
        hb_ref[:, ns + q * sq:ns + (q + 1) * sq] = _dot(uq, bim_ref[q])
    l_re, l_im = lre_ref[...], lim_ref[...]

    def step(t, carry):
        h_re, h_im = carry
        r = pl.ds(pl.multiple_of(t * rows, rows), rows)
        n_re = l_re * h_re - l_im * h_im + hb_ref[r, 0:ns]
        n_im = l_re * h_im + l_im * h_re + hb_ref[r, ns:2 * ns]
        hb_ref[r, 0:ns] = n_re
        hb_ref[r, ns:2 * ns] = n_im
        return n_re, n_im

    h_re, h_im = lax.fori_loop(0, steps, step, (h_ref[:, 0:ns], h_ref[:, ns:2 * ns]), unroll=4)
    h_ref[:, 0:ns] = h_re
    h_ref[:, ns:2 * ns] = h_im
    ys = []
    for q in range(nq):
        ys.append(_dot(hb_ref[:, q * sq:(q + 1) * sq].astype(BF16), cre_ref[q])
                  + _dot(hb_ref[:, ns + q * sq:ns + (q + 1) * sq].astype(BF16), cim_ref[q]))
    y = jnp.concatenate(ys, axis=1) + d_ref[...] * u
    y_ref[...] = jax.nn.gelu(y).astype(y_ref.dtype)


def s5_block_maps(bb_re, bb_im, c_re, c_im):
    nq = SSM_WIDTH // LANES
    gq = SSM_GROUPS // nq
    per_q = lambda a: jnp.stack([_block_diag(a[q * gq:(q + 1) * gq].transpose(0, 2, 1)) for q in range(nq)])
    return (per_q(bb_re).astype(BF16), per_q(bb_im).astype(BF16),
            per_q(c_re).astype(BF16), per_q(-c_im).astype(BF16))


def s5_gelu(u_tm, maps, l_re, l_im, d_skip, steps=64):
    rows = u_tm.shape[0]
    blk = steps * SUBLANES
    ns2 = 2 * SSM_STATES
    full3 = lambda a: pl.BlockSpec(a.shape, lambda c: (0, 0, 0))
    return pl.pallas_call(
        functools.partial(_s5_body, steps=steps),
        grid=(rows // blk,),
        in_specs=[pl.BlockSpec((blk, SSM_WIDTH), lambda c: (c, 0)),
                  full3(maps[0]), full3(maps[1]), full3(maps[2]), full3(maps[3]),
                  pl.BlockSpec((1, SSM_STATES), lambda c: (0, 0)),
                  pl.BlockSpec((1, SSM_STATES), lambda c: (0, 0)),
                  pl.BlockSpec((1, SSM_WIDTH), lambda c: (0, 0))],
        out_specs=pl.BlockSpec((blk, SSM_WIDTH), lambda c: (c, 0)),
        out_shape=jax.ShapeDtypeStruct((rows, SSM_WIDTH), BF16),
        scratch_shapes=[pltpu.VMEM((SUBLANES, ns2), F32),
                        pltpu.VMEM((blk, ns2), F32)],
        compiler_params=_cparams("arbitrary"),
        name="s5_scan",
    )(u_tm, *maps, l_re.reshape(1, -1), l_im.reshape(1, -1), d_skip.reshape(1, -1))


GDN_GROUP = 16


def _gdn_prep_body(q_ref, k_ref, v_ref, a_ref, b_ref, wq_ref, wk_ref, wv_ref, alog_ref, dtb_ref,
                   u_ref, w_ref, qd_ref, kd_ref, qk_ref, gl_ref, qs_ref, ks_ref, vs_ref,
                   *, seq):
    L = GDN_CHUNK
    G = min(GDN_GROUP, seq // L)
    dk = GDN_HEAD_DIM
    nc = seq // L
    hd = pl.program_id(1)
    cb = 256
    halo = SUBLANES

    def conv_blk(rb, c):
        base = pl.multiple_of(rb * cb, cb)
        prev = pl.multiple_of(jnp.maximum(base - halo, 0), halo)
        for src, wref, dst, norm, mul in ((q_ref, wq_ref, qs_ref, True, dk ** -0.5),
                                          (k_ref, wk_ref, ks_ref, True, 1.0),
                                          (v_ref, wv_ref, vs_ref, False, 1.0)):
            cur = src[pl.ds(base, cb), :]
            head = jnp.where(rb > 0, src[pl.ds(prev, halo), :], 0.0)
            ext = jnp.concatenate([head, cur], axis=0)
            wts = wref[...]
            y = ext[halo:halo + cb] * wts[GDN_CONV - 1:GDN_CONV]
            for tap in range(1, GDN_CONV):
                y = y + ext[halo - tap:halo - tap + cb] * wts[GDN_CONV - 1 - tap:GDN_CONV - tap]
            y = _silu(y)
            if norm:
                y = y * lax.rsqrt(jnp.sum(y * y, axis=-1, keepdims=True) + EPS) * mul
            dst[pl.ds(base, cb), :] = y
        return c
    lax.fori_loop(0, seq // cb, conv_blk, 0)

    gl_rows = G * L
    ri = lax.broadcasted_iota(jnp.int32, (gl_rows, L), 0) % L
    ci = lax.broadcasted_iota(jnp.int32, (gl_rows, L), 1)
    tril = ci <= ri
    strict = ci < ri
    eye = ci == ri
    ri3 = lax.broadcasted_iota(jnp.int32, (G, L, L), 1)
    ci3 = lax.broadcasted_iota(jnp.int32, (G, L, L), 2)
    strict3 = ci3 < ri3
    eye_f = jnp.where(ci3 == ri3, 1.0, 0.0)
    r2 = lax.broadcasted_iota(jnp.int32, (L, L), 0)
    c2 = lax.broadcasted_iota(jnp.int32, (L, L), 1)
    upper_f = jnp.where(r2 <= c2, 1.0, 0.0)
    a_coef = -jnp.exp(jnp.full((1, L), alog_ref[hd], F32))
    dt_bias = dtb_ref[hd]
    hp = functools.partial(jnp.dot, preferred_element_type=F32, precision=HIGHEST)

    def bmm(a, b):
        return jnp.einsum('gij,gjk->gik', a, b, preferred_element_type=F32)

    def bmm_nt(a, b):
        return jnp.einsum('gid,gjd->gij', a, b, preferred_element_type=F32)

    def to_col(rows_b):
        return jnp.sum(jnp.where(eye, rows_b, 0.0), axis=1, keepdims=True)

    def prep(cg, carry):
        c0 = pl.multiple_of(cg * G, G)
        rows = pl.ds(pl.multiple_of(cg * gl_rows, gl_rows), gl_rows)
        qc, kc, vc = qs_ref[rows, :], ks_ref[rows, :], vs_ref[rows, :]
        z = a_ref[pl.ds(c0, G), :] + dt_bias
        softplus = jnp.maximum(z, 0.0) + jnp.log(1.0 + jnp.exp(-jnp.abs(z)))
        g_rows = hp(a_coef * softplus, upper_f)
        beta_rows = _sigmoid(b_ref[pl.ds(c0, G), :])
        spread = lambda r: jnp.concatenate([jnp.broadcast_to(r[c:c + 1, :], (L, L)) for c in range(G)], axis=0)
        g_rb = spread(g_rows)
        g_col = to_col(g_rb)
        beta_col = to_col(spread(beta_rows))
        g_last = g_rb[:, L - 1:L]
        decay = jnp.where(tril, jnp.exp(jnp.where(tril, g_col - g_rb, 0.0)), 0.0)
        k_beta = kc * beta_col
        v_beta = vc * beta_col
        to3 = lambda a: a.reshape(G, L, a.shape[-1])
        kb16, k16, q16 = to3(k_beta.astype(BF16)), to3(kc.astype(BF16)), to3(qc.astype(BF16))
        kk_qk = bmm_nt(jnp.concatenate([kb16, q16], axis=1), k16)
        decay3 = to3(decay)
        a_low = jnp.where(strict3, kk_qk[:, :L] * decay3, 0.0)
        qk = kk_qk[:, L:] * decay3
        t_inv = eye_f - a_low
        a16 = a_low.astype(BF16)
        pw = bmm(a16, a16)
        span = 2
        while 2 * span < L:
            pw16 = pw.astype(BF16)
            both = bmm(jnp.concatenate([t_inv.astype(BF16), pw16], axis=1), pw16)
            t_inv = t_inv + both[:, :L]
            pw = both[:, L:]
            span *= 2
        t_inv = t_inv + bmm(t_inv.astype(BF16), pw.astype(BF16))
        rhs = jnp.concatenate([v_beta.astype(BF16), (k_beta * jnp.exp(g_col)).astype(BF16)], axis=1)
        uw = bmm(t_inv.astype(BF16), to3(rhs)).reshape(gl_rows, 2 * dk)
        u_ref[rows, :] = uw[:, :dk]
        w_ref[rows, :] = uw[:, dk:].astype(BF16)
        qd_ref[rows, :] = (qc * jnp.exp(g_col)).astype(BF16)
        k_dec = to3(kc * jnp.exp(g_last - g_col))
        kd_ref[pl.ds(c0, G)] = jnp.swapaxes(k_dec, 1, 2).astype(BF16)
        qk_ref[pl.ds(c0, G)] = qk.astype(BF16)
        gl_ref[pl.ds(c0, G), :] = jnp.broadcast_to(jnp.exp(g_rows[:, L - 1:L]), (G, LANES))
        return carry
    lax.fori_loop(0, nc // G, prep, 0)


def _gdn_scan_body(u_ref, w_ref, qd_ref, kd_ref, qk_ref, gl_ref, z_ref, gn_ref, o_ref, *st_refs, chunks):
    L = GDN_CHUNK
    d = GDN_HEAD_DIM
    nrow = u_ref.shape[0]

    @pl.when(pl.program_id(1) == 0)
    def _():
        for st in st_refs:
            st[...] = jnp.zeros_like(st)

    gn = gn_ref[...]

    def chunk(c, carry):
        rows = pl.ds(pl.multiple_of(c * L, L), L)
        chains = [(r, h) for r in range(nrow) for h in range(GDN_HEADS)]
        states = [st_refs[r * GDN_HEADS + h][...] for r, h in chains]
        firsts = []
        for (r, h), s in zip(chains, states):
            cols = slice(h * d, (h + 1) * d)
            firsts.append(_dot(jnp.concatenate([w_ref[r, rows, cols], qd_ref[r, rows, cols]], axis=0),
                               s.astype(BF16)))
        vns = []
        for (r, h), both in zip(chains, firsts):
            cols = slice(h * d, (h + 1) * d)
            vns.append((u_ref[r, rows, cols] - both[:L]).astype(BF16))
        outs = [both[L:] + _dot(qk_ref[r, h, c], vn) for (r, h), both, vn in zip(chains, firsts, vns)]
        for (r, h), s, vn in zip(chains, states, vns):
            st_refs[r * GDN_HEADS + h][...] = s * gl_ref[r, h, pl.ds(c, 1), :] + _dot(kd_ref[r, h, c], vn)
        for (r, h), o in zip(chains, outs):
            cols = slice(h * d, (h + 1) * d)
            o_ref[r, rows, cols] = (_rms(o, gn) * _silu(z_ref[r, rows, cols].astype(F32))).astype(o_ref.dtype)
        return carry
    lax.fori_loop(0, chunks, chunk, 0)


def gdn_mixer(qkv, z, a_rows, b_rows, conv_w, a_log, dt_bias, gnorm, batch, seq, ts=1024, qkv_col=0, z_col=0):
    t = batch * seq
    hds = GDN_HEADS
    nc = seq // GDN_CHUNK
    L = GDN_CHUNK
    d = GDN_HEAD_DIM
    seq_spec = lambda off: pl.BlockSpec((seq, d), lambda b, h: (b, qkv_col + off + h))
    w_spec = lambda off: pl.BlockSpec((GDN_CONV, d), lambda b, h: (0, off + h))
    row_spec = pl.BlockSpec((None, None, nc, L), lambda b, h: (b, h, 0, 0))
    smem = pl.BlockSpec(memory_space=pltpu.SMEM)
    head_out = pl.BlockSpec((seq, d), lambda b, h: (b, h))
    u, w, qd, kd, qk, gl = pl.pallas_call(
        functools.partial(_gdn_prep_body, seq=seq),
        grid=(batch, hds),
        in_specs=[seq_spec(0), seq_spec(hds), seq_spec(2 * hds), row_spec, row_spec,
                  w_spec(0), w_spec(hds), w_spec(2 * hds), smem, smem],
        out_specs=[head_out, head_out, head_out,
                   pl.BlockSpec((None, None, nc, d, L), lambda b, h: (b, h, 0, 0, 0)),
                   pl.BlockSpec((None, None, nc, L, L), lambda b, h: (b, h, 0, 0, 0)),
                   pl.BlockSpec((None, None, nc, LANES), lambda b, h: (b, h, 0, 0))],
        out_shape=[jax.ShapeDtypeStruct((t, GDN_WIDTH), F32),
                   jax.ShapeDtypeStruct((t, GDN_WIDTH), BF16),
                   jax.ShapeDtypeStruct((t, GDN_WIDTH), BF16),
                   jax.ShapeDtypeStruct((batch, hds, nc, d, L), BF16),
                   jax.ShapeDtypeStruct((batch, hds, nc, L, L), BF16),
                   jax.ShapeDtypeStruct((batch, hds, nc, LANES), F32)],
        scratch_shapes=[pltpu.VMEM((seq, d), F32), pltpu.VMEM((seq, d), F32), pltpu.VMEM((seq, d), F32)],
        compiler_params=_cparams("parallel", "parallel"),
        name="gdn_prep",
    )(qkv, qkv, qkv, a_rows, b_rows, conv_w, conv_w, conv_w, a_log, dt_bias)

    ts = min(ts, seq)
    spb = seq // ts
    cps = ts // L
    nrow = 2 if batch % 2 == 0 else 1
    by_row = lambda a: a.reshape(batch, seq, a.shape[-1])
    tok = pl.BlockSpec((nrow, ts, GDN_WIDTH), lambda b, s: (b, s, 0))
    out = pl.pallas_call(
        functools.partial(_gdn_scan_body, chunks=cps),
        grid=(batch // nrow, spb),
        in_specs=[tok, tok, tok,
                  pl.BlockSpec((nrow, hds, cps, d, L), lambda b, s: (b, 0, s, 0, 0)),
                  pl.BlockSpec((nrow, hds, cps, L, L), lambda b, s: (b, 0, s, 0, 0)),
                  pl.BlockSpec((nrow, hds, cps, LANES), lambda b, s: (b, 0, s, 0)),
                  pl.BlockSpec((nrow, ts, GDN_WIDTH), lambda b, s: (b, s, z_col)),
                  pl.BlockSpec((1, d), lambda b, s: (0, 0))],
        out_specs=tok,
        out_shape=jax.ShapeDtypeStruct((batch, seq, GDN_WIDTH), BF16),
        scratch_shapes=[pltpu.VMEM((d, d), F32)] * (nrow * hds),
        compiler_params=_cparams("parallel", "arbitrary"),
        name="gdn_scan",
    )(by_row(u), by_row(w), by_row(qd), kd, qk, gl, by_row(z), gnorm.reshape(1, d))
    return out.reshape(t, GDN_WIDTH)


def _merge_body(x_ref, ya_ref, ys_ref, yc_ref, ga_ref, gb_ref, gc_ref, wa_ref, wg_ref, wc_ref, wo_ref, o_ref):
    d = x_ref.shape[1]
    y_a = _dot(ya_ref[...], wa_ref[...])
    glu = _dot(ys_ref[...], wg_ref[...])
    y_b = glu[:, :d] * _sigmoid(glu[:, d:])
    y_c = _dot(yc_ref[...], wc_ref[...])
    gate = lambda r: _sigmoid(r[...].astype(F32))
    merged = gate(ga_ref) * y_a + gate(gb_ref) * y_b + gate(gc_ref) * y_c
    o_ref[...] = x_ref[...] + _dot(merged.astype(BF16), wo_ref[...])


def merge_project(x, y_a, y_s, y_c, gates, w_a, w_glu, w_c, w_o, tm=512, gate_col=0):
    t, d = x.shape
    row = lambda n: pl.BlockSpec((tm, n), lambda i: (i, 0))
    gate = lambda k: pl.BlockSpec((tm, d), lambda i: (i, gate_col + k))
    full = lambda a: pl.BlockSpec(a.shape, lambda i: (0, 0))
    return pl.pallas_call(
        _merge_body,
        grid=(t // tm,),
        in_specs=[row(d), row(y_a.shape[1]), row(y_s.shape[1]), row(y_c.shape[1]), gate(0), gate(1), gate(2),
                  full(w_a), full(w_glu), full(w_c), full(w_o)],
        out_specs=row(d),
        out_shape=jax.ShapeDtypeStruct((t, d), F32),
        compiler_params=_cparams("parallel"),
        name="merge_project",
    )(x, y_a, y_s, y_c, gates, gates, gates, w_a, w_glu, w_c, w_o)


def _xattn_body(x_ref, g_ref, wq_ref, k_ref, v_ref, wo_ref, o_ref):
    x = x_ref[...]
    h = _rms(x, g_ref[...]).astype(BF16)
    q = _dot(h, wq_ref[...]).astype(BF16)
    cols = [slice(hd * XA_HEAD_DIM, (hd + 1) * XA_HEAD_DIM) for hd in range(XA_HEADS)]
    scores = [_dot_nt(q[:, c], k_ref[:, c]) * (XA_HEAD_DIM ** -0.5) for c in cols]
    probs = []
    for s in scores:
        p = jnp.exp(s - jnp.max(s, axis=-1, keepdims=True))
        probs.append((p / jnp.sum(p, axis=-1, keepdims=True)).astype(BF16))
    outs = [_dot(p, v_ref[:, c]) for p, c in zip(probs, cols)]
    o = jnp.concatenate(outs, axis=1).astype(BF16)
    o_ref[...] = x + _dot(o, wo_ref[...])


def cross_attention(x, g, wq, k, v, wo, batch, seq, tm=512):
    t, d = x.shape
    m = k.shape[0] // batch
    spb = seq // tm
    full = lambda a: pl.BlockSpec(a.shape, lambda i: (0, 0))
    return pl.pallas_call(
        _xattn_body,
        grid=(t // tm,),
        in_specs=[pl.BlockSpec((tm, d), lambda i: (i, 0)),
                  pl.BlockSpec((1, d), lambda i: (0, 0)),
                  full(wq),
                  pl.BlockSpec((m, XA_WIDTH), lambda i: (i // spb, 0)),
                  pl.BlockSpec((m, XA_WIDTH), lambda i: (i // spb, 0)),
                  full(wo)],
        out_specs=pl.BlockSpec((tm, d), lambda i: (i, 0)),
        out_shape=jax.ShapeDtypeStruct((t, d), F32),
        compiler_params=_cparams("parallel"),
        name="cross_attention",
    )(x, g.reshape(1, d), wq, k, v, wo)


def _ffn_body(x_ref, g_ref, wg_ref, wu_ref, wd_ref, o_ref, h_ref, acc_ref):
    f = pl.program_id(1)

    @pl.when(f == 0)
    def _():
        h_ref[...] = _rms(x_ref[...], g_ref[...]).astype(BF16)
        acc_ref[...] = jnp.zeros_like(acc_ref)

    h = h_ref[...]
    act = (_silu(_dot(h, wg_ref[...])) * _dot(h, wu_ref[...])).astype(BF16)
    acc_ref[...] += _dot(act, wd_ref[...])

    @pl.when(f == pl.num_programs(1) - 1)
    def _():
        o_ref[...] = x_ref[...] + acc_ref[...]


def ffn(x, g, w_gate, w_up, w_down, tm=512, tf=1408):
    t, d = x.shape
    f = w_gate.shape[1]
    return pl.pallas_call(
        _ffn_body,
        grid=(t // tm, f // tf),
        in_specs=[pl.BlockSpec((tm, d), lambda i, k: (i, 0)),
                  pl.BlockSpec((1, d), lambda i, k: (0, 0)),
                  pl.BlockSpec((d, tf), lambda i, k: (0, k)),
                  pl.BlockSpec((d, tf), lambda i, k: (0, k)),
                  pl.BlockSpec((tf, d), lambda i, k: (k, 0))],
        out_specs=pl.BlockSpec((tm, d), lambda i, k: (i, 0)),
        out_shape=jax.ShapeDtypeStruct((t, d), F32),
        scratch_shapes=[pltpu.VMEM((tm, d), BF16), pltpu.VMEM((tm, d), F32)],
        compiler_params=_cparams("parallel", "arbitrary"),
        name="ffn",
    )(x, g.reshape(1, d), w_gate, w_up, w_down)


def _router_body(x_ref, g_ref, w_ref, r_ref, h_ref):
    h = _rms(x_ref[...], g_ref[...])
    h_ref[...] = h.astype(BF16)
    logits = jnp.dot(h, w_ref[...], preferred_element_type=F32, precision=HIGHEST)
    lane = lax.broadcasted_iota(jnp.int32, logits.shape, 1).astype(F32)
    logits = jnp.where(lane < N_EXPERTS, logits, -jnp.inf)
    v1 = jnp.max(logits, axis=-1, keepdims=True)
    i1 = jnp.min(jnp.where(logits == v1, lane, float(LANES)), axis=-1, keepdims=True)
    rest = jnp.where(lane == i1, -jnp.inf, logits)
    v2 = jnp.max(rest, axis=-1, keepdims=True)
    i2 = jnp.min(jnp.where(rest == v2, lane, float(LANES)), axis=-1, keepdims=True)
    e2 = jnp.exp(v2 - v1)
    w1 = 1.0 / (1.0 + e2)
    w2 = e2 / (1.0 + e2)
    r_ref[...] = (jnp.where(lane == 0.0, i1, 0.0) + jnp.where(lane == 1.0, i2, 0.0)
                  + jnp.where(lane == 2.0, w1, 0.0) + jnp.where(lane == 3.0, w2, 0.0))


def moe_router(x, g, w_router, tm=512):
    t, d = x.shape
    w = jnp.pad(w_router, ((0, 0), (0, LANES - w_router.shape[1])))
    return pl.pallas_call(
        _router_body,
        grid=(t // tm,),
        in_specs=[pl.BlockSpec((tm, d), lambda i: (i, 0)),
                  pl.BlockSpec((1, d), lambda i: (0, 0)),
                  pl.BlockSpec((d, LANES), lambda i: (0, 0))],
        out_specs=[pl.BlockSpec((tm, LANES), lambda i: (i, 0)), pl.BlockSpec((tm, d), lambda i: (i, 0))],
        out_shape=[jax.ShapeDtypeStruct((t, LANES), F32), jax.ShapeDtypeStruct((t, d), BF16)],
        compiler_params=_cparams("parallel"),
        name="moe_router",
    )(x, g.reshape(1, d), w)


MOE_TILE = 512
MOE_GTILE = 256
MOE_CHUNK = 512
MOE_RBLK = 256


def _count_le(sorted_vals, queries):
    return jnp.sum((sorted_vals[None, :] <= queries[:, None]).astype(jnp.int32), axis=1)


def _moe_plan(route, t):
    e = N_EXPERTS
    tm, ck, rb = MOE_TILE, MOE_CHUNK, MOE_RBLK
    n_tiles = (TOP_K * t) // tm + e
    p = n_tiles * tm
    n_chunks = t // ck
    i32 = jnp.int32
    e1 = route[:, 0].astype(i32)
    e2 = route[:, 1].astype(i32)
    oh1 = jax.nn.one_hot(e1, e, dtype=i32)
    oh2 = jax.nn.one_hot(e2, e, dtype=i32)
    cnt = oh1 + oh2
    incl = jnp.cumsum(cnt, axis=0)
    excl = incl - cnt
    n_e = incl[-1]
    g_e = ((n_e + tm - 1) // tm) * tm
    off_end = jnp.cumsum(g_e)
    off = off_end - g_e
    pos_all = off[None, :] + excl
    pos1 = jnp.sum(pos_all * oh1, axis=1)
    pos2 = jnp.sum(pos_all * oh2, axis=1)
    tile_start = jnp.arange(n_tiles, dtype=i32) * tm
    tile_expert = jnp.minimum(_count_le(off_end, tile_start), e - 1)
    tile_sel = jax.nn.one_hot(tile_expert, e, dtype=i32)
    tile_used = (tile_start < jnp.sum(tile_sel * (off + n_e)[None, :], axis=1)).astype(i32)
    gm = MOE_GTILE
    n_gt = p // gm
    gt_start = jnp.arange(n_gt, dtype=i32) * gm
    sel = jax.nn.one_hot(jnp.minimum(_count_le(off_end, gt_start), e - 1), e, dtype=i32)
    r0 = gt_start - jnp.sum(sel * off[None, :], axis=1)
    n_sel = jnp.sum(sel * n_e[None, :], axis=1)
    has_rows = r0 < n_sel
    r1 = jnp.minimum(r0 + gm, n_sel) - 1
    ends = incl[ck - 1::ck]
    ends_sel = jnp.sum(ends[None, :, :] * sel[:, None, :], axis=2)
    c_lo = jnp.where(has_rows, jnp.sum((ends_sel <= r0[:, None]).astype(i32), axis=1), 0)
    c_hi = jnp.where(has_rows, jnp.sum((ends_sel <= r1[:, None]).astype(i32), axis=1), 0)
    n_it = c_hi - c_lo + 1
    it_end = jnp.cumsum(n_it)
    it_start = it_end - n_it
    wmax = n_gt + e * n_chunks
    w_idx = jnp.arange(wmax, dtype=i32)
    g_tile = jnp.minimum(_count_le(it_end, w_idx), n_gt - 1)
    g_valid = w_idx < it_end[-1]
    g_chunk = jnp.where(g_valid, c_lo[g_tile] + (w_idx - it_start[g_tile]), c_hi[n_gt - 1])
    g_first = (g_valid & (w_idx == it_start[g_tile])).astype(i32)
    g_last = (g_valid & (w_idx == it_end[g_tile] - 1)).astype(i32)
    gather = (g_tile, g_chunk.astype(i32), g_first, g_last, g_valid.astype(i32))
    lo = off[None, :] + excl[::ck]
    hi = off[None, :] + ends
    nblk = jnp.where(hi > lo, (hi - 1) // rb - lo // rb + 1, 0).reshape(-1)
    blk_lo = (lo // rb).reshape(-1)
    cb_end = jnp.cumsum(nblk)
    cb_start = cb_end - nblk
    cmax = p // rb + e * n_chunks
    c_idx = jnp.arange(cmax, dtype=i32)
    pair = jnp.minimum(_count_le(cb_end, c_idx), nblk.shape[0] - 1)
    c_valid = c_idx < cb_end[-1]
    c_tile = jnp.where(c_valid, pair // e, n_chunks - 1)
    c_blk = jnp.where(c_valid, blk_lo[pair] + (c_idx - cb_start[pair]), 0)
    per_tile = jnp.sum(nblk.reshape(n_chunks, e), axis=1)
    t_end = jnp.cumsum(per_tile)
    c_first = (c_valid & (c_idx == (t_end - per_tile)[c_tile])).astype(i32)
    c_last = (c_valid & (c_idx == t_end[c_tile] - 1)).astype(i32)
    combine = (c_tile.astype(i32), c_blk.astype(i32), c_first, c_last, c_valid.astype(i32))
    return dict(n_tiles=n_tiles, p=p, tile_expert=tile_expert.astype(i32), tile_used=tile_used, pos1=pos1, pos2=pos2,
                gather=gather, combine=combine)


def _moe_gather_body(tile_ref, chunk_ref, first_ref, last_ref, valid_ref, meta_ref, h_ref,
                     o_ref, wr_ref, acc_ref, wacc_ref):
    w = pl.program_id(0)

    @pl.when(first_ref[w] == 1)
    def _():
        acc_ref[...] = jnp.zeros_like(acc_ref)
        wacc_ref[...] = jnp.zeros_like(wacc_ref)

    @pl.when(valid_ref[w] == 1)
    def _():
        row = (lax.broadcasted_iota(jnp.int32, (MOE_GTILE, MOE_CHUNK), 0) + tile_ref[w] * MOE_GTILE).astype(F32)
        hit1 = row == meta_ref[0:1, :]
        hit2 = row == meta_ref[1:2, :]
        onehot = jnp.where(hit1, 1.0, jnp.where(hit2, 1.0, 0.0)).astype(BF16)
        acc_ref[...] += _dot(onehot, h_ref[...])
        wts = jnp.where(hit1, meta_ref[2:3, :], jnp.where(hit2, meta_ref[3:4, :], 0.0))
        wacc_ref[...] += jnp.sum(wts, axis=1, keepdims=True)

    @pl.when(last_ref[w] == 1)
    def _():
        o_ref[...] = acc_ref[...].astype(o_ref.dtype)
        wr_ref[...] = jnp.broadcast_to(wacc_ref[...], wr_ref.shape)


def _moe_ffn_body(exp_ref, used_ref, x_ref, wr_ref, wg_ref, wu_ref, wd_ref, o_ref, acc_ref):
    f = pl.program_id(1)

    @pl.when(f == 0)
    def _():
        acc_ref[...] = jnp.zeros_like(acc_ref)

    @pl.when(used_ref[pl.program_id(0)] == 1)
    def _():
        x = x_ref[...]
        act = (_silu(_dot(x, wg_ref[...])) * _dot(x, wu_ref[...])).astype(BF16)
        acc_ref[...] += _dot(act, wd_ref[...])

    @pl.when(f == pl.num_programs(1) - 1)
    def _():
        o_ref[...] = (acc_ref[...] * wr_ref[:, 0:1]).astype(o_ref.dtype)


def _moe_combine_body(tile_ref, blk_ref, first_ref, last_ref, valid_ref, x_ref, pos_ref, y_ref, gf_ref,
                      o_ref, acc_ref, *, final_norm):
    w = pl.program_id(0)

    @pl.when(first_ref[w] == 1)
    def _():
        acc_ref[...] = x_ref[...]

    @pl.when(valid_ref[w] == 1)
    def _():
        base = blk_ref[w] * MOE_RBLK
        col = lax.broadcasted_iota(jnp.int32, (MOE_CHUNK, MOE_RBLK), 1) + base
        reps = MOE_RBLK // LANES
        p1 = jnp.concatenate([pos_ref[:, 0:LANES]] * reps, axis=1)
        p2 = jnp.concatenate([pos_ref[:, LANES:2 * LANES]] * reps, axis=1)
        hit = jnp.where(col == p1, 1.0, jnp.where(col == p2, 1.0, 0.0))
        acc_ref[...] += _dot(hit.astype(BF16), y_ref[...])

    @pl.when(last_ref[w] == 1)
    def _():
        o_ref[...] = _rms(acc_ref[...], gf_ref[...]) if final_norm else acc_ref[...]


def moe_ffn(x, h, route, w_gate, w_up, w_down, final_gain=None, tf=1408):
    t, d = x.shape
    f = w_gate.shape[2]
    tm, ck, rb = MOE_TILE, MOE_CHUNK, MOE_RBLK
    plan = _moe_plan(route, t)
    n_tiles, p = plan["n_tiles"], plan["p"]
    assert p < 2 ** 24, "row positions are carried exactly in f32"
    meta = jnp.stack([plan["pos1"].astype(F32), plan["pos2"].astype(F32), route[:, 2], route[:, 3]]
                     + [jnp.zeros((t,), F32)] * (SUBLANES - 4))
    pos_lanes = jnp.concatenate([jnp.broadcast_to(plan["pos1"][:, None], (t, LANES)),
                                 jnp.broadcast_to(plan["pos2"][:, None], (t, LANES))], axis=1)

    xs, w_row = pl.pallas_call(
        _moe_gather_body,
        grid_spec=pltpu.PrefetchScalarGridSpec(
            num_scalar_prefetch=5,
            grid=(plan["gather"][0].shape[0],),
            in_specs=[pl.BlockSpec((SUBLANES, ck), lambda w, tl, ch, fi, la, va: (0, ch[w])),
                      pl.BlockSpec((ck, d), lambda w, tl, ch, fi, la, va: (ch[w], 0))],
            out_specs=[pl.BlockSpec((MOE_GTILE, d), lambda w, tl, ch, fi, la, va: (tl[w], 0)),
                       pl.BlockSpec((MOE_GTILE, LANES), lambda w, tl, ch, fi, la, va: (tl[w], 0))],
            scratch_shapes=[pltpu.VMEM((MOE_GTILE, d), F32), pltpu.VMEM((MOE_GTILE, 1), F32)]),
        out_shape=[jax.ShapeDtypeStruct((p, d), BF16), jax.ShapeDtypeStruct((p, LANES), F32)],
        compiler_params=_cparams("arbitrary"),
        name="moe_gather",
    )(*plan["gather"], meta, h)

    ys = pl.pallas_call(
        _moe_ffn_body,
        grid_spec=pltpu.PrefetchScalarGridSpec(
            num_scalar_prefetch=2,
            grid=(n_tiles, f // tf),
            in_specs=[pl.BlockSpec((tm, d), lambda i, k, ex, us: (i, 0)),
                      pl.BlockSpec((tm, LANES), lambda i, k, ex, us: (i, 0)),
                      pl.BlockSpec((None, d, tf), lambda i, k, ex, us: (ex[i], 0, k)),
                      pl.BlockSpec((None, d, tf), lambda i, k, ex, us: (ex[i], 0, k)),
                      pl.BlockSpec((None, tf, d), lambda i, k, ex, us: (ex[i], k, 0))],
            out_specs=pl.BlockSpec((tm, d), lambda i, k, ex, us: (i, 0)),
            scratch_shapes=[pltpu.VMEM((tm, d), F32)]),
        out_shape=jax.ShapeDtypeStruct((p, d), BF16),
        compiler_params=_cparams("parallel", "arbitrary"),
        name="moe_ffn",
    )(plan["tile_expert"], plan["tile_used"], xs, w_row, w_gate, w_up, w_down)

    final_norm = final_gain is not None
    gain = (final_gain if final_norm else jnp.ones((d,), F32)).reshape(1, d)
    return pl.pallas_call(
        functools.partial(_moe_combine_body, final_norm=final_norm),
        grid_spec=pltpu.PrefetchScalarGridSpec(
            num_scalar_prefetch=5,
            grid=(plan["combine"][0].shape[0],),
            in_specs=[pl.BlockSpec((ck, d), lambda w, tl, bk, fi, la, va: (tl[w], 0)),
                      pl.BlockSpec((ck, 2 * LANES), lambda w, tl, bk, fi, la, va: (tl[w], 0)),
                      pl.BlockSpec((rb, d), lambda w, tl, bk, fi, la, va: (bk[w], 0)),
                      pl.BlockSpec((1, d), lambda w, tl, bk, fi, la, va: (0, 0))],
            out_specs=pl.BlockSpec((ck, d), lambda w, tl, bk, fi, la, va: (tl[w], 0)),
            scratch_shapes=[pltpu.VMEM((ck, d), F32)]),
        out_shape=jax.ShapeDtypeStruct((t, d), F32),
        compiler_params=_cparams("arbitrary"),
        name="moe_combine",
    )(*plan["combine"], x, pos_lanes, ys, gain)


def _rmsnorm_body(x_ref, g_ref, o_ref):
    o_ref[...] = _rms(x_ref[...], g_ref[...])


def rmsnorm(x, g, tm=1024):
    t, d = x.shape
    tm = min(tm, t)
    return pl.pallas_call(
        _rmsnorm_body,
        grid=(t // tm,),
        in_specs=[pl.BlockSpec((tm, d), lambda i: (i, 0)), pl.BlockSpec((1, d), lambda i: (0, 0))],
        out_specs=pl.BlockSpec((tm, d), lambda i: (i, 0)),
        out_shape=jax.ShapeDtypeStruct((t, d), F32),
        compiler_params=_cparams("parallel"),
        name="rmsnorm",
    )(x, g.reshape(1, d))


def _in_proj_slices(w):
    sizes = (MOBA_WIDTH, MOBA_WIDTH, MOBA_WIDTH, SSM_WIDTH, GDN_WIDTH, GDN_WIDTH, GDN_WIDTH,
             GDN_HEADS, GDN_HEADS, GDN_WIDTH, w.shape[0], w.shape[0], w.shape[0])
    parts, start = [], 0
    for size in sizes:
        parts.append(w[:, start:start + size])
        start += size
    return parts


def kernel(x, mem, positions, norm_mix, w_in, ssm_a_re, ssm_a_im, ssm_log_dt, ssm_b_re, ssm_b_im, ssm_c_re, ssm_c_im, ssm_d, ssm_w_glu, gdn_conv, gdn_a_log, gdn_dt_bias, gdn_norm, w_up_moba, w_up_gdn, w_out, norm_xa, norm_mem, xa_wq, xa_wk, xa_wv, xa_wo, norm_ffn, ffn_w_gate, ffn_w_up, ffn_w_down, moe_w_router, moe_w_gate, moe_w_up, moe_w_down, norm_final):
    batch, seq, d = x.shape
    depth = w_in.shape[0]
    t = batch * seq
    assert batch == SUBLANES, "the S5 scan packs the batch onto the 8 sublanes of a vreg"
    assert seq % MOBA_BLOCK == 0 and seq % GDN_CHUNK == 0
    nc = seq // GDN_CHUNK
    bf = lambda a: a.astype(BF16)

    xf = x.reshape(t, d)
    memf = mem.reshape(-1, d)
    cos, sin = rope_tables(positions)

    for l in range(depth):
        (wq_m, wk_m, wv_m, wu_s, wq_g, wk_g, wv_g, wa_g, wb_g, wz_g, wg_a, wg_b, wg_c) = _in_proj_slices(w_in[l])
        g_mix = norm_mix[l]
        w_ab = jnp.pad(jnp.concatenate([wa_g, wb_g], axis=1), ((0, 0), (0, LANES - 2 * GDN_HEADS)))
        w32 = jnp.concatenate([wq_m, wk_m, wu_s, wq_g, wk_g, wv_g, w_ab], axis=1)
        w16 = jnp.concatenate([wv_m, wz_g, wg_a, wg_b, wg_c], axis=1)
        p32, p16 = norm_matmul2(xf, g_mix, bf(w32), bf(w16), F32, BF16, tm=2048, tna=640, tnb=1024)
        c_us = 2 * MOBA_WIDTH
        c_qkv = c_us + SSM_WIDTH
        c_ab = c_qkv + 3 * GDN_WIDTH
        u_s = p32[:, c_us:c_us + SSM_WIDTH]
        ab_g = p32[:, c_ab:c_ab + 2 * GDN_HEADS]

        y_a = moba_attention(p32, p16, cos, sin, batch, seq, q_col=0, k_col=MOBA_WIDTH // LANES, v_col=0)

        l_re, l_im, bb_re, bb_im = s5_discretise(ssm_a_re[l], ssm_a_im[l], ssm_log_dt[l], ssm_b_re[l], ssm_b_im[l])
        maps = s5_block_maps(bb_re, bb_im, ssm_c_re[l], ssm_c_im[l])
        u_tm = u_s.reshape(batch, seq, SSM_WIDTH).transpose(1, 0, 2).reshape(t, SSM_WIDTH)
        y_s_tm = s5_gelu(u_tm, maps, l_re, l_im, ssm_d[l])
        y_s = y_s_tm.reshape(seq, batch, SSM_WIDTH).transpose(1, 0, 2).reshape(t, SSM_WIDTH)

        def head_rows(cols):
            return cols.reshape(batch, nc, GDN_CHUNK, GDN_HEADS).transpose(0, 3, 1, 2)
        a_rows = head_rows(ab_g[:, :GDN_HEADS])
        b_rows = head_rows(ab_g[:, GDN_HEADS:2 * GDN_HEADS])
        y_c = gdn_mixer(p32, p16, a_rows, b_rows, gdn_conv[l], gdn_a_log[l], gdn_dt_bias[l], gdn_norm[l],
                        batch, seq, qkv_col=c_qkv // LANES, z_col=MOBA_WIDTH // GDN_WIDTH)

        xf = merge_project(xf, y_a, y_s, y_c, p16, bf(w_up_moba[l]), bf(ssm_w_glu[l]), bf(w_up_gdn[l]),
                           bf(w_out[l]), gate_col=(MOBA_WIDTH + GDN_WIDTH) // d)

        k_x = norm_matmul(memf, norm_mem[l], bf(xa_wk[l]), BF16)
        v_x = norm_matmul(memf, norm_mem[l], bf(xa_wv[l]), BF16)
        xf = cross_attention(xf, norm_xa[l], bf(xa_wq[l]), k_x, v_x, bf(xa_wo[l]), batch, seq)

        if l % 2 == 0:
            i = l // 2
            xf = ffn(xf, norm_ffn[l], bf(ffn_w_gate[i]), bf(ffn_w_up[i]), bf(ffn_w_down[i]))
        else:
            i = l // 2
            route, h_moe = moe_router(xf, norm_ffn[l], moe_w_router[i])
            xf = moe_ffn(xf, h_moe, route, bf(moe_w_gate[i]), bf(moe_w_up[i]), bf(moe_w_down[i]),
                         final_gain=norm_final if l == depth - 1 else None)

    if depth % 2 == 1:
        xf = rmsnorm(xf, norm_final)
    return xf.reshape(batch, seq, d)
```

```python
import functools

import numpy as np
import jax
import jax.numpy as jnp
from jax import lax
from jax.experimental import pallas as pl
from jax.experimental.pallas import tpu as pltpu

F32 = jnp.float32
BF16 = jnp.bfloat16
HIGHEST = lax.Precision.HIGHEST

EPS = 1e-6
NEG_INF = -1e30
MOBA_HEADS = 8
MOBA_HEAD_DIM = 64
MOBA_WIDTH = MOBA_HEADS * MOBA_HEAD_DIM
MOBA_BLOCK = 256
MOBA_TOPK = 3
ROPE_THETA = 10000.0
SSM_WIDTH = 512
SSM_GROUP = 16
SSM_GROUPS = SSM_WIDTH // SSM_GROUP
SSM_STATE = 64
SSM_STATES = SSM_GROUPS * SSM_STATE
GDN_HEADS = 4
GDN_HEAD_DIM = 128
GDN_WIDTH = GDN_HEADS * GDN_HEAD_DIM
GDN_CONV = 4
GDN_CHUNK = 64
XA_HEADS = 4
XA_HEAD_DIM = 128
XA_WIDTH = XA_HEADS * XA_HEAD_DIM
N_EXPERTS = 8
TOP_K = 2

LANES = 128
SUBLANES = 8
VMEM_LIMIT = 56 * 1024 * 1024


def _cparams(*sem):
    return pltpu.CompilerParams(dimension_semantics=sem, vmem_limit_bytes=VMEM_LIMIT)


def _rms(x, g):
    return x * lax.rsqrt(jnp.mean(x * x, axis=-1, keepdims=True) + EPS) * g


def _sigmoid(x):
    return 1.0 / (1.0 + jnp.exp(-x))


def _silu(x):
    return x * _sigmoid(x)


def _dot(a, b):
    return jnp.dot(a, b, preferred_element_type=F32)


def _dot_nt(a, b):
    return lax.dot_general(a, b, (((1,), (1,)), ((), ())), preferred_element_type=F32)


def _norm_matmul_body(x_ref, g_ref, w_ref, o_ref, h_ref):
    @pl.when(pl.program_id(1) == 0)
    def _():
        h_ref[...] = _rms(x_ref[...], g_ref[...]).astype(BF16)

    o_ref[...] = _dot(h_ref[...], w_ref[...]).astype(o_ref.dtype)


def norm_matmul(x, g, w, out_dtype, tm=512, tn=512):
    t, k = x.shape
    n = w.shape[1]
    tm, tn = min(tm, t), min(tn, n)
    return pl.pallas_call(
        _norm_matmul_body,
        grid=(t // tm, n // tn),
        in_specs=[pl.BlockSpec((tm, k), lambda i, j: (i, 0)),
                  pl.BlockSpec((1, k), lambda i, j: (0, 0)),
                  pl.BlockSpec((k, tn), lambda i, j: (0, j))],
        out_specs=pl.BlockSpec((tm, tn), lambda i, j: (i, j)),
        out_shape=jax.ShapeDtypeStruct((t, n), out_dtype),
        scratch_shapes=[pltpu.VMEM((tm, k), BF16)],
        compiler_params=_cparams("parallel", "arbitrary"),
        name="norm_matmul",
    )(x, g.reshape(1, k), w)


def _norm_matmul2_body(x_ref, g_ref, wa_ref, wb_ref, oa_ref, ob_ref, h_ref, *, na):
    j = pl.program_id(1)

    @pl.when(j == 0)
    def _():
        h_ref[...] = _rms(x_ref[...], g_ref[...]).astype(BF16)

    @pl.when(j < na)
    def _():
        oa_ref[...] = _dot(h_ref[...], wa_ref[...]).astype(oa_ref.dtype)

    @pl.when(j >= na)
    def _():
        ob_ref[...] = _dot(h_ref[...], wb_ref[...]).astype(ob_ref.dtype)


def norm_matmul2(x, g, wa, wb, dtype_a, dtype_b, tm, tna, tnb):
    t, k = x.shape
    tm = min(tm, t)
    na, nb = wa.shape[1] // tna, wb.shape[1] // tnb
    return pl.pallas_call(
        functools.partial(_norm_matmul2_body, na=na),
        grid=(t // tm, na + nb),
        in_specs=[pl.BlockSpec((tm, k), lambda i, j: (i, 0)),
                  pl.BlockSpec((1, k), lambda i, j: (0, 0)),
                  pl.BlockSpec((k, tna), lambda i, j: (0, jnp.minimum(j, na - 1))),
                  pl.BlockSpec((k, tnb), lambda i, j: (0, jnp.maximum(j - na, 0)))],
        out_specs=[pl.BlockSpec((tm, tna), lambda i, j: (i, jnp.minimum(j, na - 1))),
                   pl.BlockSpec((tm, tnb), lambda i, j: (i, jnp.maximum(j - na, 0)))],
        out_shape=[jax.ShapeDtypeStruct((t, wa.shape[1]), dtype_a), jax.ShapeDtypeStruct((t, wb.shape[1]), dtype_b)],
        scratch_shapes=[pltpu.VMEM((tm, k), BF16)],
        compiler_params=_cparams("parallel", "arbitrary"),
        name="norm_matmul2",
    )(x, g.reshape(1, k), wa, wb)


def _rope_body(pos_ref, inv_ref, cos_ref, sin_ref):
    ang = pos_ref[...] * inv_ref[...]
    lane = lax.broadcasted_iota(jnp.int32, ang.shape, 1)
    first_half = (lane % MOBA_HEAD_DIM) < (MOBA_HEAD_DIM // 2)
    cos_ref[...] = jnp.cos(ang)
    s = jnp.sin(ang)
    sin_ref[...] = jnp.where(first_half, -s, s)


def rope_tables(positions):
    t = positions.size
    half = MOBA_HEAD_DIM // 2
    inv = (1.0 / (np.float32(ROPE_THETA) ** (np.arange(0, MOBA_HEAD_DIM, 2, dtype=np.float32)
                                             / np.float32(MOBA_HEAD_DIM)))).astype(np.float32)
    inv_row = jnp.asarray(np.tile(inv, LANES // half).reshape(1, LANES))
    pos = positions.astype(F32).reshape(t, 1)
    ts = min(1024, t)
    return pl.pallas_call(
        _rope_body,
        grid=(t // ts,),
        in_specs=[pl.BlockSpec((ts, 1), lambda i: (i, 0)),
                  pl.BlockSpec((1, LANES), lambda i: (0, 0))],
        out_specs=[pl.BlockSpec((ts, LANES), lambda i: (i, 0))] * 2,
        out_shape=[jax.ShapeDtypeStruct((t, LANES), F32)] * 2,
        compiler_params=_cparams("parallel"),
        name="rope_tables",
    )(pos, inv_row)


def _rope_apply(x, cos, sin_signed):
    lane = lax.broadcasted_iota(jnp.int32, x.shape, 1)
    first_half = (lane % MOBA_HEAD_DIM) < (MOBA_HEAD_DIM // 2)
    partner = jnp.where(first_half, pltpu.roll(x, LANES - MOBA_HEAD_DIM // 2, 1),
                        pltpu.roll(x, MOBA_HEAD_DIM // 2, 1))
    return x * cos + partner * sin_signed


def _moba_body(q_ref, k_ref, v_ref, cq_ref, sq_ref, ck_ref, sk_ref, o_ref, ka_ref, va_ref, km_ref, s_ref,
               mt_ref, acc_ref, *, nb):
    blk = MOBA_BLOCK
    i = pl.program_id(2)

    @pl.when(i == 0)
    def _():
        km_ref[...] = jnp.zeros_like(km_ref)
        lane_b = lax.broadcasted_iota(jnp.int32, (blk, LANES), 1)

        def rope_blk(j, c):
            rows = pl.ds(pl.multiple_of(j * blk, blk), blk)
            kr = _rope_apply(k_ref[rows, :], ck_ref[rows, :], sk_ref[rows, :])
            ka_ref[rows, 0:LANES] = kr.astype(BF16)
            ka_ref[rows, LANES:2 * LANES] = jnp.where(lane_b == j, 1.0, 0.0).astype(BF16)
            km_ref[pl.ds(j, 1), :] = jnp.mean(kr, axis=0, keepdims=True)
            v = v_ref[rows, :].astype(F32)
            va_ref[0, rows, :] = jnp.where(lane_b < MOBA_HEAD_DIM, v,
                                           jnp.where(lane_b == MOBA_HEAD_DIM, 1.0, 0.0)).astype(BF16)
            va_ref[1, rows, :] = jnp.where(lane_b >= MOBA_HEAD_DIM, v,
                                           jnp.where(lane_b == 0, 1.0, 0.0)).astype(BF16)
            return c
        lax.fori_loop(0, nb, rope_blk, 0)

    tq = 2 * blk
    lane = lax.broadcasted_iota(jnp.int32, (1, LANES), 1)
    head_a = lane < MOBA_HEAD_DIM
    q = _rope_apply(q_ref[...], cq_ref[...], sq_ref[...])
    scale = MOBA_HEAD_DIM ** -0.5
    km = km_ref[...]
    nbp = km_ref.shape[0]
    brow = lax.broadcasted_iota(jnp.int32, (nbp, tq), 0)
    qblk = 2 * i + jnp.where(lax.broadcasted_iota(jnp.int32, (1, tq), 1) >= blk, 1, 0)
    heads = (0, 1)
    q_heads = [jnp.where(head_a, q, 0.0), jnp.where(head_a, 0.0, q)]
    gates = [lax.dot_general(km, qh, (((1,), (1,)), ((), ())), preferred_element_type=F32, precision=HIGHEST)
             for qh in q_heads]
    q_augs = []
    brow_f = brow.astype(F32)
    for h in heads:
        gate = jnp.where(brow < qblk, gates[h], NEG_INF)
        picked = jnp.zeros((nbp, tq), F32)
        for _ in range(MOBA_TOPK):
            top = jnp.max(gate, axis=0, keepdims=True)
            first = jnp.min(jnp.where(gate == top, brow_f, float(nbp)), axis=0, keepdims=True)
            hit = brow_f == first
            picked = jnp.where(hit, 1.0, picked)
            gate = jnp.where(hit, -jnp.inf, gate)
        keep = ((picked > 0.5) & (brow < qblk)) | (brow == qblk)
        bias_t = jnp.concatenate([jnp.where(keep, 0.0, NEG_INF), jnp.zeros((LANES - nbp, tq), F32)],
                                 axis=0)
        q_augs.append(jnp.concatenate([q_heads[h] * scale, bias_t.T], axis=1).astype(BF16))

    def pair_rows(jp):
        return pl.ds(pl.multiple_of(jp * tq, tq), tq)

    r_idx = lax.broadcasted_iota(jnp.int32, (tq, tq), 0)
    c_idx = lax.broadcasted_iota(jnp.int32, (tq, tq), 1)
    k_diag = ka_ref[pair_rows(i), :]
    s_diag = [_dot_nt(q_augs[h], k_diag) for h in heads]
    for h in heads:
        s_d = jnp.where(c_idx <= r_idx, s_diag[h], NEG_INF)
        s_ref[h, i] = s_d
        m_t = s_d[:, 0:LANES]
        for col in range(LANES, tq, LANES):
            m_t = jnp.maximum(m_t, s_d[:, col:col + LANES])
        mt_ref[h] = m_t

    def loop2(n, body):
        def two(k, c):
            body([2 * k, 2 * k + 1])
            return c
        lax.fori_loop(0, lax.shift_right_logical(n, 1), two, 0)

        @pl.when((n & 1) == 1)
        def _():
            body([n - 1])

    def scores(jps):
        k2s = [ka_ref[pair_rows(jp), :] for jp in jps]
        s2s = [[_dot_nt(q_augs[h], k2) for h in heads] for k2 in k2s]
        for jp, s2h in zip(jps, s2s):
            for h in heads:
                s_ref[h, jp] = s2h[h]
        for h in heads:
            m_t = mt_ref[h]
            for s2h in s2s:
                for col in range(0, tq, LANES):
                    m_t = jnp.maximum(m_t, s2h[h][:, col:col + LANES])
            mt_ref[h] = m_t
    loop2(i, scores)

    ms = [jnp.max(mt_ref[h], axis=-1, keepdims=True) for h in heads]
    for h in heads:
        acc_ref[h] = jnp.zeros((tq, LANES), F32)

    def values(jps):
        ps = [[jnp.exp(s_ref[h, jp] - ms[h]).astype(BF16) for h in heads] for jp in jps]
        pvs = [[_dot(p[h], va_ref[h, pair_rows(jp), :]) for h in heads] for jp, p in zip(jps, ps)]
        for h in heads:
            tot = pvs[0][h]
            for pv in pvs[1:]:
                tot = tot + pv[h]
            acc_ref[h] += tot
    loop2(i + 1, values)
    acc_a, acc_b = acc_ref[0], acc_ref[1]
    out_a = acc_a / acc_a[:, MOBA_HEAD_DIM:MOBA_HEAD_DIM + 1]
    out_b = acc_b / acc_b[:, 0:1]
    o_ref[...] = jnp.where(head_a, out_a, out_b).astype(o_ref.dtype)


def moba_attention(qk, v, cos, sin, batch, seq, q_col=0, k_col=MOBA_WIDTH // LANES, v_col=0):
    nb = seq // MOBA_BLOCK
    assert nb % 2 == 0
    npair = nb // 2
    pairs = MOBA_WIDTH // LANES
    tq = 2 * MOBA_BLOCK
    t = batch * seq
    return pl.pallas_call(
        functools.partial(_moba_body, nb=nb),
        grid=(batch, pairs, npair),
        in_specs=[pl.BlockSpec((tq, LANES), lambda b, p, i: (b * npair + i, q_col + p)),
                  pl.BlockSpec((seq, LANES), lambda b, p, i: (b, k_col + p)),
                  pl.BlockSpec((seq, LANES), lambda b, p, i: (b, v_col + p)),
                  pl.BlockSpec((tq, LANES), lambda b, p, i: (b * npair + i, 0)),
                  pl.BlockSpec((tq, LANES), lambda b, p, i: (b * npair + i, 0)),
                  pl.BlockSpec((seq, LANES), lambda b, p, i: (b, 0)),
                  pl.BlockSpec((seq, LANES), lambda b, p, i: (b, 0))],
        out_specs=pl.BlockSpec((tq, LANES), lambda b, p, i: (b * npair + i, p)),
        out_shape=jax.ShapeDtypeStruct((t, MOBA_WIDTH), BF16),
        scratch_shapes=[pltpu.VMEM((seq, 2 * LANES), BF16),
                        pltpu.VMEM((2, seq, LANES), BF16),
                        pltpu.VMEM((-(-nb // SUBLANES) * SUBLANES, LANES), F32),
                        pltpu.VMEM((2, npair, tq, tq), F32),
                        pltpu.VMEM((2, tq, LANES), F32),
                        pltpu.VMEM((2, tq, LANES), F32)],
        compiler_params=_cparams("parallel", "parallel", "arbitrary"),
        name="moba_attention",
    )(qk, qk, v, cos, sin, cos, sin)


def _s5_disc_body(are_ref, aim_ref, ldt_ref, bre_ref, bim_ref, lre_ref, lim_ref, bbre_ref, bbim_ref):
    a_re, a_im = are_ref[...], aim_ref[...]
    dt = jnp.exp(ldt_ref[...])
    mag = jnp.exp(a_re * dt)
    l_re = mag * jnp.cos(a_im * dt)
    l_im = mag * jnp.sin(a_im * dt)
    lre_ref[...] = l_re
    lim_ref[...] = l_im
    x, y = l_re - 1.0, l_im
    den = a_re * a_re + a_im * a_im
    c_re = (x * a_re + y * a_im) / den
    c_im = (y * a_re - x * a_im) / den
    b_re, b_im = bre_ref[...], bim_ref[...]
    bbre_ref[...] = c_re * b_re - c_im * b_im
    bbim_ref[...] = c_re * b_im + c_im * b_re


def s5_discretise(a_re, a_im, log_dt, b_re, b_im):
    g, p = a_re.shape
    n = b_re.shape[-1]
    col = lambda a: a.reshape(g * p, 1)
    ldt = jnp.broadcast_to(log_dt[:, None], (g, p))
    outs = pl.pallas_call(
        _s5_disc_body,
        out_shape=[jax.ShapeDtypeStruct((g * p, 1), F32)] * 2 + [jax.ShapeDtypeStruct((g * p, n), F32)] * 2,
        name="s5_discretise",
    )(col(a_re), col(a_im), col(ldt), b_re.reshape(g * p, n), b_im.reshape(g * p, n))
    l_re, l_im, bb_re, bb_im = outs
    return l_re.reshape(g, p), l_im.reshape(g, p), bb_re.reshape(g, p, n), bb_im.reshape(g, p, n)


def _block_diag(blocks):
    g, r, c = blocks.shape
    eye = jnp.eye(g, dtype=blocks.dtype)
    return (blocks[:, :, None, :] * eye[:, None, :, None]).reshape(g * r, g * c)


def _s5_body(u_ref, bre_ref, bim_ref, cre_ref, cim_ref, lre_ref, lim_ref, d_ref, y_ref, h_ref, hb_ref, *, steps):
    rows = SUBLANES
    ns = SSM_STATES
    nq = SSM_WIDTH // LANES
    sq = ns // nq

    @pl.when(pl.program_id(0) == 0)
    def _():
        h_ref[...] = jnp.zeros_like(h_ref)

    u = u_ref[...]
    u16 = u.astype(BF16)
    for q in range(nq):
        uq = u16[:, q * LANES:(q + 1) * LANES]
        hb_ref[:, q * sq:(q + 1) * sq] = _dot(uq, bre_ref[q])
        hb_ref[:, ns + q * sq:ns + (q + 1) * sq] = _dot(uq, bim_ref[q])
    l_re, l_im = lre_ref[...], lim_ref[...]

    def step(t, carry):
        h_re, h_im = carry
        r = pl.ds(pl.multiple_of(t * rows, rows), rows)
        n_re = l_re * h_re - l_im * h_im + hb_ref[r, 0:ns]
        n_im = l_re * h_im + l_im * h_re + hb_ref[r, ns:2 * ns]
        hb_ref[r, 0:ns] = n_re
        hb_ref[r, ns:2 * ns] = n_im
        return n_re, n_im

    h_re, h_im = lax.fori_loop(0, steps, step, (h_ref[:, 0:ns], h_ref[:, ns:2 * ns]), unroll=4)
    h_ref[:, 0:ns] = h_re
    h_ref[:, ns:2 * ns] = h_im
    ys = []
    for q in range(nq):
        ys.append(_dot(hb_ref[:, q * sq:(q + 1) * sq].astype(BF16), cre_ref[q])
                  + _dot(hb_ref[:, ns + q * sq:ns + (q + 1) * sq].astype(BF16), cim_ref[q]))
    y = jnp.concatenate(ys, axis=1) + d_ref[...] * u
    y_ref[...] = jax.nn.gelu(y).astype(y_ref.dtype)


def s5_block_maps(bb_re, bb_im, c_re, c_im):
    nq = SSM_WIDTH // LANES
    gq = SSM_GROUPS // nq
    per_q = lambda a: jnp.stack([_block_diag(a[q * gq:(q + 1) * gq].transpose(0, 2, 1)) for q in range(nq)])
    return (per_q(bb_re).astype(BF16), per_q(bb_im).astype(BF16),
            per_q(c_re).astype(BF16), per_q(-c_im).astype(BF16))


def s5_gelu(u_tm, maps, l_re, l_im, d_skip, steps=64):
    rows = u_tm.shape[0]
    blk = steps * SUBLANES
    ns2 = 2 * SSM_STATES
    full3 = lambda a: pl.BlockSpec(a.shape, lambda c: (0, 0, 0))
    return pl.pallas_call(
        functools.partial(_s5_body, steps=steps),
        grid=(rows // blk,),
        in_specs=[pl.BlockSpec((blk, SSM_WIDTH), lambda c: (c, 0)),
                  full3(maps[0]), full3(maps[1]), full3(maps[2]), full3(maps[3]),
                  pl.BlockSpec((1, SSM_STATES), lambda c: (0, 0)),
                  pl.BlockSpec((1, SSM_STATES), lambda c: (0, 0)),
                  pl.BlockSpec((1, SSM_WIDTH), lambda c: (0, 0))],
        out_specs=pl.BlockSpec((blk, SSM_WIDTH), lambda c: (c, 0)),
        out_shape=jax.ShapeDtypeStruct((rows, SSM_WIDTH), BF16),
        scratch_shapes=[pltpu.VMEM((SUBLANES, ns2), F32),
                        pltpu.VMEM((blk, ns2), F32)],
        compiler_params=_cparams("arbitrary"),
        name="s5_scan",
    )(u_tm, *maps, l_re.reshape(1, -1), l_im.reshape(1, -1), d_skip.reshape(1, -1))


GDN_GROUP = 16


def _gdn_prep_body(q_ref, k_ref, v_ref, a_ref, b_ref, wq_ref, wk_ref, wv_ref, alog_ref, dtb_ref,
                   u_ref, w_ref, qd_ref, kd_ref, qk_ref, gl_ref, qs_ref, ks_ref, vs_ref,
                   *, seq):
    L = GDN_CHUNK
    G = min(GDN_GROUP, seq // L)
    dk = GDN_HEAD_DIM
    nc = seq // L
    hd = pl.program_id(1)
    cb = 256
    halo = SUBLANES

    def conv_blk(rb, c):
        base = pl.multiple_of(rb * cb, cb)
        prev = pl.multiple_of(jnp.maximum(base - halo, 0), halo)
        for src, wref, dst, norm, mul in ((q_ref, wq_ref, qs_ref, True, dk ** -0.5),
                                          (k_ref, wk_ref, ks_ref, True, 1.0),
                                          (v_ref, wv_ref, vs_ref, False, 1.0)):
            cur = src[pl.ds(base, cb), :]
            head = jnp.where(rb > 0, src[pl.ds(prev, halo), :], 0.0)
            ext = jnp.concatenate([head, cur], axis=0)
            wts = wref[...]
            y = ext[halo:halo + cb] * wts[GDN_CONV - 1:GDN_CONV]
            for tap in range(1, GDN_CONV):
                y = y + ext[halo - tap:halo - tap + cb] * wts[GDN_CONV - 1 - tap:GDN_CONV - tap]
            y = _silu(y)
            if norm:
                y = y * lax.rsqrt(jnp.sum(y * y, axis=-1, keepdims=True) + EPS) * mul
            dst[pl.ds(base, cb), :] = y
        return c
    lax.fori_loop(0, seq // cb, conv_blk, 0)

    gl_rows = G * L
    ri = lax.broadcasted_iota(jnp.int32, (gl_rows, L), 0) % L
    ci = lax.broadcasted_iota(jnp.int32, (gl_rows, L), 1)
    tril = ci <= ri
    strict = ci < ri
    eye = ci == ri
    ri3 = lax.broadcasted_iota(jnp.int32, (G, L, L), 1)
    ci3 = lax.broadcasted_iota(jnp.int32, (G, L, L), 2)
    strict3 = ci3 < ri3
    eye_f = jnp.where(ci3 == ri3, 1.0, 0.0)
    r2 = lax.broadcasted_iota(jnp.int32, (L, L), 0)
    c2 = lax.broadcasted_iota(jnp.int32, (L, L), 1)
    upper_f = jnp.where(r2 <= c2, 1.0, 0.0)
    a_coef = -jnp.exp(jnp.full((1, L), alog_ref[hd], F32))
    dt_bias = dtb_ref[hd]
    hp = functools.partial(jnp.dot, preferred_element_type=F32, precision=HIGHEST)

    def bmm(a, b):
        return jnp.einsum('gij,gjk->gik', a, b, preferred_element_type=F32)

    def bmm_nt(a, b):
        return jnp.einsum('gid,gjd->gij', a, b, preferred_element_type=F32)

    def to_col(rows_b):
        return jnp.sum(jnp.where(eye, rows_b, 0.0), axis=1, keepdims=True)

    def prep(cg, carry):
        c0 = pl.multiple_of(cg * G, G)
        rows = pl.ds(pl.multiple_of(cg * gl_rows, gl_rows), gl_rows)
        qc, kc, vc = qs_ref[rows, :], ks_ref[rows, :], vs_ref[rows, :]
        z = a_ref[pl.ds(c0, G), :] + dt_bias
        softplus = jnp.maximum(z, 0.0) + jnp.log(1.0 + jnp.exp(-jnp.abs(z)))
        g_rows = hp(a_coef * softplus, upper_f)
        beta_rows = _sigmoid(b_ref[pl.ds(c0, G), :])
        spread = lambda r: jnp.concatenate([jnp.broadcast_to(r[c:c + 1, :], (L, L)) for c in range(G)], axis=0)
        g_rb = spread(g_rows)
        g_col = to_col(g_rb)
        beta_col = to_col(spread(beta_rows))
        g_last = g_rb[:, L - 1:L]
        decay = jnp.where(tril, jnp.exp(jnp.where(tril, g_col - g_rb, 0.0)), 0.0)
        k_beta = kc * beta_col
        v_beta = vc * beta_col
        to3 = lambda a: a.reshape(G, L, a.shape[-1])
        kb16, k16, q16 = to3(k_beta.astype(BF16)), to3(kc.astype(BF16)), to3(qc.astype(BF16))
        kk_qk = bmm_nt(jnp.concatenate([kb16, q16], axis=1), k16)
        decay3 = to3(decay)
        a_low = jnp.where(strict3, kk_qk[:, :L] * decay3, 0.0)
        qk = kk_qk[:, L:] * decay3
        t_inv = eye_f - a_low
        a16 = a_low.astype(BF16)
        pw = bmm(a16, a16)
        span = 2
        while 2 * span < L:
            pw16 = pw.astype(BF16)
            both = bmm(jnp.concatenate([t_inv.astype(BF16), pw16], axis=1), pw16)
            t_inv = t_inv + both[:, :L]
            pw = both[:, L:]
            span *= 2
        t_inv = t_inv + bmm(t_inv.astype(BF16), pw.astype(BF16))
        rhs = jnp.concatenate([v_beta.astype(BF16), (k_beta * jnp.exp(g_col)).astype(BF16)], axis=1)
        uw = bmm(t_inv.astype(BF16), to3(rhs)).reshape(gl_rows, 2 * dk)
        u_ref[rows, :] = uw[:, :dk]
        w_ref[rows, :] = uw[:, dk:].astype(BF16)
        qd_ref[rows, :] = (qc * jnp.exp(g_col)).astype(BF16)
        k_dec = to3(kc * jnp.exp(g_last - g_col))
        kd_ref[pl.ds(c0, G)] = jnp.swapaxes(k_dec, 1, 2).astype(BF16)
        qk_ref[pl.ds(c0, G)] = qk.astype(BF16)
        gl_ref[pl.ds(c0, G), :] = jnp.broadcast_to(jnp.exp(g_rows[:, L - 1:L]), (G, LANES))
        return carry
    lax.fori_loop(0, nc // G, prep, 0)


def _gdn_scan_body(u_ref, w_ref, qd_ref, kd_ref, qk_ref, gl_ref, z_ref, gn_ref, o_ref, *st_refs, chunks):
    L = GDN_CHUNK
    d = GDN_HEAD_DIM
    nrow = u_ref.shape[0]

    @pl.when(pl.program_id(1) == 0)
    def _():
        for st in st_refs:
            st[...] = jnp.zeros_like(st)

    gn = gn_ref[...]

    def chunk(c, carry):
        rows = pl.ds(pl.multiple_of(c * L, L), L)
        chains = [(r, h) for r in range(nrow) for h in range(GDN_HEADS)]
        states = [st_refs[r * GDN_HEADS + h][...] for r, h in chains]
        firsts = []
        for (r, h), s in zip(chains, states):
            cols = slice(h * d, (h + 1) * d)
            firsts.append(_dot(jnp.concatenate([w_ref[r, rows, cols], qd_ref[r, rows, cols]], axis=0),
                               s.astype(BF16)))
        vns = []
        for (r, h), both in zip(chains, firsts):
            cols = slice(h * d, (h + 1) * d)
            vns.append((u_ref[r, rows, cols] - both[:L]).astype(BF16))
        outs = [both[L:] + _dot(qk_ref[r, h, c], vn) for (r, h), both, vn in zip(chains, firsts, vns)]
        for (r, h), s, vn in zip(chains, states, vns):
            st_refs[r * GDN_HEADS + h][...] = s * gl_ref[r, h, pl.ds(c, 1), :] + _dot(kd_ref[r, h, c], vn)
        for (r, h), o in zip(chains, outs):
            cols = slice(h * d, (h + 1) * d)
            o_ref[r, rows, cols] = (_rms(o, gn) * _silu(z_ref[r, rows, cols].astype(F32))).astype(o_ref.dtype)
        return carry
    lax.fori_loop(0, chunks, chunk, 0)


def gdn_mixer(qkv, z, a_rows, b_rows, conv_w, a_log, dt_bias, gnorm, batch, seq, ts=1024, qkv_col=0, z_col=0):
    t = batch * seq
    hds = GDN_HEADS
    nc = seq // GDN_CHUNK
    L = GDN_CHUNK
    d = GDN_HEAD_DIM
    seq_spec = lambda off: pl.BlockSpec((seq, d), lambda b, h: (b, qkv_col + off + h))
    w_spec = lambda off: pl.BlockSpec((GDN_CONV, d), lambda b, h: (0, off + h))
    row_spec = pl.BlockSpec((None, None, nc, L), lambda b, h: (b, h, 0, 0))
    smem = pl.BlockSpec(memory_space=pltpu.SMEM)
    head_out = pl.BlockSpec((seq, d), lambda b, h: (b, h))
    u, w, qd, kd, qk, gl = pl.pallas_call(
        functools.partial(_gdn_prep_body, seq=seq),
        grid=(batch, hds),
        in_specs=[seq_spec(0), seq_spec(hds), seq_spec(2 * hds), row_spec, row_spec,
                  w_spec(0), w_spec(hds), w_spec(2 * hds), smem, smem],
        out_specs=[head_out, head_out, head_out,
                   pl.BlockSpec((None, None, nc, d, L), lambda b, h: (b, h, 0, 0, 0)),
                   pl.BlockSpec((None, None, nc, L, L), lambda b, h: (b, h, 0, 0, 0)),
                   pl.BlockSpec((None, None, nc, LANES), lambda b, h: (b, h, 0, 0))],
        out_shape=[jax.ShapeDtypeStruct((t, GDN_WIDTH), F32),
                   jax.ShapeDtypeStruct((t, GDN_WIDTH), BF16),
                   jax.ShapeDtypeStruct((t, GDN_WIDTH), BF16),
                   jax.ShapeDtypeStruct((batch, hds, nc, d, L), BF16),
                   jax.ShapeDtypeStruct((batch, hds, nc, L, L), BF16),
                   jax.ShapeDtypeStruct((batch, hds, nc, LANES), F32)],
        scratch_shapes=[pltpu.VMEM((seq, d), F32), pltpu.VMEM((seq, d), F32), pltpu.VMEM((seq, d), F32)],
        compiler_params=_cparams("parallel", "parallel"),
        name="gdn_prep",
    )(qkv, qkv, qkv, a_rows, b_rows, conv_w, conv_w, conv_w, a_log, dt_bias)

    ts = min(ts, seq)
    spb = seq // ts
    cps = ts // L
    nrow = 2 if batch % 2 == 0 else 1
    by_row = lambda a: a.reshape(batch, seq, a.shape[-1])
    tok = pl.BlockSpec((nrow, ts, GDN_WIDTH), lambda b, s: (b, s, 0))
    out = pl.pallas_call(
        functools.partial(_gdn_scan_body, chunks=cps),
        grid=(batch // nrow, spb),
        in_specs=[tok, tok, tok,
                  pl.BlockSpec((nrow, hds, cps, d, L), lambda b, s: (b, 0, s, 0, 0)),
                  pl.BlockSpec((nrow, hds, cps, L, L), lambda b, s: (b, 0, s, 0, 0)),
                  pl.BlockSpec((nrow, hds, cps, LANES), lambda b, s: (b, 0, s, 0)),
                  pl.BlockSpec((nrow, ts, GDN_WIDTH), lambda b, s: (b, s, z_col)),
                  pl.BlockSpec((1, d), lambda b, s: (0, 0))],
        out_specs=tok,
        out_shape=jax.ShapeDtypeStruct((batch, seq, GDN_WIDTH), BF16),
        scratch_shapes=[pltpu.VMEM((d, d), F32)] * (nrow * hds),
        compiler_params=_cparams("parallel", "arbitrary"),
        name="gdn_scan",
    )(by_row(u), by_row(w), by_row(qd), kd, qk, gl, by_row(z), gnorm.reshape(1, d))
    return out.reshape(t, GDN_WIDTH)


def _merge_xattn_body(x_ref, ya_ref, ys_ref, yc_ref, ga_ref, gb_ref, gc_ref, wa_ref, wg_ref, wc_ref, wo_ref,
                      gx_ref, wq_ref, k_ref, v_ref, wxo_ref, o_ref):
    d = x_ref.shape[1]
    y_a = _dot(ya_ref[...], wa_ref[...])
    glu = _dot(ys_ref[...], wg_ref[...])
    y_b = glu[:, :d] * _sigmoid(glu[:, d:])
    y_c = _dot(yc_ref[...], wc_ref[...])
    gate = lambda r: _sigmoid(r[...].astype(F32))
    merged = gate(ga_ref) * y_a + gate(gb_ref) * y_b + gate(gc_ref) * y_c
    x1 = x_ref[...] + _dot(merged.astype(BF16), wo_ref[...])

    h = _rms(x1, gx_ref[...]).astype(BF16)
    q = _dot(h, wq_ref[...]).astype(BF16)
    cols = [slice(hd * XA_HEAD_DIM, (hd + 1) * XA_HEAD_DIM) for hd in range(XA_HEADS)]
    scores = [_dot_nt(q[:, c], k_ref[:, c]) * (XA_HEAD_DIM ** -0.5) for c in cols]
    probs = []
    for s in scores:
        p = jnp.exp(s - jnp.max(s, axis=-1, keepdims=True))
        probs.append((p / jnp.sum(p, axis=-1, keepdims=True)).astype(BF16))
    outs = [_dot(p, v_ref[:, c]) for p, c in zip(probs, cols)]
    o = jnp.concatenate(outs, axis=1).astype(BF16)
    o_ref[...] = x1 + _dot(o, wxo_ref[...])


def merge_cross_attention(x, y_a, y_s, y_c, gates, w_a, w_glu, w_c, w_o, g_xa, wq, k, v, wo_xa, batch, seq,
                          tm=512, gate_col=0):
    t, d = x.shape
    m = k.shape[0] // batch
    spb = seq // tm
    row = lambda n: pl.BlockSpec((tm, n), lambda i: (i, 0))
    gate = lambda j: pl.BlockSpec((tm, d), lambda i: (i, gate_col + j))
    full = lambda a: pl.BlockSpec(a.shape, lambda i: (0, 0))
    mem = pl.BlockSpec((m, XA_WIDTH), lambda i: (i // spb, 0))
    return pl.pallas_call(
        _merge_xattn_body,
        grid=(t // tm,),
        in_specs=[row(d), row(y_a.shape[1]), row(y_s.shape[1]), row(y_c.shape[1]), gate(0), gate(1), gate(2),
                  full(w_a), full(w_glu), full(w_c), full(w_o),
                  pl.BlockSpec((1, d), lambda i: (0, 0)), full(wq), mem, mem, full(wo_xa)],
        out_specs=row(d),
        out_shape=jax.ShapeDtypeStruct((t, d), F32),
        compiler_params=_cparams("parallel"),
        name="merge_cross_attention",
    )(x, y_a, y_s, y_c, gates, gates, gates, w_a, w_glu, w_c, w_o, g_xa.reshape(1, d), wq, k, v, wo_xa)


def _ffn_body(x_ref, g_ref, wg_ref, wu_ref, wd_ref, o_ref, h_ref, acc_ref):
    f = pl.program_id(1)

    @pl.when(f == 0)
    def _():
        h_ref[...] = _rms(x_ref[...], g_ref[...]).astype(BF16)
        acc_ref[...] = jnp.zeros_like(acc_ref)

    h = h_ref[...]
    act = (_silu(_dot(h, wg_ref[...])) * _dot(h, wu_ref[...])).astype(BF16)
    acc_ref[...] += _dot(act, wd_ref[...])

    @pl.when(f == pl.num_programs(1) - 1)
    def _():
        o_ref[...] = x_ref[...] + acc_ref[...]


def ffn(x, g, w_gate, w_up, w_down, tm=512, tf=1408):
    t, d = x.shape
    f = w_gate.shape[1]
    return pl.pallas_call(
        _ffn_body,
        grid=(t // tm, f // tf),
        in_specs=[pl.BlockSpec((tm, d), lambda i, k: (i, 0)),
                  pl.BlockSpec((1, d), lambda i, k: (0, 0)),
                  pl.BlockSpec((d, tf), lambda i, k: (0, k)),
                  pl.BlockSpec((d, tf), lambda i, k: (0, k)),
                  pl.BlockSpec((tf, d), lambda i, k: (k, 0))],
        out_specs=pl.BlockSpec((tm, d), lambda i, k: (i, 0)),
        out_shape=jax.ShapeDtypeStruct((t, d), F32),
        scratch_shapes=[pltpu.VMEM((tm, d), BF16), pltpu.VMEM((tm, d), F32)],
        compiler_params=_cparams("parallel", "arbitrary"),
        name="ffn",
    )(x, g.reshape(1, d), w_gate, w_up, w_down)


def _router_body(x_ref, g_ref, w_ref, r_ref, h_ref):
    h = _rms(x_ref[...], g_ref[...])
    h_ref[...] = h.astype(BF16)
    logits = jnp.dot(h, w_ref[...], preferred_element_type=F32, precision=HIGHEST)
    lane = lax.broadcasted_iota(jnp.int32, logits.shape, 1).astype(F32)
    logits = jnp.where(lane < N_EXPERTS, logits, -jnp.inf)
    v1 = jnp.max(logits, axis=-1, keepdims=True)
    i1 = jnp.min(jnp.where(logits == v1, lane, float(LANES)), axis=-1, keepdims=True)
    rest = jnp.where(lane == i1, -jnp.inf, logits)
    v2 = jnp.max(rest, axis=-1, keepdims=True)
    i2 = jnp.min(jnp.where(rest == v2, lane, float(LANES)), axis=-1, keepdims=True)
    e2 = jnp.exp(v2 - v1)
    w1 = 1.0 / (1.0 + e2)
    w2 = e2 / (1.0 + e2)
    r_ref[...] = (jnp.where(lane == 0.0, i1, 0.0) + jnp.where(lane == 1.0, i2, 0.0)
                  + jnp.where(lane == 2.0, w1, 0.0) + jnp.where(lane == 3.0, w2, 0.0))


def moe_router(x, g, w_router, tm=512):
    t, d = x.shape
    w = jnp.pad(w_router, ((0, 0), (0, LANES - w_router.shape[1])))
    return pl.pallas_call(
        _router_body,
        grid=(t // tm,),
        in_specs=[pl.BlockSpec((tm, d), lambda i: (i, 0)),
                  pl.BlockSpec((1, d), lambda i: (0, 0)),
                  pl.BlockSpec((d, LANES), lambda i: (0, 0))],
        out_specs=[pl.BlockSpec((tm, LANES), lambda i: (i, 0)), pl.BlockSpec((tm, d), lambda i: (i, 0))],
        out_shape=[jax.ShapeDtypeStruct((t, LANES), F32), jax.ShapeDtypeStruct((t, d), BF16)],
        compiler_params=_cparams("parallel"),
        name="moe_router",
    )(x, g.reshape(1, d), w)


MOE_TILE = 512
MOE_GTILE = 256
MOE_CHUNK = 512
MOE_RBLK = 256


def _count_le(sorted_vals, queries):
    return jnp.sum((sorted_vals[None, :] <= queries[:, None]).astype(jnp.int32), axis=1)


def _moe_plan(route, t):
    e = N_EXPERTS
    tm, ck, rb = MOE_TILE, MOE_CHUNK, MOE_RBLK
    n_tiles = (TOP_K * t) // tm + e
    p = n_tiles * tm
    n_chunks = t // ck
    i32 = jnp.int32
    e1 = route[:, 0].astype(i32)
    e2 = route[:, 1].astype(i32)
    oh1 = jax.nn.one_hot(e1, e, dtype=i32)
    oh2 = jax.nn.one_hot(e2, e, dtype=i32)
    cnt = oh1 + oh2
    incl = jnp.cumsum(cnt, axis=0)
    excl = incl - cnt
    n_e = incl[-1]
    g_e = ((n_e + tm - 1) // tm) * tm
    off_end = jnp.cumsum(g_e)
    off = off_end - g_e
    pos_all = off[None, :] + excl
    pos1 = jnp.sum(pos_all * oh1, axis=1)
    pos2 = jnp.sum(pos_all * oh2, axis=1)
    tile_start = jnp.arange(n_tiles, dtype=i32) * tm
    tile_expert = jnp.minimum(_count_le(off_end, tile_start), e - 1)
    tile_sel = jax.nn.one_hot(tile_expert, e, dtype=i32)
    tile_used = (tile_start < jnp.sum(tile_sel * (off + n_e)[None, :], axis=1)).astype(i32)
    gm = MOE_GTILE
    n_gt = p // gm
    gt_start = jnp.arange(n_gt, dtype=i32) * gm
    sel = jax.nn.one_hot(jnp.minimum(_count_le(off_end, gt_start), e - 1), e, dtype=i32)
    r0 = gt_start - jnp.sum(sel * off[None, :], axis=1)
    n_sel = jnp.sum(sel * n_e[None, :], axis=1)
    has_rows = r0 < n_sel
    r1 = jnp.minimum(r0 + gm, n_sel) - 1
    ends = incl[ck - 1::ck]
    ends_sel = jnp.sum(ends[None, :, :] * sel[:, None, :], axis=2)
    c_lo = jnp.where(has_rows, jnp.sum((ends_sel <= r0[:, None]).astype(i32), axis=1), 0)
    c_hi = jnp.where(has_rows, jnp.sum((ends_sel <= r1[:, None]).astype(i32), axis=1), 0)
    n_it = c_hi - c_lo + 1
    it_end = jnp.cumsum(n_it)
    it_start = it_end - n_it
    wmax = n_gt + e * n_chunks
    w_idx = jnp.arange(wmax, dtype=i32)
    g_tile = jnp.minimum(_count_le(it_end, w_idx), n_gt - 1)
    g_valid = w_idx < it_end[-1]
    g_chunk = jnp.where(g_valid, c_lo[g_tile] + (w_idx - it_start[g_tile]), c_hi[n_gt - 1])
    g_first = (g_valid & (w_idx == it_start[g_tile])).astype(i32)
    g_last = (g_valid & (w_idx == it_end[g_tile] - 1)).astype(i32)
    gather = (g_tile, g_chunk.astype(i32), g_first, g_last, g_valid.astype(i32))
    lo = off[None, :] + excl[::ck]
    hi = off[None, :] + ends
    nblk = jnp.where(hi > lo, (hi - 1) // rb - lo // rb + 1, 0).reshape(-1)
    blk_lo = (lo // rb).reshape(-1)
    cb_end = jnp.cumsum(nblk)
    cb_start = cb_end - nblk
    cmax = p // rb + e * n_chunks
    c_idx = jnp.arange(cmax, dtype=i32)
    pair = jnp.minimum(_count_le(cb_end, c_idx), nblk.shape[0] - 1)
    c_valid = c_idx < cb_end[-1]
    c_tile = jnp.where(c_valid, pair // e, n_chunks - 1)
    c_blk = jnp.where(c_valid, blk_lo[pair] + (c_idx - cb_start[pair]), 0)
    per_tile = jnp.sum(nblk.reshape(n_chunks, e), axis=1)
    t_end = jnp.cumsum(per_tile)
    c_first = (c_valid & (c_idx == (t_end - per_tile)[c_tile])).astype(i32)
    c_last = (c_valid & (c_idx == t_end[c_tile] - 1)).astype(i32)
    combine = (c_tile.astype(i32), c_blk.astype(i32), c_first, c_last, c_valid.astype(i32))
    return dict(n_tiles=n_tiles, p=p, tile_expert=tile_expert.astype(i32), tile_used=tile_used, pos1=pos1, pos2=pos2,
                gather=gather, combine=combine)


def _moe_gather_body(tile_ref, chunk_ref, first_ref, last_ref, valid_ref, meta_ref, h_ref,
                     o_ref, wr_ref, acc_ref, wacc_ref):
    w = pl.program_id(0)

    @pl.when(first_ref[w] == 1)
    def _():
        acc_ref[...] = jnp.zeros_like(acc_ref)
        wacc_ref[...] = jnp.zeros_like(wacc_ref)

    @pl.when(valid_ref[w] == 1)
    def _():
        row = (lax.broadcasted_iota(jnp.int32, (MOE_GTILE, MOE_CHUNK), 0) + tile_ref[w] * MOE_GTILE).astype(F32)
        hit1 = row == meta_ref[0:1, :]
        hit2 = row == meta_ref[1:2, :]
        onehot = jnp.where(hit1, 1.0, jnp.where(hit2, 1.0, 0.0)).astype(BF16)
        acc_ref[...] += _dot(onehot, h_ref[...])
        wts = jnp.where(hit1, meta_ref[2:3, :], jnp.where(hit2, meta_ref[3:4, :], 0.0))
        wacc_ref[...] += jnp.sum(wts, axis=1, keepdims=True)

    @pl.when(last_ref[w] == 1)
    def _():
        o_ref[...] = acc_ref[...].astype(o_ref.dtype)
        wr_ref[...] = jnp.broadcast_to(wacc_ref[...], wr_ref.shape)


def _moe_ffn_body(exp_ref, used_ref, x_ref, wr_ref, wg_ref, wu_ref, wd_ref, o_ref, acc_ref):
    f = pl.program_id(1)

    @pl.when(f == 0)
    def _():
        acc_ref[...] = jnp.zeros_like(acc_ref)

    @pl.when(used_ref[pl.program_id(0)] == 1)
    def _():
        x = x_ref[...]
        act = (_silu(_dot(x, wg_ref[...])) * _dot(x, wu_ref[...])).astype(BF16)
        acc_ref[...] += _dot(act, wd_ref[...])

    @pl.when(f == pl.num_programs(1) - 1)
    def _():
        o_ref[...] = (acc_ref[...] * wr_ref[:, 0:1]).astype(o_ref.dtype)


def _moe_combine_body(tile_ref, blk_ref, first_ref, last_ref, valid_ref, x_ref, pos_ref, y_ref, gf_ref,
                      o_ref, acc_ref, *, final_norm):
    w = pl.program_id(0)

    @pl.when(first_ref[w] == 1)
    def _():
        acc_ref[...] = x_ref[...]

    @pl.when(valid_ref[w] == 1)
    def _():
        base = blk_ref[w] * MOE_RBLK
        col = lax.broadcasted_iota(jnp.int32, (MOE_CHUNK, MOE_RBLK), 1) + base
        reps = MOE_RBLK // LANES
        p1 = jnp.concatenate([pos_ref[:, 0:LANES]] * reps, axis=1)
        p2 = jnp.concatenate([pos_ref[:, LANES:2 * LANES]] * reps, axis=1)
        hit = jnp.where(col == p1, 1.0, jnp.where(col == p2, 1.0, 0.0))
        acc_ref[...] += _dot(hit.astype(BF16), y_ref[...])

    @pl.when(last_ref[w] == 1)
    def _():
        o_ref[...] = _rms(acc_ref[...], gf_ref[...]) if final_norm else acc_ref[...]


def moe_ffn(x, h, route, w_gate, w_up, w_down, final_gain=None, tf=1408):
    t, d = x.shape
    f = w_gate.shape[2]
    tm, ck, rb = MOE_TILE, MOE_CHUNK, MOE_RBLK
    plan = _moe_plan(route, t)
    n_tiles, p = plan["n_tiles"], plan["p"]
    assert p < 2 ** 24, "row positions are carried exactly in f32"
    meta = jnp.stack([plan["pos1"].astype(F32), plan["pos2"].astype(F32), route[:, 2], route[:, 3]]
                     + [jnp.zeros((t,), F32)] * (SUBLANES - 4))
    pos_lanes = jnp.concatenate([jnp.broadcast_to(plan["pos1"][:, None], (t, LANES)),
                                 jnp.broadcast_to(plan["pos2"][:, None], (t, LANES))], axis=1)

    xs, w_row = pl.pallas_call(
        _moe_gather_body,
        grid_spec=pltpu.PrefetchScalarGridSpec(
            num_scalar_prefetch=5,
            grid=(plan["gather"][0].shape[0],),
            in_specs=[pl.BlockSpec((SUBLANES, ck), lambda w, tl, ch, fi, la, va: (0, ch[w])),
                      pl.BlockSpec((ck, d), lambda w, tl, ch, fi, la, va: (ch[w], 0))],
            out_specs=[pl.BlockSpec((MOE_GTILE, d), lambda w, tl, ch, fi, la, va: (tl[w], 0)),
                       pl.BlockSpec((MOE_GTILE, LANES), lambda w, tl, ch, fi, la, va: (tl[w], 0))],
            scratch_shapes=[pltpu.VMEM((MOE_GTILE, d), F32), pltpu.VMEM((MOE_GTILE, 1), F32)]),
        out_shape=[jax.ShapeDtypeStruct((p, d), BF16), jax.ShapeDtypeStruct((p, LANES), F32)],
        compiler_params=_cparams("arbitrary"),
        name="moe_gather",
    )(*plan["gather"], meta, h)

    ys = pl.pallas_call(
        _moe_ffn_body,
        grid_spec=pltpu.PrefetchScalarGridSpec(
            num_scalar_prefetch=2,
            grid=(n_tiles, f // tf),
            in_specs=[pl.BlockSpec((tm, d), lambda i, k, ex, us: (i, 0)),
                      pl.BlockSpec((tm, LANES), lambda i, k, ex, us: (i, 0)),
                      pl.BlockSpec((None, d, tf), lambda i, k, ex, us: (ex[i], 0, k)),
                      pl.BlockSpec((None, d, tf), lambda i, k, ex, us: (ex[i], 0, k)),
                      pl.BlockSpec((None, tf, d), lambda i, k, ex, us: (ex[i], k, 0))],
            out_specs=pl.BlockSpec((tm, d), lambda i, k, ex, us: (i, 0)),
            scratch_shapes=[pltpu.VMEM((tm, d), F32)]),
        out_shape=jax.ShapeDtypeStruct((p, d), BF16),
        compiler_params=_cparams("parallel", "arbitrary"),
        name="moe_ffn",
    )(plan["tile_expert"], plan["tile_used"], xs, w_row, w_gate, w_up, w_down)

    final_norm = final_gain is not None
    gain = (final_gain if final_norm else jnp.ones((d,), F32)).reshape(1, d)
    return pl.pallas_call(
        functools.partial(_moe_combine_body, final_norm=final_norm),
        grid_spec=pltpu.PrefetchScalarGridSpec(
            num_scalar_prefetch=5,
            grid=(plan["combine"][0].shape[0],),
            in_specs=[pl.BlockSpec((ck, d), lambda w, tl, bk, fi, la, va: (tl[w], 0)),
                      pl.BlockSpec((ck, 2 * LANES), lambda w, tl, bk, fi, la, va: (tl[w], 0)),
                      pl.BlockSpec((rb, d), lambda w, tl, bk, fi, la, va: (bk[w], 0)),
                      pl.BlockSpec((1, d), lambda w, tl, bk, fi, la, va: (0, 0))],
            out_specs=pl.BlockSpec((ck, d), lambda w, tl, bk, fi, la, va: (tl[w], 0)),
            scratch_shapes=[pltpu.VMEM((ck, d), F32)]),
        out_shape=jax.ShapeDtypeStruct((t, d), F32),
        compiler_params=_cparams("arbitrary"),
        name="moe_combine",
    )(*plan["combine"], x, pos_lanes, ys, gain)


def _rmsnorm_body(x_ref, g_ref, o_ref):
    o_ref[...] = _rms(x_ref[...], g_ref[...])


def rmsnorm(x, g, tm=1024):
    t, d = x.shape
    tm = min(tm, t)
    return pl.pallas_call(
        _rmsnorm_body,
        grid=(t // tm,),
        in_specs=[pl.BlockSpec((tm, d), lambda i: (i, 0)), pl.BlockSpec((1, d), lambda i: (0, 0))],
        out_specs=pl.BlockSpec((tm, d), lambda i: (i, 0)),
        out_shape=jax.ShapeDtypeStruct((t, d), F32),
        compiler_params=_cparams("parallel"),
        name="rmsnorm",
    )(x, g.reshape(1, d))


def _in_proj_slices(w):
    sizes = (MOBA_WIDTH, MOBA_WIDTH, MOBA_WIDTH, SSM_WIDTH, GDN_WIDTH, GDN_WIDTH, GDN_WIDTH,
             GDN_HEADS, GDN_HEADS, GDN_WIDTH, w.shape[0], w.shape[0], w.shape[0])
    parts, start = [], 0
    for size in sizes:
        parts.append(w[:, start:start + size])
        start += size
    return parts


def kernel(x, mem, positions, norm_mix, w_in, ssm_a_re, ssm_a_im, ssm_log_dt, ssm_b_re, ssm_b_im, ssm_c_re, ssm_c_im, ssm_d, ssm_w_glu, gdn_conv, gdn_a_log, gdn_dt_bias, gdn_norm, w_up_moba, w_up_gdn, w_out, norm_xa, norm_mem, xa_wq, xa_wk, xa_wv, xa_wo, norm_ffn, ffn_w_gate, ffn_w_up, ffn_w_down, moe_w_router, moe_w_gate, moe_w_up, moe_w_down, norm_final):
    batch, seq, d = x.shape
    depth = w_in.shape[0]
    t = batch * seq
    assert batch == SUBLANES, "the S5 scan packs the batch onto the 8 sublanes of a vreg"
    assert seq % MOBA_BLOCK == 0 and seq % GDN_CHUNK == 0
    nc = seq // GDN_CHUNK
    bf = lambda a: a.astype(BF16)

    xf = x.reshape(t, d)
    memf = mem.reshape(-1, d)
    cos, sin = rope_tables(positions)

    for l in range(depth):
        (wq_m, wk_m, wv_m, wu_s, wq_g, wk_g, wv_g, wa_g, wb_g, wz_g, wg_a, wg_b, wg_c) = _in_proj_slices(w_in[l])
        g_mix = norm_mix[l]
        w_ab = jnp.pad(jnp.concatenate([wa_g, wb_g], axis=1), ((0, 0), (0, LANES - 2 * GDN_HEADS)))
        w32 = jnp.concatenate([wq_m, wk_m, wu_s, wq_g, wk_g, wv_g, w_ab], axis=1)
        w16 = jnp.concatenate([wv_m, wz_g, wg_a, wg_b, wg_c], axis=1)
        p32, p16 = norm_matmul2(xf, g_mix, bf(w32), bf(w16), F32, BF16, tm=2048, tna=640, tnb=1024)
        c_us = 2 * MOBA_WIDTH
        c_qkv = c_us + SSM_WIDTH
        c_ab = c_qkv + 3 * GDN_WIDTH
        u_s = p32[:, c_us:c_us + SSM_WIDTH]
        ab_g = p32[:, c_ab:c_ab + 2 * GDN_HEADS]

        y_a = moba_attention(p32, p16, cos, sin, batch, seq, q_col=0, k_col=MOBA_WIDTH // LANES, v_col=0)

        l_re, l_im, bb_re, bb_im = s5_discretise(ssm_a_re[l], ssm_a_im[l], ssm_log_dt[l], ssm_b_re[l], ssm_b_im[l])
        maps = s5_block_maps(bb_re, bb_im, ssm_c_re[l], ssm_c_im[l])
        u_tm = u_s.reshape(batch, seq, SSM_WIDTH).transpose(1, 0, 2).reshape(t, SSM_WIDTH)
        y_s_tm = s5_gelu(u_tm, maps, l_re, l_im, ssm_d[l])
        y_s = y_s_tm.reshape(seq, batch, SSM_WIDTH).transpose(1, 0, 2).reshape(t, SSM_WIDTH)

        def head_rows(cols):
            return cols.reshape(batch, nc, GDN_CHUNK, GDN_HEADS).transpose(0, 3, 1, 2)
        a_rows = head_rows(ab_g[:, :GDN_HEADS])
        b_rows = head_rows(ab_g[:, GDN_HEADS:2 * GDN_HEADS])
        y_c = gdn_mixer(p32, p16, a_rows, b_rows, gdn_conv[l], gdn_a_log[l], gdn_dt_bias[l], gdn_norm[l],
                        batch, seq, qkv_col=c_qkv // LANES, z_col=MOBA_WIDTH // GDN_WIDTH)

        k_x = norm_matmul(memf, norm_mem[l], bf(xa_wk[l]), BF16)
        v_x = norm_matmul(memf, norm_mem[l], bf(xa_wv[l]), BF16)
        xf = merge_cross_attention(xf, y_a, y_s, y_c, p16, bf(w_up_moba[l]), bf(ssm_w_glu[l]), bf(w_up_gdn[l]),
                                   bf(w_out[l]), norm_xa[l], bf(xa_wq[l]), k_x, v_x, bf(xa_wo[l]), batch, seq,
                                   gate_col=(MOBA_WIDTH + GDN_WIDTH) // d)

        if l % 2 == 0:
            i = l // 2
            xf = ffn(xf, norm_ffn[l], bf(ffn_w_gate[i]), bf(ffn_w_up[i]), bf(ffn_w_down[i]))
        else:
            i = l // 2
            route, h_moe = moe_router(xf, norm_ffn[l], moe_w_router[i])
            xf = moe_ffn(xf, h_moe, route, bf(moe_w_gate[i]), bf(moe_w_up[i]), bf(moe_w_down[i]),
                         final_gain=norm_final if l == depth - 1 else None)

    if depth % 2 == 1:
        xf = rmsnorm(xf, norm_final)
    return xf.reshape(batch, seq, d)
```

```python
import functools

import numpy as np
import jax
import jax.numpy as jnp
from jax import lax
from jax.experimental import pallas as pl
from jax.experimental.pallas import tpu as pltpu

F32 = jnp.float32
BF16 = jnp.bfloat16
HIGHEST = lax.Precision.HIGHEST

EPS = 1e-6
NEG_INF = -1e30
MOBA_HEADS = 8
MOBA_HEAD_DIM = 64
MOBA_WIDTH = MOBA_HEADS * MOBA_HEAD_DIM
MOBA_BLOCK = 256
MOBA_TOPK = 3
ROPE_THETA = 10000.0
SSM_WIDTH = 512
SSM_GROUP = 16
SSM_GROUPS = SSM_WIDTH // SSM_GROUP
SSM_STATE = 64
SSM_STATES = SSM_GROUPS * SSM_STATE
GDN_HEADS = 4
GDN_HEAD_DIM = 128
GDN_WIDTH = GDN_HEADS * GDN_HEAD_DIM
GDN_CONV = 4
GDN_CHUNK = 64
XA_HEADS = 4
XA_HEAD_DIM = 128
XA_WIDTH = XA_HEADS * XA_HEAD_DIM
N_EXPERTS = 8
TOP_K = 2

LANES = 128
SUBLANES = 8
VMEM_LIMIT = 56 * 1024 * 1024


def _cparams(*sem):
    return pltpu.CompilerParams(dimension_semantics=sem, vmem_limit_bytes=VMEM_LIMIT)


def _rms(x, g):
    return x * lax.rsqrt(jnp.mean(x * x, axis=-1, keepdims=True) + EPS) * g


def _sigmoid(x):
    return 1.0 / (1.0 + jnp.exp(-x))


def _silu(x):
    return x * _sigmoid(x)


def _dot(a, b):
    return jnp.dot(a, b, preferred_element_type=F32)


def _dot_nt(a, b):
    return lax.dot_general(a, b, (((1,), (1,)), ((), ())), preferred_element_type=F32)


def _norm_matmul_body(x_ref, g_ref, w_ref, o_ref, h_ref):
    @pl.when(pl.program_id(1) == 0)
    def _():
        h_ref[...] = _rms(x_ref[...], g_ref[...]).astype(BF16)

    o_ref[...] = _dot(h_ref[...], w_ref[...]).astype(o_ref.dtype)


def norm_matmul(x, g, w, out_dtype, tm=512, tn=512):
    t, k = x.shape
    n = w.shape[1]
    tm, tn = min(tm, t), min(tn, n)
    return pl.pallas_call(
        _norm_matmul_body,
        grid=(t // tm, n // tn),
        in_specs=[pl.BlockSpec((tm, k), lambda i, j: (i, 0)),
                  pl.BlockSpec((1, k), lambda i, j: (0, 0)),
                  pl.BlockSpec((k, tn), lambda i, j: (0, j))],
        out_specs=pl.BlockSpec((tm, tn), lambda i, j: (i, j)),
        out_shape=jax.ShapeDtypeStruct((t, n), out_dtype),
        scratch_shapes=[pltpu.VMEM((tm, k), BF16)],
        compiler_params=_cparams("parallel", "arbitrary"),
        name="norm_matmul",
    )(x, g.reshape(1, k), w)


def _norm_matmul2_body(x_ref, g_ref, wa_ref, wb_ref, oa_ref, ob_ref, h_ref, *, na):
    j = pl.program_id(1)

    @pl.when(j == 0)
    def _():
        h_ref[...] = _rms(x_ref[...], g_ref[...]).astype(BF16)

    @pl.when(j < na)
    def _():
        oa_ref[...] = _dot(h_ref[...], wa_ref[...]).astype(oa_ref.dtype)

    @pl.when(j >= na)
    def _():
        ob_ref[...] = _dot(h_ref[...], wb_ref[...]).astype(ob_ref.dtype)


def norm_matmul2(x, g, wa, wb, dtype_a, dtype_b, tm, tna, tnb):
    t, k = x.shape
    tm = min(tm, t)
    na, nb = wa.shape[1] // tna, wb.shape[1] // tnb
    return pl.pallas_call(
        functools.partial(_norm_matmul2_body, na=na),
        grid=(t // tm, na + nb),
        in_specs=[pl.BlockSpec((tm, k), lambda i, j: (i, 0)),
                  pl.BlockSpec((1, k), lambda i, j: (0, 0)),
                  pl.BlockSpec((k, tna), lambda i, j: (0, jnp.minimum(j, na - 1))),
                  pl.BlockSpec((k, tnb), lambda i, j: (0, jnp.maximum(j - na, 0)))],
        out_specs=[pl.BlockSpec((tm, tna), lambda i, j: (i, jnp.minimum(j, na - 1))),
                   pl.BlockSpec((tm, tnb), lambda i, j: (i, jnp.maximum(j - na, 0)))],
        out_shape=[jax.ShapeDtypeStruct((t, wa.shape[1]), dtype_a), jax.ShapeDtypeStruct((t, wb.shape[1]), dtype_b)],
        scratch_shapes=[pltpu.VMEM((tm, k), BF16)],
        compiler_params=_cparams("parallel", "arbitrary"),
        name="norm_matmul2",
    )(x, g.reshape(1, k), wa, wb)


def _rope_body(pos_ref, inv_ref, cos_ref, sin_ref):
    ang = pos_ref[...] * inv_ref[...]
    lane = lax.broadcasted_iota(jnp.int32, ang.shape, 1)
    first_half = (lane % MOBA_HEAD_DIM) < (MOBA_HEAD_DIM // 2)
    cos_ref[...] = jnp.cos(ang)
    s = jnp.sin(ang)
    sin_ref[...] = jnp.where(first_half, -s, s)


def rope_tables(positions):
    t = positions.size
    half = MOBA_HEAD_DIM // 2
    inv = (1.0 / (np.float32(ROPE_THETA) ** (np.arange(0, MOBA_HEAD_DIM, 2, dtype=np.float32)
                                             / np.float32(MOBA_HEAD_DIM)))).astype(np.float32)
    inv_row = jnp.asarray(np.tile(inv, LANES // half).reshape(1, LANES))
    pos = positions.astype(F32).reshape(t, 1)
    ts = min(1024, t)
    return pl.pallas_call(
        _rope_body,
        grid=(t // ts,),
        in_specs=[pl.BlockSpec((ts, 1), lambda i: (i, 0)),
                  pl.BlockSpec((1, LANES), lambda i: (0, 0))],
        out_specs=[pl.BlockSpec((ts, LANES), lambda i: (i, 0))] * 2,
        out_shape=[jax.ShapeDtypeStruct((t, LANES), F32)] * 2,
        compiler_params=_cparams("parallel"),
        name="rope_tables",
    )(pos, inv_row)


def _rope_apply(x, cos, sin_signed):
    lane = lax.broadcasted_iota(jnp.int32, x.shape, 1)
    first_half = (lane % MOBA_HEAD_DIM) < (MOBA_HEAD_DIM // 2)
    partner = jnp.where(first_half, pltpu.roll(x, LANES - MOBA_HEAD_DIM // 2, 1),
                        pltpu.roll(x, MOBA_HEAD_DIM // 2, 1))
    return x * cos + partner * sin_signed


def _moba_body(q_ref, k_ref, v_ref, cq_ref, sq_ref, ck_ref, sk_ref, o_ref, ka_ref, va_ref, km_ref, s_ref,
               mt_ref, acc_ref, *, nb):
    blk = MOBA_BLOCK
    i = pl.program_id(2)

    @pl.when(i == 0)
    def _():
        km_ref[...] = jnp.zeros_like(km_ref)
        lane_b = lax.broadcasted_iota(jnp.int32, (blk, LANES), 1)

        def rope_blk(j, c):
            rows = pl.ds(pl.multiple_of(j * blk, blk), blk)
            kr = _rope_apply(k_ref[rows, :], ck_ref[rows, :], sk_ref[rows, :])
            ka_ref[rows, 0:LANES] = kr.astype(BF16)
            ka_ref[rows, LANES:2 * LANES] = jnp.where(lane_b == j, 1.0, 0.0).astype(BF16)
            km_ref[pl.ds(j, 1), :] = jnp.mean(kr, axis=0, keepdims=True)
            v = v_ref[rows, :].astype(F32)
            va_ref[0, rows, :] = jnp.where(lane_b < MOBA_HEAD_DIM, v,
                                           jnp.where(lane_b == MOBA_HEAD_DIM, 1.0, 0.0)).astype(BF16)
            va_ref[1, rows, :] = jnp.where(lane_b >= MOBA_HEAD_DIM, v,
                                           jnp.where(lane_b == 0, 1.0, 0.0)).astype(BF16)
            return c
        lax.fori_loop(0, nb, rope_blk, 0)

    tq = 2 * blk
    lane = lax.broadcasted_iota(jnp.int32, (1, LANES), 1)
    head_a = lane < MOBA_HEAD_DIM
    q = _rope_apply(q_ref[...], cq_ref[...], sq_ref[...])
    scale = MOBA_HEAD_DIM ** -0.5
    km = km_ref[...]
    nbp = km_ref.shape[0]
    brow = lax.broadcasted_iota(jnp.int32, (nbp, tq), 0)
    qblk = 2 * i + jnp.where(lax.broadcasted_iota(jnp.int32, (1, tq), 1) >= blk, 1, 0)
    heads = (0, 1)
    q_heads = [jnp.where(head_a, q, 0.0), jnp.where(head_a, 0.0, q)]
    gates = [lax.dot_general(km, qh, (((1,), (1,)), ((), ())), preferred_element_type=F32, precision=HIGHEST)
             for qh in q_heads]
    q_augs = []
    brow_f = brow.astype(F32)
    for h in heads:
        gate = jnp.where(brow < qblk, gates[h], NEG_INF)
        picked = jnp.zeros((nbp, tq), F32)
        for _ in range(MOBA_TOPK):
            top = jnp.max(gate, axis=0, keepdims=True)
            first = jnp.min(jnp.where(gate == top, brow_f, float(nbp)), axis=0, keepdims=True)
            hit = brow_f == first
            picked = jnp.where(hit, 1.0, picked)
            gate = jnp.where(hit, -jnp.inf, gate)
        keep = ((picked > 0.5) & (brow < qblk)) | (brow == qblk)
        bias_t = jnp.concatenate([jnp.where(keep, 0.0, NEG_INF), jnp.zeros((LANES - nbp, tq), F32)],
                                 axis=0)
        q_augs.append(jnp.concatenate([q_heads[h] * scale, bias_t.T], axis=1).astype(BF16))

    def pair_rows(jp):
        return pl.ds(pl.multiple_of(jp * tq, tq), tq)

    r_idx = lax.broadcasted_iota(jnp.int32, (tq, tq), 0)
    c_idx = lax.broadcasted_iota(jnp.int32, (tq, tq), 1)
    k_diag = ka_ref[pair_rows(i), :]
    s_diag = [_dot_nt(q_augs[h], k_diag) for h in heads]
    for h in heads:
        s_d = jnp.where(c_idx <= r_idx, s_diag[h], NEG_INF)
        s_ref[h, i] = s_d
        m_t = s_d[:, 0:LANES]
        for col in range(LANES, tq, LANES):
            m_t = jnp.maximum(m_t, s_d[:, col:col + LANES])
        mt_ref[h] = m_t

    def loop2(n, body):
        def two(k, c):
            body([2 * k, 2 * k + 1])
            return c
        lax.fori_loop(0, lax.shift_right_logical(n, 1), two, 0)

        @pl.when((n & 1) == 1)
        def _():
            body([n - 1])

    def scores(jps):
        k2s = [ka_ref[pair_rows(jp), :] for jp in jps]
        s2s = [[_dot_nt(q_augs[h], k2) for h in heads] for k2 in k2s]
        for jp, s2h in zip(jps, s2s):
            for h in heads:
                s_ref[h, jp] = s2h[h]
        for h in heads:
            m_t = mt_ref[h]
            for s2h in s2s:
                for col in range(0, tq, LANES):
                    m_t = jnp.maximum(m_t, s2h[h][:, col:col + LANES])
            mt_ref[h] = m_t
    loop2(i, scores)

    ms = [jnp.max(mt_ref[h], axis=-1, keepdims=True) for h in heads]
    for h in heads:
        acc_ref[h] = jnp.zeros((tq, LANES), F32)

    def values(jps):
        ps = [[jnp.exp(s_ref[h, jp] - ms[h]).astype(BF16) for h in heads] for jp in jps]
        pvs = [[_dot(p[h], va_ref[h, pair_rows(jp), :]) for h in heads] for jp, p in zip(jps, ps)]
        for h in heads:
            tot = pvs[0][h]
            for pv in pvs[1:]:
                tot = tot + pv[h]
            acc_ref[h] += tot
    loop2(i + 1, values)
    acc_a, acc_b = acc_ref[0], acc_ref[1]
    out_a = acc_a / acc_a[:, MOBA_HEAD_DIM:MOBA_HEAD_DIM + 1]
    out_b = acc_b / acc_b[:, 0:1]
    o_ref[...] = jnp.where(head_a, out_a, out_b).astype(o_ref.dtype)


def moba_attention(qk, v, cos, sin, batch, seq, q_col=0, k_col=MOBA_WIDTH // LANES, v_col=0):
    nb = seq // MOBA_BLOCK
    assert nb % 2 == 0
    npair = nb // 2
    pairs = MOBA_WIDTH // LANES
    tq = 2 * MOBA_BLOCK
    t = batch * seq
    return pl.pallas_call(
        functools.partial(_moba_body, nb=nb),
        grid=(batch, pairs, npair),
        in_specs=[pl.BlockSpec((tq, LANES), lambda b, p, i: (b * npair + i, q_col + p)),
                  pl.BlockSpec((seq, LANES), lambda b, p, i: (b, k_col + p)),
                  pl.BlockSpec((seq, LANES), lambda b, p, i: (b, v_col + p)),
                  pl.BlockSpec((tq, LANES), lambda b, p, i: (b * npair + i, 0)),
                  pl.BlockSpec((tq, LANES), lambda b, p, i: (b * npair + i, 0)),
                  pl.BlockSpec((seq, LANES), lambda b, p, i: (b, 0)),
                  pl.BlockSpec((seq, LANES), lambda b, p, i: (b, 0))],
        out_specs=pl.BlockSpec((tq, LANES), lambda b, p, i: (b * npair + i, p)),
        out_shape=jax.ShapeDtypeStruct((t, MOBA_WIDTH), BF16),
        scratch_shapes=[pltpu.VMEM((seq, 2 * LANES), BF16),
                        pltpu.VMEM((2, seq, LANES), BF16),
                        pltpu.VMEM((-(-nb // SUBLANES) * SUBLANES, LANES), F32),
                        pltpu.VMEM((2, npair, tq, tq), F32),
                        pltpu.VMEM((2, tq, LANES), F32),
                        pltpu.VMEM((2, tq, LANES), F32)],
        compiler_params=_cparams("parallel", "parallel", "arbitrary"),
        name="moba_attention",
    )(qk, qk, v, cos, sin, cos, sin)


def _s5_disc_body(are_ref, aim_ref, ldt_ref, bre_ref, bim_ref, lre_ref, lim_ref, bbre_ref, bbim_ref):
    a_re, a_im = are_ref[...], aim_ref[...]
    dt = jnp.exp(ldt_ref[...])
    mag = jnp.exp(a_re * dt)
    l_re = mag * jnp.cos(a_im * dt)
    l_im = mag * jnp.sin(a_im * dt)
    lre_ref[...] = l_re
    lim_ref[...] = l_im
    x, y = l_re - 1.0, l_im
    den = a_re * a_re + a_im * a_im
    c_re = (x * a_re + y * a_im) / den
    c_im = (y * a_re - x * a_im) / den
    b_re, b_im = bre_ref[...], bim_ref[...]
    bbre_ref[...] = c_re * b_re - c_im * b_im
    bbim_ref[...] = c_re * b_im + c_im * b_re


def s5_discretise(a_re, a_im, log_dt, b_re, b_im):
    g, p = a_re.shape
    n = b_re.shape[-1]
    col = lambda a: a.reshape(g * p, 1)
    ldt = jnp.broadcast_to(log_dt[:, None], (g, p))
    outs = pl.pallas_call(
        _s5_disc_body,
        out_shape=[jax.ShapeDtypeStruct((g * p, 1), F32)] * 2 + [jax.ShapeDtypeStruct((g * p, n), F32)] * 2,
        name="s5_discretise",
    )(col(a_re), col(a_im), col(ldt), b_re.reshape(g * p, n), b_im.reshape(g * p, n))
    l_re, l_im, bb_re, bb_im = outs
    return l_re.reshape(g, p), l_im.reshape(g, p), bb_re.reshape(g, p, n), bb_im.reshape(g, p, n)


def _block_diag(blocks):
    g, r, c = blocks.shape
    eye = jnp.eye(g, dtype=blocks.dtype)
    return (blocks[:, :, None, :] * eye[:, None, :, None]).reshape(g * r, g * c)


def _s5_body(u_ref, bre_ref, bim_ref, cre_ref, cim_ref, lre_ref, lim_ref, d_ref, y_ref, h_ref, hb_ref, *, steps):
    rows = SUBLANES
    ns = SSM_STATES
    nq = SSM_WIDTH // LANES
    sq = ns // nq

    @pl.when(pl.program_id(0) == 0)
    def _():
        h_ref[...] = jnp.zeros_like(h_ref)

    u = u_ref[...]
    u16 = u.astype(BF16)
    for q in range(nq):
        uq = u16[:, q * LANES:(q + 1) * LANES]
        hb_ref[:, q * sq:(q + 1) * sq] = _dot(uq, bre_ref[q])
        hb_ref[:, ns + q * sq:ns + (q + 1) * sq] = _dot(uq, bim_ref[q])
    l_re, l_im = lre_ref[...], lim_ref[...]

    def step(t, carry):
        h_re, h_im = carry
        r = pl.ds(pl.multiple_of(t * rows, rows), rows)
        n_re = l_re * h_re - l_im * h_im + hb_ref[r, 0:ns]
        n_im = l_re * h_im + l_im * h_re + hb_ref[r, ns:2 * ns]
        hb_ref[r, 0:ns] = n_re
        hb_ref[r, ns:2 * ns] = n_im
        return n_re, n_im

    h_re, h_im = lax.fori_loop(0, steps, step, (h_ref[:, 0:ns], h_ref[:, ns:2 * ns]), unroll=4)
    h_ref[:, 0:ns] = h_re
    h_ref[:, ns:2 * ns] = h_im
    ys = []
    for q in range(nq):
        ys.append(_dot(hb_ref[:, q * sq:(q + 1) * sq].astype(BF16), cre_ref[q])
                  + _dot(hb_ref[:, ns + q * sq:ns + (q + 1) * sq].astype(BF16), cim_ref[q]))
    y = jnp.concatenate(ys, axis=1) + d_ref[...] * u
    y_ref[...] = jax.nn.gelu(y).astype(y_ref.dtype)


def s5_block_maps(bb_re, bb_im, c_re, c_im):
    nq = SSM_WIDTH // LANES
    gq = SSM_GROUPS // nq
    per_q = lambda a: jnp.stack([_block_diag(a[q * gq:(q + 1) * gq].transpose(0, 2, 1)) for q in range(nq)])
    return (per_q(bb_re).astype(BF16), per_q(bb_im).astype(BF16),
            per_q(c_re).astype(BF16), per_q(-c_im).astype(BF16))


def s5_gelu(u_tm, maps, l_re, l_im, d_skip, steps=64):
    rows = u_tm.shape[0]
    blk = steps * SUBLANES
    ns2 = 2 * SSM_STATES
    full3 = lambda a: pl.BlockSpec(a.shape, lambda c: (0, 0, 0))
    return pl.pallas_call(
        functools.partial(_s5_body, steps=steps),
        grid=(rows // blk,),
        in_specs=[pl.BlockSpec((blk, SSM_WIDTH), lambda c: (c, 0)),
                  full3(maps[0]), full3(maps[1]), full3(maps[2]), full3(maps[3]),
                  pl.BlockSpec((1, SSM_STATES), lambda c: (0, 0)),
                  pl.BlockSpec((1, SSM_STATES), lambda c: (0, 0)),
                  pl.BlockSpec((1, SSM_WIDTH), lambda c: (0, 0))],
        out_specs=pl.BlockSpec((blk, SSM_WIDTH), lambda c: (c, 0)),
        out_shape=jax.ShapeDtypeStruct((rows, SSM_WIDTH), BF16),
        scratch_shapes=[pltpu.VMEM((SUBLANES, ns2), F32),
                        pltpu.VMEM((blk, ns2), F32)],
        compiler_params=_cparams("arbitrary"),
        name="s5_scan",
    )(u_tm, *maps, l_re.reshape(1, -1), l_im.reshape(1, -1), d_skip.reshape(1, -1))


GDN_GROUP = 16


def _gdn_prep_body(q_ref, k_ref, v_ref, a_ref, b_ref, wq_ref, wk_ref, wv_ref, alog_ref, dtb_ref,
                   u_ref, w_ref, qd_ref, kd_ref, qk_ref, gl_ref, qs_ref, ks_ref, vs_ref,
                   *, seq):
    L = GDN_CHUNK
    G = min(GDN_GROUP, seq // L)
    dk = GDN_HEAD_DIM
    nc = seq // L
    hd = pl.program_id(1)
    cb = 256
    halo = SUBLANES

    def conv_blk(rb, c):
        base = pl.multiple_of(rb * cb, cb)
        prev = pl.multiple_of(jnp.maximum(base - halo, 0), halo)
        for src, wref, dst, norm, mul in ((q_ref, wq_ref, qs_ref, True, dk ** -0.5),
                                          (k_ref, wk_ref, ks_ref, True, 1.0),
                                          (v_ref, wv_ref, vs_ref, False, 1.0)):
            cur = src[pl.ds(base, cb), :]
            head = jnp.where(rb > 0, src[pl.ds(prev, halo), :], 0.0)
            ext = jnp.concatenate([head, cur], axis=0)
            wts = wref[...]
            y = ext[halo:halo + cb] * wts[GDN_CONV - 1:GDN_CONV]
            for tap in range(1, GDN_CONV):
                y = y + ext[halo - tap:halo - tap + cb] * wts[GDN_CONV - 1 - tap:GDN_CONV - tap]
            y = _silu(y)
            if norm:
                y = y * lax.rsqrt(jnp.sum(y * y, axis=-1, keepdims=True) + EPS) * mul
            dst[pl.ds(base, cb), :] = y
        return c
    lax.fori_loop(0, seq // cb, conv_blk, 0)

    gl_rows = G * L
    ri = lax.broadcasted_iota(jnp.int32, (gl_rows, L), 0) % L
    ci = lax.broadcasted_iota(jnp.int32, (gl_rows, L), 1)
    tril = ci <= ri
    strict = ci < ri
    eye = ci == ri
    ri3 = lax.broadcasted_iota(jnp.int32, (G, L, L), 1)
    ci3 = lax.broadcasted_iota(jnp.int32, (G, L, L), 2)
    strict3 = ci3 < ri3
    eye_f = jnp.where(ci3 == ri3, 1.0, 0.0)
    r2 = lax.broadcasted_iota(jnp.int32, (L, L), 0)
    c2 = lax.broadcasted_iota(jnp.int32, (L, L), 1)
    upper_f = jnp.where(r2 <= c2, 1.0, 0.0)
    a_coef = -jnp.exp(jnp.full((1, L), alog_ref[hd], F32))
    dt_bias = dtb_ref[hd]
    hp = functools.partial(jnp.dot, preferred_element_type=F32, precision=HIGHEST)

    def bmm(a, b):
        return jnp.einsum('gij,gjk->gik', a, b, preferred_element_type=F32)

    def bmm_nt(a, b):
        return jnp.einsum('gid,gjd->gij', a, b, preferred_element_type=F32)

    def to_col(rows_b):
        return jnp.sum(jnp.where(eye, rows_b, 0.0), axis=1, keepdims=True)

    def prep(cg, carry):
        c0 = pl.multiple_of(cg * G, G)
        rows = pl.ds(pl.multiple_of(cg * gl_rows, gl_rows), gl_rows)
        qc, kc, vc = qs_ref[rows, :], ks_ref[rows, :], vs_ref[rows, :]
        z = a_ref[pl.ds(c0, G), :] + dt_bias
        softplus = jnp.maximum(z, 0.0) + jnp.log(1.0 + jnp.exp(-jnp.abs(z)))
        g_rows = hp(a_coef * softplus, upper_f)
        beta_rows = _sigmoid(b_ref[pl.ds(c0, G), :])
        spread = lambda r: jnp.concatenate([jnp.broadcast_to(r[c:c + 1, :], (L, L)) for c in range(G)], axis=0)
        g_rb = spread(g_rows)
        g_col = to_col(g_rb)
        beta_col = to_col(spread(beta_rows))
        g_last = g_rb[:, L - 1:L]
        decay = jnp.where(tril, jnp.exp(jnp.where(tril, g_col - g_rb, 0.0)), 0.0)
        k_beta = kc * beta_col
        v_beta = vc * beta_col
        to3 = lambda a: a.reshape(G, L, a.shape[-1])
        kb16, k16, q16 = to3(k_beta.astype(BF16)), to3(kc.astype(BF16)), to3(qc.astype(BF16))
        kk_qk = bmm_nt(jnp.concatenate([kb16, q16], axis=1), k16)
        decay3 = to3(decay)
        a_low = jnp.where(strict3, kk_qk[:, :L] * decay3, 0.0)
        qk = kk_qk[:, L:] * decay3
        t_inv = eye_f - a_low
        a16 = a_low.astype(BF16)
        pw = bmm(a16, a16)
        span = 2
        while 2 * span < L:
            pw16 = pw.astype(BF16)
            both = bmm(jnp.concatenate([t_inv.astype(BF16), pw16], axis=1), pw16)
            t_inv = t_inv + both[:, :L]
            pw = both[:, L:]
            span *= 2
        t_inv = t_inv + bmm(t_inv.astype(BF16), pw.astype(BF16))
        rhs = jnp.concatenate([v_beta.astype(BF16), (k_beta * jnp.exp(g_col)).astype(BF16)], axis=1)
        uw = bmm(t_inv.astype(BF16), to3(rhs)).reshape(gl_rows, 2 * dk)
        u_ref[rows, :] = uw[:, :dk]
        w_ref[rows, :] = uw[:, dk:].astype(BF16)
        qd_ref[rows, :] = (qc * jnp.exp(g_col)).astype(BF16)
        k_dec = to3(kc * jnp.exp(g_last - g_col))
        kd_ref[pl.ds(c0, G)] = jnp.swapaxes(k_dec, 1, 2).astype(BF16)
        qk_ref[pl.ds(c0, G)] = qk.astype(BF16)
        gl_ref[pl.ds(c0, G), :] = jnp.broadcast_to(jnp.exp(g_rows[:, L - 1:L]), (G, LANES))
        return carry
    lax.fori_loop(0, nc // G, prep, 0)


def _gdn_scan_body(u_ref, w_ref, qd_ref, kd_ref, qk_ref, gl_ref, z_ref, gn_ref, o_ref, *st_refs, chunks):
    L = GDN_CHUNK
    d = GDN_HEAD_DIM
    nrow = u_ref.shape[0]

    @pl.when(pl.program_id(1) == 0)
    def _():
        for st in st_refs:
            st[...] = jnp.zeros_like(st)

    gn = gn_ref[...]

    def chunk(c, carry):
        rows = pl.ds(pl.multiple_of(c * L, L), L)
        chains = [(r, h) for r in range(nrow) for h in range(GDN_HEADS)]
        states = [st_refs[r * GDN_HEADS + h][...] for r, h in chains]
        firsts = []
        for (r, h), s in zip(chains, states):
            cols = slice(h * d, (h + 1) * d)
            firsts.append(_dot(jnp.concatenate([w_ref[r, rows, cols], qd_ref[r, rows, cols]], axis=0),
                               s.astype(BF16)))
        vns = []
        for (r, h), both in zip(chains, firsts):
            cols = slice(h * d, (h + 1) * d)
            vns.append((u_ref[r, rows, cols] - both[:L]).astype(BF16))
        outs = [both[L:] + _dot(qk_ref[r, h, c], vn) for (r, h), both, vn in zip(chains, firsts, vns)]
        for (r, h), s, vn in zip(chains, states, vns):
            st_refs[r * GDN_HEADS + h][...] = s * gl_ref[r, h, pl.ds(c, 1), :] + _dot(kd_ref[r, h, c], vn)
        for (r, h), o in zip(chains, outs):
            cols = slice(h * d, (h + 1) * d)
            o_ref[r, rows, cols] = (_rms(o, gn) * _silu(z_ref[r, rows, cols].astype(F32))).astype(o_ref.dtype)
        return carry
    lax.fori_loop(0, chunks, chunk, 0)


def gdn_mixer(qkv, z, a_rows, b_rows, conv_w, a_log, dt_bias, gnorm, batch, seq, ts=1024, qkv_col=0, z_col=0):
    t = batch * seq
    hds = GDN_HEADS
    nc = seq // GDN_CHUNK
    L = GDN_CHUNK
    d = GDN_HEAD_DIM
    seq_spec = lambda off: pl.BlockSpec((seq, d), lambda b, h: (b, qkv_col + off + h))
    w_spec = lambda off: pl.BlockSpec((GDN_CONV, d), lambda b, h: (0, off + h))
    row_spec = pl.BlockSpec((None, None, nc, L), lambda b, h: (b, h, 0, 0))
    smem = pl.BlockSpec(memory_space=pltpu.SMEM)
    head_out = pl.BlockSpec((seq, d), lambda b, h: (b, h))
    u, w, qd, kd, qk, gl = pl.pallas_call(
        functools.partial(_gdn_prep_body, seq=seq),
        grid=(batch, hds),
        in_specs=[seq_spec(0), seq_spec(hds), seq_spec(2 * hds), row_spec, row_spec,
                  w_spec(0), w_spec(hds), w_spec(2 * hds), smem, smem],
        out_specs=[head_out, head_out, head_out,
                   pl.BlockSpec((None, None, nc, d, L), lambda b, h: (b, h, 0, 0, 0)),
                   pl.BlockSpec((None, None, nc, L, L), lambda b, h: (b, h, 0, 0, 0)),
                   pl.BlockSpec((None, None, nc, LANES), lambda b, h: (b, h, 0, 0))],
        out_shape=[jax.ShapeDtypeStruct((t, GDN_WIDTH), F32),
                   jax.ShapeDtypeStruct((t, GDN_WIDTH), BF16),
                   jax.ShapeDtypeStruct((t, GDN_WIDTH), BF16),
                   jax.ShapeDtypeStruct((batch, hds, nc, d, L), BF16),
                   jax.ShapeDtypeStruct((batch, hds, nc, L, L), BF16),
                   jax.ShapeDtypeStruct((batch, hds, nc, LANES), F32)],
        scratch_shapes=[pltpu.VMEM((seq, d), F32), pltpu.VMEM((seq, d), F32), pltpu.VMEM((seq, d), F32)],
        compiler_params=_cparams("parallel", "parallel"),
        name="gdn_prep",
    )(qkv, qkv, qkv, a_rows, b_rows, conv_w, conv_w, conv_w, a_log, dt_bias)

    ts = min(ts, seq)
    spb = seq // ts
    cps = ts // L
    nrow = 2 if batch % 2 == 0 else 1
    by_row = lambda a: a.reshape(batch, seq, a.shape[-1])
    tok = pl.BlockSpec((nrow, ts, GDN_WIDTH), lambda b, s: (b, s, 0))
    out = pl.pallas_call(
        functools.partial(_gdn_scan_body, chunks=cps),
        grid=(batch // nrow, spb),
        in_specs=[tok, tok, tok,
                  pl.BlockSpec((nrow, hds, cps, d, L), lambda b, s: (b, 0, s, 0, 0)),
                  pl.BlockSpec((nrow, hds, cps, L, L), lambda b, s: (b, 0, s, 0, 0)),
                  pl.BlockSpec((nrow, hds, cps, LANES), lambda b, s: (b, 0, s, 0)),
                  pl.BlockSpec((nrow, ts, GDN_WIDTH), lambda b, s: (b, s, z_col)),
                  pl.BlockSpec((1, d), lambda b, s: (0, 0))],
        out_specs=tok,
        out_shape=jax.ShapeDtypeStruct((batch, seq, GDN_WIDTH), BF16),
        scratch_shapes=[pltpu.VMEM((d, d), F32)] * (nrow * hds),
        compiler_params=_cparams("parallel", "arbitrary"),
        name="gdn_scan",
    )(by_row(u), by_row(w), by_row(qd), kd, qk, gl, by_row(z), gnorm.reshape(1, d))
    return out.reshape(t, GDN_WIDTH)


def _route_record(h, w_hi, w_lo):
    h_hi = h.astype(BF16)
    h_lo = (h - h_hi.astype(F32)).astype(BF16)
    logits = _dot(h_hi, w_hi) + (_dot(h_hi, w_lo) + _dot(h_lo, w_hi))
    lane = lax.broadcasted_iota(jnp.int32, logits.shape, 1).astype(F32)
    logits = jnp.where(lane < N_EXPERTS, logits, -jnp.inf)
    v1 = jnp.max(logits, axis=-1, keepdims=True)
    i1 = jnp.min(jnp.where(logits == v1, lane, float(LANES)), axis=-1, keepdims=True)
    rest = jnp.where(lane == i1, -jnp.inf, logits)
    v2 = jnp.max(rest, axis=-1, keepdims=True)
    i2 = jnp.min(jnp.where(rest == v2, lane, float(LANES)), axis=-1, keepdims=True)
    e2 = jnp.exp(v2 - v1)
    w1 = 1.0 / (1.0 + e2)
    w2 = e2 / (1.0 + e2)
    return (jnp.where(lane == 0.0, i1, 0.0) + jnp.where(lane == 1.0, i2, 0.0)
            + jnp.where(lane == 2.0, w1, 0.0) + jnp.where(lane == 3.0, w2, 0.0))


def _merge_xattn_body(x_ref, ya_ref, ys_ref, yc_ref, ga_ref, gb_ref, gc_ref, wa_ref, wg_ref, wc_ref, wo_ref,
                      gx_ref, wq_ref, k_ref, v_ref, wxo_ref, *rest, route):
    o_ref = rest[3] if route else rest[0]
    d = x_ref.shape[1]
    y_a = _dot(ya_ref[...], wa_ref[...])
    glu = _dot(ys_ref[...], wg_ref[...])
    y_b = glu[:, :d] * _sigmoid(glu[:, d:])
    y_c = _dot(yc_ref[...], wc_ref[...])
    gate = lambda r: _sigmoid(r[...].astype(F32))
    merged = gate(ga_ref) * y_a + gate(gb_ref) * y_b + gate(gc_ref) * y_c
    x1 = x_ref[...] + _dot(merged.astype(BF16), wo_ref[...])

    h = _rms(x1, gx_ref[...]).astype(BF16)
    q = _dot(h, wq_ref[...]).astype(BF16)
    cols = [slice(hd * XA_HEAD_DIM, (hd + 1) * XA_HEAD_DIM) for hd in range(XA_HEADS)]
    scores = [_dot_nt(q[:, c], k_ref[:, c]) * (XA_HEAD_DIM ** -0.5) for c in cols]
    probs = []
    for s in scores:
        p = jnp.exp(s - jnp.max(s, axis=-1, keepdims=True))
        probs.append((p / jnp.sum(p, axis=-1, keepdims=True)).astype(BF16))
    outs = [_dot(p, v_ref[:, c]) for p, c in zip(probs, cols)]
    o = jnp.concatenate(outs, axis=1).astype(BF16)
    x2 = x1 + _dot(o, wxo_ref[...])
    o_ref[...] = x2
    if route:
        gr_ref, whi_ref, wlo_ref, _, r_ref, hm_ref = rest
        h_moe = _rms(x2, gr_ref[...])
        hm_ref[...] = h_moe.astype(BF16)
        r_ref[...] = _route_record(h_moe, whi_ref[...], wlo_ref[...])


def merge_cross_attention(x, y_a, y_s, y_c, gates, w_a, w_glu, w_c, w_o, g_xa, wq, k, v, wo_xa, batch, seq,
                          tm=512, gate_col=0, router=None):
    t, d = x.shape
    m = k.shape[0] // batch
    spb = seq // tm
    row = lambda n: pl.BlockSpec((tm, n), lambda i: (i, 0))
    gate = lambda j: pl.BlockSpec((tm, d), lambda i: (i, gate_col + j))
    full = lambda a: pl.BlockSpec(a.shape, lambda i: (0, 0))
    vec = pl.BlockSpec((1, d), lambda i: (0, 0))
    mem = pl.BlockSpec((m, XA_WIDTH), lambda i: (i // spb, 0))
    in_specs = [row(d), row(y_a.shape[1]), row(y_s.shape[1]), row(y_c.shape[1]), gate(0), gate(1), gate(2),
                full(w_a), full(w_glu), full(w_c), full(w_o), vec, full(wq), mem, mem, full(wo_xa)]
    args = [x, y_a, y_s, y_c, gates, gates, gates, w_a, w_glu, w_c, w_o, g_xa.reshape(1, d), wq, k, v, wo_xa]
    out_specs, out_shape = row(d), jax.ShapeDtypeStruct((t, d), F32)
    if router is not None:
        g_ffn, w_router = router
        w_pad = jnp.pad(w_router, ((0, 0), (0, LANES - w_router.shape[1])))
        w_hi = w_pad.astype(BF16)
        w_lo = (w_pad - w_hi.astype(F32)).astype(BF16)
        in_specs += [vec, full(w_hi), full(w_lo)]
        args += [g_ffn.reshape(1, d), w_hi, w_lo]
        out_specs = [row(d), row(LANES), row(d)]
        out_shape = [out_shape, jax.ShapeDtypeStruct((t, LANES), F32), jax.ShapeDtypeStruct((t, d), BF16)]
    return pl.pallas_call(
        functools.partial(_merge_xattn_body, route=router is not None),
        grid=(t // tm,),
        in_specs=in_specs,
        out_specs=out_specs,
        out_shape=out_shape,
        compiler_params=_cparams("parallel"),
        name="merge_cross_attention",
    )(*args)


def _ffn_body(x_ref, g_ref, wg_ref, wu_ref, wd_ref, o_ref, h_ref, acc_ref):
    f = pl.program_id(1)

    @pl.when(f == 0)
    def _():
        h_ref[...] = _rms(x_ref[...], g_ref[...]).astype(BF16)
        acc_ref[...] = jnp.zeros_like(acc_ref)

    h = h_ref[...]
    act = (_silu(_dot(h, wg_ref[...])) * _dot(h, wu_ref[...])).astype(BF16)
    acc_ref[...] += _dot(act, wd_ref[...])

    @pl.when(f == pl.num_programs(1) - 1)
    def _():
        o_ref[...] = x_ref[...] + acc_ref[...]


def ffn(x, g, w_gate, w_up, w_down, tm=512, tf=1408):
    t, d = x.shape
    f = w_gate.shape[1]
    return pl.pallas_call(
        _ffn_body,
        grid=(t // tm, f // tf),
        in_specs=[pl.BlockSpec((tm, d), lambda i, k: (i, 0)),
                  pl.BlockSpec((1, d), lambda i, k: (0, 0)),
                  pl.BlockSpec((d, tf), lambda i, k: (0, k)),
                  pl.BlockSpec((d, tf), lambda i, k: (0, k)),
                  pl.BlockSpec((tf, d), lambda i, k: (k, 0))],
        out_specs=pl.BlockSpec((tm, d), lambda i, k: (i, 0)),
        out_shape=jax.ShapeDtypeStruct((t, d), F32),
        scratch_shapes=[pltpu.VMEM((tm, d), BF16), pltpu.VMEM((tm, d), F32)],
        compiler_params=_cparams("parallel", "arbitrary"),
        name="ffn",
    )(x, g.reshape(1, d), w_gate, w_up, w_down)


MOE_TILE = 512
MOE_GTILE = 256
MOE_CHUNK = 512
MOE_RBLK = 256


def _count_le(sorted_vals, queries):
    return jnp.sum((sorted_vals[None, :] <= queries[:, None]).astype(jnp.int32), axis=1)


def _moe_plan(route, t):
    e = N_EXPERTS
    tm, ck, rb = MOE_TILE, MOE_CHUNK, MOE_RBLK
    n_tiles = (TOP_K * t) // tm + e
    p = n_tiles * tm
    n_chunks = t // ck
    i32 = jnp.int32
    e1 = route[:, 0].astype(i32)
    e2 = route[:, 1].astype(i32)
    oh1 = jax.nn.one_hot(e1, e, dtype=i32)
    oh2 = jax.nn.one_hot(e2, e, dtype=i32)
    cnt = oh1 + oh2
    incl = jnp.cumsum(cnt, axis=0)
    excl = incl - cnt
    n_e = incl[-1]
    g_e = ((n_e + tm - 1) // tm) * tm
    off_end = jnp.cumsum(g_e)
    off = off_end - g_e
    pos_all = off[None, :] + excl
    pos1 = jnp.sum(pos_all * oh1, axis=1)
    pos2 = jnp.sum(pos_all * oh2, axis=1)
    tile_start = jnp.arange(n_tiles, dtype=i32) * tm
    tile_expert = jnp.minimum(_count_le(off_end, tile_start), e - 1)
    tile_sel = jax.nn.one_hot(tile_expert, e, dtype=i32)
    tile_used = (tile_start < jnp.sum(tile_sel * (off + n_e)[None, :], axis=1)).astype(i32)
    gm = MOE_GTILE
    n_gt = p // gm
    gt_start = jnp.arange(n_gt, dtype=i32) * gm
    sel = jax.nn.one_hot(jnp.minimum(_count_le(off_end, gt_start), e - 1), e, dtype=i32)
    r0 = gt_start - jnp.sum(sel * off[None, :], axis=1)
    n_sel = jnp.sum(sel * n_e[None, :], axis=1)
    has_rows = r0 < n_sel
    r1 = jnp.minimum(r0 + gm, n_sel) - 1
    ends = incl[ck - 1::ck]
    ends_sel = jnp.sum(ends[None, :, :] * sel[:, None, :], axis=2)
    c_lo = jnp.where(has_rows, jnp.sum((ends_sel <= r0[:, None]).astype(i32), axis=1), 0)
    c_hi = jnp.where(has_rows, jnp.sum((ends_sel <= r1[:, None]).astype(i32), axis=1), 0)
    n_it = c_hi - c_lo + 1
    it_end = jnp.cumsum(n_it)
    it_start = it_end - n_it
    wmax = n_gt + e * n_chunks
    w_idx = jnp.arange(wmax, dtype=i32)
    g_tile = jnp.minimum(_count_le(it_end, w_idx), n_gt - 1)
    g_valid = w_idx < it_end[-1]
    g_chunk = jnp.where(g_valid, c_lo[g_tile] + (w_idx - it_start[g_tile]), c_hi[n_gt - 1])
    g_first = (g_valid & (w_idx == it_start[g_tile])).astype(i32)
    g_last = (g_valid & (w_idx == it_end[g_tile] - 1)).astype(i32)
    gather = (g_tile, g_chunk.astype(i32), g_first, g_last, g_valid.astype(i32))
    lo = off[None, :] + excl[::ck]
    hi = off[None, :] + ends
    nblk = jnp.where(hi > lo, (hi - 1) // rb - lo // rb + 1, 0).reshape(-1)
    blk_lo = (lo // rb).reshape(-1)
    cb_end = jnp.cumsum(nblk)
    cb_start = cb_end - nblk
    cmax = p // rb + e * n_chunks
    c_idx = jnp.arange(cmax, dtype=i32)
    pair = jnp.minimum(_count_le(cb_end, c_idx), nblk.shape[0] - 1)
    c_valid = c_idx < cb_end[-1]
    c_tile = jnp.where(c_valid, pair // e, n_chunks - 1)
    c_blk = jnp.where(c_valid, blk_lo[pair] + (c_idx - cb_start[pair]), 0)
    per_tile = jnp.sum(nblk.reshape(n_chunks, e), axis=1)
    t_end = jnp.cumsum(per_tile)
    c_first = (c_valid & (c_idx == (t_end - per_tile)[c_tile])).astype(i32)
    c_last = (c_valid & (c_idx == t_end[c_tile] - 1)).astype(i32)
    combine = (c_tile.astype(i32), c_blk.astype(i32), c_first, c_last, c_valid.astype(i32))
    return dict(n_tiles=n_tiles, p=p, tile_expert=tile_expert.astype(i32), tile_used=tile_used, pos1=pos1, pos2=pos2,
                gather=gather, combine=combine)


def _moe_gather_body(tile_ref, chunk_ref, first_ref, last_ref, valid_ref, meta_ref, h_ref,
                     o_ref, wr_ref, acc_ref, wacc_ref):
    w = pl.program_id(0)

    @pl.when(first_ref[w] == 1)
    def _():
        acc_ref[...] = jnp.zeros_like(acc_ref)
        wacc_ref[...] = jnp.zeros_like(wacc_ref)

    @pl.when(valid_ref[w] == 1)
    def _():
        row = (lax.broadcasted_iota(jnp.int32, (MOE_GTILE, MOE_CHUNK), 0) + tile_ref[w] * MOE_GTILE).astype(F32)
        hit1 = row == meta_ref[0:1, :]
        hit2 = row == meta_ref[1:2, :]
        onehot = jnp.where(hit1, 1.0, jnp.where(hit2, 1.0, 0.0)).astype(BF16)
        acc_ref[...] += _dot(onehot, h_ref[...])
        wts = jnp.where(hit1, meta_ref[2:3, :], jnp.where(hit2, meta_ref[3:4, :], 0.0))
        wacc_ref[...] += jnp.sum(wts, axis=1, keepdims=True)

    @pl.when(last_ref[w] == 1)
    def _():
        o_ref[...] = acc_ref[...].astype(o_ref.dtype)
        wr_ref[...] = jnp.broadcast_to(wacc_ref[...], wr_ref.shape)


def _moe_ffn_body(exp_ref, used_ref, x_ref, wr_ref, wg_ref, wu_ref, wd_ref, o_ref, acc_ref):
    f = pl.program_id(1)

    @pl.when(f == 0)
    def _():
        acc_ref[...] = jnp.zeros_like(acc_ref)

    @pl.when(used_ref[pl.program_id(0)] == 1)
    def _():
        x = x_ref[...]
        act = (_silu(_dot(x, wg_ref[...])) * _dot(x, wu_ref[...])).astype(BF16)
        acc_ref[...] += _dot(act, wd_ref[...])

    @pl.when(f == pl.num_programs(1) - 1)
    def _():
        o_ref[...] = (acc_ref[...] * wr_ref[:, 0:1]).astype(o_ref.dtype)


def _moe_combine_body(tile_ref, blk_ref, first_ref, last_ref, valid_ref, x_ref, pos_ref, y_ref, gf_ref,
                      o_ref, acc_ref, *, final_norm):
    w = pl.program_id(0)

    @pl.when(first_ref[w] == 1)
    def _():
        acc_ref[...] = x_ref[...]

    @pl.when(valid_ref[w] == 1)
    def _():
        base = blk_ref[w] * MOE_RBLK
        col = lax.broadcasted_iota(jnp.int32, (MOE_CHUNK, MOE_RBLK), 1) + base
        reps = MOE_RBLK // LANES
        p1 = jnp.concatenate([pos_ref[:, 0:LANES]] * reps, axis=1)
        p2 = jnp.concatenate([pos_ref[:, LANES:2 * LANES]] * reps, axis=1)
        hit = jnp.where(col == p1, 1.0, jnp.where(col == p2, 1.0, 0.0))
        acc_ref[...] += _dot(hit.astype(BF16), y_ref[...])

    @pl.when(last_ref[w] == 1)
    def _():
        o_ref[...] = _rms(acc_ref[...], gf_ref[...]) if final_norm else acc_ref[...]


def moe_ffn(x, h, route, w_gate, w_up, w_down, final_gain=None, tf=1408):
    t, d = x.shape
    f = w_gate.shape[2]
    tm, ck, rb = MOE_TILE, MOE_CHUNK, MOE_RBLK
    plan = _moe_plan(route, t)
    n_tiles, p = plan["n_tiles"], plan["p"]
    assert p < 2 ** 24, "row positions are carried exactly in f32"
    meta = jnp.stack([plan["pos1"].astype(F32), plan["pos2"].astype(F32), route[:, 2], route[:, 3]]
                     + [jnp.zeros((t,), F32)] * (SUBLANES - 4))
    pos_lanes = jnp.concatenate([jnp.broadcast_to(plan["pos1"][:, None], (t, LANES)),
                                 jnp.broadcast_to(plan["pos2"][:, None], (t, LANES))], axis=1)

    xs, w_row = pl.pallas_call(
        _moe_gather_body,
        grid_spec=pltpu.PrefetchScalarGridSpec(
            num_scalar_prefetch=5,
            grid=(plan["gather"][0].shape[0],),
            in_specs=[pl.BlockSpec((SUBLANES, ck), lambda w, tl, ch, fi, la, va: (0, ch[w])),
                      pl.BlockSpec((ck, d), lambda w, tl, ch, fi, la, va: (ch[w], 0))],
            out_specs=[pl.BlockSpec((MOE_GTILE, d), lambda w, tl, ch, fi, la, va: (tl[w], 0)),
                       pl.BlockSpec((MOE_GTILE, LANES), lambda w, tl, ch, fi, la, va: (tl[w], 0))],
            scratch_shapes=[pltpu.VMEM((MOE_GTILE, d), F32), pltpu.VMEM((MOE_GTILE, 1), F32)]),
        out_shape=[jax.ShapeDtypeStruct((p, d), BF16), jax.ShapeDtypeStruct((p, LANES), F32)],
        compiler_params=_cparams("arbitrary"),
        name="moe_gather",
    )(*plan["gather"], meta, h)

    ys = pl.pallas_call(
        _moe_ffn_body,
        grid_spec=pltpu.PrefetchScalarGridSpec(
            num_scalar_prefetch=2,
            grid=(n_tiles, f // tf),
            in_specs=[pl.BlockSpec((tm, d), lambda i, k, ex, us: (i, 0)),
                      pl.BlockSpec((tm, LANES), lambda i, k, ex, us: (i, 0)),
                      pl.BlockSpec((None, d, tf), lambda i, k, ex, us: (ex[i], 0, k)),
                      pl.BlockSpec((None, d, tf), lambda i, k, ex, us: (ex[i], 0, k)),
                      pl.BlockSpec((None, tf, d), lambda i, k, ex, us: (ex[i], k, 0))],
            out_specs=pl.BlockSpec((tm, d), lambda i, k, ex, us: (i, 0)),
            scratch_shapes=[pltpu.VMEM((tm, d), F32)]),
        out_shape=jax.ShapeDtypeStruct((p, d), BF16),
        compiler_params=_cparams("parallel", "arbitrary"),
        name="moe_ffn",
    )(plan["tile_expert"], plan["tile_used"], xs, w_row, w_gate, w_up, w_down)

    final_norm = final_gain is not None
    gain = (final_gain if final_norm else jnp.ones((d,), F32)).reshape(1, d)
    return pl.pallas_call(
        functools.partial(_moe_combine_body, final_norm=final_norm),
        grid_spec=pltpu.PrefetchScalarGridSpec(
            num_scalar_prefetch=5,
            grid=(plan["combine"][0].shape[0],),
            in_specs=[pl.BlockSpec((ck, d), lambda w, tl, bk, fi, la, va: (tl[w], 0)),
                      pl.BlockSpec((ck, 2 * LANES), lambda w, tl, bk, fi, la, va: (tl[w], 0)),
                      pl.BlockSpec((rb, d), lambda w, tl, bk, fi, la, va: (bk[w], 0)),
                      pl.BlockSpec((1, d), lambda w, tl, bk, fi, la, va: (0, 0))],
            out_specs=pl.BlockSpec((ck, d), lambda w, tl, bk, fi, la, va: (tl[w], 0)),
            scratch_shapes=[pltpu.VMEM((ck, d), F32)]),
        out_shape=jax.ShapeDtypeStruct((t, d), F32),
        compiler_params=_cparams("arbitrary"),
        name="moe_combine",
    )(*plan["combine"], x, pos_lanes, ys, gain)


def _rmsnorm_body(x_ref, g_ref, o_ref):
    o_ref[...] = _rms(x_ref[...], g_ref[...])


def rmsnorm(x, g, tm=1024):
    t, d = x.shape
    tm = min(tm, t)
    return pl.pallas_call(
        _rmsnorm_body,
        grid=(t // tm,),
        in_specs=[pl.BlockSpec((tm, d), lambda i: (i, 0)), pl.BlockSpec((1, d), lambda i: (0, 0))],
        out_specs=pl.BlockSpec((tm, d), lambda i: (i, 0)),
        out_shape=jax.ShapeDtypeStruct((t, d), F32),
        compiler_params=_cparams("parallel"),
        name="rmsnorm",
    )(x, g.reshape(1, d))


def _in_proj_slices(w):
    sizes = (MOBA_WIDTH, MOBA_WIDTH, MOBA_WIDTH, SSM_WIDTH, GDN_WIDTH, GDN_WIDTH, GDN_WIDTH,
             GDN_HEADS, GDN_HEADS, GDN_WIDTH, w.shape[0], w.shape[0], w.shape[0])
    parts, start = [], 0
    for size in sizes:
        parts.append(w[:, start:start + size])
        start += size
    return parts


def kernel(x, mem, positions, norm_mix, w_in, ssm_a_re, ssm_a_im, ssm_log_dt, ssm_b_re, ssm_b_im, ssm_c_re, ssm_c_im, ssm_d, ssm_w_glu, gdn_conv, gdn_a_log, gdn_dt_bias, gdn_norm, w_up_moba, w_up_gdn, w_out, norm_xa, norm_mem, xa_wq, xa_wk, xa_wv, xa_wo, norm_ffn, ffn_w_gate, ffn_w_up, ffn_w_down, moe_w_router, moe_w_gate, moe_w_up, moe_w_down, norm_final):
    batch, seq, d = x.shape
    depth = w_in.shape[0]
    t = batch * seq
    assert batch == SUBLANES, "the S5 scan packs the batch onto the 8 sublanes of a vreg"
    assert seq % MOBA_BLOCK == 0 and seq % GDN_CHUNK == 0
    nc = seq // GDN_CHUNK
    bf = lambda a: a.astype(BF16)

    xf = x.reshape(t, d)
    memf = mem.reshape(-1, d)
    cos, sin = rope_tables(positions)

    for l in range(depth):
        (wq_m, wk_m, wv_m, wu_s, wq_g, wk_g, wv_g, wa_g, wb_g, wz_g, wg_a, wg_b, wg_c) = _in_proj_slices(w_in[l])
        g_mix = norm_mix[l]
        w_ab = jnp.pad(jnp.concatenate([wa_g, wb_g], axis=1), ((0, 0), (0, LANES - 2 * GDN_HEADS)))
        w32 = jnp.concatenate([wq_m, wk_m, wu_s, wq_g, wk_g, wv_g, w_ab], axis=1)
        w16 = jnp.concatenate([wv_m, wz_g, wg_a, wg_b, wg_c], axis=1)
        p32, p16 = norm_matmul2(xf, g_mix, bf(w32), bf(w16), F32, BF16, tm=2048, tna=640, tnb=1024)
        c_us = 2 * MOBA_WIDTH
        c_qkv = c_us + SSM_WIDTH
        c_ab = c_qkv + 3 * GDN_WIDTH
        u_s = p32[:, c_us:c_us + SSM_WIDTH]
        ab_g = p32[:, c_ab:c_ab + 2 * GDN_HEADS]

        y_a = moba_attention(p32, p16, cos, sin, batch, seq, q_col=0, k_col=MOBA_WIDTH // LANES, v_col=0)

        l_re, l_im, bb_re, bb_im = s5_discretise(ssm_a_re[l], ssm_a_im[l], ssm_log_dt[l], ssm_b_re[l], ssm_b_im[l])
        maps = s5_block_maps(bb_re, bb_im, ssm_c_re[l], ssm_c_im[l])
        u_tm = u_s.reshape(batch, seq, SSM_WIDTH).transpose(1, 0, 2).reshape(t, SSM_WIDTH)
        y_s_tm = s5_gelu(u_tm, maps, l_re, l_im, ssm_d[l])
        y_s = y_s_tm.reshape(seq, batch, SSM_WIDTH).transpose(1, 0, 2).reshape(t, SSM_WIDTH)

        def head_rows(cols):
            return cols.reshape(batch, nc, GDN_CHUNK, GDN_HEADS).transpose(0, 3, 1, 2)
        a_rows = head_rows(ab_g[:, :GDN_HEADS])
        b_rows = head_rows(ab_g[:, GDN_HEADS:2 * GDN_HEADS])
        y_c = gdn_mixer(p32, p16, a_rows, b_rows, gdn_conv[l], gdn_a_log[l], gdn_dt_bias[l], gdn_norm[l],
                        batch, seq, qkv_col=c_qkv // LANES, z_col=MOBA_WIDTH // GDN_WIDTH)

        k_x = norm_matmul(memf, norm_mem[l], bf(xa_wk[l]), BF16)
        v_x = norm_matmul(memf, norm_mem[l], bf(xa_wv[l]), BF16)
        i = l // 2
        is_moe = l % 2 == 1
        res = merge_cross_attention(xf, y_a, y_s, y_c, p16, bf(w_up_moba[l]), bf(ssm_w_glu[l]), bf(w_up_gdn[l]),
                                    bf(w_out[l]), norm_xa[l], bf(xa_wq[l]), k_x, v_x, bf(xa_wo[l]), batch, seq,
                                    gate_col=(MOBA_WIDTH + GDN_WIDTH) // d,
                                    router=(norm_ffn[l], moe_w_router[i]) if is_moe else None)

        if is_moe:
            xf, route, h_moe = res
            xf = moe_ffn(xf, h_moe, route, bf(moe_w_gate[i]), bf(moe_w_up[i]), bf(moe_w_down[i]),
                         final_gain=norm_final if l == depth - 1 else None)
        else:
            xf = ffn(res, norm_ffn[l], bf(ffn_w_gate[i]), bf(ffn_w_up[i]), bf(ffn_w_down[i]))

    if depth % 2 == 1:
        xf = rmsnorm(xf, norm_final)
    return xf.reshape(batch, seq, d)
```

```python
import functools

import numpy as np
import jax
import jax.numpy as jnp
from jax import lax
from jax.experimental import pallas as pl
from jax.experimental.pallas import tpu as pltpu

F32 = jnp.float32
BF16 = jnp.bfloat16
HIGHEST = lax.Precision.HIGHEST

EPS = 1e-6
NEG_INF = -1e30
MOBA_HEADS = 8
MOBA_HEAD_DIM = 64
MOBA_WIDTH = MOBA_HEADS * MOBA_HEAD_DIM
MOBA_BLOCK = 256
MOBA_TOPK = 3
ROPE_THETA = 10000.0
SSM_WIDTH = 512
SSM_GROUP = 16
SSM_GROUPS = SSM_WIDTH // SSM_GROUP
SSM_STATE = 64
SSM_STATES = SSM_GROUPS * SSM_STATE
GDN_HEADS = 4
GDN_HEAD_DIM = 128
GDN_WIDTH = GDN_HEADS * GDN_HEAD_DIM
GDN_CONV = 4
GDN_CHUNK = 64
XA_HEADS = 4
XA_HEAD_DIM = 128
XA_WIDTH = XA_HEADS * XA_HEAD_DIM
N_EXPERTS = 8
TOP_K = 2

LANES = 128
SUBLANES = 8
VMEM_LIMIT = 56 * 1024 * 1024


def _cparams(*sem):
    return pltpu.CompilerParams(dimension_semantics=sem, vmem_limit_bytes=VMEM_LIMIT)


def _rms(x, g):
    return x * lax.rsqrt(jnp.mean(x * x, axis=-1, keepdims=True) + EPS) * g


def _sigmoid(x):
    return 1.0 / (1.0 + jnp.exp(-x))


def _silu(x):
    return x * _sigmoid(x)


def _dot(a, b):
    return jnp.dot(a, b, preferred_element_type=F32)


def _dot_nt(a, b):
    return lax.dot_general(a, b, (((1,), (1,)), ((), ())), preferred_element_type=F32)


def _norm_matmul_body(x_ref, g_ref, w_ref, o_ref, h_ref):
    @pl.when(pl.program_id(1) == 0)
    def _():
        h_ref[...] = _rms(x_ref[...], g_ref[...]).astype(BF16)

    o_ref[...] = _dot(h_ref[...], w_ref[...]).astype(o_ref.dtype)


def norm_matmul(x, g, w, out_dtype, tm=512, tn=512):
    t, k = x.shape
    n = w.shape[1]
    tm, tn = min(tm, t), min(tn, n)
    return pl.pallas_call(
        _norm_matmul_body,
        grid=(t // tm, n // tn),
        in_specs=[pl.BlockSpec((tm, k), lambda i, j: (i, 0)),
                  pl.BlockSpec((1, k), lambda i, j: (0, 0)),
                  pl.BlockSpec((k, tn), lambda i, j: (0, j))],
        out_specs=pl.BlockSpec((tm, tn), lambda i, j: (i, j)),
        out_shape=jax.ShapeDtypeStruct((t, n), out_dtype),
        scratch_shapes=[pltpu.VMEM((tm, k), BF16)],
        compiler_params=_cparams("parallel", "arbitrary"),
        name="norm_matmul",
    )(x, g.reshape(1, k), w)


def _norm_matmul2_body(x_ref, g_ref, wa_ref, wb_ref, oa_ref, ob_ref, h_ref, *, na):
    j = pl.program_id(1)

    @pl.when(j == 0)
    def _():
        h_ref[...] = _rms(x_ref[...], g_ref[...]).astype(BF16)

    @pl.when(j < na)
    def _():
        oa_ref[...] = _dot(h_ref[...], wa_ref[...]).astype(oa_ref.dtype)

    @pl.when(j >= na)
    def _():
        ob_ref[...] = _dot(h_ref[...], wb_ref[...]).astype(ob_ref.dtype)


def norm_matmul2(x, g, wa, wb, dtype_a, dtype_b, tm, tna, tnb):
    t, k = x.shape
    tm = min(tm, t)
    na, nb = wa.shape[1] // tna, wb.shape[1] // tnb
    return pl.pallas_call(
        functools.partial(_norm_matmul2_body, na=na),
        grid=(t // tm, na + nb),
        in_specs=[pl.BlockSpec((tm, k), lambda i, j: (i, 0)),
                  pl.BlockSpec((1, k), lambda i, j: (0, 0)),
                  pl.BlockSpec((k, tna), lambda i, j: (0, jnp.minimum(j, na - 1))),
                  pl.BlockSpec((k, tnb), lambda i, j: (0, jnp.maximum(j - na, 0)))],
        out_specs=[pl.BlockSpec((tm, tna), lambda i, j: (i, jnp.minimum(j, na - 1))),
                   pl.BlockSpec((tm, tnb), lambda i, j: (i, jnp.maximum(j - na, 0)))],
        out_shape=[jax.ShapeDtypeStruct((t, wa.shape[1]), dtype_a), jax.ShapeDtypeStruct((t, wb.shape[1]), dtype_b)],
        scratch_shapes=[pltpu.VMEM((tm, k), BF16)],
        compiler_params=_cparams("parallel", "arbitrary"),
        name="norm_matmul2",
    )(x, g.reshape(1, k), wa, wb)


def _rope_body(pos_ref, inv_ref, cos_ref, sin_ref):
    ang = pos_ref[...] * inv_ref[...]
    lane = lax.broadcasted_iota(jnp.int32, ang.shape, 1)
    first_half = (lane % MOBA_HEAD_DIM) < (MOBA_HEAD_DIM // 2)
    cos_ref[...] = jnp.cos(ang)
    s = jnp.sin(ang)
    sin_ref[...] = jnp.where(first_half, -s, s)


def rope_tables(positions):
    t = positions.size
    half = MOBA_HEAD_DIM // 2
    inv = (1.0 / (np.float32(ROPE_THETA) ** (np.arange(0, MOBA_HEAD_DIM, 2, dtype=np.float32)
                                             / np.float32(MOBA_HEAD_DIM)))).astype(np.float32)
    inv_row = jnp.asarray(np.tile(inv, LANES // half).reshape(1, LANES))
    pos = positions.astype(F32).reshape(t, 1)
    ts = min(1024, t)
    return pl.pallas_call(
        _rope_body,
        grid=(t // ts,),
        in_specs=[pl.BlockSpec((ts, 1), lambda i: (i, 0)),
                  pl.BlockSpec((1, LANES), lambda i: (0, 0))],
        out_specs=[pl.BlockSpec((ts, LANES), lambda i: (i, 0))] * 2,
        out_shape=[jax.ShapeDtypeStruct((t, LANES), F32)] * 2,
        compiler_params=_cparams("parallel"),
        name="rope_tables",
    )(pos, inv_row)


def _rope_apply(x, cos, sin_signed):
    lane = lax.broadcasted_iota(jnp.int32, x.shape, 1)
    first_half = (lane % MOBA_HEAD_DIM) < (MOBA_HEAD_DIM // 2)
    partner = jnp.where(first_half, pltpu.roll(x, LANES - MOBA_HEAD_DIM // 2, 1),
                        pltpu.roll(x, MOBA_HEAD_DIM // 2, 1))
    return x * cos + partner * sin_signed


def _moba_body(q_ref, k_ref, v_ref, cq_ref, sq_ref, ck_ref, sk_ref, o_ref, ka_ref, va_ref, km_ref, s_ref,
               mt_ref, acc_ref, *, nb):
    blk = MOBA_BLOCK
    i = pl.program_id(2)

    @pl.when(i == 0)
    def _():
        km_ref[...] = jnp.zeros_like(km_ref)
        lane_b = lax.broadcasted_iota(jnp.int32, (blk, LANES), 1)

        def rope_blk(j, c):
            rows = pl.ds(pl.multiple_of(j * blk, blk), blk)
            kr = _rope_apply(k_ref[rows, :], ck_ref[rows, :], sk_ref[rows, :])
            ka_ref[rows, 0:LANES] = kr.astype(BF16)
            ka_ref[rows, LANES:2 * LANES] = jnp.where(lane_b == j, 1.0, 0.0).astype(BF16)
            km_ref[pl.ds(j, 1), :] = jnp.mean(kr, axis=0, keepdims=True)
            v = v_ref[rows, :].astype(F32)
            va_ref[0, rows, :] = jnp.where(lane_b < MOBA_HEAD_DIM, v,
                                           jnp.where(lane_b == MOBA_HEAD_DIM, 1.0, 0.0)).astype(BF16)
            va_ref[1, rows, :] = jnp.where(lane_b >= MOBA_HEAD_DIM, v,
                                           jnp.where(lane_b == 0, 1.0, 0.0)).astype(BF16)
            return c
        lax.fori_loop(0, nb, rope_blk, 0)

    tq = 2 * blk
    lane = lax.broadcasted_iota(jnp.int32, (1, LANES), 1)
    head_a = lane < MOBA_HEAD_DIM
    q = _rope_apply(q_ref[...], cq_ref[...], sq_ref[...])
    scale = MOBA_HEAD_DIM ** -0.5
    km = km_ref[...]
    nbp = km_ref.shape[0]
    brow = lax.broadcasted_iota(jnp.int32, (nbp, tq), 0)
    qblk = 2 * i + jnp.where(lax.broadcasted_iota(jnp.int32, (1, tq), 1) >= blk, 1, 0)
    heads = (0, 1)
    q_heads = [jnp.where(head_a, q, 0.0), jnp.where(head_a, 0.0, q)]
    km_hi = km.astype(BF16)
    km_lo = (km - km_hi.astype(F32)).astype(BF16)
    gates = []
    for qh in q_heads:
        q_hi = qh.astype(BF16)
        q_lo = (qh - q_hi.astype(F32)).astype(BF16)
        gates.append(_dot_nt(km_hi, q_hi) + (_dot_nt(km_hi, q_lo) + _dot_nt(km_lo, q_hi)))
    q_augs = []
    brow_f = brow.astype(F32)
    for h in heads:
        gate = jnp.where(brow < qblk, gates[h], NEG_INF)
        picked = jnp.zeros((nbp, tq), F32)
        for _ in range(MOBA_TOPK):
            top = jnp.max(gate, axis=0, keepdims=True)
            first = jnp.min(jnp.where(gate == top, brow_f, float(nbp)), axis=0, keepdims=True)
            hit = brow_f == first
            picked = jnp.where(hit, 1.0, picked)
            gate = jnp.where(hit, -jnp.inf, gate)
        keep = ((picked > 0.5) & (brow < qblk)) | (brow == qblk)
        bias_t = jnp.concatenate([jnp.where(keep, 0.0, NEG_INF), jnp.zeros((LANES - nbp, tq), F32)],
                                 axis=0)
        q_augs.append(jnp.concatenate([q_heads[h] * scale, bias_t.T], axis=1).astype(BF16))

    def pair_rows(jp):
        return pl.ds(pl.multiple_of(jp * tq, tq), tq)

    r_idx = lax.broadcasted_iota(jnp.int32, (tq, tq), 0)
    c_idx = lax.broadcasted_iota(jnp.int32, (tq, tq), 1)
    k_diag = ka_ref[pair_rows(i), :]
    s_diag = [_dot_nt(q_augs[h], k_diag) for h in heads]
    for h in heads:
        s_d = jnp.where(c_idx <= r_idx, s_diag[h], NEG_INF)
        s_ref[h, i] = s_d
        m_t = s_d[:, 0:LANES]
        for col in range(LANES, tq, LANES):
            m_t = jnp.maximum(m_t, s_d[:, col:col + LANES])
        mt_ref[h] = m_t

    def loop2(n, body):
        def two(k, c):
            body([2 * k, 2 * k + 1])
            return c
        lax.fori_loop(0, lax.shift_right_logical(n, 1), two, 0)

        @pl.when((n & 1) == 1)
        def _():
            body([n - 1])

    def scores(jps):
        k2s = [ka_ref[pair_rows(jp), :] for jp in jps]
        s2s = [[_dot_nt(q_augs[h], k2) for h in heads] for k2 in k2s]
        for jp, s2h in zip(jps, s2s):
            for h in heads:
                s_ref[h, jp] = s2h[h]
        for h in heads:
            m_t = mt_ref[h]
            for s2h in s2s:
                for col in range(0, tq, LANES):
                    m_t = jnp.maximum(m_t, s2h[h][:, col:col + LANES])
            mt_ref[h] = m_t
    loop2(i, scores)

    ms = [jnp.max(mt_ref[h], axis=-1, keepdims=True) for h in heads]
    for h in heads:
        acc_ref[h] = jnp.zeros((tq, LANES), F32)

    def values(jps):
        ps = [[jnp.exp(s_ref[h, jp] - ms[h]).astype(BF16) for h in heads] for jp in jps]
        pvs = [[_dot(p[h], va_ref[h, pair_rows(jp), :]) for h in heads] for jp, p in zip(jps, ps)]
        for h in heads:
            tot = pvs[0][h]
            for pv in pvs[1:]:
                tot = tot + pv[h]
            acc_ref[h] += tot
    loop2(i + 1, values)
    acc_a, acc_b = acc_ref[0], acc_ref[1]
    out_a = acc_a / acc_a[:, MOBA_HEAD_DIM:MOBA_HEAD_DIM + 1]
    out_b = acc_b / acc_b[:, 0:1]
    o_ref[...] = jnp.where(head_a, out_a, out_b).astype(o_ref.dtype)


def moba_attention(qk, v, cos, sin, batch, seq, q_col=0, k_col=MOBA_WIDTH // LANES, v_col=0):
    nb = seq // MOBA_BLOCK
    assert nb % 2 == 0
    npair = nb // 2
    pairs = MOBA_WIDTH // LANES
    tq = 2 * MOBA_BLOCK
    t = batch * seq
    return pl.pallas_call(
        functools.partial(_moba_body, nb=nb),
        grid=(batch, pairs, npair),
        in_specs=[pl.BlockSpec((tq, LANES), lambda b, p, i: (b * npair + i, q_col + p)),
                  pl.BlockSpec((seq, LANES), lambda b, p, i: (b, k_col + p)),
                  pl.BlockSpec((seq, LANES), lambda b, p, i: (b, v_col + p)),
                  pl.BlockSpec((tq, LANES), lambda b, p, i: (b * npair + i, 0)),
                  pl.BlockSpec((tq, LANES), lambda b, p, i: (b * npair + i, 0)),
                  pl.BlockSpec((seq, LANES), lambda b, p, i: (b, 0)),
                  pl.BlockSpec((seq, LANES), lambda b, p, i: (b, 0))],
        out_specs=pl.BlockSpec((tq, LANES), lambda b, p, i: (b * npair + i, p)),
        out_shape=jax.ShapeDtypeStruct((t, MOBA_WIDTH), BF16),
        scratch_shapes=[pltpu.VMEM((seq, 2 * LANES), BF16),
                        pltpu.VMEM((2, seq, LANES), BF16),
                        pltpu.VMEM((-(-nb // SUBLANES) * SUBLANES, LANES), F32),
                        pltpu.VMEM((2, npair, tq, tq), F32),
                        pltpu.VMEM((2, tq, LANES), F32),
                        pltpu.VMEM((2, tq, LANES), F32)],
        compiler_params=_cparams("parallel", "parallel", "arbitrary"),
        name="moba_attention",
    )(qk, qk, v, cos, sin, cos, sin)


def _s5_disc_body(are_ref, aim_ref, ldt_ref, bre_ref, bim_ref, lre_ref, lim_ref, bbre_ref, bbim_ref):
    a_re, a_im = are_ref[...], aim_ref[...]
    dt = jnp.exp(ldt_ref[...])
    mag = jnp.exp(a_re * dt)
    l_re = mag * jnp.cos(a_im * dt)
    l_im = mag * jnp.sin(a_im * dt)
    lre_ref[...] = l_re
    lim_ref[...] = l_im
    x, y = l_re - 1.0, l_im
    den = a_re * a_re + a_im * a_im
    c_re = (x * a_re + y * a_im) / den
    c_im = (y * a_re - x * a_im) / den
    b_re, b_im = bre_ref[...], bim_ref[...]
    bbre_ref[...] = c_re * b_re - c_im * b_im
    bbim_ref[...] = c_re * b_im + c_im * b_re


def s5_discretise(a_re, a_im, log_dt, b_re, b_im):
    g, p = a_re.shape
    n = b_re.shape[-1]
    col = lambda a: a.reshape(g * p, 1)
    ldt = jnp.broadcast_to(log_dt[:, None], (g, p))
    outs = pl.pallas_call(
        _s5_disc_body,
        out_shape=[jax.ShapeDtypeStruct((g * p, 1), F32)] * 2 + [jax.ShapeDtypeStruct((g * p, n), F32)] * 2,
        name="s5_discretise",
    )(col(a_re), col(a_im), col(ldt), b_re.reshape(g * p, n), b_im.reshape(g * p, n))
    l_re, l_im, bb_re, bb_im = outs
    return l_re.reshape(g, p), l_im.reshape(g, p), bb_re.reshape(g, p, n), bb_im.reshape(g, p, n)


def _block_diag(blocks):
    g, r, c = blocks.shape
    eye = jnp.eye(g, dtype=blocks.dtype)
    return (blocks[:, :, None, :] * eye[:, None, :, None]).reshape(g * r, g * c)


def _s5_body(u_ref, bre_ref, bim_ref, cre_ref, cim_ref, lre_ref, lim_ref, d_ref, y_ref, h_ref, hb_ref, *, steps):
    rows = SUBLANES
    ns = SSM_STATES
    nq = SSM_WIDTH // LANES
    sq = ns // nq

    @pl.when(pl.program_id(0) == 0)
    def _():
        h_ref[...] = jnp.zeros_like(h_ref)

    u = u_ref[...]
    u16 = u.astype(BF16)
    for q in range(nq):
        uq = u16[:, q * LANES:(q + 1) * LANES]
        hb_ref[:, q * sq:(q + 1) * sq] = _dot(uq, bre_ref[q])
        hb_ref[:, ns + q * sq:ns + (q + 1) * sq] = _dot(uq, bim_ref[q])
    l_re, l_im = lre_ref[...], lim_ref[...]

    def step(t, carry):
        h_re, h_im = carry
        r = pl.ds(pl.multiple_of(t * rows, rows), rows)
        n_re = l_re * h_re - l_im * h_im + hb_ref[r, 0:ns]
        n_im = l_re * h_im + l_im * h_re + hb_ref[r, ns:2 * ns]
        hb_ref[r, 0:ns] = n_re
        hb_ref[r, ns:2 * ns] = n_im
        return n_re, n_im

    h_re, h_im = lax.fori_loop(0, steps, step, (h_ref[:, 0:ns], h_ref[:, ns:2 * ns]), unroll=4)
    h_ref[:, 0:ns] = h_re
    h_ref[:, ns:2 * ns] = h_im
    ys = []
    for q in range(nq):
        ys.append(_dot(hb_ref[:, q * sq:(q + 1) * sq].astype(BF16), cre_ref[q])
                  + _dot(hb_ref[:, ns + q * sq:ns + (q + 1) * sq].astype(BF16), cim_ref[q]))
    y = jnp.concatenate(ys, axis=1) + d_ref[...] * u
    y_ref[...] = jax.nn.gelu(y).astype(y_ref.dtype)


def s5_block_maps(bb_re, bb_im, c_re, c_im):
    nq = SSM_WIDTH // LANES
    gq = SSM_GROUPS // nq
    per_q = lambda a: jnp.stack([_block_diag(a[q * gq:(q + 1) * gq].transpose(0, 2, 1)) for q in range(nq)])
    return (per_q(bb_re).astype(BF16), per_q(bb_im).astype(BF16),
            per_q(c_re).astype(BF16), per_q(-c_im).astype(BF16))


def s5_gelu(u_tm, maps, l_re, l_im, d_skip, steps=64):
    rows = u_tm.shape[0]
    blk = steps * SUBLANES
    ns2 = 2 * SSM_STATES
    full3 = lambda a: pl.BlockSpec(a.shape, lambda c: (0, 0, 0))
    return pl.pallas_call(
        functools.partial(_s5_body, steps=steps),
        grid=(rows // blk,),
        in_specs=[pl.BlockSpec((blk, SSM_WIDTH), lambda c: (c, 0)),
                  full3(maps[0]), full3(maps[1]), full3(maps[2]), full3(maps[3]),
                  pl.BlockSpec((1, SSM_STATES), lambda c: (0, 0)),
                  pl.BlockSpec((1, SSM_STATES), lambda c: (0, 0)),
                  pl.BlockSpec((1, SSM_WIDTH), lambda c: (0, 0))],
        out_specs=pl.BlockSpec((blk, SSM_WIDTH), lambda c: (c, 0)),
        out_shape=jax.ShapeDtypeStruct((rows, SSM_WIDTH), BF16),
        scratch_shapes=[pltpu.VMEM((SUBLANES, ns2), F32),
                        pltpu.VMEM((blk, ns2), F32)],
        compiler_params=_cparams("arbitrary"),
        name="s5_scan",
    )(u_tm, *maps, l_re.reshape(1, -1), l_im.reshape(1, -1), d_skip.reshape(1, -1))


GDN_GROUP = 16


def _gdn_prep_body(q_ref, k_ref, v_ref, a_ref, b_ref, wq_ref, wk_ref, wv_ref, alog_ref, dtb_ref,
                   u_ref, w_ref, qd_ref, kd_ref, qk_ref, gl_ref, qs_ref, ks_ref, vs_ref,
                   *, seq):
    L = GDN_CHUNK
    G = min(GDN_GROUP, seq // L)
    dk = GDN_HEAD_DIM
    nc = seq // L
    hd = pl.program_id(1)
    cb = 256
    halo = SUBLANES

    def conv_blk(rb, c):
        base = pl.multiple_of(rb * cb, cb)
        prev = pl.multiple_of(jnp.maximum(base - halo, 0), halo)
        for src, wref, dst, norm, mul in ((q_ref, wq_ref, qs_ref, True, dk ** -0.5),
                                          (k_ref, wk_ref, ks_ref, True, 1.0),
                                          (v_ref, wv_ref, vs_ref, False, 1.0)):
            cur = src[pl.ds(base, cb), :]
            head = jnp.where(rb > 0, src[pl.ds(prev, halo), :], 0.0)
            ext = jnp.concatenate([head, cur], axis=0)
            wts = wref[...]
            y = ext[halo:halo + cb] * wts[GDN_CONV - 1:GDN_CONV]
            for tap in range(1, GDN_CONV):
                y = y + ext[halo - tap:halo - tap + cb] * wts[GDN_CONV - 1 - tap:GDN_CONV - tap]
            y = _silu(y)
            if norm:
                y = y * lax.rsqrt(jnp.sum(y * y, axis=-1, keepdims=True) + EPS) * mul
            dst[pl.ds(base, cb), :] = y
        return c
    lax.fori_loop(0, seq // cb, conv_blk, 0)

    gl_rows = G * L
    ri = lax.broadcasted_iota(jnp.int32, (gl_rows, L), 0) % L
    ci = lax.broadcasted_iota(jnp.int32, (gl_rows, L), 1)
    tril = ci <= ri
    strict = ci < ri
    eye = ci == ri
    ri3 = lax.broadcasted_iota(jnp.int32, (G, L, L), 1)
    ci3 = lax.broadcasted_iota(jnp.int32, (G, L, L), 2)
    strict3 = ci3 < ri3
    eye_f = jnp.where(ci3 == ri3, 1.0, 0.0)
    r2 = lax.broadcasted_iota(jnp.int32, (L, L), 0)
    c2 = lax.broadcasted_iota(jnp.int32, (L, L), 1)
    upper_f = jnp.where(r2 <= c2, 1.0, 0.0)
    a_coef = -jnp.exp(jnp.full((1, L), alog_ref[hd], F32))
    dt_bias = dtb_ref[hd]
    hp = functools.partial(jnp.dot, preferred_element_type=F32, precision=HIGHEST)

    def bmm(a, b):
        return jnp.einsum('gij,gjk->gik', a, b, preferred_element_type=F32)

    def bmm_nt(a, b):
        return jnp.einsum('gid,gjd->gij', a, b, preferred_element_type=F32)

    def to_col(rows_b):
        return jnp.sum(jnp.where(eye, rows_b, 0.0), axis=1, keepdims=True)

    def prep(cg, carry):
        c0 = pl.multiple_of(cg * G, G)
        rows = pl.ds(pl.multiple_of(cg * gl_rows, gl_rows), gl_rows)
        qc, kc, vc = qs_ref[rows, :], ks_ref[rows, :], vs_ref[rows, :]
        z = a_ref[pl.ds(c0, G), :] + dt_bias
        softplus = jnp.maximum(z, 0.0) + jnp.log(1.0 + jnp.exp(-jnp.abs(z)))
        g_rows = hp(a_coef * softplus, upper_f)
        beta_rows = _sigmoid(b_ref[pl.ds(c0, G), :])
        spread = lambda r: jnp.concatenate([jnp.broadcast_to(r[c:c + 1, :], (L, L)) for c in range(G)], axis=0)
        g_rb = spread(g_rows)
        g_col = to_col(g_rb)
        beta_col = to_col(spread(beta_rows))
        g_last = g_rb[:, L - 1:L]
        decay = jnp.where(tril, jnp.exp(jnp.where(tril, g_col - g_rb, 0.0)), 0.0)
        k_beta = kc * beta_col
        v_beta = vc * beta_col
        to3 = lambda a: a.reshape(G, L, a.shape[-1])
        kb16, k16, q16 = to3(k_beta.astype(BF16)), to3(kc.astype(BF16)), to3(qc.astype(BF16))
        kk_qk = bmm_nt(jnp.concatenate([kb16, q16], axis=1), k16)
        decay3 = to3(decay)
        a_low = jnp.where(strict3, kk_qk[:, :L] * decay3, 0.0)
        qk = kk_qk[:, L:] * decay3
        t_inv = eye_f - a_low
        a16 = a_low.astype(BF16)
        pw = bmm(a16, a16)
        span = 2
        while 2 * span < L:
            pw16 = pw.astype(BF16)
            both = bmm(jnp.concatenate([t_inv.astype(BF16), pw16], axis=1), pw16)
            t_inv = t_inv + both[:, :L]
            pw = both[:, L:]
            span *= 2
        t_inv = t_inv + bmm(t_inv.astype(BF16), pw.astype(BF16))
        rhs = jnp.concatenate([v_beta.astype(BF16), (k_beta * jnp.exp(g_col)).astype(BF16)], axis=1)
        uw = bmm(t_inv.astype(BF16), to3(rhs)).reshape(gl_rows, 2 * dk)
        u_ref[rows, :] = uw[:, :dk]
        w_ref[rows, :] = uw[:, dk:].astype(BF16)
        qd_ref[rows, :] = (qc * jnp.exp(g_col)).astype(BF16)
        k_dec = to3(kc * jnp.exp(g_last - g_col))
        kd_ref[pl.ds(c0, G)] = jnp.swapaxes(k_dec, 1, 2).astype(BF16)
        qk_ref[pl.ds(c0, G)] = qk.astype(BF16)
        gl_ref[pl.ds(c0, G), :] = jnp.broadcast_to(jnp.exp(g_rows[:, L - 1:L]), (G, LANES))
        return carry
    lax.fori_loop(0, nc // G, prep, 0)


def _gdn_scan_body(u_ref, w_ref, qd_ref, kd_ref, qk_ref, gl_ref, z_ref, gn_ref, o_ref, *st_refs, chunks):
    L = GDN_CHUNK
    d = GDN_HEAD_DIM
    nrow = u_ref.shape[0]

    @pl.when(pl.program_id(1) == 0)
    def _():
        for st in st_refs:
            st[...] = jnp.zeros_like(st)

    gn = gn_ref[...]

    def chunk(c, carry):
        rows = pl.ds(pl.multiple_of(c * L, L), L)
        chains = [(r, h) for r in range(nrow) for h in range(GDN_HEADS)]
        states = [st_refs[r * GDN_HEADS + h][...] for r, h in chains]
        firsts = []
        for (r, h), s in zip(chains, states):
            cols = slice(h * d, (h + 1) * d)
            firsts.append(_dot(jnp.concatenate([w_ref[r, rows, cols], qd_ref[r, rows, cols]], axis=0),
                               s.astype(BF16)))
        vns = []
        for (r, h), both in zip(chains, firsts):
            cols = slice(h * d, (h + 1) * d)
            vns.append((u_ref[r, rows, cols] - both[:L]).astype(BF16))
        outs = [both[L:] + _dot(qk_ref[r, h, c], vn) for (r, h), both, vn in zip(chains, firsts, vns)]
        for (r, h), s, vn in zip(chains, states, vns):
            st_refs[r * GDN_HEADS + h][...] = s * gl_ref[r, h, pl.ds(c, 1), :] + _dot(kd_ref[r, h, c], vn)
        for (r, h), o in zip(chains, outs):
            cols = slice(h * d, (h + 1) * d)
            o_ref[r, rows, cols] = (_rms(o, gn) * _silu(z_ref[r, rows, cols].astype(F32))).astype(o_ref.dtype)
        return carry
    lax.fori_loop(0, chunks, chunk, 0)


def gdn_mixer(qkv, z, a_rows, b_rows, conv_w, a_log, dt_bias, gnorm, batch, seq, ts=1024, qkv_col=0, z_col=0):
    t = batch * seq
    hds = GDN_HEADS
    nc = seq // GDN_CHUNK
    L = GDN_CHUNK
    d = GDN_HEAD_DIM
    seq_spec = lambda off: pl.BlockSpec((seq, d), lambda b, h: (b, qkv_col + off + h))
    w_spec = lambda off: pl.BlockSpec((GDN_CONV, d), lambda b, h: (0, off + h))
    row_spec = pl.BlockSpec((None, None, nc, L), lambda b, h: (b, h, 0, 0))
    smem = pl.BlockSpec(memory_space=pltpu.SMEM)
    head_out = pl.BlockSpec((seq, d), lambda b, h: (b, h))
    u, w, qd, kd, qk, gl = pl.pallas_call(
        functools.partial(_gdn_prep_body, seq=seq),
        grid=(batch, hds),
        in_specs=[seq_spec(0), seq_spec(hds), seq_spec(2 * hds), row_spec, row_spec,
                  w_spec(0), w_spec(hds), w_spec(2 * hds), smem, smem],
        out_specs=[head_out, head_out, head_out,
                   pl.BlockSpec((None, None, nc, d, L), lambda b, h: (b, h, 0, 0, 0)),
                   pl.BlockSpec((None, None, nc, L, L), lambda b, h: (b, h, 0, 0, 0)),
                   pl.BlockSpec((None, None, nc, LANES), lambda b, h: (b, h, 0, 0))],
        out_shape=[jax.ShapeDtypeStruct((t, GDN_WIDTH), F32),
                   jax.ShapeDtypeStruct((t, GDN_WIDTH), BF16),
                   jax.ShapeDtypeStruct((t, GDN_WIDTH), BF16),
                   jax.ShapeDtypeStruct((batch, hds, nc, d, L), BF16),
                   jax.ShapeDtypeStruct((batch, hds, nc, L, L), BF16),
                   jax.ShapeDtypeStruct((batch, hds, nc, LANES), F32)],
        scratch_shapes=[pltpu.VMEM((seq, d), F32), pltpu.VMEM((seq, d), F32), pltpu.VMEM((seq, d), F32)],
        compiler_params=_cparams("parallel", "parallel"),
        name="gdn_prep",
    )(qkv, qkv, qkv, a_rows, b_rows, conv_w, conv_w, conv_w, a_log, dt_bias)

    ts = min(ts, seq)
    spb = seq // ts
    cps = ts // L
    nrow = 2 if batch % 2 == 0 else 1
    by_row = lambda a: a.reshape(batch, seq, a.shape[-1])
    tok = pl.BlockSpec((nrow, ts, GDN_WIDTH), lambda b, s: (b, s, 0))
    out = pl.pallas_call(
        functools.partial(_gdn_scan_body, chunks=cps),
        grid=(batch // nrow, spb),
        in_specs=[tok, tok, tok,
                  pl.BlockSpec((nrow, hds, cps, d, L), lambda b, s: (b, 0, s, 0, 0)),
                  pl.BlockSpec((nrow, hds, cps, L, L), lambda b, s: (b, 0, s, 0, 0)),
                  pl.BlockSpec((nrow, hds, cps, LANES), lambda b, s: (b, 0, s, 0)),
                  pl.BlockSpec((nrow, ts, GDN_WIDTH), lambda b, s: (b, s, z_col)),
                  pl.BlockSpec((1, d), lambda b, s: (0, 0))],
        out_specs=tok,
        out_shape=jax.ShapeDtypeStruct((batch, seq, GDN_WIDTH), BF16),
        scratch_shapes=[pltpu.VMEM((d, d), F32)] * (nrow * hds),
        compiler_params=_cparams("parallel", "arbitrary"),
        name="gdn_scan",
    )(by_row(u), by_row(w), by_row(qd), kd, qk, gl, by_row(z), gnorm.reshape(1, d))
    return out.reshape(t, GDN_WIDTH)


def _merge_xattn_body(x_ref, ya_ref, ys_ref, yc_ref, ga_ref, gb_ref, gc_ref, wa_ref, wg_ref, wc_ref, wo_ref,
                      gx_ref, wq_ref, k_ref, v_ref, wxo_ref, o_ref):
    d = x_ref.shape[1]
    y_a = _dot(ya_ref[...], wa_ref[...])
    glu = _dot(ys_ref[...], wg_ref[...])
    y_b = glu[:, :d] * _sigmoid(glu[:, d:])
    y_c = _dot(yc_ref[...], wc_ref[...])
    gate = lambda r: _sigmoid(r[...].astype(F32))
    merged = gate(ga_ref) * y_a + gate(gb_ref) * y_b + gate(gc_ref) * y_c
    x1 = x_ref[...] + _dot(merged.astype(BF16), wo_ref[...])

    h = _rms(x1, gx_ref[...]).astype(BF16)
    q = _dot(h, wq_ref[...]).astype(BF16)
    cols = [slice(hd * XA_HEAD_DIM, (hd + 1) * XA_HEAD_DIM) for hd in range(XA_HEADS)]
    scores = [_dot_nt(q[:, c], k_ref[:, c]) * (XA_HEAD_DIM ** -0.5) for c in cols]
    probs = []
    for s in scores:
        p = jnp.exp(s - jnp.max(s, axis=-1, keepdims=True))
        probs.append((p / jnp.sum(p, axis=-1, keepdims=True)).astype(BF16))
    outs = [_dot(p, v_ref[:, c]) for p, c in zip(probs, cols)]
    o = jnp.concatenate(outs, axis=1).astype(BF16)
    o_ref[...] = x1 + _dot(o, wxo_ref[...])


def merge_cross_attention(x, y_a, y_s, y_c, gates, w_a, w_glu, w_c, w_o, g_xa, wq, k, v, wo_xa, batch, seq,
                          tm=512, gate_col=0):
    t, d = x.shape
    m = k.shape[0] // batch
    spb = seq // tm
    row = lambda n: pl.BlockSpec((tm, n), lambda i: (i, 0))
    gate = lambda j: pl.BlockSpec((tm, d), lambda i: (i, gate_col + j))
    full = lambda a: pl.BlockSpec(a.shape, lambda i: (0, 0))
    mem = pl.BlockSpec((m, XA_WIDTH), lambda i: (i // spb, 0))
    return pl.pallas_call(
        _merge_xattn_body,
        grid=(t // tm,),
        in_specs=[row(d), row(y_a.shape[1]), row(y_s.shape[1]), row(y_c.shape[1]), gate(0), gate(1), gate(2),
                  full(w_a), full(w_glu), full(w_c), full(w_o),
                  pl.BlockSpec((1, d), lambda i: (0, 0)), full(wq), mem, mem, full(wo_xa)],
        out_specs=row(d),
        out_shape=jax.ShapeDtypeStruct((t, d), F32),
        compiler_params=_cparams("parallel"),
        name="merge_cross_attention",
    )(x, y_a, y_s, y_c, gates, gates, gates, w_a, w_glu, w_c, w_o, g_xa.reshape(1, d), wq, k, v, wo_xa)


def _ffn_body(x_ref, g_ref, wg_ref, wu_ref, wd_ref, o_ref, h_ref, acc_ref):
    f = pl.program_id(1)

    @pl.when(f == 0)
    def _():
        h_ref[...] = _rms(x_ref[...], g_ref[...]).astype(BF16)
        acc_ref[...] = jnp.zeros_like(acc_ref)

    h = h_ref[...]
    act = (_silu(_dot(h, wg_ref[...])) * _dot(h, wu_ref[...])).astype(BF16)
    acc_ref[...] += _dot(act, wd_ref[...])

    @pl.when(f == pl.num_programs(1) - 1)
    def _():
        o_ref[...] = x_ref[...] + acc_ref[...]


def ffn(x, g, w_gate, w_up, w_down, tm=512, tf=1408):
    t, d = x.shape
    f = w_gate.shape[1]
    return pl.pallas_call(
        _ffn_body,
        grid=(t // tm, f // tf),
        in_specs=[pl.BlockSpec((tm, d), lambda i, k: (i, 0)),
                  pl.BlockSpec((1, d), lambda i, k: (0, 0)),
                  pl.BlockSpec((d, tf), lambda i, k: (0, k)),
                  pl.BlockSpec((d, tf), lambda i, k: (0, k)),
                  pl.BlockSpec((tf, d), lambda i, k: (k, 0))],
        out_specs=pl.BlockSpec((tm, d), lambda i, k: (i, 0)),
        out_shape=jax.ShapeDtypeStruct((t, d), F32),
        scratch_shapes=[pltpu.VMEM((tm, d), BF16), pltpu.VMEM((tm, d), F32)],
        compiler_params=_cparams("parallel", "arbitrary"),
        name="ffn",
    )(x, g.reshape(1, d), w_gate, w_up, w_down)


def _router_body(x_ref, g_ref, whi_ref, wlo_ref, r_ref, h_ref):
    h = _rms(x_ref[...], g_ref[...])
    h_hi = h.astype(BF16)
    h_ref[...] = h_hi
    h_lo = (h - h_hi.astype(F32)).astype(BF16)
    logits = _dot(h_hi, whi_ref[...]) + (_dot(h_hi, wlo_ref[...]) + _dot(h_lo, whi_ref[...]))
    lane = lax.broadcasted_iota(jnp.int32, logits.shape, 1).astype(F32)
    logits = jnp.where(lane < N_EXPERTS, logits, -jnp.inf)
    v1 = jnp.max(logits, axis=-1, keepdims=True)
    i1 = jnp.min(jnp.where(logits == v1, lane, float(LANES)), axis=-1, keepdims=True)
    rest = jnp.where(lane == i1, -jnp.inf, logits)
    v2 = jnp.max(rest, axis=-1, keepdims=True)
    i2 = jnp.min(jnp.where(rest == v2, lane, float(LANES)), axis=-1, keepdims=True)
    e2 = jnp.exp(v2 - v1)
    w1 = 1.0 / (1.0 + e2)
    w2 = e2 / (1.0 + e2)
    r_ref[...] = (jnp.where(lane == 0.0, i1, 0.0) + jnp.where(lane == 1.0, i2, 0.0)
                  + jnp.where(lane == 2.0, w1, 0.0) + jnp.where(lane == 3.0, w2, 0.0))


def moe_router(x, g, w_router, tm=512):
    t, d = x.shape
    w = jnp.pad(w_router, ((0, 0), (0, LANES - w_router.shape[1])))
    w_hi = w.astype(BF16)
    w_lo = (w - w_hi.astype(F32)).astype(BF16)
    return pl.pallas_call(
        _router_body,
        grid=(t // tm,),
        in_specs=[pl.BlockSpec((tm, d), lambda i: (i, 0)),
                  pl.BlockSpec((1, d), lambda i: (0, 0)),
                  pl.BlockSpec((d, LANES), lambda i: (0, 0)),
                  pl.BlockSpec((d, LANES), lambda i: (0, 0))],
        out_specs=[pl.BlockSpec((tm, LANES), lambda i: (i, 0)), pl.BlockSpec((tm, d), lambda i: (i, 0))],
        out_shape=[jax.ShapeDtypeStruct((t, LANES), F32), jax.ShapeDtypeStruct((t, d), BF16)],
        compiler_params=_cparams("parallel"),
        name="moe_router",
    )(x, g.reshape(1, d), w_hi, w_lo)


MOE_TILE = 512
MOE_GTILE = 256
MOE_CHUNK = 512
MOE_RBLK = 256


def _count_le(sorted_vals, queries):
    return jnp.sum((sorted_vals[None, :] <= queries[:, None]).astype(jnp.int32), axis=1)


def _moe_plan(route, t):
    e = N_EXPERTS
    tm, ck, rb = MOE_TILE, MOE_CHUNK, MOE_RBLK
    n_tiles = (TOP_K * t) // tm + e
    p = n_tiles * tm
    n_chunks = t // ck
    i32 = jnp.int32
    e1 = route[:, 0].astype(i32)
    e2 = route[:, 1].astype(i32)
    oh1 = jax.nn.one_hot(e1, e, dtype=i32)
    oh2 = jax.nn.one_hot(e2, e, dtype=i32)
    cnt = oh1 + oh2
    incl = jnp.cumsum(cnt, axis=0)
    excl = incl - cnt
    n_e = incl[-1]
    g_e = ((n_e + tm - 1) // tm) * tm
    off_end = jnp.cumsum(g_e)
    off = off_end - g_e
    pos_all = off[None, :] + excl
    pos1 = jnp.sum(pos_all * oh1, axis=1)
    pos2 = jnp.sum(pos_all * oh2, axis=1)
    tile_start = jnp.arange(n_tiles, dtype=i32) * tm
    tile_expert = jnp.minimum(_count_le(off_end, tile_start), e - 1)
    tile_sel = jax.nn.one_hot(tile_expert, e, dtype=i32)
    tile_used = (tile_start < jnp.sum(tile_sel * (off + n_e)[None, :], axis=1)).astype(i32)
    gm = MOE_GTILE
    n_gt = p // gm
    gt_start = jnp.arange(n_gt, dtype=i32) * gm
    sel = jax.nn.one_hot(jnp.minimum(_count_le(off_end, gt_start), e - 1), e, dtype=i32)
    r0 = gt_start - jnp.sum(sel * off[None, :], axis=1)
    n_sel = jnp.sum(sel * n_e[None, :], axis=1)
    has_rows = r0 < n_sel
    r1 = jnp.minimum(r0 + gm, n_sel) - 1
    ends = incl[ck - 1::ck]
    ends_sel = jnp.sum(ends[None, :, :] * sel[:, None, :], axis=2)
    c_lo = jnp.where(has_rows, jnp.sum((ends_sel <= r0[:, None]).astype(i32), axis=1), 0)
    c_hi = jnp.where(has_rows, jnp.sum((ends_sel <= r1[:, None]).astype(i32), axis=1), 0)
    n_it = c_hi - c_lo + 1
    it_end = jnp.cumsum(n_it)
    it_start = it_end - n_it
    wmax = n_gt + e * n_chunks
    w_idx = jnp.arange(wmax, dtype=i32)
    g_tile = jnp.minimum(_count_le(it_end, w_idx), n_gt - 1)
    g_valid = w_idx < it_end[-1]
    g_chunk = jnp.where(g_valid, c_lo[g_tile] + (w_idx - it_start[g_tile]), c_hi[n_gt - 1])
    g_first = (g_valid & (w_idx == it_start[g_tile])).astype(i32)
    g_last = (g_valid & (w_idx == it_end[g_tile] - 1)).astype(i32)
    gather = (g_tile, g_chunk.astype(i32), g_first, g_last, g_valid.astype(i32))
    lo = off[None, :] + excl[::ck]
    hi = off[None, :] + ends
    nblk = jnp.where(hi > lo, (hi - 1) // rb - lo // rb + 1, 0).reshape(-1)
    blk_lo = (lo // rb).reshape(-1)
    cb_end = jnp.cumsum(nblk)
    cb_start = cb_end - nblk
    cmax = p // rb + e * n_chunks
    c_idx = jnp.arange(cmax, dtype=i32)
    pair = jnp.minimum(_count_le(cb_end, c_idx), nblk.shape[0] - 1)
    c_valid = c_idx < cb_end[-1]
    c_tile = jnp.where(c_valid, pair // e, n_chunks - 1)
    c_blk = jnp.where(c_valid, blk_lo[pair] + (c_idx - cb_start[pair]), 0)
    per_tile = jnp.sum(nblk.reshape(n_chunks, e), axis=1)
    t_end = jnp.cumsum(per_tile)
    c_first = (c_valid & (c_idx == (t_end - per_tile)[c_tile])).astype(i32)
    c_last = (c_valid & (c_idx == t_end[c_tile] - 1)).astype(i32)
    combine = (c_tile.astype(i32), c_blk.astype(i32), c_first, c_last, c_valid.astype(i32))
    return dict(n_tiles=n_tiles, p=p, tile_expert=tile_expert.astype(i32), tile_used=tile_used, pos1=pos1, pos2=pos2,
                gather=gather, combine=combine)


def _moe_gather_body(tile_ref, chunk_ref, first_ref, last_ref, valid_ref, meta_ref, h_ref,
                     o_ref, wr_ref, acc_ref, wacc_ref):
    w = pl.program_id(0)

    @pl.when(first_ref[w] == 1)
    def _():
        acc_ref[...] = jnp.zeros_like(acc_ref)
        wacc_ref[...] = jnp.zeros_like(wacc_ref)

    @pl.when(valid_ref[w] == 1)
    def _():
        row = (lax.broadcasted_iota(jnp.int32, (MOE_GTILE, MOE_CHUNK), 0) + tile_ref[w] * MOE_GTILE).astype(F32)
        hit1 = row == meta_ref[0:1, :]
        hit2 = row == meta_ref[1:2, :]
        onehot = jnp.where(hit1, 1.0, jnp.where(hit2, 1.0, 0.0)).astype(BF16)
        acc_ref[...] += _dot(onehot, h_ref[...])
        wts = jnp.where(hit1, meta_ref[2:3, :], jnp.where(hit2, meta_ref[3:4, :], 0.0))
        wacc_ref[...] += jnp.sum(wts, axis=1, keepdims=True)

    @pl.when(last_ref[w] == 1)
    def _():
        o_ref[...] = acc_ref[...].astype(o_ref.dtype)
        wr_ref[...] = jnp.broadcast_to(wacc_ref[...], wr_ref.shape)


def _moe_ffn_body(exp_ref, used_ref, x_ref, wr_ref, wg_ref, wu_ref, wd_ref, o_ref, acc_ref):
    f = pl.program_id(1)

    @pl.when(f == 0)
    def _():
        acc_ref[...] = jnp.zeros_like(acc_ref)

    @pl.when(used_ref[pl.program_id(0)] == 1)
    def _():
        x = x_ref[...]
        act = (_silu(_dot(x, wg_ref[...])) * _dot(x, wu_ref[...])).astype(BF16)
        acc_ref[...] += _dot(act, wd_ref[...])

    @pl.when(f == pl.num_programs(1) - 1)
    def _():
        o_ref[...] = (acc_ref[...] * wr_ref[:, 0:1]).astype(o_ref.dtype)


def _moe_combine_body(tile_ref, blk_ref, first_ref, last_ref, valid_ref, x_ref, pos_ref, y_ref, gf_ref,
                      o_ref, acc_ref, *, final_norm):
    w = pl.program_id(0)

    @pl.when(first_ref[w] == 1)
    def _():
        acc_ref[...] = x_ref[...]

    @pl.when(valid_ref[w] == 1)
    def _():
        base = blk_ref[w] * MOE_RBLK
        col = lax.broadcasted_iota(jnp.int32, (MOE_CHUNK, MOE_RBLK), 1) + base
        reps = MOE_RBLK // LANES
        p1 = jnp.concatenate([pos_ref[:, 0:LANES]] * reps, axis=1)
        p2 = jnp.concatenate([pos_ref[:, LANES:2 * LANES]] * reps, axis=1)
        hit = jnp.where(col == p1, 1.0, jnp.where(col == p2, 1.0, 0.0))
        acc_ref[...] += _dot(hit.astype(BF16), y_ref[...])

    @pl.when(last_ref[w] == 1)
    def _():
        o_ref[...] = _rms(acc_ref[...], gf_ref[...]) if final_norm else acc_ref[...]


def moe_ffn(x, h, route, w_gate, w_up, w_down, final_gain=None, tf=1408):
    t, d = x.shape
    f = w_gate.shape[2]
    tm, ck, rb = MOE_TILE, MOE_CHUNK, MOE_RBLK
    plan = _moe_plan(route, t)
    n_tiles, p = plan["n_tiles"], plan["p"]
    assert p < 2 ** 24, "row positions are carried exactly in f32"
    meta = jnp.stack([plan["pos1"].astype(F32), plan["pos2"].astype(F32), route[:, 2], route[:, 3]]
                     + [jnp.zeros((t,), F32)] * (SUBLANES - 4))
    pos_lanes = jnp.concatenate([jnp.broadcast_to(plan["pos1"][:, None], (t, LANES)),
                                 jnp.broadcast_to(plan["pos2"][:, None], (t, LANES))], axis=1)

    xs, w_row = pl.pallas_call(
        _moe_gather_body,
        grid_spec=pltpu.PrefetchScalarGridSpec(
            num_scalar_prefetch=5,
            grid=(plan["gather"][0].shape[0],),
            in_specs=[pl.BlockSpec((SUBLANES, ck), lambda w, tl, ch, fi, la, va: (0, ch[w])),
                      pl.BlockSpec((ck, d), lambda w, tl, ch, fi, la, va: (ch[w], 0))],
            out_specs=[pl.BlockSpec((MOE_GTILE, d), lambda w, tl, ch, fi, la, va: (tl[w], 0)),
                       pl.BlockSpec((MOE_GTILE, LANES), lambda w, tl, ch, fi, la, va: (tl[w], 0))],
            scratch_shapes=[pltpu.VMEM((MOE_GTILE, d), F32), pltpu.VMEM((MOE_GTILE, 1), F32)]),
        out_shape=[jax.ShapeDtypeStruct((p, d), BF16), jax.ShapeDtypeStruct((p, LANES), F32)],
        compiler_params=_cparams("arbitrary"),
        name="moe_gather",
    )(*plan["gather"], meta, h)

    ys = pl.pallas_call(
        _moe_ffn_body,
        grid_spec=pltpu.PrefetchScalarGridSpec(
            num_scalar_prefetch=2,
            grid=(n_tiles, f // tf),
            in_specs=[pl.BlockSpec((tm, d), lambda i, k, ex, us: (i, 0)),
                      pl.BlockSpec((tm, LANES), lambda i, k, ex, us: (i, 0)),
                      pl.BlockSpec((None, d, tf), lambda i, k, ex, us: (ex[i], 0, k)),
                      pl.BlockSpec((None, d, tf), lambda i, k, ex, us: (ex[i], 0, k)),
                      pl.BlockSpec((None, tf, d), lambda i, k, ex, us: (ex[i], k, 0))],
            out_specs=pl.BlockSpec((tm, d), lambda i, k, ex, us: (i, 0)),
            scratch_shapes=[pltpu.VMEM((tm, d), F32)]),
        out_shape=jax.ShapeDtypeStruct((p, d), BF16),
        compiler_params=_cparams("parallel", "arbitrary"),
        name="moe_ffn",
    )(plan["tile_expert"], plan["tile_used"], xs, w_row, w_gate, w_up, w_down)

    final_norm = final_gain is not None
    gain = (final_gain if final_norm else jnp.ones((d,), F32)).reshape(1, d)
    return pl.pallas_call(
        functools.partial(_moe_combine_body, final_norm=final_norm),
        grid_spec=pltpu.PrefetchScalarGridSpec(
            num_scalar_prefetch=5,
            grid=(plan["combine"][0].shape[0],),
            in_specs=[pl.BlockSpec((ck, d), lambda w, tl, bk, fi, la, va: (tl[w], 0)),
                      pl.BlockSpec((ck, 2 * LANES), lambda w, tl, bk, fi, la, va: (tl[w], 0)),
                      pl.BlockSpec((rb, d), lambda w, tl, bk, fi, la, va: (bk[w], 0)),
                      pl.BlockSpec((1, d), lambda w, tl, bk, fi, la, va: (0, 0))],
            out_specs=pl.BlockSpec((ck, d), lambda w, tl, bk, fi, la, va: (tl[w], 0)),
            scratch_shapes=[pltpu.VMEM((ck, d), F32)]),
        out_shape=jax.ShapeDtypeStruct((t, d), F32),
        compiler_params=_cparams("arbitrary"),
        name="moe_combine",
    )(*plan["combine"], x, pos_lanes, ys, gain)


def _rmsnorm_body(x_ref, g_ref, o_ref):
    o_ref[...] = _rms(x_ref[...], g_ref[...])


def rmsnorm(x, g, tm=1024):
    t, d = x.shape
    tm = min(tm, t)
    return pl.pallas_call(
        _rmsnorm_body,
        grid=(t // tm,),
        in_specs=[pl.BlockSpec((tm, d), lambda i: (i, 0)), pl.BlockSpec((1, d), lambda i: (0, 0))],
        out_specs=pl.BlockSpec((tm, d), lambda i: (i, 0)),
        out_shape=jax.ShapeDtypeStruct((t, d), F32),
        compiler_params=_cparams("parallel"),
        name="rmsnorm",
    )(x, g.reshape(1, d))


def _in_proj_slices(w):
    sizes = (MOBA_WIDTH, MOBA_WIDTH, MOBA_WIDTH, SSM_WIDTH, GDN_WIDTH, GDN_WIDTH, GDN_WIDTH,
             GDN_HEADS, GDN_HEADS, GDN_WIDTH, w.shape[0], w.shape[0], w.shape[0])
    parts, start = [], 0
    for size in sizes:
        parts.append(w[:, start:start + size])
        start += size
    return parts


def kernel(x, mem, positions, norm_mix, w_in, ssm_a_re, ssm_a_im, ssm_log_dt, ssm_b_re, ssm_b_im, ssm_c_re, ssm_c_im, ssm_d, ssm_w_glu, gdn_conv, gdn_a_log, gdn_dt_bias, gdn_norm, w_up_moba, w_up_gdn, w_out, norm_xa, norm_mem, xa_wq, xa_wk, xa_wv, xa_wo, norm_ffn, ffn_w_gate, ffn_w_up, ffn_w_down, moe_w_router, moe_w_gate, moe_w_up, moe_w_down, norm_final):
    batch, seq, d = x.shape
    depth = w_in.shape[0]
    t = batch * seq
    assert batch == SUBLANES, "the S5 scan packs the batch onto the 8 sublanes of a vreg"
    assert seq % MOBA_BLOCK == 0 and seq % GDN_CHUNK == 0
    nc = seq // GDN_CHUNK
    bf = lambda a: a.astype(BF16)

    xf = x.reshape(t, d)
    memf = mem.reshape(-1, d)
    cos, sin = rope_tables(positions)

    for l in range(depth):
        (wq_m, wk_m, wv_m, wu_s, wq_g, wk_g, wv_g, wa_g, wb_g, wz_g, wg_a, wg_b, wg_c) = _in_proj_slices(w_in[l])
        g_mix = norm_mix[l]
        w_ab = jnp.pad(jnp.concatenate([wa_g, wb_g], axis=1), ((0, 0), (0, LANES - 2 * GDN_HEADS)))
        w32 = jnp.concatenate([wq_m, wk_m, wu_s, wq_g, wk_g, wv_g, w_ab], axis=1)
        w16 = jnp.concatenate([wv_m, wz_g, wg_a, wg_b, wg_c], axis=1)
        p32, p16 = norm_matmul2(xf, g_mix, bf(w32), bf(w16), F32, BF16, tm=2048, tna=640, tnb=1024)
        c_us = 2 * MOBA_WIDTH
        c_qkv = c_us + SSM_WIDTH
        c_ab = c_qkv + 3 * GDN_WIDTH
        u_s = p32[:, c_us:c_us + SSM_WIDTH]
        ab_g = p32[:, c_ab:c_ab + 2 * GDN_HEADS]

        y_a = moba_attention(p32, p16, cos, sin, batch, seq, q_col=0, k_col=MOBA_WIDTH // LANES, v_col=0)

        l_re, l_im, bb_re, bb_im = s5_discretise(ssm_a_re[l], ssm_a_im[l], ssm_log_dt[l], ssm_b_re[l], ssm_b_im[l])
        maps = s5_block_maps(bb_re, bb_im, ssm_c_re[l], ssm_c_im[l])
        u_tm = u_s.reshape(batch, seq, SSM_WIDTH).transpose(1, 0, 2).reshape(t, SSM_WIDTH)
        y_s_tm = s5_gelu(u_tm, maps, l_re, l_im, ssm_d[l])
        y_s = y_s_tm.reshape(seq, batch, SSM_WIDTH).transpose(1, 0, 2).reshape(t, SSM_WIDTH)

        def head_rows(cols):
            return cols.reshape(batch, nc, GDN_CHUNK, GDN_HEADS).transpose(0, 3, 1, 2)
        a_rows = head_rows(ab_g[:, :GDN_HEADS])
        b_rows = head_rows(ab_g[:, GDN_HEADS:2 * GDN_HEADS])
        y_c = gdn_mixer(p32, p16, a_rows, b_rows, gdn_conv[l], gdn_a_log[l], gdn_dt_bias[l], gdn_norm[l],
                        batch, seq, qkv_col=c_qkv // LANES, z_col=MOBA_WIDTH // GDN_WIDTH)

        k_x = norm_matmul(memf, norm_mem[l], bf(xa_wk[l]), BF16)
        v_x = norm_matmul(memf, norm_mem[l], bf(xa_wv[l]), BF16)
        xf = merge_cross_attention(xf, y_a, y_s, y_c, p16, bf(w_up_moba[l]), bf(ssm_w_glu[l]), bf(w_up_gdn[l]),
                                   bf(w_out[l]), norm_xa[l], bf(xa_wq[l]), k_x, v_x, bf(xa_wo[l]), batch, seq,
                                   gate_col=(MOBA_WIDTH + GDN_WIDTH) // d)

        if l % 2 == 0:
            i = l // 2
            xf = ffn(xf, norm_ffn[l], bf(ffn_w_gate[i]), bf(ffn_w_up[i]), bf(ffn_w_down[i]))
        else:
            i = l // 2
            route, h_moe = moe_router(xf, norm_ffn[l], moe_w_router[i])
            xf = moe_ffn(xf, h_moe, route, bf(moe_w_gate[i]), bf(moe_w_up[i]), bf(moe_w_down[i]),
                         final_gain=norm_final if l == depth - 1 else None)

    if depth % 2 == 1:
        xf = rmsnorm(xf, norm_final)
    return xf.reshape(batch, seq, d)
```

```python
import functools

import numpy as np
import jax
import jax.numpy as jnp
from jax import lax
from jax.experimental import pallas as pl
from jax.experimental.pallas import tpu as pltpu

F32 = jnp.float32
BF16 = jnp.bfloat16
HIGHEST = lax.Precision.HIGHEST

EPS = 1e-6
NEG_INF = -1e30
MOBA_HEADS = 8
MOBA_HEAD_DIM = 64
MOBA_WIDTH = MOBA_HEADS * MOBA_HEAD_DIM
MOBA_BLOCK = 256
MOBA_TOPK = 3
ROPE_THETA = 10000.0
SSM_WIDTH = 512
SSM_GROUP = 16
SSM_GROUPS = SSM_WIDTH // SSM_GROUP
SSM_STATE = 64
SSM_STATES = SSM_GROUPS * SSM_STATE
GDN_HEADS = 4
GDN_HEAD_DIM = 128
GDN_WIDTH = GDN_HEADS * GDN_HEAD_DIM
GDN_CONV = 4
GDN_CHUNK = 64
XA_HEADS = 4
XA_HEAD_DIM = 128
XA_WIDTH = XA_HEADS * XA_HEAD_DIM
N_EXPERTS = 8
TOP_K = 2

LANES = 128
SUBLANES = 8
VMEM_LIMIT = 56 * 1024 * 1024


def _cparams(*sem):
    return pltpu.CompilerParams(dimension_semantics=sem, vmem_limit_bytes=VMEM_LIMIT)


def _rms(x, g):
    return x * lax.rsqrt(jnp.mean(x * x, axis=-1, keepdims=True) + EPS) * g


def _sigmoid(x):
    return 1.0 / (1.0 + jnp.exp(-x))


def _silu(x):
    return x * _sigmoid(x)


def _dot(a, b):
    return jnp.dot(a, b, preferred_element_type=F32)


def _dot_nt(a, b):
    return lax.dot_general(a, b, (((1,), (1,)), ((), ())), preferred_element_type=F32)


def _norm_matmul_body(x_ref, g_ref, w_ref, o_ref, h_ref):
    @pl.when(pl.program_id(1) == 0)
    def _():
        h_ref[...] = _rms(x_ref[...], g_ref[...]).astype(BF16)

    o_ref[...] = _dot(h_ref[...], w_ref[...]).astype(o_ref.dtype)


def norm_matmul(x, g, w, out_dtype, tm=512, tn=512):
    t, k = x.shape
    n = w.shape[1]
    tm, tn = min(tm, t), min(tn, n)
    return pl.pallas_call(
        _norm_matmul_body,
        grid=(t // tm, n // tn),
        in_specs=[pl.BlockSpec((tm, k), lambda i, j: (i, 0)),
                  pl.BlockSpec((1, k), lambda i, j: (0, 0)),
                  pl.BlockSpec((k, tn), lambda i, j: (0, j))],
        out_specs=pl.BlockSpec((tm, tn), lambda i, j: (i, j)),
        out_shape=jax.ShapeDtypeStruct((t, n), out_dtype),
        scratch_shapes=[pltpu.VMEM((tm, k), BF16)],
        compiler_params=_cparams("parallel", "arbitrary"),
        name="norm_matmul",
    )(x, g.reshape(1, k), w)


def _norm_matmul2_body(x_ref, g_ref, wa_ref, wb_ref, oa_ref, ob_ref, h_ref, *, na):
    j = pl.program_id(1)

    @pl.when(j == 0)
    def _():
        h_ref[...] = _rms(x_ref[...], g_ref[...]).astype(BF16)

    @pl.when(j < na)
    def _():
        oa_ref[...] = _dot(h_ref[...], wa_ref[...]).astype(oa_ref.dtype)

    @pl.when(j >= na)
    def _():
        ob_ref[...] = _dot(h_ref[...], wb_ref[...]).astype(ob_ref.dtype)


def norm_matmul2(x, g, wa, wb, dtype_a, dtype_b, tm, tna, tnb):
    t, k = x.shape
    tm = min(tm, t)
    na, nb = wa.shape[1] // tna, wb.shape[1] // tnb
    return pl.pallas_call(
        functools.partial(_norm_matmul2_body, na=na),
        grid=(t // tm, na + nb),
        in_specs=[pl.BlockSpec((tm, k), lambda i, j: (i, 0)),
                  pl.BlockSpec((1, k), lambda i, j: (0, 0)),
                  pl.BlockSpec((k, tna), lambda i, j: (0, jnp.minimum(j, na - 1))),
                  pl.BlockSpec((k, tnb), lambda i, j: (0, jnp.maximum(j - na, 0)))],
        out_specs=[pl.BlockSpec((tm, tna), lambda i, j: (i, jnp.minimum(j, na - 1))),
                   pl.BlockSpec((tm, tnb), lambda i, j: (i, jnp.maximum(j - na, 0)))],
        out_shape=[jax.ShapeDtypeStruct((t, wa.shape[1]), dtype_a), jax.ShapeDtypeStruct((t, wb.shape[1]), dtype_b)],
        scratch_shapes=[pltpu.VMEM((tm, k), BF16)],
        compiler_params=_cparams("parallel", "arbitrary"),
        name="norm_matmul2",
    )(x, g.reshape(1, k), wa, wb)


def _rope_body(pos_ref, inv_ref, cos_ref, sin_ref):
    ang = pos_ref[...] * inv_ref[...]
    lane = lax.broadcasted_iota(jnp.int32, ang.shape, 1)
    first_half = (lane % MOBA_HEAD_DIM) < (MOBA_HEAD_DIM // 2)
    cos_ref[...] = jnp.cos(ang)
    s = jnp.sin(ang)
    sin_ref[...] = jnp.where(first_half, -s, s)


def rope_tables(positions):
    t = positions.size
    half = MOBA_HEAD_DIM // 2
    inv = (1.0 / (np.float32(ROPE_THETA) ** (np.arange(0, MOBA_HEAD_DIM, 2, dtype=np.float32)
                                             / np.float32(MOBA_HEAD_DIM)))).astype(np.float32)
    inv_row = jnp.asarray(np.tile(inv, LANES // half).reshape(1, LANES))
    pos = positions.astype(F32).reshape(t, 1)
    ts = min(1024, t)
    return pl.pallas_call(
        _rope_body,
        grid=(t // ts,),
        in_specs=[pl.BlockSpec((ts, 1), lambda i: (i, 0)),
                  pl.BlockSpec((1, LANES), lambda i: (0, 0))],
        out_specs=[pl.BlockSpec((ts, LANES), lambda i: (i, 0))] * 2,
        out_shape=[jax.ShapeDtypeStruct((t, LANES), F32)] * 2,
        compiler_params=_cparams("parallel"),
        name="rope_tables",
    )(pos, inv_row)


def _rope_apply(x, cos, sin_signed):
    lane = lax.broadcasted_iota(jnp.int32, x.shape, 1)
    first_half = (lane % MOBA_HEAD_DIM) < (MOBA_HEAD_DIM // 2)
    partner = jnp.where(first_half, pltpu.roll(x, LANES - MOBA_HEAD_DIM // 2, 1),
                        pltpu.roll(x, MOBA_HEAD_DIM // 2, 1))
    return x * cos + partner * sin_signed


def _moba_body(q_ref, k_ref, v_ref, cq_ref, sq_ref, ck_ref, sk_ref, o_ref, ka_ref, va_ref, km_ref, s_ref,
               mt_ref, acc_ref, *, nb):
    blk = MOBA_BLOCK
    i = pl.program_id(2)

    @pl.when(i == 0)
    def _():
        km_ref[...] = jnp.zeros_like(km_ref)
        lane_b = lax.broadcasted_iota(jnp.int32, (blk, LANES), 1)

        def rope_blk(j, c):
            rows = pl.ds(pl.multiple_of(j * blk, blk), blk)
            kr = _rope_apply(k_ref[rows, :], ck_ref[rows, :], sk_ref[rows, :])
            ka_ref[rows, 0:LANES] = kr.astype(BF16)
            ka_ref[rows, LANES:2 * LANES] = jnp.where(lane_b == j, 1.0, 0.0).astype(BF16)
            km_ref[pl.ds(j, 1), :] = jnp.mean(kr, axis=0, keepdims=True)
            v = v_ref[rows, :].astype(F32)
            va_ref[0, rows, :] = jnp.where(lane_b < MOBA_HEAD_DIM, v,
                                           jnp.where(lane_b == MOBA_HEAD_DIM, 1.0, 0.0)).astype(BF16)
            va_ref[1, rows, :] = jnp.where(lane_b >= MOBA_HEAD_DIM, v,
                                           jnp.where(lane_b == 0, 1.0, 0.0)).astype(BF16)
            return c
        lax.fori_loop(0, nb, rope_blk, 0)

    tq = 2 * blk
    lane = lax.broadcasted_iota(jnp.int32, (1, LANES), 1)
    head_a = lane < MOBA_HEAD_DIM
    q = _rope_apply(q_ref[...], cq_ref[...], sq_ref[...])
    scale = MOBA_HEAD_DIM ** -0.5
    km = km_ref[...]
    nbp = km_ref.shape[0]
    brow = lax.broadcasted_iota(jnp.int32, (nbp, tq), 0)
    qblk = 2 * i + jnp.where(lax.broadcasted_iota(jnp.int32, (1, tq), 1) >= blk, 1, 0)
    heads = (0, 1)
    q_heads = [jnp.where(head_a, q, 0.0), jnp.where(head_a, 0.0, q)]
    km_hi = km.astype(BF16)
    km_lo = (km - km_hi.astype(F32)).astype(BF16)
    gates = []
    for qh in q_heads:
        q_hi = qh.astype(BF16)
        q_lo = (qh - q_hi.astype(F32)).astype(BF16)
        gates.append(_dot_nt(km_hi, q_hi) + (_dot_nt(km_hi, q_lo) + _dot_nt(km_lo, q_hi)))
    q_augs = []
    brow_f = brow.astype(F32)
    for h in heads:
        gate = jnp.where(brow < qblk, gates[h], NEG_INF)
        picked = jnp.zeros((nbp, tq), F32)
        for _ in range(MOBA_TOPK):
            top = jnp.max(gate, axis=0, keepdims=True)
            first = jnp.min(jnp.where(gate == top, brow_f, float(nbp)), axis=0, keepdims=True)
            hit = brow_f == first
            picked = jnp.where(hit, 1.0, picked)
            gate = jnp.where(hit, -jnp.inf, gate)
        keep = ((picked > 0.5) & (brow < qblk)) | (brow == qblk)
        bias_t = jnp.concatenate([jnp.where(keep, 0.0, NEG_INF), jnp.zeros((LANES - nbp, tq), F32)],
                                 axis=0)
        q_augs.append(jnp.concatenate([q_heads[h] * scale, bias_t.T], axis=1).astype(BF16))

    def pair_rows(jp):
        return pl.ds(pl.multiple_of(jp * tq, tq), tq)

    r_idx = lax.broadcasted_iota(jnp.int32, (tq, tq), 0)
    c_idx = lax.broadcasted_iota(jnp.int32, (tq, tq), 1)
    k_diag = ka_ref[pair_rows(i), :]
    s_diag = [_dot_nt(q_augs[h], k_diag) for h in heads]
    for h in heads:
        s_d = jnp.where(c_idx <= r_idx, s_diag[h], NEG_INF)
        s_ref[h, i] = s_d
        m_t = s_d[:, 0:LANES]
        for col in range(LANES, tq, LANES):
            m_t = jnp.maximum(m_t, s_d[:, col:col + LANES])
        mt_ref[h] = m_t

    def loop2(n, body):
        def two(k, c):
            body([2 * k, 2 * k + 1])
            return c
        lax.fori_loop(0, lax.shift_right_logical(n, 1), two, 0)

        @pl.when((n & 1) == 1)
        def _():
            body([n - 1])

    def scores(jps):
        k2s = [ka_ref[pair_rows(jp), :] for jp in jps]
        s2s = [[_dot_nt(q_augs[h], k2) for h in heads] for k2 in k2s]
        for jp, s2h in zip(jps, s2s):
            for h in heads:
                s_ref[h, jp] = s2h[h]
        for h in heads:
            m_t = mt_ref[h]
            for s2h in s2s:
                for col in range(0, tq, LANES):
                    m_t = jnp.maximum(m_t, s2h[h][:, col:col + LANES])
            mt_ref[h] = m_t
    loop2(i, scores)

    ms = [jnp.max(mt_ref[h], axis=-1, keepdims=True) for h in heads]
    for h in heads:
        acc_ref[h] = jnp.zeros((tq, LANES), F32)

    def values(jps):
        ps = [[jnp.exp(s_ref[h, jp] - ms[h]).astype(BF16) for h in heads] for jp in jps]
        pvs = [[_dot(p[h], va_ref[h, pair_rows(jp), :]) for h in heads] for jp, p in zip(jps, ps)]
        for h in heads:
            tot = pvs[0][h]
            for pv in pvs[1:]:
                tot = tot + pv[h]
            acc_ref[h] += tot
    loop2(i + 1, values)
    acc_a, acc_b = acc_ref[0], acc_ref[1]
    out_a = acc_a / acc_a[:, MOBA_HEAD_DIM:MOBA_HEAD_DIM + 1]
    out_b = acc_b / acc_b[:, 0:1]
    o_ref[...] = jnp.where(head_a, out_a, out_b).astype(o_ref.dtype)


def moba_attention(qk, v, cos, sin, batch, seq, q_col=0, k_col=MOBA_WIDTH // LANES, v_col=0):
    nb = seq // MOBA_BLOCK
    assert nb % 2 == 0
    npair = nb // 2
    pairs = MOBA_WIDTH // LANES
    tq = 2 * MOBA_BLOCK
    t = batch * seq
    return pl.pallas_call(
        functools.partial(_moba_body, nb=nb),
        grid=(batch, pairs, npair),
        in_specs=[pl.BlockSpec((tq, LANES), lambda b, p, i: (b * npair + i, q_col + p)),
                  pl.BlockSpec((seq, LANES), lambda b, p, i: (b, k_col + p)),
                  pl.BlockSpec((seq, LANES), lambda b, p, i: (b, v_col + p)),
                  pl.BlockSpec((tq, LANES), lambda b, p, i: (b * npair + i, 0)),
                  pl.BlockSpec((tq, LANES), lambda b, p, i: (b * npair + i, 0)),
                  pl.BlockSpec((seq, LANES), lambda b, p, i: (b, 0)),
                  pl.BlockSpec((seq, LANES), lambda b, p, i: (b, 0))],
        out_specs=pl.BlockSpec((tq, LANES), lambda b, p, i: (b * npair + i, p)),
        out_shape=jax.ShapeDtypeStruct((t, MOBA_WIDTH), BF16),
        scratch_shapes=[pltpu.VMEM((seq, 2 * LANES), BF16),
                        pltpu.VMEM((2, seq, LANES), BF16),
                        pltpu.VMEM((-(-nb // SUBLANES) * SUBLANES, LANES), F32),
                        pltpu.VMEM((2, npair, tq, tq), F32),
                        pltpu.VMEM((2, tq, LANES), F32),
                        pltpu.VMEM((2, tq, LANES), F32)],
        compiler_params=_cparams("parallel", "parallel", "arbitrary"),
        name="moba_attention",
    )(qk, qk, v, cos, sin, cos, sin)


def _s5_disc_body(are_ref, aim_ref, ldt_ref, bre_ref, bim_ref, lre_ref, lim_ref, bbre_ref, bbim_ref):
    a_re, a_im = are_ref[...], aim_ref[...]
    dt = jnp.exp(ldt_ref[...])
    mag = jnp.exp(a_re * dt)
    l_re = mag * jnp.cos(a_im * dt)
    l_im = mag * jnp.sin(a_im * dt)
    lre_ref[...] = l_re
    lim_ref[...] = l_im
    x, y = l_re - 1.0, l_im
    den = a_re * a_re + a_im * a_im
    c_re = (x * a_re + y * a_im) / den
    c_im = (y * a_re - x * a_im) / den
    b_re, b_im = bre_ref[...], bim_ref[...]
    bbre_ref[...] = c_re * b_re - c_im * b_im
    bbim_ref[...] = c_re * b_im + c_im * b_re


def s5_discretise(a_re, a_im, log_dt, b_re, b_im):
    g, p = a_re.shape
    n = b_re.shape[-1]
    col = lambda a: a.reshape(g * p, 1)
    ldt = jnp.broadcast_to(log_dt[:, None], (g, p))
    outs = pl.pallas_call(
        _s5_disc_body,
        out_shape=[jax.ShapeDtypeStruct((g * p, 1), F32)] * 2 + [jax.ShapeDtypeStruct((g * p, n), F32)] * 2,
        name="s5_discretise",
    )(col(a_re), col(a_im), col(ldt), b_re.reshape(g * p, n), b_im.reshape(g * p, n))
    l_re, l_im, bb_re, bb_im = outs
    return l_re.reshape(g, p), l_im.reshape(g, p), bb_re.reshape(g, p, n), bb_im.reshape(g, p, n)


def _block_diag(blocks):
    g, r, c = blocks.shape
    eye = jnp.eye(g, dtype=blocks.dtype)
    return (blocks[:, :, None, :] * eye[:, None, :, None]).reshape(g * r, g * c)


def _s5_body(u_ref, bre_ref, bim_ref, cre_ref, cim_ref, lre_ref, lim_ref, d_ref, y_ref, h_ref, hb_ref, *, steps):
    rows = SUBLANES
    ns = SSM_STATES
    nq = SSM_WIDTH // LANES
    sq = ns // nq

    @pl.when(pl.program_id(0) == 0)
    def _():
        h_ref[...] = jnp.zeros_like(h_ref)

    u = u_ref[...]
    u16 = u.astype(BF16)
    for q in range(nq):
        uq = u16[:, q * LANES:(q + 1) * LANES]
        hb_ref[:, q * sq:(q + 1) * sq] = _dot(uq, bre_ref[q])
        hb_ref[:, ns + q * sq:ns + (q + 1) * sq] = _dot(uq, bim_ref[q])
    l_re, l_im = lre_ref[...], lim_ref[...]

    def step(t, carry):
        h_re, h_im = carry
        r = pl.ds(pl.multiple_of(t * rows, rows), rows)
        n_re = l_re * h_re - l_im * h_im + hb_ref[r, 0:ns]
        n_im = l_re * h_im + l_im * h_re + hb_ref[r, ns:2 * ns]
        hb_ref[r, 0:ns] = n_re
        hb_ref[r, ns:2 * ns] = n_im
        return n_re, n_im

    h_re, h_im = lax.fori_loop(0, steps, step, (h_ref[:, 0:ns], h_ref[:, ns:2 * ns]), unroll=4)
    h_ref[:, 0:ns] = h_re
    h_ref[:, ns:2 * ns] = h_im
    ys = []
    for q in range(nq):
        ys.append(_dot(hb_ref[:, q * sq:(q + 1) * sq].astype(BF16), cre_ref[q])
                  + _dot(hb_ref[:, ns + q * sq:ns + (q + 1) * sq].astype(BF16), cim_ref[q]))
    y = jnp.concatenate(ys, axis=1) + d_ref[...] * u
    y_ref[...] = jax.nn.gelu(y).astype(y_ref.dtype)


def s5_block_maps(bb_re, bb_im, c_re, c_im):
    nq = SSM_WIDTH // LANES
    gq = SSM_GROUPS // nq
    per_q = lambda a: jnp.stack([_block_diag(a[q * gq:(q + 1) * gq].transpose(0, 2, 1)) for q in range(nq)])
    return (per_q(bb_re).astype(BF16), per_q(bb_im).astype(BF16),
            per_q(c_re).astype(BF16), per_q(-c_im).astype(BF16))


def s5_gelu(u_tm, maps, l_re, l_im, d_skip, steps=64):
    rows = u_tm.shape[0]
    blk = steps * SUBLANES
    ns2 = 2 * SSM_STATES
    full3 = lambda a: pl.BlockSpec(a.shape, lambda c: (0, 0, 0))
    return pl.pallas_call(
        functools.partial(_s5_body, steps=steps),
        grid=(rows // blk,),
        in_specs=[pl.BlockSpec((blk, SSM_WIDTH), lambda c: (c, 0)),
                  full3(maps[0]), full3(maps[1]), full3(maps[2]), full3(maps[3]),
                  pl.BlockSpec((1, SSM_STATES), lambda c: (0, 0)),
                  pl.BlockSpec((1, SSM_STATES), lambda c: (0, 0)),
                  pl.BlockSpec((1, SSM_WIDTH), lambda c: (0, 0))],
        out_specs=pl.BlockSpec((blk, SSM_WIDTH), lambda c: (c, 0)),
        out_shape=jax.ShapeDtypeStruct((rows, SSM_WIDTH), BF16),
        scratch_shapes=[pltpu.VMEM((SUBLANES, ns2), F32),
                        pltpu.VMEM((blk, ns2), F32)],
        compiler_params=_cparams("arbitrary"),
        name="s5_scan",
    )(u_tm, *maps, l_re.reshape(1, -1), l_im.reshape(1, -1), d_skip.reshape(1, -1))


GDN_GROUP = 16


def _gdn_prep_body(q_ref, k_ref, v_ref, a_ref, b_ref, wq_ref, wk_ref, wv_ref, alog_ref, dtb_ref,
                   u_ref, w_ref, qd_ref, kd_ref, qk_ref, gl_ref, qs_ref, ks_ref, vs_ref,
                   *, seq):
    L = GDN_CHUNK
    G = min(GDN_GROUP, seq // L)
    dk = GDN_HEAD_DIM
    nc = seq // L
    hd = pl.program_id(1)
    cb = 256
    halo = SUBLANES

    def conv_blk(rb, c):
        base = pl.multiple_of(rb * cb, cb)
        prev = pl.multiple_of(jnp.maximum(base - halo, 0), halo)
        for src, wref, dst, norm, mul in ((q_ref, wq_ref, qs_ref, True, dk ** -0.5),
                                          (k_ref, wk_ref, ks_ref, True, 1.0),
                                          (v_ref, wv_ref, vs_ref, False, 1.0)):
            cur = src[pl.ds(base, cb), :]
            head = jnp.where(rb > 0, src[pl.ds(prev, halo), :], 0.0)
            ext = jnp.concatenate([head, cur], axis=0)
            wts = wref[...]
            y = ext[halo:halo + cb] * wts[GDN_CONV - 1:GDN_CONV]
            for tap in range(1, GDN_CONV):
                y = y + ext[halo - tap:halo - tap + cb] * wts[GDN_CONV - 1 - tap:GDN_CONV - tap]
            y = _silu(y)
            if norm:
                y = y * lax.rsqrt(jnp.sum(y * y, axis=-1, keepdims=True) + EPS) * mul
            dst[pl.ds(base, cb), :] = y
        return c
    lax.fori_loop(0, seq // cb, conv_blk, 0)

    gl_rows = G * L
    ri = lax.broadcasted_iota(jnp.int32, (gl_rows, L), 0) % L
    ci = lax.broadcasted_iota(jnp.int32, (gl_rows, L), 1)
    tril = ci <= ri
    strict = ci < ri
    eye = ci == ri
    ri3 = lax.broadcasted_iota(jnp.int32, (G, L, L), 1)
    ci3 = lax.broadcasted_iota(jnp.int32, (G, L, L), 2)
    strict3 = ci3 < ri3
    eye_f = jnp.where(ci3 == ri3, 1.0, 0.0)
    r2 = lax.broadcasted_iota(jnp.int32, (L, L), 0)
    c2 = lax.broadcasted_iota(jnp.int32, (L, L), 1)
    upper_f = jnp.where(r2 <= c2, 1.0, 0.0)
    a_coef = -jnp.exp(jnp.full((1, L), alog_ref[hd], F32))
    dt_bias = dtb_ref[hd]
    hp = functools.partial(jnp.dot, preferred_element_type=F32, precision=HIGHEST)

    def bmm(a, b):
        return jnp.einsum('gij,gjk->gik', a, b, preferred_element_type=F32)

    def bmm_nt(a, b):
        return jnp.einsum('gid,gjd->gij', a, b, preferred_element_type=F32)

    def to_col(rows_b):
        return jnp.sum(jnp.where(eye, rows_b, 0.0), axis=1, keepdims=True)

    def prep(cg, carry):
        c0 = pl.multiple_of(cg * G, G)
        rows = pl.ds(pl.multiple_of(cg * gl_rows, gl_rows), gl_rows)
        qc, kc, vc = qs_ref[rows, :], ks_ref[rows, :], vs_ref[rows, :]
        z = a_ref[pl.ds(c0, G), :] + dt_bias
        softplus = jnp.maximum(z, 0.0) + jnp.log(1.0 + jnp.exp(-jnp.abs(z)))
        g_rows = hp(a_coef * softplus, upper_f)
        beta_rows = _sigmoid(b_ref[pl.ds(c0, G), :])
        spread = lambda r: jnp.concatenate([jnp.broadcast_to(r[c:c + 1, :], (L, L)) for c in range(G)], axis=0)
        g_rb = spread(g_rows)
        g_col = to_col(g_rb)
        beta_col = to_col(spread(beta_rows))
        g_last = g_rb[:, L - 1:L]
        decay = jnp.where(tril, jnp.exp(jnp.where(tril, g_col - g_rb, 0.0)), 0.0)
        k_beta = kc * beta_col
        v_beta = vc * beta_col
        to3 = lambda a: a.reshape(G, L, a.shape[-1])
        kb16, k16, q16 = to3(k_beta.astype(BF16)), to3(kc.astype(BF16)), to3(qc.astype(BF16))
        kk_qk = bmm_nt(jnp.concatenate([kb16, q16], axis=1), k16)
        decay3 = to3(decay)
        a_low = jnp.where(strict3, kk_qk[:, :L] * decay3, 0.0)
        qk = kk_qk[:, L:] * decay3
        t_inv = eye_f - a_low
        a16 = a_low.astype(BF16)
        pw = bmm(a16, a16)
        span = 2
        while 2 * span < L:
            pw16 = pw.astype(BF16)
            both = bmm(jnp.concatenate([t_inv.astype(BF16), pw16], axis=1), pw16)
            t_inv = t_inv + both[:, :L]
            pw = both[:, L:]
            span *= 2
        t_inv = t_inv + bmm(t_inv.astype(BF16), pw.astype(BF16))
        rhs = jnp.concatenate([v_beta.astype(BF16), (k_beta * jnp.exp(g_col)).astype(BF16)], axis=1)
        uw = bmm(t_inv.astype(BF16), to3(rhs)).reshape(gl_rows, 2 * dk)
        u_ref[rows, :] = uw[:, :dk]
        w_ref[rows, :] = uw[:, dk:].astype(BF16)
        qd_ref[rows, :] = (qc * jnp.exp(g_col)).astype(BF16)
        k_dec = to3(kc * jnp.exp(g_last - g_col))
        kd_ref[pl.ds(c0, G)] = jnp.swapaxes(k_dec, 1, 2).astype(BF16)
        qk_ref[pl.ds(c0, G)] = qk.astype(BF16)
        gl_ref[pl.ds(c0, G), :] = jnp.broadcast_to(jnp.exp(g_rows[:, L - 1:L]), (G, LANES))
        return carry
    lax.fori_loop(0, nc // G, prep, 0)


def _gdn_scan_body(u_ref, w_ref, qd_ref, kd_ref, qk_ref, gl_ref, z_ref, gn_ref, o_ref, *st_refs, chunks):
    L = GDN_CHUNK
    d = GDN_HEAD_DIM
    nrow = u_ref.shape[0]

    @pl.when(pl.program_id(1) == 0)
    def _():
        for st in st_refs:
            st[...] = jnp.zeros_like(st)

    gn = gn_ref[...]

    def chunk(c, carry):
        rows = pl.ds(pl.multiple_of(c * L, L), L)
        chains = [(r, h) for r in range(nrow) for h in range(GDN_HEADS)]
        states = [st_refs[r * GDN_HEADS + h][...] for r, h in chains]
        firsts = []
        for (r, h), s in zip(chains, states):
            cols = slice(h * d, (h + 1) * d)
            firsts.append(_dot(jnp.concatenate([w_ref[r, rows, cols], qd_ref[r, rows, cols]], axis=0),
                               s.astype(BF16)))
        vns = []
        for (r, h), both in zip(chains, firsts):
            cols = slice(h * d, (h + 1) * d)
            vns.append((u_ref[r, rows, cols] - both[:L]).astype(BF16))
        outs = [both[L:] + _dot(qk_ref[r, h, c], vn) for (r, h), both, vn in zip(chains, firsts, vns)]
        for (r, h), s, vn in zip(chains, states, vns):
            st_refs[r * GDN_HEADS + h][...] = s * gl_ref[r, h, pl.ds(c, 1), :] + _dot(kd_ref[r, h, c], vn)
        for (r, h), o in zip(chains, outs):
            cols = slice(h * d, (h + 1) * d)
            o_ref[r, rows, cols] = (_rms(o, gn) * _silu(z_ref[r, rows, cols].astype(F32))).astype(o_ref.dtype)
        return carry
    lax.fori_loop(0, chunks, chunk, 0)


def gdn_mixer(qkv, z, a_rows, b_rows, conv_w, a_log, dt_bias, gnorm, batch, seq, ts=1024, qkv_col=0, z_col=0):
    t = batch * seq
    hds = GDN_HEADS
    nc = seq // GDN_CHUNK
    L = GDN_CHUNK
    d = GDN_HEAD_DIM
    seq_spec = lambda off: pl.BlockSpec((seq, d), lambda b, h: (b, qkv_col + off + h))
    w_spec = lambda off: pl.BlockSpec((GDN_CONV, d), lambda b, h: (0, off + h))
    row_spec = pl.BlockSpec((None, None, nc, L), lambda b, h: (b, h, 0, 0))
    smem = pl.BlockSpec(memory_space=pltpu.SMEM)
    head_out = pl.BlockSpec((seq, d), lambda b, h: (b, h))
    u, w, qd, kd, qk, gl = pl.pallas_call(
        functools.partial(_gdn_prep_body, seq=seq),
        grid=(batch, hds),
        in_specs=[seq_spec(0), seq_spec(hds), seq_spec(2 * hds), row_spec, row_spec,
                  w_spec(0), w_spec(hds), w_spec(2 * hds), smem, smem],
        out_specs=[head_out, head_out, head_out,
                   pl.BlockSpec((None, None, nc, d, L), lambda b, h: (b, h, 0, 0, 0)),
                   pl.BlockSpec((None, None, nc, L, L), lambda b, h: (b, h, 0, 0, 0)),
                   pl.BlockSpec((None, None, nc, LANES), lambda b, h: (b, h, 0, 0))],
        out_shape=[jax.ShapeDtypeStruct((t, GDN_WIDTH), F32),
                   jax.ShapeDtypeStruct((t, GDN_WIDTH), BF16),
                   jax.ShapeDtypeStruct((t, GDN_WIDTH), BF16),
                   jax.ShapeDtypeStruct((batch, hds, nc, d, L), BF16),
                   jax.ShapeDtypeStruct((batch, hds, nc, L, L), BF16),
                   jax.ShapeDtypeStruct((batch, hds, nc, LANES), F32)],
        scratch_shapes=[pltpu.VMEM((seq, d), F32), pltpu.VMEM((seq, d), F32), pltpu.VMEM((seq, d), F32)],
        compiler_params=_cparams("parallel", "parallel"),
        name="gdn_prep",
    )(qkv, qkv, qkv, a_rows, b_rows, conv_w, conv_w, conv_w, a_log, dt_bias)

    ts = min(ts, seq)
    spb = seq // ts
    cps = ts // L
    nrow = 2 if batch % 2 == 0 else 1
    by_row = lambda a: a.reshape(batch, seq, a.shape[-1])
    tok = pl.BlockSpec((nrow, ts, GDN_WIDTH), lambda b, s: (b, s, 0))
    out = pl.pallas_call(
        functools.partial(_gdn_scan_body, chunks=cps),
        grid=(batch // nrow, spb),
        in_specs=[tok, tok, tok,
                  pl.BlockSpec((nrow, hds, cps, d, L), lambda b, s: (b, 0, s, 0, 0)),
                  pl.BlockSpec((nrow, hds, cps, L, L), lambda b, s: (b, 0, s, 0, 0)),
                  pl.BlockSpec((nrow, hds, cps, LANES), lambda b, s: (b, 0, s, 0)),
                  pl.BlockSpec((nrow, ts, GDN_WIDTH), lambda b, s: (b, s, z_col)),
                  pl.BlockSpec((1, d), lambda b, s: (0, 0))],
        out_specs=tok,
        out_shape=jax.ShapeDtypeStruct((batch, seq, GDN_WIDTH), BF16),
        scratch_shapes=[pltpu.VMEM((d, d), F32)] * (nrow * hds),
        compiler_params=_cparams("parallel", "arbitrary"),
        name="gdn_scan",
    )(by_row(u), by_row(w), by_row(qd), kd, qk, gl, by_row(z), gnorm.reshape(1, d))
    return out.reshape(t, GDN_WIDTH)


def _merge_xattn_body(x_ref, ya_ref, ys_ref, yc_ref, ga_ref, gb_ref, gc_ref, wa_ref, wg_ref, wc_ref, wo_ref,
                      gx_ref, wq_ref, k_ref, v_ref, wxo_ref, o_ref):
    d = x_ref.shape[1]
    y_a = _dot(ya_ref[...], wa_ref[...])
    glu = _dot(ys_ref[...], wg_ref[...])
    y_b = glu[:, :d] * _sigmoid(glu[:, d:])
    y_c = _dot(yc_ref[...], wc_ref[...])
    gate = lambda r: _sigmoid(r[...].astype(F32))
    merged = gate(ga_ref) * y_a + gate(gb_ref) * y_b + gate(gc_ref) * y_c
    x1 = x_ref[...] + _dot(merged.astype(BF16), wo_ref[...])

    h = _rms(x1, gx_ref[...]).astype(BF16)
    q = _dot(h, wq_ref[...]).astype(BF16)
    cols = [slice(hd * XA_HEAD_DIM, (hd + 1) * XA_HEAD_DIM) for hd in range(XA_HEADS)]
    scores = [_dot_nt(q[:, c], k_ref[:, c]) * (XA_HEAD_DIM ** -0.5) for c in cols]
    probs = []
    for s in scores:
        p = jnp.exp(s - jnp.max(s, axis=-1, keepdims=True))
        probs.append((p / jnp.sum(p, axis=-1, keepdims=True)).astype(BF16))
    outs = [_dot(p, v_ref[:, c]) for p, c in zip(probs, cols)]
    o = jnp.concatenate(outs, axis=1).astype(BF16)
    o_ref[...] = x1 + _dot(o, wxo_ref[...])


def merge_cross_attention(x, y_a, y_s, y_c, gates, w_a, w_glu, w_c, w_o, g_xa, wq, k, v, wo_xa, batch, seq,
                          tm=512, gate_col=0):
    t, d = x.shape
    m = k.shape[0] // batch
    spb = seq // tm
    row = lambda n: pl.BlockSpec((tm, n), lambda i: (i, 0))
    gate = lambda j: pl.BlockSpec((tm, d), lambda i: (i, gate_col + j))
    full = lambda a: pl.BlockSpec(a.shape, lambda i: (0, 0))
    mem = pl.BlockSpec((m, XA_WIDTH), lambda i: (i // spb, 0))
    return pl.pallas_call(
        _merge_xattn_body,
        grid=(t // tm,),
        in_specs=[row(d), row(y_a.shape[1]), row(y_s.shape[1]), row(y_c.shape[1]), gate(0), gate(1), gate(2),
                  full(w_a), full(w_glu), full(w_c), full(w_o),
                  pl.BlockSpec((1, d), lambda i: (0, 0)), full(wq), mem, mem, full(wo_xa)],
        out_specs=row(d),
        out_shape=jax.ShapeDtypeStruct((t, d), F32),
        compiler_params=_cparams("parallel"),
        name="merge_cross_attention",
    )(x, y_a, y_s, y_c, gates, gates, gates, w_a, w_glu, w_c, w_o, g_xa.reshape(1, d), wq, k, v, wo_xa)


def _ffn_body(x_ref, g_ref, wg_ref, wu_ref, wd_ref, o_ref, h_ref, acc_ref):
    f = pl.program_id(1)

    @pl.when(f == 0)
    def _():
        h_ref[...] = _rms(x_ref[...], g_ref[...]).astype(BF16)
        acc_ref[...] = jnp.zeros_like(acc_ref)

    h = h_ref[...]
    act = (_silu(_dot(h, wg_ref[...])) * _dot(h, wu_ref[...])).astype(BF16)
    acc_ref[...] += _dot(act, wd_ref[...])

    @pl.when(f == pl.num_programs(1) - 1)
    def _():
        o_ref[...] = x_ref[...] + acc_ref[...]


def ffn(x, g, w_gate, w_up, w_down, tm=512, tf=1408):
    t, d = x.shape
    f = w_gate.shape[1]
    return pl.pallas_call(
        _ffn_body,
        grid=(t // tm, f // tf),
        in_specs=[pl.BlockSpec((tm, d), lambda i, k: (i, 0)),
                  pl.BlockSpec((1, d), lambda i, k: (0, 0)),
                  pl.BlockSpec((d, tf), lambda i, k: (0, k)),
                  pl.BlockSpec((d, tf), lambda i, k: (0, k)),
                  pl.BlockSpec((tf, d), lambda i, k: (k, 0))],
        out_specs=pl.BlockSpec((tm, d), lambda i, k: (i, 0)),
        out_shape=jax.ShapeDtypeStruct((t, d), F32),
        scratch_shapes=[pltpu.VMEM((tm, d), BF16), pltpu.VMEM((tm, d), F32)],
        compiler_params=_cparams("parallel", "arbitrary"),
        name="ffn",
    )(x, g.reshape(1, d), w_gate, w_up, w_down)


def _router_body(x_ref, g_ref, whi_ref, wlo_ref, r_ref, h_ref):
    h = _rms(x_ref[...], g_ref[...])
    h_hi = h.astype(BF16)
    h_ref[...] = h_hi
    h_lo = (h - h_hi.astype(F32)).astype(BF16)
    logits = _dot(h_hi, whi_ref[...]) + (_dot(h_hi, wlo_ref[...]) + _dot(h_lo, whi_ref[...]))
    lane = lax.broadcasted_iota(jnp.int32, logits.shape, 1).astype(F32)
    logits = jnp.where(lane < N_EXPERTS, logits, -jnp.inf)
    v1 = jnp.max(logits, axis=-1, keepdims=True)
    i1 = jnp.min(jnp.where(logits == v1, lane, float(LANES)), axis=-1, keepdims=True)
    rest = jnp.where(lane == i1, -jnp.inf, logits)
    v2 = jnp.max(rest, axis=-1, keepdims=True)
    i2 = jnp.min(jnp.where(rest == v2, lane, float(LANES)), axis=-1, keepdims=True)
    e2 = jnp.exp(v2 - v1)
    w1 = 1.0 / (1.0 + e2)
    w2 = e2 / (1.0 + e2)
    r_ref[...] = (jnp.where(lane == 0.0, i1, 0.0) + jnp.where(lane == 1.0, i2, 0.0)
                  + jnp.where(lane == 2.0, w1, 0.0) + jnp.where(lane == 3.0, w2, 0.0))


def moe_router(x, g, w_router, tm=512):
    t, d = x.shape
    w = jnp.pad(w_router, ((0, 0), (0, LANES - w_router.shape[1])))
    w_hi = w.astype(BF16)
    w_lo = (w - w_hi.astype(F32)).astype(BF16)
    return pl.pallas_call(
        _router_body,
        grid=(t // tm,),
        in_specs=[pl.BlockSpec((tm, d), lambda i: (i, 0)),
                  pl.BlockSpec((1, d), lambda i: (0, 0)),
                  pl.BlockSpec((d, LANES), lambda i: (0, 0)),
                  pl.BlockSpec((d, LANES), lambda i: (0, 0))],
        out_specs=[pl.BlockSpec((tm, LANES), lambda i: (i, 0)), pl.BlockSpec((tm, d), lambda i: (i, 0))],
        out_shape=[jax.ShapeDtypeStruct((t, LANES), F32), jax.ShapeDtypeStruct((t, d), BF16)],
        compiler_params=_cparams("parallel"),
        name="moe_router",
    )(x, g.reshape(1, d), w_hi, w_lo)


MOE_TILE = 512
MOE_GTILE = 256
MOE_GCHUNK = 1024
MOE_CHUNK = 512
MOE_RBLK = 256


def _count_le(sorted_vals, queries):
    return jnp.sum((sorted_vals[None, :] <= queries[:, None]).astype(jnp.int32), axis=1)


def _moe_plan(route, t):
    e = N_EXPERTS
    tm, ck, rb = MOE_TILE, MOE_CHUNK, MOE_RBLK
    n_tiles = (TOP_K * t) // tm + e
    p = n_tiles * tm
    n_chunks = t // ck
    i32 = jnp.int32
    e1 = route[:, 0].astype(i32)
    e2 = route[:, 1].astype(i32)
    oh1 = jax.nn.one_hot(e1, e, dtype=i32)
    oh2 = jax.nn.one_hot(e2, e, dtype=i32)
    cnt = oh1 + oh2
    incl = jnp.cumsum(cnt, axis=0)
    excl = incl - cnt
    n_e = incl[-1]
    g_e = ((n_e + tm - 1) // tm) * tm
    off_end = jnp.cumsum(g_e)
    off = off_end - g_e
    pos_all = off[None, :] + excl
    pos1 = jnp.sum(pos_all * oh1, axis=1)
    pos2 = jnp.sum(pos_all * oh2, axis=1)
    tile_start = jnp.arange(n_tiles, dtype=i32) * tm
    tile_expert = jnp.minimum(_count_le(off_end, tile_start), e - 1)
    tile_sel = jax.nn.one_hot(tile_expert, e, dtype=i32)
    tile_used = (tile_start < jnp.sum(tile_sel * (off + n_e)[None, :], axis=1)).astype(i32)
    gm = MOE_GTILE
    n_gt = p // gm
    gt_start = jnp.arange(n_gt, dtype=i32) * gm
    sel = jax.nn.one_hot(jnp.minimum(_count_le(off_end, gt_start), e - 1), e, dtype=i32)
    r0 = gt_start - jnp.sum(sel * off[None, :], axis=1)
    n_sel = jnp.sum(sel * n_e[None, :], axis=1)
    has_rows = r0 < n_sel
    r1 = jnp.minimum(r0 + gm, n_sel) - 1
    gk = MOE_GCHUNK
    n_gchunks = t // gk
    g_ends = incl[gk - 1::gk]
    ends_sel = jnp.sum(g_ends[None, :, :] * sel[:, None, :], axis=2)
    c_lo = jnp.where(has_rows, jnp.sum((ends_sel <= r0[:, None]).astype(i32), axis=1), 0)
    c_hi = jnp.where(has_rows, jnp.sum((ends_sel <= r1[:, None]).astype(i32), axis=1), 0)
    n_it = c_hi - c_lo + 1
    it_end = jnp.cumsum(n_it)
    it_start = it_end - n_it
    wmax = n_gt + e * n_gchunks
    w_idx = jnp.arange(wmax, dtype=i32)
    g_tile = jnp.minimum(_count_le(it_end, w_idx), n_gt - 1)
    g_valid = w_idx < it_end[-1]
    g_chunk = jnp.where(g_valid, c_lo[g_tile] + (w_idx - it_start[g_tile]), c_hi[n_gt - 1])
    g_first = (g_valid & (w_idx == it_start[g_tile])).astype(i32)
    g_last = (g_valid & (w_idx == it_end[g_tile] - 1)).astype(i32)
    gather = (g_tile, g_chunk.astype(i32), g_first, g_last, g_valid.astype(i32))
    lo = off[None, :] + excl[::ck]
    hi = off[None, :] + incl[ck - 1::ck]
    nblk = jnp.where(hi > lo, (hi - 1) // rb - lo // rb + 1, 0).reshape(-1)
    blk_lo = (lo // rb).reshape(-1)
    cb_end = jnp.cumsum(nblk)
    cb_start = cb_end - nblk
    cmax = p // rb + e * n_chunks
    c_idx = jnp.arange(cmax, dtype=i32)
    pair = jnp.minimum(_count_le(cb_end, c_idx), nblk.shape[0] - 1)
    c_valid = c_idx < cb_end[-1]
    c_tile = jnp.where(c_valid, pair // e, n_chunks - 1)
    c_blk = jnp.where(c_valid, blk_lo[pair] + (c_idx - cb_start[pair]), 0)
    per_tile = jnp.sum(nblk.reshape(n_chunks, e), axis=1)
    t_end = jnp.cumsum(per_tile)
    c_first = (c_valid & (c_idx == (t_end - per_tile)[c_tile])).astype(i32)
    c_last = (c_valid & (c_idx == t_end[c_tile] - 1)).astype(i32)
    combine = (c_tile.astype(i32), c_blk.astype(i32), c_first, c_last, c_valid.astype(i32))
    return dict(n_tiles=n_tiles, p=p, tile_expert=tile_expert.astype(i32), tile_used=tile_used, pos1=pos1, pos2=pos2,
                gather=gather, combine=combine)


def _moe_gather_body(tile_ref, chunk_ref, first_ref, last_ref, valid_ref, meta_ref, h_ref,
                     o_ref, wr_ref, acc_ref, wacc_ref):
    w = pl.program_id(0)

    @pl.when(first_ref[w] == 1)
    def _():
        acc_ref[...] = jnp.zeros_like(acc_ref)
        wacc_ref[...] = jnp.zeros_like(wacc_ref)

    @pl.when(valid_ref[w] == 1)
    def _():
        row = (lax.broadcasted_iota(jnp.int32, (MOE_GTILE, MOE_GCHUNK), 0) + tile_ref[w] * MOE_GTILE).astype(F32)
        hit1 = row == meta_ref[0:1, :]
        hit2 = row == meta_ref[1:2, :]
        onehot = jnp.where(hit1, 1.0, jnp.where(hit2, 1.0, 0.0)).astype(BF16)
        acc_ref[...] += _dot(onehot, h_ref[...])
        wts = jnp.where(hit1, meta_ref[2:3, :], jnp.where(hit2, meta_ref[3:4, :], 0.0))
        wacc_ref[...] += jnp.sum(wts, axis=1, keepdims=True)

    @pl.when(last_ref[w] == 1)
    def _():
        o_ref[...] = acc_ref[...].astype(o_ref.dtype)
        wr_ref[...] = jnp.broadcast_to(wacc_ref[...], wr_ref.shape)


def _moe_ffn_body(exp_ref, used_ref, x_ref, wr_ref, wg_ref, wu_ref, wd_ref, o_ref, acc_ref):
    f = pl.program_id(1)

    @pl.when(f == 0)
    def _():
        acc_ref[...] = jnp.zeros_like(acc_ref)

    @pl.when(used_ref[pl.program_id(0)] == 1)
    def _():
        x = x_ref[...]
        act = (_silu(_dot(x, wg_ref[...])) * _dot(x, wu_ref[...])).astype(BF16)
        acc_ref[...] += _dot(act, wd_ref[...])

    @pl.when(f == pl.num_programs(1) - 1)
    def _():
        o_ref[...] = (acc_ref[...] * wr_ref[:, 0:1]).astype(o_ref.dtype)


def _moe_combine_body(tile_ref, blk_ref, first_ref, last_ref, valid_ref, x_ref, pos_ref, y_ref, gf_ref,
                      o_ref, acc_ref, *, final_norm):
    w = pl.program_id(0)

    @pl.when(first_ref[w] == 1)
    def _():
        acc_ref[...] = x_ref[...]

    @pl.when(valid_ref[w] == 1)
    def _():
        base = blk_ref[w] * MOE_RBLK
        col = lax.broadcasted_iota(jnp.int32, (MOE_CHUNK, MOE_RBLK), 1) + base
        reps = MOE_RBLK // LANES
        p1 = jnp.concatenate([pos_ref[:, 0:LANES]] * reps, axis=1)
        p2 = jnp.concatenate([pos_ref[:, LANES:2 * LANES]] * reps, axis=1)
        hit = jnp.where(col == p1, 1.0, jnp.where(col == p2, 1.0, 0.0))
        acc_ref[...] += _dot(hit.astype(BF16), y_ref[...])

    @pl.when(last_ref[w] == 1)
    def _():
        o_ref[...] = _rms(acc_ref[...], gf_ref[...]) if final_norm else acc_ref[...]


def moe_ffn(x, h, route, w_gate, w_up, w_down, final_gain=None, tf=1408):
    t, d = x.shape
    f = w_gate.shape[2]
    tm, ck, rb = MOE_TILE, MOE_CHUNK, MOE_RBLK
    plan = _moe_plan(route, t)
    n_tiles, p = plan["n_tiles"], plan["p"]
    assert p < 2 ** 24, "row positions are carried exactly in f32"
    meta = jnp.stack([plan["pos1"].astype(F32), plan["pos2"].astype(F32), route[:, 2], route[:, 3]]
                     + [jnp.zeros((t,), F32)] * (SUBLANES - 4))
    pos_lanes = jnp.concatenate([jnp.broadcast_to(plan["pos1"][:, None], (t, LANES)),
                                 jnp.broadcast_to(plan["pos2"][:, None], (t, LANES))], axis=1)

    xs, w_row = pl.pallas_call(
        _moe_gather_body,
        grid_spec=pltpu.PrefetchScalarGridSpec(
            num_scalar_prefetch=5,
            grid=(plan["gather"][0].shape[0],),
            in_specs=[pl.BlockSpec((SUBLANES, MOE_GCHUNK), lambda w, tl, ch, fi, la, va: (0, ch[w])),
                      pl.BlockSpec((MOE_GCHUNK, d), lambda w, tl, ch, fi, la, va: (ch[w], 0))],
            out_specs=[pl.BlockSpec((MOE_GTILE, d), lambda w, tl, ch, fi, la, va: (tl[w], 0)),
                       pl.BlockSpec((MOE_GTILE, LANES), lambda w, tl, ch, fi, la, va: (tl[w], 0))],
            scratch_shapes=[pltpu.VMEM((MOE_GTILE, d), F32), pltpu.VMEM((MOE_GTILE, 1), F32)]),
        out_shape=[jax.ShapeDtypeStruct((p, d), BF16), jax.ShapeDtypeStruct((p, LANES), F32)],
        compiler_params=_cparams("arbitrary"),
        name="moe_gather",
    )(*plan["gather"], meta, h)

    ys = pl.pallas_call(
        _moe_ffn_body,
        grid_spec=pltpu.PrefetchScalarGridSpec(
            num_scalar_prefetch=2,
            grid=(n_tiles, f // tf),
            in_specs=[pl.BlockSpec((tm, d), lambda i, k, ex, us: (i, 0)),
                      pl.BlockSpec((tm, LANES), lambda i, k, ex, us: (i, 0)),
                      pl.BlockSpec((None, d, tf), lambda i, k, ex, us: (ex[i], 0, k)),
                      pl.BlockSpec((None, d, tf), lambda i, k, ex, us: (ex[i], 0, k)),
                      pl.BlockSpec((None, tf, d), lambda i, k, ex, us: (ex[i], k, 0))],
            out_specs=pl.BlockSpec((tm, d), lambda i, k, ex, us: (i, 0)),
            scratch_shapes=[pltpu.VMEM((tm, d), F32)]),
        out_shape=jax.ShapeDtypeStruct((p, d), BF16),
        compiler_params=_cparams("parallel", "arbitrary"),
        name="moe_ffn",
    )(plan["tile_expert"], plan["tile_used"], xs, w_row, w_gate, w_up, w_down)

    final_norm = final_gain is not None
    gain = (final_gain if final_norm else jnp.ones((d,), F32)).reshape(1, d)
    return pl.pallas_call(
        functools.partial(_moe_combine_body, final_norm=final_norm),
        grid_spec=pltpu.PrefetchScalarGridSpec(
            num_scalar_prefetch=5,
            grid=(plan["combine"][0].shape[0],),
            in_specs=[pl.BlockSpec((ck, d), lambda w, tl, bk, fi, la, va: (tl[w], 0)),
                      pl.BlockSpec((ck, 2 * LANES), lambda w, tl, bk, fi, la, va: (tl[w], 0)),
                      pl.BlockSpec((rb, d), lambda w, tl, bk, fi, la, va: (bk[w], 0)),
                      pl.BlockSpec((1, d), lambda w, tl, bk, fi, la, va: (0, 0))],
            out_specs=pl.BlockSpec((ck, d), lambda w, tl, bk, fi, la, va: (tl[w], 0)),
            scratch_shapes=[pltpu.VMEM((ck, d), F32)]),
        out_shape=jax.ShapeDtypeStruct((t, d), F32),
        compiler_params=_cparams("arbitrary"),
        name="moe_combine",
    )(*plan["combine"], x, pos_lanes, ys, gain)


def _rmsnorm_body(x_ref, g_ref, o_ref):
    o_ref[...] = _rms(x_ref[...], g_ref[...])


def rmsnorm(x, g, tm=1024):
    t, d = x.shape
    tm = min(tm, t)
    return pl.pallas_call(
        _rmsnorm_body,
        grid=(t // tm,),
        in_specs=[pl.BlockSpec((tm, d), lambda i: (i, 0)), pl.BlockSpec((1, d), lambda i: (0, 0))],
        out_specs=pl.BlockSpec((tm, d), lambda i: (i, 0)),
        out_shape=jax.ShapeDtypeStruct((t, d), F32),
        compiler_params=_cparams("parallel"),
        name="rmsnorm",
    )(x, g.reshape(1, d))


def _in_proj_slices(w):
    sizes = (MOBA_WIDTH, MOBA_WIDTH, MOBA_WIDTH, SSM_WIDTH, GDN_WIDTH, GDN_WIDTH, GDN_WIDTH,
             GDN_HEADS, GDN_HEADS, GDN_WIDTH, w.shape[0], w.shape[0], w.shape[0])
    parts, start = [], 0
    for size in sizes:
        parts.append(w[:, start:start + size])
        start += size
    return parts


def kernel(x, mem, positions, norm_mix, w_in, ssm_a_re, ssm_a_im, ssm_log_dt, ssm_b_re, ssm_b_im, ssm_c_re, ssm_c_im, ssm_d, ssm_w_glu, gdn_conv, gdn_a_log, gdn_dt_bias, gdn_norm, w_up_moba, w_up_gdn, w_out, norm_xa, norm_mem, xa_wq, xa_wk, xa_wv, xa_wo, norm_ffn, ffn_w_gate, ffn_w_up, ffn_w_down, moe_w_router, moe_w_gate, moe_w_up, moe_w_down, norm_final):
    batch, seq, d = x.shape
    depth = w_in.shape[0]
    t = batch * seq
    assert batch == SUBLANES, "the S5 scan packs the batch onto the 8 sublanes of a vreg"
    assert seq % MOBA_BLOCK == 0 and seq % GDN_CHUNK == 0
    nc = seq // GDN_CHUNK
    bf = lambda a: a.astype(BF16)

    xf = x.reshape(t, d)
    memf = mem.reshape(-1, d)
    cos, sin = rope_tables(positions)

    for l in range(depth):
        (wq_m, wk_m, wv_m, wu_s, wq_g, wk_g, wv_g, wa_g, wb_g, wz_g, wg_a, wg_b, wg_c) = _in_proj_slices(w_in[l])
        g_mix = norm_mix[l]
        w_ab = jnp.pad(jnp.concatenate([wa_g, wb_g], axis=1), ((0, 0), (0, LANES - 2 * GDN_HEADS)))
        w32 = jnp.concatenate([wq_m, wk_m, wu_s, wq_g, wk_g, wv_g, w_ab], axis=1)
        w16 = jnp.concatenate([wv_m, wz_g, wg_a, wg_b, wg_c], axis=1)
        p32, p16 = norm_matmul2(xf, g_mix, bf(w32), bf(w16), F32, BF16, tm=2048, tna=640, tnb=1024)
        c_us = 2 * MOBA_WIDTH
        c_qkv = c_us + SSM_WIDTH
        c_ab = c_qkv + 3 * GDN_WIDTH
        u_s = p32[:, c_us:c_us + SSM_WIDTH]
        ab_g = p32[:, c_ab:c_ab + 2 * GDN_HEADS]

        y_a = moba_attention(p32, p16, cos, sin, batch, seq, q_col=0, k_col=MOBA_WIDTH // LANES, v_col=0)

        l_re, l_im, bb_re, bb_im = s5_discretise(ssm_a_re[l], ssm_a_im[l], ssm_log_dt[l], ssm_b_re[l], ssm_b_im[l])
        maps = s5_block_maps(bb_re, bb_im, ssm_c_re[l], ssm_c_im[l])
        u_tm = u_s.reshape(batch, seq, SSM_WIDTH).transpose(1, 0, 2).reshape(t, SSM_WIDTH)
        y_s_tm = s5_gelu(u_tm, maps, l_re, l_im, ssm_d[l])
        y_s = y_s_tm.reshape(seq, batch, SSM_WIDTH).transpose(1, 0, 2).reshape(t, SSM_WIDTH)

        def head_rows(cols):
            return cols.reshape(batch, nc, GDN_CHUNK, GDN_HEADS).transpose(0, 3, 1, 2)
        a_rows = head_rows(ab_g[:, :GDN_HEADS])
        b_rows = head_rows(ab_g[:, GDN_HEADS:2 * GDN_HEADS])
        y_c = gdn_mixer(p32, p16, a_rows, b_rows, gdn_conv[l], gdn_a_log[l], gdn_dt_bias[l], gdn_norm[l],
                        batch, seq, qkv_col=c_qkv // LANES, z_col=MOBA_WIDTH // GDN_WIDTH)

        k_x = norm_matmul(memf, norm_mem[l], bf(xa_wk[l]), BF16)
        v_x = norm_matmul(memf, norm_mem[l], bf(xa_wv[l]), BF16)
        xf = merge_cross_attention(xf, y_a, y_s, y_c, p16, bf(w_up_moba[l]), bf(ssm_w_glu[l]), bf(w_up_gdn[l]),
                                   bf(w_out[l]), norm_xa[l], bf(xa_wq[l]), k_x, v_x, bf(xa_wo[l]), batch, seq,
                                   gate_col=(MOBA_WIDTH + GDN_WIDTH) // d)

        if l % 2 == 0:
            i = l // 2
            xf = ffn(xf, norm_ffn[l], bf(ffn_w_gate[i]), bf(ffn_w_up[i]), bf(ffn_w_down[i]))
        else:
            i = l // 2
            route, h_moe = moe_router(xf, norm_ffn[l], moe_w_router[i])
            xf = moe_ffn(xf, h_moe, route, bf(moe_w_gate[i]), bf(moe_w_up[i]), bf(moe_w_down[i]),
                         final_gain=norm_final if l == depth - 1 else None)

    if depth % 2 == 1:
        xf = rmsnorm(xf, norm_final)
    return xf.reshape(batch, seq, d)
```

```python
import functools

import numpy as np
import jax
import jax.numpy as jnp
from jax import lax
from jax.experimental import pallas as pl
from jax.experimental.pallas import tpu as pltpu

F32 = jnp.float32
BF16 = jnp.bfloat16
HIGHEST = lax.Precision.HIGHEST

EPS = 1e-6
NEG_INF = -1e30
MOBA_HEADS = 8
MOBA_HEAD_DIM = 64
MOBA_WIDTH = MOBA_HEADS * MOBA_HEAD_DIM
MOBA_BLOCK = 256
MOBA_TOPK = 3
ROPE_THETA = 10000.0
SSM_WIDTH = 512
SSM_GROUP = 16
SSM_GROUPS = SSM_WIDTH // SSM_GROUP
SSM_STATE = 64
SSM_STATES = SSM_GROUPS * SSM_STATE
GDN_HEADS = 4
GDN_HEAD_DIM = 128
GDN_WIDTH = GDN_HEADS * GDN_HEAD_DIM
GDN_CONV = 4
GDN_CHUNK = 64
XA_HEADS = 4
XA_HEAD_DIM = 128
XA_WIDTH = XA_HEADS * XA_HEAD_DIM
N_EXPERTS = 8
TOP_K = 2

LANES = 128
SUBLANES = 8
VMEM_LIMIT = 56 * 1024 * 1024


def _cparams(*sem):
    return pltpu.CompilerParams(dimension_semantics=sem, vmem_limit_bytes=VMEM_LIMIT)


def _rms(x, g):
    return x * lax.rsqrt(jnp.mean(x * x, axis=-1, keepdims=True) + EPS) * g


def _sigmoid(x):
    return 1.0 / (1.0 + jnp.exp(-x))


def _silu(x):
    return x * _sigmoid(x)


def _dot(a, b):
    return jnp.dot(a, b, preferred_element_type=F32)


def _dot_nt(a, b):
    return lax.dot_general(a, b, (((1,), (1,)), ((), ())), preferred_element_type=F32)


def _norm_matmul_body(x_ref, g_ref, w_ref, o_ref, h_ref):
    @pl.when(pl.program_id(1) == 0)
    def _():
        h_ref[...] = _rms(x_ref[...], g_ref[...]).astype(BF16)

    o_ref[...] = _dot(h_ref[...], w_ref[...]).astype(o_ref.dtype)


def norm_matmul(x, g, w, out_dtype, tm=512, tn=512):
    t, k = x.shape
    n = w.shape[1]
    tm, tn = min(tm, t), min(tn, n)
    return pl.pallas_call(
        _norm_matmul_body,
        grid=(t // tm, n // tn),
        in_specs=[pl.BlockSpec((tm, k), lambda i, j: (i, 0)),
                  pl.BlockSpec((1, k), lambda i, j: (0, 0)),
                  pl.BlockSpec((k, tn), lambda i, j: (0, j))],
        out_specs=pl.BlockSpec((tm, tn), lambda i, j: (i, j)),
        out_shape=jax.ShapeDtypeStruct((t, n), out_dtype),
        scratch_shapes=[pltpu.VMEM((tm, k), BF16)],
        compiler_params=_cparams("parallel", "arbitrary"),
        name="norm_matmul",
    )(x, g.reshape(1, k), w)


def _norm_matmul2_body(x_ref, g_ref, wa_ref, wb_ref, oa_ref, ob_ref, h_ref, *, na):
    j = pl.program_id(1)

    @pl.when(j == 0)
    def _():
        h_ref[...] = _rms(x_ref[...], g_ref[...]).astype(BF16)

    @pl.when(j < na)
    def _():
        oa_ref[...] = _dot(h_ref[...], wa_ref[...]).astype(oa_ref.dtype)

    @pl.when(j >= na)
    def _():
        ob_ref[...] = _dot(h_ref[...], wb_ref[...]).astype(ob_ref.dtype)


def norm_matmul2(x, g, wa, wb, dtype_a, dtype_b, tm, tna, tnb):
    t, k = x.shape
    tm = min(tm, t)
    na, nb = wa.shape[1] // tna, wb.shape[1] // tnb
    return pl.pallas_call(
        functools.partial(_norm_matmul2_body, na=na),
        grid=(t // tm, na + nb),
        in_specs=[pl.BlockSpec((tm, k), lambda i, j: (i, 0)),
                  pl.BlockSpec((1, k), lambda i, j: (0, 0)),
                  pl.BlockSpec((k, tna), lambda i, j: (0, jnp.minimum(j, na - 1))),
                  pl.BlockSpec((k, tnb), lambda i, j: (0, jnp.maximum(j - na, 0)))],
        out_specs=[pl.BlockSpec((tm, tna), lambda i, j: (i, jnp.minimum(j, na - 1))),
                   pl.BlockSpec((tm, tnb), lambda i, j: (i, jnp.maximum(j - na, 0)))],
        out_shape=[jax.ShapeDtypeStruct((t, wa.shape[1]), dtype_a), jax.ShapeDtypeStruct((t, wb.shape[1]), dtype_b)],
        scratch_shapes=[pltpu.VMEM((tm, k), BF16)],
        compiler_params=_cparams("parallel", "arbitrary"),
        name="norm_matmul2",
    )(x, g.reshape(1, k), wa, wb)


def _rope_body(pos_ref, inv_ref, cos_ref, sin_ref):
    ang = pos_ref[...] * inv_ref[...]
    lane = lax.broadcasted_iota(jnp.int32, ang.shape, 1)
    first_half = (lane % MOBA_HEAD_DIM) < (MOBA_HEAD_DIM // 2)
    cos_ref[...] = jnp.cos(ang)
    s = jnp.sin(ang)
    sin_ref[...] = jnp.where(first_half, -s, s)


def rope_tables(positions):
    t = positions.size
    half = MOBA_HEAD_DIM // 2
    inv = (1.0 / (np.float32(ROPE_THETA) ** (np.arange(0, MOBA_HEAD_DIM, 2, dtype=np.float32)
                                             / np.float32(MOBA_HEAD_DIM)))).astype(np.float32)
    inv_row = jnp.asarray(np.tile(inv, LANES // half).reshape(1, LANES))
    pos = positions.astype(F32).reshape(t, 1)
    ts = min(1024, t)
    return pl.pallas_call(
        _rope_body,
        grid=(t // ts,),
        in_specs=[pl.BlockSpec((ts, 1), lambda i: (i, 0)),
                  pl.BlockSpec((1, LANES), lambda i: (0, 0))],
        out_specs=[pl.BlockSpec((ts, LANES), lambda i: (i, 0))] * 2,
        out_shape=[jax.ShapeDtypeStruct((t, LANES), F32)] * 2,
        compiler_params=_cparams("parallel"),
        name="rope_tables",
    )(pos, inv_row)


def _rope_apply(x, cos, sin_signed):
    lane = lax.broadcasted_iota(jnp.int32, x.shape, 1)
    first_half = (lane % MOBA_HEAD_DIM) < (MOBA_HEAD_DIM // 2)
    partner = jnp.where(first_half, pltpu.roll(x, LANES - MOBA_HEAD_DIM // 2, 1),
                        pltpu.roll(x, MOBA_HEAD_DIM // 2, 1))
    return x * cos + partner * sin_signed


def _moba_body(q_ref, k_ref, v_ref, cq_ref, sq_ref, ck_ref, sk_ref, o_ref, ka_ref, va_ref, km_ref, s_ref,
               mt_ref, acc_ref, *, nb):
    blk = MOBA_BLOCK
    i = pl.program_id(2)

    @pl.when(i == 0)
    def _():
        km_ref[...] = jnp.zeros_like(km_ref)
        lane_b = lax.broadcasted_iota(jnp.int32, (blk, LANES), 1)

        def rope_blk(j, c):
            rows = pl.ds(pl.multiple_of(j * blk, blk), blk)
            kr = _rope_apply(k_ref[rows, :], ck_ref[rows, :], sk_ref[rows, :])
            ka_ref[rows, 0:LANES] = kr.astype(BF16)
            ka_ref[rows, LANES:2 * LANES] = jnp.where(lane_b == j, 1.0, 0.0).astype(BF16)
            km_ref[pl.ds(j, 1), :] = jnp.mean(kr, axis=0, keepdims=True)
            v = v_ref[rows, :].astype(F32)
            va_ref[0, rows, :] = jnp.where(lane_b < MOBA_HEAD_DIM, v,
                                           jnp.where(lane_b == MOBA_HEAD_DIM, 1.0, 0.0)).astype(BF16)
            va_ref[1, rows, :] = jnp.where(lane_b >= MOBA_HEAD_DIM, v,
                                           jnp.where(lane_b == 0, 1.0, 0.0)).astype(BF16)
            return c
        lax.fori_loop(0, nb, rope_blk, 0)

    tq = 2 * blk
    lane = lax.broadcasted_iota(jnp.int32, (1, LANES), 1)
    head_a = lane < MOBA_HEAD_DIM
    q = _rope_apply(q_ref[...], cq_ref[...], sq_ref[...])
    scale = MOBA_HEAD_DIM ** -0.5
    km = km_ref[...]
    nbp = km_ref.shape[0]
    brow = lax.broadcasted_iota(jnp.int32, (nbp, tq), 0)
    qblk = 2 * i + jnp.where(lax.broadcasted_iota(jnp.int32, (1, tq), 1) >= blk, 1, 0)
    heads = (0, 1)
    q_heads = [jnp.where(head_a, q, 0.0), jnp.where(head_a, 0.0, q)]
    km_hi = km.astype(BF16)
    km_lo = (km - km_hi.astype(F32)).astype(BF16)
    gates = []
    for qh in q_heads:
        q_hi = qh.astype(BF16)
        q_lo = (qh - q_hi.astype(F32)).astype(BF16)
        gates.append(_dot_nt(km_hi, q_hi) + (_dot_nt(km_hi, q_lo) + _dot_nt(km_lo, q_hi)))
    q_augs = []
    brow_f = brow.astype(F32)
    for h in heads:
        gate = jnp.where(brow < qblk, gates[h], NEG_INF)
        picked = jnp.zeros((nbp, tq), F32)
        for _ in range(MOBA_TOPK):
            top = jnp.max(gate, axis=0, keepdims=True)
            first = jnp.min(jnp.where(gate == top, brow_f, float(nbp)), axis=0, keepdims=True)
            hit = brow_f == first
            picked = jnp.where(hit, 1.0, picked)
            gate = jnp.where(hit, -jnp.inf, gate)
        keep = ((picked > 0.5) & (brow < qblk)) | (brow == qblk)
        bias_t = jnp.concatenate([jnp.where(keep, 0.0, NEG_INF), jnp.zeros((LANES - nbp, tq), F32)],
                                 axis=0)
        q_augs.append(jnp.concatenate([q_heads[h] * scale, bias_t.T], axis=1).astype(BF16))

    def pair_rows(jp):
        return pl.ds(pl.multiple_of(jp * tq, tq), tq)

    r_idx = lax.broadcasted_iota(jnp.int32, (tq, tq), 0)
    c_idx = lax.broadcasted_iota(jnp.int32, (tq, tq), 1)
    k_diag = ka_ref[pair_rows(i), :]
    s_diag = [_dot_nt(q_augs[h], k_diag) for h in heads]
    for h in heads:
        s_d = jnp.where(c_idx <= r_idx, s_diag[h], NEG_INF)
        s_ref[h, i] = s_d
        m_t = s_d[:, 0:LANES]
        for col in range(LANES, tq, LANES):
            m_t = jnp.maximum(m_t, s_d[:, col:col + LANES])
        mt_ref[h] = m_t

    def loop2(n, body):
        def two(k, c):
            body([2 * k, 2 * k + 1])
            return c
        lax.fori_loop(0, lax.shift_right_logical(n, 1), two, 0)

        @pl.when((n & 1) == 1)
        def _():
            body([n - 1])

    def scores(jps):
        k2s = [ka_ref[pair_rows(jp), :] for jp in jps]
        s2s = [[_dot_nt(q_augs[h], k2) for h in heads] for k2 in k2s]
        for jp, s2h in zip(jps, s2s):
            for h in heads:
                s_ref[h, jp] = s2h[h]
        for h in heads:
            m_t = mt_ref[h]
            for s2h in s2s:
                for col in range(0, tq, LANES):
                    m_t = jnp.maximum(m_t, s2h[h][:, col:col + LANES])
            mt_ref[h] = m_t
    loop2(i, scores)

    ms = [jnp.max(mt_ref[h], axis=-1, keepdims=True) for h in heads]
    for h in heads:
        acc_ref[h] = jnp.zeros((tq, LANES), F32)

    def values(jps):
        ps = [[jnp.exp(s_ref[h, jp] - ms[h]).astype(BF16) for h in heads] for jp in jps]
        pvs = [[_dot(p[h], va_ref[h, pair_rows(jp), :]) for h in heads] for jp, p in zip(jps, ps)]
        for h in heads:
            tot = pvs[0][h]
            for pv in pvs[1:]:
                tot = tot + pv[h]
            acc_ref[h] += tot
    loop2(i + 1, values)
    acc_a, acc_b = acc_ref[0], acc_ref[1]
    out_a = acc_a / acc_a[:, MOBA_HEAD_DIM:MOBA_HEAD_DIM + 1]
    out_b = acc_b / acc_b[:, 0:1]
    o_ref[...] = jnp.where(head_a, out_a, out_b).astype(o_ref.dtype)


def moba_attention(qk, v, cos, sin, batch, seq, q_col=0, k_col=MOBA_WIDTH // LANES, v_col=0):
    nb = seq // MOBA_BLOCK
    assert nb % 2 == 0
    npair = nb // 2
    pairs = MOBA_WIDTH // LANES
    tq = 2 * MOBA_BLOCK
    t = batch * seq
    return pl.pallas_call(
        functools.partial(_moba_body, nb=nb),
        grid=(batch, pairs, npair),
        in_specs=[pl.BlockSpec((tq, LANES), lambda b, p, i: (b * npair + i, q_col + p)),
                  pl.BlockSpec((seq, LANES), lambda b, p, i: (b, k_col + p)),
                  pl.BlockSpec((seq, LANES), lambda b, p, i: (b, v_col + p)),
                  pl.BlockSpec((tq, LANES), lambda b, p, i: (b * npair + i, 0)),
                  pl.BlockSpec((tq, LANES), lambda b, p, i: (b * npair + i, 0)),
                  pl.BlockSpec((seq, LANES), lambda b, p, i: (b, 0)),
                  pl.BlockSpec((seq, LANES), lambda b, p, i: (b, 0))],
        out_specs=pl.BlockSpec((tq, LANES), lambda b, p, i: (b * npair + i, p)),
        out_shape=jax.ShapeDtypeStruct((t, MOBA_WIDTH), BF16),
        scratch_shapes=[pltpu.VMEM((seq, 2 * LANES), BF16),
                        pltpu.VMEM((2, seq, LANES), BF16),
                        pltpu.VMEM((-(-nb // SUBLANES) * SUBLANES, LANES), F32),
                        pltpu.VMEM((2, npair, tq, tq), F32),
                        pltpu.VMEM((2, tq, LANES), F32),
                        pltpu.VMEM((2, tq, LANES), F32)],
        compiler_params=_cparams("parallel", "parallel", "arbitrary"),
        name="moba_attention",
    )(qk, qk, v, cos, sin, cos, sin)


def _s5_disc_body(are_ref, aim_ref, ldt_ref, bre_ref, bim_ref, lre_ref, lim_ref, bbre_ref, bbim_ref):
    a_re, a_im = are_ref[...], aim_ref[...]
    dt = jnp.exp(ldt_ref[...])
    mag = jnp.exp(a_re * dt)
    l_re = mag * jnp.cos(a_im * dt)
    l_im = mag * jnp.sin(a_im * dt)
    lre_ref[...] = l_re
    lim_ref[...] = l_im
    x, y = l_re - 1.0, l_im
    den = a_re * a_re + a_im * a_im
    c_re = (x * a_re + y * a_im) / den
    c_im = (y * a_re - x * a_im) / den
    b_re, b_im = bre_ref[...], bim_ref[...]
    bbre_ref[...] = c_re * b_re - c_im * b_im
    bbim_ref[...] = c_re * b_im + c_im * b_re


def s5_discretise(a_re, a_im, log_dt, b_re, b_im):
    g, p = a_re.shape
    n = b_re.shape[-1]
    col = lambda a: a.reshape(g * p, 1)
    ldt = jnp.broadcast_to(log_dt[:, None], (g, p))
    outs = pl.pallas_call(
        _s5_disc_body,
        out_shape=[jax.ShapeDtypeStruct((g * p, 1), F32)] * 2 + [jax.ShapeDtypeStruct((g * p, n), F32)] * 2,
        name="s5_discretise",
    )(col(a_re), col(a_im), col(ldt), b_re.reshape(g * p, n), b_im.reshape(g * p, n))
    l_re, l_im, bb_re, bb_im = outs
    return l_re.reshape(g, p), l_im.reshape(g, p), bb_re.reshape(g, p, n), bb_im.reshape(g, p, n)


def _block_diag(blocks):
    g, r, c = blocks.shape
    eye = jnp.eye(g, dtype=blocks.dtype)
    return (blocks[:, :, None, :] * eye[:, None, :, None]).reshape(g * r, g * c)


def _s5_body(u_ref, bre_ref, bim_ref, cre_ref, cim_ref, lre_ref, lim_ref, d_ref, y_ref, h_ref, hb_ref, *, steps):
    rows = SUBLANES
    ns = SSM_STATES
    nq = SSM_WIDTH // LANES
    sq = ns // nq

    @pl.when(pl.program_id(0) == 0)
    def _():
        h_ref[...] = jnp.zeros_like(h_ref)

    u = u_ref[...]
    u16 = u.astype(BF16)
    for q in range(nq):
        uq = u16[:, q * LANES:(q + 1) * LANES]
        hb_ref[:, q * sq:(q + 1) * sq] = _dot(uq, bre_ref[q])
        hb_ref[:, ns + q * sq:ns + (q + 1) * sq] = _dot(uq, bim_ref[q])
    l_re, l_im = lre_ref[...], lim_ref[...]

    def step(t, carry):
        h_re, h_im = carry
        r = pl.ds(pl.multiple_of(t * rows, rows), rows)
        n_re = l_re * h_re - l_im * h_im + hb_ref[r, 0:ns]
        n_im = l_re * h_im + l_im * h_re + hb_ref[r, ns:2 * ns]
        hb_ref[r, 0:ns] = n_re
        hb_ref[r, ns:2 * ns] = n_im
        return n_re, n_im

    h_re, h_im = lax.fori_loop(0, steps, step, (h_ref[:, 0:ns], h_ref[:, ns:2 * ns]), unroll=4)
    h_ref[:, 0:ns] = h_re
    h_ref[:, ns:2 * ns] = h_im
    ys = []
    for q in range(nq):
        ys.append(_dot(hb_ref[:, q * sq:(q + 1) * sq].astype(BF16), cre_ref[q])
                  + _dot(hb_ref[:, ns + q * sq:ns + (q + 1) * sq].astype(BF16), cim_ref[q]))
    y = jnp.concatenate(ys, axis=1) + d_ref[...] * u
    y_ref[...] = jax.nn.gelu(y).astype(y_ref.dtype)


def s5_block_maps(bb_re, bb_im, c_re, c_im):
    nq = SSM_WIDTH // LANES
    gq = SSM_GROUPS // nq
    per_q = lambda a: jnp.stack([_block_diag(a[q * gq:(q + 1) * gq].transpose(0, 2, 1)) for q in range(nq)])
    return (per_q(bb_re).astype(BF16), per_q(bb_im).astype(BF16),
            per_q(c_re).astype(BF16), per_q(-c_im).astype(BF16))


def s5_gelu(u_tm, maps, l_re, l_im, d_skip, steps=64):
    rows = u_tm.shape[0]
    blk = steps * SUBLANES
    ns2 = 2 * SSM_STATES
    full3 = lambda a: pl.BlockSpec(a.shape, lambda c: (0, 0, 0))
    return pl.pallas_call(
        functools.partial(_s5_body, steps=steps),
        grid=(rows // blk,),
        in_specs=[pl.BlockSpec((blk, SSM_WIDTH), lambda c: (c, 0)),
                  full3(maps[0]), full3(maps[1]), full3(maps[2]), full3(maps[3]),
                  pl.BlockSpec((1, SSM_STATES), lambda c: (0, 0)),
                  pl.BlockSpec((1, SSM_STATES), lambda c: (0, 0)),
                  pl.BlockSpec((1, SSM_WIDTH), lambda c: (0, 0))],
        out_specs=pl.BlockSpec((blk, SSM_WIDTH), lambda c: (c, 0)),
        out_shape=jax.ShapeDtypeStruct((rows, SSM_WIDTH), BF16),
        scratch_shapes=[pltpu.VMEM((SUBLANES, ns2), F32),
                        pltpu.VMEM((blk, ns2), F32)],
        compiler_params=_cparams("arbitrary"),
        name="s5_scan",
    )(u_tm, *maps, l_re.reshape(1, -1), l_im.reshape(1, -1), d_skip.reshape(1, -1))


GDN_GROUP = 16


def _gdn_prep_body(q_ref, k_ref, v_ref, a_ref, b_ref, wq_ref, wk_ref, wv_ref, alog_ref, dtb_ref,
                   u_ref, w_ref, qd_ref, kd_ref, qk_ref, gl_ref, qs_ref, ks_ref, vs_ref,
                   *, seq):
    L = GDN_CHUNK
    G = min(GDN_GROUP, seq // L)
    dk = GDN_HEAD_DIM
    nc = seq // L
    hd = pl.program_id(1)
    cb = 256
    halo = SUBLANES

    def conv_blk(rb, c):
        base = pl.multiple_of(rb * cb, cb)
        prev = pl.multiple_of(jnp.maximum(base - halo, 0), halo)
        for src, wref, dst, norm, mul in ((q_ref, wq_ref, qs_ref, True, dk ** -0.5),
                                          (k_ref, wk_ref, ks_ref, True, 1.0),
                                          (v_ref, wv_ref, vs_ref, False, 1.0)):
            cur = src[pl.ds(base, cb), :]
            head = jnp.where(rb > 0, src[pl.ds(prev, halo), :], 0.0)
            ext = jnp.concatenate([head, cur], axis=0)
            wts = wref[...]
            y = ext[halo:halo + cb] * wts[GDN_CONV - 1:GDN_CONV]
            for tap in range(1, GDN_CONV):
                y = y + ext[halo - tap:halo - tap + cb] * wts[GDN_CONV - 1 - tap:GDN_CONV - tap]
            y = _silu(y)
            if norm:
                y = y * lax.rsqrt(jnp.sum(y * y, axis=-1, keepdims=True) + EPS) * mul
            dst[pl.ds(base, cb), :] = y
        return c
    lax.fori_loop(0, seq // cb, conv_blk, 0)

    gl_rows = G * L
    ri = lax.broadcasted_iota(jnp.int32, (gl_rows, L), 0) % L
    ci = lax.broadcasted_iota(jnp.int32, (gl_rows, L), 1)
    tril = ci <= ri
    strict = ci < ri
    eye = ci == ri
    ri3 = lax.broadcasted_iota(jnp.int32, (G, L, L), 1)
    ci3 = lax.broadcasted_iota(jnp.int32, (G, L, L), 2)
    strict3 = ci3 < ri3
    eye_f = jnp.where(ci3 == ri3, 1.0, 0.0)
    r2 = lax.broadcasted_iota(jnp.int32, (L, L), 0)
    c2 = lax.broadcasted_iota(jnp.int32, (L, L), 1)
    upper_f = jnp.where(r2 <= c2, 1.0, 0.0)
    a_coef = -jnp.exp(jnp.full((1, L), alog_ref[hd], F32))
    dt_bias = dtb_ref[hd]
    hp = functools.partial(jnp.dot, preferred_element_type=F32, precision=HIGHEST)

    def bmm(a, b):
        return jnp.einsum('gij,gjk->gik', a, b, preferred_element_type=F32)

    def bmm_nt(a, b):
        return jnp.einsum('gid,gjd->gij', a, b, preferred_element_type=F32)

    def to_col(rows_b):
        return jnp.sum(jnp.where(eye, rows_b, 0.0), axis=1, keepdims=True)

    def prep(cg, carry):
        c0 = pl.multiple_of(cg * G, G)
        rows = pl.ds(pl.multiple_of(cg * gl_rows, gl_rows), gl_rows)
        qc, kc, vc = qs_ref[rows, :], ks_ref[rows, :], vs_ref[rows, :]
        z = a_ref[pl.ds(c0, G), :] + dt_bias
        softplus = jnp.maximum(z, 0.0) + jnp.log(1.0 + jnp.exp(-jnp.abs(z)))
        g_rows = hp(a_coef * softplus, upper_f)
        beta_rows = _sigmoid(b_ref[pl.ds(c0, G), :])
        spread = lambda r: jnp.concatenate([jnp.broadcast_to(r[c:c + 1, :], (L, L)) for c in range(G)], axis=0)
        g_rb = spread(g_rows)
        g_col = to_col(g_rb)
        beta_col = to_col(spread(beta_rows))
        g_last = g_rb[:, L - 1:L]
        decay = jnp.where(tril, jnp.exp(jnp.where(tril, g_col - g_rb, 0.0)), 0.0)
        k_beta = kc * beta_col
        v_beta = vc * beta_col
        to3 = lambda a: a.reshape(G, L, a.shape[-1])
        kb16, k16, q16 = to3(k_beta.astype(BF16)), to3(kc.astype(BF16)), to3(qc.astype(BF16))
        kk_qk = bmm_nt(jnp.concatenate([kb16, q16], axis=1), k16)
        decay3 = to3(decay)
        a_low = jnp.where(strict3, kk_qk[:, :L] * decay3, 0.0)
        qk = kk_qk[:, L:] * decay3
        t_inv = eye_f - a_low
        a16 = a_low.astype(BF16)
        pw = bmm(a16, a16)
        span = 2
        while 2 * span < L:
            pw16 = pw.astype(BF16)
            both = bmm(jnp.concatenate([t_inv.astype(BF16), pw16], axis=1), pw16)
            t_inv = t_inv + both[:, :L]
            pw = both[:, L:]
            span *= 2
        t_inv = t_inv + bmm(t_inv.astype(BF16), pw.astype(BF16))
        rhs = jnp.concatenate([v_beta.astype(BF16), (k_beta * jnp.exp(g_col)).astype(BF16)], axis=1)
        uw = bmm(t_inv.astype(BF16), to3(rhs)).reshape(gl_rows, 2 * dk)
        u_ref[rows, :] = uw[:, :dk]
        w_ref[rows, :] = uw[:, dk:].astype(BF16)
        qd_ref[rows, :] = (qc * jnp.exp(g_col)).astype(BF16)
        k_dec = to3(kc * jnp.exp(g_last - g_col))
        kd_ref[pl.ds(c0, G)] = jnp.swapaxes(k_dec, 1, 2).astype(BF16)
        qk_ref[pl.ds(c0, G)] = qk.astype(BF16)
        gl_ref[pl.ds(c0, G), :] = jnp.broadcast_to(jnp.exp(g_rows[:, L - 1:L]), (G, LANES))
        return carry
    lax.fori_loop(0, nc // G, prep, 0)


def _gdn_scan_body(u_ref, w_ref, qd_ref, kd_ref, qk_ref, gl_ref, z_ref, gn_ref, o_ref, *st_refs, chunks):
    L = GDN_CHUNK
    d = GDN_HEAD_DIM
    nrow = u_ref.shape[0]

    @pl.when(pl.program_id(1) == 0)
    def _():
        for st in st_refs:
            st[...] = jnp.zeros_like(st)

    gn = gn_ref[...]

    def chunk(c, carry):
        rows = pl.ds(pl.multiple_of(c * L, L), L)
        chains = [(r, h) for r in range(nrow) for h in range(GDN_HEADS)]
        states = [st_refs[r * GDN_HEADS + h][...] for r, h in chains]
        firsts = []
        for (r, h), s in zip(chains, states):
            cols = slice(h * d, (h + 1) * d)
            firsts.append(_dot(jnp.concatenate([w_ref[r, rows, cols], qd_ref[r, rows, cols]], axis=0),
                               s.astype(BF16)))
        vns = []
        for (r, h), both in zip(chains, firsts):
            cols = slice(h * d, (h + 1) * d)
            vns.append((u_ref[r, rows, cols] - both[:L]).astype(BF16))
        outs = [both[L:] + _dot(qk_ref[r, h, c], vn) for (r, h), both, vn in zip(chains, firsts, vns)]
        for (r, h), s, vn in zip(chains, states, vns):
            st_refs[r * GDN_HEADS + h][...] = s * gl_ref[r, h, pl.ds(c, 1), :] + _dot(kd_ref[r, h, c], vn)
        for (r, h), o in zip(chains, outs):
            cols = slice(h * d, (h + 1) * d)
            o_ref[r, rows, cols] = (_rms(o, gn) * _silu(z_ref[r, rows, cols].astype(F32))).astype(o_ref.dtype)
        return carry
    lax.fori_loop(0, chunks, chunk, 0)


def gdn_mixer(qkv, z, a_rows, b_rows, conv_w, a_log, dt_bias, gnorm, batch, seq, ts=1024, qkv_col=0, z_col=0):
    t = batch * seq
    hds = GDN_HEADS
    nc = seq // GDN_CHUNK
    L = GDN_CHUNK
    d = GDN_HEAD_DIM
    seq_spec = lambda off: pl.BlockSpec((seq, d), lambda b, h: (b, qkv_col + off + h))
    w_spec = lambda off: pl.BlockSpec((GDN_CONV, d), lambda b, h: (0, off + h))
    row_spec = pl.BlockSpec((None, None, nc, L), lambda b, h: (b, h, 0, 0))
    smem = pl.BlockSpec(memory_space=pltpu.SMEM)
    head_out = pl.BlockSpec((seq, d), lambda b, h: (b, h))
    u, w, qd, kd, qk, gl = pl.pallas_call(
        functools.partial(_gdn_prep_body, seq=seq),
        grid=(batch, hds),
        in_specs=[seq_spec(0), seq_spec(hds), seq_spec(2 * hds), row_spec, row_spec,
                  w_spec(0), w_spec(hds), w_spec(2 * hds), smem, smem],
        out_specs=[head_out, head_out, head_out,
                   pl.BlockSpec((None, None, nc, d, L), lambda b, h: (b, h, 0, 0, 0)),
                   pl.BlockSpec((None, None, nc, L, L), lambda b, h: (b, h, 0, 0, 0)),
                   pl.BlockSpec((None, None, nc, LANES), lambda b, h: (b, h, 0, 0))],
        out_shape=[jax.ShapeDtypeStruct((t, GDN_WIDTH), F32),
                   jax.ShapeDtypeStruct((t, GDN_WIDTH), BF16),
                   jax.ShapeDtypeStruct((t, GDN_WIDTH), BF16),
                   jax.ShapeDtypeStruct((batch, hds, nc, d, L), BF16),
                   jax.ShapeDtypeStruct((batch, hds, nc, L, L), BF16),
                   jax.ShapeDtypeStruct((batch, hds, nc, LANES), F32)],
        scratch_shapes=[pltpu.VMEM((seq, d), F32), pltpu.VMEM((seq, d), F32), pltpu.VMEM((seq, d), F32)],
        compiler_params=_cparams("parallel", "parallel"),
        name="gdn_prep",
    )(qkv, qkv, qkv, a_rows, b_rows, conv_w, conv_w, conv_w, a_log, dt_bias)

    ts = min(ts, seq)
    spb = seq // ts
    cps = ts // L
    nrow = 2 if batch % 2 == 0 else 1
    by_row = lambda a: a.reshape(batch, seq, a.shape[-1])
    tok = pl.BlockSpec((nrow, ts, GDN_WIDTH), lambda b, s: (b, s, 0))
    out = pl.pallas_call(
        functools.partial(_gdn_scan_body, chunks=cps),
        grid=(batch // nrow, spb),
        in_specs=[tok, tok, tok,
                  pl.BlockSpec((nrow, hds, cps, d, L), lambda b, s: (b, 0, s, 0, 0)),
                  pl.BlockSpec((nrow, hds, cps, L, L), lambda b, s: (b, 0, s, 0, 0)),
                  pl.BlockSpec((nrow, hds, cps, LANES), lambda b, s: (b, 0, s, 0)),
                  pl.BlockSpec((nrow, ts, GDN_WIDTH), lambda b, s: (b, s, z_col)),
                  pl.BlockSpec((1, d), lambda b, s: (0, 0))],
        out_specs=tok,
        out_shape=jax.ShapeDtypeStruct((batch, seq, GDN_WIDTH), BF16),
        scratch_shapes=[pltpu.VMEM((d, d), F32)] * (nrow * hds),
        compiler_params=_cparams("parallel", "arbitrary"),
        name="gdn_scan",
    )(by_row(u), by_row(w), by_row(qd), kd, qk, gl, by_row(z), gnorm.reshape(1, d))
    return out.reshape(t, GDN_WIDTH)


def _merge_xattn_body(x_ref, ya_ref, ys_ref, yc_ref, ga_ref, gb_ref, gc_ref, wa_ref, wg_ref, wc_ref, wo_ref,
                      gx_ref, wq_ref, k_ref, v_ref, wxo_ref, o_ref):
    d = x_ref.shape[1]
    y_a = _dot(ya_ref[...], wa_ref[...])
    glu = _dot(ys_ref[...], wg_ref[...])
    y_b = glu[:, :d] * _sigmoid(glu[:, d:])
    y_c = _dot(yc_ref[...], wc_ref[...])
    gate = lambda r: _sigmoid(r[...].astype(F32))
    merged = gate(ga_ref) * y_a + gate(gb_ref) * y_b + gate(gc_ref) * y_c
    x1 = x_ref[...] + _dot(merged.astype(BF16), wo_ref[...])

    h = _rms(x1, gx_ref[...]).astype(BF16)
    q = _dot(h, wq_ref[...]).astype(BF16)
    cols = [slice(hd * XA_HEAD_DIM, (hd + 1) * XA_HEAD_DIM) for hd in range(XA_HEADS)]
    scores = [_dot_nt(q[:, c], k_ref[:, c]) * (XA_HEAD_DIM ** -0.5) for c in cols]
    probs = []
    for s in scores:
        p = jnp.exp(s - jnp.max(s, axis=-1, keepdims=True))
        probs.append((p / jnp.sum(p, axis=-1, keepdims=True)).astype(BF16))
    outs = [_dot(p, v_ref[:, c]) for p, c in zip(probs, cols)]
    o = jnp.concatenate(outs, axis=1).astype(BF16)
    o_ref[...] = x1 + _dot(o, wxo_ref[...])


def merge_cross_attention(x, y_a, y_s, y_c, gates, w_a, w_glu, w_c, w_o, g_xa, wq, k, v, wo_xa, batch, seq,
                          tm=512, gate_col=0):
    t, d = x.shape
    m = k.shape[0] // batch
    spb = seq // tm
    row = lambda n: pl.BlockSpec((tm, n), lambda i: (i, 0))
    gate = lambda j: pl.BlockSpec((tm, d), lambda i: (i, gate_col + j))
    full = lambda a: pl.BlockSpec(a.shape, lambda i: (0, 0))
    mem = pl.BlockSpec((m, XA_WIDTH), lambda i: (i // spb, 0))
    return pl.pallas_call(
        _merge_xattn_body,
        grid=(t // tm,),
        in_specs=[row(d), row(y_a.shape[1]), row(y_s.shape[1]), row(y_c.shape[1]), gate(0), gate(1), gate(2),
                  full(w_a), full(w_glu), full(w_c), full(w_o),
                  pl.BlockSpec((1, d), lambda i: (0, 0)), full(wq), mem, mem, full(wo_xa)],
        out_specs=row(d),
        out_shape=jax.ShapeDtypeStruct((t, d), F32),
        compiler_params=_cparams("parallel"),
        name="merge_cross_attention",
    )(x, y_a, y_s, y_c, gates, gates, gates, w_a, w_glu, w_c, w_o, g_xa.reshape(1, d), wq, k, v, wo_xa)


def _ffn_body(x_ref, g_ref, wg_ref, wu_ref, wd_ref, o_ref, h_ref, acc_ref):
    f = pl.program_id(1)

    @pl.when(f == 0)
    def _():
        h_ref[...] = _rms(x_ref[...], g_ref[...]).astype(BF16)
        acc_ref[...] = jnp.zeros_like(acc_ref)

    h = h_ref[...]
    act = (_silu(_dot(h, wg_ref[...])) * _dot(h, wu_ref[...])).astype(BF16)
    acc_ref[...] += _dot(act, wd_ref[...])

    @pl.when(f == pl.num_programs(1) - 1)
    def _():
        o_ref[...] = x_ref[...] + acc_ref[...]


def ffn(x, g, w_gate, w_up, w_down, tm=512, tf=1408):
    t, d = x.shape
    f = w_gate.shape[1]
    return pl.pallas_call(
        _ffn_body,
        grid=(t // tm, f // tf),
        in_specs=[pl.BlockSpec((tm, d), lambda i, k: (i, 0)),
                  pl.BlockSpec((1, d), lambda i, k: (0, 0)),
                  pl.BlockSpec((d, tf), lambda i, k: (0, k)),
                  pl.BlockSpec((d, tf), lambda i, k: (0, k)),
                  pl.BlockSpec((tf, d), lambda i, k: (k, 0))],
        out_specs=pl.BlockSpec((tm, d), lambda i, k: (i, 0)),
        out_shape=jax.ShapeDtypeStruct((t, d), F32),
        scratch_shapes=[pltpu.VMEM((tm, d), BF16), pltpu.VMEM((tm, d), F32)],
        compiler_params=_cparams("parallel", "arbitrary"),
        name="ffn",
    )(x, g.reshape(1, d), w_gate, w_up, w_down)


def _router_body(x_ref, g_ref, whi_ref, wlo_ref, r_ref, h_ref):
    h = _rms(x_ref[...], g_ref[...])
    h_hi = h.astype(BF16)
    h_ref[...] = h_hi
    h_lo = (h - h_hi.astype(F32)).astype(BF16)
    logits = _dot(h_hi, whi_ref[...]) + (_dot(h_hi, wlo_ref[...]) + _dot(h_lo, whi_ref[...]))
    lane = lax.broadcasted_iota(jnp.int32, logits.shape, 1).astype(F32)
    logits = jnp.where(lane < N_EXPERTS, logits, -jnp.inf)
    v1 = jnp.max(logits, axis=-1, keepdims=True)
    i1 = jnp.min(jnp.where(logits == v1, lane, float(LANES)), axis=-1, keepdims=True)
    rest = jnp.where(lane == i1, -jnp.inf, logits)
    v2 = jnp.max(rest, axis=-1, keepdims=True)
    i2 = jnp.min(jnp.where(rest == v2, lane, float(LANES)), axis=-1, keepdims=True)
    e2 = jnp.exp(v2 - v1)
    w1 = 1.0 / (1.0 + e2)
    w2 = e2 / (1.0 + e2)
    r_ref[...] = (jnp.where(lane == 0.0, i1, 0.0) + jnp.where(lane == 1.0, i2, 0.0)
                  + jnp.where(lane == 2.0, w1, 0.0) + jnp.where(lane == 3.0, w2, 0.0))


def moe_router(x, g, w_router, tm=512):
    t, d = x.shape
    w = jnp.pad(w_router, ((0, 0), (0, LANES - w_router.shape[1])))
    w_hi = w.astype(BF16)
    w_lo = (w - w_hi.astype(F32)).astype(BF16)
    return pl.pallas_call(
        _router_body,
        grid=(t // tm,),
        in_specs=[pl.BlockSpec((tm, d), lambda i: (i, 0)),
                  pl.BlockSpec((1, d), lambda i: (0, 0)),
                  pl.BlockSpec((d, LANES), lambda i: (0, 0)),
                  pl.BlockSpec((d, LANES), lambda i: (0, 0))],
        out_specs=[pl.BlockSpec((tm, LANES), lambda i: (i, 0)), pl.BlockSpec((tm, d), lambda i: (i, 0))],
        out_shape=[jax.ShapeDtypeStruct((t, LANES), F32), jax.ShapeDtypeStruct((t, d), BF16)],
        compiler_params=_cparams("parallel"),
        name="moe_router",
    )(x, g.reshape(1, d), w_hi, w_lo)


MOE_TILE = 512
MOE_GTILE = 256
MOE_GCHUNK = 1024
MOE_CHUNK = 512
MOE_RBLK = 256


def _count_le(sorted_vals, queries):
    return jnp.sum((sorted_vals[None, :] <= queries[:, None]).astype(jnp.int32), axis=1)


def _moe_plan(route, t):
    e = N_EXPERTS
    tm, ck, rb = MOE_TILE, MOE_CHUNK, MOE_RBLK
    n_tiles = (TOP_K * t) // tm + e
    p = n_tiles * tm
    n_chunks = t // ck
    i32 = jnp.int32
    e1 = route[:, 0].astype(i32)
    e2 = route[:, 1].astype(i32)
    oh1 = jax.nn.one_hot(e1, e, dtype=i32)
    oh2 = jax.nn.one_hot(e2, e, dtype=i32)
    cnt = oh1 + oh2
    incl = jnp.cumsum(cnt, axis=0)
    excl = incl - cnt
    n_e = incl[-1]
    g_e = ((n_e + tm - 1) // tm) * tm
    off_end = jnp.cumsum(g_e)
    off = off_end - g_e
    pos_all = off[None, :] + excl
    pos1 = jnp.sum(pos_all * oh1, axis=1)
    pos2 = jnp.sum(pos_all * oh2, axis=1)
    tile_start = jnp.arange(n_tiles, dtype=i32) * tm
    tile_expert = jnp.minimum(_count_le(off_end, tile_start), e - 1)
    tile_sel = jax.nn.one_hot(tile_expert, e, dtype=i32)
    tile_used = (tile_start < jnp.sum(tile_sel * (off + n_e)[None, :], axis=1)).astype(i32)
    gm = MOE_GTILE
    n_gt = p // gm
    gt_start = jnp.arange(n_gt, dtype=i32) * gm
    sel = jax.nn.one_hot(jnp.minimum(_count_le(off_end, gt_start), e - 1), e, dtype=i32)
    r0 = gt_start - jnp.sum(sel * off[None, :], axis=1)
    n_sel = jnp.sum(sel * n_e[None, :], axis=1)
    has_rows = r0 < n_sel
    r1 = jnp.minimum(r0 + gm, n_sel) - 1
    gk = MOE_GCHUNK
    n_gchunks = t // gk
    g_ends = incl[gk - 1::gk]
    ends_sel = jnp.sum(g_ends[None, :, :] * sel[:, None, :], axis=2)
    c_lo = jnp.where(has_rows, jnp.sum((ends_sel <= r0[:, None]).astype(i32), axis=1), 0)
    c_hi = jnp.where(has_rows, jnp.sum((ends_sel <= r1[:, None]).astype(i32), axis=1), 0)
    n_it = c_hi - c_lo + 1
    it_end = jnp.cumsum(n_it)
    it_start = it_end - n_it
    wmax = n_gt + e * n_gchunks
    w_idx = jnp.arange(wmax, dtype=i32)
    g_tile = jnp.minimum(_count_le(it_end, w_idx), n_gt - 1)
    g_valid = w_idx < it_end[-1]
    g_chunk = jnp.where(g_valid, c_lo[g_tile] + (w_idx - it_start[g_tile]), c_hi[n_gt - 1])
    g_first = (g_valid & (w_idx == it_start[g_tile])).astype(i32)
    g_last = (g_valid & (w_idx == it_end[g_tile] - 1)).astype(i32)
    gather = (g_tile, g_chunk.astype(i32), g_first, g_last, g_valid.astype(i32))
    lo = off[None, :] + excl[::ck]
    hi = off[None, :] + incl[ck - 1::ck]
    nblk = jnp.where(hi > lo, (hi - 1) // rb - lo // rb + 1, 0).reshape(-1)
    blk_lo = (lo // rb).reshape(-1)
    cb_end = jnp.cumsum(nblk)
    cb_start = cb_end - nblk
    cmax = p // rb + e * n_chunks
    c_idx = jnp.arange(cmax, dtype=i32)
    pair = jnp.minimum(_count_le(cb_end, c_idx), nblk.shape[0] - 1)
    c_valid = c_idx < cb_end[-1]
    c_tile = jnp.where(c_valid, pair // e, n_chunks - 1)
    c_blk = jnp.where(c_valid, blk_lo[pair] + (c_idx - cb_start[pair]), 0)
    per_tile = jnp.sum(nblk.reshape(n_chunks, e), axis=1)
    t_end = jnp.cumsum(per_tile)
    c_first = (c_valid & (c_idx == (t_end - per_tile)[c_tile])).astype(i32)
    c_last = (c_valid & (c_idx == t_end[c_tile] - 1)).astype(i32)
    combine = (c_tile.astype(i32), c_blk.astype(i32), c_first, c_last, c_valid.astype(i32))
    return dict(n_tiles=n_tiles, p=p, tile_expert=tile_expert.astype(i32), tile_used=tile_used, pos1=pos1, pos2=pos2,
                gather=gather, combine=combine)


MOE_IN_BUFS = 3
MOE_OUT_BUFS = 2


def _moe_gather_body(tile_ref, chunk_ref, first_ref, last_ref, count_ref, meta_hbm, h_hbm, xs_hbm, wr_hbm,
                     hbuf, mbuf, obuf, wbuf, acc_ref, wacc_ref, in_sem, out_sem):
    n = count_ref[0]
    nin, nout = MOE_IN_BUFS, MOE_OUT_BUFS

    def reads(w, slot):
        c = chunk_ref[w]
        return (pltpu.make_async_copy(h_hbm.at[c], hbuf.at[slot], in_sem.at[0, slot]),
                pltpu.make_async_copy(meta_hbm.at[c], mbuf.at[slot], in_sem.at[1, slot]))

    def writes(tile, slot):
        return (pltpu.make_async_copy(obuf.at[slot], xs_hbm.at[tile], out_sem.at[0, slot]),
                pltpu.make_async_copy(wbuf.at[slot], wr_hbm.at[tile], out_sem.at[1, slot]))

    for k in range(nin - 1):
        @pl.when(k < n)
        def _():
            for cp in reads(k, k):
                cp.start()

    def item(w, n_out):
        slot = lax.rem(w, nin)
        ahead = w + (nin - 1)

        @pl.when(ahead < n)
        def _():
            for cp in reads(ahead, lax.rem(ahead, nin)):
                cp.start()

        for cp in reads(w, slot):
            cp.wait()

        @pl.when(first_ref[w] == 1)
        def _():
            acc_ref[...] = jnp.zeros_like(acc_ref)
            wacc_ref[...] = jnp.zeros_like(wacc_ref)

        tile = tile_ref[w]
        meta = mbuf[slot]
        row = (lax.broadcasted_iota(jnp.int32, (MOE_GTILE, MOE_GCHUNK), 0) + tile * MOE_GTILE).astype(F32)
        hit1 = row == meta[0:1, :]
        hit2 = row == meta[1:2, :]
        onehot = jnp.where(hit1, 1.0, jnp.where(hit2, 1.0, 0.0)).astype(BF16)
        acc_ref[...] += _dot(onehot, hbuf[slot])
        wts = jnp.where(hit1, meta[2:3, :], jnp.where(hit2, meta[3:4, :], 0.0))
        wacc_ref[...] += jnp.sum(wts, axis=1, keepdims=True)

        is_last = last_ref[w] == 1

        @pl.when(is_last)
        def _():
            oslot = lax.rem(n_out, nout)

            @pl.when(n_out >= nout)
            def _():
                for cp in writes(tile, oslot):
                    cp.wait()
            obuf[oslot] = acc_ref[...].astype(obuf.dtype)
            wbuf[oslot] = jnp.broadcast_to(wacc_ref[...], wbuf.shape[1:])
            for cp in writes(tile, oslot):
                cp.start()
        return n_out + jnp.where(is_last, 1, 0)

    n_out = lax.fori_loop(0, n, item, 0)
    for k in range(1, nout + 1):
        @pl.when(n_out >= k)
        def _():
            for cp in writes(0, lax.rem(n_out - k, nout)):
                cp.wait()


def _moe_ffn_body(exp_ref, used_ref, x_ref, wr_ref, wg_ref, wu_ref, wd_ref, o_ref, acc_ref):
    f = pl.program_id(1)

    @pl.when(f == 0)
    def _():
        acc_ref[...] = jnp.zeros_like(acc_ref)

    @pl.when(used_ref[pl.program_id(0)] == 1)
    def _():
        x = x_ref[...]
        act = (_silu(_dot(x, wg_ref[...])) * _dot(x, wu_ref[...])).astype(BF16)
        acc_ref[...] += _dot(act, wd_ref[...])

    @pl.when(f == pl.num_programs(1) - 1)
    def _():
        o_ref[...] = (acc_ref[...] * wr_ref[:, 0:1]).astype(o_ref.dtype)


def _moe_combine_body(tile_ref, blk_ref, first_ref, last_ref, valid_ref, x_ref, pos_ref, y_ref, gf_ref,
                      o_ref, acc_ref, *, final_norm):
    w = pl.program_id(0)

    @pl.when(first_ref[w] == 1)
    def _():
        acc_ref[...] = x_ref[...]

    @pl.when(valid_ref[w] == 1)
    def _():
        base = blk_ref[w] * MOE_RBLK
        col = lax.broadcasted_iota(jnp.int32, (MOE_CHUNK, MOE_RBLK), 1) + base
        reps = MOE_RBLK // LANES
        p1 = jnp.concatenate([pos_ref[:, 0:LANES]] * reps, axis=1)
        p2 = jnp.concatenate([pos_ref[:, LANES:2 * LANES]] * reps, axis=1)
        hit = jnp.where(col == p1, 1.0, jnp.where(col == p2, 1.0, 0.0))
        acc_ref[...] += _dot(hit.astype(BF16), y_ref[...])

    @pl.when(last_ref[w] == 1)
    def _():
        o_ref[...] = _rms(acc_ref[...], gf_ref[...]) if final_norm else acc_ref[...]


def moe_ffn(x, h, route, w_gate, w_up, w_down, final_gain=None, tf=1408):
    t, d = x.shape
    f = w_gate.shape[2]
    tm, ck, rb = MOE_TILE, MOE_CHUNK, MOE_RBLK
    plan = _moe_plan(route, t)
    n_tiles, p = plan["n_tiles"], plan["p"]
    assert p < 2 ** 24, "row positions are carried exactly in f32"
    meta = jnp.stack([plan["pos1"].astype(F32), plan["pos2"].astype(F32), route[:, 2], route[:, 3]]
                     + [jnp.zeros((t,), F32)] * (SUBLANES - 4))
    pos_lanes = jnp.concatenate([jnp.broadcast_to(plan["pos1"][:, None], (t, LANES)),
                                 jnp.broadcast_to(plan["pos2"][:, None], (t, LANES))], axis=1)

    gm, gk = MOE_GTILE, MOE_GCHUNK
    g_tile, g_chunk, g_first, g_last, g_valid = plan["gather"]
    smem = pl.BlockSpec(memory_space=pltpu.SMEM)
    hbm = pl.BlockSpec(memory_space=pl.ANY)
    xs, w_row = pl.pallas_call(
        _moe_gather_body,
        in_specs=[smem, smem, smem, smem, smem, hbm, hbm],
        out_specs=[hbm, hbm],
        out_shape=[jax.ShapeDtypeStruct((p // gm, gm, d), BF16), jax.ShapeDtypeStruct((p // gm, gm, LANES), F32)],
        scratch_shapes=[pltpu.VMEM((MOE_IN_BUFS, gk, d), BF16), pltpu.VMEM((MOE_IN_BUFS, SUBLANES, gk), F32),
                        pltpu.VMEM((MOE_OUT_BUFS, gm, d), BF16), pltpu.VMEM((MOE_OUT_BUFS, gm, LANES), F32),
                        pltpu.VMEM((gm, d), F32), pltpu.VMEM((gm, 1), F32),
                        pltpu.SemaphoreType.DMA((2, MOE_IN_BUFS)), pltpu.SemaphoreType.DMA((2, MOE_OUT_BUFS))],
        compiler_params=pltpu.CompilerParams(vmem_limit_bytes=VMEM_LIMIT),
        name="moe_gather",
    )(g_tile, g_chunk, g_first, g_last, jnp.sum(g_valid).reshape(1),
      meta.reshape(SUBLANES, t // gk, gk).transpose(1, 0, 2), h.reshape(t // gk, gk, d))
    xs = xs.reshape(p, d)
    w_row = w_row.reshape(p, LANES)

    ys = pl.pallas_call(
        _moe_ffn_body,
        grid_spec=pltpu.PrefetchScalarGridSpec(
            num_scalar_prefetch=2,
            grid=(n_tiles, f // tf),
            in_specs=[pl.BlockSpec((tm, d), lambda i, k, ex, us: (i, 0)),
                      pl.BlockSpec((tm, LANES), lambda i, k, ex, us: (i, 0)),
                      pl.BlockSpec((None, d, tf), lambda i, k, ex, us: (ex[i], 0, k)),
                      pl.BlockSpec((None, d, tf), lambda i, k, ex, us: (ex[i], 0, k)),
                      pl.BlockSpec((None, tf, d), lambda i, k, ex, us: (ex[i], k, 0))],
            out_specs=pl.BlockSpec((tm, d), lambda i, k, ex, us: (i, 0)),
            scratch_shapes=[pltpu.VMEM((tm, d), F32)]),
        out_shape=jax.ShapeDtypeStruct((p, d), BF16),
        compiler_params=_cparams("parallel", "arbitrary"),
        name="moe_ffn",
    )(plan["tile_expert"], plan["tile_used"], xs, w_row, w_gate, w_up, w_down)

    final_norm = final_gain is not None
    gain = (final_gain if final_norm else jnp.ones((d,), F32)).reshape(1, d)
    return pl.pallas_call(
        functools.partial(_moe_combine_body, final_norm=final_norm),
        grid_spec=pltpu.PrefetchScalarGridSpec(
            num_scalar_prefetch=5,
            grid=(plan["combine"][0].shape[0],),
            in_specs=[pl.BlockSpec((ck, d), lambda w, tl, bk, fi, la, va: (tl[w], 0)),
                      pl.BlockSpec((ck, 2 * LANES), lambda w, tl, bk, fi, la, va: (tl[w], 0)),
                      pl.BlockSpec((rb, d), lambda w, tl, bk, fi, la, va: (bk[w], 0)),
                      pl.BlockSpec((1, d), lambda w, tl, bk, fi, la, va: (0, 0))],
            out_specs=pl.BlockSpec((ck, d), lambda w, tl, bk, fi, la, va: (tl[w], 0)),
            scratch_shapes=[pltpu.VMEM((ck, d), F32)]),
        out_shape=jax.ShapeDtypeStruct((t, d), F32),
        compiler_params=_cparams("arbitrary"),
        name="moe_combine",
    )(*plan["combine"], x, pos_lanes, ys, gain)


def _rmsnorm_body(x_ref, g_ref, o_ref):
    o_ref[...] = _rms(x_ref[...], g_ref[...])


def rmsnorm(x, g, tm=1024):
    t, d = x.shape
    tm = min(tm, t)
    return pl.pallas_call(
        _rmsnorm_body,
        grid=(t // tm,),
        in_specs=[pl.BlockSpec((tm, d), lambda i: (i, 0)), pl.BlockSpec((1, d), lambda i: (0, 0))],
        out_specs=pl.BlockSpec((tm, d), lambda i: (i, 0)),
        out_shape=jax.ShapeDtypeStruct((t, d), F32),
        compiler_params=_cparams("parallel"),
        name="rmsnorm",
    )(x, g.reshape(1, d))


def _in_proj_slices(w):
    sizes = (MOBA_WIDTH, MOBA_WIDTH, MOBA_WIDTH, SSM_WIDTH, GDN_WIDTH, GDN_WIDTH, GDN_WIDTH,
             GDN_HEADS, GDN_HEADS, GDN_WIDTH, w.shape[0], w.shape[0], w.shape[0])
    parts, start = [], 0
    for size in sizes:
        parts.append(w[:, start:start + size])
        start += size
    return parts


def kernel(x, mem, positions, norm_mix, w_in, ssm_a_re, ssm_a_im, ssm_log_dt, ssm_b_re, ssm_b_im, ssm_c_re, ssm_c_im, ssm_d, ssm_w_glu, gdn_conv, gdn_a_log, gdn_dt_bias, gdn_norm, w_up_moba, w_up_gdn, w_out, norm_xa, norm_mem, xa_wq, xa_wk, xa_wv, xa_wo, norm_ffn, ffn_w_gate, ffn_w_up, ffn_w_down, moe_w_router, moe_w_gate, moe_w_up, moe_w_down, norm_final):
    batch, seq, d = x.shape
    depth = w_in.shape[0]
    t = batch * seq
    assert batch == SUBLANES, "the S5 scan packs the batch onto the 8 sublanes of a vreg"
    assert seq % MOBA_BLOCK == 0 and seq % GDN_CHUNK == 0
    nc = seq // GDN_CHUNK
    bf = lambda a: a.astype(BF16)

    xf = x.reshape(t, d)
    memf = mem.reshape(-1, d)
    cos, sin = rope_tables(positions)

    for l in range(depth):
        (wq_m, wk_m, wv_m, wu_s, wq_g, wk_g, wv_g, wa_g, wb_g, wz_g, wg_a, wg_b, wg_c) = _in_proj_slices(w_in[l])
        g_mix = norm_mix[l]
        w_ab = jnp.pad(jnp.concatenate([wa_g, wb_g], axis=1), ((0, 0), (0, LANES - 2 * GDN_HEADS)))
        w32 = jnp.concatenate([wq_m, wk_m, wu_s, wq_g, wk_g, wv_g, w_ab], axis=1)
        w16 = jnp.concatenate([wv_m, wz_g, wg_a, wg_b, wg_c], axis=1)
        p32, p16 = norm_matmul2(xf, g_mix, bf(w32), bf(w16), F32, BF16, tm=2048, tna=640, tnb=1024)
        c_us = 2 * MOBA_WIDTH
        c_qkv = c_us + SSM_WIDTH
        c_ab = c_qkv + 3 * GDN_WIDTH
        u_s = p32[:, c_us:c_us + SSM_WIDTH]
        ab_g = p32[:, c_ab:c_ab + 2 * GDN_HEADS]

        y_a = moba_attention(p32, p16, cos, sin, batch, seq, q_col=0, k_col=MOBA_WIDTH // LANES, v_col=0)

        l_re, l_im, bb_re, bb_im = s5_discretise(ssm_a_re[l], ssm_a_im[l], ssm_log_dt[l], ssm_b_re[l], ssm_b_im[l])
        maps = s5_block_maps(bb_re, bb_im, ssm_c_re[l], ssm_c_im[l])
        u_tm = u_s.reshape(batch, seq, SSM_WIDTH).transpose(1, 0, 2).reshape(t, SSM_WIDTH)
        y_s_tm = s5_gelu(u_tm, maps, l_re, l_im, ssm_d[l])
        y_s = y_s_tm.reshape(seq, batch, SSM_WIDTH).transpose(1, 0, 2).reshape(t, SSM_WIDTH)

        def head_rows(cols):
            return cols.reshape(batch, nc, GDN_CHUNK, GDN_HEADS).transpose(0, 3, 1, 2)
        a_rows = head_rows(ab_g[:, :GDN_HEADS])
        b_rows = head_rows(ab_g[:, GDN_HEADS:2 * GDN_HEADS])
        y_c = gdn_mixer(p32, p16, a_rows, b_rows, gdn_conv[l], gdn_a_log[l], gdn_dt_bias[l], gdn_norm[l],
                        batch, seq, qkv_col=c_qkv // LANES, z_col=MOBA_WIDTH // GDN_WIDTH)

        k_x = norm_matmul(memf, norm_mem[l], bf(xa_wk[l]), BF16)
        v_x = norm_matmul(memf, norm_mem[l], bf(xa_wv[l]), BF16)
        xf = merge_cross_attention(xf, y_a, y_s, y_c, p16, bf(w_up_moba[l]), bf(ssm_w_glu[l]), bf(w_up_gdn[l]),
                                   bf(w_out[l]), norm_xa[l], bf(xa_wq[l]), k_x, v_x, bf(xa_wo[l]), batch, seq,
                                   gate_col=(MOBA_WIDTH + GDN_WIDTH) // d)

        if l % 2 == 0:
            i = l // 2
            xf = ffn(xf, norm_ffn[l], bf(ffn_w_gate[i]), bf(ffn_w_up[i]), bf(ffn_w_down[i]))
        else:
            i = l // 2
            route, h_moe = moe_router(xf, norm_ffn[l], moe_w_router[i])
            xf = moe_ffn(xf, h_moe, route, bf(moe_w_gate[i]), bf(moe_w_up[i]), bf(moe_w_down[i]),
                         final_gain=norm_final if l == depth - 1 else None)

    if depth % 2 == 1:
        xf = rmsnorm(xf, norm_final)
    return xf.reshape(batch, seq, d)
```

```python
import functools

import numpy as np
import jax
import jax.numpy as jnp
from jax import lax
from jax.experimental import pallas as pl
from jax.experimental.pallas import tpu as pltpu

F32 = jnp.float32
BF16 = jnp.bfloat16
HIGHEST = lax.Precision.HIGHEST

EPS = 1e-6
NEG_INF = -1e30
MOBA_HEADS = 8
MOBA_HEAD_DIM = 64
MOBA_WIDTH = MOBA_HEADS * MOBA_HEAD_DIM
MOBA_BLOCK = 256
MOBA_TOPK = 3
ROPE_THETA = 10000.0
SSM_WIDTH = 512
SSM_GROUP = 16
SSM_GROUPS = SSM_WIDTH // SSM_GROUP
SSM_STATE = 64
SSM_STATES = SSM_GROUPS * SSM_STATE
GDN_HEADS = 4
GDN_HEAD_DIM = 128
GDN_WIDTH = GDN_HEADS * GDN_HEAD_DIM
GDN_CONV = 4
GDN_CHUNK = 64
XA_HEADS = 4
XA_HEAD_DIM = 128
XA_WIDTH = XA_HEADS * XA_HEAD_DIM
N_EXPERTS = 8
TOP_K = 2

LANES = 128
SUBLANES = 8
VMEM_LIMIT = 56 * 1024 * 1024


def _cparams(*sem):
    return pltpu.CompilerParams(dimension_semantics=sem, vmem_limit_bytes=VMEM_LIMIT)


def _rms(x, g):
    return x * lax.rsqrt(jnp.mean(x * x, axis=-1, keepdims=True) + EPS) * g


def _sigmoid(x):
    return 1.0 / (1.0 + jnp.exp(-x))


def _silu(x):
    return x * _sigmoid(x)


def _dot(a, b):
    return jnp.dot(a, b, preferred_element_type=F32)


def _dot_nt(a, b):
    return lax.dot_general(a, b, (((1,), (1,)), ((), ())), preferred_element_type=F32)


def _norm_matmul_body(x_ref, g_ref, w_ref, o_ref, h_ref):
    @pl.when(pl.program_id(1) == 0)
    def _():
        h_ref[...] = _rms(x_ref[...], g_ref[...]).astype(BF16)

    o_ref[...] = _dot(h_ref[...], w_ref[...]).astype(o_ref.dtype)


def norm_matmul(x, g, w, out_dtype, tm=512, tn=512):
    t, k = x.shape
    n = w.shape[1]
    tm, tn = min(tm, t), min(tn, n)
    return pl.pallas_call(
        _norm_matmul_body,
        grid=(t // tm, n // tn),
        in_specs=[pl.BlockSpec((tm, k), lambda i, j: (i, 0)),
                  pl.BlockSpec((1, k), lambda i, j: (0, 0)),
                  pl.BlockSpec((k, tn), lambda i, j: (0, j))],
        out_specs=pl.BlockSpec((tm, tn), lambda i, j: (i, j)),
        out_shape=jax.ShapeDtypeStruct((t, n), out_dtype),
        scratch_shapes=[pltpu.VMEM((tm, k), BF16)],
        compiler_params=_cparams("parallel", "arbitrary"),
        name="norm_matmul",
    )(x, g.reshape(1, k), w)


def _norm_matmul2_body(x_ref, g_ref, wa_ref, wb_ref, oa_ref, ob_ref, h_ref, *, na):
    j = pl.program_id(1)

    @pl.when(j == 0)
    def _():
        h_ref[...] = _rms(x_ref[...], g_ref[...]).astype(BF16)

    @pl.when(j < na)
    def _():
        oa_ref[...] = _dot(h_ref[...], wa_ref[...]).astype(oa_ref.dtype)

    @pl.when(j >= na)
    def _():
        ob_ref[...] = _dot(h_ref[...], wb_ref[...]).astype(ob_ref.dtype)


def norm_matmul2(x, g, wa, wb, dtype_a, dtype_b, tm, tna, tnb):
    t, k = x.shape
    tm = min(tm, t)
    na, nb = wa.shape[1] // tna, wb.shape[1] // tnb
    return pl.pallas_call(
        functools.partial(_norm_matmul2_body, na=na),
        grid=(t // tm, na + nb),
        in_specs=[pl.BlockSpec((tm, k), lambda i, j: (i, 0)),
                  pl.BlockSpec((1, k), lambda i, j: (0, 0)),
                  pl.BlockSpec((k, tna), lambda i, j: (0, jnp.minimum(j, na - 1))),
                  pl.BlockSpec((k, tnb), lambda i, j: (0, jnp.maximum(j - na, 0)))],
        out_specs=[pl.BlockSpec((tm, tna), lambda i, j: (i, jnp.minimum(j, na - 1))),
                   pl.BlockSpec((tm, tnb), lambda i, j: (i, jnp.maximum(j - na, 0)))],
        out_shape=[jax.ShapeDtypeStruct((t, wa.shape[1]), dtype_a), jax.ShapeDtypeStruct((t, wb.shape[1]), dtype_b)],
        scratch_shapes=[pltpu.VMEM((tm, k), BF16)],
        compiler_params=_cparams("parallel", "arbitrary"),
        name="norm_matmul2",
    )(x, g.reshape(1, k), wa, wb)


def _rope_body(pos_ref, inv_ref, cos_ref, sin_ref):
    ang = pos_ref[...] * inv_ref[...]
    lane = lax.broadcasted_iota(jnp.int32, ang.shape, 1)
    first_half = (lane % MOBA_HEAD_DIM) < (MOBA_HEAD_DIM // 2)
    cos_ref[...] = jnp.cos(ang)
    s = jnp.sin(ang)
    sin_ref[...] = jnp.where(first_half, -s, s)


def rope_tables(positions):
    t = positions.size
    half = MOBA_HEAD_DIM // 2
    inv = (1.0 / (np.float32(ROPE_THETA) ** (np.arange(0, MOBA_HEAD_DIM, 2, dtype=np.float32)
                                             / np.float32(MOBA_HEAD_DIM)))).astype(np.float32)
    inv_row = jnp.asarray(np.tile(inv, LANES // half).reshape(1, LANES))
    pos = positions.astype(F32).reshape(t, 1)
    ts = min(1024, t)
    return pl.pallas_call(
        _rope_body,
        grid=(t // ts,),
        in_specs=[pl.BlockSpec((ts, 1), lambda i: (i, 0)),
                  pl.BlockSpec((1, LANES), lambda i: (0, 0))],
        out_specs=[pl.BlockSpec((ts, LANES), lambda i: (i, 0))] * 2,
        out_shape=[jax.ShapeDtypeStruct((t, LANES), F32)] * 2,
        compiler_params=_cparams("parallel"),
        name="rope_tables",
    )(pos, inv_row)


def _rope_apply(x, cos, sin_signed):
    lane = lax.broadcasted_iota(jnp.int32, x.shape, 1)
    first_half = (lane % MOBA_HEAD_DIM) < (MOBA_HEAD_DIM // 2)
    partner = jnp.where(first_half, pltpu.roll(x, LANES - MOBA_HEAD_DIM // 2, 1),
                        pltpu.roll(x, MOBA_HEAD_DIM // 2, 1))
    return x * cos + partner * sin_signed


def _moba_body(q_ref, k_ref, v_ref, cq_ref, sq_ref, ck_ref, sk_ref, o_ref, ka_ref, va_ref, km_ref, s_ref,
               mt_ref, acc_ref, *, nb):
    blk = MOBA_BLOCK
    i = pl.program_id(2)

    @pl.when(i == 0)
    def _():
        km_ref[...] = jnp.zeros_like(km_ref)
        lane_b = lax.broadcasted_iota(jnp.int32, (blk, LANES), 1)

        def rope_blk(j, c):
            rows = pl.ds(pl.multiple_of(j * blk, blk), blk)
            kr = _rope_apply(k_ref[rows, :], ck_ref[rows, :], sk_ref[rows, :])
            ka_ref[rows, 0:LANES] = kr.astype(BF16)
            ka_ref[rows, LANES:2 * LANES] = jnp.where(lane_b == j, 1.0, 0.0).astype(BF16)
            km_ref[pl.ds(j, 1), :] = jnp.mean(kr, axis=0, keepdims=True)
            v = v_ref[rows, :].astype(F32)
            va_ref[0, rows, :] = jnp.where(lane_b < MOBA_HEAD_DIM, v,
                                           jnp.where(lane_b == MOBA_HEAD_DIM, 1.0, 0.0)).astype(BF16)
            va_ref[1, rows, :] = jnp.where(lane_b >= MOBA_HEAD_DIM, v,
                                           jnp.where(lane_b == 0, 1.0, 0.0)).astype(BF16)
            return c
        lax.fori_loop(0, nb, rope_blk, 0)

    tq = 2 * blk
    lane = lax.broadcasted_iota(jnp.int32, (1, LANES), 1)
    head_a = lane < MOBA_HEAD_DIM
    q = _rope_apply(q_ref[...], cq_ref[...], sq_ref[...])
    scale = MOBA_HEAD_DIM ** -0.5
    km = km_ref[...]
    nbp = km_ref.shape[0]
    brow = lax.broadcasted_iota(jnp.int32, (nbp, tq), 0)
    qblk = 2 * i + jnp.where(lax.broadcasted_iota(jnp.int32, (1, tq), 1) >= blk, 1, 0)
    heads = (0, 1)
    q_heads = [jnp.where(head_a, q, 0.0), jnp.where(head_a, 0.0, q)]
    km_hi = km.astype(BF16)
    km_lo = (km - km_hi.astype(F32)).astype(BF16)
    gates = []
    for qh in q_heads:
        q_hi = qh.astype(BF16)
        q_lo = (qh - q_hi.astype(F32)).astype(BF16)
        gates.append(_dot_nt(km_hi, q_hi) + (_dot_nt(km_hi, q_lo) + _dot_nt(km_lo, q_hi)))
    q_augs = []
    brow_f = brow.astype(F32)
    for h in heads:
        gate = jnp.where(brow < qblk, gates[h], NEG_INF)
        picked = jnp.zeros((nbp, tq), F32)
        for _ in range(MOBA_TOPK):
            top = jnp.max(gate, axis=0, keepdims=True)
            first = jnp.min(jnp.where(gate == top, brow_f, float(nbp)), axis=0, keepdims=True)
            hit = brow_f == first
            picked = jnp.where(hit, 1.0, picked)
            gate = jnp.where(hit, -jnp.inf, gate)
        keep = ((picked > 0.5) & (brow < qblk)) | (brow == qblk)
        bias_t = jnp.concatenate([jnp.where(keep, 0.0, NEG_INF), jnp.zeros((LANES - nbp, tq), F32)],
                                 axis=0)
        q_augs.append(jnp.concatenate([q_heads[h] * scale, bias_t.T], axis=1).astype(BF16))

    def pair_rows(jp):
        return pl.ds(pl.multiple_of(jp * tq, tq), tq)

    r_idx = lax.broadcasted_iota(jnp.int32, (tq, tq), 0)
    c_idx = lax.broadcasted_iota(jnp.int32, (tq, tq), 1)
    k_diag = ka_ref[pair_rows(i), :]
    s_diag = [_dot_nt(q_augs[h], k_diag) for h in heads]
    for h in heads:
        s_d = jnp.where(c_idx <= r_idx, s_diag[h], NEG_INF)
        s_ref[h, i] = s_d
        m_t = s_d[:, 0:LANES]
        for col in range(LANES, tq, LANES):
            m_t = jnp.maximum(m_t, s_d[:, col:col + LANES])
        mt_ref[h] = m_t

    def loop2(n, body):
        def two(k, c):
            body([2 * k, 2 * k + 1])
            return c
        lax.fori_loop(0, lax.shift_right_logical(n, 1), two, 0)

        @pl.when((n & 1) == 1)
        def _():
            body([n - 1])

    def scores(jps):
        k2s = [ka_ref[pair_rows(jp), :] for jp in jps]
        s2s = [[_dot_nt(q_augs[h], k2) for h in heads] for k2 in k2s]
        for jp, s2h in zip(jps, s2s):
            for h in heads:
                s_ref[h, jp] = s2h[h]
        for h in heads:
            m_t = mt_ref[h]
            for s2h in s2s:
                for col in range(0, tq, LANES):
                    m_t = jnp.maximum(m_t, s2h[h][:, col:col + LANES])
            mt_ref[h] = m_t
    loop2(i, scores)

    ms = [jnp.max(mt_ref[h], axis=-1, keepdims=True) for h in heads]
    for h in heads:
        acc_ref[h] = jnp.zeros((tq, LANES), F32)

    def values(jps):
        ps = [[jnp.exp(s_ref[h, jp] - ms[h]).astype(BF16) for h in heads] for jp in jps]
        pvs = [[_dot(p[h], va_ref[h, pair_rows(jp), :]) for h in heads] for jp, p in zip(jps, ps)]
        for h in heads:
            tot = pvs[0][h]
            for pv in pvs[1:]:
                tot = tot + pv[h]
            acc_ref[h] += tot
    loop2(i + 1, values)
    acc_a, acc_b = acc_ref[0], acc_ref[1]
    out_a = acc_a / acc_a[:, MOBA_HEAD_DIM:MOBA_HEAD_DIM + 1]
    out_b = acc_b / acc_b[:, 0:1]
    o_ref[...] = jnp.where(head_a, out_a, out_b).astype(o_ref.dtype)


def moba_attention(qk, v, cos, sin, batch, seq, q_col=0, k_col=MOBA_WIDTH // LANES, v_col=0):
    nb = seq // MOBA_BLOCK
    assert nb % 2 == 0
    npair = nb // 2
    pairs = MOBA_WIDTH // LANES
    tq = 2 * MOBA_BLOCK
    t = batch * seq
    return pl.pallas_call(
        functools.partial(_moba_body, nb=nb),
        grid=(batch, pairs, npair),
        in_specs=[pl.BlockSpec((tq, LANES), lambda b, p, i: (b * npair + i, q_col + p)),
                  pl.BlockSpec((seq, LANES), lambda b, p, i: (b, k_col + p)),
                  pl.BlockSpec((seq, LANES), lambda b, p, i: (b, v_col + p)),
                  pl.BlockSpec((tq, LANES), lambda b, p, i: (b * npair + i, 0)),
                  pl.BlockSpec((tq, LANES), lambda b, p, i: (b * npair + i, 0)),
                  pl.BlockSpec((seq, LANES), lambda b, p, i: (b, 0)),
                  pl.BlockSpec((seq, LANES), lambda b, p, i: (b, 0))],
        out_specs=pl.BlockSpec((tq, LANES), lambda b, p, i: (b * npair + i, p)),
        out_shape=jax.ShapeDtypeStruct((t, MOBA_WIDTH), BF16),
        scratch_shapes=[pltpu.VMEM((seq, 2 * LANES), BF16),
                        pltpu.VMEM((2, seq, LANES), BF16),
                        pltpu.VMEM((-(-nb // SUBLANES) * SUBLANES, LANES), F32),
                        pltpu.VMEM((2, npair, tq, tq), F32),
                        pltpu.VMEM((2, tq, LANES), F32),
                        pltpu.VMEM((2, tq, LANES), F32)],
        compiler_params=_cparams("parallel", "parallel", "arbitrary"),
        name="moba_attention",
    )(qk, qk, v, cos, sin, cos, sin)


def _s5_disc_body(are_ref, aim_ref, ldt_ref, bre_ref, bim_ref, lre_ref, lim_ref, bbre_ref, bbim_ref):
    a_re, a_im = are_ref[...], aim_ref[...]
    dt = jnp.exp(ldt_ref[...])
    mag = jnp.exp(a_re * dt)
    l_re = mag * jnp.cos(a_im * dt)
    l_im = mag * jnp.sin(a_im * dt)
    lre_ref[...] = l_re
    lim_ref[...] = l_im
    x, y = l_re - 1.0, l_im
    den = a_re * a_re + a_im * a_im
    c_re = (x * a_re + y * a_im) / den
    c_im = (y * a_re - x * a_im) / den
    b_re, b_im = bre_ref[...], bim_ref[...]
    bbre_ref[...] = c_re * b_re - c_im * b_im
    bbim_ref[...] = c_re * b_im + c_im * b_re


def s5_discretise(a_re, a_im, log_dt, b_re, b_im):
    g, p = a_re.shape
    n = b_re.shape[-1]
    col = lambda a: a.reshape(g * p, 1)
    ldt = jnp.broadcast_to(log_dt[:, None], (g, p))
    outs = pl.pallas_call(
        _s5_disc_body,
        out_shape=[jax.ShapeDtypeStruct((g * p, 1), F32)] * 2 + [jax.ShapeDtypeStruct((g * p, n), F32)] * 2,
        name="s5_discretise",
    )(col(a_re), col(a_im), col(ldt), b_re.reshape(g * p, n), b_im.reshape(g * p, n))
    l_re, l_im, bb_re, bb_im = outs
    return l_re.reshape(g, p), l_im.reshape(g, p), bb_re.reshape(g, p, n), bb_im.reshape(g, p, n)


def _block_diag(blocks):
    g, r, c = blocks.shape
    eye = jnp.eye(g, dtype=blocks.dtype)
    return (blocks[:, :, None, :] * eye[:, None, :, None]).reshape(g * r, g * c)


def _s5_body(u_ref, bre_ref, bim_ref, cre_ref, cim_ref, lre_ref, lim_ref, d_ref, y_ref, h_ref, hb_ref, *, steps):
    rows = SUBLANES
    ns = SSM_STATES
    nq = SSM_WIDTH // LANES
    sq = ns // nq

    @pl.when(pl.program_id(0) == 0)
    def _():
        h_ref[...] = jnp.zeros_like(h_ref)

    u = u_ref[...]
    u16 = u.astype(BF16)
    for q in range(nq):
        uq = u16[:, q * LANES:(q + 1) * LANES]
        hb_ref[:, q * sq:(q + 1) * sq] = _dot(uq, bre_ref[q])
        hb_ref[:, ns + q * sq:ns + (q + 1) * sq] = _dot(uq, bim_ref[q])
    l_re, l_im = lre_ref[...], lim_ref[...]

    def step(t, carry):
        h_re, h_im = carry
        r = pl.ds(pl.multiple_of(t * rows, rows), rows)
        n_re = l_re * h_re - l_im * h_im + hb_ref[r, 0:ns]
        n_im = l_re * h_im + l_im * h_re + hb_ref[r, ns:2 * ns]
        hb_ref[r, 0:ns] = n_re
        hb_ref[r, ns:2 * ns] = n_im
        return n_re, n_im

    h_re, h_im = lax.fori_loop(0, steps, step, (h_ref[:, 0:ns], h_ref[:, ns:2 * ns]), unroll=4)
    h_ref[:, 0:ns] = h_re
    h_ref[:, ns:2 * ns] = h_im
    ys = []
    for q in range(nq):
        ys.append(_dot(hb_ref[:, q * sq:(q + 1) * sq].astype(BF16), cre_ref[q])
                  + _dot(hb_ref[:, ns + q * sq:ns + (q + 1) * sq].astype(BF16), cim_ref[q]))
    y = jnp.concatenate(ys, axis=1) + d_ref[...] * u
    y_ref[...] = jax.nn.gelu(y).astype(y_ref.dtype)


def s5_block_maps(bb_re, bb_im, c_re, c_im):
    nq = SSM_WIDTH // LANES
    gq = SSM_GROUPS // nq
    per_q = lambda a: jnp.stack([_block_diag(a[q * gq:(q + 1) * gq].transpose(0, 2, 1)) for q in range(nq)])
    return (per_q(bb_re).astype(BF16), per_q(bb_im).astype(BF16),
            per_q(c_re).astype(BF16), per_q(-c_im).astype(BF16))


def s5_gelu(u_tm, maps, l_re, l_im, d_skip, steps=64):
    rows = u_tm.shape[0]
    blk = steps * SUBLANES
    ns2 = 2 * SSM_STATES
    full3 = lambda a: pl.BlockSpec(a.shape, lambda c: (0, 0, 0))
    return pl.pallas_call(
        functools.partial(_s5_body, steps=steps),
        grid=(rows // blk,),
        in_specs=[pl.BlockSpec((blk, SSM_WIDTH), lambda c: (c, 0)),
                  full3(maps[0]), full3(maps[1]), full3(maps[2]), full3(maps[3]),
                  pl.BlockSpec((1, SSM_STATES), lambda c: (0, 0)),
                  pl.BlockSpec((1, SSM_STATES), lambda c: (0, 0)),
                  pl.BlockSpec((1, SSM_WIDTH), lambda c: (0, 0))],
        out_specs=pl.BlockSpec((blk, SSM_WIDTH), lambda c: (c, 0)),
        out_shape=jax.ShapeDtypeStruct((rows, SSM_WIDTH), BF16),
        scratch_shapes=[pltpu.VMEM((SUBLANES, ns2), F32),
                        pltpu.VMEM((blk, ns2), F32)],
        compiler_params=_cparams("arbitrary"),
        name="s5_scan",
    )(u_tm, *maps, l_re.reshape(1, -1), l_im.reshape(1, -1), d_skip.reshape(1, -1))


GDN_GROUP = 16


def _gdn_prep_body(q_ref, k_ref, v_ref, a_ref, b_ref, wq_ref, wk_ref, wv_ref, alog_ref, dtb_ref,
                   u_ref, w_ref, qd_ref, kd_ref, qk_ref, gl_ref, qs_ref, ks_ref, vs_ref,
                   *, seq):
    L = GDN_CHUNK
    G = min(GDN_GROUP, seq // L)
    dk = GDN_HEAD_DIM
    nc = seq // L
    hd = pl.program_id(1)
    cb = 256
    halo = SUBLANES

    def conv_blk(rb, c):
        base = pl.multiple_of(rb * cb, cb)
        prev = pl.multiple_of(jnp.maximum(base - halo, 0), halo)
        for src, wref, dst, norm, mul in ((q_ref, wq_ref, qs_ref, True, dk ** -0.5),
                                          (k_ref, wk_ref, ks_ref, True, 1.0),
                                          (v_ref, wv_ref, vs_ref, False, 1.0)):
            cur = src[pl.ds(base, cb), :]
            head = jnp.where(rb > 0, src[pl.ds(prev, halo), :], 0.0)
            ext = jnp.concatenate([head, cur], axis=0)
            wts = wref[...]
            y = ext[halo:halo + cb] * wts[GDN_CONV - 1:GDN_CONV]
            for tap in range(1, GDN_CONV):
                y = y + ext[halo - tap:halo - tap + cb] * wts[GDN_CONV - 1 - tap:GDN_CONV - tap]
            y = _silu(y)
            if norm:
                y = y * lax.rsqrt(jnp.sum(y * y, axis=-1, keepdims=True) + EPS) * mul
            dst[pl.ds(base, cb), :] = y
        return c
    lax.fori_loop(0, seq // cb, conv_blk, 0)

    gl_rows = G * L
    ri = lax.broadcasted_iota(jnp.int32, (gl_rows, L), 0) % L
    ci = lax.broadcasted_iota(jnp.int32, (gl_rows, L), 1)
    tril = ci <= ri
    strict = ci < ri
    eye = ci == ri
    ri3 = lax.broadcasted_iota(jnp.int32, (G, L, L), 1)
    ci3 = lax.broadcasted_iota(jnp.int32, (G, L, L), 2)
    strict3 = ci3 < ri3
    eye_f = jnp.where(ci3 == ri3, 1.0, 0.0)
    r2 = lax.broadcasted_iota(jnp.int32, (L, L), 0)
    c2 = lax.broadcasted_iota(jnp.int32, (L, L), 1)
    upper_f = jnp.where(r2 <= c2, 1.0, 0.0)
    a_coef = -jnp.exp(jnp.full((1, L), alog_ref[hd], F32))
    dt_bias = dtb_ref[hd]
    hp = functools.partial(jnp.dot, preferred_element_type=F32, precision=HIGHEST)

    def bmm(a, b):
        return jnp.einsum('gij,gjk->gik', a, b, preferred_element_type=F32)

    def bmm_nt(a, b):
        return jnp.einsum('gid,gjd->gij', a, b, preferred_element_type=F32)

    def to_col(rows_b):
        return jnp.sum(jnp.where(eye, rows_b, 0.0), axis=1, keepdims=True)

    def prep(cg, carry):
        c0 = pl.multiple_of(cg * G, G)
        rows = pl.ds(pl.multiple_of(cg * gl_rows, gl_rows), gl_rows)
        qc, kc, vc = qs_ref[rows, :], ks_ref[rows, :], vs_ref[rows, :]
        z = a_ref[pl.ds(c0, G), :] + dt_bias
        softplus = jnp.maximum(z, 0.0) + jnp.log(1.0 + jnp.exp(-jnp.abs(z)))
        g_rows = hp(a_coef * softplus, upper_f)
        beta_rows = _sigmoid(b_ref[pl.ds(c0, G), :])
        spread = lambda r: jnp.concatenate([jnp.broadcast_to(r[c:c + 1, :], (L, L)) for c in range(G)], axis=0)
        g_rb = spread(g_rows)
        g_col = to_col(g_rb)
        beta_col = to_col(spread(beta_rows))
        g_last = g_rb[:, L - 1:L]
        decay = jnp.where(tril, jnp.exp(jnp.where(tril, g_col - g_rb, 0.0)), 0.0)
        k_beta = kc * beta_col
        v_beta = vc * beta_col
        to3 = lambda a: a.reshape(G, L, a.shape[-1])
        kb16, k16, q16 = to3(k_beta.astype(BF16)), to3(kc.astype(BF16)), to3(qc.astype(BF16))
        kk_qk = bmm_nt(jnp.concatenate([kb16, q16], axis=1), k16)
        decay3 = to3(decay)
        a_low = jnp.where(strict3, kk_qk[:, :L] * decay3, 0.0)
        qk = kk_qk[:, L:] * decay3
        t_inv = eye_f - a_low
        a16 = a_low.astype(BF16)
        pw = bmm(a16, a16)
        span = 2
        while 2 * span < L:
            pw16 = pw.astype(BF16)
            both = bmm(jnp.concatenate([t_inv.astype(BF16), pw16], axis=1), pw16)
            t_inv = t_inv + both[:, :L]
            pw = both[:, L:]
            span *= 2
        t_inv = t_inv + bmm(t_inv.astype(BF16), pw.astype(BF16))
        rhs = jnp.concatenate([v_beta.astype(BF16), (k_beta * jnp.exp(g_col)).astype(BF16)], axis=1)
        uw = bmm(t_inv.astype(BF16), to3(rhs)).reshape(gl_rows, 2 * dk)
        u_ref[rows, :] = uw[:, :dk]
        w_ref[rows, :] = uw[:, dk:].astype(BF16)
        qd_ref[rows, :] = (qc * jnp.exp(g_col)).astype(BF16)
        k_dec = to3(kc * jnp.exp(g_last - g_col))
        kd_ref[pl.ds(c0, G)] = jnp.swapaxes(k_dec, 1, 2).astype(BF16)
        qk_ref[pl.ds(c0, G)] = qk.astype(BF16)
        gl_ref[pl.ds(c0, G), :] = jnp.broadcast_to(jnp.exp(g_rows[:, L - 1:L]), (G, LANES))
        return carry
    lax.fori_loop(0, nc // G, prep, 0)


def _gdn_scan_body(u_ref, w_ref, qd_ref, kd_ref, qk_ref, gl_ref, z_ref, gn_ref, o_ref, *st_refs, chunks):
    L = GDN_CHUNK
    d = GDN_HEAD_DIM
    nrow = u_ref.shape[0]

    @pl.when(pl.program_id(1) == 0)
    def _():
        for st in st_refs:
            st[...] = jnp.zeros_like(st)

    gn = gn_ref[...]

    def chunk(c, carry):
        rows = pl.ds(pl.multiple_of(c * L, L), L)
        chains = [(r, h) for r in range(nrow) for h in range(GDN_HEADS)]
        states = [st_refs[r * GDN_HEADS + h][...] for r, h in chains]
        firsts = []
        for (r, h), s in zip(chains, states):
            cols = slice(h * d, (h + 1) * d)
            firsts.append(_dot(jnp.concatenate([w_ref[r, rows, cols], qd_ref[r, rows, cols]], axis=0),
                               s.astype(BF16)))
        vns = []
        for (r, h), both in zip(chains, firsts):
            cols = slice(h * d, (h + 1) * d)
            vns.append((u_ref[r, rows, cols] - both[:L]).astype(BF16))
        outs = [both[L:] + _dot(qk_ref[r, h, c], vn) for (r, h), both, vn in zip(chains, firsts, vns)]
        for (r, h), s, vn in zip(chains, states, vns):
            st_refs[r * GDN_HEADS + h][...] = s * gl_ref[r, h, pl.ds(c, 1), :] + _dot(kd_ref[r, h, c], vn)
        for (r, h), o in zip(chains, outs):
            cols = slice(h * d, (h + 1) * d)
            o_ref[r, rows, cols] = (_rms(o, gn) * _silu(z_ref[r, rows, cols].astype(F32))).astype(o_ref.dtype)
        return carry
    lax.fori_loop(0, chunks, chunk, 0)


def gdn_mixer(qkv, z, a_rows, b_rows, conv_w, a_log, dt_bias, gnorm, batch, seq, ts=1024, qkv_col=0, z_col=0):
    t = batch * seq
    hds = GDN_HEADS
    nc = seq // GDN_CHUNK
    L = GDN_CHUNK
    d = GDN_HEAD_DIM
    seq_spec = lambda off: pl.BlockSpec((seq, d), lambda b, h: (b, qkv_col + off + h))
    w_spec = lambda off: pl.BlockSpec((GDN_CONV, d), lambda b, h: (0, off + h))
    row_spec = pl.BlockSpec((None, None, nc, L), lambda b, h: (b, h, 0, 0))
    smem = pl.BlockSpec(memory_space=pltpu.SMEM)
    head_out = pl.BlockSpec((seq, d), lambda b, h: (b, h))
    u, w, qd, kd, qk, gl = pl.pallas_call(
        functools.partial(_gdn_prep_body, seq=seq),
        grid=(batch, hds),
        in_specs=[seq_spec(0), seq_spec(hds), seq_spec(2 * hds), row_spec, row_spec,
                  w_spec(0), w_spec(hds), w_spec(2 * hds), smem, smem],
        out_specs=[head_out, head_out, head_out,
                   pl.BlockSpec((None, None, nc, d, L), lambda b, h: (b, h, 0, 0, 0)),
                   pl.BlockSpec((None, None, nc, L, L), lambda b, h: (b, h, 0, 0, 0)),
                   pl.BlockSpec((None, None, nc, LANES), lambda b, h: (b, h, 0, 0))],
        out_shape=[jax.ShapeDtypeStruct((t, GDN_WIDTH), F32),
                   jax.ShapeDtypeStruct((t, GDN_WIDTH), BF16),
                   jax.ShapeDtypeStruct((t, GDN_WIDTH), BF16),
                   jax.ShapeDtypeStruct((batch, hds, nc, d, L), BF16),
                   jax.ShapeDtypeStruct((batch, hds, nc, L, L), BF16),
                   jax.ShapeDtypeStruct((batch, hds, nc, LANES), F32)],
        scratch_shapes=[pltpu.VMEM((seq, d), F32), pltpu.VMEM((seq, d), F32), pltpu.VMEM((seq, d), F32)],
        compiler_params=_cparams("parallel", "parallel"),
        name="gdn_prep",
    )(qkv, qkv, qkv, a_rows, b_rows, conv_w, conv_w, conv_w, a_log, dt_bias)

    ts = min(ts, seq)
    spb = seq // ts
    cps = ts // L
    nrow = 2 if batch % 2 == 0 else 1
    by_row = lambda a: a.reshape(batch, seq, a.shape[-1])
    tok = pl.BlockSpec((nrow, ts, GDN_WIDTH), lambda b, s: (b, s, 0))
    out = pl.pallas_call(
        functools.partial(_gdn_scan_body, chunks=cps),
        grid=(batch // nrow, spb),
        in_specs=[tok, tok, tok,
                  pl.BlockSpec((nrow, hds, cps, d, L), lambda b, s: (b, 0, s, 0, 0)),
                  pl.BlockSpec((nrow, hds, cps, L, L), lambda b, s: (b, 0, s, 0, 0)),
                  pl.BlockSpec((nrow, hds, cps, LANES), lambda b, s: (b, 0, s, 0)),
                  pl.BlockSpec((nrow, ts, GDN_WIDTH), lambda b, s: (b, s, z_col)),
                  pl.BlockSpec((1, d), lambda b, s: (0, 0))],
        out_specs=tok,
        out_shape=jax.ShapeDtypeStruct((batch, seq, GDN_WIDTH), BF16),
        scratch_shapes=[pltpu.VMEM((d, d), F32)] * (nrow * hds),
        compiler_params=_cparams("parallel", "arbitrary"),
        name="gdn_scan",
    )(by_row(u), by_row(w), by_row(qd), kd, qk, gl, by_row(z), gnorm.reshape(1, d))
    return out.reshape(t, GDN_WIDTH)


def _merge_xattn_body(x_ref, ya_ref, ys_ref, yc_ref, ga_ref, gb_ref, gc_ref, wa_ref, wg_ref, wc_ref, wo_ref,
                      gx_ref, wq_ref, k_ref, v_ref, wxo_ref, o_ref):
    d = x_ref.shape[1]
    y_a = _dot(ya_ref[...], wa_ref[...])
    glu = _dot(ys_ref[...], wg_ref[...])
    y_b = glu[:, :d] * _sigmoid(glu[:, d:])
    y_c = _dot(yc_ref[...], wc_ref[...])
    gate = lambda r: _sigmoid(r[...].astype(F32))
    merged = gate(ga_ref) * y_a + gate(gb_ref) * y_b + gate(gc_ref) * y_c
    x1 = x_ref[...] + _dot(merged.astype(BF16), wo_ref[...])

    h = _rms(x1, gx_ref[...]).astype(BF16)
    q = _dot(h, wq_ref[...]).astype(BF16)
    cols = [slice(hd * XA_HEAD_DIM, (hd + 1) * XA_HEAD_DIM) for hd in range(XA_HEADS)]
    scores = [_dot_nt(q[:, c], k_ref[:, c]) * (XA_HEAD_DIM ** -0.5) for c in cols]
    probs = []
    for s in scores:
        p = jnp.exp(s - jnp.max(s, axis=-1, keepdims=True))
        probs.append((p / jnp.sum(p, axis=-1, keepdims=True)).astype(BF16))
    outs = [_dot(p, v_ref[:, c]) for p, c in zip(probs, cols)]
    o = jnp.concatenate(outs, axis=1).astype(BF16)
    o_ref[...] = x1 + _dot(o, wxo_ref[...])


def merge_cross_attention(x, y_a, y_s, y_c, gates, w_a, w_glu, w_c, w_o, g_xa, wq, k, v, wo_xa, batch, seq,
                          tm=512, gate_col=0):
    t, d = x.shape
    m = k.shape[0] // batch
    spb = seq // tm
    row = lambda n: pl.BlockSpec((tm, n), lambda i: (i, 0))
    gate = lambda j: pl.BlockSpec((tm, d), lambda i: (i, gate_col + j))
    full = lambda a: pl.BlockSpec(a.shape, lambda i: (0, 0))
    mem = pl.BlockSpec((m, XA_WIDTH), lambda i: (i // spb, 0))
    return pl.pallas_call(
        _merge_xattn_body,
        grid=(t // tm,),
        in_specs=[row(d), row(y_a.shape[1]), row(y_s.shape[1]), row(y_c.shape[1]), gate(0), gate(1), gate(2),
                  full(w_a), full(w_glu), full(w_c), full(w_o),
                  pl.BlockSpec((1, d), lambda i: (0, 0)), full(wq), mem, mem, full(wo_xa)],
        out_specs=row(d),
        out_shape=jax.ShapeDtypeStruct((t, d), F32),
        compiler_params=_cparams("parallel"),
        name="merge_cross_attention",
    )(x, y_a, y_s, y_c, gates, gates, gates, w_a, w_glu, w_c, w_o, g_xa.reshape(1, d), wq, k, v, wo_xa)


def _ffn_body(x_ref, g_ref, wg_ref, wu_ref, wd_ref, o_ref, h_ref, acc_ref):
    f = pl.program_id(1)

    @pl.when(f == 0)
    def _():
        h_ref[...] = _rms(x_ref[...], g_ref[...]).astype(BF16)
        acc_ref[...] = jnp.zeros_like(acc_ref)

    h = h_ref[...]
    act = (_silu(_dot(h, wg_ref[...])) * _dot(h, wu_ref[...])).astype(BF16)
    acc_ref[...] += _dot(act, wd_ref[...])

    @pl.when(f == pl.num_programs(1) - 1)
    def _():
        o_ref[...] = x_ref[...] + acc_ref[...]


def ffn(x, g, w_gate, w_up, w_down, tm=512, tf=1408):
    t, d = x.shape
    f = w_gate.shape[1]
    return pl.pallas_call(
        _ffn_body,
        grid=(t // tm, f // tf),
        in_specs=[pl.BlockSpec((tm, d), lambda i, k: (i, 0)),
                  pl.BlockSpec((1, d), lambda i, k: (0, 0)),
                  pl.BlockSpec((d, tf), lambda i, k: (0, k)),
                  pl.BlockSpec((d, tf), lambda i, k: (0, k)),
                  pl.BlockSpec((tf, d), lambda i, k: (k, 0))],
        out_specs=pl.BlockSpec((tm, d), lambda i, k: (i, 0)),
        out_shape=jax.ShapeDtypeStruct((t, d), F32),
        scratch_shapes=[pltpu.VMEM((tm, d), BF16), pltpu.VMEM((tm, d), F32)],
        compiler_params=_cparams("parallel", "arbitrary"),
        name="ffn",
    )(x, g.reshape(1, d), w_gate, w_up, w_down)


def _router_body(x_ref, g_ref, whi_ref, wlo_ref, r_ref, h_ref):
    h = _rms(x_ref[...], g_ref[...])
    h_hi = h.astype(BF16)
    h_ref[...] = h_hi
    h_lo = (h - h_hi.astype(F32)).astype(BF16)
    logits = _dot(h_hi, whi_ref[...]) + (_dot(h_hi, wlo_ref[...]) + _dot(h_lo, whi_ref[...]))
    lane = lax.broadcasted_iota(jnp.int32, logits.shape, 1).astype(F32)
    logits = jnp.where(lane < N_EXPERTS, logits, -jnp.inf)
    v1 = jnp.max(logits, axis=-1, keepdims=True)
    i1 = jnp.min(jnp.where(logits == v1, lane, float(LANES)), axis=-1, keepdims=True)
    rest = jnp.where(lane == i1, -jnp.inf, logits)
    v2 = jnp.max(rest, axis=-1, keepdims=True)
    i2 = jnp.min(jnp.where(rest == v2, lane, float(LANES)), axis=-1, keepdims=True)
    e2 = jnp.exp(v2 - v1)
    w1 = 1.0 / (1.0 + e2)
    w2 = e2 / (1.0 + e2)
    r_ref[...] = (jnp.where(lane == 0.0, i1, 0.0) + jnp.where(lane == 1.0, i2, 0.0)
                  + jnp.where(lane == 2.0, w1, 0.0) + jnp.where(lane == 3.0, w2, 0.0))


def moe_router(x, g, w_router, tm=512):
    t, d = x.shape
    w = jnp.pad(w_router, ((0, 0), (0, LANES - w_router.shape[1])))
    w_hi = w.astype(BF16)
    w_lo = (w - w_hi.astype(F32)).astype(BF16)
    return pl.pallas_call(
        _router_body,
        grid=(t // tm,),
        in_specs=[pl.BlockSpec((tm, d), lambda i: (i, 0)),
                  pl.BlockSpec((1, d), lambda i: (0, 0)),
                  pl.BlockSpec((d, LANES), lambda i: (0, 0)),
                  pl.BlockSpec((d, LANES), lambda i: (0, 0))],
        out_specs=[pl.BlockSpec((tm, LANES), lambda i: (i, 0)), pl.BlockSpec((tm, d), lambda i: (i, 0))],
        out_shape=[jax.ShapeDtypeStruct((t, LANES), F32), jax.ShapeDtypeStruct((t, d), BF16)],
        compiler_params=_cparams("parallel"),
        name="moe_router",
    )(x, g.reshape(1, d), w_hi, w_lo)


MOE_TILE = 512
MOE_GTILE = 256
MOE_GCHUNK = 1024
MOE_CHUNK = 512
MOE_RBLK = 256


def _count_le(sorted_vals, queries):
    return jnp.sum((sorted_vals[None, :] <= queries[:, None]).astype(jnp.int32), axis=1)


def _moe_plan(route, t):
    e = N_EXPERTS
    tm, ck, rb = MOE_TILE, MOE_CHUNK, MOE_RBLK
    n_tiles = (TOP_K * t) // tm + e
    p = n_tiles * tm
    n_chunks = t // ck
    i32 = jnp.int32
    e1 = route[:, 0].astype(i32)
    e2 = route[:, 1].astype(i32)
    oh1 = jax.nn.one_hot(e1, e, dtype=i32)
    oh2 = jax.nn.one_hot(e2, e, dtype=i32)
    cnt = oh1 + oh2
    incl = jnp.cumsum(cnt, axis=0)
    excl = incl - cnt
    n_e = incl[-1]
    g_e = ((n_e + tm - 1) // tm) * tm
    off_end = jnp.cumsum(g_e)
    off = off_end - g_e
    pos_all = off[None, :] + excl
    pos1 = jnp.sum(pos_all * oh1, axis=1)
    pos2 = jnp.sum(pos_all * oh2, axis=1)
    tile_start = jnp.arange(n_tiles, dtype=i32) * tm
    tile_expert = jnp.minimum(_count_le(off_end, tile_start), e - 1)
    tile_sel = jax.nn.one_hot(tile_expert, e, dtype=i32)
    tile_used = (tile_start < jnp.sum(tile_sel * (off + n_e)[None, :], axis=1)).astype(i32)
    gm = MOE_GTILE
    n_gt = p // gm
    gt_start = jnp.arange(n_gt, dtype=i32) * gm
    sel = jax.nn.one_hot(jnp.minimum(_count_le(off_end, gt_start), e - 1), e, dtype=i32)
    r0 = gt_start - jnp.sum(sel * off[None, :], axis=1)
    n_sel = jnp.sum(sel * n_e[None, :], axis=1)
    has_rows = r0 < n_sel
    r1 = jnp.minimum(r0 + gm, n_sel) - 1
    gk = MOE_GCHUNK
    n_gchunks = t // gk
    g_ends = incl[gk - 1::gk]
    ends_sel = jnp.sum(g_ends[None, :, :] * sel[:, None, :], axis=2)
    c_lo = jnp.where(has_rows, jnp.sum((ends_sel <= r0[:, None]).astype(i32), axis=1), 0)
    c_hi = jnp.where(has_rows, jnp.sum((ends_sel <= r1[:, None]).astype(i32), axis=1), 0)
    n_it = c_hi - c_lo + 1
    it_end = jnp.cumsum(n_it)
    it_start = it_end - n_it
    wmax = n_gt + e * n_gchunks
    w_idx = jnp.arange(wmax, dtype=i32)
    g_tile = jnp.minimum(_count_le(it_end, w_idx), n_gt - 1)
    g_valid = w_idx < it_end[-1]
    g_chunk = jnp.where(g_valid, c_lo[g_tile] + (w_idx - it_start[g_tile]), c_hi[n_gt - 1])
    g_first = (g_valid & (w_idx == it_start[g_tile])).astype(i32)
    g_last = (g_valid & (w_idx == it_end[g_tile] - 1)).astype(i32)
    gather = (g_tile, g_chunk.astype(i32), g_first, g_last, g_valid.astype(i32))
    lo = off[None, :] + excl[::ck]
    hi = off[None, :] + incl[ck - 1::ck]
    nblk = jnp.where(hi > lo, (hi - 1) // rb - lo // rb + 1, 0).reshape(-1)
    blk_lo = (lo // rb).reshape(-1)
    cb_end = jnp.cumsum(nblk)
    cb_start = cb_end - nblk
    cmax = p // rb + e * n_chunks
    c_idx = jnp.arange(cmax, dtype=i32)
    pair = jnp.minimum(_count_le(cb_end, c_idx), nblk.shape[0] - 1)
    c_valid = c_idx < cb_end[-1]
    c_tile = jnp.where(c_valid, pair // e, n_chunks - 1)
    c_blk = jnp.where(c_valid, blk_lo[pair] + (c_idx - cb_start[pair]), 0)
    per_tile = jnp.sum(nblk.reshape(n_chunks, e), axis=1)
    t_end = jnp.cumsum(per_tile)
    c_first = (c_valid & (c_idx == (t_end - per_tile)[c_tile])).astype(i32)
    c_last = (c_valid & (c_idx == t_end[c_tile] - 1)).astype(i32)
    combine = ((t_end - per_tile).astype(i32), per_tile.astype(i32), c_blk.astype(i32))
    return dict(n_tiles=n_tiles, p=p, tile_expert=tile_expert.astype(i32), tile_used=tile_used, pos1=pos1, pos2=pos2,
                gather=gather, combine=combine)


MOE_IN_BUFS = 3
MOE_OUT_BUFS = 2


def _moe_gather_body(tile_ref, chunk_ref, first_ref, last_ref, count_ref, meta_hbm, h_hbm, xs_hbm, wr_hbm,
                     hbuf, mbuf, obuf, wbuf, acc_ref, wacc_ref, in_sem, out_sem):
    n = count_ref[0]
    nin, nout = MOE_IN_BUFS, MOE_OUT_BUFS

    def reads(w, slot):
        c = chunk_ref[w]
        return (pltpu.make_async_copy(h_hbm.at[c], hbuf.at[slot], in_sem.at[0, slot]),
                pltpu.make_async_copy(meta_hbm.at[c], mbuf.at[slot], in_sem.at[1, slot]))

    def writes(tile, slot):
        return (pltpu.make_async_copy(obuf.at[slot], xs_hbm.at[tile], out_sem.at[0, slot]),
                pltpu.make_async_copy(wbuf.at[slot], wr_hbm.at[tile], out_sem.at[1, slot]))

    for k in range(nin - 1):
        @pl.when(k < n)
        def _():
            for cp in reads(k, k):
                cp.start()

    def item(w, n_out):
        slot = lax.rem(w, nin)
        ahead = w + (nin - 1)

        @pl.when(ahead < n)
        def _():
            for cp in reads(ahead, lax.rem(ahead, nin)):
                cp.start()

        for cp in reads(w, slot):
            cp.wait()

        @pl.when(first_ref[w] == 1)
        def _():
            acc_ref[...] = jnp.zeros_like(acc_ref)
            wacc_ref[...] = jnp.zeros_like(wacc_ref)

        tile = tile_ref[w]
        meta = mbuf[slot]
        row = (lax.broadcasted_iota(jnp.int32, (MOE_GTILE, MOE_GCHUNK), 0) + tile * MOE_GTILE).astype(F32)
        hit1 = row == meta[0:1, :]
        hit2 = row == meta[1:2, :]
        onehot = jnp.where(hit1, 1.0, jnp.where(hit2, 1.0, 0.0)).astype(BF16)
        acc_ref[...] += _dot(onehot, hbuf[slot])
        wts = jnp.where(hit1, meta[2:3, :], jnp.where(hit2, meta[3:4, :], 0.0))
        wacc_ref[...] += jnp.sum(wts, axis=1, keepdims=True)

        is_last = last_ref[w] == 1

        @pl.when(is_last)
        def _():
            oslot = lax.rem(n_out, nout)

            @pl.when(n_out >= nout)
            def _():
                for cp in writes(tile, oslot):
                    cp.wait()
            obuf[oslot] = acc_ref[...].astype(obuf.dtype)
            wbuf[oslot] = jnp.broadcast_to(wacc_ref[...], wbuf.shape[1:])
            for cp in writes(tile, oslot):
                cp.start()
        return n_out + jnp.where(is_last, 1, 0)

    n_out = lax.fori_loop(0, n, item, 0)
    for k in range(1, nout + 1):
        @pl.when(n_out >= k)
        def _():
            for cp in writes(0, lax.rem(n_out - k, nout)):
                cp.wait()


def _moe_ffn_body(exp_ref, used_ref, x_ref, wr_ref, wg_ref, wu_ref, wd_ref, o_ref, acc_ref):
    f = pl.program_id(1)

    @pl.when(f == 0)
    def _():
        acc_ref[...] = jnp.zeros_like(acc_ref)

    @pl.when(used_ref[pl.program_id(0)] == 1)
    def _():
        x = x_ref[...]
        act = (_silu(_dot(x, wg_ref[...])) * _dot(x, wu_ref[...])).astype(BF16)
        acc_ref[...] += _dot(act, wd_ref[...])

    @pl.when(f == pl.num_programs(1) - 1)
    def _():
        o_ref[...] = (acc_ref[...] * wr_ref[:, 0:1]).astype(o_ref.dtype)


def _moe_combine_body(start_ref, count_ref, blk_ref, x_ref, pos_ref, gf_ref, ys_hbm, o_ref, ybuf, sem, acc_ref,
                      *, final_norm):
    i = pl.program_id(0)
    s0, n = start_ref[i], count_ref[i]
    n_pairs = lax.shift_right_logical(n + 1, 1)
    rb = MOE_RBLK

    def item_block(j):
        return blk_ref[s0 + jnp.minimum(j, n - 1)]

    def fetch(j, slot, half):
        return pltpu.make_async_copy(ys_hbm.at[item_block(j)], ybuf.at[slot, half], sem.at[slot, half])

    fetch(0, 0, 0).start()
    fetch(1, 0, 1).start()
    acc_ref[...] = x_ref[...]
    col = lax.broadcasted_iota(jnp.int32, (MOE_CHUNK, rb), 1)
    reps = rb // LANES
    p1 = jnp.concatenate([pos_ref[:, 0:LANES]] * reps, axis=1)
    p2 = jnp.concatenate([pos_ref[:, LANES:2 * LANES]] * reps, axis=1)

    def pair(k, carry):
        slot = k & 1

        @pl.when(k + 1 < n_pairs)
        def _():
            fetch(2 * k + 2, 1 - slot, 0).start()
            fetch(2 * k + 3, 1 - slot, 1).start()

        fetch(2 * k, slot, 0).wait()
        fetch(2 * k + 1, slot, 1).wait()
        base0 = item_block(2 * k) * rb
        base1 = jnp.where(2 * k + 1 < n, item_block(2 * k + 1) * rb, -(2 ** 30))
        hits = []
        for base in (base0, base1):
            c = col + base
            hits.append(jnp.where(c == p1, 1.0, jnp.where(c == p2, 1.0, 0.0)).astype(BF16))
        y2 = ybuf[slot].reshape(2 * rb, ybuf.shape[-1])
        acc_ref[...] += _dot(jnp.concatenate(hits, axis=1), y2)
        return carry
    lax.fori_loop(0, n_pairs, pair, 0)
    o_ref[...] = _rms(acc_ref[...], gf_ref[...]) if final_norm else acc_ref[...]


def moe_ffn(x, h, route, w_gate, w_up, w_down, final_gain=None, tf=1408):
    t, d = x.shape
    f = w_gate.shape[2]
    tm, ck, rb = MOE_TILE, MOE_CHUNK, MOE_RBLK
    plan = _moe_plan(route, t)
    n_tiles, p = plan["n_tiles"], plan["p"]
    assert p < 2 ** 24, "row positions are carried exactly in f32"
    meta = jnp.stack([plan["pos1"].astype(F32), plan["pos2"].astype(F32), route[:, 2], route[:, 3]]
                     + [jnp.zeros((t,), F32)] * (SUBLANES - 4))
    pos_lanes = jnp.concatenate([jnp.broadcast_to(plan["pos1"][:, None], (t, LANES)),
                                 jnp.broadcast_to(plan["pos2"][:, None], (t, LANES))], axis=1)

    gm, gk = MOE_GTILE, MOE_GCHUNK
    g_tile, g_chunk, g_first, g_last, g_valid = plan["gather"]
    smem = pl.BlockSpec(memory_space=pltpu.SMEM)
    hbm = pl.BlockSpec(memory_space=pl.ANY)
    xs, w_row = pl.pallas_call(
        _moe_gather_body,
        in_specs=[smem, smem, smem, smem, smem, hbm, hbm],
        out_specs=[hbm, hbm],
        out_shape=[jax.ShapeDtypeStruct((p // gm, gm, d), BF16), jax.ShapeDtypeStruct((p // gm, gm, LANES), F32)],
        scratch_shapes=[pltpu.VMEM((MOE_IN_BUFS, gk, d), BF16), pltpu.VMEM((MOE_IN_BUFS, SUBLANES, gk), F32),
                        pltpu.VMEM((MOE_OUT_BUFS, gm, d), BF16), pltpu.VMEM((MOE_OUT_BUFS, gm, LANES), F32),
                        pltpu.VMEM((gm, d), F32), pltpu.VMEM((gm, 1), F32),
                        pltpu.SemaphoreType.DMA((2, MOE_IN_BUFS)), pltpu.SemaphoreType.DMA((2, MOE_OUT_BUFS))],
        compiler_params=pltpu.CompilerParams(vmem_limit_bytes=VMEM_LIMIT),
        name="moe_gather",
    )(g_tile, g_chunk, g_first, g_last, jnp.sum(g_valid).reshape(1),
      meta.reshape(SUBLANES, t // gk, gk).transpose(1, 0, 2), h.reshape(t // gk, gk, d))
    xs = xs.reshape(p, d)
    w_row = w_row.reshape(p, LANES)

    ys = pl.pallas_call(
        _moe_ffn_body,
        grid_spec=pltpu.PrefetchScalarGridSpec(
            num_scalar_prefetch=2,
            grid=(n_tiles, f // tf),
            in_specs=[pl.BlockSpec((tm, d), lambda i, k, ex, us: (i, 0)),
                      pl.BlockSpec((tm, LANES), lambda i, k, ex, us: (i, 0)),
                      pl.BlockSpec((None, d, tf), lambda i, k, ex, us: (ex[i], 0, k)),
                      pl.BlockSpec((None, d, tf), lambda i, k, ex, us: (ex[i], 0, k)),
                      pl.BlockSpec((None, tf, d), lambda i, k, ex, us: (ex[i], k, 0))],
            out_specs=pl.BlockSpec((tm, d), lambda i, k, ex, us: (i, 0)),
            scratch_shapes=[pltpu.VMEM((tm, d), F32)]),
        out_shape=jax.ShapeDtypeStruct((p, d), BF16),
        compiler_params=_cparams("parallel", "arbitrary"),
        name="moe_ffn",
    )(plan["tile_expert"], plan["tile_used"], xs, w_row, w_gate, w_up, w_down)

    final_norm = final_gain is not None
    gain = (final_gain if final_norm else jnp.ones((d,), F32)).reshape(1, d)
    return pl.pallas_call(
        functools.partial(_moe_combine_body, final_norm=final_norm),
        grid_spec=pltpu.PrefetchScalarGridSpec(
            num_scalar_prefetch=3,
            grid=(t // ck,),
            in_specs=[pl.BlockSpec((ck, d), lambda i, st, ct, bk: (i, 0)),
                      pl.BlockSpec((ck, 2 * LANES), lambda i, st, ct, bk: (i, 0)),
                      pl.BlockSpec((1, d), lambda i, st, ct, bk: (0, 0)),
                      pl.BlockSpec(memory_space=pl.ANY)],
            out_specs=pl.BlockSpec((ck, d), lambda i, st, ct, bk: (i, 0)),
            scratch_shapes=[pltpu.VMEM((2, 2, rb, d), BF16), pltpu.SemaphoreType.DMA((2, 2)),
                            pltpu.VMEM((ck, d), F32)]),
        out_shape=jax.ShapeDtypeStruct((t, d), F32),
        compiler_params=_cparams("arbitrary"),
        name="moe_combine",
    )(*plan["combine"], x, pos_lanes, gain, ys.reshape(p // rb, rb, d))


def _rmsnorm_body(x_ref, g_ref, o_ref):
    o_ref[...] = _rms(x_ref[...], g_ref[...])


def rmsnorm(x, g, tm=1024):
    t, d = x.shape
    tm = min(tm, t)
    return pl.pallas_call(
        _rmsnorm_body,
        grid=(t // tm,),
        in_specs=[pl.BlockSpec((tm, d), lambda i: (i, 0)), pl.BlockSpec((1, d), lambda i: (0, 0))],
        out_specs=pl.BlockSpec((tm, d), lambda i: (i, 0)),
        out_shape=jax.ShapeDtypeStruct((t, d), F32),
        compiler_params=_cparams("parallel"),
        name="rmsnorm",
    )(x, g.reshape(1, d))


def _in_proj_slices(w):
    sizes = (MOBA_WIDTH, MOBA_WIDTH, MOBA_WIDTH, SSM_WIDTH, GDN_WIDTH, GDN_WIDTH, GDN_WIDTH,
             GDN_HEADS, GDN_HEADS, GDN_WIDTH, w.shape[0], w.shape[0], w.shape[0])
    parts, start = [], 0
    for size in sizes:
        parts.append(w[:, start:start + size])
        start += size
    return parts


def kernel(x, mem, positions, norm_mix, w_in, ssm_a_re, ssm_a_im, ssm_log_dt, ssm_b_re, ssm_b_im, ssm_c_re, ssm_c_im, ssm_d, ssm_w_glu, gdn_conv, gdn_a_log, gdn_dt_bias, gdn_norm, w_up_moba, w_up_gdn, w_out, norm_xa, norm_mem, xa_wq, xa_wk, xa_wv, xa_wo, norm_ffn, ffn_w_gate, ffn_w_up, ffn_w_down, moe_w_router, moe_w_gate, moe_w_up, moe_w_down, norm_final):
    batch, seq, d = x.shape
    depth = w_in.shape[0]
    t = batch * seq
    assert batch == SUBLANES, "the S5 scan packs the batch onto the 8 sublanes of a vreg"
    assert seq % MOBA_BLOCK == 0 and seq % GDN_CHUNK == 0
    nc = seq // GDN_CHUNK
    bf = lambda a: a.astype(BF16)

    xf = x.reshape(t, d)
    memf = mem.reshape(-1, d)
    cos, sin = rope_tables(positions)

    for l in range(depth):
        (wq_m, wk_m, wv_m, wu_s, wq_g, wk_g, wv_g, wa_g, wb_g, wz_g, wg_a, wg_b, wg_c) = _in_proj_slices(w_in[l])
        g_mix = norm_mix[l]
        w_ab = jnp.pad(jnp.concatenate([wa_g, wb_g], axis=1), ((0, 0), (0, LANES - 2 * GDN_HEADS)))
        w32 = jnp.concatenate([wq_m, wk_m, wu_s, wq_g, wk_g, wv_g, w_ab], axis=1)
        w16 = jnp.concatenate([wv_m, wz_g, wg_a, wg_b, wg_c], axis=1)
        p32, p16 = norm_matmul2(xf, g_mix, bf(w32), bf(w16), F32, BF16, tm=2048, tna=640, tnb=1024)
        c_us = 2 * MOBA_WIDTH
        c_qkv = c_us + SSM_WIDTH
        c_ab = c_qkv + 3 * GDN_WIDTH
        u_s = p32[:, c_us:c_us + SSM_WIDTH]
        ab_g = p32[:, c_ab:c_ab + 2 * GDN_HEADS]

        y_a = moba_attention(p32, p16, cos, sin, batch, seq, q_col=0, k_col=MOBA_WIDTH // LANES, v_col=0)

        l_re, l_im, bb_re, bb_im = s5_discretise(ssm_a_re[l], ssm_a_im[l], ssm_log_dt[l], ssm_b_re[l], ssm_b_im[l])
        maps = s5_block_maps(bb_re, bb_im, ssm_c_re[l], ssm_c_im[l])
        u_tm = u_s.reshape(batch, seq, SSM_WIDTH).transpose(1, 0, 2).reshape(t, SSM_WIDTH)
        y_s_tm = s5_gelu(u_tm, maps, l_re, l_im, ssm_d[l])
        y_s = y_s_tm.reshape(seq, batch, SSM_WIDTH).transpose(1, 0, 2).reshape(t, SSM_WIDTH)

        def head_rows(cols):
            return cols.reshape(batch, nc, GDN_CHUNK, GDN_HEADS).transpose(0, 3, 1, 2)
        a_rows = head_rows(ab_g[:, :GDN_HEADS])
        b_rows = head_rows(ab_g[:, GDN_HEADS:2 * GDN_HEADS])
        y_c = gdn_mixer(p32, p16, a_rows, b_rows, gdn_conv[l], gdn_a_log[l], gdn_dt_bias[l], gdn_norm[l],
                        batch, seq, qkv_col=c_qkv // LANES, z_col=MOBA_WIDTH // GDN_WIDTH)

        k_x = norm_matmul(memf, norm_mem[l], bf(xa_wk[l]), BF16)
        v_x = norm_matmul(memf, norm_mem[l], bf(xa_wv[l]), BF16)
        xf = merge_cross_attention(xf, y_a, y_s, y_c, p16, bf(w_up_moba[l]), bf(ssm_w_glu[l]), bf(w_up_gdn[l]),
                                   bf(w_out[l]), norm_xa[l], bf(xa_wq[l]), k_x, v_x, bf(xa_wo[l]), batch, seq,
                                   gate_col=(MOBA_WIDTH + GDN_WIDTH) // d)

        if l % 2 == 0:
            i = l // 2
            xf = ffn(xf, norm_ffn[l], bf(ffn_w_gate[i]), bf(ffn_w_up[i]), bf(ffn_w_down[i]))
        else:
            i = l // 2
            route, h_moe = moe_router(xf, norm_ffn[l], moe_w_router[i])
            xf = moe_ffn(xf, h_moe, route, bf(moe_w_gate[i]), bf(moe_w_up[i]), bf(moe_w_down[i]),
                         final_gain=norm_final if l == depth - 1 else None)

    if depth % 2 == 1:
        xf = rmsnorm(xf, norm_final)
    return xf.reshape(batch, seq, d)
```

```python
import functools

import numpy as np
import jax
import jax.numpy as jnp
from jax import lax
from jax.experimental import pallas as pl
from jax.experimental.pallas import tpu as pltpu

F32 = jnp.float32
BF16 = jnp.bfloat16
HIGHEST = lax.Precision.HIGHEST

EPS = 1e-6
NEG_INF = -1e30
MOBA_HEADS = 8
MOBA_HEAD_DIM = 64
MOBA_WIDTH = MOBA_HEADS * MOBA_HEAD_DIM
MOBA_BLOCK = 256
MOBA_TOPK = 3
ROPE_THETA = 10000.0
SSM_WIDTH = 512
SSM_GROUP = 16
SSM_GROUPS = SSM_WIDTH // SSM_GROUP
SSM_STATE = 64
SSM_STATES = SSM_GROUPS * SSM_STATE
GDN_HEADS = 4
GDN_HEAD_DIM = 128
GDN_WIDTH = GDN_HEADS * GDN_HEAD_DIM
GDN_CONV = 4
GDN_CHUNK = 64
XA_HEADS = 4
XA_HEAD_DIM = 128
XA_WIDTH = XA_HEADS * XA_HEAD_DIM
N_EXPERTS = 8
TOP_K = 2

LANES = 128
SUBLANES = 8
VMEM_LIMIT = 56 * 1024 * 1024


def _cparams(*sem):
    return pltpu.CompilerParams(dimension_semantics=sem, vmem_limit_bytes=VMEM_LIMIT)


def _rms(x, g):
    return x * lax.rsqrt(jnp.mean(x * x, axis=-1, keepdims=True) + EPS) * g


def _sigmoid(x):
    return 1.0 / (1.0 + jnp.exp(-x))


def _silu(x):
    return x * _sigmoid(x)


def _dot(a, b):
    return jnp.dot(a, b, preferred_element_type=F32)


def _dot_nt(a, b):
    return lax.dot_general(a, b, (((1,), (1,)), ((), ())), preferred_element_type=F32)


def _norm_matmul_body(x_ref, g_ref, w_ref, o_ref, h_ref):
    @pl.when(pl.program_id(1) == 0)
    def _():
        h_ref[...] = _rms(x_ref[...], g_ref[...]).astype(BF16)

    o_ref[...] = _dot(h_ref[...], w_ref[...]).astype(o_ref.dtype)


def norm_matmul(x, g, w, out_dtype, tm=512, tn=512):
    t, k = x.shape
    n = w.shape[1]
    tm, tn = min(tm, t), min(tn, n)
    return pl.pallas_call(
        _norm_matmul_body,
        grid=(t // tm, n // tn),
        in_specs=[pl.BlockSpec((tm, k), lambda i, j: (i, 0)),
                  pl.BlockSpec((1, k), lambda i, j: (0, 0)),
                  pl.BlockSpec((k, tn), lambda i, j: (0, j))],
        out_specs=pl.BlockSpec((tm, tn), lambda i, j: (i, j)),
        out_shape=jax.ShapeDtypeStruct((t, n), out_dtype),
        scratch_shapes=[pltpu.VMEM((tm, k), BF16)],
        compiler_params=_cparams("parallel", "arbitrary"),
        name="norm_matmul",
    )(x, g.reshape(1, k), w)


def _norm_matmul2_body(x_ref, g_ref, wa_ref, wb_ref, oa_ref, ob_ref, h_ref, *, na):
    j = pl.program_id(1)

    @pl.when(j == 0)
    def _():
        h_ref[...] = _rms(x_ref[...], g_ref[...]).astype(BF16)

    @pl.when(j < na)
    def _():
        oa_ref[...] = _dot(h_ref[...], wa_ref[...]).astype(oa_ref.dtype)

    @pl.when(j >= na)
    def _():
        ob_ref[...] = _dot(h_ref[...], wb_ref[...]).astype(ob_ref.dtype)


def norm_matmul2(x, g, wa, wb, dtype_a, dtype_b, tm, tna, tnb):
    t, k = x.shape
    tm = min(tm, t)
    na, nb = wa.shape[1] // tna, wb.shape[1] // tnb
    return pl.pallas_call(
        functools.partial(_norm_matmul2_body, na=na),
        grid=(t // tm, na + nb),
        in_specs=[pl.BlockSpec((tm, k), lambda i, j: (i, 0)),
                  pl.BlockSpec((1, k), lambda i, j: (0, 0)),
                  pl.BlockSpec((k, tna), lambda i, j: (0, jnp.minimum(j, na - 1))),
                  pl.BlockSpec((k, tnb), lambda i, j: (0, jnp.maximum(j - na, 0)))],
        out_specs=[pl.BlockSpec((tm, tna), lambda i, j: (i, jnp.minimum(j, na - 1))),
                   pl.BlockSpec((tm, tnb), lambda i, j: (i, jnp.maximum(j - na, 0)))],
        out_shape=[jax.ShapeDtypeStruct((t, wa.shape[1]), dtype_a), jax.ShapeDtypeStruct((t, wb.shape[1]), dtype_b)],
        scratch_shapes=[pltpu.VMEM((tm, k), BF16)],
        compiler_params=_cparams("parallel", "arbitrary"),
        name="norm_matmul2",
    )(x, g.reshape(1, k), wa, wb)


def _rope_body(pos_ref, inv_ref, cos_ref, sin_ref):
    ang = pos_ref[...] * inv_ref[...]
    lane = lax.broadcasted_iota(jnp.int32, ang.shape, 1)
    first_half = (lane % MOBA_HEAD_DIM) < (MOBA_HEAD_DIM // 2)
    cos_ref[...] = jnp.cos(ang)
    s = jnp.sin(ang)
    sin_ref[...] = jnp.where(first_half, -s, s)


def rope_tables(positions):
    t = positions.size
    half = MOBA_HEAD_DIM // 2
    inv = (1.0 / (np.float32(ROPE_THETA) ** (np.arange(0, MOBA_HEAD_DIM, 2, dtype=np.float32)
                                             / np.float32(MOBA_HEAD_DIM)))).astype(np.float32)
    inv_row = jnp.asarray(np.tile(inv, LANES // half).reshape(1, LANES))
    pos = positions.astype(F32).reshape(t, 1)
    ts = min(1024, t)
    return pl.pallas_call(
        _rope_body,
        grid=(t // ts,),
        in_specs=[pl.BlockSpec((ts, 1), lambda i: (i, 0)),
                  pl.BlockSpec((1, LANES), lambda i: (0, 0))],
        out_specs=[pl.BlockSpec((ts, LANES), lambda i: (i, 0))] * 2,
        out_shape=[jax.ShapeDtypeStruct((t, LANES), F32)] * 2,
        compiler_params=_cparams("parallel"),
        name="rope_tables",
    )(pos, inv_row)


def _rope_apply(x, cos, sin_signed):
    lane = lax.broadcasted_iota(jnp.int32, x.shape, 1)
    first_half = (lane % MOBA_HEAD_DIM) < (MOBA_HEAD_DIM // 2)
    partner = jnp.where(first_half, pltpu.roll(x, LANES - MOBA_HEAD_DIM // 2, 1),
                        pltpu.roll(x, MOBA_HEAD_DIM // 2, 1))
    return x * cos + partner * sin_signed


def _moba_body(q_ref, k_ref, v_ref, cq_ref, sq_ref, ck_ref, sk_ref, o_ref, ka_ref, va_ref, km_ref, s_ref,
               mt_ref, acc_ref, *, nb):
    blk = MOBA_BLOCK
    i = pl.program_id(2)

    @pl.when(i == 0)
    def _():
        km_ref[...] = jnp.zeros_like(km_ref)
        lane_b = lax.broadcasted_iota(jnp.int32, (blk, LANES), 1)

        def rope_blk(j, c):
            rows = pl.ds(pl.multiple_of(j * blk, blk), blk)
            kr = _rope_apply(k_ref[rows, :], ck_ref[rows, :], sk_ref[rows, :])
            ka_ref[rows, 0:LANES] = kr.astype(BF16)
            ka_ref[rows, LANES:2 * LANES] = jnp.where(lane_b == j, 1.0, 0.0).astype(BF16)
            km_ref[pl.ds(j, 1), :] = jnp.mean(kr, axis=0, keepdims=True)
            v = v_ref[rows, :].astype(F32)
            va_ref[0, rows, :] = jnp.where(lane_b < MOBA_HEAD_DIM, v,
                                           jnp.where(lane_b == MOBA_HEAD_DIM, 1.0, 0.0)).astype(BF16)
            va_ref[1, rows, :] = jnp.where(lane_b >= MOBA_HEAD_DIM, v,
                                           jnp.where(lane_b == 0, 1.0, 0.0)).astype(BF16)
            return c
        lax.fori_loop(0, nb, rope_blk, 0)

    tq = 2 * blk
    lane = lax.broadcasted_iota(jnp.int32, (1, LANES), 1)
    head_a = lane < MOBA_HEAD_DIM
    q = _rope_apply(q_ref[...], cq_ref[...], sq_ref[...])
    scale = MOBA_HEAD_DIM ** -0.5
    km = km_ref[...]
    nbp = km_ref.shape[0]
    brow = lax.broadcasted_iota(jnp.int32, (nbp, tq), 0)
    qblk = 2 * i + jnp.where(lax.broadcasted_iota(jnp.int32, (1, tq), 1) >= blk, 1, 0)
    heads = (0, 1)
    q_heads = [jnp.where(head_a, q, 0.0), jnp.where(head_a, 0.0, q)]
    km_hi = km.astype(BF16)
    km_lo = (km - km_hi.astype(F32)).astype(BF16)
    gates = []
    for qh in q_heads:
        q_hi = qh.astype(BF16)
        q_lo = (qh - q_hi.astype(F32)).astype(BF16)
        gates.append(_dot_nt(km_hi, q_hi) + (_dot_nt(km_hi, q_lo) + _dot_nt(km_lo, q_hi)))
    q_augs = []
    brow_f = brow.astype(F32)
    for h in heads:
        gate = jnp.where(brow < qblk, gates[h], NEG_INF)
        picked = jnp.zeros((nbp, tq), F32)
        for _ in range(MOBA_TOPK):
            top = jnp.max(gate, axis=0, keepdims=True)
            first = jnp.min(jnp.where(gate == top, brow_f, float(nbp)), axis=0, keepdims=True)
            hit = brow_f == first
            picked = jnp.where(hit, 1.0, picked)
            gate = jnp.where(hit, -jnp.inf, gate)
        keep = ((picked > 0.5) & (brow < qblk)) | (brow == qblk)
        bias_t = jnp.concatenate([jnp.where(keep, 0.0, NEG_INF), jnp.zeros((LANES - nbp, tq), F32)],
                                 axis=0)
        q_augs.append(jnp.concatenate([q_heads[h] * scale, bias_t.T], axis=1).astype(BF16))

    def pair_rows(jp):
        return pl.ds(pl.multiple_of(jp * tq, tq), tq)

    r_idx = lax.broadcasted_iota(jnp.int32, (tq, tq), 0)
    c_idx = lax.broadcasted_iota(jnp.int32, (tq, tq), 1)
    k_diag = ka_ref[pair_rows(i), :]
    s_diag = [_dot_nt(q_augs[h], k_diag) for h in heads]
    for h in heads:
        s_d = jnp.where(c_idx <= r_idx, s_diag[h], NEG_INF)
        s_ref[h, i] = s_d
        m_t = s_d[:, 0:LANES]
        for col in range(LANES, tq, LANES):
            m_t = jnp.maximum(m_t, s_d[:, col:col + LANES])
        mt_ref[h] = m_t

    def loop2(n, body):
        def two(k, c):
            body([2 * k, 2 * k + 1])
            return c
        lax.fori_loop(0, lax.shift_right_logical(n, 1), two, 0)

        @pl.when((n & 1) == 1)
        def _():
            body([n - 1])

    def scores(jps):
        k2s = [ka_ref[pair_rows(jp), :] for jp in jps]
        s2s = [[_dot_nt(q_augs[h], k2) for h in heads] for k2 in k2s]
        for jp, s2h in zip(jps, s2s):
            for h in heads:
                s_ref[h, jp] = s2h[h]
        for h in heads:
            m_t = mt_ref[h]
            for s2h in s2s:
                for col in range(0, tq, LANES):
                    m_t = jnp.maximum(m_t, s2h[h][:, col:col + LANES])
            mt_ref[h] = m_t
    loop2(i, scores)

    ms = [jnp.max(mt_ref[h], axis=-1, keepdims=True) for h in heads]
    for h in heads:
        acc_ref[h] = jnp.zeros((tq, LANES), F32)

    def values(jps):
        ps = [[jnp.exp(s_ref[h, jp] - ms[h]).astype(BF16) for h in heads] for jp in jps]
        pvs = [[_dot(p[h], va_ref[h, pair_rows(jp), :]) for h in heads] for jp, p in zip(jps, ps)]
        for h in heads:
            tot = pvs[0][h]
            for pv in pvs[1:]:
                tot = tot + pv[h]
            acc_ref[h] += tot
    loop2(i + 1, values)
    acc_a, acc_b = acc_ref[0], acc_ref[1]
    out_a = acc_a / acc_a[:, MOBA_HEAD_DIM:MOBA_HEAD_DIM + 1]
    out_b = acc_b / acc_b[:, 0:1]
    o_ref[...] = jnp.where(head_a, out_a, out_b).astype(o_ref.dtype)


def moba_attention(qk, v, cos, sin, batch, seq, q_col=0, k_col=MOBA_WIDTH // LANES, v_col=0):
    nb = seq // MOBA_BLOCK
    assert nb % 2 == 0
    npair = nb // 2
    pairs = MOBA_WIDTH // LANES
    tq = 2 * MOBA_BLOCK
    t = batch * seq
    return pl.pallas_call(
        functools.partial(_moba_body, nb=nb),
        grid=(batch, pairs, npair),
        in_specs=[pl.BlockSpec((tq, LANES), lambda b, p, i: (b * npair + i, q_col + p)),
                  pl.BlockSpec((seq, LANES), lambda b, p, i: (b, k_col + p)),
                  pl.BlockSpec((seq, LANES), lambda b, p, i: (b, v_col + p)),
                  pl.BlockSpec((tq, LANES), lambda b, p, i: (b * npair + i, 0)),
                  pl.BlockSpec((tq, LANES), lambda b, p, i: (b * npair + i, 0)),
                  pl.BlockSpec((seq, LANES), lambda b, p, i: (b, 0)),
                  pl.BlockSpec((seq, LANES), lambda b, p, i: (b, 0))],
        out_specs=pl.BlockSpec((tq, LANES), lambda b, p, i: (b * npair + i, p)),
        out_shape=jax.ShapeDtypeStruct((t, MOBA_WIDTH), BF16),
        scratch_shapes=[pltpu.VMEM((seq, 2 * LANES), BF16),
                        pltpu.VMEM((2, seq, LANES), BF16),
                        pltpu.VMEM((-(-nb // SUBLANES) * SUBLANES, LANES), F32),
                        pltpu.VMEM((2, npair, tq, tq), F32),
                        pltpu.VMEM((2, tq, LANES), F32),
                        pltpu.VMEM((2, tq, LANES), F32)],
        compiler_params=_cparams("parallel", "parallel", "arbitrary"),
        name="moba_attention",
    )(qk, qk, v, cos, sin, cos, sin)


def _s5_disc_body(are_ref, aim_ref, ldt_ref, bre_ref, bim_ref, lre_ref, lim_ref, bbre_ref, bbim_ref):
    a_re, a_im = are_ref[...], aim_ref[...]
    dt = jnp.exp(ldt_ref[...])
    mag = jnp.exp(a_re * dt)
    l_re = mag * jnp.cos(a_im * dt)
    l_im = mag * jnp.sin(a_im * dt)
    lre_ref[...] = l_re
    lim_ref[...] = l_im
    x, y = l_re - 1.0, l_im
    den = a_re * a_re + a_im * a_im
    c_re = (x * a_re + y * a_im) / den
    c_im = (y * a_re - x * a_im) / den
    b_re, b_im = bre_ref[...], bim_ref[...]
    bbre_ref[...] = c_re * b_re - c_im * b_im
    bbim_ref[...] = c_re * b_im + c_im * b_re


def s5_discretise(a_re, a_im, log_dt, b_re, b_im):
    g, p = a_re.shape
    n = b_re.shape[-1]
    col = lambda a: a.reshape(g * p, 1)
    ldt = jnp.broadcast_to(log_dt[:, None], (g, p))
    outs = pl.pallas_call(
        _s5_disc_body,
        out_shape=[jax.ShapeDtypeStruct((g * p, 1), F32)] * 2 + [jax.ShapeDtypeStruct((g * p, n), F32)] * 2,
        name="s5_discretise",
    )(col(a_re), col(a_im), col(ldt), b_re.reshape(g * p, n), b_im.reshape(g * p, n))
    l_re, l_im, bb_re, bb_im = outs
    return l_re.reshape(g, p), l_im.reshape(g, p), bb_re.reshape(g, p, n), bb_im.reshape(g, p, n)


def _block_diag(blocks):
    g, r, c = blocks.shape
    eye = jnp.eye(g, dtype=blocks.dtype)
    return (blocks[:, :, None, :] * eye[:, None, :, None]).reshape(g * r, g * c)


def _s5_body(u_ref, bre_ref, bim_ref, cre_ref, cim_ref, lre_ref, lim_ref, d_ref, y_ref, h_ref, hb_ref, *, steps):
    rows = SUBLANES
    ns = SSM_STATES
    nq = SSM_WIDTH // LANES
    sq = ns // nq

    @pl.when(pl.program_id(0) == 0)
    def _():
        h_ref[...] = jnp.zeros_like(h_ref)

    u = u_ref[...]
    u16 = u.astype(BF16)
    for q in range(nq):
        uq = u16[:, q * LANES:(q + 1) * LANES]
        hb_ref[:, q * sq:(q + 1) * sq] = _dot(uq, bre_ref[q])
        hb_ref[:, ns + q * sq:ns + (q + 1) * sq] = _dot(uq, bim_ref[q])
    l_re, l_im = lre_ref[...], lim_ref[...]

    def step(t, carry):
        h_re, h_im = carry
        r = pl.ds(pl.multiple_of(t * rows, rows), rows)
        n_re = l_re * h_re - l_im * h_im + hb_ref[r, 0:ns]
        n_im = l_re * h_im + l_im * h_re + hb_ref[r, ns:2 * ns]
        hb_ref[r, 0:ns] = n_re
        hb_ref[r, ns:2 * ns] = n_im
        return n_re, n_im

    h_re, h_im = lax.fori_loop(0, steps, step, (h_ref[:, 0:ns], h_ref[:, ns:2 * ns]), unroll=4)
    h_ref[:, 0:ns] = h_re
    h_ref[:, ns:2 * ns] = h_im
    ys = []
    for q in range(nq):
        ys.append(_dot(hb_ref[:, q * sq:(q + 1) * sq].astype(BF16), cre_ref[q])
                  + _dot(hb_ref[:, ns + q * sq:ns + (q + 1) * sq].astype(BF16), cim_ref[q]))
    y = jnp.concatenate(ys, axis=1) + d_ref[...] * u
    y_ref[...] = jax.nn.gelu(y).astype(y_ref.dtype)


def s5_block_maps(bb_re, bb_im, c_re, c_im):
    nq = SSM_WIDTH // LANES
    gq = SSM_GROUPS // nq
    per_q = lambda a: jnp.stack([_block_diag(a[q * gq:(q + 1) * gq].transpose(0, 2, 1)) for q in range(nq)])
    return (per_q(bb_re).astype(BF16), per_q(bb_im).astype(BF16),
            per_q(c_re).astype(BF16), per_q(-c_im).astype(BF16))


def s5_gelu(u_tm, maps, l_re, l_im, d_skip, steps=64):
    rows = u_tm.shape[0]
    blk = steps * SUBLANES
    ns2 = 2 * SSM_STATES
    full3 = lambda a: pl.BlockSpec(a.shape, lambda c: (0, 0, 0))
    return pl.pallas_call(
        functools.partial(_s5_body, steps=steps),
        grid=(rows // blk,),
        in_specs=[pl.BlockSpec((blk, SSM_WIDTH), lambda c: (c, 0)),
                  full3(maps[0]), full3(maps[1]), full3(maps[2]), full3(maps[3]),
                  pl.BlockSpec((1, SSM_STATES), lambda c: (0, 0)),
                  pl.BlockSpec((1, SSM_STATES), lambda c: (0, 0)),
                  pl.BlockSpec((1, SSM_WIDTH), lambda c: (0, 0))],
        out_specs=pl.BlockSpec((blk, SSM_WIDTH), lambda c: (c, 0)),
        out_shape=jax.ShapeDtypeStruct((rows, SSM_WIDTH), BF16),
        scratch_shapes=[pltpu.VMEM((SUBLANES, ns2), F32),
                        pltpu.VMEM((blk, ns2), F32)],
        compiler_params=_cparams("arbitrary"),
        name="s5_scan",
    )(u_tm, *maps, l_re.reshape(1, -1), l_im.reshape(1, -1), d_skip.reshape(1, -1))


GDN_GROUP = 16


def _gdn_prep_body(q_ref, k_ref, v_ref, a_ref, b_ref, wq_ref, wk_ref, wv_ref, alog_ref, dtb_ref,
                   u_ref, w_ref, qd_ref, kd_ref, qk_ref, gl_ref, qs_ref, ks_ref, vs_ref,
                   *, seq):
    L = GDN_CHUNK
    G = min(GDN_GROUP, seq // L)
    dk = GDN_HEAD_DIM
    nc = seq // L
    hd = pl.program_id(1)
    cb = 256
    halo = SUBLANES

    def conv_blk(rb, c):
        base = pl.multiple_of(rb * cb, cb)
        prev = pl.multiple_of(jnp.maximum(base - halo, 0), halo)
        for src, wref, dst, norm, mul in ((q_ref, wq_ref, qs_ref, True, dk ** -0.5),
                                          (k_ref, wk_ref, ks_ref, True, 1.0),
                                          (v_ref, wv_ref, vs_ref, False, 1.0)):
            cur = src[pl.ds(base, cb), :]
            head = jnp.where(rb > 0, src[pl.ds(prev, halo), :], 0.0)
            ext = jnp.concatenate([head, cur], axis=0)
            wts = wref[...]
            y = ext[halo:halo + cb] * wts[GDN_CONV - 1:GDN_CONV]
            for tap in range(1, GDN_CONV):
                y = y + ext[halo - tap:halo - tap + cb] * wts[GDN_CONV - 1 - tap:GDN_CONV - tap]
            y = _silu(y)
            if norm:
                y = y * lax.rsqrt(jnp.sum(y * y, axis=-1, keepdims=True) + EPS) * mul
            dst[pl.ds(base, cb), :] = y
        return c
    lax.fori_loop(0, seq // cb, conv_blk, 0)

    gl_rows = G * L
    ri = lax.broadcasted_iota(jnp.int32, (gl_rows, L), 0) % L
    ci = lax.broadcasted_iota(jnp.int32, (gl_rows, L), 1)
    tril = ci <= ri
    strict = ci < ri
    eye = ci == ri
    ri3 = lax.broadcasted_iota(jnp.int32, (G, L, L), 1)
    ci3 = lax.broadcasted_iota(jnp.int32, (G, L, L), 2)
    strict3 = ci3 < ri3
    eye_f = jnp.where(ci3 == ri3, 1.0, 0.0)
    r2 = lax.broadcasted_iota(jnp.int32, (L, L), 0)
    c2 = lax.broadcasted_iota(jnp.int32, (L, L), 1)
    upper_f = jnp.where(r2 <= c2, 1.0, 0.0)
    a_coef = -jnp.exp(jnp.full((1, L), alog_ref[hd], F32))
    dt_bias = dtb_ref[hd]
    hp = functools.partial(jnp.dot, preferred_element_type=F32, precision=HIGHEST)

    def bmm(a, b):
        return jnp.einsum('gij,gjk->gik', a, b, preferred_element_type=F32)

    def bmm_nt(a, b):
        return jnp.einsum('gid,gjd->gij', a, b, preferred_element_type=F32)

    def to_col(rows_b):
        return jnp.sum(jnp.where(eye, rows_b, 0.0), axis=1, keepdims=True)

    def prep(cg, carry):
        c0 = pl.multiple_of(cg * G, G)
        rows = pl.ds(pl.multiple_of(cg * gl_rows, gl_rows), gl_rows)
        qc, kc, vc = qs_ref[rows, :], ks_ref[rows, :], vs_ref[rows, :]
        z = a_ref[pl.ds(c0, G), :] + dt_bias
        softplus = jnp.maximum(z, 0.0) + jnp.log(1.0 + jnp.exp(-jnp.abs(z)))
        g_rows = hp(a_coef * softplus, upper_f)
        beta_rows = _sigmoid(b_ref[pl.ds(c0, G), :])
        spread = lambda r: jnp.concatenate([jnp.broadcast_to(r[c:c + 1, :], (L, L)) for c in range(G)], axis=0)
        g_rb = spread(g_rows)
        g_col = to_col(g_rb)
        beta_col = to_col(spread(beta_rows))
        g_last = g_rb[:, L - 1:L]
        decay = jnp.where(tril, jnp.exp(jnp.where(tril, g_col - g_rb, 0.0)), 0.0)
        k_beta = kc * beta_col
        v_beta = vc * beta_col
        to3 = lambda a: a.reshape(G, L, a.shape[-1])
        kb16, k16, q16 = to3(k_beta.astype(BF16)), to3(kc.astype(BF16)), to3(qc.astype(BF16))
        kk_qk = bmm_nt(jnp.concatenate([kb16, q16], axis=1), k16)
        decay3 = to3(decay)
        a_low = jnp.where(strict3, kk_qk[:, :L] * decay3, 0.0)
        qk = kk_qk[:, L:] * decay3
        t_inv = eye_f - a_low
        a16 = a_low.astype(BF16)
        pw = bmm(a16, a16)
        span = 2
        while 2 * span < L:
            pw16 = pw.astype(BF16)
            both = bmm(jnp.concatenate([t_inv.astype(BF16), pw16], axis=1), pw16)
            t_inv = t_inv + both[:, :L]
            pw = both[:, L:]
            span *= 2
        t_inv = t_inv + bmm(t_inv.astype(BF16), pw.astype(BF16))
        rhs = jnp.concatenate([v_beta.astype(BF16), (k_beta * jnp.exp(g_col)).astype(BF16)], axis=1)
        uw = bmm(t_inv.astype(BF16), to3(rhs)).reshape(gl_rows, 2 * dk)
        u_ref[rows, :] = uw[:, :dk]
        w_ref[rows, :] = uw[:, dk:].astype(BF16)
        qd_ref[rows, :] = (qc * jnp.exp(g_col)).astype(BF16)
        k_dec = to3(kc * jnp.exp(g_last - g_col))
        kd_ref[pl.ds(c0, G)] = jnp.swapaxes(k_dec, 1, 2).astype(BF16)
        qk_ref[pl.ds(c0, G)] = qk.astype(BF16)
        gl_ref[pl.ds(c0, G), :] = jnp.broadcast_to(jnp.exp(g_rows[:, L - 1:L]), (G, LANES))
        return carry
    lax.fori_loop(0, nc // G, prep, 0)


def _gdn_scan_body(u_ref, w_ref, qd_ref, kd_ref, qk_ref, gl_ref, z_ref, gn_ref, o_ref, *st_refs, chunks):
    L = GDN_CHUNK
    d = GDN_HEAD_DIM
    nrow = u_ref.shape[0]

    @pl.when(pl.program_id(1) == 0)
    def _():
        for st in st_refs:
            st[...] = jnp.zeros_like(st)

    gn = gn_ref[...]

    def chunk(c, carry):
        rows = pl.ds(pl.multiple_of(c * L, L), L)
        chains = [(r, h) for r in range(nrow) for h in range(GDN_HEADS)]
        states = [st_refs[r * GDN_HEADS + h][...] for r, h in chains]
        firsts = []
        for (r, h), s in zip(chains, states):
            cols = slice(h * d, (h + 1) * d)
            firsts.append(_dot(jnp.concatenate([w_ref[r, rows, cols], qd_ref[r, rows, cols]], axis=0),
                               s.astype(BF16)))
        vns = []
        for (r, h), both in zip(chains, firsts):
            cols = slice(h * d, (h + 1) * d)
            vns.append((u_ref[r, rows, cols] - both[:L]).astype(BF16))
        outs = [both[L:] + _dot(qk_ref[r, h, c], vn) for (r, h), both, vn in zip(chains, firsts, vns)]
        for (r, h), s, vn in zip(chains, states, vns):
            st_refs[r * GDN_HEADS + h][...] = s * gl_ref[r, h, pl.ds(c, 1), :] + _dot(kd_ref[r, h, c], vn)
        for (r, h), o in zip(chains, outs):
            cols = slice(h * d, (h + 1) * d)
            o_ref[r, rows, cols] = (_rms(o, gn) * _silu(z_ref[r, rows, cols].astype(F32))).astype(o_ref.dtype)
        return carry
    lax.fori_loop(0, chunks, chunk, 0)


def gdn_mixer(qkv, z, a_rows, b_rows, conv_w, a_log, dt_bias, gnorm, batch, seq, ts=1024, qkv_col=0, z_col=0):
    t = batch * seq
    hds = GDN_HEADS
    nc = seq // GDN_CHUNK
    L = GDN_CHUNK
    d = GDN_HEAD_DIM
    seq_spec = lambda off: pl.BlockSpec((seq, d), lambda b, h: (b, qkv_col + off + h))
    w_spec = lambda off: pl.BlockSpec((GDN_CONV, d), lambda b, h: (0, off + h))
    row_spec = pl.BlockSpec((None, None, nc, L), lambda b, h: (b, h, 0, 0))
    smem = pl.BlockSpec(memory_space=pltpu.SMEM)
    head_out = pl.BlockSpec((seq, d), lambda b, h: (b, h))
    u, w, qd, kd, qk, gl = pl.pallas_call(
        functools.partial(_gdn_prep_body, seq=seq),
        grid=(batch, hds),
        in_specs=[seq_spec(0), seq_spec(hds), seq_spec(2 * hds), row_spec, row_spec,
                  w_spec(0), w_spec(hds), w_spec(2 * hds), smem, smem],
        out_specs=[head_out, head_out, head_out,
                   pl.BlockSpec((None, None, nc, d, L), lambda b, h: (b, h, 0, 0, 0)),
                   pl.BlockSpec((None, None, nc, L, L), lambda b, h: (b, h, 0, 0, 0)),
                   pl.BlockSpec((None, None, nc, LANES), lambda b, h: (b, h, 0, 0))],
        out_shape=[jax.ShapeDtypeStruct((t, GDN_WIDTH), F32),
                   jax.ShapeDtypeStruct((t, GDN_WIDTH), BF16),
                   jax.ShapeDtypeStruct((t, GDN_WIDTH), BF16),
                   jax.ShapeDtypeStruct((batch, hds, nc, d, L), BF16),
                   jax.ShapeDtypeStruct((batch, hds, nc, L, L), BF16),
                   jax.ShapeDtypeStruct((batch, hds, nc, LANES), F32)],
        scratch_shapes=[pltpu.VMEM((seq, d), F32), pltpu.VMEM((seq, d), F32), pltpu.VMEM((seq, d), F32)],
        compiler_params=_cparams("parallel", "parallel"),
        name="gdn_prep",
    )(qkv, qkv, qkv, a_rows, b_rows, conv_w, conv_w, conv_w, a_log, dt_bias)

    ts = min(ts, seq)
    spb = seq // ts
    cps = ts // L
    nrow = 2 if batch % 2 == 0 else 1
    by_row = lambda a: a.reshape(batch, seq, a.shape[-1])
    tok = pl.BlockSpec((nrow, ts, GDN_WIDTH), lambda b, s: (b, s, 0))
    out = pl.pallas_call(
        functools.partial(_gdn_scan_body, chunks=cps),
        grid=(batch // nrow, spb),
        in_specs=[tok, tok, tok,
                  pl.BlockSpec((nrow, hds, cps, d, L), lambda b, s: (b, 0, s, 0, 0)),
                  pl.BlockSpec((nrow, hds, cps, L, L), lambda b, s: (b, 0, s, 0, 0)),
                  pl.BlockSpec((nrow, hds, cps, LANES), lambda b, s: (b, 0, s, 0)),
                  pl.BlockSpec((nrow, ts, GDN_WIDTH), lambda b, s: (b, s, z_col)),
                  pl.BlockSpec((1, d), lambda b, s: (0, 0))],
        out_specs=tok,
        out_shape=jax.ShapeDtypeStruct((batch, seq, GDN_WIDTH), BF16),
        scratch_shapes=[pltpu.VMEM((d, d), F32)] * (nrow * hds),
        compiler_params=_cparams("parallel", "arbitrary"),
        name="gdn_scan",
    )(by_row(u), by_row(w), by_row(qd), kd, qk, gl, by_row(z), gnorm.reshape(1, d))
    return out.reshape(t, GDN_WIDTH)


def _merge_xattn_body(x_ref, ya_ref, ys_ref, yc_ref, ga_ref, gb_ref, gc_ref, wa_ref, wg_ref, wc_ref, wo_ref,
                      gx_ref, wq_ref, k_ref, v_ref, wxo_ref, o_ref):
    d = x_ref.shape[1]
    y_a = _dot(ya_ref[...], wa_ref[...])
    glu = _dot(ys_ref[...], wg_ref[...])
    y_b = glu[:, :d] * _sigmoid(glu[:, d:])
    y_c = _dot(yc_ref[...], wc_ref[...])
    gate = lambda r: _sigmoid(r[...].astype(F32))
    merged = gate(ga_ref) * y_a + gate(gb_ref) * y_b + gate(gc_ref) * y_c
    x1 = x_ref[...] + _dot(merged.astype(BF16), wo_ref[...])

    h = _rms(x1, gx_ref[...]).astype(BF16)
    q = _dot(h, wq_ref[...]).astype(BF16)
    cols = [slice(hd * XA_HEAD_DIM, (hd + 1) * XA_HEAD_DIM) for hd in range(XA_HEADS)]
    scores = [_dot_nt(q[:, c], k_ref[:, c]) * (XA_HEAD_DIM ** -0.5) for c in cols]
    probs = []
    for s in scores:
        p = jnp.exp(s - jnp.max(s, axis=-1, keepdims=True))
        probs.append((p / jnp.sum(p, axis=-1, keepdims=True)).astype(BF16))
    outs = [_dot(p, v_ref[:, c]) for p, c in zip(probs, cols)]
    o = jnp.concatenate(outs, axis=1).astype(BF16)
    o_ref[...] = x1 + _dot(o, wxo_ref[...])


def merge_cross_attention(x, y_a, y_s, y_c, gates, w_a, w_glu, w_c, w_o, g_xa, wq, k, v, wo_xa, batch, seq,
                          tm=512, gate_col=0):
    t, d = x.shape
    m = k.shape[0] // batch
    spb = seq // tm
    row = lambda n: pl.BlockSpec((tm, n), lambda i: (i, 0))
    gate = lambda j: pl.BlockSpec((tm, d), lambda i: (i, gate_col + j))
    full = lambda a: pl.BlockSpec(a.shape, lambda i: (0, 0))
    mem = pl.BlockSpec((m, XA_WIDTH), lambda i: (i // spb, 0))
    return pl.pallas_call(
        _merge_xattn_body,
        grid=(t // tm,),
        in_specs=[row(d), row(y_a.shape[1]), row(y_s.shape[1]), row(y_c.shape[1]), gate(0), gate(1), gate(2),
                  full(w_a), full(w_glu), full(w_c), full(w_o),
                  pl.BlockSpec((1, d), lambda i: (0, 0)), full(wq), mem, mem, full(wo_xa)],
        out_specs=row(d),
        out_shape=jax.ShapeDtypeStruct((t, d), F32),
        compiler_params=_cparams("parallel"),
        name="merge_cross_attention",
    )(x, y_a, y_s, y_c, gates, gates, gates, w_a, w_glu, w_c, w_o, g_xa.reshape(1, d), wq, k, v, wo_xa)


def _ffn_body(x_ref, g_ref, wg_ref, wu_ref, wd_ref, o_ref, h_ref, acc_ref):
    f = pl.program_id(1)

    @pl.when(f == 0)
    def _():
        h_ref[...] = _rms(x_ref[...], g_ref[...]).astype(BF16)
        acc_ref[...] = jnp.zeros_like(acc_ref)

    h = h_ref[...]
    act = (_silu(_dot(h, wg_ref[...])) * _dot(h, wu_ref[...])).astype(BF16)
    acc_ref[...] += _dot(act, wd_ref[...])

    @pl.when(f == pl.num_programs(1) - 1)
    def _():
        o_ref[...] = x_ref[...] + acc_ref[...]


def ffn(x, g, w_gate, w_up, w_down, tm=512, tf=1408):
    t, d = x.shape
    f = w_gate.shape[1]
    return pl.pallas_call(
        _ffn_body,
        grid=(t // tm, f // tf),
        in_specs=[pl.BlockSpec((tm, d), lambda i, k: (i, 0)),
                  pl.BlockSpec((1, d), lambda i, k: (0, 0)),
                  pl.BlockSpec((d, tf), lambda i, k: (0, k)),
                  pl.BlockSpec((d, tf), lambda i, k: (0, k)),
                  pl.BlockSpec((tf, d), lambda i, k: (k, 0))],
        out_specs=pl.BlockSpec((tm, d), lambda i, k: (i, 0)),
        out_shape=jax.ShapeDtypeStruct((t, d), F32),
        scratch_shapes=[pltpu.VMEM((tm, d), BF16), pltpu.VMEM((tm, d), F32)],
        compiler_params=_cparams("parallel", "arbitrary"),
        name="ffn",
    )(x, g.reshape(1, d), w_gate, w_up, w_down)


def _router_body(x_ref, g_ref, whi_ref, wlo_ref, r_ref, h_ref):
    h = _rms(x_ref[...], g_ref[...])
    h_hi = h.astype(BF16)
    h_ref[...] = h_hi
    h_lo = (h - h_hi.astype(F32)).astype(BF16)
    logits = _dot(h_hi, whi_ref[...]) + (_dot(h_hi, wlo_ref[...]) + _dot(h_lo, whi_ref[...]))
    lane = lax.broadcasted_iota(jnp.int32, logits.shape, 1).astype(F32)
    logits = jnp.where(lane < N_EXPERTS, logits, -jnp.inf)
    v1 = jnp.max(logits, axis=-1, keepdims=True)
    i1 = jnp.min(jnp.where(logits == v1, lane, float(LANES)), axis=-1, keepdims=True)
    rest = jnp.where(lane == i1, -jnp.inf, logits)
    v2 = jnp.max(rest, axis=-1, keepdims=True)
    i2 = jnp.min(jnp.where(rest == v2, lane, float(LANES)), axis=-1, keepdims=True)
    e2 = jnp.exp(v2 - v1)
    w1 = 1.0 / (1.0 + e2)
    w2 = e2 / (1.0 + e2)
    r_ref[...] = (jnp.where(lane == 0.0, i1, 0.0) + jnp.where(lane == 1.0, i2, 0.0)
                  + jnp.where(lane == 2.0, w1, 0.0) + jnp.where(lane == 3.0, w2, 0.0))


def moe_router(x, g, w_router, tm=512):
    t, d = x.shape
    w = jnp.pad(w_router, ((0, 0), (0, LANES - w_router.shape[1])))
    w_hi = w.astype(BF16)
    w_lo = (w - w_hi.astype(F32)).astype(BF16)
    return pl.pallas_call(
        _router_body,
        grid=(t // tm,),
        in_specs=[pl.BlockSpec((tm, d), lambda i: (i, 0)),
                  pl.BlockSpec((1, d), lambda i: (0, 0)),
                  pl.BlockSpec((d, LANES), lambda i: (0, 0)),
                  pl.BlockSpec((d, LANES), lambda i: (0, 0))],
        out_specs=[pl.BlockSpec((tm, LANES), lambda i: (i, 0)), pl.BlockSpec((tm, d), lambda i: (i, 0))],
        out_shape=[jax.ShapeDtypeStruct((t, LANES), F32), jax.ShapeDtypeStruct((t, d), BF16)],
        compiler_params=_cparams("parallel"),
        name="moe_router",
    )(x, g.reshape(1, d), w_hi, w_lo)


MOE_TILE = 512
MOE_GTILE = 256
MOE_GCHUNK = 1024
MOE_CHUNK = 512
MOE_RBLK = 256


def _count_le(sorted_vals, queries):
    return jnp.sum((sorted_vals[None, :] <= queries[:, None]).astype(jnp.int32), axis=1)


def _moe_plan(route, t):
    e = N_EXPERTS
    tm, ck, rb = MOE_TILE, MOE_CHUNK, MOE_RBLK
    n_tiles = (TOP_K * t) // tm + e
    p = n_tiles * tm
    n_chunks = t // ck
    i32 = jnp.int32
    e1 = route[:, 0].astype(i32)
    e2 = route[:, 1].astype(i32)
    oh1 = jax.nn.one_hot(e1, e, dtype=i32)
    oh2 = jax.nn.one_hot(e2, e, dtype=i32)
    cnt = oh1 + oh2
    incl = jnp.cumsum(cnt, axis=0)
    excl = incl - cnt
    n_e = incl[-1]
    g_e = ((n_e + tm - 1) // tm) * tm
    off_end = jnp.cumsum(g_e)
    off = off_end - g_e
    pos_all = off[None, :] + excl
    pos1 = jnp.sum(pos_all * oh1, axis=1)
    pos2 = jnp.sum(pos_all * oh2, axis=1)
    tile_start = jnp.arange(n_tiles, dtype=i32) * tm
    tile_expert = jnp.minimum(_count_le(off_end, tile_start), e - 1)
    tile_sel = jax.nn.one_hot(tile_expert, e, dtype=i32)
    tile_used = (tile_start < jnp.sum(tile_sel * (off + n_e)[None, :], axis=1)).astype(i32)
    gm = MOE_GTILE
    n_gt = p // gm
    gt_start = jnp.arange(n_gt, dtype=i32) * gm
    sel = jax.nn.one_hot(jnp.minimum(_count_le(off_end, gt_start), e - 1), e, dtype=i32)
    r0 = gt_start - jnp.sum(sel * off[None, :], axis=1)
    n_sel = jnp.sum(sel * n_e[None, :], axis=1)
    has_rows = r0 < n_sel
    r1 = jnp.minimum(r0 + gm, n_sel) - 1
    gk = MOE_GCHUNK
    n_gchunks = t // gk
    g_ends = incl[gk - 1::gk]
    ends_sel = jnp.sum(g_ends[None, :, :] * sel[:, None, :], axis=2)
    c_lo = jnp.where(has_rows, jnp.sum((ends_sel <= r0[:, None]).astype(i32), axis=1), 0)
    c_hi = jnp.where(has_rows, jnp.sum((ends_sel <= r1[:, None]).astype(i32), axis=1), 0)
    n_it = c_hi - c_lo + 1
    it_end = jnp.cumsum(n_it)
    it_start = it_end - n_it
    wmax = n_gt + e * n_gchunks
    w_idx = jnp.arange(wmax, dtype=i32)
    g_tile = jnp.minimum(_count_le(it_end, w_idx), n_gt - 1)
    g_valid = w_idx < it_end[-1]
    g_chunk = jnp.where(g_valid, c_lo[g_tile] + (w_idx - it_start[g_tile]), c_hi[n_gt - 1])
    g_first = (g_valid & (w_idx == it_start[g_tile])).astype(i32)
    g_last = (g_valid & (w_idx == it_end[g_tile] - 1)).astype(i32)
    gather = (g_tile, g_chunk.astype(i32), g_first, g_last, g_valid.astype(i32))
    lo = off[None, :] + excl[::ck]
    hi = off[None, :] + incl[ck - 1::ck]
    nblk = jnp.where(hi > lo, (hi - 1) // rb - lo // rb + 1, 0).reshape(-1)
    blk_lo = (lo // rb).reshape(-1)
    cb_end = jnp.cumsum(nblk)
    cb_start = cb_end - nblk
    cmax = p // rb + e * n_chunks
    c_idx = jnp.arange(cmax, dtype=i32)
    pair = jnp.minimum(_count_le(cb_end, c_idx), nblk.shape[0] - 1)
    c_valid = c_idx < cb_end[-1]
    c_tile = jnp.where(c_valid, pair // e, n_chunks - 1)
    c_blk = jnp.where(c_valid, blk_lo[pair] + (c_idx - cb_start[pair]), 0)
    per_tile = jnp.sum(nblk.reshape(n_chunks, e), axis=1)
    t_end = jnp.cumsum(per_tile)
    c_first = (c_valid & (c_idx == (t_end - per_tile)[c_tile])).astype(i32)
    c_last = (c_valid & (c_idx == t_end[c_tile] - 1)).astype(i32)
    combine = ((t_end - per_tile).astype(i32), per_tile.astype(i32), c_blk.astype(i32))
    return dict(n_tiles=n_tiles, p=p, tile_expert=tile_expert.astype(i32), tile_used=tile_used, pos1=pos1, pos2=pos2,
                gather=gather, combine=combine)


MOE_IN_BUFS = 3
MOE_OUT_BUFS = 2


def _moe_gather_body(tile_ref, chunk_ref, first_ref, last_ref, count_ref, meta_hbm, h_hbm, xs_hbm, wr_hbm,
                     hbuf, mbuf, obuf, wbuf, acc_ref, wacc_ref, in_sem, out_sem):
    n = count_ref[0]
    nin, nout = MOE_IN_BUFS, MOE_OUT_BUFS

    def reads(w, slot):
        c = chunk_ref[w]
        return (pltpu.make_async_copy(h_hbm.at[c], hbuf.at[slot], in_sem.at[0, slot]),
                pltpu.make_async_copy(meta_hbm.at[c], mbuf.at[slot], in_sem.at[1, slot]))

    def writes(tile, slot):
        return (pltpu.make_async_copy(obuf.at[slot], xs_hbm.at[tile], out_sem.at[0, slot]),
                pltpu.make_async_copy(wbuf.at[slot], wr_hbm.at[tile], out_sem.at[1, slot]))

    for k in range(nin - 1):
        @pl.when(k < n)
        def _():
            for cp in reads(k, k):
                cp.start()

    def item(w, n_out):
        slot = lax.rem(w, nin)
        ahead = w + (nin - 1)

        @pl.when(ahead < n)
        def _():
            for cp in reads(ahead, lax.rem(ahead, nin)):
                cp.start()

        for cp in reads(w, slot):
            cp.wait()

        @pl.when(first_ref[w] == 1)
        def _():
            acc_ref[...] = jnp.zeros_like(acc_ref)
            wacc_ref[...] = jnp.zeros_like(wacc_ref)

        tile = tile_ref[w]
        meta = mbuf[slot]
        row = (lax.broadcasted_iota(jnp.int32, (MOE_GTILE, MOE_GCHUNK), 0) + tile * MOE_GTILE).astype(F32)
        hit1 = row == meta[0:1, :]
        hit2 = row == meta[1:2, :]
        onehot = jnp.where(hit1, 1.0, jnp.where(hit2, 1.0, 0.0)).astype(BF16)
        acc_ref[...] += _dot(onehot, hbuf[slot])
        wts = jnp.where(hit1, meta[2:3, :], jnp.where(hit2, meta[3:4, :], 0.0))
        wacc_ref[...] += jnp.sum(wts, axis=1, keepdims=True)

        is_last = last_ref[w] == 1

        @pl.when(is_last)
        def _():
            oslot = lax.rem(n_out, nout)

            @pl.when(n_out >= nout)
            def _():
                for cp in writes(tile, oslot):
                    cp.wait()
            obuf[oslot] = acc_ref[...].astype(obuf.dtype)
            wbuf[oslot] = jnp.broadcast_to(wacc_ref[...], wbuf.shape[1:])
            for cp in writes(tile, oslot):
                cp.start()
        return n_out + jnp.where(is_last, 1, 0)

    n_out = lax.fori_loop(0, n, item, 0)
    for k in range(1, nout + 1):
        @pl.when(n_out >= k)
        def _():
            for cp in writes(0, lax.rem(n_out - k, nout)):
                cp.wait()


def _moe_ffn_body(exp_ref, used_ref, x_ref, wr_ref, wg_ref, wu_ref, wd_ref, o_ref, acc_ref):
    f = pl.program_id(1)

    @pl.when(f == 0)
    def _():
        acc_ref[...] = jnp.zeros_like(acc_ref)

    @pl.when(used_ref[pl.program_id(0)] == 1)
    def _():
        x = x_ref[...]
        act = (_silu(_dot(x, wg_ref[...])) * _dot(x, wu_ref[...])).astype(BF16)
        acc_ref[...] += _dot(act, wd_ref[...])

    @pl.when(f == pl.num_programs(1) - 1)
    def _():
        o_ref[...] = (acc_ref[...] * wr_ref[:, 0:1]).astype(o_ref.dtype)


def _moe_combine_body(start_ref, count_ref, blk_ref, x_ref, pos_ref, gf_ref, ys_hbm, o_ref, ybuf, sem, acc_ref,
                      *, final_norm):
    i = pl.program_id(0)
    s0, n = start_ref[i], count_ref[i]
    n_pairs = lax.shift_right_logical(n + 1, 1)
    rb = MOE_RBLK

    def item_block(j, first=s0, count=n):
        return blk_ref[first + jnp.minimum(j, count - 1)]

    def fetch(j, slot, half, first=s0, count=n):
        return pltpu.make_async_copy(ys_hbm.at[item_block(j, first, count)], ybuf.at[slot, half],
                                     sem.at[slot, half])

    @pl.when(i == 0)
    def _():
        fetch(0, 0, 0).start()
        fetch(1, 0, 1).start()
    acc_ref[...] = x_ref[...]
    col = lax.broadcasted_iota(jnp.int32, (MOE_CHUNK, rb), 1)
    reps = rb // LANES
    p1 = jnp.concatenate([pos_ref[:, 0:LANES]] * reps, axis=1)
    p2 = jnp.concatenate([pos_ref[:, LANES:2 * LANES]] * reps, axis=1)

    def pair(k, carry):
        slot = k & 1

        @pl.when(k + 1 < n_pairs)
        def _():
            fetch(2 * k + 2, 1 - slot, 0).start()
            fetch(2 * k + 3, 1 - slot, 1).start()

        fetch(2 * k, slot, 0).wait()
        fetch(2 * k + 1, slot, 1).wait()
        base0 = item_block(2 * k) * rb
        base1 = jnp.where(2 * k + 1 < n, item_block(2 * k + 1) * rb, -(2 ** 30))
        hits = []
        for base in (base0, base1):
            c = col + base
            hits.append(jnp.where(c == p1, 1.0, jnp.where(c == p2, 1.0, 0.0)).astype(BF16))
        y2 = ybuf[slot].reshape(2 * rb, ybuf.shape[-1])
        acc_ref[...] += _dot(jnp.concatenate(hits, axis=1), y2)
        return carry
    lax.fori_loop(0, n_pairs, pair, 0)

    @pl.when(i + 1 < pl.num_programs(0))
    def _():
        s1, n1 = start_ref[i + 1], count_ref[i + 1]
        fetch(0, 0, 0, s1, n1).start()
        fetch(1, 0, 1, s1, n1).start()
    o_ref[...] = _rms(acc_ref[...], gf_ref[...]) if final_norm else acc_ref[...]


def moe_ffn(x, h, route, w_gate, w_up, w_down, final_gain=None, tf=1408):
    t, d = x.shape
    f = w_gate.shape[2]
    tm, ck, rb = MOE_TILE, MOE_CHUNK, MOE_RBLK
    plan = _moe_plan(route, t)
    n_tiles, p = plan["n_tiles"], plan["p"]
    assert p < 2 ** 24, "row positions are carried exactly in f32"
    meta = jnp.stack([plan["pos1"].astype(F32), plan["pos2"].astype(F32), route[:, 2], route[:, 3]]
                     + [jnp.zeros((t,), F32)] * (SUBLANES - 4))
    pos_lanes = jnp.concatenate([jnp.broadcast_to(plan["pos1"][:, None], (t, LANES)),
                                 jnp.broadcast_to(plan["pos2"][:, None], (t, LANES))], axis=1)

    gm, gk = MOE_GTILE, MOE_GCHUNK
    g_tile, g_chunk, g_first, g_last, g_valid = plan["gather"]
    smem = pl.BlockSpec(memory_space=pltpu.SMEM)
    hbm = pl.BlockSpec(memory_space=pl.ANY)
    xs, w_row = pl.pallas_call(
        _moe_gather_body,
        in_specs=[smem, smem, smem, smem, smem, hbm, hbm],
        out_specs=[hbm, hbm],
        out_shape=[jax.ShapeDtypeStruct((p // gm, gm, d), BF16), jax.ShapeDtypeStruct((p // gm, gm, LANES), F32)],
        scratch_shapes=[pltpu.VMEM((MOE_IN_BUFS, gk, d), BF16), pltpu.VMEM((MOE_IN_BUFS, SUBLANES, gk), F32),
                        pltpu.VMEM((MOE_OUT_BUFS, gm, d), BF16), pltpu.VMEM((MOE_OUT_BUFS, gm, LANES), F32),
                        pltpu.VMEM((gm, d), F32), pltpu.VMEM((gm, 1), F32),
                        pltpu.SemaphoreType.DMA((2, MOE_IN_BUFS)), pltpu.SemaphoreType.DMA((2, MOE_OUT_BUFS))],
        compiler_params=pltpu.CompilerParams(vmem_limit_bytes=VMEM_LIMIT),
        name="moe_gather",
    )(g_tile, g_chunk, g_first, g_last, jnp.sum(g_valid).reshape(1),
      meta.reshape(SUBLANES, t // gk, gk).transpose(1, 0, 2), h.reshape(t // gk, gk, d))
    xs = xs.reshape(p, d)
    w_row = w_row.reshape(p, LANES)

    ys = pl.pallas_call(
        _moe_ffn_body,
        grid_spec=pltpu.PrefetchScalarGridSpec(
            num_scalar_prefetch=2,
            grid=(n_tiles, f // tf),
            in_specs=[pl.BlockSpec((tm, d), lambda i, k, ex, us: (i, 0)),
                      pl.BlockSpec((tm, LANES), lambda i, k, ex, us: (i, 0)),
                      pl.BlockSpec((None, d, tf), lambda i, k, ex, us: (ex[i], 0, k)),
                      pl.BlockSpec((None, d, tf), lambda i, k, ex, us: (ex[i], 0, k)),
                      pl.BlockSpec((None, tf, d), lambda i, k, ex, us: (ex[i], k, 0))],
            out_specs=pl.BlockSpec((tm, d), lambda i, k, ex, us: (i, 0)),
            scratch_shapes=[pltpu.VMEM((tm, d), F32)]),
        out_shape=jax.ShapeDtypeStruct((p, d), BF16),
        compiler_params=_cparams("parallel", "arbitrary"),
        name="moe_ffn",
    )(plan["tile_expert"], plan["tile_used"], xs, w_row, w_gate, w_up, w_down)

    final_norm = final_gain is not None
    gain = (final_gain if final_norm else jnp.ones((d,), F32)).reshape(1, d)
    return pl.pallas_call(
        functools.partial(_moe_combine_body, final_norm=final_norm),
        grid_spec=pltpu.PrefetchScalarGridSpec(
            num_scalar_prefetch=3,
            grid=(t // ck,),
            in_specs=[pl.BlockSpec((ck, d), lambda i, st, ct, bk: (i, 0)),
                      pl.BlockSpec((ck, 2 * LANES), lambda i, st, ct, bk: (i, 0)),
                      pl.BlockSpec((1, d), lambda i, st, ct, bk: (0, 0)),
                      pl.BlockSpec(memory_space=pl.ANY)],
            out_specs=pl.BlockSpec((ck, d), lambda i, st, ct, bk: (i, 0)),
            scratch_shapes=[pltpu.VMEM((2, 2, rb, d), BF16), pltpu.SemaphoreType.DMA((2, 2)),
                            pltpu.VMEM((ck, d), F32)]),
        out_shape=jax.ShapeDtypeStruct((t, d), F32),
        compiler_params=_cparams("arbitrary"),
        name="moe_combine",
    )(*plan["combine"], x, pos_lanes, gain, ys.reshape(p // rb, rb, d))


def _rmsnorm_body(x_ref, g_ref, o_ref):
    o_ref[...] = _rms(x_ref[...], g_ref[...])


def rmsnorm(x, g, tm=1024):
    t, d = x.shape
    tm = min(tm, t)
    return pl.pallas_call(
        _rmsnorm_body,
        grid=(t // tm,),
        in_specs=[pl.BlockSpec((tm, d), lambda i: (i, 0)), pl.BlockSpec((1, d), lambda i: (0, 0))],
        out_specs=pl.BlockSpec((tm, d), lambda i: (i, 0)),
        out_shape=jax.ShapeDtypeStruct((t, d), F32),
        compiler_params=_cparams("parallel"),
        name="rmsnorm",
    )(x, g.reshape(1, d))


def _in_proj_slices(w):
    sizes = (MOBA_WIDTH, MOBA_WIDTH, MOBA_WIDTH, SSM_WIDTH, GDN_WIDTH, GDN_WIDTH, GDN_WIDTH,
             GDN_HEADS, GDN_HEADS, GDN_WIDTH, w.shape[0], w.shape[0], w.shape[0])
    parts, start = [], 0
    for size in sizes:
        parts.append(w[:, start:start + size])
        start += size
    return parts


def kernel(x, mem, positions, norm_mix, w_in, ssm_a_re, ssm_a_im, ssm_log_dt, ssm_b_re, ssm_b_im, ssm_c_re, ssm_c_im, ssm_d, ssm_w_glu, gdn_conv, gdn_a_log, gdn_dt_bias, gdn_norm, w_up_moba, w_up_gdn, w_out, norm_xa, norm_mem, xa_wq, xa_wk, xa_wv, xa_wo, norm_ffn, ffn_w_gate, ffn_w_up, ffn_w_down, moe_w_router, moe_w_gate, moe_w_up, moe_w_down, norm_final):
    batch, seq, d = x.shape
    depth = w_in.shape[0]
    t = batch * seq
    assert batch == SUBLANES, "the S5 scan packs the batch onto the 8 sublanes of a vreg"
    assert seq % MOBA_BLOCK == 0 and seq % GDN_CHUNK == 0
    nc = seq // GDN_CHUNK
    bf = lambda a: a.astype(BF16)

    xf = x.reshape(t, d)
    memf = mem.reshape(-1, d)
    cos, sin = rope_tables(positions)

    for l in range(depth):
        (wq_m, wk_m, wv_m, wu_s, wq_g, wk_g, wv_g, wa_g, wb_g, wz_g, wg_a, wg_b, wg_c) = _in_proj_slices(w_in[l])
        g_mix = norm_mix[l]
        w_ab = jnp.pad(jnp.concatenate([wa_g, wb_g], axis=1), ((0, 0), (0, LANES - 2 * GDN_HEADS)))
        w32 = jnp.concatenate([wq_m, wk_m, wu_s, wq_g, wk_g, wv_g, w_ab], axis=1)
        w16 = jnp.concatenate([wv_m, wz_g, wg_a, wg_b, wg_c], axis=1)
        p32, p16 = norm_matmul2(xf, g_mix, bf(w32), bf(w16), F32, BF16, tm=2048, tna=640, tnb=1024)
        c_us = 2 * MOBA_WIDTH
        c_qkv = c_us + SSM_WIDTH
        c_ab = c_qkv + 3 * GDN_WIDTH
        u_s = p32[:, c_us:c_us + SSM_WIDTH]
        ab_g = p32[:, c_ab:c_ab + 2 * GDN_HEADS]

        y_a = moba_attention(p32, p16, cos, sin, batch, seq, q_col=0, k_col=MOBA_WIDTH // LANES, v_col=0)

        l_re, l_im, bb_re, bb_im = s5_discretise(ssm_a_re[l], ssm_a_im[l], ssm_log_dt[l], ssm_b_re[l], ssm_b_im[l])
        maps = s5_block_maps(bb_re, bb_im, ssm_c_re[l], ssm_c_im[l])
        u_tm = u_s.reshape(batch, seq, SSM_WIDTH).transpose(1, 0, 2).reshape(t, SSM_WIDTH)
        y_s_tm = s5_gelu(u_tm, maps, l_re, l_im, ssm_d[l])
        y_s = y_s_tm.reshape(seq, batch, SSM_WIDTH).transpose(1, 0, 2).reshape(t, SSM_WIDTH)

        def head_rows(cols):
            return cols.reshape(batch, nc, GDN_CHUNK, GDN_HEADS).transpose(0, 3, 1, 2)
        a_rows = head_rows(ab_g[:, :GDN_HEADS])
        b_rows = head_rows(ab_g[:, GDN_HEADS:2 * GDN_HEADS])
        y_c = gdn_mixer(p32, p16, a_rows, b_rows, gdn_conv[l], gdn_a_log[l], gdn_dt_bias[l], gdn_norm[l],
                        batch, seq, qkv_col=c_qkv // LANES, z_col=MOBA_WIDTH // GDN_WIDTH)

        k_x = norm_matmul(memf, norm_mem[l], bf(xa_wk[l]), BF16)
        v_x = norm_matmul(memf, norm_mem[l], bf(xa_wv[l]), BF16)
        xf = merge_cross_attention(xf, y_a, y_s, y_c, p16, bf(w_up_moba[l]), bf(ssm_w_glu[l]), bf(w_up_gdn[l]),
                                   bf(w_out[l]), norm_xa[l], bf(xa_wq[l]), k_x, v_x, bf(xa_wo[l]), batch, seq,
                                   gate_col=(MOBA_WIDTH + GDN_WIDTH) // d)

        if l % 2 == 0:
            i = l // 2
            xf = ffn(xf, norm_ffn[l], bf(ffn_w_gate[i]), bf(ffn_w_up[i]), bf(ffn_w_down[i]))
        else:
            i = l // 2
            route, h_moe = moe_router(xf, norm_ffn[l], moe_w_router[i])
            xf = moe_ffn(xf, h_moe, route, bf(moe_w_gate[i]), bf(moe_w_up[i]), bf(moe_w_down[i]),
                         final_gain=norm_final if l == depth - 1 else None)

    if depth % 2 == 1:
        xf = rmsnorm(xf, norm_final)
    return xf.reshape(batch, seq, d)
```
